```python
import math
import jax, jax.numpy as jnp
from jax import lax
import numpy as np

D_MODEL = 1024
BATCH = 8
SEQ = 2048
DEPTH = 2

FFN_DIM = 2816
RG_WIDTH = D_MODEL
RG_BLOCKS = 16
RG_BLOCK_DIM = RG_WIDTH // RG_BLOCKS
RG_C = 8.0
CONV_WIDTH = 4
ATT_GROUPS = ((128, 1), (512, 4), (2048, 16))
ATT_HEADS_PER_GROUP = 4
ATT_HEADS = ATT_HEADS_PER_GROUP * len(ATT_GROUPS)
ATT_HEAD_DIM = 64
ATT_WIDTH = ATT_HEADS * ATT_HEAD_DIM
DN_HEADS = 8
DN_HEAD_DIM = 128
DN_WIDTH = DN_HEADS * DN_HEAD_DIM
DN_CHUNK = 64
N_BRANCH = 3
EPS = 1e-6
NEG_INF = -1e30

IN_SPLITS = (RG_WIDTH, RG_WIDTH, ATT_WIDTH, ATT_WIDTH, ATT_WIDTH,
             DN_WIDTH, DN_WIDTH, DN_WIDTH, DN_WIDTH, DN_HEADS, DN_HEADS, N_BRANCH * D_MODEL)
IN_DIM = sum(IN_SPLITS)
IN_OFFSETS = tuple(int(v) for v in np.cumsum(IN_SPLITS)[:-1])
BRANCH_DIM = RG_WIDTH + ATT_WIDTH + DN_WIDTH
BRANCH_OFFSETS = (RG_WIDTH, RG_WIDTH + ATT_WIDTH)

kernel_name = 'hybrid_rglru_dilated_attn_gated_deltanet_macaron'


def rms_norm(x, g):
    xf = x.astype(jnp.float32)
    y = xf * lax.rsqrt(jnp.mean(xf * xf, axis=-1, keepdims=True) + EPS)
    return (y * g.astype(jnp.float32)).astype(x.dtype)


def l2_norm(x):
    xf = x.astype(jnp.float32)
    return xf * lax.rsqrt(jnp.sum(xf * xf, axis=-1, keepdims=True) + EPS)


def swiglu(x, w_gate, w_up, w_down):
    return (jax.nn.silu(x @ w_gate) * (x @ w_up)) @ w_down


def causal_depthwise_conv(x, w):
    K = w.shape[0]
    T = x.shape[1]
    xp = jnp.pad(x, ((0, 0), (K - 1, 0), (0, 0)))
    out = xp[:, 0:T] * w[0]
    for k in range(1, K):
        out = out + xp[:, k:k + T] * w[k]
    return out


def rg_lru(x, w_r, b_r, w_i, b_i, lam):
    B, T, C = x.shape
    xb = x.reshape(B, T, RG_BLOCKS, RG_BLOCK_DIM)
    r = jax.nn.sigmoid((jnp.einsum('btnc,ncd->btnd', xb, w_r).reshape(B, T, C) + b_r).astype(jnp.float32))
    i = jax.nn.sigmoid((jnp.einsum('btnc,ncd->btnd', xb, w_i).reshape(B, T, C) + b_i).astype(jnp.float32))
    log_a = -RG_C * r * jax.nn.softplus(-lam.astype(jnp.float32))
    a = jnp.exp(log_a)
    b = jnp.sqrt(-jnp.expm1(2.0 * log_a)) * (i * x.astype(jnp.float32))

    def combine(c1, c2):
        a1, b1 = c1
        a2, b2 = c2
        return a1 * a2, a2 * b1 + b2

    _, h = lax.associative_scan(combine, (a, b), axis=1)
    return h.astype(x.dtype)


def alibi_slopes():
    h = jnp.arange(1, ATT_HEADS + 1, dtype=jnp.float32)
    return jnp.exp2(-8.0 * h / ATT_HEADS)


def dilated_window_attention(q, k, v, slopes, window, dilation):
    B, T, H, hd = q.shape
    span = window // dilation
    L = T // dilation
    nblk = -(-L // span)
    Lp = nblk * span

    def to_blocks(t):
        t = t.reshape(B, L, dilation, H, hd).transpose(0, 2, 1, 3, 4)
        t = jnp.pad(t, ((0, 0), (0, 0), (0, Lp - L), (0, 0), (0, 0)))
        return t.reshape(B, dilation, nblk, span, H, hd)

    def with_prev(t):
        prev = jnp.pad(t, ((0, 0), (0, 0), (1, 0), (0, 0), (0, 0), (0, 0)))[:, :, :-1]
        return jnp.concatenate([prev, t], axis=3)

    qb = to_blocks(q)
    kw = with_prev(to_blocks(k))
    vw = with_prev(to_blocks(v))
    s = jnp.einsum('brnqhe,brnkhe->brnhqk', qb, kw).astype(jnp.float32)
    qi = jnp.arange(span)[:, None]
    kj = jnp.arange(2 * span)[None, :]
    delta = qi + span - kj
    blk = jnp.arange(nblk)[:, None, None]
    valid = (delta >= 0) & (delta <= span) & ((blk > 0) | (kj >= span))
    bias = -slopes[:, None, None] * (delta * dilation).astype(jnp.float32)
    s = jnp.where(valid[:, None], s + bias, NEG_INF)
    m = jnp.max(s, axis=-1, keepdims=True)
    p = jnp.exp(s - m)
    den = jnp.sum(p, axis=-1, keepdims=True)
    o = jnp.einsum('brnhqk,brnkhe->brnqhe', (p / den).astype(v.dtype), vw)
    lse = (m + jnp.log(den))[..., 0]
    o = o.reshape(B, dilation, Lp, H, hd)[:, :, :L].transpose(0, 2, 1, 3, 4).reshape(B, T, H, hd)
    lse = lse.transpose(0, 1, 2, 4, 3).reshape(B, dilation, Lp, H)[:, :, :L]
    lse = lse.transpose(0, 2, 1, 3).reshape(B, T, H)
    return o, lse


def chunk_gated_delta_rule(q, k, v, g_log, beta):
    B, T, H, dk = q.shape
    dv = v.shape[-1]
    C = DN_CHUNK
    N = T // C
    f32 = jnp.float32

    def chunks(t):
        t = t.astype(f32).reshape((B, N, C, H) + t.shape[3:])
        return jnp.moveaxis(t, 3, 1)

    qc, kc, vc = chunks(q), chunks(k), chunks(v)
    gc, bc = chunks(g_log), chunks(beta)
    gam = jnp.cumsum(gc, axis=-1)
    incl = jnp.tril(jnp.ones((C, C), bool))
    strict = jnp.tril(jnp.ones((C, C), bool), -1)
    diff = gam[..., :, None] - gam[..., None, :]
    decay = jnp.where(incl, jnp.exp(jnp.where(incl, diff, 0.0)), 0.0)
    kb = kc * bc[..., None]
    vb = vc * bc[..., None]
    a = jnp.where(strict, jnp.einsum('bhnid,bhnjd->bhnij', kb, kc) * decay, 0.0)
    rhs = jnp.concatenate([vb, kb * jnp.exp(gam)[..., None]], axis=-1)
    sol = lax.linalg.triangular_solve(a + jnp.eye(C, dtype=f32), rhs, left_side=True, lower=True,
                                      unit_diagonal=True)
    u0, wk = sol[..., :dv], sol[..., dv:]
    qk = jnp.where(incl, jnp.einsum('bhnid,bhnjd->bhnij', qc, kc) * decay, 0.0)
    q_dec = qc * jnp.exp(gam)[..., None]
    k_dec = kc * jnp.exp(gam[..., -1:] - gam)[..., None]
    c_dec = jnp.exp(gam[..., -1])
    xs = tuple(jnp.moveaxis(t, 2, 0) for t in (u0, wk, qk, q_dec, k_dec, c_dec))

    def step(S, inp):
        u0_n, w_n, qk_n, qd_n, kd_n, cd_n = inp
        u = u0_n - jnp.einsum('bhck,bhkv->bhcv', w_n, S)
        o = jnp.einsum('bhck,bhkv->bhcv', qd_n, S) + jnp.einsum('bhij,bhjv->bhiv', qk_n, u)
        S = S * cd_n[..., None, None] + jnp.einsum('bhck,bhcv->bhkv', kd_n, u)
        return S, o

    S0 = jnp.zeros((B, H, dk, dv), f32)
    _, o = lax.scan(step, S0, xs)
    return jnp.transpose(o, (1, 0, 3, 2, 4)).reshape(B, T, H, dv)


def hybrid_mixer(u, w_in, rg_conv_w, rg_conv_b, rg_w_r, rg_b_r, rg_w_i, rg_b_i, rg_lambda,
                 att_q_norm, att_k_norm, dn_conv_w, dn_a_log, dn_dt_bias, dn_out_norm, w_branch, w_out):
    B, T, _ = u.shape
    proj = u @ w_in
    (rg_x, rg_gate, aq, ak, av, dq, dk_, dv_, dz, d_beta, d_alpha, merge_logits) = jnp.split(
        proj, IN_OFFSETS, axis=-1)

    xa = causal_depthwise_conv(rg_x, rg_conv_w) + rg_conv_b
    ya = rg_lru(xa, rg_w_r, rg_b_r, rg_w_i, rg_b_i, rg_lambda) * jax.nn.gelu(rg_gate)

    hshape = (B, T, ATT_HEADS, ATT_HEAD_DIM)
    q = rms_norm(aq.reshape(hshape), att_q_norm) * (ATT_HEAD_DIM ** -0.5)
    k = rms_norm(ak.reshape(hshape), att_k_norm)
    v = av.reshape(hshape)
    slopes = alibi_slopes()
    outs, lses = [], []
    for g, (window, dilation) in enumerate(ATT_GROUPS):
        hs = slice(g * ATT_HEADS_PER_GROUP, (g + 1) * ATT_HEADS_PER_GROUP)
        o_g, lse_g = dilated_window_attention(q[:, :, hs], k[:, :, hs], v[:, :, hs], slopes[hs], window, dilation)
        outs.append(o_g)
        lses.append(lse_g)
    wts = jax.nn.softmax(jnp.stack(lses, axis=0), axis=0)
    o_att = jnp.stack(outs, axis=0) * wts[..., None].astype(v.dtype)
    yb = o_att.transpose(1, 2, 0, 3, 4).reshape(B, T, ATT_WIDTH)

    qkv = jax.nn.silu(causal_depthwise_conv(jnp.concatenate([dq, dk_, dv_], axis=-1), dn_conv_w))
    cq, ck, cv = jnp.split(qkv, 3, axis=-1)
    dshape = (B, T, DN_HEADS, DN_HEAD_DIM)
    qd = l2_norm(cq.reshape(dshape)) * (DN_HEAD_DIM ** -0.5)
    kd = l2_norm(ck.reshape(dshape))
    vd = cv.reshape(dshape)
    beta = jax.nn.sigmoid(d_beta.astype(jnp.float32))
    g_log = -jnp.exp(dn_a_log.astype(jnp.float32)) * jax.nn.softplus(
        d_alpha.astype(jnp.float32) + dn_dt_bias.astype(jnp.float32))
    o_dn = chunk_gated_delta_rule(qd, kd, vd, g_log, beta)
    yc = (rms_norm(o_dn, dn_out_norm) * jax.nn.silu(dz.reshape(dshape).astype(jnp.float32)))
    yc = yc.reshape(B, T, DN_WIDTH).astype(u.dtype)

    gates = jax.nn.sigmoid(merge_logits.reshape(B, T, N_BRANCH, D_MODEL))
    wa, wb, wc = jnp.split(w_branch, BRANCH_OFFSETS, axis=0)
    y = gates[:, :, 0] * (ya @ wa) + gates[:, :, 1] * (yb @ wb) + gates[:, :, 2] * (yc @ wc)
    return y @ w_out


def _fwd_setup_inputs(seed: int = 0) -> dict:
    key = jax.random.key(seed)
    ks = jax.random.split(key, 32)
    f32 = jnp.float32
    L, D, F = DEPTH, D_MODEL, FFN_DIM

    def dense(k, shape, fan_in):
        return jax.random.normal(k, shape, f32) * (fan_in ** -0.5)

    def gain(k, shape):
        return 1.0 + 0.05 * jax.random.normal(k, shape, f32)

    def bias(k, shape):
        return 0.02 * jax.random.normal(k, shape, f32)

    x = jax.random.normal(ks[0], (BATCH, SEQ, D), f32)
    a0 = jax.random.uniform(ks[13], (L, RG_WIDTH), f32, 0.9, 0.999)
    s = a0 ** (1.0 / RG_C)
    rg_lambda = jnp.log(s) - jnp.log1p(-s)
    dn_a_log = jnp.log(jax.random.uniform(ks[17], (L, DN_HEADS), f32, 1.0, 16.0))
    dt = jnp.exp(jax.random.uniform(ks[18], (L, DN_HEADS), f32, math.log(1e-3), math.log(1e-1)))
    dn_dt_bias = dt + jnp.log(-jnp.expm1(-dt))
    w_branch = jnp.concatenate([
        dense(ks[20], (L, RG_WIDTH, D), RG_WIDTH),
        dense(ks[21], (L, ATT_WIDTH, D), ATT_WIDTH),
        dense(ks[22], (L, DN_WIDTH, D), DN_WIDTH)], axis=1)
    return {
        'x': x,
        'ffn1_norm': gain(ks[1], (L, D)),
        'ffn1_w_gate': dense(ks[2], (L, D, F), D),
        'ffn1_w_up': dense(ks[3], (L, D, F), D),
        'ffn1_w_down': dense(ks[4], (L, F, D), F),
        'mix_norm': gain(ks[5], (L, D)),
        'w_in': dense(ks[6], (L, D, IN_DIM), D),
        'rg_conv_w': dense(ks[7], (L, CONV_WIDTH, RG_WIDTH), CONV_WIDTH),
        'rg_conv_b': bias(ks[8], (L, RG_WIDTH)),
        'rg_w_r': dense(ks[9], (L, RG_BLOCKS, RG_BLOCK_DIM, RG_BLOCK_DIM), RG_BLOCK_DIM),
        'rg_b_r': bias(ks[10], (L, RG_WIDTH)),
        'rg_w_i': dense(ks[11], (L, RG_BLOCKS, RG_BLOCK_DIM, RG_BLOCK_DIM), RG_BLOCK_DIM),
        'rg_b_i': bias(ks[12], (L, RG_WIDTH)),
        'rg_lambda': rg_lambda,
        'att_q_norm': gain(ks[14], (L, ATT_HEADS, ATT_HEAD_DIM)),
        'att_k_norm': gain(ks[15], (L, ATT_HEADS, ATT_HEAD_DIM)),
        'dn_conv_w': dense(ks[16], (L, CONV_WIDTH, 3 * DN_WIDTH), CONV_WIDTH),
        'dn_a_log': dn_a_log,
        'dn_dt_bias': dn_dt_bias,
        'dn_out_norm': gain(ks[19], (L, DN_HEADS, DN_HEAD_DIM)),
        'w_branch': w_branch,
        'w_out': dense(ks[23], (L, D, D), D),
        'ffn2_norm': gain(ks[24], (L, D)),
        'ffn2_w_gate': dense(ks[25], (L, D, F), D),
        'ffn2_w_up': dense(ks[26], (L, D, F), D),
        'ffn2_w_down': dense(ks[27], (L, F, D), F),
    }


def _fwd_reference(x, ffn1_norm, ffn1_w_gate, ffn1_w_up, ffn1_w_down, mix_norm, w_in,
              rg_conv_w, rg_conv_b, rg_w_r, rg_b_r, rg_w_i, rg_b_i, rg_lambda,
              att_q_norm, att_k_norm, dn_conv_w, dn_a_log, dn_dt_bias, dn_out_norm,
              w_branch, w_out, ffn2_norm, ffn2_w_gate, ffn2_w_up, ffn2_w_down):
    for l in range(DEPTH):
        x = x + 0.5 * swiglu(rms_norm(x, ffn1_norm[l]), ffn1_w_gate[l], ffn1_w_up[l], ffn1_w_down[l])
        x = x + hybrid_mixer(rms_norm(x, mix_norm[l]), w_in[l], rg_conv_w[l], rg_conv_b[l],
                             rg_w_r[l], rg_b_r[l], rg_w_i[l], rg_b_i[l], rg_lambda[l],
                             att_q_norm[l], att_k_norm[l], dn_conv_w[l], dn_a_log[l], dn_dt_bias[l],
                             dn_out_norm[l], w_branch[l], w_out[l])
        x = x + 0.5 * swiglu(rms_norm(x, ffn2_norm[l]), ffn2_w_gate[l], ffn2_w_up[l], ffn2_w_down[l])
    return x


import jax as _jax
import jax.numpy as _jnp

TWIN_FORMAT = 'train_step'
FWD_PARAMS = ['x', 'ffn1_norm', 'ffn1_w_gate', 'ffn1_w_up', 'ffn1_w_down', 'mix_norm', 'w_in', 'rg_conv_w', 'rg_conv_b', 'rg_w_r', 'rg_b_r', 'rg_w_i', 'rg_b_i', 'rg_lambda', 'att_q_norm', 'att_k_norm', 'dn_conv_w', 'dn_a_log', 'dn_dt_bias', 'dn_out_norm', 'w_branch', 'w_out', 'ffn2_norm', 'ffn2_w_gate', 'ffn2_w_up', 'ffn2_w_down']
TWIN_WEIGHTS = ['ffn1_norm', 'ffn1_w_gate', 'ffn1_w_up', 'ffn1_w_down', 'mix_norm', 'w_in', 'rg_conv_w', 'rg_conv_b', 'rg_w_r', 'rg_b_r', 'rg_w_i', 'rg_b_i', 'rg_lambda', 'att_q_norm', 'att_k_norm', 'dn_conv_w', 'dn_a_log', 'dn_dt_bias', 'dn_out_norm', 'w_branch', 'w_out', 'ffn2_norm', 'ffn2_w_gate', 'ffn2_w_up', 'ffn2_w_down']
TWIN_DIFF_INPUT = 'x'
TWIN_INPUTS = ['x', 'ffn1_norm', 'ffn1_w_gate', 'ffn1_w_up', 'ffn1_w_down', 'mix_norm', 'w_in', 'rg_conv_w', 'rg_conv_b', 'rg_w_r', 'rg_b_r', 'rg_w_i', 'rg_b_i', 'rg_lambda', 'att_q_norm', 'att_k_norm', 'dn_conv_w', 'dn_a_log', 'dn_dt_bias', 'dn_out_norm', 'w_branch', 'w_out', 'ffn2_norm', 'ffn2_w_gate', 'ffn2_w_up', 'ffn2_w_down', 'loss_target', 'm_ffn1_norm', 'm_ffn1_w_gate', 'm_ffn1_w_up', 'm_ffn1_w_down', 'm_mix_norm', 'm_w_in', 'm_rg_conv_w', 'm_rg_conv_b', 'm_rg_w_r', 'm_rg_b_r', 'm_rg_w_i', 'm_rg_b_i', 'm_rg_lambda', 'm_att_q_norm', 'm_att_k_norm', 'm_dn_conv_w', 'm_dn_a_log', 'm_dn_dt_bias', 'm_dn_out_norm', 'm_w_branch', 'm_w_out', 'm_ffn2_norm', 'm_ffn2_w_gate', 'm_ffn2_w_up', 'm_ffn2_w_down', 'v_ffn1_norm', 'v_ffn1_w_gate', 'v_ffn1_w_up', 'v_ffn1_w_down', 'v_mix_norm', 'v_w_in', 'v_rg_conv_w', 'v_rg_conv_b', 'v_rg_w_r', 'v_rg_b_r', 'v_rg_w_i', 'v_rg_b_i', 'v_rg_lambda', 'v_att_q_norm', 'v_att_k_norm', 'v_dn_conv_w', 'v_dn_a_log', 'v_dn_dt_bias', 'v_dn_out_norm', 'v_w_branch', 'v_w_out', 'v_ffn2_norm', 'v_ffn2_w_gate', 'v_ffn2_w_up', 'v_ffn2_w_down']
TWIN_OUTPUTS = ['loss', 'grad_x', 'grad_ffn1_norm', 'grad_ffn1_w_gate', 'grad_ffn1_w_up', 'grad_ffn1_w_down', 'grad_mix_norm', 'grad_w_in', 'grad_rg_conv_w', 'grad_rg_conv_b', 'grad_rg_w_r', 'grad_rg_b_r', 'grad_rg_w_i', 'grad_rg_b_i', 'grad_rg_lambda', 'grad_att_q_norm', 'grad_att_k_norm', 'grad_dn_conv_w', 'grad_dn_a_log', 'grad_dn_dt_bias', 'grad_dn_out_norm', 'grad_w_branch', 'grad_w_out', 'grad_ffn2_norm', 'grad_ffn2_w_gate', 'grad_ffn2_w_up', 'grad_ffn2_w_down', 'delta_ffn1_norm', 'delta_ffn1_w_gate', 'delta_ffn1_w_up', 'delta_ffn1_w_down', 'delta_mix_norm', 'delta_w_in', 'delta_rg_conv_w', 'delta_rg_conv_b', 'delta_rg_w_r', 'delta_rg_b_r', 'delta_rg_w_i', 'delta_rg_b_i', 'delta_rg_lambda', 'delta_att_q_norm', 'delta_att_k_norm', 'delta_dn_conv_w', 'delta_dn_a_log', 'delta_dn_dt_bias', 'delta_dn_out_norm', 'delta_w_branch', 'delta_w_out', 'delta_ffn2_norm', 'delta_ffn2_w_gate', 'delta_ffn2_w_up', 'delta_ffn2_w_down', 'new_m_ffn1_norm', 'new_m_ffn1_w_gate', 'new_m_ffn1_w_up', 'new_m_ffn1_w_down', 'new_m_mix_norm', 'new_m_w_in', 'new_m_rg_conv_w', 'new_m_rg_conv_b', 'new_m_rg_w_r', 'new_m_rg_b_r', 'new_m_rg_w_i', 'new_m_rg_b_i', 'new_m_rg_lambda', 'new_m_att_q_norm', 'new_m_att_k_norm', 'new_m_dn_conv_w', 'new_m_dn_a_log', 'new_m_dn_dt_bias', 'new_m_dn_out_norm', 'new_m_w_branch', 'new_m_w_out', 'new_m_ffn2_norm', 'new_m_ffn2_w_gate', 'new_m_ffn2_w_up', 'new_m_ffn2_w_down', 'new_v_ffn1_norm', 'new_v_ffn1_w_gate', 'new_v_ffn1_w_up', 'new_v_ffn1_w_down', 'new_v_mix_norm', 'new_v_w_in', 'new_v_rg_conv_w', 'new_v_rg_conv_b', 'new_v_rg_w_r', 'new_v_rg_b_r', 'new_v_rg_w_i', 'new_v_rg_b_i', 'new_v_rg_lambda', 'new_v_att_q_norm', 'new_v_att_k_norm', 'new_v_dn_conv_w', 'new_v_dn_a_log', 'new_v_dn_dt_bias', 'new_v_dn_out_norm', 'new_v_w_branch', 'new_v_w_out', 'new_v_ffn2_norm', 'new_v_ffn2_w_gate', 'new_v_ffn2_w_up', 'new_v_ffn2_w_down']
TWIN_LEAF_KINDS = {'loss': 'loss', 'grad_x': 'grad_x', 'grad_ffn1_norm': 'grad_w', 'grad_ffn1_w_gate': 'grad_w', 'grad_ffn1_w_up': 'grad_w', 'grad_ffn1_w_down': 'grad_w', 'grad_mix_norm': 'grad_w', 'grad_w_in': 'grad_w', 'grad_rg_conv_w': 'grad_w', 'grad_rg_conv_b': 'grad_w', 'grad_rg_w_r': 'grad_w', 'grad_rg_b_r': 'grad_w', 'grad_rg_w_i': 'grad_w', 'grad_rg_b_i': 'grad_w', 'grad_rg_lambda': 'grad_w', 'grad_att_q_norm': 'grad_w', 'grad_att_k_norm': 'grad_w', 'grad_dn_conv_w': 'grad_w', 'grad_dn_a_log': 'grad_w', 'grad_dn_dt_bias': 'grad_w', 'grad_dn_out_norm': 'grad_w', 'grad_w_branch': 'grad_w', 'grad_w_out': 'grad_w', 'grad_ffn2_norm': 'grad_w', 'grad_ffn2_w_gate': 'grad_w', 'grad_ffn2_w_up': 'grad_w', 'grad_ffn2_w_down': 'grad_w', 'delta_ffn1_norm': 'delta_w', 'delta_ffn1_w_gate': 'delta_w', 'delta_ffn1_w_up': 'delta_w', 'delta_ffn1_w_down': 'delta_w', 'delta_mix_norm': 'delta_w', 'delta_w_in': 'delta_w', 'delta_rg_conv_w': 'delta_w', 'delta_rg_conv_b': 'delta_w', 'delta_rg_w_r': 'delta_w', 'delta_rg_b_r': 'delta_w', 'delta_rg_w_i': 'delta_w', 'delta_rg_b_i': 'delta_w', 'delta_rg_lambda': 'delta_w', 'delta_att_q_norm': 'delta_w', 'delta_att_k_norm': 'delta_w', 'delta_dn_conv_w': 'delta_w', 'delta_dn_a_log': 'delta_w', 'delta_dn_dt_bias': 'delta_w', 'delta_dn_out_norm': 'delta_w', 'delta_w_branch': 'delta_w', 'delta_w_out': 'delta_w', 'delta_ffn2_norm': 'delta_w', 'delta_ffn2_w_gate': 'delta_w', 'delta_ffn2_w_up': 'delta_w', 'delta_ffn2_w_down': 'delta_w', 'new_m_ffn1_norm': 'new_m', 'new_m_ffn1_w_gate': 'new_m', 'new_m_ffn1_w_up': 'new_m', 'new_m_ffn1_w_down': 'new_m', 'new_m_mix_norm': 'new_m', 'new_m_w_in': 'new_m', 'new_m_rg_conv_w': 'new_m', 'new_m_rg_conv_b': 'new_m', 'new_m_rg_w_r': 'new_m', 'new_m_rg_b_r': 'new_m', 'new_m_rg_w_i': 'new_m', 'new_m_rg_b_i': 'new_m', 'new_m_rg_lambda': 'new_m', 'new_m_att_q_norm': 'new_m', 'new_m_att_k_norm': 'new_m', 'new_m_dn_conv_w': 'new_m', 'new_m_dn_a_log': 'new_m', 'new_m_dn_dt_bias': 'new_m', 'new_m_dn_out_norm': 'new_m', 'new_m_w_branch': 'new_m', 'new_m_w_out': 'new_m', 'new_m_ffn2_norm': 'new_m', 'new_m_ffn2_w_gate': 'new_m', 'new_m_ffn2_w_up': 'new_m', 'new_m_ffn2_w_down': 'new_m', 'new_v_ffn1_norm': 'new_v', 'new_v_ffn1_w_gate': 'new_v', 'new_v_ffn1_w_up': 'new_v', 'new_v_ffn1_w_down': 'new_v', 'new_v_mix_norm': 'new_v', 'new_v_w_in': 'new_v', 'new_v_rg_conv_w': 'new_v', 'new_v_rg_conv_b': 'new_v', 'new_v_rg_w_r': 'new_v', 'new_v_rg_b_r': 'new_v', 'new_v_rg_w_i': 'new_v', 'new_v_rg_b_i': 'new_v', 'new_v_rg_lambda': 'new_v', 'new_v_att_q_norm': 'new_v', 'new_v_att_k_norm': 'new_v', 'new_v_dn_conv_w': 'new_v', 'new_v_dn_a_log': 'new_v', 'new_v_dn_dt_bias': 'new_v', 'new_v_dn_out_norm': 'new_v', 'new_v_w_branch': 'new_v', 'new_v_w_out': 'new_v', 'new_v_ffn2_norm': 'new_v', 'new_v_ffn2_w_gate': 'new_v', 'new_v_ffn2_w_up': 'new_v', 'new_v_ffn2_w_down': 'new_v'}


def _forward(args):
    return _fwd_reference(*[args[k] for k in FWD_PARAMS])


def _output_shape():
    out = _jax.eval_shape(lambda: _forward(_fwd_setup_inputs(0)))
    return out.shape, out.dtype

N_MICROBATCH = 1
ADAM_LR = 0.001
ADAM_B1 = 0.9
ADAM_B2 = 0.999
ADAM_EPS = 1e-08
ADAM_WD = 0.01
ADAM_STEP = 10
PER_EXAMPLE_BATCH_AXIS = {'x': 0, 'loss_target': 0}
SHARED_INPUTS = []
_WEIGHT_DTYPES = {'ffn1_norm': _jnp.float32, 'ffn1_w_gate': _jnp.float32, 'ffn1_w_up': _jnp.float32, 'ffn1_w_down': _jnp.float32, 'mix_norm': _jnp.float32, 'w_in': _jnp.float32, 'rg_conv_w': _jnp.float32, 'rg_conv_b': _jnp.float32, 'rg_w_r': _jnp.float32, 'rg_b_r': _jnp.float32, 'rg_w_i': _jnp.float32, 'rg_b_i': _jnp.float32, 'rg_lambda': _jnp.float32, 'att_q_norm': _jnp.float32, 'att_k_norm': _jnp.float32, 'dn_conv_w': _jnp.float32, 'dn_a_log': _jnp.float32, 'dn_dt_bias': _jnp.float32, 'dn_out_norm': _jnp.float32, 'w_branch': _jnp.float32, 'w_out': _jnp.float32, 'ffn2_norm': _jnp.float32, 'ffn2_w_gate': _jnp.float32, 'ffn2_w_up': _jnp.float32, 'ffn2_w_down': _jnp.float32}
MOMENT_SCALE = {'ffn1_norm': 3.034018e+00, 'ffn1_w_gate': 9.610769e-02, 'ffn1_w_up': 1.018241e-01, 'ffn1_w_down': 1.679057e-01, 'mix_norm': 4.407161e+00, 'w_in': 2.006852e-01, 'rg_conv_w': 8.517253e-01, 'rg_conv_b': 4.438818e+00, 'rg_w_r': 2.289219e-01, 'rg_b_r': 1.429429e-01, 'rg_w_i': 4.080534e-01, 'rg_b_i': 4.237630e-01, 'rg_lambda': 2.609897e-01, 'att_q_norm': 7.606289e-02, 'att_k_norm': 7.591921e-02, 'dn_conv_w': 2.439701e-01, 'dn_a_log': 7.992966e+00, 'dn_dt_bias': 7.641961e+00, 'dn_out_norm': 2.282469e+00, 'w_branch': 3.694137e-01, 'w_out': 5.917896e-01, 'ffn2_norm': 3.109643e+00, 'ffn2_w_gate': 6.877150e-02, 'ffn2_w_up': 7.514921e-02, 'ffn2_w_down': 1.233579e-01}


def _to_microbatches(a, axis):
    t = _jnp.moveaxis(a, axis, 0)
    t = t.reshape((N_MICROBATCH, t.shape[0] // N_MICROBATCH) + t.shape[1:])
    return _jnp.moveaxis(t, 1, axis + 1)


def setup_inputs(seed: int = 0) -> dict:
    inp = _fwd_setup_inputs(seed)
    key = _jax.random.fold_in(_jax.random.key(seed), 7919)
    shape, _ = _output_shape()
    out = dict(inp)
    out["loss_target"] = _jax.random.normal(_jax.random.fold_in(key, 0), shape, _jnp.float32)
    for i, name in enumerate(TWIN_WEIGHTS):
        w = inp[name].astype(_jnp.float32)
        if MOMENT_SCALE is None:
            s = _jnp.sqrt(_jnp.mean(_jnp.square(w)) + 1e-30)
        else:
            s = MOMENT_SCALE[name]
        km, kv = _jax.random.split(_jax.random.fold_in(key, i + 1))
        out[name] = w
        out["m_" + name] = s * _jax.random.normal(km, w.shape, _jnp.float32)
        out["v_" + name] = (s * s) * _jax.random.uniform(kv, w.shape, _jnp.float32, 0.5, 1.5)
    if N_MICROBATCH > 1:
        for name, axis in PER_EXAMPLE_BATCH_AXIS.items():
            out[name] = _to_microbatches(out[name], axis)
    return {'x': out['x'], 'ffn1_norm': out['ffn1_norm'], 'ffn1_w_gate': out['ffn1_w_gate'], 'ffn1_w_up': out['ffn1_w_up'], 'ffn1_w_down': out['ffn1_w_down'], 'mix_norm': out['mix_norm'], 'w_in': out['w_in'], 'rg_conv_w': out['rg_conv_w'], 'rg_conv_b': out['rg_conv_b'], 'rg_w_r': out['rg_w_r'], 'rg_b_r': out['rg_b_r'], 'rg_w_i': out['rg_w_i'], 'rg_b_i': out['rg_b_i'], 'rg_lambda': out['rg_lambda'], 'att_q_norm': out['att_q_norm'], 'att_k_norm': out['att_k_norm'], 'dn_conv_w': out['dn_conv_w'], 'dn_a_log': out['dn_a_log'], 'dn_dt_bias': out['dn_dt_bias'], 'dn_out_norm': out['dn_out_norm'], 'w_branch': out['w_branch'], 'w_out': out['w_out'], 'ffn2_norm': out['ffn2_norm'], 'ffn2_w_gate': out['ffn2_w_gate'], 'ffn2_w_up': out['ffn2_w_up'], 'ffn2_w_down': out['ffn2_w_down'], 'loss_target': out['loss_target'], 'm_ffn1_norm': out['m_ffn1_norm'], 'm_ffn1_w_gate': out['m_ffn1_w_gate'], 'm_ffn1_w_up': out['m_ffn1_w_up'], 'm_ffn1_w_down': out['m_ffn1_w_down'], 'm_mix_norm': out['m_mix_norm'], 'm_w_in': out['m_w_in'], 'm_rg_conv_w': out['m_rg_conv_w'], 'm_rg_conv_b': out['m_rg_conv_b'], 'm_rg_w_r': out['m_rg_w_r'], 'm_rg_b_r': out['m_rg_b_r'], 'm_rg_w_i': out['m_rg_w_i'], 'm_rg_b_i': out['m_rg_b_i'], 'm_rg_lambda': out['m_rg_lambda'], 'm_att_q_norm': out['m_att_q_norm'], 'm_att_k_norm': out['m_att_k_norm'], 'm_dn_conv_w': out['m_dn_conv_w'], 'm_dn_a_log': out['m_dn_a_log'], 'm_dn_dt_bias': out['m_dn_dt_bias'], 'm_dn_out_norm': out['m_dn_out_norm'], 'm_w_branch': out['m_w_branch'], 'm_w_out': out['m_w_out'], 'm_ffn2_norm': out['m_ffn2_norm'], 'm_ffn2_w_gate': out['m_ffn2_w_gate'], 'm_ffn2_w_up': out['m_ffn2_w_up'], 'm_ffn2_w_down': out['m_ffn2_w_down'], 'v_ffn1_norm': out['v_ffn1_norm'], 'v_ffn1_w_gate': out['v_ffn1_w_gate'], 'v_ffn1_w_up': out['v_ffn1_w_up'], 'v_ffn1_w_down': out['v_ffn1_w_down'], 'v_mix_norm': out['v_mix_norm'], 'v_w_in': out['v_w_in'], 'v_rg_conv_w': out['v_rg_conv_w'], 'v_rg_conv_b': out['v_rg_conv_b'], 'v_rg_w_r': out['v_rg_w_r'], 'v_rg_b_r': out['v_rg_b_r'], 'v_rg_w_i': out['v_rg_w_i'], 'v_rg_b_i': out['v_rg_b_i'], 'v_rg_lambda': out['v_rg_lambda'], 'v_att_q_norm': out['v_att_q_norm'], 'v_att_k_norm': out['v_att_k_norm'], 'v_dn_conv_w': out['v_dn_conv_w'], 'v_dn_a_log': out['v_dn_a_log'], 'v_dn_dt_bias': out['v_dn_dt_bias'], 'v_dn_out_norm': out['v_dn_out_norm'], 'v_w_branch': out['v_w_branch'], 'v_w_out': out['v_w_out'], 'v_ffn2_norm': out['v_ffn2_norm'], 'v_ffn2_w_gate': out['v_ffn2_w_gate'], 'v_ffn2_w_up': out['v_ffn2_w_up'], 'v_ffn2_w_down': out['v_ffn2_w_down']}


def _loss(weights, diff, rest, loss_target):
    with _jax.named_scope("forward"):
        args = {**rest, TWIN_DIFF_INPUT: diff, **{k: w.astype(_WEIGHT_DTYPES[k]) for k, w in weights.items()}}
        y = _forward(args)
    with _jax.named_scope("loss_head"):
        err = _jnp.square(y.astype(_jnp.float32) - loss_target)
        return 0.5 * _jnp.sum(_jnp.mean(err, axis=-1)) if err.ndim else 0.5 * err


def _adamw(w, g, m, v):
    m = ADAM_B1 * m + (1.0 - ADAM_B1) * g
    v = ADAM_B2 * v + (1.0 - ADAM_B2) * _jnp.square(g)
    m_hat = m / (1.0 - ADAM_B1 ** ADAM_STEP)
    v_hat = v / (1.0 - ADAM_B2 ** ADAM_STEP)
    delta = -ADAM_LR * (m_hat / (_jnp.sqrt(v_hat) + ADAM_EPS) + ADAM_WD * w)
    return delta, m, v


def reference(x, ffn1_norm, ffn1_w_gate, ffn1_w_up, ffn1_w_down, mix_norm, w_in, rg_conv_w, rg_conv_b, rg_w_r, rg_b_r, rg_w_i, rg_b_i, rg_lambda, att_q_norm, att_k_norm, dn_conv_w, dn_a_log, dn_dt_bias, dn_out_norm, w_branch, w_out, ffn2_norm, ffn2_w_gate, ffn2_w_up, ffn2_w_down, loss_target, m_ffn1_norm, m_ffn1_w_gate, m_ffn1_w_up, m_ffn1_w_down, m_mix_norm, m_w_in, m_rg_conv_w, m_rg_conv_b, m_rg_w_r, m_rg_b_r, m_rg_w_i, m_rg_b_i, m_rg_lambda, m_att_q_norm, m_att_k_norm, m_dn_conv_w, m_dn_a_log, m_dn_dt_bias, m_dn_out_norm, m_w_branch, m_w_out, m_ffn2_norm, m_ffn2_w_gate, m_ffn2_w_up, m_ffn2_w_down, v_ffn1_norm, v_ffn1_w_gate, v_ffn1_w_up, v_ffn1_w_down, v_mix_norm, v_w_in, v_rg_conv_w, v_rg_conv_b, v_rg_w_r, v_rg_b_r, v_rg_w_i, v_rg_b_i, v_rg_lambda, v_att_q_norm, v_att_k_norm, v_dn_conv_w, v_dn_a_log, v_dn_dt_bias, v_dn_out_norm, v_w_branch, v_w_out, v_ffn2_norm, v_ffn2_w_gate, v_ffn2_w_up, v_ffn2_w_down):
    given = dict(x=x, ffn1_norm=ffn1_norm, ffn1_w_gate=ffn1_w_gate, ffn1_w_up=ffn1_w_up, ffn1_w_down=ffn1_w_down, mix_norm=mix_norm, w_in=w_in, rg_conv_w=rg_conv_w, rg_conv_b=rg_conv_b, rg_w_r=rg_w_r, rg_b_r=rg_b_r, rg_w_i=rg_w_i, rg_b_i=rg_b_i, rg_lambda=rg_lambda, att_q_norm=att_q_norm, att_k_norm=att_k_norm, dn_conv_w=dn_conv_w, dn_a_log=dn_a_log, dn_dt_bias=dn_dt_bias, dn_out_norm=dn_out_norm, w_branch=w_branch, w_out=w_out, ffn2_norm=ffn2_norm, ffn2_w_gate=ffn2_w_gate, ffn2_w_up=ffn2_w_up, ffn2_w_down=ffn2_w_down, loss_target=loss_target, m_ffn1_norm=m_ffn1_norm, m_ffn1_w_gate=m_ffn1_w_gate, m_ffn1_w_up=m_ffn1_w_up, m_ffn1_w_down=m_ffn1_w_down, m_mix_norm=m_mix_norm, m_w_in=m_w_in, m_rg_conv_w=m_rg_conv_w, m_rg_conv_b=m_rg_conv_b, m_rg_w_r=m_rg_w_r, m_rg_b_r=m_rg_b_r, m_rg_w_i=m_rg_w_i, m_rg_b_i=m_rg_b_i, m_rg_lambda=m_rg_lambda, m_att_q_norm=m_att_q_norm, m_att_k_norm=m_att_k_norm, m_dn_conv_w=m_dn_conv_w, m_dn_a_log=m_dn_a_log, m_dn_dt_bias=m_dn_dt_bias, m_dn_out_norm=m_dn_out_norm, m_w_branch=m_w_branch, m_w_out=m_w_out, m_ffn2_norm=m_ffn2_norm, m_ffn2_w_gate=m_ffn2_w_gate, m_ffn2_w_up=m_ffn2_w_up, m_ffn2_w_down=m_ffn2_w_down, v_ffn1_norm=v_ffn1_norm, v_ffn1_w_gate=v_ffn1_w_gate, v_ffn1_w_up=v_ffn1_w_up, v_ffn1_w_down=v_ffn1_w_down, v_mix_norm=v_mix_norm, v_w_in=v_w_in, v_rg_conv_w=v_rg_conv_w, v_rg_conv_b=v_rg_conv_b, v_rg_w_r=v_rg_w_r, v_rg_b_r=v_rg_b_r, v_rg_w_i=v_rg_w_i, v_rg_b_i=v_rg_b_i, v_rg_lambda=v_rg_lambda, v_att_q_norm=v_att_q_norm, v_att_k_norm=v_att_k_norm, v_dn_conv_w=v_dn_conv_w, v_dn_a_log=v_dn_a_log, v_dn_dt_bias=v_dn_dt_bias, v_dn_out_norm=v_dn_out_norm, v_w_branch=v_w_branch, v_w_out=v_w_out, v_ffn2_norm=v_ffn2_norm, v_ffn2_w_gate=v_ffn2_w_gate, v_ffn2_w_up=v_ffn2_w_up, v_ffn2_w_down=v_ffn2_w_down)
    weights = {n: given[n] for n in TWIN_WEIGHTS}
    shared = {n: given[n] for n in SHARED_INPUTS}
    per_example = {n: given[n] for n in ['x']}
    grad_fn = _jax.value_and_grad(_loss, argnums=(0, 1))

    def one_microbatch(ex, loss_target):
        ex = dict(ex)
        diff = ex.pop(TWIN_DIFF_INPUT)
        return grad_fn(weights, diff, {**shared, **ex}, loss_target)

    if N_MICROBATCH == 1:
        loss, (grad_w, grad_x) = one_microbatch(per_example, given["loss_target"])
    else:
        def body(carry, xs):
            loss_sum, grad_sum = carry
            l_k, (gw_k, gx_k) = one_microbatch(xs[0], xs[1])
            with _jax.named_scope("update"):
                return (loss_sum + l_k, _jax.tree.map(_jnp.add, grad_sum, gw_k)), gx_k

        init = (_jnp.zeros((), _jnp.float32), _jax.tree.map(_jnp.zeros_like, weights))
        (loss, grad_w), grad_x = _jax.lax.scan(body, init, (per_example, given["loss_target"]))
    with _jax.named_scope("update"):
        delta_w, new_m, new_v = {}, {}, {}
        for n in TWIN_WEIGHTS:
            delta_w[n], new_m[n], new_v[n] = _adamw(weights[n], grad_w[n], given["m_" + n], given["v_" + n])
    return (loss, grad_x, *[grad_w[n] for n in TWIN_WEIGHTS], *[delta_w[n] for n in TWIN_WEIGHTS],
            *[new_m[n] for n in TWIN_WEIGHTS], *[new_v[n] for n in TWIN_WEIGHTS])
```

```python
import functools
import math

import jax
import jax.numpy as jnp
import numpy as np
from jax import lax
from jax.experimental import pallas as pl
from jax.experimental.pallas import tpu as pltpu

F32 = jnp.float32
BF16 = jnp.bfloat16
HI = lax.Precision.HIGHEST
MESH = pl.DeviceIdType.MESH

D_MODEL = 1024
FFN_DIM = 2816
RG_C = 8.0
ATT_GROUPS = ((128, 1), (512, 4), (2048, 16))
ATT_HEADS = 12
ATT_HEAD_DIM = 64
ATT_SPAN = 128
DN_HEADS = 8
DN_HEAD_DIM = 128
DN_CHUNK = 64
EPS = 1e-6
NEG_INF = -1e30
N_CHIPS = 4
N_DEV = 8

ADAM_LR, ADAM_B1, ADAM_B2, ADAM_EPS, ADAM_WD, ADAM_STEP = 0.001, 0.9, 0.999, 1e-08, 0.01, 10

LANES = 128
VMEM_LIMIT = 56 * 1024 * 1024


def _params(*sem):
    return pltpu.CompilerParams(dimension_semantics=sem or None, vmem_limit_bytes=VMEM_LIMIT)


def _sigmoid(x):
    return 1.0 / (1.0 + jnp.exp(-x))


def _silu(x):
    return x * _sigmoid(x)


def _softplus(x):
    return jnp.maximum(x, 0.0) + jnp.log(1.0 + jnp.exp(-jnp.abs(x)))


def _gelu(x):
    return 0.5 * x * (1.0 + jnp.tanh(math.sqrt(2.0 / math.pi) * (x + 0.044715 * (x * x * x))))


def _neg_expm1(x):
    series = -x * (1.0 + x * (0.5 + x * (1.0 / 6 + x * (1.0 / 24 + x * (1.0 / 120 + x * (1.0 / 720))))))
    return jnp.where(x > -0.25, series, 1.0 - jnp.exp(x))


def _rms(x, g):
    return x * lax.rsqrt(jnp.mean(x * x, axis=-1, keepdims=True) + EPS) * g


def _dot(a, b, dims=(((1,), (0,)), ((), ()))):
    return lax.dot_general(a.astype(BF16), b.astype(BF16), dims, preferred_element_type=F32)


def _dot_nt(a, b):
    return _dot(a, b, (((1,), (1,)), ((), ())))


def _dot_tn(a, b):
    return _dot(a, b, (((0,), (0,)), ((), ())))


def _dot_hi(a, b):
    return lax.dot_general(a, b, (((1,), (0,)), ((), ())), precision=HI, preferred_element_type=F32)


def _rows(shape):
    return lax.broadcasted_iota(jnp.int32, shape, 0)


def _roll_down(x, s, fill):
    return jnp.where(_rows(x.shape) >= s, pltpu.roll(x, s, 0), fill)


def _roll_up(x, s, fill):
    n = x.shape[0]
    return jnp.where(_rows(x.shape) < n - s, pltpu.roll(x, n - s, 0), fill)


@functools.partial(jax.custom_vjp, nondiff_argnums=(1,))
def _shift(x, s):
    return _roll_down(x, s, 0.0)


def _shift_fwd(x, s):
    return _roll_down(x, s, 0.0), None


def _shift_bwd(s, _, g):
    return (_roll_up(g, s, 0.0),)


_shift.defvjp(_shift_fwd, _shift_bwd)


def _causal_conv(x, w):
    return w[0:1] * _shift(x, 3) + w[1:2] * _shift(x, 2) + w[2:3] * _shift(x, 1) + w[3:4] * x


@jax.custom_vjp
def _lin_scan(a, b):
    return _lin_scan_fwd(a, b)[0]


def _lin_scan_fwd(a, b):
    a0 = a
    s = 1
    while s < a.shape[0]:
        b = a * _roll_down(b, s, 0.0) + b
        a = a * _roll_down(a, s, 1.0)
        s *= 2
    return b, (a0, b)


def _lin_scan_bwd(res, g):
    a, h = res
    c = _roll_up(a, 1, 0.0)
    s = 1
    while s < a.shape[0]:
        g = c * _roll_up(g, s, 0.0) + g
        c = c * _roll_up(c, s, 1.0)
        s *= 2
    return g * _roll_down(h, 1, 0.0), g


_lin_scan.defvjp(_lin_scan_fwd, _lin_scan_bwd)


def _row_tile(t):
    return 256 if t % 256 == 0 else t


def _rowwise_fwd_call(name, f, rows, pars, tile):
    t = rows[0].shape[0]
    outs = jax.eval_shape(f, *[jax.ShapeDtypeStruct((tile, r.shape[1]), F32) for r in rows],
                          *[jax.ShapeDtypeStruct(p.shape, F32) for p in pars])
    nr, npar = len(rows), len(pars)

    def body(*refs):
        ins = [r[...] for r in refs[:nr + npar]]
        res = f(*ins)
        for o_ref, o in zip(refs[nr + npar:], res):
            o_ref[...] = o.astype(o_ref.dtype)

    return pl.pallas_call(
        body, name=name, grid=(t // tile,),
        in_specs=[pl.BlockSpec((tile, r.shape[1]), lambda i: (i, 0)) for r in rows]
        + [pl.BlockSpec(p.shape, lambda i: (0, 0)) for p in pars],
        out_specs=[pl.BlockSpec((tile, o.shape[1]), lambda i: (i, 0)) for o in outs],
        out_shape=[jax.ShapeDtypeStruct((t, o.shape[1]), F32) for o in outs],
        compiler_params=_params("parallel"),
    )(*rows, *pars)


def _rowwise_bwd_call(name, f, rows, pars, cts, tile):
    t = rows[0].shape[0]
    nr, npar, nct = len(rows), len(pars), len(cts)

    def body(*refs):
        ins = [r[...] for r in refs[:nr + npar]]
        gs = tuple(r[...] for r in refs[nr + npar:nr + npar + nct])
        outs = refs[nr + npar + nct:]
        _, vjp = jax.vjp(f, *ins)
        d = vjp(gs)
        for o_ref, v in zip(outs[:nr], d[:nr]):
            o_ref[...] = v

        @pl.when(pl.program_id(0) == 0)
        def _():
            for o_ref in outs[nr:]:
                o_ref[...] = jnp.zeros_like(o_ref)

        for o_ref, v in zip(outs[nr:], d[nr:]):
            o_ref[...] += v

    res = pl.pallas_call(
        body, name=name, grid=(t // tile,),
        in_specs=[pl.BlockSpec((tile, r.shape[1]), lambda i: (i, 0)) for r in rows]
        + [pl.BlockSpec(p.shape, lambda i: (0, 0)) for p in pars]
        + [pl.BlockSpec((tile, c.shape[1]), lambda i: (i, 0)) for c in cts],
        out_specs=[pl.BlockSpec((tile, r.shape[1]), lambda i: (i, 0)) for r in rows]
        + [pl.BlockSpec(p.shape, lambda i: (0, 0)) for p in pars],
        out_shape=[jax.ShapeDtypeStruct(r.shape, F32) for r in rows]
        + [jax.ShapeDtypeStruct(p.shape, F32) for p in pars],
        compiler_params=_params("arbitrary"),
    )(*rows, *pars, *cts)
    return tuple(res[:nr]), tuple(res[nr:])


def rowwise(name, f, rows, pars=()):
    tile = _row_tile(rows[0].shape[0])

    @jax.custom_vjp
    def op(rows, pars):
        return tuple(_rowwise_fwd_call(name, f, rows, pars, tile))

    def op_fwd(rows, pars):
        return op(rows, pars), (rows, pars)

    def op_bwd(res, cts):
        return _rowwise_bwd_call(name + "_bwd", f, res[0], res[1], tuple(cts), tile)

    op.defvjp(op_fwd, op_bwd)
    return op(tuple(rows), tuple(pars))


def _tile_of(n, cap):
    best = None
    for c in range(LANES, min(n, cap) + 1, LANES):
        if n % c == 0:
            best = c
    return best or n


def _mm_nn(name, a, w):
    m, k = a.shape
    n = w.shape[1]
    tm, tn = 512, _tile_of(n, 1024)

    def body(a_ref, w_ref, o_ref):
        o_ref[...] = _dot(a_ref[...], w_ref[...])

    return pl.pallas_call(
        body, name=name, grid=(m // tm, n // tn),
        in_specs=[pl.BlockSpec((tm, k), lambda i, j: (i, 0)), pl.BlockSpec((k, tn), lambda i, j: (0, j))],
        out_specs=pl.BlockSpec((tm, tn), lambda i, j: (i, j)),
        out_shape=jax.ShapeDtypeStruct((m, n), F32),
        compiler_params=_params("parallel", "parallel"),
    )(a, w)


def _mm_nt(name, dy, w):
    m, n = dy.shape
    k = w.shape[0]
    tm, tk = 512, _tile_of(k, 512)

    def body(dy_ref, w_ref, o_ref):
        o_ref[...] = _dot_nt(dy_ref[...], w_ref[...])

    return pl.pallas_call(
        body, name=name, grid=(m // tm, k // tk),
        in_specs=[pl.BlockSpec((tm, n), lambda i, j: (i, 0)), pl.BlockSpec((tk, n), lambda i, j: (j, 0))],
        out_specs=pl.BlockSpec((tm, tk), lambda i, j: (i, j)),
        out_shape=jax.ShapeDtypeStruct((m, k), F32),
        compiler_params=_params("parallel", "parallel"),
    )(dy, w)


def _mm_tn(name, a, dy):
    m, k = a.shape
    n = dy.shape[1]
    tk, tn = _tile_of(k, 512), _tile_of(n, 512)

    def body(a_ref, dy_ref, o_ref):
        o_ref[...] = _dot_tn(a_ref[...], dy_ref[...]).astype(BF16)

    return pl.pallas_call(
        body, name=name, grid=(k // tk, n // tn),
        in_specs=[pl.BlockSpec((m, tk), lambda i, j: (0, i)), pl.BlockSpec((m, tn), lambda i, j: (0, j))],
        out_specs=pl.BlockSpec((tk, tn), lambda i, j: (i, j)),
        out_shape=jax.ShapeDtypeStruct((k, n), BF16),
        compiler_params=_params("parallel", "parallel"),
    )(a, dy)


def mm(name, a, w):
    @jax.custom_vjp
    def op(a, w):
        return _mm_nn(name, a, w)

    def op_fwd(a, w):
        return op(a, w), (a, w)

    def op_bwd(res, dy):
        a, w = res
        return _mm_nt(name + "_da", dy, w), _mm_tn(name + "_dw", a, dy)

    op.defvjp(op_fwd, op_bwd)
    return op(a, w)


def _colwise_specs(cols, pars, par_block):
    t = cols[0].shape[0]
    specs = [pl.BlockSpec((t, LANES), lambda j: (0, j)) for _ in cols]
    for p, blk in zip(pars, par_block):
        if blk == "lane":
            specs.append(pl.BlockSpec((p.shape[0], LANES), lambda j: (0, j)))
        else:
            specs.append(pl.BlockSpec((1,) + p.shape[1:], lambda j: (j, 0, 0)))
    return specs


def _colwise_fwd_call(name, f, cols, pars, par_block, n_out):
    t, c = cols[0].shape
    nc, npar = len(cols), len(pars)

    def body(*refs):
        ins = [r[...] for r in refs[:nc]] + [r[...] if b == "lane" else r[0] for r, b in zip(refs[nc:nc + npar], par_block)]
        res = f(*ins)
        for o_ref, o in zip(refs[nc + npar:], res):
            o_ref[...] = o

    return pl.pallas_call(
        body, name=name, grid=(c // LANES,),
        in_specs=_colwise_specs(cols, pars, par_block),
        out_specs=[pl.BlockSpec((t, LANES), lambda j: (0, j)) for _ in range(n_out)],
        out_shape=[jax.ShapeDtypeStruct((t, c), F32) for _ in range(n_out)],
        compiler_params=_params("parallel"),
    )(*cols, *pars)


def _colwise_bwd_call(name, f, cols, pars, par_block, cts):
    t, c = cols[0].shape
    nc, npar, nct = len(cols), len(pars), len(cts)

    def body(*refs):
        ins = [r[...] for r in refs[:nc]] + [r[...] if b == "lane" else r[0] for r, b in zip(refs[nc:nc + npar], par_block)]
        gs = tuple(r[...] for r in refs[nc + npar:nc + npar + nct])
        outs = refs[nc + npar + nct:]
        _, vjp = jax.vjp(f, *ins)
        d = vjp(gs)
        for o_ref, v in zip(outs[:nc], d[:nc]):
            o_ref[...] = v
        for o_ref, v, b in zip(outs[nc:], d[nc:], par_block):
            if b == "lane":
                o_ref[...] = v
            else:
                o_ref[0] = v

    res = pl.pallas_call(
        body, name=name, grid=(c // LANES,),
        in_specs=_colwise_specs(cols, pars, par_block) + [pl.BlockSpec((t, LANES), lambda j: (0, j)) for _ in cts],
        out_specs=_colwise_specs(cols, pars, par_block),
        out_shape=[jax.ShapeDtypeStruct(v.shape, F32) for v in (*cols, *pars)],
        compiler_params=_params("parallel"),
    )(*cols, *pars, *cts)
    return tuple(res[:nc]), tuple(res[nc:])


def colwise(name, f, cols, pars, par_block, n_out):
    @jax.custom_vjp
    def op(cols, pars):
        return tuple(_colwise_fwd_call(name, f, cols, pars, par_block, n_out))

    def op_fwd(cols, pars):
        return op(cols, pars), (cols, pars)

    def op_bwd(res, cts):
        return _colwise_bwd_call(name + "_bwd", f, res[0], res[1], par_block, tuple(cts))

    op.defvjp(op_fwd, op_bwd)
    return op(tuple(cols), tuple(pars))


def _rg_block(x, gate, cw, cb, wr, br, wi, bi, lam):
    xa = _causal_conv(x, cw) + cb
    r = _sigmoid(_dot(xa, wr) + br)
    i = _sigmoid(_dot(xa, wi) + bi)
    log_a = -RG_C * r * _softplus(-lam)
    a = jnp.exp(log_a)
    b = jnp.sqrt(_neg_expm1(2.0 * log_a)) * (i * xa)
    return (_lin_scan(a, b) * _gelu(gate),)


def _dn_conv_block(mode):
    def f(x, cw):
        c = _silu(_causal_conv(x, cw))
        if mode == "v":
            return (c,)
        c = c * lax.rsqrt(jnp.sum(c * c, axis=-1, keepdims=True) + EPS)
        return (c * (DN_HEAD_DIM ** -0.5),) if mode == "q" else (c,)
    return f


def _block_diag(w):
    w = w.reshape(8, 2, 64, 64)
    z = jnp.zeros((8, 64, 64), w.dtype)
    top = jnp.concatenate([w[:, 0], z], axis=2)
    bot = jnp.concatenate([z, w[:, 1]], axis=2)
    return jnp.concatenate([top, bot], axis=1)


DN_HP = 2


def _dn_step(S, q, k, v, gb, h):
    c = DN_CHUNK
    lane = lax.broadcasted_iota(jnp.int32, gb.shape, 1)
    beta = jnp.sum(jnp.where(lane == h, gb, 0.0), axis=1, keepdims=True)
    g = jnp.sum(jnp.where(lane == h + DN_HEADS, gb, 0.0), axis=1, keepdims=True)
    ri = lax.broadcasted_iota(jnp.int32, (c, c), 0)
    ci = lax.broadcasted_iota(jnp.int32, (c, c), 1)
    incl, strict = ri >= ci, ri > ci
    eye = (ri == ci).astype(F32)
    ones = jnp.ones((c, c), F32)
    g_row = _dot_hi(ones, eye * g)
    gam = jnp.sum(jnp.where(incl, g_row, 0.0), axis=1, keepdims=True)
    gam_row = _dot_hi(ones, eye * gam)
    gam_last = jnp.sum(g, axis=0, keepdims=True)
    decay = jnp.where(incl, jnp.exp(jnp.where(incl, gam - gam_row, 0.0)), 0.0)
    kb = k * beta
    vb = v * beta
    a = jnp.where(strict, _dot_nt(kb, k) * decay, 0.0)
    p = -a
    tinv = eye + p
    for _ in range(5):
        p = _dot_hi(p, p)
        tinv = tinv + _dot_hi(tinv, p)
    e_gam = jnp.exp(gam)
    u0 = _dot_hi(tinv, vb)
    wk = _dot_hi(tinv, kb * e_gam)
    qk = jnp.where(incl, _dot_nt(q, k) * decay, 0.0)
    q_dec = q * e_gam
    k_dec = k * jnp.exp(gam_last - gam)
    u = u0 - _dot(wk, S)
    o = _dot(q_dec, S) + _dot(qk, u)
    s_new = S * jnp.exp(gam_last) + _dot_tn(k_dec, u)
    return s_new, o


def _dn_fwd_call(q, k, v, gb):
    t, w = q.shape
    n, hp, hd, c = t // DN_CHUNK, DN_HP, DN_HEAD_DIM, DN_CHUNK

    def body(q_ref, k_ref, v_ref, gb_ref, o_ref, s0_ref, s_scr):
        @pl.when(pl.program_id(1) == 0)
        def _():
            s_scr[...] = jnp.zeros_like(s_scr)

        gbv = gb_ref[...]
        for j in range(hp):
            sl = slice(j * hd, (j + 1) * hd)
            s_j = s_scr[j]
            s0_ref[j, 0] = s_j
            s_new, o = _dn_step(s_j, q_ref[:, sl], k_ref[:, sl], v_ref[:, sl], gbv, pl.program_id(0) * hp + j)
            o_ref[:, sl] = o
            s_scr[j] = s_new

    blk = pl.BlockSpec((c, hp * hd), lambda g, i: (i, g))
    return pl.pallas_call(
        body, name="dn_core", grid=(DN_HEADS // hp, n),
        in_specs=[blk, blk, blk, pl.BlockSpec((c, LANES), lambda g, i: (i, 0))],
        out_specs=[blk, pl.BlockSpec((hp, 1, hd, hd), lambda g, i: (g, i, 0, 0))],
        out_shape=[jax.ShapeDtypeStruct((t, w), F32), jax.ShapeDtypeStruct((DN_HEADS, n, hd, hd), F32)],
        scratch_shapes=[pltpu.VMEM((hp, hd, hd), F32)],
        compiler_params=_params("parallel", "arbitrary"),
    )(q, k, v, gb)


def _dn_bwd_call(q, k, v, gb, s0, do):
    t, w = q.shape
    n, hp, hd, c = t // DN_CHUNK, DN_HP, DN_HEAD_DIM, DN_CHUNK
    ng = DN_HEADS // hp

    def body(q_ref, k_ref, v_ref, gb_ref, s0_ref, do_ref, dq_ref, dk_ref, dv_ref, dgb_ref, ds_scr):
        @pl.when(pl.program_id(1) == 0)
        def _():
            ds_scr[...] = jnp.zeros_like(ds_scr)

        gbv = gb_ref[...]
        dgb = jnp.zeros_like(gbv)
        for j in range(hp):
            sl = slice(j * hd, (j + 1) * hd)
            h = pl.program_id(0) * hp + j
            _, vjp = jax.vjp(lambda s_, q_, k_, v_, gb_: _dn_step(s_, q_, k_, v_, gb_, h),
                             s0_ref[j, 0], q_ref[:, sl], k_ref[:, sl], v_ref[:, sl], gbv)
            ds, dq, dk, dv, dgb_j = vjp((ds_scr[j], do_ref[:, sl]))
            ds_scr[j] = ds
            dq_ref[:, sl] = dq
            dk_ref[:, sl] = dk
            dv_ref[:, sl] = dv
            dgb = dgb + dgb_j
        dgb_ref[0] = dgb

    blk = pl.BlockSpec((c, hp * hd), lambda g, i: (n - 1 - i, g))
    res = pl.pallas_call(
        body, name="dn_core_bwd", grid=(ng, n),
        in_specs=[blk, blk, blk, pl.BlockSpec((c, LANES), lambda g, i: (n - 1 - i, 0)),
                  pl.BlockSpec((hp, 1, hd, hd), lambda g, i: (g, n - 1 - i, 0, 0)), blk],
        out_specs=[blk, blk, blk, pl.BlockSpec((1, c, LANES), lambda g, i: (g, n - 1 - i, 0))],
        out_shape=[jax.ShapeDtypeStruct((t, w), F32)] * 3 + [jax.ShapeDtypeStruct((ng, t, LANES), F32)],
        scratch_shapes=[pltpu.VMEM((hp, hd, hd), F32)],
        compiler_params=_params("parallel", "arbitrary"),
    )(q, k, v, gb, s0, do)
    return res[0], res[1], res[2], jnp.sum(res[3], axis=0)


@jax.custom_vjp
def dn_core(q, k, v, gb):
    return _dn_fwd_call(q, k, v, gb)[0]


def _dn_core_fwd(q, k, v, gb):
    o, s0 = _dn_fwd_call(q, k, v, gb)
    return o, (q, k, v, gb, s0)


def _dn_core_bwd(res, do):
    return _dn_bwd_call(*res, do)


dn_core.defvjp(_dn_core_fwd, _dn_core_bwd)


def _att_block(q, kp, kc, vp, vc, qn, kn, slope, has_prev, dil):
    s = ATT_SPAN
    qh = _rms(q, qn) * (ATT_HEAD_DIM ** -0.5)
    qi = lax.broadcasted_iota(jnp.int32, (s, s), 0)
    kj = lax.broadcasted_iota(jnp.int32, (s, s), 1)
    d_p = qi + s - kj
    d_c = qi - kj
    s_p = _dot_nt(qh, _rms(kp, kn)) - slope * (d_p * dil).astype(F32)
    s_c = _dot_nt(qh, _rms(kc, kn)) - slope * (d_c * dil).astype(F32)
    s_p = jnp.where((d_p <= s) & (has_prev > 0), s_p, NEG_INF)
    s_c = jnp.where(d_c >= 0, s_c, NEG_INF)
    m = lax.stop_gradient(jnp.maximum(jnp.max(s_p, axis=1, keepdims=True), jnp.max(s_c, axis=1, keepdims=True)))
    p_p = jnp.exp(s_p - m)
    p_c = jnp.exp(s_c - m)
    den = jnp.sum(p_p, axis=1, keepdims=True) + jnp.sum(p_c, axis=1, keepdims=True)
    o = _dot(p_p / den, vp) + _dot(p_c / den, vc)
    lse = m + jnp.log(den)
    return o, jnp.broadcast_to(lse, o.shape)


def _att_specs(dil, nblk):
    cur = pl.BlockSpec((1, ATT_SPAN, ATT_HEAD_DIM), lambda i, n: (i, n, 0))
    prev = pl.BlockSpec((1, ATT_SPAN, ATT_HEAD_DIM), lambda i, n: (i, jnp.maximum(n - 1, 0), 0))
    gain = pl.BlockSpec((1, 1, ATT_HEAD_DIM), lambda i, n: (i // dil, 0, 0))
    return cur, prev, gain


def _att_slope(group, dil):
    head = (pl.program_id(0) // dil + (4 * group + 1)).astype(F32)
    return jnp.exp(jnp.full((1, 1), -8.0 / ATT_HEADS * math.log(2.0), F32) * head)


def _att_fwd_call(name, group, dil, q, k, v, qn, kn):
    r, l, e = q.shape
    nblk = l // ATT_SPAN
    cur, prev, gain = _att_specs(dil, nblk)

    def body(q_ref, kp_ref, kc_ref, vp_ref, vc_ref, qn_ref, kn_ref, o_ref, lse_ref):
        o, lse = _att_block(q_ref[0], kp_ref[0], kc_ref[0], vp_ref[0], vc_ref[0], qn_ref[0], kn_ref[0],
                            _att_slope(group, dil), pl.program_id(1), dil)
        o_ref[0] = o
        lse_ref[0] = lse

    return pl.pallas_call(
        body, name=name, grid=(r, nblk),
        in_specs=[cur, prev, cur, prev, cur, gain, gain], out_specs=[cur, cur],
        out_shape=[jax.ShapeDtypeStruct(q.shape, F32)] * 2,
        compiler_params=_params("parallel", "arbitrary"),
    )(q, k, k, v, v, qn, kn)


def _att_bwd_call(name, group, dil, q, k, v, qn, kn, do, dlse):
    r, l, e = q.shape
    nblk = l // ATT_SPAN
    cur, prev, gain = _att_specs(dil, nblk)

    def body(q_ref, kp_ref, kc_ref, vp_ref, vc_ref, qn_ref, kn_ref, do_ref, dlse_ref,
             dq_ref, dkp_ref, dkc_ref, dvp_ref, dvc_ref, dqn_ref, dkn_ref):
        slope, has_prev = _att_slope(group, dil), pl.program_id(1)
        _, vjp = jax.vjp(lambda *a: _att_block(*a, slope, has_prev, dil),
                         q_ref[0], kp_ref[0], kc_ref[0], vp_ref[0], vc_ref[0], qn_ref[0], kn_ref[0])
        dq, dkp, dkc, dvp, dvc, dqn, dkn = vjp((do_ref[0], dlse_ref[0]))
        dq_ref[0], dkp_ref[0], dkc_ref[0], dvp_ref[0], dvc_ref[0] = dq, dkp, dkc, dvp, dvc

        @pl.when((pl.program_id(0) % dil == 0) & (pl.program_id(1) == 0))
        def _():
            dqn_ref[...] = jnp.zeros_like(dqn_ref)
            dkn_ref[...] = jnp.zeros_like(dkn_ref)

        dqn_ref[0] += dqn
        dkn_ref[0] += dkn

    res = pl.pallas_call(
        body, name=name + "_bwd", grid=(r, nblk),
        in_specs=[cur, prev, cur, prev, cur, gain, gain, cur, cur],
        out_specs=[cur] * 5 + [gain, gain],
        out_shape=[jax.ShapeDtypeStruct(q.shape, F32)] * 5 + [jax.ShapeDtypeStruct(qn.shape, F32)] * 2,
        compiler_params=_params("arbitrary", "arbitrary"),
    )(q, k, k, v, v, qn, kn, do, dlse)
    dq, dkp, dkc, dvp, dvc, dqn, dkn = res
    back = lambda g: jnp.pad(g[:, ATT_SPAN:], ((0, 0), (0, ATT_SPAN), (0, 0)))
    return dq, dkc + back(dkp), dvc + back(dvp), dqn, dkn


def att_group(name, group, dil, q, k, v, qn, kn):
    @jax.custom_vjp
    def op(q, k, v, qn, kn):
        return tuple(_att_fwd_call(name, group, dil, q, k, v, qn, kn))

    def op_fwd(q, k, v, qn, kn):
        return op(q, k, v, qn, kn), (q, k, v, qn, kn)

    def op_bwd(res, cts):
        return _att_bwd_call(name, group, dil, *res, *cts)

    op.defvjp(op_fwd, op_bwd)
    return op(q, k, v, qn, kn)


def _att_mix(o1, o2, o3, l1, l2, l3):
    m = jnp.maximum(jnp.maximum(l1, l2), l3)
    e1, e2, e3 = jnp.exp(l1 - m), jnp.exp(l2 - m), jnp.exp(l3 - m)
    s = e1 + e2 + e3
    return (jnp.concatenate([o1 * (e1 / s), o2 * (e2 / s), o3 * (e3 / s)], axis=1),)


def att_branch(name, pa, qn, kn, groups=ATT_GROUPS):
    t = pa.shape[0]
    hg, e = 4, ATT_HEAD_DIM
    q, k, v = (pa[:, i * 768:(i + 1) * 768].reshape(t, ATT_HEADS, e) for i in range(3))
    outs, lses = [], []
    for g, (window, dil) in enumerate(groups):
        assert window // dil == ATT_SPAN
        l = t // dil
        to_r = lambda a: a[:, hg * g:hg * (g + 1)].reshape(l, dil, hg, e).transpose(2, 1, 0, 3).reshape(hg * dil, l, e)
        back = lambda a: a.reshape(hg, dil, l, e).transpose(2, 1, 0, 3).reshape(t, hg * e)
        o, lse = att_group(f"{name}_att{g}", g, dil, to_r(q), to_r(k), to_r(v),
                           qn[hg * g:hg * (g + 1)].reshape(hg, 1, e), kn[hg * g:hg * (g + 1)].reshape(hg, 1, e))
        outs.append(back(o))
        lses.append(back(lse))
    return rowwise(f"{name}_attmix", _att_mix, outs + lses)[0]


def dn_gates(name, ba, a_log, dt_bias):
    place = lambda p: jnp.pad(p.reshape(1, DN_HEADS), ((0, 0), (DN_HEADS, LANES - 2 * DN_HEADS)))

    def f(x, al, dt):
        lane = lax.broadcasted_iota(jnp.int32, x.shape, 1)
        return (jnp.where(lane < DN_HEADS, _sigmoid(x), -jnp.exp(al) * _softplus(x + dt)),)

    return rowwise(name, f, (ba,), (place(a_log), place(dt_bias)))[0]


def _dn_out(o, z, g):
    parts = []
    for h in range(DN_HEADS):
        sl = slice(h * DN_HEAD_DIM, (h + 1) * DN_HEAD_DIM)
        parts.append(_rms(o[:, sl], g[:, sl]) * _silu(z[:, sl]))
    return (jnp.concatenate(parts, axis=1),)


def _merge(ml, za, zb, zc):
    d = D_MODEL
    return (_sigmoid(ml[:, :d]) * za + _sigmoid(ml[:, d:2 * d]) * zb + _sigmoid(ml[:, 2 * d:]) * zc,)


def _swiglu_act(g, u):
    return (_silu(g) * u,)


def add_norm(name, x, pend, scale, gain):
    if pend is None:
        return x, rowwise(name, lambda a, g: (_rms(a, g),), (x,), (gain,))[0]

    def f(a, b, g):
        s = a + scale * b
        return s, _rms(s, g)

    return rowwise(name, f, (x, pend), (gain,))


def ffn(name, h, wg, wu, wd):
    a = rowwise(name + "_act", _swiglu_act, (mm(name + "_g", h, wg), mm(name + "_u", h, wu)))[0]
    return mm(name + "_d", a, wd)


W_IN_PIECES = (("rgx", 0, 1024), ("gate", 1024, 1024), ("att", 2048, 2304), ("dq", 4352, 1024), ("dk", 5376, 1024),
               ("dv", 6400, 1024), ("dz", 7424, 1024), ("ba", 8448, 16), ("mrg", 8464, 3072))
RG_PAR_BLOCKS = ("lane", "lane", "blk", "lane", "blk", "lane", "lane")


def mixer(name, u, w, p):
    pr = {k: mm(f"{name}_in_{k}", u, w["in_" + k]) for k, _, _ in W_IN_PIECES}
    ya = colwise(name + "_rg", _rg_block, (pr["rgx"], pr["gate"]),
                 (w["rg_conv_w"], p["rg_conv_b"], _block_diag(p["rg_w_r"]), p["rg_b_r"], _block_diag(p["rg_w_i"]),
                  p["rg_b_i"], p["rg_lambda"]), RG_PAR_BLOCKS, 1)[0]
    yb = att_branch(name, pr["att"], p["att_q_norm"], p["att_k_norm"])
    cw = w["dn_conv_w"]
    cq = colwise(name + "_dnq", _dn_conv_block("q"), (pr["dq"],), (cw[:, :1024],), ("lane",), 1)[0]
    ck = colwise(name + "_dnk", _dn_conv_block("k"), (pr["dk"],), (cw[:, 1024:2048],), ("lane",), 1)[0]
    cv = colwise(name + "_dnv", _dn_conv_block("v"), (pr["dv"],), (cw[:, 2048:],), ("lane",), 1)[0]
    gb = dn_gates(name + "_dngate", pr["ba"], p["dn_a_log"], p["dn_dt_bias"])
    o_dn = dn_core(cq, ck, cv, gb)
    yc = rowwise(name + "_dnout", _dn_out, (o_dn, pr["dz"]), (p["dn_out_norm"].reshape(1, D_MODEL),))[0]
    y = rowwise(name + "_merge", _merge, (pr["mrg"], mm(name + "_ba", ya, w["br_a"]), mm(name + "_bb", yb, w["br_b"]),
                                          mm(name + "_bc", yc, w["br_c"])))[0]
    return mm(name + "_out", y, w["w_out"])


def _loss_call(x, pend, target):
    t, d = x.shape
    tile = _row_tile(t)

    def body(x_ref, p_ref, t_ref, loss_ref, g_ref):
        err = x_ref[...] + 0.5 * p_ref[...] - t_ref[...]
        g_ref[...] = err * (1.0 / d)

        @pl.when(pl.program_id(0) == 0)
        def _():
            loss_ref[...] = jnp.zeros_like(loss_ref)

        loss_ref[...] += jnp.full(loss_ref.shape, 0.5 / d, F32) * jnp.sum(err * err)

    blk = pl.BlockSpec((tile, d), lambda i: (i, 0))
    loss, g = pl.pallas_call(
        body, name="loss", grid=(t // tile,), in_specs=[blk, blk, blk],
        out_specs=[pl.BlockSpec((8, LANES), lambda i: (0, 0)), blk],
        out_shape=[jax.ShapeDtypeStruct((8, LANES), F32), jax.ShapeDtypeStruct((t, d), F32)],
        compiler_params=_params("arbitrary"),
    )(x, pend, target)
    return loss[0, 0], g


@jax.custom_vjp
def loss_op(x, pend, target):
    return _loss_call(x, pend, target)[0]


def _loss_fwd(x, pend, target):
    loss, g = _loss_call(x, pend, target)
    return loss, g


def _loss_bwd(g, ct):
    return ct * g, (0.5 * ct) * g, None


loss_op.defvjp(_loss_fwd, _loss_bwd)


def local_loss(w, p, x, target):
    pend, scale = None, 0.0
    for l in range(len(w)):
        n = f"L{l}"
        x, h = add_norm(n + "_n1", x, pend, scale, p[l]["ffn1_norm"])
        pend, scale = ffn(n + "_f1", h, w[l]["ffn1_w_gate"], w[l]["ffn1_w_up"], w[l]["ffn1_w_down"]), 0.5
        x, h = add_norm(n + "_nm", x, pend, scale, p[l]["mix_norm"])
        pend, scale = mixer(n + "_mx", h, w[l], p[l]), 1.0
        x, h = add_norm(n + "_n2", x, pend, scale, p[l]["ffn2_norm"])
        pend, scale = ffn(n + "_f2", h, w[l]["ffn2_w_gate"], w[l]["ffn2_w_up"], w[l]["ffn2_w_down"]), 0.5
    return loss_op(x, pend, target)


WEIGHT_NAMES = ("ffn1_norm", "ffn1_w_gate", "ffn1_w_up", "ffn1_w_down", "mix_norm", "w_in", "rg_conv_w", "rg_conv_b",
                "rg_w_r", "rg_b_r", "rg_w_i", "rg_b_i", "rg_lambda", "att_q_norm", "att_k_norm", "dn_conv_w", "dn_a_log",
                "dn_dt_bias", "dn_out_norm", "w_branch", "w_out", "ffn2_norm", "ffn2_w_gate", "ffn2_w_up", "ffn2_w_down")
MATRICES = (("ffn1_w_gate", 2), ("ffn1_w_up", 2), ("ffn1_w_down", 1), ("w_in", 2), ("w_branch", 1), ("w_out", 1),
            ("ffn2_w_gate", 2), ("ffn2_w_up", 2), ("ffn2_w_down", 1))
CONVS = (("rg_conv_w", 2), ("dn_conv_w", 2))
SHARD_AXIS = dict(MATRICES + CONVS)
SMALL_NAMES = tuple(n for n in WEIGHT_NAMES if n not in SHARD_AXIS)
ROW_PARAMS = ("ffn1_norm", "mix_norm", "rg_conv_b", "rg_b_r", "rg_b_i", "rg_lambda", "ffn2_norm")
PACK_COLS = 1024
PACK_ROW_ALIGN = 32


def _shard_major(a, axis):
    s = a.shape[axis] // N_CHIPS
    return jnp.moveaxis(a.reshape(a.shape[:axis] + (N_CHIPS, s) + a.shape[axis + 1:]), axis, 0)


def _shard_minor(a, axis):
    a = jnp.moveaxis(a, 0, axis)
    return a.reshape(a.shape[:axis] + (N_CHIPS * a.shape[axis + 1],) + a.shape[axis + 2:])


def _pad_rows(flat, lead):
    n = flat.shape[-1]
    unit = PACK_COLS * PACK_ROW_ALIGN
    total = -(-n // unit) * unit
    flat = jnp.pad(flat, [(0, 0)] * lead + [(0, total - n)])
    return flat.reshape(flat.shape[:lead] + (total // PACK_COLS, PACK_COLS))


def pack_local_shards(shards):
    parts = [shards[n].astype(BF16).reshape(-1) for n, _ in MATRICES]
    parts += [lax.bitcast_convert_type(shards[n], BF16).reshape(-1) for n, _ in CONVS]
    return _pad_rows(jnp.concatenate(parts), 0)


def unpack_gathered(g, shards):
    flat = g.reshape(N_CHIPS, -1)
    out, off = {}, 0
    for n, ax in MATRICES:
        sz = shards[n].size
        out[n] = _shard_minor(flat[:, off:off + sz].reshape((N_CHIPS,) + shards[n].shape), ax)
        off += sz
    for n, ax in CONVS:
        sz = 2 * shards[n].size
        seg = flat[:, off:off + sz].reshape((N_CHIPS,) + shards[n].shape + (2,))
        out[n] = _shard_minor(lax.bitcast_convert_type(seg, F32), ax)
        off += sz
    return out


def pack_matrix_grads(full):
    parts = [_shard_major(full[n], ax).reshape(N_CHIPS, -1) for n, ax in MATRICES]
    return _pad_rows(jnp.concatenate(parts, axis=1), 1)


def unpack_matrix_grads(f, shards):
    flat = f.reshape(-1)
    out, off = {}, 0
    for n, _ in MATRICES:
        sz = shards[n].size
        out[n] = flat[off:off + sz].reshape(shards[n].shape)
        off += sz
    return out


def layer_weights(full, l):
    w = {n: full[n][l] for n in ("ffn1_w_gate", "ffn1_w_up", "ffn1_w_down", "ffn2_w_gate", "ffn2_w_up", "ffn2_w_down",
                                 "w_out", "rg_conv_w", "dn_conv_w")}
    for k, off, n in W_IN_PIECES:
        piece = full["w_in"][l][:, off:off + n]
        w["in_" + k] = jnp.pad(piece, ((0, 0), (0, LANES - n))) if n < LANES else piece
    wb = full["w_branch"][l]
    w["br_a"], w["br_b"], w["br_c"] = wb[:1024], wb[1024:1792], wb[1792:]
    return w


def layer_weight_grads(gw):
    out = {n: jnp.stack([g[n] for g in gw]) for n in ("ffn1_w_gate", "ffn1_w_up", "ffn1_w_down", "ffn2_w_gate", "ffn2_w_up",
                                                       "ffn2_w_down", "w_out", "rg_conv_w", "dn_conv_w")}
    out["w_in"] = jnp.stack([jnp.concatenate([g["in_" + k][:, :n] for k, _, n in W_IN_PIECES], axis=1) for g in gw])
    out["w_branch"] = jnp.stack([jnp.concatenate([g["br_a"], g["br_b"], g["br_c"]], axis=0) for g in gw])
    return out


def layer_small(small, l):
    p = {n: small[n][l] for n in SMALL_NAMES}
    for n in ROW_PARAMS:
        p[n] = small[n][l:l + 1]
    return p


def layer_small_grads(gp, small):
    return {n: jnp.stack([g[n] for g in gp]).reshape(small[n].shape) for n in SMALL_NAMES}


HBM_SPEC = pl.BlockSpec(memory_space=pl.ANY)


def _place():
    x, y, c = lax.axis_index("x"), lax.axis_index("y"), lax.axis_index("c")
    other_chips = [(1 - x, y), (x, 1 - y), (1 - x, 1 - y)]
    return x, y, c, 2 * x + y, (x, y, 1 - c), other_chips


def allgather_shards(p):
    r, w = p.shape
    hr = r // 2

    def body(p_ref, out_ref, send_sems, recv_sems, local_sem):
        x, y, c, me, sibling, chips = _place()

        def half(k, hc):
            return out_ref.at[k, pl.ds(pl.multiple_of(hc * hr, 16), hr), :]

        def copy(j, src, dst, to):
            return pltpu.make_async_remote_copy(src_ref=src, dst_ref=dst, send_sem=send_sems.at[j], recv_sem=recv_sems.at[j],
                                                device_id=to, device_id_type=MESH)

        mine = pltpu.make_async_copy(p_ref, out_ref.at[me], local_sem)
        mine.start()
        my_half = p_ref.at[pl.ds(pl.multiple_of(c * hr, 16), hr), :]
        first = [copy(j, my_half, half(me, c), (cx, cy, c)) for j, (cx, cy) in enumerate(chips)]
        for cp in first:
            cp.start()
        passed = []
        for j, (cx, cy) in enumerate(chips):
            k = 2 * cx + cy
            copy(j, my_half, half(k, c), (cx, cy, c)).wait_recv()
            fwd = copy(3 + j, half(k, c), half(k, c), sibling)
            fwd.start()
            passed.append(fwd)
        for j, (cx, cy) in enumerate(chips):
            k = 2 * cx + cy
            copy(3 + j, my_half, half(k, 1 - c), sibling).wait_recv()
        for cp in first + passed:
            cp.wait_send()
        mine.wait()

    return pl.pallas_call(
        body, name="allgather_shards", out_shape=jax.ShapeDtypeStruct((N_CHIPS, r, w), p.dtype),
        in_specs=[HBM_SPEC], out_specs=HBM_SPEC,
        scratch_shapes=[pltpu.SemaphoreType.DMA((6,)), pltpu.SemaphoreType.DMA((6,)), pltpu.SemaphoreType.DMA],
    )(p)


def sibling_swap_halves(g):
    k, r, w = g.shape
    hr = r // 2

    def body(g_ref, out_ref, send_sem, recv_sem):
        x, y, c, me, sibling, chips = _place()
        cp = pltpu.make_async_remote_copy(src_ref=g_ref.at[:, pl.ds(pl.multiple_of((1 - c) * hr, 16), hr), :], dst_ref=out_ref,
                                          send_sem=send_sem, recv_sem=recv_sem, device_id=sibling, device_id_type=MESH)
        cp.start()
        cp.wait()

    return pl.pallas_call(
        body, name="sibling_swap_halves", out_shape=jax.ShapeDtypeStruct((k, hr, w), g.dtype),
        in_specs=[HBM_SPEC], out_specs=HBM_SPEC,
        scratch_shapes=[pltpu.SemaphoreType.DMA, pltpu.SemaphoreType.DMA],
    )(g)


def scatter_to_chips(s):
    n, h, w = s.shape

    def body(s_ref, out_ref, send_sems, recv_sems, local_sem):
        x, y, c, me, sibling, chips = _place()
        mine = pltpu.make_async_copy(s_ref.at[me], out_ref.at[me], local_sem)
        mine.start()
        sends = []
        for j, (cx, cy) in enumerate(chips):
            cp = pltpu.make_async_remote_copy(src_ref=s_ref.at[2 * cx + cy], dst_ref=out_ref.at[me], send_sem=send_sems.at[j],
                                              recv_sem=recv_sems.at[j], device_id=(cx, cy, c), device_id_type=MESH)
            cp.start()
            sends.append(cp)
        for j, (cx, cy) in enumerate(chips):
            pltpu.make_async_remote_copy(src_ref=s_ref.at[me], dst_ref=out_ref.at[2 * cx + cy], send_sem=send_sems.at[j],
                                         recv_sem=recv_sems.at[j], device_id=(cx, cy, c), device_id_type=MESH).wait_recv()
        for cp in sends:
            cp.wait_send()
        mine.wait()

    return pl.pallas_call(
        body, name="scatter_to_chips", out_shape=jax.ShapeDtypeStruct(s.shape, s.dtype),
        in_specs=[HBM_SPEC], out_specs=HBM_SPEC,
        scratch_shapes=[pltpu.SemaphoreType.DMA((3,)), pltpu.SemaphoreType.DMA((3,)), pltpu.SemaphoreType.DMA],
    )(s)


def sibling_share_halves(f):
    h, w = f.shape

    def body(f_ref, out_ref, send_sem, recv_sem, local_sem):
        x, y, c, me, sibling, chips = _place()
        rows = lambda hc: out_ref.at[pl.ds(pl.multiple_of(hc * h, 8), h), :]
        mine = pltpu.make_async_copy(f_ref, rows(c), local_sem)
        mine.start()
        cp = pltpu.make_async_remote_copy(src_ref=f_ref, dst_ref=rows(c), send_sem=send_sem, recv_sem=recv_sem,
                                          device_id=sibling, device_id_type=MESH)
        cp.start()
        pltpu.make_async_remote_copy(src_ref=f_ref, dst_ref=rows(1 - c), send_sem=send_sem, recv_sem=recv_sem,
                                     device_id=sibling, device_id_type=MESH).wait_recv()
        cp.wait_send()
        mine.wait()

    return pl.pallas_call(
        body, name="sibling_share_halves", out_shape=jax.ShapeDtypeStruct((2 * h, w), f.dtype),
        in_specs=[HBM_SPEC], out_specs=HBM_SPEC,
        scratch_shapes=[pltpu.SemaphoreType.DMA, pltpu.SemaphoreType.DMA, pltpu.SemaphoreType.DMA],
    )(f)


def allgather_small(v):
    m_per, n = v.shape

    def body(x_ref, out_ref, send_sems, recv_sems, local_sem):
        x, y, c, _, sibling, chips = _place()
        me = (x, y, c)

        def rows(px, py, pc):
            return out_ref.at[pl.ds((4 * px + 2 * py + pc) * m_per, m_per), :]

        def copy(k, block, to, src=None):
            return pltpu.make_async_remote_copy(src_ref=rows(*block) if src is None else src, dst_ref=rows(*block),
                                                send_sem=send_sems.at[k], recv_sem=recv_sems.at[k], device_id=to, device_id_type=MESH)

        mine = pltpu.make_async_copy(x_ref, rows(*me), local_sem)
        mine.start()
        first = [copy(0, me, sibling, src=x_ref)]
        first += [copy(1 + j, me, (*chip, c), src=x_ref) for j, chip in enumerate(chips)]
        for cp in first:
            cp.start()
        passed = [copy(4 + j, (*chip, c), sibling) for j, chip in enumerate(chips)]
        for j, chip in enumerate(chips):
            copy(1 + j, (*chip, c), me).wait_recv()
            passed[j].start()
        copy(0, sibling, me).wait_recv()
        for j, chip in enumerate(chips):
            copy(4 + j, (*chip, 1 - c), me).wait_recv()
        for cp in first + passed:
            cp.wait_send()
        mine.wait()

    return pl.pallas_call(
        body, name="allgather_small", out_shape=jax.ShapeDtypeStruct((N_DEV * m_per, n), v.dtype),
        in_specs=[pl.BlockSpec(memory_space=pltpu.VMEM)], out_specs=pl.BlockSpec(memory_space=pltpu.VMEM),
        scratch_shapes=[pltpu.SemaphoreType.DMA((7,)), pltpu.SemaphoreType.DMA((7,)), pltpu.SemaphoreType.DMA],
        compiler_params=pltpu.CompilerParams(vmem_limit_bytes=VMEM_LIMIT),
    )(v)


RS_TILE = 1616


def add_sibling_half(g, a):
    n, hr, w = a.shape
    nt = hr // RS_TILE
    c = lax.axis_index("c").astype(jnp.int32).reshape(1)

    def body(c_ref, g_ref, a_ref, o_ref):
        o_ref[...] = (g_ref[...].astype(F32) + a_ref[...].astype(F32)).astype(o_ref.dtype)

    return pl.pallas_call(
        body, name="add_sibling_half", out_shape=jax.ShapeDtypeStruct(a.shape, a.dtype),
        grid_spec=pltpu.PrefetchScalarGridSpec(
            num_scalar_prefetch=1, grid=(n, nt),
            in_specs=[pl.BlockSpec((1, RS_TILE, w), lambda k, i, c_ref: (k, c_ref[0] * nt + i, 0)),
                      pl.BlockSpec((1, RS_TILE, w), lambda k, i, c_ref: (k, i, 0))],
            out_specs=pl.BlockSpec((1, RS_TILE, w), lambda k, i, c_ref: (k, i, 0))),
        compiler_params=_params("parallel", "parallel"),
    )(c, g, a)


def sum_slabs(b, tile):
    k, h, w = b.shape

    def body(b_ref, o_ref):
        acc = b_ref[0].astype(F32)
        for i in range(1, k):
            acc = acc + b_ref[i].astype(F32)
        o_ref[...] = acc

    return pl.pallas_call(
        body, name=f"sum_slabs{k}", grid=(h // tile,),
        in_specs=[pl.BlockSpec((k, tile, w), lambda i: (0, i, 0))], out_specs=pl.BlockSpec((tile, w), lambda i: (i, 0)),
        out_shape=jax.ShapeDtypeStruct((h, w), F32), compiler_params=_params("parallel"),
    )(b)


def _adam_block(w, g, m, v):
    m = ADAM_B1 * m + (1.0 - ADAM_B1) * g
    v = ADAM_B2 * v + (1.0 - ADAM_B2) * (g * g)
    m_hat = m / (1.0 - ADAM_B1 ** ADAM_STEP)
    v_hat = v / (1.0 - ADAM_B2 ** ADAM_STEP)
    return -ADAM_LR * (m_hat / (jnp.sqrt(v_hat) + ADAM_EPS) + ADAM_WD * w), m, v


def adamw(name, w, g, m, v):
    shape = w.shape
    cols = shape[-1]
    rows = w.size // cols
    tile = 128 if rows % 128 == 0 else rows
    flat = [a.reshape(rows, cols) for a in (w, g, m, v)]

    def body(w_ref, g_ref, m_ref, v_ref, d_ref, nm_ref, nv_ref):
        d_ref[...], nm_ref[...], nv_ref[...] = _adam_block(w_ref[...], g_ref[...], m_ref[...], v_ref[...])

    blk = pl.BlockSpec((tile, cols), lambda i: (i, 0))
    res = pl.pallas_call(
        body, name=name, grid=(rows // tile,), in_specs=[blk] * 4, out_specs=[blk] * 3,
        out_shape=[jax.ShapeDtypeStruct((rows, cols), F32)] * 3, compiler_params=_params("parallel"),
    )(*flat)
    return tuple(r.reshape(shape) for r in res)


def _pack_small(grads):
    flat = jnp.concatenate([grads[n].reshape(-1) for n in SMALL_NAMES + tuple(n for n, _ in CONVS)])
    n = flat.shape[0]
    total = -(-n // (8 * LANES)) * (8 * LANES)
    return jnp.pad(flat, (0, total - n)).reshape(-1, LANES)


def _unpack_small(v, shapes):
    flat = v.reshape(-1)
    out, off = {}, 0
    for n in SMALL_NAMES + tuple(n for n, _ in CONVS):
        sz = int(np.prod(shapes[n]))
        out[n] = flat[off:off + sz].reshape(shapes[n])
        off += sz
    return out


def kernel(x, ffn1_norm, ffn1_w_gate, ffn1_w_up, ffn1_w_down, mix_norm, w_in, rg_conv_w, rg_conv_b, rg_w_r, rg_b_r, rg_w_i, rg_b_i, rg_lambda, att_q_norm, att_k_norm, dn_conv_w, dn_a_log, dn_dt_bias, dn_out_norm, w_branch, w_out, ffn2_norm, ffn2_w_gate, ffn2_w_up, ffn2_w_down, loss_target, m_ffn1_norm, m_ffn1_w_gate, m_ffn1_w_up, m_ffn1_w_down, m_mix_norm, m_w_in, m_rg_conv_w, m_rg_conv_b, m_rg_w_r, m_rg_b_r, m_rg_w_i, m_rg_b_i, m_rg_lambda, m_att_q_norm, m_att_k_norm, m_dn_conv_w, m_dn_a_log, m_dn_dt_bias, m_dn_out_norm, m_w_branch, m_w_out, m_ffn2_norm, m_ffn2_w_gate, m_ffn2_w_up, m_ffn2_w_down, v_ffn1_norm, v_ffn1_w_gate, v_ffn1_w_up, v_ffn1_w_down, v_mix_norm, v_w_in, v_rg_conv_w, v_rg_conv_b, v_rg_w_r, v_rg_b_r, v_rg_w_i, v_rg_b_i, v_rg_lambda, v_att_q_norm, v_att_k_norm, v_dn_conv_w, v_dn_a_log, v_dn_dt_bias, v_dn_out_norm, v_w_branch, v_w_out, v_ffn2_norm, v_ffn2_w_gate, v_ffn2_w_up, v_ffn2_w_down):
    given = dict(locals())
    shards = {n: given[n] for n in SHARD_AXIS}
    small = {n: given[n] for n in SMALL_NAMES}
    n_layers = ffn1_norm.shape[0]

    full = unpack_gathered(allgather_shards(pack_local_shards(shards)), shards)
    w = [layer_weights(full, l) for l in range(n_layers)]
    p = [layer_small(small, l) for l in range(n_layers)]

    loss, (gw, gp, gx) = jax.value_and_grad(local_loss, argnums=(0, 1, 2))(w, p, x[0], loss_target[0])
    loss = lax.psum(loss, ("x", "y", "c"))
    g_full = layer_weight_grads(gw)

    packed = pack_matrix_grads(g_full)
    mine = add_sibling_half(packed, sibling_swap_halves(packed))
    half_sum = sum_slabs(scatter_to_chips(mine), RS_TILE)
    grads = unpack_matrix_grads(sibling_share_halves(half_sum), shards)

    g_small = dict(layer_small_grads(gp, small), **{n: g_full[n] for n, _ in CONVS})
    packed_small = _pack_small(g_small)
    gathered = allgather_small(packed_small).reshape(N_DEV, packed_small.shape[0], LANES)
    summed = _unpack_small(sum_slabs(gathered, packed_small.shape[0]), {n: g.shape for n, g in g_small.items()})
    chip = 2 * lax.axis_index("x") + lax.axis_index("y")
    for n in SMALL_NAMES:
        grads[n] = summed[n]
    for n, ax in CONVS:
        s = shards[n].shape[ax]
        grads[n] = lax.dynamic_slice_in_dim(summed[n], chip * s, s, axis=ax)

    upd = {n: adamw("adamw_" + n, given[n], grads[n], given["m_" + n], given["v_" + n]) for n in WEIGHT_NAMES}
    return (loss, gx[None], *[grads[n] for n in WEIGHT_NAMES], *[upd[n][0] for n in WEIGHT_NAMES],
            *[upd[n][1] for n in WEIGHT_NAMES], *[upd[n][2] for n in WEIGHT_NAMES])
```

```python
import functools
import math

import jax
import jax.numpy as jnp
import numpy as np
from jax import lax
from jax.experimental import pallas as pl
from jax.experimental.pallas import tpu as pltpu

F32 = jnp.float32
BF16 = jnp.bfloat16
HI = lax.Precision.HIGHEST
MESH = pl.DeviceIdType.MESH

D_MODEL = 1024
FFN_DIM = 2816
RG_C = 8.0
ATT_GROUPS = ((128, 1), (512, 4), (2048, 16))
ATT_HEADS = 12
ATT_HEAD_DIM = 64
ATT_SPAN = 128
DN_HEADS = 8
DN_HEAD_DIM = 128
DN_CHUNK = 64
EPS = 1e-6
NEG_INF = -1e30
N_CHIPS = 4
N_DEV = 8

ADAM_LR, ADAM_B1, ADAM_B2, ADAM_EPS, ADAM_WD, ADAM_STEP = 0.001, 0.9, 0.999, 1e-08, 0.01, 10

LANES = 128
VMEM_LIMIT = 56 * 1024 * 1024


def _params(*sem):
    return pltpu.CompilerParams(dimension_semantics=sem or None, vmem_limit_bytes=VMEM_LIMIT)


def _sigmoid(x):
    return 1.0 / (1.0 + jnp.exp(-x))


def _silu(x):
    return x * _sigmoid(x)


def _softplus(x):
    return jnp.maximum(x, 0.0) + jnp.log(1.0 + jnp.exp(-jnp.abs(x)))


def _gelu(x):
    return 0.5 * x * (1.0 + jnp.tanh(math.sqrt(2.0 / math.pi) * (x + 0.044715 * (x * x * x))))


def _neg_expm1(x):
    series = -x * (1.0 + x * (0.5 + x * (1.0 / 6 + x * (1.0 / 24 + x * (1.0 / 120 + x * (1.0 / 720))))))
    return jnp.where(x > -0.25, series, 1.0 - jnp.exp(x))


def _rms(x, g):
    return x * lax.rsqrt(jnp.mean(x * x, axis=-1, keepdims=True) + EPS) * g


def _dot(a, b, dims=(((1,), (0,)), ((), ()))):
    return lax.dot_general(a.astype(BF16), b.astype(BF16), dims, preferred_element_type=F32)


def _dot_nt(a, b):
    return _dot(a, b, (((1,), (1,)), ((), ())))


def _dot_tn(a, b):
    return _dot(a, b, (((0,), (0,)), ((), ())))


def _dot_hi(a, b):
    return lax.dot_general(a, b, (((1,), (0,)), ((), ())), precision=HI, preferred_element_type=F32)


def _rows(shape):
    return lax.broadcasted_iota(jnp.int32, shape, 0)


def _roll_down(x, s, fill):
    return jnp.where(_rows(x.shape) >= s, pltpu.roll(x, s, 0), fill)


def _roll_up(x, s, fill):
    n = x.shape[0]
    return jnp.where(_rows(x.shape) < n - s, pltpu.roll(x, n - s, 0), fill)


@functools.partial(jax.custom_vjp, nondiff_argnums=(1,))
def _shift(x, s):
    return _roll_down(x, s, 0.0)


def _shift_fwd(x, s):
    return _roll_down(x, s, 0.0), None


def _shift_bwd(s, _, g):
    return (_roll_up(g, s, 0.0),)


_shift.defvjp(_shift_fwd, _shift_bwd)


def _causal_conv(x, w):
    return w[0:1] * _shift(x, 3) + w[1:2] * _shift(x, 2) + w[2:3] * _shift(x, 1) + w[3:4] * x


@jax.custom_vjp
def _lin_scan(a, b):
    return _lin_scan_fwd(a, b)[0]


def _lin_scan_fwd(a, b):
    a0 = a
    s = 1
    while s < a.shape[0]:
        b = a * _roll_down(b, s, 0.0) + b
        a = a * _roll_down(a, s, 1.0)
        s *= 2
    return b, (a0, b)


def _lin_scan_bwd(res, g):
    a, h = res
    c = _roll_up(a, 1, 0.0)
    s = 1
    while s < a.shape[0]:
        g = c * _roll_up(g, s, 0.0) + g
        c = c * _roll_up(c, s, 1.0)
        s *= 2
    return g * _roll_down(h, 1, 0.0), g


_lin_scan.defvjp(_lin_scan_fwd, _lin_scan_bwd)


def _row_tile(t):
    return 256 if t % 256 == 0 else t


def _rowwise_fwd_call(name, f, rows, pars, tile):
    t = rows[0].shape[0]
    outs = jax.eval_shape(f, *[jax.ShapeDtypeStruct((tile, r.shape[1]), F32) for r in rows],
                          *[jax.ShapeDtypeStruct(p.shape, F32) for p in pars])
    nr, npar = len(rows), len(pars)

    def body(*refs):
        ins = [r[...] for r in refs[:nr + npar]]
        res = f(*ins)
        for o_ref, o in zip(refs[nr + npar:], res):
            o_ref[...] = o.astype(o_ref.dtype)

    return pl.pallas_call(
        body, name=name, grid=(t // tile,),
        in_specs=[pl.BlockSpec((tile, r.shape[1]), lambda i: (i, 0)) for r in rows]
        + [pl.BlockSpec(p.shape, lambda i: (0, 0)) for p in pars],
        out_specs=[pl.BlockSpec((tile, o.shape[1]), lambda i: (i, 0)) for o in outs],
        out_shape=[jax.ShapeDtypeStruct((t, o.shape[1]), F32) for o in outs],
        compiler_params=_params("parallel"),
    )(*rows, *pars)


def _rowwise_bwd_call(name, f, rows, pars, cts, tile):
    t = rows[0].shape[0]
    nr, npar, nct = len(rows), len(pars), len(cts)

    def body(*refs):
        ins = [r[...] for r in refs[:nr + npar]]
        gs = tuple(r[...] for r in refs[nr + npar:nr + npar + nct])
        outs = refs[nr + npar + nct:]
        _, vjp = jax.vjp(f, *ins)
        d = vjp(gs)
        for o_ref, v in zip(outs[:nr], d[:nr]):
            o_ref[...] = v

        @pl.when(pl.program_id(0) == 0)
        def _():
            for o_ref in outs[nr:]:
                o_ref[...] = jnp.zeros_like(o_ref)

        for o_ref, v in zip(outs[nr:], d[nr:]):
            o_ref[...] += v

    res = pl.pallas_call(
        body, name=name, grid=(t // tile,),
        in_specs=[pl.BlockSpec((tile, r.shape[1]), lambda i: (i, 0)) for r in rows]
        + [pl.BlockSpec(p.shape, lambda i: (0, 0)) for p in pars]
        + [pl.BlockSpec((tile, c.shape[1]), lambda i: (i, 0)) for c in cts],
        out_specs=[pl.BlockSpec((tile, r.shape[1]), lambda i: (i, 0)) for r in rows]
        + [pl.BlockSpec(p.shape, lambda i: (0, 0)) for p in pars],
        out_shape=[jax.ShapeDtypeStruct(r.shape, F32) for r in rows]
        + [jax.ShapeDtypeStruct(p.shape, F32) for p in pars],
        compiler_params=_params("arbitrary"),
    )(*rows, *pars, *cts)
    return tuple(res[:nr]), tuple(res[nr:])


def rowwise(name, f, rows, pars=()):
    tile = _row_tile(rows[0].shape[0])

    @jax.custom_vjp
    def op(rows, pars):
        return tuple(_rowwise_fwd_call(name, f, rows, pars, tile))

    def op_fwd(rows, pars):
        return op(rows, pars), (rows, pars)

    def op_bwd(res, cts):
        return _rowwise_bwd_call(name + "_bwd", f, res[0], res[1], tuple(cts), tile)

    op.defvjp(op_fwd, op_bwd)
    return op(tuple(rows), tuple(pars))


MM_TM = 512


def _tile_of(n, cap):
    best = None
    for c in range(LANES, min(n, cap) + 1, LANES):
        if n % c == 0:
            best = c
    return best or n


def _mmc_fwd(name, h, w):
    m, k = h.shape
    j, _, n = w.shape
    tm, tn = MM_TM, _tile_of(n, 1408)

    def body(h_ref, w_ref, o_ref):
        o_ref[0] = _dot(h_ref[...], w_ref[0])

    return pl.pallas_call(
        body, name=name, grid=(m // tm, j, n // tn),
        in_specs=[pl.BlockSpec((tm, k), lambda i, b, c: (i, 0)), pl.BlockSpec((1, k, tn), lambda i, b, c: (b, 0, c))],
        out_specs=pl.BlockSpec((1, tm, tn), lambda i, b, c: (b, i, c)),
        out_shape=jax.ShapeDtypeStruct((j, m, n), F32),
        compiler_params=_params("parallel", "parallel", "parallel"),
    )(h, w)


def _mmc_dh(name, dy, w):
    j, m, n = dy.shape
    k = w.shape[1]
    tm, tn = MM_TM, _tile_of(n, 1408)

    def body(dy_ref, w_ref, o_ref):
        part = _dot_nt(dy_ref[0], w_ref[0])

        @pl.when((pl.program_id(1) == 0) & (pl.program_id(2) == 0))
        def _():
            o_ref[...] = part

        @pl.when((pl.program_id(1) > 0) | (pl.program_id(2) > 0))
        def _():
            o_ref[...] += part

    return pl.pallas_call(
        body, name=name, grid=(m // tm, j, n // tn),
        in_specs=[pl.BlockSpec((1, tm, tn), lambda i, b, c: (b, i, c)), pl.BlockSpec((1, k, tn), lambda i, b, c: (b, 0, c))],
        out_specs=pl.BlockSpec((tm, k), lambda i, b, c: (i, 0)),
        out_shape=jax.ShapeDtypeStruct((m, k), F32),
        compiler_params=_params("parallel", "arbitrary", "arbitrary"),
    )(dy, w)


def _mmc_dw(name, h, dy):
    m, k = h.shape
    j, _, n = dy.shape
    tk, tn = _tile_of(k, 512), _tile_of(n, 1152)

    def body(h_ref, dy_ref, o_ref):
        o_ref[0] = _dot_tn(h_ref[...], dy_ref[0]).astype(BF16)

    return pl.pallas_call(
        body, name=name, grid=(j, k // tk, n // tn),
        in_specs=[pl.BlockSpec((m, tk), lambda b, i, c: (0, i)), pl.BlockSpec((1, m, tn), lambda b, i, c: (b, 0, c))],
        out_specs=pl.BlockSpec((1, tk, tn), lambda b, i, c: (b, i, c)),
        out_shape=jax.ShapeDtypeStruct((j, k, n), BF16),
        compiler_params=_params("parallel", "parallel", "parallel"),
    )(h, dy)


def mm_cols(name, h, w):
    @jax.custom_vjp
    def op(h, w):
        return _mmc_fwd(name, h, w)

    def op_fwd(h, w):
        return op(h, w), (h, w)

    def op_bwd(res, dy):
        h, w = res
        return _mmc_dh(name + "_dh", dy, w), _mmc_dw(name + "_dw", h, dy)

    op.defvjp(op_fwd, op_bwd)
    return op(h, w)


def _mmr_fwd(name, a, w):
    j, m, n = a.shape
    nn = w.shape[2]
    tm, tn = MM_TM, _tile_of(nn, 1024)

    def body(a_ref, w_ref, o_ref):
        part = _dot(a_ref[0], w_ref[0])

        @pl.when(pl.program_id(2) == 0)
        def _():
            o_ref[...] = part

        @pl.when(pl.program_id(2) > 0)
        def _():
            o_ref[...] += part

    return pl.pallas_call(
        body, name=name, grid=(m // tm, nn // tn, j),
        in_specs=[pl.BlockSpec((1, tm, n), lambda i, c, b: (b, i, 0)), pl.BlockSpec((1, n, tn), lambda i, c, b: (b, 0, c))],
        out_specs=pl.BlockSpec((tm, tn), lambda i, c, b: (i, c)),
        out_shape=jax.ShapeDtypeStruct((m, nn), F32),
        compiler_params=_params("parallel", "parallel", "arbitrary"),
    )(a, w)


def _mmr_da(name, dy, w):
    m, nn = dy.shape
    j, n, _ = w.shape
    tm = MM_TM

    def body(dy_ref, w_ref, o_ref):
        o_ref[0] = _dot_nt(dy_ref[...], w_ref[0])

    return pl.pallas_call(
        body, name=name, grid=(m // tm, j),
        in_specs=[pl.BlockSpec((tm, nn), lambda i, b: (i, 0)), pl.BlockSpec((1, n, nn), lambda i, b: (b, 0, 0))],
        out_specs=pl.BlockSpec((1, tm, n), lambda i, b: (b, i, 0)),
        out_shape=jax.ShapeDtypeStruct((j, m, n), F32),
        compiler_params=_params("parallel", "parallel"),
    )(dy, w)


def _mmr_dw(name, a, dy):
    j, m, n = a.shape
    nn = dy.shape[1]
    tn = _tile_of(nn, 512)

    def body(a_ref, dy_ref, o_ref):
        o_ref[0] = _dot_tn(a_ref[0], dy_ref[...]).astype(BF16)

    return pl.pallas_call(
        body, name=name, grid=(j, nn // tn),
        in_specs=[pl.BlockSpec((1, m, n), lambda b, c: (b, 0, 0)), pl.BlockSpec((m, tn), lambda b, c: (0, c))],
        out_specs=pl.BlockSpec((1, n, tn), lambda b, c: (b, 0, c)),
        out_shape=jax.ShapeDtypeStruct((j, n, nn), BF16),
        compiler_params=_params("parallel", "parallel"),
    )(a, dy)


def mm_rows(name, a, w):
    @jax.custom_vjp
    def op(a, w):
        return _mmr_fwd(name, a, w)

    def op_fwd(a, w):
        return op(a, w), (a, w)

    def op_bwd(res, dy):
        a, w = res
        return _mmr_da(name + "_da", dy, w), _mmr_dw(name + "_dw", a, dy)

    op.defvjp(op_fwd, op_bwd)
    return op(a, w)


def _colwise_specs(cols, pars, par_block):
    t = cols[0].shape[0]
    specs = [pl.BlockSpec((t, LANES), lambda j: (0, j)) for _ in cols]
    for p, blk in zip(pars, par_block):
        if blk == "lane":
            specs.append(pl.BlockSpec((p.shape[0], LANES), lambda j: (0, j)))
        else:
            specs.append(pl.BlockSpec((1,) + p.shape[1:], lambda j: (j, 0, 0)))
    return specs


def _colwise_fwd_call(name, f, cols, pars, par_block, n_out):
    t, c = cols[0].shape
    nc, npar = len(cols), len(pars)

    def body(*refs):
        ins = [r[...] for r in refs[:nc]] + [r[...] if b == "lane" else r[0] for r, b in zip(refs[nc:nc + npar], par_block)]
        res = f(*ins)
        for o_ref, o in zip(refs[nc + npar:], res):
            o_ref[...] = o

    return pl.pallas_call(
        body, name=name, grid=(c // LANES,),
        in_specs=_colwise_specs(cols, pars, par_block),
        out_specs=[pl.BlockSpec((t, LANES), lambda j: (0, j)) for _ in range(n_out)],
        out_shape=[jax.ShapeDtypeStruct((t, c), F32) for _ in range(n_out)],
        compiler_params=_params("parallel"),
    )(*cols, *pars)


def _colwise_bwd_call(name, f, cols, pars, par_block, cts):
    t, c = cols[0].shape
    nc, npar, nct = len(cols), len(pars), len(cts)

    def body(*refs):
        ins = [r[...] for r in refs[:nc]] + [r[...] if b == "lane" else r[0] for r, b in zip(refs[nc:nc + npar], par_block)]
        gs = tuple(r[...] for r in refs[nc + npar:nc + npar + nct])
        outs = refs[nc + npar + nct:]
        _, vjp = jax.vjp(f, *ins)
        d = vjp(gs)
        for o_ref, v in zip(outs[:nc], d[:nc]):
            o_ref[...] = v
        for o_ref, v, b in zip(outs[nc:], d[nc:], par_block):
            if b == "lane":
                o_ref[...] = v
            else:
                o_ref[0] = v

    res = pl.pallas_call(
        body, name=name, grid=(c // LANES,),
        in_specs=_colwise_specs(cols, pars, par_block) + [pl.BlockSpec((t, LANES), lambda j: (0, j)) for _ in cts],
        out_specs=_colwise_specs(cols, pars, par_block),
        out_shape=[jax.ShapeDtypeStruct(v.shape, F32) for v in (*cols, *pars)],
        compiler_params=_params("parallel"),
    )(*cols, *pars, *cts)
    return tuple(res[:nc]), tuple(res[nc:])


def colwise(name, f, cols, pars, par_block, n_out):
    @jax.custom_vjp
    def op(cols, pars):
        return tuple(_colwise_fwd_call(name, f, cols, pars, par_block, n_out))

    def op_fwd(cols, pars):
        return op(cols, pars), (cols, pars)

    def op_bwd(res, cts):
        return _colwise_bwd_call(name + "_bwd", f, res[0], res[1], par_block, tuple(cts))

    op.defvjp(op_fwd, op_bwd)
    return op(tuple(cols), tuple(pars))


def _rg_block(x, gate, cw, cb, wr, br, wi, bi, lam):
    xa = _causal_conv(x, cw) + cb
    r = _sigmoid(_dot(xa, wr) + br)
    i = _sigmoid(_dot(xa, wi) + bi)
    log_a = -RG_C * r * _softplus(-lam)
    a = jnp.exp(log_a)
    b = jnp.sqrt(_neg_expm1(2.0 * log_a)) * (i * xa)
    return (_lin_scan(a, b) * _gelu(gate),)


def _dn_conv_block(mode):
    def f(x, cw):
        c = _silu(_causal_conv(x, cw))
        if mode == "v":
            return (c,)
        c = c * lax.rsqrt(jnp.sum(c * c, axis=-1, keepdims=True) + EPS)
        return (c * (DN_HEAD_DIM ** -0.5),) if mode == "q" else (c,)
    return f


def _block_diag(w):
    w = w.reshape(8, 2, 64, 64)
    z = jnp.zeros((8, 64, 64), w.dtype)
    top = jnp.concatenate([w[:, 0], z], axis=2)
    bot = jnp.concatenate([z, w[:, 1]], axis=2)
    return jnp.concatenate([top, bot], axis=1)


DN_HP = 2


def _dn_step(S, q, k, v, gb, h):
    c = DN_CHUNK
    lane = lax.broadcasted_iota(jnp.int32, gb.shape, 1)
    beta = jnp.sum(jnp.where(lane == h, gb, 0.0), axis=1, keepdims=True)
    g = jnp.sum(jnp.where(lane == h + DN_HEADS, gb, 0.0), axis=1, keepdims=True)
    ri = lax.broadcasted_iota(jnp.int32, (c, c), 0)
    ci = lax.broadcasted_iota(jnp.int32, (c, c), 1)
    incl, strict = ri >= ci, ri > ci
    eye = (ri == ci).astype(F32)
    ones = jnp.ones((c, c), F32)
    g_row = _dot_hi(ones, eye * g)
    gam = jnp.sum(jnp.where(incl, g_row, 0.0), axis=1, keepdims=True)
    gam_row = _dot_hi(ones, eye * gam)
    gam_last = jnp.sum(g, axis=0, keepdims=True)
    decay = jnp.where(incl, jnp.exp(jnp.where(incl, gam - gam_row, 0.0)), 0.0)
    kb = k * beta
    vb = v * beta
    a = jnp.where(strict, _dot_nt(kb, k) * decay, 0.0)
    p = -a
    tinv = eye + p
    for _ in range(5):
        p = _dot_hi(p, p)
        tinv = tinv + _dot_hi(tinv, p)
    e_gam = jnp.exp(gam)
    u0 = _dot_hi(tinv, vb)
    wk = _dot_hi(tinv, kb * e_gam)
    qk = jnp.where(incl, _dot_nt(q, k) * decay, 0.0)
    q_dec = q * e_gam
    k_dec = k * jnp.exp(gam_last - gam)
    u = u0 - _dot(wk, S)
    o = _dot(q_dec, S) + _dot(qk, u)
    s_new = S * jnp.exp(gam_last) + _dot_tn(k_dec, u)
    return s_new, o


def _dn_fwd_call(q, k, v, gb):
    t, w = q.shape
    n, hp, hd, c = t // DN_CHUNK, DN_HP, DN_HEAD_DIM, DN_CHUNK

    def body(q_ref, k_ref, v_ref, gb_ref, o_ref, s0_ref, s_scr):
        @pl.when(pl.program_id(1) == 0)
        def _():
            s_scr[...] = jnp.zeros_like(s_scr)

        gbv = gb_ref[...]
        for j in range(hp):
            sl = slice(j * hd, (j + 1) * hd)
            s_j = s_scr[j]
            s0_ref[j, 0] = s_j
            s_new, o = _dn_step(s_j, q_ref[:, sl], k_ref[:, sl], v_ref[:, sl], gbv, pl.program_id(0) * hp + j)
            o_ref[:, sl] = o
            s_scr[j] = s_new

    blk = pl.BlockSpec((c, hp * hd), lambda g, i: (i, g))
    return pl.pallas_call(
        body, name="dn_core", grid=(DN_HEADS // hp, n),
        in_specs=[blk, blk, blk, pl.BlockSpec((c, LANES), lambda g, i: (i, 0))],
        out_specs=[blk, pl.BlockSpec((hp, 1, hd, hd), lambda g, i: (g, i, 0, 0))],
        out_shape=[jax.ShapeDtypeStruct((t, w), F32), jax.ShapeDtypeStruct((DN_HEADS, n, hd, hd), F32)],
        scratch_shapes=[pltpu.VMEM((hp, hd, hd), F32)],
        compiler_params=_params("parallel", "arbitrary"),
    )(q, k, v, gb)


def _dn_bwd_call(q, k, v, gb, s0, do):
    t, w = q.shape
    n, hp, hd, c = t // DN_CHUNK, DN_HP, DN_HEAD_DIM, DN_CHUNK
    ng = DN_HEADS // hp

    def body(q_ref, k_ref, v_ref, gb_ref, s0_ref, do_ref, dq_ref, dk_ref, dv_ref, dgb_ref, ds_scr):
        @pl.when(pl.program_id(1) == 0)
        def _():
            ds_scr[...] = jnp.zeros_like(ds_scr)

        gbv = gb_ref[...]
        dgb = jnp.zeros_like(gbv)
        for j in range(hp):
            sl = slice(j * hd, (j + 1) * hd)
            h = pl.program_id(0) * hp + j
            _, vjp = jax.vjp(lambda s_, q_, k_, v_, gb_: _dn_step(s_, q_, k_, v_, gb_, h),
                             s0_ref[j, 0], q_ref[:, sl], k_ref[:, sl], v_ref[:, sl], gbv)
            ds, dq, dk, dv, dgb_j = vjp((ds_scr[j], do_ref[:, sl]))
            ds_scr[j] = ds
            dq_ref[:, sl] = dq
            dk_ref[:, sl] = dk
            dv_ref[:, sl] = dv
            dgb = dgb + dgb_j
        dgb_ref[0] = dgb

    blk = pl.BlockSpec((c, hp * hd), lambda g, i: (n - 1 - i, g))
    res = pl.pallas_call(
        body, name="dn_core_bwd", grid=(ng, n),
        in_specs=[blk, blk, blk, pl.BlockSpec((c, LANES), lambda g, i: (n - 1 - i, 0)),
                  pl.BlockSpec((hp, 1, hd, hd), lambda g, i: (g, n - 1 - i, 0, 0)), blk],
        out_specs=[blk, blk, blk, pl.BlockSpec((1, c, LANES), lambda g, i: (g, n - 1 - i, 0))],
        out_shape=[jax.ShapeDtypeStruct((t, w), F32)] * 3 + [jax.ShapeDtypeStruct((ng, t, LANES), F32)],
        scratch_shapes=[pltpu.VMEM((hp, hd, hd), F32)],
        compiler_params=_params("parallel", "arbitrary"),
    )(q, k, v, gb, s0, do)
    return res[0], res[1], res[2], jnp.sum(res[3], axis=0)


@jax.custom_vjp
def dn_core(q, k, v, gb):
    return _dn_fwd_call(q, k, v, gb)[0]


def _dn_core_fwd(q, k, v, gb):
    o, s0 = _dn_fwd_call(q, k, v, gb)
    return o, (q, k, v, gb, s0)


def _dn_core_bwd(res, do):
    return _dn_bwd_call(*res, do)


dn_core.defvjp(_dn_core_fwd, _dn_core_bwd)


def _att_block(q, kp, kc, vp, vc, qn, kn, slope, has_prev, dil):
    s = ATT_SPAN
    qh = _rms(q, qn) * (ATT_HEAD_DIM ** -0.5)
    qi = lax.broadcasted_iota(jnp.int32, (s, s), 0)
    kj = lax.broadcasted_iota(jnp.int32, (s, s), 1)
    d_p = qi + s - kj
    d_c = qi - kj
    s_p = _dot_nt(qh, _rms(kp, kn)) - slope * (d_p * dil).astype(F32)
    s_c = _dot_nt(qh, _rms(kc, kn)) - slope * (d_c * dil).astype(F32)
    s_p = jnp.where((d_p <= s) & (has_prev > 0), s_p, NEG_INF)
    s_c = jnp.where(d_c >= 0, s_c, NEG_INF)
    m = lax.stop_gradient(jnp.maximum(jnp.max(s_p, axis=1, keepdims=True), jnp.max(s_c, axis=1, keepdims=True)))
    p_p = jnp.exp(s_p - m)
    p_c = jnp.exp(s_c - m)
    den = jnp.sum(p_p, axis=1, keepdims=True) + jnp.sum(p_c, axis=1, keepdims=True)
    o = _dot(p_p / den, vp) + _dot(p_c / den, vc)
    lse = m + jnp.log(den)
    return o, jnp.broadcast_to(lse, o.shape)


def _att_specs(dil, nblk):
    cur = pl.BlockSpec((1, ATT_SPAN, ATT_HEAD_DIM), lambda i, n: (i, n, 0))
    prev = pl.BlockSpec((1, ATT_SPAN, ATT_HEAD_DIM), lambda i, n: (i, jnp.maximum(n - 1, 0), 0))
    gain = pl.BlockSpec((1, 1, ATT_HEAD_DIM), lambda i, n: (i // dil, 0, 0))
    return cur, prev, gain


def _att_slope(group, dil):
    head = (pl.program_id(0) // dil + (4 * group + 1)).astype(F32)
    return jnp.exp(jnp.full((1, 1), -8.0 / ATT_HEADS * math.log(2.0), F32) * head)


def _att_fwd_call(name, group, dil, q, k, v, qn, kn):
    r, l, e = q.shape
    nblk = l // ATT_SPAN
    cur, prev, gain = _att_specs(dil, nblk)

    def body(q_ref, kp_ref, kc_ref, vp_ref, vc_ref, qn_ref, kn_ref, o_ref, lse_ref):
        o, lse = _att_block(q_ref[0], kp_ref[0], kc_ref[0], vp_ref[0], vc_ref[0], qn_ref[0], kn_ref[0],
                            _att_slope(group, dil), pl.program_id(1), dil)
        o_ref[0] = o
        lse_ref[0] = lse

    return pl.pallas_call(
        body, name=name, grid=(r, nblk),
        in_specs=[cur, prev, cur, prev, cur, gain, gain], out_specs=[cur, cur],
        out_shape=[jax.ShapeDtypeStruct(q.shape, F32)] * 2,
        compiler_params=_params("parallel", "arbitrary"),
    )(q, k, k, v, v, qn, kn)


def _att_bwd_call(name, group, dil, q, k, v, qn, kn, do, dlse):
    r, l, e = q.shape
    nblk = l // ATT_SPAN
    cur, prev, gain = _att_specs(dil, nblk)

    def body(q_ref, kp_ref, kc_ref, vp_ref, vc_ref, qn_ref, kn_ref, do_ref, dlse_ref,
             dq_ref, dkp_ref, dkc_ref, dvp_ref, dvc_ref, dqn_ref, dkn_ref):
        slope, has_prev = _att_slope(group, dil), pl.program_id(1)
        _, vjp = jax.vjp(lambda *a: _att_block(*a, slope, has_prev, dil),
                         q_ref[0], kp_ref[0], kc_ref[0], vp_ref[0], vc_ref[0], qn_ref[0], kn_ref[0])
        dq, dkp, dkc, dvp, dvc, dqn, dkn = vjp((do_ref[0], dlse_ref[0]))
        dq_ref[0], dkp_ref[0], dkc_ref[0], dvp_ref[0], dvc_ref[0] = dq, dkp, dkc, dvp, dvc

        @pl.when((pl.program_id(0) % dil == 0) & (pl.program_id(1) == 0))
        def _():
            dqn_ref[...] = jnp.zeros_like(dqn_ref)
            dkn_ref[...] = jnp.zeros_like(dkn_ref)

        dqn_ref[0] += dqn
        dkn_ref[0] += dkn

    res = pl.pallas_call(
        body, name=name + "_bwd", grid=(r, nblk),
        in_specs=[cur, prev, cur, prev, cur, gain, gain, cur, cur],
        out_specs=[cur] * 5 + [gain, gain],
        out_shape=[jax.ShapeDtypeStruct(q.shape, F32)] * 5 + [jax.ShapeDtypeStruct(qn.shape, F32)] * 2,
        compiler_params=_params("arbitrary", "arbitrary"),
    )(q, k, k, v, v, qn, kn, do, dlse)
    dq, dkp, dkc, dvp, dvc, dqn, dkn = res
    back = lambda g: jnp.pad(g[:, ATT_SPAN:], ((0, 0), (0, ATT_SPAN), (0, 0)))
    return dq, dkc + back(dkp), dvc + back(dvp), dqn, dkn


def att_group(name, group, dil, q, k, v, qn, kn):
    @jax.custom_vjp
    def op(q, k, v, qn, kn):
        return tuple(_att_fwd_call(name, group, dil, q, k, v, qn, kn))

    def op_fwd(q, k, v, qn, kn):
        return op(q, k, v, qn, kn), (q, k, v, qn, kn)

    def op_bwd(res, cts):
        return _att_bwd_call(name, group, dil, *res, *cts)

    op.defvjp(op_fwd, op_bwd)
    return op(q, k, v, qn, kn)


def _att_mix(o1, o2, o3, l1, l2, l3):
    m = jnp.maximum(jnp.maximum(l1, l2), l3)
    e1, e2, e3 = jnp.exp(l1 - m), jnp.exp(l2 - m), jnp.exp(l3 - m)
    s = e1 + e2 + e3
    return (jnp.concatenate([o1 * (e1 / s), o2 * (e2 / s), o3 * (e3 / s)], axis=1),)


def att_branch(name, pa, qn, kn, groups=ATT_GROUPS):
    t = pa.shape[0]
    hg, e = 4, ATT_HEAD_DIM
    q, k, v = (pa[:, i * 768:(i + 1) * 768].reshape(t, ATT_HEADS, e) for i in range(3))
    outs, lses = [], []
    for g, (window, dil) in enumerate(groups):
        assert window // dil == ATT_SPAN
        l = t // dil
        to_r = lambda a: a[:, hg * g:hg * (g + 1)].reshape(l, dil, hg, e).transpose(2, 1, 0, 3).reshape(hg * dil, l, e)
        back = lambda a: a.reshape(hg, dil, l, e).transpose(2, 1, 0, 3).reshape(t, hg * e)
        o, lse = att_group(f"{name}_att{g}", g, dil, to_r(q), to_r(k), to_r(v),
                           qn[hg * g:hg * (g + 1)].reshape(hg, 1, e), kn[hg * g:hg * (g + 1)].reshape(hg, 1, e))
        outs.append(back(o))
        lses.append(back(lse))
    return rowwise(f"{name}_attmix", _att_mix, outs + lses)[0]


def dn_gates(name, ba, a_log, dt_bias):
    place = lambda p: jnp.pad(p.reshape(1, DN_HEADS), ((0, 0), (DN_HEADS, LANES - 2 * DN_HEADS)))

    def f(x, al, dt):
        lane = lax.broadcasted_iota(jnp.int32, x.shape, 1)
        return (jnp.where(lane < DN_HEADS, _sigmoid(x), -jnp.exp(al) * _softplus(x + dt)),)

    return rowwise(name, f, (ba,), (place(a_log), place(dt_bias)))[0]


def _dn_out(o, z, g):
    parts = []
    for h in range(DN_HEADS):
        sl = slice(h * DN_HEAD_DIM, (h + 1) * DN_HEAD_DIM)
        parts.append(_rms(o[:, sl], g[:, sl]) * _silu(z[:, sl]))
    return (jnp.concatenate(parts, axis=1),)


def _merge(ml, za, zb, zc):
    d = D_MODEL
    return (_sigmoid(ml[:, :d]) * za + _sigmoid(ml[:, d:2 * d]) * zb + _sigmoid(ml[:, 2 * d:]) * zc,)


def _swiglu_act(g, u):
    return (_silu(g) * u,)


def add_norm(name, x, pend, scale, gain):
    if pend is None:
        return x, rowwise(name, lambda a, g: (_rms(a, g),), (x,), (gain,))[0]

    def f(a, b, g):
        s = a + scale * b
        return s, _rms(s, g)

    return rowwise(name, f, (x, pend), (gain,))


def ffn(name, h, wg, wu, wd):
    g, u = mm_cols(name + "_g", h, wg), mm_cols(name + "_u", h, wu)
    j, t, n = g.shape
    a = rowwise(name + "_act", _swiglu_act, (g.reshape(j * t, n), u.reshape(j * t, n)))[0]
    return mm_rows(name + "_d", a.reshape(j, t, n), wd)


W_IN_PIECES = (("rgx", 0, 1024), ("gate", 1024, 1024), ("att", 2048, 2304), ("dq", 4352, 1024), ("dk", 5376, 1024),
               ("dv", 6400, 1024), ("dz", 7424, 1024), ("ba", 8448, 16), ("mrg", 8464, 3072))
RG_PAR_BLOCKS = ("lane", "lane", "blk", "lane", "blk", "lane", "lane")


def mixer(name, u, w, p):
    mm = lambda nm, a, wt: mm_rows(nm, a[None], wt[None])
    pr = {k: mm_cols(f"{name}_in_{k}", u, w["in_" + k][None])[0] for k, _, _ in W_IN_PIECES}
    ya = colwise(name + "_rg", _rg_block, (pr["rgx"], pr["gate"]),
                 (w["rg_conv_w"], p["rg_conv_b"], _block_diag(p["rg_w_r"]), p["rg_b_r"], _block_diag(p["rg_w_i"]),
                  p["rg_b_i"], p["rg_lambda"]), RG_PAR_BLOCKS, 1)[0]
    yb = att_branch(name, pr["att"], p["att_q_norm"], p["att_k_norm"])
    cw = w["dn_conv_w"]
    cq = colwise(name + "_dnq", _dn_conv_block("q"), (pr["dq"],), (cw[:, :1024],), ("lane",), 1)[0]
    ck = colwise(name + "_dnk", _dn_conv_block("k"), (pr["dk"],), (cw[:, 1024:2048],), ("lane",), 1)[0]
    cv = colwise(name + "_dnv", _dn_conv_block("v"), (pr["dv"],), (cw[:, 2048:],), ("lane",), 1)[0]
    gb = dn_gates(name + "_dngate", pr["ba"], p["dn_a_log"], p["dn_dt_bias"])
    o_dn = dn_core(cq, ck, cv, gb)
    yc = rowwise(name + "_dnout", _dn_out, (o_dn, pr["dz"]), (p["dn_out_norm"].reshape(1, D_MODEL),))[0]
    y = rowwise(name + "_merge", _merge, (pr["mrg"], mm(name + "_ba", ya, w["br_a"]), mm(name + "_bb", yb, w["br_b"]),
                                          mm(name + "_bc", yc, w["br_c"])))[0]
    return mm(name + "_out", y, w["w_out"])


def _loss_call(x, pend, target):
    t, d = x.shape
    tile = _row_tile(t)

    def body(x_ref, p_ref, t_ref, loss_ref, g_ref):
        err = x_ref[...] + 0.5 * p_ref[...] - t_ref[...]
        g_ref[...] = err * (1.0 / d)

        @pl.when(pl.program_id(0) == 0)
        def _():
            loss_ref[...] = jnp.zeros_like(loss_ref)

        loss_ref[...] += jnp.full(loss_ref.shape, 0.5 / d, F32) * jnp.sum(err * err)

    blk = pl.BlockSpec((tile, d), lambda i: (i, 0))
    loss, g = pl.pallas_call(
        body, name="loss", grid=(t // tile,), in_specs=[blk, blk, blk],
        out_specs=[pl.BlockSpec((8, LANES), lambda i: (0, 0)), blk],
        out_shape=[jax.ShapeDtypeStruct((8, LANES), F32), jax.ShapeDtypeStruct((t, d), F32)],
        compiler_params=_params("arbitrary"),
    )(x, pend, target)
    return loss[0, 0], g


@jax.custom_vjp
def loss_op(x, pend, target):
    return _loss_call(x, pend, target)[0]


def _loss_fwd(x, pend, target):
    loss, g = _loss_call(x, pend, target)
    return loss, g


def _loss_bwd(g, ct):
    return ct * g, (0.5 * ct) * g, None


loss_op.defvjp(_loss_fwd, _loss_bwd)


def local_loss(w, p, x, target):
    pend, scale = None, 0.0
    for l in range(len(w)):
        n = f"L{l}"
        x, h = add_norm(n + "_n1", x, pend, scale, p[l]["ffn1_norm"])
        pend, scale = ffn(n + "_f1", h, w[l]["ffn1_w_gate"], w[l]["ffn1_w_up"], w[l]["ffn1_w_down"]), 0.5
        x, h = add_norm(n + "_nm", x, pend, scale, p[l]["mix_norm"])
        pend, scale = mixer(n + "_mx", h, w[l], p[l]), 1.0
        x, h = add_norm(n + "_n2", x, pend, scale, p[l]["ffn2_norm"])
        pend, scale = ffn(n + "_f2", h, w[l]["ffn2_w_gate"], w[l]["ffn2_w_up"], w[l]["ffn2_w_down"]), 0.5
    return loss_op(x, pend, target)


WEIGHT_NAMES = ("ffn1_norm", "ffn1_w_gate", "ffn1_w_up", "ffn1_w_down", "mix_norm", "w_in", "rg_conv_w", "rg_conv_b",
                "rg_w_r", "rg_b_r", "rg_w_i", "rg_b_i", "rg_lambda", "att_q_norm", "att_k_norm", "dn_conv_w", "dn_a_log",
                "dn_dt_bias", "dn_out_norm", "w_branch", "w_out", "ffn2_norm", "ffn2_w_gate", "ffn2_w_up", "ffn2_w_down")
MATRICES = (("ffn1_w_gate", 2), ("ffn1_w_up", 2), ("ffn1_w_down", 1), ("w_in", 2), ("w_branch", 1), ("w_out", 1),
            ("ffn2_w_gate", 2), ("ffn2_w_up", 2), ("ffn2_w_down", 1))
CONVS = (("rg_conv_w", 2), ("dn_conv_w", 2))
SHARD_AXIS = dict(MATRICES + CONVS)
SMALL_NAMES = tuple(n for n in WEIGHT_NAMES if n not in SHARD_AXIS)
ROW_PARAMS = ("ffn1_norm", "mix_norm", "rg_conv_b", "rg_b_r", "rg_b_i", "rg_lambda", "ffn2_norm")
FFN_MATS = ("ffn1_w_gate", "ffn1_w_up", "ffn1_w_down", "ffn2_w_gate", "ffn2_w_up", "ffn2_w_down")
W_IN_SHARD = 2884


def _shard_minor(a, axis):
    a = jnp.moveaxis(a, 0, axis)
    return a.reshape(a.shape[:axis] + (N_CHIPS * a.shape[axis + 1],) + a.shape[axis + 2:])


def _w_in_piece(g, l, off, n):
    s = W_IN_SHARD
    parts = [g[j, l][:, max(off, j * s) - j * s:min(off + n, (j + 1) * s) - j * s]
             for j in range(N_CHIPS) if max(off, j * s) < min(off + n, (j + 1) * s)]
    return jnp.concatenate(parts, axis=1) if len(parts) > 1 else parts[0]


def _w_in_chip_grad(gl, j):
    s = W_IN_SHARD
    parts = [gl["in_" + k][:, max(off, j * s) - off:min(off + n, (j + 1) * s) - off]
             for k, off, n in W_IN_PIECES if max(off, j * s) < min(off + n, (j + 1) * s)]
    return jnp.concatenate(parts, axis=1)


def layer_weights(g, conv, l):
    w = {n: g[n][:, l] for n in FFN_MATS}
    w["w_out"] = g["w_out"][:, l].reshape(D_MODEL, D_MODEL)
    for k, off, n in W_IN_PIECES:
        piece = _w_in_piece(g["w_in"], l, off, n)
        w["in_" + k] = jnp.pad(piece, ((0, 0), (0, LANES - n))) if n < LANES else piece
    wb = g["w_branch"][:, l].reshape(-1, D_MODEL)
    w["br_a"], w["br_b"], w["br_c"] = wb[:1024], wb[1024:1792], wb[1792:]
    w["rg_conv_w"], w["dn_conv_w"] = conv["rg_conv_w"][l], conv["dn_conv_w"][l]
    return w


def layer_weight_grads(gw):
    stack = lambda f: jnp.stack([f(gl) for gl in gw], axis=1)
    out = {n: stack(lambda gl: gl[n]) for n in FFN_MATS}
    out["w_out"] = stack(lambda gl: gl["w_out"].reshape(N_CHIPS, -1, D_MODEL))
    out["w_branch"] = stack(lambda gl: jnp.concatenate([gl["br_a"], gl["br_b"], gl["br_c"]], axis=0).reshape(N_CHIPS, -1, D_MODEL))
    out["w_in"] = stack(lambda gl: jnp.stack([_w_in_chip_grad(gl, j) for j in range(N_CHIPS)]))
    conv = {n: jnp.stack([gl[n] for gl in gw]) for n, _ in CONVS}
    return out, conv


def layer_small(small, l):
    p = {n: small[n][l] for n in SMALL_NAMES}
    for n in ROW_PARAMS:
        p[n] = small[n][l:l + 1]
    return p


def layer_small_grads(gp, small):
    return {n: jnp.stack([g[n] for g in gp]).reshape(small[n].shape) for n in SMALL_NAMES}


HBM_SPEC = pl.BlockSpec(memory_space=pl.ANY)


def _place():
    x, y, c = lax.axis_index("x"), lax.axis_index("y"), lax.axis_index("c")
    other_chips = [(1 - x, y), (x, 1 - y), (1 - x, 1 - y)]
    return x, y, c, 2 * x + y, (x, y, 1 - c), other_chips


def _half_rows(ref, lead, hc):
    hr = ref.shape[-2] // 2
    return ref.at[(*lead, slice(None), pl.ds(pl.multiple_of(hc * hr, 16), hr), slice(None))]


def allgather_mats(mats):
    n = len(mats)

    def body(*refs):
        ins, outs = refs[:n], refs[n:2 * n]
        send_sems, recv_sems, local_sems = refs[2 * n:]
        x, y, c, me, sibling, chips = _place()

        def copy(s, src, dst, to):
            return pltpu.make_async_remote_copy(src_ref=src, dst_ref=dst, send_sem=send_sems.at[s], recv_sem=recv_sems.at[s],
                                                device_id=to, device_id_type=MESH)

        local, first, passed = [], [], []
        for i in range(n):
            cp = pltpu.make_async_copy(ins[i], outs[i].at[me], local_sems.at[i])
            cp.start()
            local.append(cp)
        for j, (cx, cy) in enumerate(chips):
            for i in range(n):
                cp = copy(6 * i + j, _half_rows(ins[i], (), c), _half_rows(outs[i], (me,), c), (cx, cy, c))
                cp.start()
                first.append(cp)
        for j, (cx, cy) in enumerate(chips):
            k = 2 * cx + cy
            for i in range(n):
                copy(6 * i + j, _half_rows(ins[i], (), c), _half_rows(outs[i], (k,), c), (cx, cy, c)).wait_recv()
                cp = copy(6 * i + 3 + j, _half_rows(outs[i], (k,), c), _half_rows(outs[i], (k,), c), sibling)
                cp.start()
                passed.append(cp)
        for j, (cx, cy) in enumerate(chips):
            k = 2 * cx + cy
            for i in range(n):
                copy(6 * i + 3 + j, _half_rows(ins[i], (), c), _half_rows(outs[i], (k,), 1 - c), sibling).wait_recv()
        for cp in first + passed:
            cp.wait_send()
        for cp in local:
            cp.wait()

    return pl.pallas_call(
        body, name="allgather_mats", out_shape=[jax.ShapeDtypeStruct((N_CHIPS,) + m.shape, m.dtype) for m in mats],
        in_specs=[HBM_SPEC] * n, out_specs=[HBM_SPEC] * n,
        scratch_shapes=[pltpu.SemaphoreType.DMA((6 * n,)), pltpu.SemaphoreType.DMA((6 * n,)), pltpu.SemaphoreType.DMA((n,))],
    )(*mats)


PEER_FLIPS = tuple((fx, fy, fc) for fx in (0, 1) for fy in (0, 1) for fc in (0, 1))[1:]


def exchange_grad_pieces(gs):
    n = len(gs)

    def body(*refs):
        ins, outs = refs[:n], refs[n:2 * n]
        send_sems, recv_sems, local_sems = refs[2 * n:]
        x, y, c, me, sibling, chips = _place()
        my_dev = 4 * x + 2 * y + c
        flip = lambda v, f: 1 - v if f else v
        local, sends = [], []
        for i in range(n):
            cp = pltpu.make_async_copy(_half_rows(ins[i], (me,), c), outs[i].at[my_dev], local_sems.at[i])
            cp.start()
            local.append(cp)
        for r, (fx, fy, fc) in enumerate(PEER_FLIPS):
            px, py, pc = flip(x, fx), flip(y, fy), flip(c, fc)
            for i in range(n):
                cp = pltpu.make_async_remote_copy(
                    src_ref=_half_rows(ins[i], (2 * px + py,), pc), dst_ref=outs[i].at[my_dev], send_sem=send_sems.at[7 * i + r],
                    recv_sem=recv_sems.at[7 * i + r], device_id=(px, py, pc), device_id_type=MESH)
                cp.start()
                sends.append(cp)
        for r, (fx, fy, fc) in enumerate(PEER_FLIPS):
            px, py, pc = flip(x, fx), flip(y, fy), flip(c, fc)
            for i in range(n):
                pltpu.make_async_remote_copy(
                    src_ref=_half_rows(ins[i], (me,), c), dst_ref=outs[i].at[4 * px + 2 * py + pc], send_sem=send_sems.at[7 * i + r],
                    recv_sem=recv_sems.at[7 * i + r], device_id=(px, py, pc), device_id_type=MESH).wait_recv()
        for cp in sends:
            cp.wait_send()
        for cp in local:
            cp.wait()

    return pl.pallas_call(
        body, name="exchange_grad_pieces",
        out_shape=[jax.ShapeDtypeStruct((N_DEV, g.shape[1], g.shape[2] // 2, g.shape[3]), g.dtype) for g in gs],
        in_specs=[HBM_SPEC] * n, out_specs=[HBM_SPEC] * n,
        scratch_shapes=[pltpu.SemaphoreType.DMA((7 * n,)), pltpu.SemaphoreType.DMA((7 * n,)), pltpu.SemaphoreType.DMA((n,))],
    )(*gs)


def sibling_share_halves(fs):
    n = len(fs)

    def body(*refs):
        ins, outs = refs[:n], refs[n:2 * n]
        send_sems, recv_sems, local_sems = refs[2 * n:]
        x, y, c, me, sibling, chips = _place()
        local, sends = [], []
        for i in range(n):
            cp = pltpu.make_async_copy(ins[i], _half_rows(outs[i], (), c), local_sems.at[i])
            cp.start()
            local.append(cp)
            cp = pltpu.make_async_remote_copy(src_ref=ins[i], dst_ref=_half_rows(outs[i], (), c), send_sem=send_sems.at[i],
                                              recv_sem=recv_sems.at[i], device_id=sibling, device_id_type=MESH)
            cp.start()
            sends.append(cp)
        for i in range(n):
            pltpu.make_async_remote_copy(src_ref=ins[i], dst_ref=_half_rows(outs[i], (), 1 - c), send_sem=send_sems.at[i],
                                         recv_sem=recv_sems.at[i], device_id=sibling, device_id_type=MESH).wait_recv()
        for cp in sends:
            cp.wait_send()
        for cp in local:
            cp.wait()

    return pl.pallas_call(
        body, name="sibling_share_halves",
        out_shape=[jax.ShapeDtypeStruct((f.shape[0], 2 * f.shape[1], f.shape[2]), f.dtype) for f in fs],
        in_specs=[HBM_SPEC] * n, out_specs=[HBM_SPEC] * n,
        scratch_shapes=[pltpu.SemaphoreType.DMA((n,)), pltpu.SemaphoreType.DMA((n,)), pltpu.SemaphoreType.DMA((n,))],
    )(*fs)


def allgather_small(name, v):
    m_per, n = v.shape

    def body(x_ref, out_ref, send_sems, recv_sems, local_sem):
        x, y, c, _, sibling, chips = _place()
        me = (x, y, c)

        def rows(px, py, pc):
            return out_ref.at[pl.ds((4 * px + 2 * py + pc) * m_per, m_per), :]

        def copy(k, block, to, src=None):
            return pltpu.make_async_remote_copy(src_ref=rows(*block) if src is None else src, dst_ref=rows(*block),
                                                send_sem=send_sems.at[k], recv_sem=recv_sems.at[k], device_id=to, device_id_type=MESH)

        mine = pltpu.make_async_copy(x_ref, rows(*me), local_sem)
        mine.start()
        first = [copy(0, me, sibling, src=x_ref)]
        first += [copy(1 + j, me, (*chip, c), src=x_ref) for j, chip in enumerate(chips)]
        for cp in first:
            cp.start()
        passed = [copy(4 + j, (*chip, c), sibling) for j, chip in enumerate(chips)]
        for j, chip in enumerate(chips):
            copy(1 + j, (*chip, c), me).wait_recv()
            passed[j].start()
        copy(0, sibling, me).wait_recv()
        for j, chip in enumerate(chips):
            copy(4 + j, (*chip, 1 - c), me).wait_recv()
        for cp in first + passed:
            cp.wait_send()
        mine.wait()

    return pl.pallas_call(
        body, name=name, out_shape=jax.ShapeDtypeStruct((N_DEV * m_per, n), v.dtype),
        in_specs=[pl.BlockSpec(memory_space=pltpu.VMEM)], out_specs=pl.BlockSpec(memory_space=pltpu.VMEM),
        scratch_shapes=[pltpu.SemaphoreType.DMA((7,)), pltpu.SemaphoreType.DMA((7,)), pltpu.SemaphoreType.DMA],
        compiler_params=pltpu.CompilerParams(vmem_limit_bytes=VMEM_LIMIT),
    )(v)


SUM_BLOCK_ELEMS = 512 * 1024


def sum_slabs(name, b):
    k, l, h, w = b.shape
    tile = max(t for t in range(8, h + 1, 8) if h % t == 0 and (t % 16 == 0 or t == h) and (t * w <= SUM_BLOCK_ELEMS or t <= 16))

    def body(b_ref, o_ref):
        acc = b_ref[0, 0].astype(F32)
        for i in range(1, k):
            acc = acc + b_ref[i, 0].astype(F32)
        o_ref[0] = acc

    return pl.pallas_call(
        body, name=name, grid=(l, h // tile),
        in_specs=[pl.BlockSpec((k, 1, tile, w), lambda a, i: (0, a, i, 0))],
        out_specs=pl.BlockSpec((1, tile, w), lambda a, i: (a, i, 0)),
        out_shape=jax.ShapeDtypeStruct((l, h, w), F32), compiler_params=_params("parallel", "parallel"),
    )(b)


def _adam_block(w, g, m, v):
    m = ADAM_B1 * m + (1.0 - ADAM_B1) * g
    v = ADAM_B2 * v + (1.0 - ADAM_B2) * (g * g)
    m_hat = m / (1.0 - ADAM_B1 ** ADAM_STEP)
    v_hat = v / (1.0 - ADAM_B2 ** ADAM_STEP)
    return -ADAM_LR * (m_hat / (jnp.sqrt(v_hat) + ADAM_EPS) + ADAM_WD * w), m, v


def adamw(name, w, g, m, v):
    shape = w.shape
    cols = shape[-1]
    rows = w.size // cols
    tile = 128 if rows % 128 == 0 else rows
    flat = [a.reshape(rows, cols) for a in (w, g, m, v)]

    def body(w_ref, g_ref, m_ref, v_ref, d_ref, nm_ref, nv_ref):
        d_ref[...], nm_ref[...], nv_ref[...] = _adam_block(w_ref[...], g_ref[...], m_ref[...], v_ref[...])

    blk = pl.BlockSpec((tile, cols), lambda i: (i, 0))
    res = pl.pallas_call(
        body, name=name, grid=(rows // tile,), in_specs=[blk] * 4, out_specs=[blk] * 3,
        out_shape=[jax.ShapeDtypeStruct((rows, cols), F32)] * 3, compiler_params=_params("parallel"),
    )(*flat)
    return tuple(r.reshape(shape) for r in res)


def _pack_small(grads):
    flat = jnp.concatenate([grads[n].reshape(-1) for n in SMALL_NAMES + tuple(n for n, _ in CONVS)])
    n = flat.shape[0]
    total = -(-n // (8 * LANES)) * (8 * LANES)
    return jnp.pad(flat, (0, total - n)).reshape(-1, LANES)


def _unpack_small(v, shapes):
    flat = v.reshape(-1)
    out, off = {}, 0
    for n in SMALL_NAMES + tuple(n for n, _ in CONVS):
        sz = int(np.prod(shapes[n]))
        out[n] = flat[off:off + sz].reshape(shapes[n])
        off += sz
    return out


def kernel(x, ffn1_norm, ffn1_w_gate, ffn1_w_up, ffn1_w_down, mix_norm, w_in, rg_conv_w, rg_conv_b, rg_w_r, rg_b_r, rg_w_i, rg_b_i, rg_lambda, att_q_norm, att_k_norm, dn_conv_w, dn_a_log, dn_dt_bias, dn_out_norm, w_branch, w_out, ffn2_norm, ffn2_w_gate, ffn2_w_up, ffn2_w_down, loss_target, m_ffn1_norm, m_ffn1_w_gate, m_ffn1_w_up, m_ffn1_w_down, m_mix_norm, m_w_in, m_rg_conv_w, m_rg_conv_b, m_rg_w_r, m_rg_b_r, m_rg_w_i, m_rg_b_i, m_rg_lambda, m_att_q_norm, m_att_k_norm, m_dn_conv_w, m_dn_a_log, m_dn_dt_bias, m_dn_out_norm, m_w_branch, m_w_out, m_ffn2_norm, m_ffn2_w_gate, m_ffn2_w_up, m_ffn2_w_down, v_ffn1_norm, v_ffn1_w_gate, v_ffn1_w_up, v_ffn1_w_down, v_mix_norm, v_w_in, v_rg_conv_w, v_rg_conv_b, v_rg_w_r, v_rg_b_r, v_rg_w_i, v_rg_b_i, v_rg_lambda, v_att_q_norm, v_att_k_norm, v_dn_conv_w, v_dn_a_log, v_dn_dt_bias, v_dn_out_norm, v_w_branch, v_w_out, v_ffn2_norm, v_ffn2_w_gate, v_ffn2_w_up, v_ffn2_w_down):
    given = dict(locals())
    small = {n: given[n] for n in SMALL_NAMES}
    n_layers = ffn1_norm.shape[0]
    mat_names = [n for n, _ in MATRICES]
    conv_names = [n for n, _ in CONVS]

    gathered = dict(zip(mat_names, allgather_mats([given[n].astype(BF16) for n in mat_names])))
    taps = jnp.concatenate([given[n].reshape(-1) for n in conv_names]).reshape(-1, LANES)
    taps = allgather_small("allgather_taps", taps).reshape(N_CHIPS, 2, -1)[:, 0]
    conv, off = {}, 0
    for n, ax in CONVS:
        sz = given[n].size
        conv[n] = _shard_minor(taps[:, off:off + sz].reshape((N_CHIPS,) + given[n].shape), ax)
        off += sz
    w = [layer_weights(gathered, conv, l) for l in range(n_layers)]
    p = [layer_small(small, l) for l in range(n_layers)]

    loss, (gw, gp, gx) = jax.value_and_grad(local_loss, argnums=(0, 1, 2))(w, p, x[0], loss_target[0])
    loss = lax.psum(loss, ("x", "y", "c"))
    g_mats, g_conv = layer_weight_grads(gw)

    pieces = exchange_grad_pieces([g_mats[n] for n in mat_names])
    halves = [sum_slabs("sum_" + n, b) for n, b in zip(mat_names, pieces)]
    grads = dict(zip(mat_names, sibling_share_halves(halves)))

    g_small = dict(layer_small_grads(gp, small), **g_conv)
    packed_small = _pack_small(g_small)
    slabs = allgather_small("allgather_small", packed_small).reshape(N_DEV, 1, packed_small.shape[0], LANES)
    summed = _unpack_small(sum_slabs("sum_small", slabs)[0], {n: g.shape for n, g in g_small.items()})
    chip = 2 * lax.axis_index("x") + lax.axis_index("y")
    for n in SMALL_NAMES:
        grads[n] = summed[n]
    for n, ax in CONVS:
        s = given[n].shape[ax]
        grads[n] = lax.dynamic_slice_in_dim(summed[n], chip * s, s, axis=ax)

    upd = {n: adamw("adamw_" + n, given[n], grads[n], given["m_" + n], given["v_" + n]) for n in WEIGHT_NAMES}
    return (loss, gx[None], *[grads[n] for n in WEIGHT_NAMES], *[upd[n][0] for n in WEIGHT_NAMES],
            *[upd[n][1] for n in WEIGHT_NAMES], *[upd[n][2] for n in WEIGHT_NAMES])
```

```python
import functools
import math

import jax
import jax.numpy as jnp
import numpy as np
from jax import lax
from jax.experimental import pallas as pl
from jax.experimental.pallas import tpu as pltpu

F32 = jnp.float32
BF16 = jnp.bfloat16
MESH = pl.DeviceIdType.MESH

D_MODEL = 1024
FFN_DIM = 2816
RG_C = 8.0
ATT_GROUPS = ((128, 1), (512, 4), (2048, 16))
ATT_HEADS = 12
ATT_HEAD_DIM = 64
ATT_SPAN = 128
DN_HEADS = 8
DN_HEAD_DIM = 128
DN_CHUNK = 64
EPS = 1e-6
NEG_INF = -1e30
N_CHIPS = 4
N_DEV = 8

ADAM_LR, ADAM_B1, ADAM_B2, ADAM_EPS, ADAM_WD, ADAM_STEP = 0.001, 0.9, 0.999, 1e-08, 0.01, 10

LANES = 128
VMEM_LIMIT = 56 * 1024 * 1024


def _params(*sem):
    return pltpu.CompilerParams(dimension_semantics=sem or None, vmem_limit_bytes=VMEM_LIMIT)


def _sigmoid(x):
    return 1.0 / (1.0 + jnp.exp(-x))


def _silu(x):
    return x * _sigmoid(x)


def _softplus(x):
    return jnp.maximum(x, 0.0) + jnp.log(1.0 + jnp.exp(-jnp.abs(x)))


def _gelu(x):
    return 0.5 * x * (1.0 + jnp.tanh(math.sqrt(2.0 / math.pi) * (x + 0.044715 * (x * x * x))))


def _neg_expm1(x):
    series = -x * (1.0 + x * (0.5 + x * (1.0 / 6 + x * (1.0 / 24 + x * (1.0 / 120 + x * (1.0 / 720))))))
    return jnp.where(x > -0.25, series, 1.0 - jnp.exp(x))


def _rms(x, g):
    return x * lax.rsqrt(jnp.mean(x * x, axis=-1, keepdims=True) + EPS) * g


_MM_DIMS = {"nn": (((1,), (0,)), ((), ())), "nt": (((1,), (1,)), ((), ())), "tn": (((0,), (0,)), ((), ()))}


def _split(a):
    hi = a.astype(BF16)
    return hi, (a - hi.astype(F32)).astype(BF16)


def _mxu(a, b, form, passes):
    dg = lambda p, q: lax.dot_general(p, q, _MM_DIMS[form], preferred_element_type=F32)
    if passes == 1:
        return dg(a.astype(BF16), b.astype(BF16))
    (a_hi, a_lo), (b_hi, b_lo) = _split(a), _split(b)
    return dg(a_hi, b_hi) + (dg(a_hi, b_lo) + dg(a_lo, b_hi))


@functools.partial(jax.custom_vjp, nondiff_argnums=(2, 3))
def _mm(a, b, form, passes):
    return _mxu(a, b, form, passes)


def _mm_fwd(a, b, form, passes):
    return _mxu(a, b, form, passes), (a, b)


def _mm_bwd(form, passes, res, g):
    a, b = res
    if form == "nn":
        return _mm(g, b, "nt", passes), _mm(a, g, "tn", passes)
    if form == "nt":
        return _mm(g, b, "nn", passes), _mm(g, a, "tn", passes)
    return _mm(b, g, "nt", passes), _mm(a, g, "nn", passes)


_mm.defvjp(_mm_fwd, _mm_bwd)


def _dot(a, b):
    return _mm(a, b, "nn", 1)


def _dot_nt(a, b):
    return _mm(a, b, "nt", 1)


def _dot_tn(a, b):
    return _mm(a, b, "tn", 1)


def _dot3(a, b):
    return _mm(a, b, "nn", 3)


def _rows(shape):
    return lax.broadcasted_iota(jnp.int32, shape, 0)


def _roll_down(x, s, fill):
    return jnp.where(_rows(x.shape) >= s, pltpu.roll(x, s, 0), fill)


def _roll_up(x, s, fill):
    n = x.shape[0]
    return jnp.where(_rows(x.shape) < n - s, pltpu.roll(x, n - s, 0), fill)


@functools.partial(jax.custom_vjp, nondiff_argnums=(1,))
def _shift(x, s):
    return _roll_down(x, s, 0.0)


def _shift_fwd(x, s):
    return _roll_down(x, s, 0.0), None


def _shift_bwd(s, _, g):
    return (_roll_up(g, s, 0.0),)


_shift.defvjp(_shift_fwd, _shift_bwd)


def _causal_conv(x, w):
    return w[0:1] * _shift(x, 3) + w[1:2] * _shift(x, 2) + w[2:3] * _shift(x, 1) + w[3:4] * x


@jax.custom_vjp
def _lin_scan(a, b):
    return _lin_scan_fwd(a, b)[0]


def _lin_scan_fwd(a, b):
    a0 = a
    s = 1
    while s < a.shape[0]:
        b = a * _roll_down(b, s, 0.0) + b
        a = a * _roll_down(a, s, 1.0)
        s *= 2
    return b, (a0, b)


def _lin_scan_bwd(res, g):
    a, h = res
    c = _roll_up(a, 1, 0.0)
    s = 1
    while s < a.shape[0]:
        g = c * _roll_up(g, s, 0.0) + g
        c = c * _roll_up(c, s, 1.0)
        s *= 2
    return g * _roll_down(h, 1, 0.0), g


_lin_scan.defvjp(_lin_scan_fwd, _lin_scan_bwd)


@jax.custom_vjp
def _cumsum_rows(x):
    s = 1
    while s < x.shape[0]:
        x = x + _roll_down(x, s, 0.0)
        s *= 2
    return x


def _cumsum_rows_fwd(x):
    return _cumsum_rows(x), None


def _cumsum_rows_bwd(_, g):
    s = 1
    while s < g.shape[0]:
        g = g + _roll_up(g, s, 0.0)
        s *= 2
    return (g,)


_cumsum_rows.defvjp(_cumsum_rows_fwd, _cumsum_rows_bwd)


def _row_tile(t):
    return 256 if t % 256 == 0 else t


def _rowwise_fwd_call(name, f, rows, pars, tile):
    t = rows[0].shape[0]
    outs = jax.eval_shape(f, *[jax.ShapeDtypeStruct((tile, r.shape[1]), F32) for r in rows],
                          *[jax.ShapeDtypeStruct(p.shape, F32) for p in pars])
    nr, npar = len(rows), len(pars)

    def body(*refs):
        ins = [r[...] for r in refs[:nr + npar]]
        res = f(*ins)
        for o_ref, o in zip(refs[nr + npar:], res):
            o_ref[...] = o.astype(o_ref.dtype)

    return pl.pallas_call(
        body, name=name, grid=(t // tile,),
        in_specs=[pl.BlockSpec((tile, r.shape[1]), lambda i: (i, 0)) for r in rows]
        + [pl.BlockSpec(p.shape, lambda i: (0, 0)) for p in pars],
        out_specs=[pl.BlockSpec((tile, o.shape[1]), lambda i: (i, 0)) for o in outs],
        out_shape=[jax.ShapeDtypeStruct((t, o.shape[1]), F32) for o in outs],
        compiler_params=_params("parallel"),
    )(*rows, *pars)


def _rowwise_bwd_call(name, f, rows, pars, cts, tile):
    t = rows[0].shape[0]
    nr, npar, nct = len(rows), len(pars), len(cts)

    def body(*refs):
        ins = [r[...] for r in refs[:nr + npar]]
        gs = tuple(r[...] for r in refs[nr + npar:nr + npar + nct])
        outs = refs[nr + npar + nct:]
        _, vjp = jax.vjp(f, *ins)
        d = vjp(gs)
        for o_ref, v in zip(outs[:nr], d[:nr]):
            o_ref[...] = v

        @pl.when(pl.program_id(0) == 0)
        def _():
            for o_ref in outs[nr:]:
                o_ref[...] = jnp.zeros_like(o_ref)

        for o_ref, v in zip(outs[nr:], d[nr:]):
            o_ref[...] += v

    res = pl.pallas_call(
        body, name=name, grid=(t // tile,),
        in_specs=[pl.BlockSpec((tile, r.shape[1]), lambda i: (i, 0)) for r in rows]
        + [pl.BlockSpec(p.shape, lambda i: (0, 0)) for p in pars]
        + [pl.BlockSpec((tile, c.shape[1]), lambda i: (i, 0)) for c in cts],
        out_specs=[pl.BlockSpec((tile, r.shape[1]), lambda i: (i, 0)) for r in rows]
        + [pl.BlockSpec(p.shape, lambda i: (0, 0)) for p in pars],
        out_shape=[jax.ShapeDtypeStruct(r.shape, F32) for r in rows]
        + [jax.ShapeDtypeStruct(p.shape, F32) for p in pars],
        compiler_params=_params("arbitrary"),
    )(*rows, *pars, *cts)
    return tuple(res[:nr]), tuple(res[nr:])


def rowwise(name, f, rows, pars=()):
    tile = _row_tile(rows[0].shape[0])

    @jax.custom_vjp
    def op(rows, pars):
        return tuple(_rowwise_fwd_call(name, f, rows, pars, tile))

    def op_fwd(rows, pars):
        return op(rows, pars), (rows, pars)

    def op_bwd(res, cts):
        return _rowwise_bwd_call(name + "_bwd", f, res[0], res[1], tuple(cts), tile)

    op.defvjp(op_fwd, op_bwd)
    return op(tuple(rows), tuple(pars))


MM_TM = 512


def _tile_of(n, cap):
    best = None
    for c in range(LANES, min(n, cap) + 1, LANES):
        if n % c == 0:
            best = c
    return best or n


def _mmc_fwd(name, h, w):
    m, k = h.shape
    j, _, n = w.shape
    tm, tn = MM_TM, _tile_of(n, 1408)

    def body(h_ref, w_ref, o_ref):
        o_ref[0] = _dot(h_ref[...], w_ref[0])

    return pl.pallas_call(
        body, name=name, grid=(m // tm, j, n // tn),
        in_specs=[pl.BlockSpec((tm, k), lambda i, b, c: (i, 0)), pl.BlockSpec((1, k, tn), lambda i, b, c: (b, 0, c))],
        out_specs=pl.BlockSpec((1, tm, tn), lambda i, b, c: (b, i, c)),
        out_shape=jax.ShapeDtypeStruct((j, m, n), F32),
        compiler_params=_params("parallel", "parallel", "parallel"),
    )(h, w)


def _mmc_dh(name, dy, w):
    j, m, n = dy.shape
    k = w.shape[1]
    tm, tn = MM_TM, _tile_of(n, 1408)

    def body(dy_ref, w_ref, o_ref):
        part = _dot_nt(dy_ref[0], w_ref[0])

        @pl.when((pl.program_id(1) == 0) & (pl.program_id(2) == 0))
        def _():
            o_ref[...] = part

        @pl.when((pl.program_id(1) > 0) | (pl.program_id(2) > 0))
        def _():
            o_ref[...] += part

    return pl.pallas_call(
        body, name=name, grid=(m // tm, j, n // tn),
        in_specs=[pl.BlockSpec((1, tm, tn), lambda i, b, c: (b, i, c)), pl.BlockSpec((1, k, tn), lambda i, b, c: (b, 0, c))],
        out_specs=pl.BlockSpec((tm, k), lambda i, b, c: (i, 0)),
        out_shape=jax.ShapeDtypeStruct((m, k), F32),
        compiler_params=_params("parallel", "arbitrary", "arbitrary"),
    )(dy, w)


def _mmc_dw(name, h, dy):
    m, k = h.shape
    j, _, n = dy.shape
    tk, tn = _tile_of(k, 512), _tile_of(n, 1152)

    def body(h_ref, dy_ref, o_ref):
        o_ref[0] = _dot_tn(h_ref[...], dy_ref[0]).astype(BF16)

    return pl.pallas_call(
        body, name=name, grid=(j, k // tk, n // tn),
        in_specs=[pl.BlockSpec((m, tk), lambda b, i, c: (0, i)), pl.BlockSpec((1, m, tn), lambda b, i, c: (b, 0, c))],
        out_specs=pl.BlockSpec((1, tk, tn), lambda b, i, c: (b, i, c)),
        out_shape=jax.ShapeDtypeStruct((j, k, n), BF16),
        compiler_params=_params("parallel", "parallel", "parallel"),
    )(h, dy)


def mm_cols(name, h, w):
    @jax.custom_vjp
    def op(h, w):
        return _mmc_fwd(name, h, w)

    def op_fwd(h, w):
        return op(h, w), (h, w)

    def op_bwd(res, dy):
        h, w = res
        return _mmc_dh(name + "_dh", dy, w), _mmc_dw(name + "_dw", h, dy)

    op.defvjp(op_fwd, op_bwd)
    return op(h, w)


def _mmr_fwd(name, a, w):
    j, m, n = a.shape
    nn = w.shape[2]
    tm, tn = MM_TM, _tile_of(nn, 1024)

    def body(a_ref, w_ref, o_ref):
        part = _dot(a_ref[0], w_ref[0])

        @pl.when(pl.program_id(2) == 0)
        def _():
            o_ref[...] = part

        @pl.when(pl.program_id(2) > 0)
        def _():
            o_ref[...] += part

    return pl.pallas_call(
        body, name=name, grid=(m // tm, nn // tn, j),
        in_specs=[pl.BlockSpec((1, tm, n), lambda i, c, b: (b, i, 0)), pl.BlockSpec((1, n, tn), lambda i, c, b: (b, 0, c))],
        out_specs=pl.BlockSpec((tm, tn), lambda i, c, b: (i, c)),
        out_shape=jax.ShapeDtypeStruct((m, nn), F32),
        compiler_params=_params("parallel", "parallel", "arbitrary"),
    )(a, w)


def _mmr_da(name, dy, w):
    m, nn = dy.shape
    j, n, _ = w.shape
    tm = MM_TM

    def body(dy_ref, w_ref, o_ref):
        o_ref[0] = _dot_nt(dy_ref[...], w_ref[0])

    return pl.pallas_call(
        body, name=name, grid=(m // tm, j),
        in_specs=[pl.BlockSpec((tm, nn), lambda i, b: (i, 0)), pl.BlockSpec((1, n, nn), lambda i, b: (b, 0, 0))],
        out_specs=pl.BlockSpec((1, tm, n), lambda i, b: (b, i, 0)),
        out_shape=jax.ShapeDtypeStruct((j, m, n), F32),
        compiler_params=_params("parallel", "parallel"),
    )(dy, w)


def _mmr_dw(name, a, dy):
    j, m, n = a.shape
    nn = dy.shape[1]
    tn = _tile_of(nn, 512)

    def body(a_ref, dy_ref, o_ref):
        o_ref[0] = _dot_tn(a_ref[0], dy_ref[...]).astype(BF16)

    return pl.pallas_call(
        body, name=name, grid=(j, nn // tn),
        in_specs=[pl.BlockSpec((1, m, n), lambda b, c: (b, 0, 0)), pl.BlockSpec((m, tn), lambda b, c: (0, c))],
        out_specs=pl.BlockSpec((1, n, tn), lambda b, c: (b, 0, c)),
        out_shape=jax.ShapeDtypeStruct((j, n, nn), BF16),
        compiler_params=_params("parallel", "parallel"),
    )(a, dy)


def mm_rows(name, a, w):
    @jax.custom_vjp
    def op(a, w):
        return _mmr_fwd(name, a, w)

    def op_fwd(a, w):
        return op(a, w), (a, w)

    def op_bwd(res, dy):
        a, w = res
        return _mmr_da(name + "_da", dy, w), _mmr_dw(name + "_dw", a, dy)

    op.defvjp(op_fwd, op_bwd)
    return op(a, w)


def _colwise_specs(cols, pars, par_block):
    t = cols[0].shape[0]
    specs = [pl.BlockSpec((t, LANES), lambda j: (0, j)) for _ in cols]
    for p, blk in zip(pars, par_block):
        if blk == "lane":
            specs.append(pl.BlockSpec((p.shape[0], LANES), lambda j: (0, j)))
        else:
            specs.append(pl.BlockSpec((1,) + p.shape[1:], lambda j: (j, 0, 0)))
    return specs


def _colwise_fwd_call(name, f, cols, pars, par_block, n_out):
    t, c = cols[0].shape
    nc, npar = len(cols), len(pars)

    def body(*refs):
        ins = [r[...] for r in refs[:nc]] + [r[...] if b == "lane" else r[0] for r, b in zip(refs[nc:nc + npar], par_block)]
        res = f(*ins)
        for o_ref, o in zip(refs[nc + npar:], res):
            o_ref[...] = o

    return pl.pallas_call(
        body, name=name, grid=(c // LANES,),
        in_specs=_colwise_specs(cols, pars, par_block),
        out_specs=[pl.BlockSpec((t, LANES), lambda j: (0, j)) for _ in range(n_out)],
        out_shape=[jax.ShapeDtypeStruct((t, c), F32) for _ in range(n_out)],
        compiler_params=_params("parallel"),
    )(*cols, *pars)


def _colwise_bwd_call(name, f, cols, pars, par_block, cts):
    t, c = cols[0].shape
    nc, npar, nct = len(cols), len(pars), len(cts)

    def body(*refs):
        ins = [r[...] for r in refs[:nc]] + [r[...] if b == "lane" else r[0] for r, b in zip(refs[nc:nc + npar], par_block)]
        gs = tuple(r[...] for r in refs[nc + npar:nc + npar + nct])
        outs = refs[nc + npar + nct:]
        _, vjp = jax.vjp(f, *ins)
        d = vjp(gs)
        for o_ref, v in zip(outs[:nc], d[:nc]):
            o_ref[...] = v
        for o_ref, v, b in zip(outs[nc:], d[nc:], par_block):
            if b == "lane":
                o_ref[...] = v
            else:
                o_ref[0] = v

    res = pl.pallas_call(
        body, name=name, grid=(c // LANES,),
        in_specs=_colwise_specs(cols, pars, par_block) + [pl.BlockSpec((t, LANES), lambda j: (0, j)) for _ in cts],
        out_specs=_colwise_specs(cols, pars, par_block),
        out_shape=[jax.ShapeDtypeStruct(v.shape, F32) for v in (*cols, *pars)],
        compiler_params=_params("parallel"),
    )(*cols, *pars, *cts)
    return tuple(res[:nc]), tuple(res[nc:])


def colwise(name, f, cols, pars, par_block, n_out):
    @jax.custom_vjp
    def op(cols, pars):
        return tuple(_colwise_fwd_call(name, f, cols, pars, par_block, n_out))

    def op_fwd(cols, pars):
        return op(cols, pars), (cols, pars)

    def op_bwd(res, cts):
        return _colwise_bwd_call(name + "_bwd", f, res[0], res[1], par_block, tuple(cts))

    op.defvjp(op_fwd, op_bwd)
    return op(tuple(cols), tuple(pars))


def _rg_block(x, gate, cw, cb, wr, br, wi, bi, lam):
    xa = _causal_conv(x, cw) + cb
    r = _sigmoid(_dot(xa, wr) + br)
    i = _sigmoid(_dot(xa, wi) + bi)
    log_a = -RG_C * r * _softplus(-lam)
    a = jnp.exp(log_a)
    b = jnp.sqrt(_neg_expm1(2.0 * log_a)) * (i * xa)
    return (_lin_scan(a, b) * _gelu(gate),)


def _dn_conv_block(mode):
    def f(x, cw):
        c = _silu(_causal_conv(x, cw))
        if mode == "v":
            return (c,)
        c = c * lax.rsqrt(jnp.sum(c * c, axis=-1, keepdims=True) + EPS)
        return (c * (DN_HEAD_DIM ** -0.5),) if mode == "q" else (c,)
    return f


def _block_diag(w):
    w = w.reshape(8, 2, 64, 64)
    z = jnp.zeros((8, 64, 64), w.dtype)
    top = jnp.concatenate([w[:, 0], z], axis=2)
    bot = jnp.concatenate([z, w[:, 1]], axis=2)
    return jnp.concatenate([top, bot], axis=1)


DN_HP = 4


def _dn_step(S, q, k, v, gb, h):
    c = DN_CHUNK
    lane = lax.broadcasted_iota(jnp.int32, gb.shape, 1)
    beta = jnp.sum(jnp.where(lane == h, gb, 0.0), axis=1, keepdims=True)
    g = jnp.sum(jnp.where(lane == h + DN_HEADS, gb, 0.0), axis=1, keepdims=True)
    ri = lax.broadcasted_iota(jnp.int32, (c, c), 0)
    ci = lax.broadcasted_iota(jnp.int32, (c, c), 1)
    incl, strict = ri >= ci, ri > ci
    eye = (ri == ci).astype(F32)
    gam = _cumsum_rows(g)
    gam_row = jnp.sum(jnp.where(ri <= ci, g, 0.0), axis=0, keepdims=True)
    gam_last = jnp.sum(g, axis=0, keepdims=True)
    decay = jnp.where(incl, jnp.exp(jnp.where(incl, gam - gam_row, 0.0)), 0.0)
    kb = k * beta
    vb = v * beta
    a = jnp.where(strict, _dot_nt(kb, k) * decay, 0.0)
    p = -a
    tinv = eye + p
    for _ in range(5):
        p = _dot3(p, p)
        tinv = tinv + _dot3(tinv, p)
    e_gam = jnp.exp(gam)
    u0 = _dot3(tinv, vb)
    wk = _dot3(tinv, kb * e_gam)
    qk = jnp.where(incl, _dot_nt(q, k) * decay, 0.0)
    q_dec = q * e_gam
    k_dec = k * jnp.exp(gam_last - gam)
    u = u0 - _dot(wk, S)
    o = _dot(q_dec, S) + _dot(qk, u)
    s_new = S * jnp.exp(gam_last) + _dot_tn(k_dec, u)
    return s_new, o


def _dn_fwd_call(q, k, v, gb):
    t, w = q.shape
    n, hp, hd, c = t // DN_CHUNK, DN_HP, DN_HEAD_DIM, DN_CHUNK

    def body(q_ref, k_ref, v_ref, gb_ref, o_ref, s0_ref, s_scr):
        @pl.when(pl.program_id(1) == 0)
        def _():
            s_scr[...] = jnp.zeros_like(s_scr)

        gbv = gb_ref[...]
        for j in range(hp):
            sl = slice(j * hd, (j + 1) * hd)
            s_j = s_scr[j]
            s0_ref[j, 0] = s_j
            s_new, o = _dn_step(s_j, q_ref[:, sl], k_ref[:, sl], v_ref[:, sl], gbv, pl.program_id(0) * hp + j)
            o_ref[:, sl] = o
            s_scr[j] = s_new

    blk = pl.BlockSpec((c, hp * hd), lambda g, i: (i, g))
    return pl.pallas_call(
        body, name="dn_core", grid=(DN_HEADS // hp, n),
        in_specs=[blk, blk, blk, pl.BlockSpec((c, LANES), lambda g, i: (i, 0))],
        out_specs=[blk, pl.BlockSpec((hp, 1, hd, hd), lambda g, i: (g, i, 0, 0))],
        out_shape=[jax.ShapeDtypeStruct((t, w), F32), jax.ShapeDtypeStruct((DN_HEADS, n, hd, hd), F32)],
        scratch_shapes=[pltpu.VMEM((hp, hd, hd), F32)],
        compiler_params=_params("parallel", "arbitrary"),
    )(q, k, v, gb)


def _dn_bwd_call(q, k, v, gb, s0, do):
    t, w = q.shape
    n, hp, hd, c = t // DN_CHUNK, DN_HP, DN_HEAD_DIM, DN_CHUNK
    ng = DN_HEADS // hp

    def body(q_ref, k_ref, v_ref, gb_ref, s0_ref, do_ref, dq_ref, dk_ref, dv_ref, dgb_ref, ds_scr):
        @pl.when(pl.program_id(1) == 0)
        def _():
            ds_scr[...] = jnp.zeros_like(ds_scr)

        gbv = gb_ref[...]
        dgb = jnp.zeros_like(gbv)
        for j in range(hp):
            sl = slice(j * hd, (j + 1) * hd)
            h = pl.program_id(0) * hp + j
            _, vjp = jax.vjp(lambda s_, q_, k_, v_, gb_: _dn_step(s_, q_, k_, v_, gb_, h),
                             s0_ref[j, 0], q_ref[:, sl], k_ref[:, sl], v_ref[:, sl], gbv)
            ds, dq, dk, dv, dgb_j = vjp((ds_scr[j], do_ref[:, sl]))
            ds_scr[j] = ds
            dq_ref[:, sl] = dq
            dk_ref[:, sl] = dk
            dv_ref[:, sl] = dv
            dgb = dgb + dgb_j
        dgb_ref[0] = dgb

    blk = pl.BlockSpec((c, hp * hd), lambda g, i: (n - 1 - i, g))
    res = pl.pallas_call(
        body, name="dn_core_bwd", grid=(ng, n),
        in_specs=[blk, blk, blk, pl.BlockSpec((c, LANES), lambda g, i: (n - 1 - i, 0)),
                  pl.BlockSpec((hp, 1, hd, hd), lambda g, i: (g, n - 1 - i, 0, 0)), blk],
        out_specs=[blk, blk, blk, pl.BlockSpec((1, c, LANES), lambda g, i: (g, n - 1 - i, 0))],
        out_shape=[jax.ShapeDtypeStruct((t, w), F32)] * 3 + [jax.ShapeDtypeStruct((ng, t, LANES), F32)],
        scratch_shapes=[pltpu.VMEM((hp, hd, hd), F32)],
        compiler_params=_params("parallel", "arbitrary"),
    )(q, k, v, gb, s0, do)
    return res[0], res[1], res[2], jnp.sum(res[3], axis=0)


@jax.custom_vjp
def dn_core(q, k, v, gb):
    return _dn_fwd_call(q, k, v, gb)[0]


def _dn_core_fwd(q, k, v, gb):
    o, s0 = _dn_fwd_call(q, k, v, gb)
    return o, (q, k, v, gb, s0)


def _dn_core_bwd(res, do):
    return _dn_bwd_call(*res, do)


dn_core.defvjp(_dn_core_fwd, _dn_core_bwd)


def _att_block(q, kp, kc, vp, vc, qn, kn, slope, has_prev, dil):
    s = ATT_SPAN
    qh = _rms(q, qn) * (ATT_HEAD_DIM ** -0.5)
    qi = lax.broadcasted_iota(jnp.int32, (s, s), 0)
    kj = lax.broadcasted_iota(jnp.int32, (s, s), 1)
    d_p = qi + s - kj
    d_c = qi - kj
    s_p = _dot_nt(qh, _rms(kp, kn)) - slope * (d_p * dil).astype(F32)
    s_c = _dot_nt(qh, _rms(kc, kn)) - slope * (d_c * dil).astype(F32)
    s_p = jnp.where((d_p <= s) & (has_prev > 0), s_p, NEG_INF)
    s_c = jnp.where(d_c >= 0, s_c, NEG_INF)
    m = lax.stop_gradient(jnp.maximum(jnp.max(s_p, axis=1, keepdims=True), jnp.max(s_c, axis=1, keepdims=True)))
    p_p = jnp.exp(s_p - m)
    p_c = jnp.exp(s_c - m)
    den = jnp.sum(p_p, axis=1, keepdims=True) + jnp.sum(p_c, axis=1, keepdims=True)
    o = _dot(p_p / den, vp) + _dot(p_c / den, vc)
    lse = m + jnp.log(den)
    return o, jnp.broadcast_to(lse, o.shape)


def _att_specs(dil, nblk):
    cur = pl.BlockSpec((1, ATT_SPAN, ATT_HEAD_DIM), lambda i, n: (i, n, 0))
    prev = pl.BlockSpec((1, ATT_SPAN, ATT_HEAD_DIM), lambda i, n: (i, jnp.maximum(n - 1, 0), 0))
    gain = pl.BlockSpec((1, 1, ATT_HEAD_DIM), lambda i, n: (i // dil, 0, 0))
    return cur, prev, gain


def _att_slope(group, dil):
    head = (pl.program_id(0) // dil + (4 * group + 1)).astype(F32)
    return jnp.exp(jnp.full((1, 1), -8.0 / ATT_HEADS * math.log(2.0), F32) * head)


def _att_fwd_call(name, group, dil, q, k, v, qn, kn):
    r, l, e = q.shape
    nblk = l // ATT_SPAN
    cur, prev, gain = _att_specs(dil, nblk)

    def body(q_ref, kp_ref, kc_ref, vp_ref, vc_ref, qn_ref, kn_ref, o_ref, lse_ref):
        o, lse = _att_block(q_ref[0], kp_ref[0], kc_ref[0], vp_ref[0], vc_ref[0], qn_ref[0], kn_ref[0],
                            _att_slope(group, dil), pl.program_id(1), dil)
        o_ref[0] = o
        lse_ref[0] = lse

    return pl.pallas_call(
        body, name=name, grid=(r, nblk),
        in_specs=[cur, prev, cur, prev, cur, gain, gain], out_specs=[cur, cur],
        out_shape=[jax.ShapeDtypeStruct(q.shape, F32)] * 2,
        compiler_params=_params("parallel", "arbitrary"),
    )(q, k, k, v, v, qn, kn)


def _att_bwd_call(name, group, dil, q, k, v, qn, kn, do, dlse):
    r, l, e = q.shape
    nblk = l // ATT_SPAN
    cur, prev, gain = _att_specs(dil, nblk)

    def body(q_ref, kp_ref, kc_ref, vp_ref, vc_ref, qn_ref, kn_ref, do_ref, dlse_ref,
             dq_ref, dkp_ref, dkc_ref, dvp_ref, dvc_ref, dqn_ref, dkn_ref):
        slope, has_prev = _att_slope(group, dil), pl.program_id(1)
        _, vjp = jax.vjp(lambda *a: _att_block(*a, slope, has_prev, dil),
                         q_ref[0], kp_ref[0], kc_ref[0], vp_ref[0], vc_ref[0], qn_ref[0], kn_ref[0])
        dq, dkp, dkc, dvp, dvc, dqn, dkn = vjp((do_ref[0], dlse_ref[0]))
        dq_ref[0], dkp_ref[0], dkc_ref[0], dvp_ref[0], dvc_ref[0] = dq, dkp, dkc, dvp, dvc

        @pl.when((pl.program_id(0) % dil == 0) & (pl.program_id(1) == 0))
        def _():
            dqn_ref[...] = jnp.zeros_like(dqn_ref)
            dkn_ref[...] = jnp.zeros_like(dkn_ref)

        dqn_ref[0] += dqn
        dkn_ref[0] += dkn

    res = pl.pallas_call(
        body, name=name + "_bwd", grid=(r, nblk),
        in_specs=[cur, prev, cur, prev, cur, gain, gain, cur, cur],
        out_specs=[cur] * 5 + [gain, gain],
        out_shape=[jax.ShapeDtypeStruct(q.shape, F32)] * 5 + [jax.ShapeDtypeStruct(qn.shape, F32)] * 2,
        compiler_params=_params("arbitrary", "arbitrary"),
    )(q, k, k, v, v, qn, kn, do, dlse)
    dq, dkp, dkc, dvp, dvc, dqn, dkn = res
    back = lambda g: jnp.pad(g[:, ATT_SPAN:], ((0, 0), (0, ATT_SPAN), (0, 0)))
    return dq, dkc + back(dkp), dvc + back(dvp), dqn, dkn


def att_group(name, group, dil, q, k, v, qn, kn):
    @jax.custom_vjp
    def op(q, k, v, qn, kn):
        return tuple(_att_fwd_call(name, group, dil, q, k, v, qn, kn))

    def op_fwd(q, k, v, qn, kn):
        return op(q, k, v, qn, kn), (q, k, v, qn, kn)

    def op_bwd(res, cts):
        return _att_bwd_call(name, group, dil, *res, *cts)

    op.defvjp(op_fwd, op_bwd)
    return op(q, k, v, qn, kn)


def _att_mix(o1, o2, o3, l1, l2, l3):
    m = jnp.maximum(jnp.maximum(l1, l2), l3)
    e1, e2, e3 = jnp.exp(l1 - m), jnp.exp(l2 - m), jnp.exp(l3 - m)
    s = e1 + e2 + e3
    return (jnp.concatenate([o1 * (e1 / s), o2 * (e2 / s), o3 * (e3 / s)], axis=1),)


def att_branch(name, pa, qn, kn, groups=ATT_GROUPS):
    t = pa.shape[0]
    hg, e = 4, ATT_HEAD_DIM
    q, k, v = (pa[:, i * 768:(i + 1) * 768].reshape(t, ATT_HEADS, e) for i in range(3))
    outs, lses = [], []
    for g, (window, dil) in enumerate(groups):
        assert window // dil == ATT_SPAN
        l = t // dil
        to_r = lambda a: a[:, hg * g:hg * (g + 1)].reshape(l, dil, hg, e).transpose(2, 1, 0, 3).reshape(hg * dil, l, e)
        back = lambda a: a.reshape(hg, dil, l, e).transpose(2, 1, 0, 3).reshape(t, hg * e)
        o, lse = att_group(f"{name}_att{g}", g, dil, to_r(q), to_r(k), to_r(v),
                           qn[hg * g:hg * (g + 1)].reshape(hg, 1, e), kn[hg * g:hg * (g + 1)].reshape(hg, 1, e))
        outs.append(back(o))
        lses.append(back(lse))
    return rowwise(f"{name}_attmix", _att_mix, outs + lses)[0]


def dn_gates(name, ba, a_log, dt_bias):
    place = lambda p: jnp.pad(p.reshape(1, DN_HEADS), ((0, 0), (DN_HEADS, LANES - 2 * DN_HEADS)))

    def f(x, al, dt):
        lane = lax.broadcasted_iota(jnp.int32, x.shape, 1)
        return (jnp.where(lane < DN_HEADS, _sigmoid(x), -jnp.exp(al) * _softplus(x + dt)),)

    return rowwise(name, f, (ba,), (place(a_log), place(dt_bias)))[0]


def _dn_out(o, z, g):
    parts = []
    for h in range(DN_HEADS):
        sl = slice(h * DN_HEAD_DIM, (h + 1) * DN_HEAD_DIM)
        parts.append(_rms(o[:, sl], g[:, sl]) * _silu(z[:, sl]))
    return (jnp.concatenate(parts, axis=1),)


def _merge(ml, za, zb, zc):
    d = D_MODEL
    return (_sigmoid(ml[:, :d]) * za + _sigmoid(ml[:, d:2 * d]) * zb + _sigmoid(ml[:, 2 * d:]) * zc,)


def _swiglu_act(g, u):
    return (_silu(g) * u,)


def add_norm(name, x, pend, scale, gain):
    if pend is None:
        return x, rowwise(name, lambda a, g: (_rms(a, g),), (x,), (gain,))[0]

    def f(a, b, g):
        s = a + scale * b
        return s, _rms(s, g)

    return rowwise(name, f, (x, pend), (gain,))


def ffn(name, h, wg, wu, wd):
    g, u = mm_cols(name + "_g", h, wg), mm_cols(name + "_u", h, wu)
    j, t, n = g.shape
    a = rowwise(name + "_act", _swiglu_act, (g.reshape(j * t, n), u.reshape(j * t, n)))[0]
    return mm_rows(name + "_d", a.reshape(j, t, n), wd)


W_IN_PIECES = (("rgx", 0, 1024), ("gate", 1024, 1024), ("att", 2048, 2304), ("dq", 4352, 1024), ("dk", 5376, 1024),
               ("dv", 6400, 1024), ("dz", 7424, 1024), ("ba", 8448, 16), ("mrg", 8464, 3072))
RG_PAR_BLOCKS = ("lane", "lane", "blk", "lane", "blk", "lane", "lane")


def mixer(name, u, w, p):
    mm = lambda nm, a, wt: mm_rows(nm, a[None], wt[None])
    pr = {k: mm_cols(f"{name}_in_{k}", u, w["in_" + k][None])[0] for k, _, _ in W_IN_PIECES}
    ya = colwise(name + "_rg", _rg_block, (pr["rgx"], pr["gate"]),
                 (w["rg_conv_w"], p["rg_conv_b"], _block_diag(p["rg_w_r"]), p["rg_b_r"], _block_diag(p["rg_w_i"]),
                  p["rg_b_i"], p["rg_lambda"]), RG_PAR_BLOCKS, 1)[0]
    yb = att_branch(name, pr["att"], p["att_q_norm"], p["att_k_norm"])
    cw = w["dn_conv_w"]
    cq = colwise(name + "_dnq", _dn_conv_block("q"), (pr["dq"],), (cw[:, :1024],), ("lane",), 1)[0]
    ck = colwise(name + "_dnk", _dn_conv_block("k"), (pr["dk"],), (cw[:, 1024:2048],), ("lane",), 1)[0]
    cv = colwise(name + "_dnv", _dn_conv_block("v"), (pr["dv"],), (cw[:, 2048:],), ("lane",), 1)[0]
    gb = dn_gates(name + "_dngate", pr["ba"], p["dn_a_log"], p["dn_dt_bias"])
    o_dn = dn_core(cq, ck, cv, gb)
    yc = rowwise(name + "_dnout", _dn_out, (o_dn, pr["dz"]), (p["dn_out_norm"].reshape(1, D_MODEL),))[0]
    y = rowwise(name + "_merge", _merge, (pr["mrg"], mm(name + "_ba", ya, w["br_a"]), mm(name + "_bb", yb, w["br_b"]),
                                          mm(name + "_bc", yc, w["br_c"])))[0]
    return mm(name + "_out", y, w["w_out"])


def _loss_call(x, pend, target):
    t, d = x.shape
    tile = _row_tile(t)

    def body(x_ref, p_ref, t_ref, loss_ref, g_ref):
        err = x_ref[...] + 0.5 * p_ref[...] - t_ref[...]
        g_ref[...] = err * (1.0 / d)

        @pl.when(pl.program_id(0) == 0)
        def _():
            loss_ref[...] = jnp.zeros_like(loss_ref)

        loss_ref[...] += jnp.full(loss_ref.shape, 0.5 / d, F32) * jnp.sum(err * err)

    blk = pl.BlockSpec((tile, d), lambda i: (i, 0))
    loss, g = pl.pallas_call(
        body, name="loss", grid=(t // tile,), in_specs=[blk, blk, blk],
        out_specs=[pl.BlockSpec((8, LANES), lambda i: (0, 0)), blk],
        out_shape=[jax.ShapeDtypeStruct((8, LANES), F32), jax.ShapeDtypeStruct((t, d), F32)],
        compiler_params=_params("arbitrary"),
    )(x, pend, target)
    return loss[0, 0], g


@jax.custom_vjp
def loss_op(x, pend, target):
    return _loss_call(x, pend, target)[0]


def _loss_fwd(x, pend, target):
    loss, g = _loss_call(x, pend, target)
    return loss, g


def _loss_bwd(g, ct):
    return ct * g, (0.5 * ct) * g, None


loss_op.defvjp(_loss_fwd, _loss_bwd)


def local_loss(w, p, x, target):
    pend, scale = None, 0.0
    for l in range(len(w)):
        n = f"L{l}"
        x, h = add_norm(n + "_n1", x, pend, scale, p[l]["ffn1_norm"])
        pend, scale = ffn(n + "_f1", h, w[l]["ffn1_w_gate"], w[l]["ffn1_w_up"], w[l]["ffn1_w_down"]), 0.5
        x, h = add_norm(n + "_nm", x, pend, scale, p[l]["mix_norm"])
        pend, scale = mixer(n + "_mx", h, w[l], p[l]), 1.0
        x, h = add_norm(n + "_n2", x, pend, scale, p[l]["ffn2_norm"])
        pend, scale = ffn(n + "_f2", h, w[l]["ffn2_w_gate"], w[l]["ffn2_w_up"], w[l]["ffn2_w_down"]), 0.5
    return loss_op(x, pend, target)


WEIGHT_NAMES = ("ffn1_norm", "ffn1_w_gate", "ffn1_w_up", "ffn1_w_down", "mix_norm", "w_in", "rg_conv_w", "rg_conv_b",
                "rg_w_r", "rg_b_r", "rg_w_i", "rg_b_i", "rg_lambda", "att_q_norm", "att_k_norm", "dn_conv_w", "dn_a_log",
                "dn_dt_bias", "dn_out_norm", "w_branch", "w_out", "ffn2_norm", "ffn2_w_gate", "ffn2_w_up", "ffn2_w_down")
MATRICES = (("ffn1_w_gate", 2), ("ffn1_w_up", 2), ("ffn1_w_down", 1), ("w_in", 2), ("w_branch", 1), ("w_out", 1),
            ("ffn2_w_gate", 2), ("ffn2_w_up", 2), ("ffn2_w_down", 1))
CONVS = (("rg_conv_w", 2), ("dn_conv_w", 2))
SHARD_AXIS = dict(MATRICES + CONVS)
SMALL_NAMES = tuple(n for n in WEIGHT_NAMES if n not in SHARD_AXIS)
ROW_PARAMS = ("ffn1_norm", "mix_norm", "rg_conv_b", "rg_b_r", "rg_b_i", "rg_lambda", "ffn2_norm")
FFN_MATS = ("ffn1_w_gate", "ffn1_w_up", "ffn1_w_down", "ffn2_w_gate", "ffn2_w_up", "ffn2_w_down")
W_IN_SHARD = 2884


def _shard_minor(a, axis):
    a = jnp.moveaxis(a, 0, axis)
    return a.reshape(a.shape[:axis] + (N_CHIPS * a.shape[axis + 1],) + a.shape[axis + 2:])


def _w_in_piece(g, l, off, n):
    s = W_IN_SHARD
    parts = [g[j, l][:, max(off, j * s) - j * s:min(off + n, (j + 1) * s) - j * s]
             for j in range(N_CHIPS) if max(off, j * s) < min(off + n, (j + 1) * s)]
    return jnp.concatenate(parts, axis=1) if len(parts) > 1 else parts[0]


def _w_in_chip_grad(gl, j):
    s = W_IN_SHARD
    parts = [gl["in_" + k][:, max(off, j * s) - off:min(off + n, (j + 1) * s) - off]
             for k, off, n in W_IN_PIECES if max(off, j * s) < min(off + n, (j + 1) * s)]
    return jnp.concatenate(parts, axis=1)


def layer_weights(g, conv, l):
    w = {n: g[n][:, l] for n in FFN_MATS}
    w["w_out"] = g["w_out"][:, l].reshape(D_MODEL, D_MODEL)
    for k, off, n in W_IN_PIECES:
        piece = _w_in_piece(g["w_in"], l, off, n)
        w["in_" + k] = jnp.pad(piece, ((0, 0), (0, LANES - n))) if n < LANES else piece
    wb = g["w_branch"][:, l].reshape(-1, D_MODEL)
    w["br_a"], w["br_b"], w["br_c"] = wb[:1024], wb[1024:1792], wb[1792:]
    w["rg_conv_w"], w["dn_conv_w"] = conv["rg_conv_w"][l], conv["dn_conv_w"][l]
    return w


def layer_weight_grads(gw):
    stack = lambda f: jnp.stack([f(gl) for gl in gw], axis=1)
    out = {n: stack(lambda gl: gl[n]) for n in FFN_MATS}
    out["w_out"] = stack(lambda gl: gl["w_out"].reshape(N_CHIPS, -1, D_MODEL))
    out["w_branch"] = stack(lambda gl: jnp.concatenate([gl["br_a"], gl["br_b"], gl["br_c"]], axis=0).reshape(N_CHIPS, -1, D_MODEL))
    out["w_in"] = stack(lambda gl: jnp.stack([_w_in_chip_grad(gl, j) for j in range(N_CHIPS)]))
    conv = {n: jnp.stack([gl[n] for gl in gw]) for n, _ in CONVS}
    return out, conv


def layer_small(small, l):
    p = {n: small[n][l] for n in SMALL_NAMES}
    for n in ROW_PARAMS:
        p[n] = small[n][l:l + 1]
    return p


def layer_small_grads(gp, small):
    return {n: jnp.stack([g[n] for g in gp]).reshape(small[n].shape) for n in SMALL_NAMES}


HBM_SPEC = pl.BlockSpec(memory_space=pl.ANY)


def _place():
    x, y, c = lax.axis_index("x"), lax.axis_index("y"), lax.axis_index("c")
    other_chips = [(1 - x, y), (x, 1 - y), (1 - x, 1 - y)]
    return x, y, c, 2 * x + y, (x, y, 1 - c), other_chips


def _half_rows(ref, lead, hc):
    hr = ref.shape[-2] // 2
    return ref.at[(*lead, slice(None), pl.ds(pl.multiple_of(hc * hr, 16), hr), slice(None))]


def _chip_index():
    return (2 * lax.axis_index("x") + lax.axis_index("y")).astype(jnp.int32).reshape(1)


def cast_into_block(name, w):
    l, rows, cols = w.shape
    tr = rows // 2

    def body(me_ref, w_ref, o_ref):
        o_ref[0] = w_ref[...].astype(BF16)

    return pl.pallas_call(
        body, name=name, out_shape=jax.ShapeDtypeStruct((N_CHIPS, l, rows, cols), BF16),
        grid_spec=pltpu.PrefetchScalarGridSpec(
            num_scalar_prefetch=1, grid=(l, rows // tr),
            in_specs=[pl.BlockSpec((1, tr, cols), lambda a, i, me: (a, i, 0))],
            out_specs=pl.BlockSpec((1, 1, tr, cols), lambda a, i, me: (me[0], a, i, 0))),
        compiler_params=_params("parallel", "parallel"),
    )(_chip_index(), w)


def allgather_mats(bufs):
    n = len(bufs)

    def body(*refs):
        ins, outs = refs[:n], refs[n:2 * n]
        send_sems, recv_sems = refs[2 * n:]
        x, y, c, me, sibling, chips = _place()

        def copy(s, src, dst, to):
            return pltpu.make_async_remote_copy(src_ref=src, dst_ref=dst, send_sem=send_sems.at[s], recv_sem=recv_sems.at[s],
                                                device_id=to, device_id_type=MESH)

        first, passed = [], []
        for j, (cx, cy) in enumerate(chips):
            for i in range(n):
                cp = copy(6 * i + j, _half_rows(ins[i], (me,), c), _half_rows(outs[i], (me,), c), (cx, cy, c))
                cp.start()
                first.append(cp)
        for j, (cx, cy) in enumerate(chips):
            k = 2 * cx + cy
            for i in range(n):
                copy(6 * i + j, _half_rows(ins[i], (me,), c), _half_rows(outs[i], (k,), c), (cx, cy, c)).wait_recv()
                cp = copy(6 * i + 3 + j, _half_rows(outs[i], (k,), c), _half_rows(outs[i], (k,), c), sibling)
                cp.start()
                passed.append(cp)
        for j, (cx, cy) in enumerate(chips):
            k = 2 * cx + cy
            for i in range(n):
                copy(6 * i + 3 + j, _half_rows(ins[i], (me,), c), _half_rows(outs[i], (k,), 1 - c), sibling).wait_recv()
        for cp in first + passed:
            cp.wait_send()

    return pl.pallas_call(
        body, name="allgather_mats", out_shape=[jax.ShapeDtypeStruct(b.shape, b.dtype) for b in bufs],
        in_specs=[HBM_SPEC] * n, out_specs=[HBM_SPEC] * n, input_output_aliases={i: i for i in range(n)},
        scratch_shapes=[pltpu.SemaphoreType.DMA((6 * n,)), pltpu.SemaphoreType.DMA((6 * n,))],
    )(*bufs)


PEER_FLIPS = tuple((fx, fy, fc) for fx in (0, 1) for fy in (0, 1) for fc in (0, 1))[1:]


def exchange_grad_pieces(gs):
    n = len(gs)

    def body(*refs):
        ins, outs = refs[:n], refs[n:2 * n]
        send_sems, recv_sems, local_sems = refs[2 * n:]
        x, y, c, me, sibling, chips = _place()
        my_dev = 4 * x + 2 * y + c
        flip = lambda v, f: 1 - v if f else v
        local, sends = [], []
        for i in range(n):
            cp = pltpu.make_async_copy(_half_rows(ins[i], (me,), c), outs[i].at[my_dev], local_sems.at[i])
            cp.start()
            local.append(cp)
        for r, (fx, fy, fc) in enumerate(PEER_FLIPS):
            px, py, pc = flip(x, fx), flip(y, fy), flip(c, fc)
            for i in range(n):
                cp = pltpu.make_async_remote_copy(
                    src_ref=_half_rows(ins[i], (2 * px + py,), pc), dst_ref=outs[i].at[my_dev], send_sem=send_sems.at[7 * i + r],
                    recv_sem=recv_sems.at[7 * i + r], device_id=(px, py, pc), device_id_type=MESH)
                cp.start()
                sends.append(cp)
        for r, (fx, fy, fc) in enumerate(PEER_FLIPS):
            px, py, pc = flip(x, fx), flip(y, fy), flip(c, fc)
            for i in range(n):
                pltpu.make_async_remote_copy(
                    src_ref=_half_rows(ins[i], (me,), c), dst_ref=outs[i].at[4 * px + 2 * py + pc], send_sem=send_sems.at[7 * i + r],
                    recv_sem=recv_sems.at[7 * i + r], device_id=(px, py, pc), device_id_type=MESH).wait_recv()
        for cp in sends:
            cp.wait_send()
        for cp in local:
            cp.wait()

    return pl.pallas_call(
        body, name="exchange_grad_pieces",
        out_shape=[jax.ShapeDtypeStruct((N_DEV, g.shape[1], g.shape[2] // 2, g.shape[3]), g.dtype) for g in gs],
        in_specs=[HBM_SPEC] * n, out_specs=[HBM_SPEC] * n,
        scratch_shapes=[pltpu.SemaphoreType.DMA((7 * n,)), pltpu.SemaphoreType.DMA((7 * n,)), pltpu.SemaphoreType.DMA((n,))],
    )(*gs)


def sibling_share_halves(fs):
    n = len(fs)

    def body(*refs):
        ins, outs = refs[:n], refs[n:2 * n]
        send_sems, recv_sems = refs[2 * n:]
        x, y, c, me, sibling, chips = _place()
        sends = []
        for i in range(n):
            cp = pltpu.make_async_remote_copy(src_ref=_half_rows(ins[i], (), c), dst_ref=_half_rows(outs[i], (), c),
                                              send_sem=send_sems.at[i], recv_sem=recv_sems.at[i], device_id=sibling, device_id_type=MESH)
            cp.start()
            sends.append(cp)
        for i in range(n):
            pltpu.make_async_remote_copy(src_ref=_half_rows(ins[i], (), c), dst_ref=_half_rows(outs[i], (), 1 - c),
                                         send_sem=send_sems.at[i], recv_sem=recv_sems.at[i], device_id=sibling,
                                         device_id_type=MESH).wait_recv()
        for cp in sends:
            cp.wait_send()

    return pl.pallas_call(
        body, name="sibling_share_halves", out_shape=[jax.ShapeDtypeStruct(f.shape, f.dtype) for f in fs],
        in_specs=[HBM_SPEC] * n, out_specs=[HBM_SPEC] * n, input_output_aliases={i: i for i in range(n)},
        scratch_shapes=[pltpu.SemaphoreType.DMA((n,)), pltpu.SemaphoreType.DMA((n,))],
    )(*fs)


def allgather_small(name, v):
    m_per, n = v.shape

    def body(x_ref, out_ref, send_sems, recv_sems, local_sem):
        x, y, c, _, sibling, chips = _place()
        me = (x, y, c)

        def rows(px, py, pc):
            return out_ref.at[pl.ds((4 * px + 2 * py + pc) * m_per, m_per), :]

        def copy(k, block, to, src=None):
            return pltpu.make_async_remote_copy(src_ref=rows(*block) if src is None else src, dst_ref=rows(*block),
                                                send_sem=send_sems.at[k], recv_sem=recv_sems.at[k], device_id=to, device_id_type=MESH)

        mine = pltpu.make_async_copy(x_ref, rows(*me), local_sem)
        mine.start()
        first = [copy(0, me, sibling, src=x_ref)]
        first += [copy(1 + j, me, (*chip, c), src=x_ref) for j, chip in enumerate(chips)]
        for cp in first:
            cp.start()
        passed = [copy(4 + j, (*chip, c), sibling) for j, chip in enumerate(chips)]
        for j, chip in enumerate(chips):
            copy(1 + j, (*chip, c), me).wait_recv()
            passed[j].start()
        copy(0, sibling, me).wait_recv()
        for j, chip in enumerate(chips):
            copy(4 + j, (*chip, 1 - c), me).wait_recv()
        for cp in first + passed:
            cp.wait_send()
        mine.wait()

    return pl.pallas_call(
        body, name=name, out_shape=jax.ShapeDtypeStruct((N_DEV * m_per, n), v.dtype),
        in_specs=[pl.BlockSpec(memory_space=pltpu.VMEM)], out_specs=pl.BlockSpec(memory_space=pltpu.VMEM),
        scratch_shapes=[pltpu.SemaphoreType.DMA((7,)), pltpu.SemaphoreType.DMA((7,)), pltpu.SemaphoreType.DMA],
        compiler_params=pltpu.CompilerParams(vmem_limit_bytes=VMEM_LIMIT),
    )(v)


SUM_BLOCK_ELEMS = 512 * 1024


def sum_slabs(name, b, into_half=False):
    k, l, h, w = b.shape
    tile = max(t for t in range(8, h + 1, 8) if h % t == 0 and (t % 16 == 0 or t == h) and (t * w <= SUM_BLOCK_ELEMS or t <= 16))
    nt = h // tile
    c = lax.axis_index("c").astype(jnp.int32).reshape(1) if into_half else jnp.zeros((1,), jnp.int32)

    def body(c_ref, b_ref, o_ref):
        acc = b_ref[0, 0].astype(F32)
        for i in range(1, k):
            acc = acc + b_ref[i, 0].astype(F32)
        o_ref[0] = acc

    return pl.pallas_call(
        body, name=name, out_shape=jax.ShapeDtypeStruct((l, 2 * h if into_half else h, w), F32),
        grid_spec=pltpu.PrefetchScalarGridSpec(
            num_scalar_prefetch=1, grid=(l, nt),
            in_specs=[pl.BlockSpec((k, 1, tile, w), lambda a, i, c_ref: (0, a, i, 0))],
            out_specs=pl.BlockSpec((1, tile, w), lambda a, i, c_ref: (a, c_ref[0] * nt + i, 0))),
        compiler_params=_params("parallel", "parallel"),
    )(c, b)


def _adam_block(w, g, m, v):
    m = ADAM_B1 * m + (1.0 - ADAM_B1) * g
    v = ADAM_B2 * v + (1.0 - ADAM_B2) * (g * g)
    m_hat = m / (1.0 - ADAM_B1 ** ADAM_STEP)
    v_hat = v / (1.0 - ADAM_B2 ** ADAM_STEP)
    return -ADAM_LR * (m_hat / (jnp.sqrt(v_hat) + ADAM_EPS) + ADAM_WD * w), m, v


def adamw(name, w, g, m, v):
    shape = w.shape
    cols = shape[-1]
    rows = w.size // cols
    tile = 128 if rows % 128 == 0 else rows
    flat = [a.reshape(rows, cols) for a in (w, g, m, v)]

    def body(w_ref, g_ref, m_ref, v_ref, d_ref, nm_ref, nv_ref):
        d_ref[...], nm_ref[...], nv_ref[...] = _adam_block(w_ref[...], g_ref[...], m_ref[...], v_ref[...])

    blk = pl.BlockSpec((tile, cols), lambda i: (i, 0))
    res = pl.pallas_call(
        body, name=name, grid=(rows // tile,), in_specs=[blk] * 4, out_specs=[blk] * 3,
        out_shape=[jax.ShapeDtypeStruct((rows, cols), F32)] * 3, compiler_params=_params("parallel"),
    )(*flat)
    return tuple(r.reshape(shape) for r in res)


def _pack_small(grads):
    flat = jnp.concatenate([grads[n].reshape(-1) for n in SMALL_NAMES + tuple(n for n, _ in CONVS)])
    n = flat.shape[0]
    total = -(-n // (8 * LANES)) * (8 * LANES)
    return jnp.pad(flat, (0, total - n)).reshape(-1, LANES)


def _unpack_small(v, shapes):
    flat = v.reshape(-1)
    out, off = {}, 0
    for n in SMALL_NAMES + tuple(n for n, _ in CONVS):
        sz = int(np.prod(shapes[n]))
        out[n] = flat[off:off + sz].reshape(shapes[n])
        off += sz
    return out


def kernel(x, ffn1_norm, ffn1_w_gate, ffn1_w_up, ffn1_w_down, mix_norm, w_in, rg_conv_w, rg_conv_b, rg_w_r, rg_b_r, rg_w_i, rg_b_i, rg_lambda, att_q_norm, att_k_norm, dn_conv_w, dn_a_log, dn_dt_bias, dn_out_norm, w_branch, w_out, ffn2_norm, ffn2_w_gate, ffn2_w_up, ffn2_w_down, loss_target, m_ffn1_norm, m_ffn1_w_gate, m_ffn1_w_up, m_ffn1_w_down, m_mix_norm, m_w_in, m_rg_conv_w, m_rg_conv_b, m_rg_w_r, m_rg_b_r, m_rg_w_i, m_rg_b_i, m_rg_lambda, m_att_q_norm, m_att_k_norm, m_dn_conv_w, m_dn_a_log, m_dn_dt_bias, m_dn_out_norm, m_w_branch, m_w_out, m_ffn2_norm, m_ffn2_w_gate, m_ffn2_w_up, m_ffn2_w_down, v_ffn1_norm, v_ffn1_w_gate, v_ffn1_w_up, v_ffn1_w_down, v_mix_norm, v_w_in, v_rg_conv_w, v_rg_conv_b, v_rg_w_r, v_rg_b_r, v_rg_w_i, v_rg_b_i, v_rg_lambda, v_att_q_norm, v_att_k_norm, v_dn_conv_w, v_dn_a_log, v_dn_dt_bias, v_dn_out_norm, v_w_branch, v_w_out, v_ffn2_norm, v_ffn2_w_gate, v_ffn2_w_up, v_ffn2_w_down):
    given = dict(locals())
    small = {n: given[n] for n in SMALL_NAMES}
    n_layers = ffn1_norm.shape[0]
    mat_names = [n for n, _ in MATRICES]
    conv_names = [n for n, _ in CONVS]

    gathered = dict(zip(mat_names, allgather_mats([cast_into_block("cast_" + n, given[n]) for n in mat_names])))
    taps = jnp.concatenate([given[n].reshape(-1) for n in conv_names]).reshape(-1, LANES)
    taps = allgather_small("allgather_taps", taps).reshape(N_CHIPS, 2, -1)[:, 0]
    conv, off = {}, 0
    for n, ax in CONVS:
        sz = given[n].size
        conv[n] = _shard_minor(taps[:, off:off + sz].reshape((N_CHIPS,) + given[n].shape), ax)
        off += sz
    w = [layer_weights(gathered, conv, l) for l in range(n_layers)]
    p = [layer_small(small, l) for l in range(n_layers)]

    loss, (gw, gp, gx) = jax.value_and_grad(local_loss, argnums=(0, 1, 2))(w, p, x[0], loss_target[0])
    loss = lax.psum(loss, ("x", "y", "c"))
    g_mats, g_conv = layer_weight_grads(gw)

    pieces = exchange_grad_pieces([g_mats[n] for n in mat_names])
    halves = [sum_slabs("sum_" + n, b, into_half=True) for n, b in zip(mat_names, pieces)]
    grads = dict(zip(mat_names, sibling_share_halves(halves)))

    g_small = dict(layer_small_grads(gp, small), **g_conv)
    packed_small = _pack_small(g_small)
    slabs = allgather_small("allgather_small", packed_small).reshape(N_DEV, 1, packed_small.shape[0], LANES)
    summed = _unpack_small(sum_slabs("sum_small", slabs)[0], {n: g.shape for n, g in g_small.items()})
    chip = 2 * lax.axis_index("x") + lax.axis_index("y")
    for n in SMALL_NAMES:
        grads[n] = summed[n]
    for n, ax in CONVS:
        s = given[n].shape[ax]
        grads[n] = lax.dynamic_slice_in_dim(summed[n], chip * s, s, axis=ax)

    upd = {n: adamw("adamw_" + n, given[n], grads[n], given["m_" + n], given["v_" + n]) for n in WEIGHT_NAMES}
    return (loss, gx[None], *[grads[n] for n in WEIGHT_NAMES], *[upd[n][0] for n in WEIGHT_NAMES],
            *[upd[n][1] for n in WEIGHT_NAMES], *[upd[n][2] for n in WEIGHT_NAMES])
```

```python
import functools
import math

import jax
import jax.numpy as jnp
import numpy as np
from jax import lax
from jax.experimental import pallas as pl
from jax.experimental.pallas import tpu as pltpu

F32 = jnp.float32
BF16 = jnp.bfloat16
MESH = pl.DeviceIdType.MESH

D_MODEL = 1024
FFN_DIM = 2816
RG_C = 8.0
ATT_GROUPS = ((128, 1), (512, 4), (2048, 16))
ATT_HEADS = 12
ATT_HEAD_DIM = 64
ATT_SPAN = 128
DN_HEADS = 8
DN_HEAD_DIM = 128
DN_CHUNK = 64
EPS = 1e-6
NEG_INF = -1e30
N_CHIPS = 4
N_DEV = 8

ADAM_LR, ADAM_B1, ADAM_B2, ADAM_EPS, ADAM_WD, ADAM_STEP = 0.001, 0.9, 0.999, 1e-08, 0.01, 10

LANES = 128
VMEM_LIMIT = 56 * 1024 * 1024


def _params(*sem):
    return pltpu.CompilerParams(dimension_semantics=sem or None, vmem_limit_bytes=VMEM_LIMIT)


def _sigmoid(x):
    return 1.0 / (1.0 + jnp.exp(-x))


def _silu(x):
    return x * _sigmoid(x)


def _softplus(x):
    return jnp.maximum(x, 0.0) + jnp.log(1.0 + jnp.exp(-jnp.abs(x)))


def _gelu(x):
    return 0.5 * x * (1.0 + jnp.tanh(math.sqrt(2.0 / math.pi) * (x + 0.044715 * (x * x * x))))


def _neg_expm1(x):
    series = -x * (1.0 + x * (0.5 + x * (1.0 / 6 + x * (1.0 / 24 + x * (1.0 / 120 + x * (1.0 / 720))))))
    return jnp.where(x > -0.25, series, 1.0 - jnp.exp(x))


def _rms(x, g):
    return x * lax.rsqrt(jnp.mean(x * x, axis=-1, keepdims=True) + EPS) * g


_MM_DIMS = {"nn": (((1,), (0,)), ((), ())), "nt": (((1,), (1,)), ((), ())), "tn": (((0,), (0,)), ((), ()))}


def _split(a):
    hi = a.astype(BF16)
    return hi, (a - hi.astype(F32)).astype(BF16)


def _mxu(a, b, form, passes):
    (ca, cb), _ = _MM_DIMS[form]
    if a.ndim == 3:
        dims = (((ca[0] + 1,), (cb[0] + 1,)), ((0,), (0,)))
    else:
        dims = _MM_DIMS[form]
    dg = lambda p, q: lax.dot_general(p, q, dims, preferred_element_type=F32)
    if passes == 1:
        return dg(a.astype(BF16), b.astype(BF16))
    (a_hi, a_lo), (b_hi, b_lo) = _split(a), _split(b)
    return dg(a_hi, b_hi) + (dg(a_hi, b_lo) + dg(a_lo, b_hi))


@functools.partial(jax.custom_vjp, nondiff_argnums=(2, 3))
def _mm(a, b, form, passes):
    return _mxu(a, b, form, passes)


def _mm_fwd(a, b, form, passes):
    return _mxu(a, b, form, passes), (a, b)


def _mm_bwd(form, passes, res, g):
    a, b = res
    if form == "nn":
        return _mm(g, b, "nt", passes), _mm(a, g, "tn", passes)
    if form == "nt":
        return _mm(g, b, "nn", passes), _mm(g, a, "tn", passes)
    return _mm(b, g, "nt", passes), _mm(a, g, "nn", passes)


_mm.defvjp(_mm_fwd, _mm_bwd)


def _dot(a, b):
    return _mm(a, b, "nn", 1)


def _dot_nt(a, b):
    return _mm(a, b, "nt", 1)


def _dot_tn(a, b):
    return _mm(a, b, "tn", 1)


def _dot3(a, b):
    return _mm(a, b, "nn", 3)


def _rows(shape):
    return lax.broadcasted_iota(jnp.int32, shape, len(shape) - 2)


def _roll_down(x, s, fill):
    return jnp.where(_rows(x.shape) >= s, pltpu.roll(x, s, x.ndim - 2), fill)


def _roll_up(x, s, fill):
    n = x.shape[-2]
    return jnp.where(_rows(x.shape) < n - s, pltpu.roll(x, n - s, x.ndim - 2), fill)


@functools.partial(jax.custom_vjp, nondiff_argnums=(1,))
def _shift(x, s):
    return _roll_down(x, s, 0.0)


def _shift_fwd(x, s):
    return _roll_down(x, s, 0.0), None


def _shift_bwd(s, _, g):
    return (_roll_up(g, s, 0.0),)


_shift.defvjp(_shift_fwd, _shift_bwd)


def _causal_conv(x, w):
    return w[0:1] * _shift(x, 3) + w[1:2] * _shift(x, 2) + w[2:3] * _shift(x, 1) + w[3:4] * x


@jax.custom_vjp
def _lin_scan(a, b):
    return _lin_scan_fwd(a, b)[0]


def _lin_scan_fwd(a, b):
    a0 = a
    s = 1
    while s < a.shape[0]:
        b = a * _roll_down(b, s, 0.0) + b
        a = a * _roll_down(a, s, 1.0)
        s *= 2
    return b, (a0, b)


def _lin_scan_bwd(res, g):
    a, h = res
    c = _roll_up(a, 1, 0.0)
    s = 1
    while s < a.shape[0]:
        g = c * _roll_up(g, s, 0.0) + g
        c = c * _roll_up(c, s, 1.0)
        s *= 2
    return g * _roll_down(h, 1, 0.0), g


_lin_scan.defvjp(_lin_scan_fwd, _lin_scan_bwd)


@jax.custom_vjp
def _cumsum_rows(x):
    s = 1
    while s < x.shape[-2]:
        x = x + _roll_down(x, s, 0.0)
        s *= 2
    return x


def _cumsum_rows_fwd(x):
    return _cumsum_rows(x), None


def _cumsum_rows_bwd(_, g):
    s = 1
    while s < g.shape[-2]:
        g = g + _roll_up(g, s, 0.0)
        s *= 2
    return (g,)


_cumsum_rows.defvjp(_cumsum_rows_fwd, _cumsum_rows_bwd)


def _row_tile(t):
    return 256 if t % 256 == 0 else t


def _rowwise_fwd_call(name, f, rows, pars, tile):
    t = rows[0].shape[0]
    outs = jax.eval_shape(f, *[jax.ShapeDtypeStruct((tile, r.shape[1]), F32) for r in rows],
                          *[jax.ShapeDtypeStruct(p.shape, F32) for p in pars])
    nr, npar = len(rows), len(pars)

    def body(*refs):
        ins = [r[...] for r in refs[:nr + npar]]
        res = f(*ins)
        for o_ref, o in zip(refs[nr + npar:], res):
            o_ref[...] = o.astype(o_ref.dtype)

    return pl.pallas_call(
        body, name=name, grid=(t // tile,),
        in_specs=[pl.BlockSpec((tile, r.shape[1]), lambda i: (i, 0)) for r in rows]
        + [pl.BlockSpec(p.shape, lambda i: (0, 0)) for p in pars],
        out_specs=[pl.BlockSpec((tile, o.shape[1]), lambda i: (i, 0)) for o in outs],
        out_shape=[jax.ShapeDtypeStruct((t, o.shape[1]), F32) for o in outs],
        compiler_params=_params("parallel"),
    )(*rows, *pars)


def _rowwise_bwd_call(name, f, rows, pars, cts, tile):
    t = rows[0].shape[0]
    nr, npar, nct = len(rows), len(pars), len(cts)

    def body(*refs):
        ins = [r[...] for r in refs[:nr + npar]]
        gs = tuple(r[...] for r in refs[nr + npar:nr + npar + nct])
        outs = refs[nr + npar + nct:]
        _, vjp = jax.vjp(f, *ins)
        d = vjp(gs)
        for o_ref, v in zip(outs[:nr], d[:nr]):
            o_ref[...] = v

        @pl.when(pl.program_id(0) == 0)
        def _():
            for o_ref in outs[nr:]:
                o_ref[...] = jnp.zeros_like(o_ref)

        for o_ref, v in zip(outs[nr:], d[nr:]):
            o_ref[...] += v

    res = pl.pallas_call(
        body, name=name, grid=(t // tile,),
        in_specs=[pl.BlockSpec((tile, r.shape[1]), lambda i: (i, 0)) for r in rows]
        + [pl.BlockSpec(p.shape, lambda i: (0, 0)) for p in pars]
        + [pl.BlockSpec((tile, c.shape[1]), lambda i: (i, 0)) for c in cts],
        out_specs=[pl.BlockSpec((tile, r.shape[1]), lambda i: (i, 0)) for r in rows]
        + [pl.BlockSpec(p.shape, lambda i: (0, 0)) for p in pars],
        out_shape=[jax.ShapeDtypeStruct(r.shape, F32) for r in rows]
        + [jax.ShapeDtypeStruct(p.shape, F32) for p in pars],
        compiler_params=_params("arbitrary"),
    )(*rows, *pars, *cts)
    return tuple(res[:nr]), tuple(res[nr:])


def rowwise(name, f, rows, pars=()):
    tile = _row_tile(rows[0].shape[0])

    @jax.custom_vjp
    def op(rows, pars):
        return tuple(_rowwise_fwd_call(name, f, rows, pars, tile))

    def op_fwd(rows, pars):
        return op(rows, pars), (rows, pars)

    def op_bwd(res, cts):
        return _rowwise_bwd_call(name + "_bwd", f, res[0], res[1], tuple(cts), tile)

    op.defvjp(op_fwd, op_bwd)
    return op(tuple(rows), tuple(pars))


MM_TM = 512


def _tile_of(n, cap):
    best = None
    for c in range(LANES, min(n, cap) + 1, LANES):
        if n % c == 0:
            best = c
    return best or n


def _mmc_fwd(name, h, w):
    m, k = h.shape
    j, _, n = w.shape
    tm, tn = MM_TM, _tile_of(n, 1408)

    def body(h_ref, w_ref, o_ref):
        o_ref[0] = _dot(h_ref[...], w_ref[0])

    return pl.pallas_call(
        body, name=name, grid=(m // tm, j, n // tn),
        in_specs=[pl.BlockSpec((tm, k), lambda i, b, c: (i, 0)), pl.BlockSpec((1, k, tn), lambda i, b, c: (b, 0, c))],
        out_specs=pl.BlockSpec((1, tm, tn), lambda i, b, c: (b, i, c)),
        out_shape=jax.ShapeDtypeStruct((j, m, n), F32),
        compiler_params=_params("parallel", "parallel", "parallel"),
    )(h, w)


def _mmc_dh(name, dy, w):
    j, m, n = dy.shape
    k = w.shape[1]
    tm, tn = MM_TM, _tile_of(n, 1408)

    def body(dy_ref, w_ref, o_ref):
        part = _dot_nt(dy_ref[0], w_ref[0])

        @pl.when((pl.program_id(1) == 0) & (pl.program_id(2) == 0))
        def _():
            o_ref[...] = part

        @pl.when((pl.program_id(1) > 0) | (pl.program_id(2) > 0))
        def _():
            o_ref[...] += part

    return pl.pallas_call(
        body, name=name, grid=(m // tm, j, n // tn),
        in_specs=[pl.BlockSpec((1, tm, tn), lambda i, b, c: (b, i, c)), pl.BlockSpec((1, k, tn), lambda i, b, c: (b, 0, c))],
        out_specs=pl.BlockSpec((tm, k), lambda i, b, c: (i, 0)),
        out_shape=jax.ShapeDtypeStruct((m, k), F32),
        compiler_params=_params("parallel", "arbitrary", "arbitrary"),
    )(dy, w)


def _mmc_dw(name, h, dy):
    m, k = h.shape
    j, _, n = dy.shape
    tk, tn = _tile_of(k, 512), _tile_of(n, 1152)

    def body(h_ref, dy_ref, o_ref):
        o_ref[0] = _dot_tn(h_ref[...], dy_ref[0]).astype(BF16)

    return pl.pallas_call(
        body, name=name, grid=(j, k // tk, n // tn),
        in_specs=[pl.BlockSpec((m, tk), lambda b, i, c: (0, i)), pl.BlockSpec((1, m, tn), lambda b, i, c: (b, 0, c))],
        out_specs=pl.BlockSpec((1, tk, tn), lambda b, i, c: (b, i, c)),
        out_shape=jax.ShapeDtypeStruct((j, k, n), BF16),
        compiler_params=_params("parallel", "parallel", "parallel"),
    )(h, dy)


def mm_cols(name, h, w):
    @jax.custom_vjp
    def op(h, w):
        return _mmc_fwd(name, h, w)

    def op_fwd(h, w):
        return op(h, w), (h, w)

    def op_bwd(res, dy):
        h, w = res
        return _mmc_dh(name + "_dh", dy, w), _mmc_dw(name + "_dw", h, dy)

    op.defvjp(op_fwd, op_bwd)
    return op(h, w)


def _mmr_fwd(name, a, w):
    j, m, n = a.shape
    nn = w.shape[2]
    tm, tn = MM_TM, _tile_of(nn, 1024)

    def body(a_ref, w_ref, o_ref):
        part = _dot(a_ref[0], w_ref[0])

        @pl.when(pl.program_id(2) == 0)
        def _():
            o_ref[...] = part

        @pl.when(pl.program_id(2) > 0)
        def _():
            o_ref[...] += part

    return pl.pallas_call(
        body, name=name, grid=(m // tm, nn // tn, j),
        in_specs=[pl.BlockSpec((1, tm, n), lambda i, c, b: (b, i, 0)), pl.BlockSpec((1, n, tn), lambda i, c, b: (b, 0, c))],
        out_specs=pl.BlockSpec((tm, tn), lambda i, c, b: (i, c)),
        out_shape=jax.ShapeDtypeStruct((m, nn), F32),
        compiler_params=_params("parallel", "parallel", "arbitrary"),
    )(a, w)


def _mmr_da(name, dy, w):
    m, nn = dy.shape
    j, n, _ = w.shape
    tm = MM_TM

    def body(dy_ref, w_ref, o_ref):
        o_ref[0] = _dot_nt(dy_ref[...], w_ref[0])

    return pl.pallas_call(
        body, name=name, grid=(m // tm, j),
        in_specs=[pl.BlockSpec((tm, nn), lambda i, b: (i, 0)), pl.BlockSpec((1, n, nn), lambda i, b: (b, 0, 0))],
        out_specs=pl.BlockSpec((1, tm, n), lambda i, b: (b, i, 0)),
        out_shape=jax.ShapeDtypeStruct((j, m, n), F32),
        compiler_params=_params("parallel", "parallel"),
    )(dy, w)


def _mmr_dw(name, a, dy):
    j, m, n = a.shape
    nn = dy.shape[1]
    tn = _tile_of(nn, 512)

    def body(a_ref, dy_ref, o_ref):
        o_ref[0] = _dot_tn(a_ref[0], dy_ref[...]).astype(BF16)

    return pl.pallas_call(
        body, name=name, grid=(j, nn // tn),
        in_specs=[pl.BlockSpec((1, m, n), lambda b, c: (b, 0, 0)), pl.BlockSpec((m, tn), lambda b, c: (0, c))],
        out_specs=pl.BlockSpec((1, n, tn), lambda b, c: (b, 0, c)),
        out_shape=jax.ShapeDtypeStruct((j, n, nn), BF16),
        compiler_params=_params("parallel", "parallel"),
    )(a, dy)


def mm_rows(name, a, w):
    @jax.custom_vjp
    def op(a, w):
        return _mmr_fwd(name, a, w)

    def op_fwd(a, w):
        return op(a, w), (a, w)

    def op_bwd(res, dy):
        a, w = res
        return _mmr_da(name + "_da", dy, w), _mmr_dw(name + "_dw", a, dy)

    op.defvjp(op_fwd, op_bwd)
    return op(a, w)


def _colwise_specs(cols, pars, par_block):
    t = cols[0].shape[0]
    specs = [pl.BlockSpec((t, LANES), lambda j: (0, j)) for _ in cols]
    for p, blk in zip(pars, par_block):
        if blk == "lane":
            specs.append(pl.BlockSpec((p.shape[0], LANES), lambda j: (0, j)))
        else:
            specs.append(pl.BlockSpec((1,) + p.shape[1:], lambda j: (j, 0, 0)))
    return specs


def _colwise_fwd_call(name, f, cols, pars, par_block, n_out):
    t, c = cols[0].shape
    nc, npar = len(cols), len(pars)

    def body(*refs):
        ins = [r[...] for r in refs[:nc]] + [r[...] if b == "lane" else r[0] for r, b in zip(refs[nc:nc + npar], par_block)]
        res = f(*ins)
        for o_ref, o in zip(refs[nc + npar:], res):
            o_ref[...] = o

    return pl.pallas_call(
        body, name=name, grid=(c // LANES,),
        in_specs=_colwise_specs(cols, pars, par_block),
        out_specs=[pl.BlockSpec((t, LANES), lambda j: (0, j)) for _ in range(n_out)],
        out_shape=[jax.ShapeDtypeStruct((t, c), F32) for _ in range(n_out)],
        compiler_params=_params("parallel"),
    )(*cols, *pars)


def _colwise_bwd_call(name, f, cols, pars, par_block, cts):
    t, c = cols[0].shape
    nc, npar, nct = len(cols), len(pars), len(cts)

    def body(*refs):
        ins = [r[...] for r in refs[:nc]] + [r[...] if b == "lane" else r[0] for r, b in zip(refs[nc:nc + npar], par_block)]
        gs = tuple(r[...] for r in refs[nc + npar:nc + npar + nct])
        outs = refs[nc + npar + nct:]
        _, vjp = jax.vjp(f, *ins)
        d = vjp(gs)
        for o_ref, v in zip(outs[:nc], d[:nc]):
            o_ref[...] = v
        for o_ref, v, b in zip(outs[nc:], d[nc:], par_block):
            if b == "lane":
                o_ref[...] = v
            else:
                o_ref[0] = v

    res = pl.pallas_call(
        body, name=name, grid=(c // LANES,),
        in_specs=_colwise_specs(cols, pars, par_block) + [pl.BlockSpec((t, LANES), lambda j: (0, j)) for _ in cts],
        out_specs=_colwise_specs(cols, pars, par_block),
        out_shape=[jax.ShapeDtypeStruct(v.shape, F32) for v in (*cols, *pars)],
        compiler_params=_params("parallel"),
    )(*cols, *pars, *cts)
    return tuple(res[:nc]), tuple(res[nc:])


def colwise(name, f, cols, pars, par_block, n_out):
    @jax.custom_vjp
    def op(cols, pars):
        return tuple(_colwise_fwd_call(name, f, cols, pars, par_block, n_out))

    def op_fwd(cols, pars):
        return op(cols, pars), (cols, pars)

    def op_bwd(res, cts):
        return _colwise_bwd_call(name + "_bwd", f, res[0], res[1], par_block, tuple(cts))

    op.defvjp(op_fwd, op_bwd)
    return op(tuple(cols), tuple(pars))


def _rg_block(x, gate, cw, cb, wr, br, wi, bi, lam):
    xa = _causal_conv(x, cw) + cb
    r = _sigmoid(_dot(xa, wr) + br)
    i = _sigmoid(_dot(xa, wi) + bi)
    log_a = -RG_C * r * _softplus(-lam)
    a = jnp.exp(log_a)
    b = jnp.sqrt(_neg_expm1(2.0 * log_a)) * (i * xa)
    return (_lin_scan(a, b) * _gelu(gate),)


def _dn_conv_block(mode):
    def f(x, cw):
        c = _silu(_causal_conv(x, cw))
        if mode == "v":
            return (c,)
        c = c * lax.rsqrt(jnp.sum(c * c, axis=-1, keepdims=True) + EPS)
        return (c * (DN_HEAD_DIM ** -0.5),) if mode == "q" else (c,)
    return f


def _block_diag(w):
    w = w.reshape(8, 2, 64, 64)
    z = jnp.zeros((8, 64, 64), w.dtype)
    top = jnp.concatenate([w[:, 0], z], axis=2)
    bot = jnp.concatenate([z, w[:, 1]], axis=2)
    return jnp.concatenate([top, bot], axis=1)


DN_HP = 8


def _dn_block(S, qw, kw, vw, gb, h0):
    hp, hd = S.shape[0], DN_HEAD_DIM
    heads = lambda a: jnp.concatenate([a[None, :, j * hd:(j + 1) * hd] for j in range(hp)], axis=0)
    lane = lax.broadcasted_iota(jnp.int32, gb.shape, 1)
    col = lambda i: jnp.sum(jnp.where(lane == i, gb, 0.0), axis=1, keepdims=True)[None]
    beta = jnp.concatenate([col(h0 + j) for j in range(hp)], axis=0)
    g = jnp.concatenate([col(h0 + j + DN_HEADS) for j in range(hp)], axis=0)
    s_new, o = _dn_step(S, heads(qw), heads(kw), heads(vw), beta, g)
    return s_new, jnp.concatenate([o[j:j + 1].reshape(o.shape[1:]) for j in range(hp)], axis=1)


def _dn_step(S, q, k, v, beta, g):
    c = DN_CHUNK
    ri = lax.broadcasted_iota(jnp.int32, (c, c), 0)
    ci = lax.broadcasted_iota(jnp.int32, (c, c), 1)
    incl, strict = ri >= ci, ri > ci
    eye = (ri == ci).astype(F32)
    gam = _cumsum_rows(g)
    gam_row = jnp.sum(jnp.where(ri <= ci, g, 0.0), axis=-2, keepdims=True)
    gam_last = jnp.sum(g, axis=-2, keepdims=True)
    decay = jnp.where(incl, jnp.exp(jnp.where(incl, gam - gam_row, 0.0)), 0.0)
    kb = k * beta
    vb = v * beta
    a = jnp.where(strict, _dot_nt(kb, k) * decay, 0.0)
    p = -a
    tinv = eye + p
    for _ in range(5):
        p = _dot3(p, p)
        tinv = tinv + _dot3(tinv, p)
    e_gam = jnp.exp(gam)
    u0 = _dot3(tinv, vb)
    wk = _dot3(tinv, kb * e_gam)
    qk = jnp.where(incl, _dot_nt(q, k) * decay, 0.0)
    q_dec = q * e_gam
    k_dec = k * jnp.exp(gam_last - gam)
    u = u0 - _dot(wk, S)
    o = _dot(q_dec, S) + _dot(qk, u)
    s_new = S * jnp.exp(gam_last) + _dot_tn(k_dec, u)
    return s_new, o


def _dn_fwd_call(q, k, v, gb):
    t, w = q.shape
    n, hp, hd, c = t // DN_CHUNK, DN_HP, DN_HEAD_DIM, DN_CHUNK

    def body(q_ref, k_ref, v_ref, gb_ref, o_ref, s0_ref, s_scr):
        @pl.when(pl.program_id(1) == 0)
        def _():
            s_scr[...] = jnp.zeros_like(s_scr)

        s_old = s_scr[...]
        s0_ref[:, 0] = s_old
        s_new, o = _dn_block(s_old, q_ref[...], k_ref[...], v_ref[...], gb_ref[...], pl.program_id(0) * hp)
        o_ref[...] = o
        s_scr[...] = s_new

    blk = pl.BlockSpec((c, hp * hd), lambda g, i: (i, g))
    return pl.pallas_call(
        body, name="dn_core", grid=(DN_HEADS // hp, n),
        in_specs=[blk, blk, blk, pl.BlockSpec((c, LANES), lambda g, i: (i, 0))],
        out_specs=[blk, pl.BlockSpec((hp, 1, hd, hd), lambda g, i: (g, i, 0, 0))],
        out_shape=[jax.ShapeDtypeStruct((t, w), F32), jax.ShapeDtypeStruct((DN_HEADS, n, hd, hd), F32)],
        scratch_shapes=[pltpu.VMEM((hp, hd, hd), F32)],
        compiler_params=_params("parallel", "arbitrary"),
    )(q, k, v, gb)


def _dn_bwd_call(q, k, v, gb, s0, do):
    t, w = q.shape
    n, hp, hd, c = t // DN_CHUNK, DN_HP, DN_HEAD_DIM, DN_CHUNK
    ng = DN_HEADS // hp

    def body(q_ref, k_ref, v_ref, gb_ref, s0_ref, do_ref, dq_ref, dk_ref, dv_ref, dgb_ref, ds_scr):
        @pl.when(pl.program_id(1) == 0)
        def _():
            ds_scr[...] = jnp.zeros_like(ds_scr)

        h0 = pl.program_id(0) * hp
        _, vjp = jax.vjp(lambda *a: _dn_block(*a, h0), s0_ref[:, 0], q_ref[...], k_ref[...], v_ref[...], gb_ref[...])
        ds, dq, dk, dv, dgb = vjp((ds_scr[...], do_ref[...]))
        ds_scr[...] = ds
        dq_ref[...], dk_ref[...], dv_ref[...] = dq, dk, dv
        dgb_ref[0] = dgb

    blk = pl.BlockSpec((c, hp * hd), lambda g, i: (n - 1 - i, g))
    res = pl.pallas_call(
        body, name="dn_core_bwd", grid=(ng, n),
        in_specs=[blk, blk, blk, pl.BlockSpec((c, LANES), lambda g, i: (n - 1 - i, 0)),
                  pl.BlockSpec((hp, 1, hd, hd), lambda g, i: (g, n - 1 - i, 0, 0)), blk],
        out_specs=[blk, blk, blk, pl.BlockSpec((1, c, LANES), lambda g, i: (g, n - 1 - i, 0))],
        out_shape=[jax.ShapeDtypeStruct((t, w), F32)] * 3 + [jax.ShapeDtypeStruct((ng, t, LANES), F32)],
        scratch_shapes=[pltpu.VMEM((hp, hd, hd), F32)],
        compiler_params=_params("parallel", "arbitrary"),
    )(q, k, v, gb, s0, do)
    return res[0], res[1], res[2], jnp.sum(res[3], axis=0)


@jax.custom_vjp
def dn_core(q, k, v, gb):
    return _dn_fwd_call(q, k, v, gb)[0]


def _dn_core_fwd(q, k, v, gb):
    o, s0 = _dn_fwd_call(q, k, v, gb)
    return o, (q, k, v, gb, s0)


def _dn_core_bwd(res, do):
    return _dn_bwd_call(*res, do)


dn_core.defvjp(_dn_core_fwd, _dn_core_bwd)


ATT_RB = 4


def _att_block(q, kp, kc, vp, vc, qn, kn, slope, has_prev, dil):
    s = ATT_SPAN
    qh = _rms(q, qn) * (ATT_HEAD_DIM ** -0.5)
    qi = lax.broadcasted_iota(jnp.int32, (s, s), 0)
    kj = lax.broadcasted_iota(jnp.int32, (s, s), 1)
    d_p = qi + s - kj
    d_c = qi - kj
    s_p = _dot_nt(qh, _rms(kp, kn)) - slope * (d_p * dil).astype(F32)
    s_c = _dot_nt(qh, _rms(kc, kn)) - slope * (d_c * dil).astype(F32)
    s_p = jnp.where((d_p <= s) & (has_prev > 0), s_p, NEG_INF)
    s_c = jnp.where(d_c >= 0, s_c, NEG_INF)
    m = lax.stop_gradient(jnp.maximum(jnp.max(s_p, axis=-1, keepdims=True), jnp.max(s_c, axis=-1, keepdims=True)))
    p_p = jnp.exp(s_p - m)
    p_c = jnp.exp(s_c - m)
    den = jnp.sum(p_p, axis=-1, keepdims=True) + jnp.sum(p_c, axis=-1, keepdims=True)
    o = _dot(p_p / den, vp) + _dot(p_c / den, vc)
    lse = m + jnp.log(den)
    return o, jnp.broadcast_to(lse, o.shape)


def _att_specs(dil):
    rb = ATT_RB
    hb = max(1, rb // dil)
    cur = pl.BlockSpec((rb, ATT_SPAN, ATT_HEAD_DIM), lambda i, n: (i, n, 0))
    prev = pl.BlockSpec((rb, ATT_SPAN, ATT_HEAD_DIM), lambda i, n: (i, jnp.maximum(n - 1, 0), 0))
    gain = pl.BlockSpec((hb, 1, ATT_HEAD_DIM), lambda i, n: ((i * rb) // (dil * hb), 0, 0))
    return cur, prev, gain


def _att_slope(group, dil):
    first = pl.program_id(0) * ATT_RB
    if dil == 1:
        head = first + lax.broadcasted_iota(jnp.int32, (ATT_RB, 1, 1), 0)
    else:
        head = jnp.full((1, 1, 1), first // dil, jnp.int32)
    return jnp.exp((head + (4 * group + 1)).astype(F32) * (-8.0 / ATT_HEADS * math.log(2.0)))


def _att_fwd_call(name, group, dil, q, k, v, qn, kn):
    r, l, e = q.shape
    nblk = l // ATT_SPAN
    cur, prev, gain = _att_specs(dil)

    def body(q_ref, kp_ref, kc_ref, vp_ref, vc_ref, qn_ref, kn_ref, o_ref, lse_ref):
        o_ref[...], lse_ref[...] = _att_block(q_ref[...], kp_ref[...], kc_ref[...], vp_ref[...], vc_ref[...], qn_ref[...],
                                              kn_ref[...], _att_slope(group, dil), pl.program_id(1), dil)

    return pl.pallas_call(
        body, name=name, grid=(r // ATT_RB, nblk),
        in_specs=[cur, prev, cur, prev, cur, gain, gain], out_specs=[cur, cur],
        out_shape=[jax.ShapeDtypeStruct(q.shape, F32)] * 2,
        compiler_params=_params("parallel", "arbitrary"),
    )(q, k, k, v, v, qn, kn)


def _att_bwd_call(name, group, dil, q, k, v, qn, kn, do, dlse):
    r, l, e = q.shape
    nblk = l // ATT_SPAN
    cur, prev, gain = _att_specs(dil)
    rows_per_gain = dil * max(1, ATT_RB // dil)

    def body(q_ref, kp_ref, kc_ref, vp_ref, vc_ref, qn_ref, kn_ref, do_ref, dlse_ref,
             dq_ref, dkp_ref, dkc_ref, dvp_ref, dvc_ref, dqn_ref, dkn_ref):
        slope, has_prev = _att_slope(group, dil), pl.program_id(1)
        _, vjp = jax.vjp(lambda *a: _att_block(*a, slope, has_prev, dil), q_ref[...], kp_ref[...], kc_ref[...], vp_ref[...],
                         vc_ref[...], qn_ref[...], kn_ref[...])
        dq, dkp, dkc, dvp, dvc, dqn, dkn = vjp((do_ref[...], dlse_ref[...]))
        dq_ref[...], dkp_ref[...], dkc_ref[...], dvp_ref[...], dvc_ref[...] = dq, dkp, dkc, dvp, dvc

        @pl.when(((pl.program_id(0) * ATT_RB) % rows_per_gain == 0) & (pl.program_id(1) == 0))
        def _():
            dqn_ref[...] = jnp.zeros_like(dqn_ref)
            dkn_ref[...] = jnp.zeros_like(dkn_ref)

        dqn_ref[...] += dqn
        dkn_ref[...] += dkn

    res = pl.pallas_call(
        body, name=name + "_bwd", grid=(r // ATT_RB, nblk),
        in_specs=[cur, prev, cur, prev, cur, gain, gain, cur, cur],
        out_specs=[cur] * 5 + [gain, gain],
        out_shape=[jax.ShapeDtypeStruct(q.shape, F32)] * 5 + [jax.ShapeDtypeStruct(qn.shape, F32)] * 2,
        compiler_params=_params("arbitrary", "arbitrary"),
    )(q, k, k, v, v, qn, kn, do, dlse)
    dq, dkp, dkc, dvp, dvc, dqn, dkn = res
    back = lambda g: jnp.pad(g[:, ATT_SPAN:], ((0, 0), (0, ATT_SPAN), (0, 0)))
    return dq, dkc + back(dkp), dvc + back(dvp), dqn, dkn


def att_group(name, group, dil, q, k, v, qn, kn):
    @jax.custom_vjp
    def op(q, k, v, qn, kn):
        return tuple(_att_fwd_call(name, group, dil, q, k, v, qn, kn))

    def op_fwd(q, k, v, qn, kn):
        return op(q, k, v, qn, kn), (q, k, v, qn, kn)

    def op_bwd(res, cts):
        return _att_bwd_call(name, group, dil, *res, *cts)

    op.defvjp(op_fwd, op_bwd)
    return op(q, k, v, qn, kn)


def _att_mix(o1, o2, o3, l1, l2, l3):
    m = jnp.maximum(jnp.maximum(l1, l2), l3)
    e1, e2, e3 = jnp.exp(l1 - m), jnp.exp(l2 - m), jnp.exp(l3 - m)
    s = e1 + e2 + e3
    return (jnp.concatenate([o1 * (e1 / s), o2 * (e2 / s), o3 * (e3 / s)], axis=1),)


def att_branch(name, pa, qn, kn, groups=ATT_GROUPS):
    t = pa.shape[0]
    hg, e = 4, ATT_HEAD_DIM
    q, k, v = (pa[:, i * 768:(i + 1) * 768].reshape(t, ATT_HEADS, e) for i in range(3))
    outs, lses = [], []
    for g, (window, dil) in enumerate(groups):
        assert window // dil == ATT_SPAN
        l = t // dil
        to_r = lambda a: a[:, hg * g:hg * (g + 1)].reshape(l, dil, hg, e).transpose(2, 1, 0, 3).reshape(hg * dil, l, e)
        back = lambda a: a.reshape(hg, dil, l, e).transpose(2, 1, 0, 3).reshape(t, hg * e)
        o, lse = att_group(f"{name}_att{g}", g, dil, to_r(q), to_r(k), to_r(v),
                           qn[hg * g:hg * (g + 1)].reshape(hg, 1, e), kn[hg * g:hg * (g + 1)].reshape(hg, 1, e))
        outs.append(back(o))
        lses.append(back(lse))
    return rowwise(f"{name}_attmix", _att_mix, outs + lses)[0]


def dn_gates(name, ba, a_log, dt_bias):
    place = lambda p: jnp.pad(p.reshape(1, DN_HEADS), ((0, 0), (DN_HEADS, LANES - 2 * DN_HEADS)))

    def f(x, al, dt):
        lane = lax.broadcasted_iota(jnp.int32, x.shape, 1)
        return (jnp.where(lane < DN_HEADS, _sigmoid(x), -jnp.exp(al) * _softplus(x + dt)),)

    return rowwise(name, f, (ba,), (place(a_log), place(dt_bias)))[0]


def _dn_out(o, z, g):
    parts = []
    for h in range(DN_HEADS):
        sl = slice(h * DN_HEAD_DIM, (h + 1) * DN_HEAD_DIM)
        parts.append(_rms(o[:, sl], g[:, sl]) * _silu(z[:, sl]))
    return (jnp.concatenate(parts, axis=1),)


def _merge(ml, za, zb, zc):
    d = D_MODEL
    return (_sigmoid(ml[:, :d]) * za + _sigmoid(ml[:, d:2 * d]) * zb + _sigmoid(ml[:, 2 * d:]) * zc,)


def _swiglu_act(g, u):
    return (_silu(g) * u,)


def add_norm(name, x, pend, scale, gain):
    if pend is None:
        return x, rowwise(name, lambda a, g: (_rms(a, g),), (x,), (gain,))[0]

    def f(a, b, g):
        s = a + scale * b
        return s, _rms(s, g)

    return rowwise(name, f, (x, pend), (gain,))


def ffn(name, h, wg, wu, wd):
    g, u = mm_cols(name + "_g", h, wg), mm_cols(name + "_u", h, wu)
    j, t, n = g.shape
    a = rowwise(name + "_act", _swiglu_act, (g.reshape(j * t, n), u.reshape(j * t, n)))[0]
    return mm_rows(name + "_d", a.reshape(j, t, n), wd)


W_IN_PIECES = (("rgx", 0, 1024), ("gate", 1024, 1024), ("att", 2048, 2304), ("dq", 4352, 1024), ("dk", 5376, 1024),
               ("dv", 6400, 1024), ("dz", 7424, 1024), ("ba", 8448, 16), ("mrg", 8464, 3072))
RG_PAR_BLOCKS = ("lane", "lane", "blk", "lane", "blk", "lane", "lane")


def mixer(name, u, w, p):
    mm = lambda nm, a, wt: mm_rows(nm, a[None], wt[None])
    pr = {k: mm_cols(f"{name}_in_{k}", u, w["in_" + k][None])[0] for k, _, _ in W_IN_PIECES}
    ya = colwise(name + "_rg", _rg_block, (pr["rgx"], pr["gate"]),
                 (w["rg_conv_w"], p["rg_conv_b"], _block_diag(p["rg_w_r"]), p["rg_b_r"], _block_diag(p["rg_w_i"]),
                  p["rg_b_i"], p["rg_lambda"]), RG_PAR_BLOCKS, 1)[0]
    yb = att_branch(name, pr["att"], p["att_q_norm"], p["att_k_norm"])
    cw = w["dn_conv_w"]
    cq = colwise(name + "_dnq", _dn_conv_block("q"), (pr["dq"],), (cw[:, :1024],), ("lane",), 1)[0]
    ck = colwise(name + "_dnk", _dn_conv_block("k"), (pr["dk"],), (cw[:, 1024:2048],), ("lane",), 1)[0]
    cv = colwise(name + "_dnv", _dn_conv_block("v"), (pr["dv"],), (cw[:, 2048:],), ("lane",), 1)[0]
    gb = dn_gates(name + "_dngate", pr["ba"], p["dn_a_log"], p["dn_dt_bias"])
    o_dn = dn_core(cq, ck, cv, gb)
    yc = rowwise(name + "_dnout", _dn_out, (o_dn, pr["dz"]), (p["dn_out_norm"].reshape(1, D_MODEL),))[0]
    y = rowwise(name + "_merge", _merge, (pr["mrg"], mm(name + "_ba", ya, w["br_a"]), mm(name + "_bb", yb, w["br_b"]),
                                          mm(name + "_bc", yc, w["br_c"])))[0]
    return mm(name + "_out", y, w["w_out"])


def _loss_call(x, pend, target):
    t, d = x.shape
    tile = _row_tile(t)

    def body(x_ref, p_ref, t_ref, loss_ref, g_ref):
        err = x_ref[...] + 0.5 * p_ref[...] - t_ref[...]
        g_ref[...] = err * (1.0 / d)

        @pl.when(pl.program_id(0) == 0)
        def _():
            loss_ref[...] = jnp.zeros_like(loss_ref)

        loss_ref[...] += jnp.full(loss_ref.shape, 0.5 / d, F32) * jnp.sum(err * err)

    blk = pl.BlockSpec((tile, d), lambda i: (i, 0))
    loss, g = pl.pallas_call(
        body, name="loss", grid=(t // tile,), in_specs=[blk, blk, blk],
        out_specs=[pl.BlockSpec((8, LANES), lambda i: (0, 0)), blk],
        out_shape=[jax.ShapeDtypeStruct((8, LANES), F32), jax.ShapeDtypeStruct((t, d), F32)],
        compiler_params=_params("arbitrary"),
    )(x, pend, target)
    return loss[0, 0], g


@jax.custom_vjp
def loss_op(x, pend, target):
    return _loss_call(x, pend, target)[0]


def _loss_fwd(x, pend, target):
    loss, g = _loss_call(x, pend, target)
    return loss, g


def _loss_bwd(g, ct):
    return ct * g, (0.5 * ct) * g, None


loss_op.defvjp(_loss_fwd, _loss_bwd)


def local_loss(w, p, x, target):
    pend, scale = None, 0.0
    for l in range(len(w)):
        n = f"L{l}"
        x, h = add_norm(n + "_n1", x, pend, scale, p[l]["ffn1_norm"])
        pend, scale = ffn(n + "_f1", h, w[l]["ffn1_w_gate"], w[l]["ffn1_w_up"], w[l]["ffn1_w_down"]), 0.5
        x, h = add_norm(n + "_nm", x, pend, scale, p[l]["mix_norm"])
        pend, scale = mixer(n + "_mx", h, w[l], p[l]), 1.0
        x, h = add_norm(n + "_n2", x, pend, scale, p[l]["ffn2_norm"])
        pend, scale = ffn(n + "_f2", h, w[l]["ffn2_w_gate"], w[l]["ffn2_w_up"], w[l]["ffn2_w_down"]), 0.5
    return loss_op(x, pend, target)


WEIGHT_NAMES = ("ffn1_norm", "ffn1_w_gate", "ffn1_w_up", "ffn1_w_down", "mix_norm", "w_in", "rg_conv_w", "rg_conv_b",
                "rg_w_r", "rg_b_r", "rg_w_i", "rg_b_i", "rg_lambda", "att_q_norm", "att_k_norm", "dn_conv_w", "dn_a_log",
                "dn_dt_bias", "dn_out_norm", "w_branch", "w_out", "ffn2_norm", "ffn2_w_gate", "ffn2_w_up", "ffn2_w_down")
MATRICES = (("ffn1_w_gate", 2), ("ffn1_w_up", 2), ("ffn1_w_down", 1), ("w_in", 2), ("w_branch", 1), ("w_out", 1),
            ("ffn2_w_gate", 2), ("ffn2_w_up", 2), ("ffn2_w_down", 1))
CONVS = (("rg_conv_w", 2), ("dn_conv_w", 2))
SHARD_AXIS = dict(MATRICES + CONVS)
SMALL_NAMES = tuple(n for n in WEIGHT_NAMES if n not in SHARD_AXIS)
ROW_PARAMS = ("ffn1_norm", "mix_norm", "rg_conv_b", "rg_b_r", "rg_b_i", "rg_lambda", "ffn2_norm")
FFN_MATS = ("ffn1_w_gate", "ffn1_w_up", "ffn1_w_down", "ffn2_w_gate", "ffn2_w_up", "ffn2_w_down")
W_IN_SHARD = 2884


def _shard_minor(a, axis):
    a = jnp.moveaxis(a, 0, axis)
    return a.reshape(a.shape[:axis] + (N_CHIPS * a.shape[axis + 1],) + a.shape[axis + 2:])


def _w_in_piece(g, l, off, n):
    s = W_IN_SHARD
    parts = [g[j, l][:, max(off, j * s) - j * s:min(off + n, (j + 1) * s) - j * s]
             for j in range(N_CHIPS) if max(off, j * s) < min(off + n, (j + 1) * s)]
    return jnp.concatenate(parts, axis=1) if len(parts) > 1 else parts[0]


def _w_in_chip_grad(gl, j):
    s = W_IN_SHARD
    parts = [gl["in_" + k][:, max(off, j * s) - off:min(off + n, (j + 1) * s) - off]
             for k, off, n in W_IN_PIECES if max(off, j * s) < min(off + n, (j + 1) * s)]
    return jnp.concatenate(parts, axis=1)


def layer_weights(g, conv, l):
    w = {n: g[n][:, l] for n in FFN_MATS}
    w["w_out"] = g["w_out"][:, l].reshape(D_MODEL, D_MODEL)
    for k, off, n in W_IN_PIECES:
        piece = _w_in_piece(g["w_in"], l, off, n)
        w["in_" + k] = jnp.pad(piece, ((0, 0), (0, LANES - n))) if n < LANES else piece
    wb = g["w_branch"][:, l].reshape(-1, D_MODEL)
    w["br_a"], w["br_b"], w["br_c"] = wb[:1024], wb[1024:1792], wb[1792:]
    w["rg_conv_w"], w["dn_conv_w"] = conv["rg_conv_w"][l], conv["dn_conv_w"][l]
    return w


def layer_weight_grads(gw):
    stack = lambda f: jnp.stack([f(gl) for gl in gw], axis=1)
    out = {n: stack(lambda gl: gl[n]) for n in FFN_MATS}
    out["w_out"] = stack(lambda gl: gl["w_out"].reshape(N_CHIPS, -1, D_MODEL))
    out["w_branch"] = stack(lambda gl: jnp.concatenate([gl["br_a"], gl["br_b"], gl["br_c"]], axis=0).reshape(N_CHIPS, -1, D_MODEL))
    out["w_in"] = stack(lambda gl: jnp.stack([_w_in_chip_grad(gl, j) for j in range(N_CHIPS)]))
    conv = {n: jnp.stack([gl[n] for gl in gw]) for n, _ in CONVS}
    return out, conv


def layer_small(small, l):
    p = {n: small[n][l] for n in SMALL_NAMES}
    for n in ROW_PARAMS:
        p[n] = small[n][l:l + 1]
    return p


def layer_small_grads(gp, small):
    return {n: jnp.stack([g[n] for g in gp]).reshape(small[n].shape) for n in SMALL_NAMES}


HBM_SPEC = pl.BlockSpec(memory_space=pl.ANY)


def _place():
    x, y, c = lax.axis_index("x"), lax.axis_index("y"), lax.axis_index("c")
    other_chips = [(1 - x, y), (x, 1 - y), (1 - x, 1 - y)]
    return x, y, c, 2 * x + y, (x, y, 1 - c), other_chips


def _half_rows(ref, lead, hc):
    hr = ref.shape[-2] // 2
    return ref.at[(*lead, slice(None), pl.ds(pl.multiple_of(hc * hr, 16), hr), slice(None))]


def _chip_index():
    return (2 * lax.axis_index("x") + lax.axis_index("y")).astype(jnp.int32).reshape(1)


def cast_into_block(name, w):
    l, rows, cols = w.shape
    tr = rows // 2

    def body(me_ref, w_ref, o_ref):
        o_ref[0] = w_ref[...].astype(BF16)

    return pl.pallas_call(
        body, name=name, out_shape=jax.ShapeDtypeStruct((N_CHIPS, l, rows, cols), BF16),
        grid_spec=pltpu.PrefetchScalarGridSpec(
            num_scalar_prefetch=1, grid=(l, rows // tr),
            in_specs=[pl.BlockSpec((1, tr, cols), lambda a, i, me: (a, i, 0))],
            out_specs=pl.BlockSpec((1, 1, tr, cols), lambda a, i, me: (me[0], a, i, 0))),
        compiler_params=_params("parallel", "parallel"),
    )(_chip_index(), w)


def allgather_mats(bufs):
    n = len(bufs)

    def body(*refs):
        ins, outs = refs[:n], refs[n:2 * n]
        send_sems, recv_sems = refs[2 * n:]
        x, y, c, me, sibling, chips = _place()

        def copy(s, src, dst, to):
            return pltpu.make_async_remote_copy(src_ref=src, dst_ref=dst, send_sem=send_sems.at[s], recv_sem=recv_sems.at[s],
                                                device_id=to, device_id_type=MESH)

        first, passed = [], []
        for j, (cx, cy) in enumerate(chips):
            for i in range(n):
                cp = copy(6 * i + j, _half_rows(ins[i], (me,), c), _half_rows(outs[i], (me,), c), (cx, cy, c))
                cp.start()
                first.append(cp)
        for j, (cx, cy) in enumerate(chips):
            k = 2 * cx + cy
            for i in range(n):
                copy(6 * i + j, _half_rows(ins[i], (me,), c), _half_rows(outs[i], (k,), c), (cx, cy, c)).wait_recv()
                cp = copy(6 * i + 3 + j, _half_rows(outs[i], (k,), c), _half_rows(outs[i], (k,), c), sibling)
                cp.start()
                passed.append(cp)
        for j, (cx, cy) in enumerate(chips):
            k = 2 * cx + cy
            for i in range(n):
                copy(6 * i + 3 + j, _half_rows(ins[i], (me,), c), _half_rows(outs[i], (k,), 1 - c), sibling).wait_recv()
        for cp in first + passed:
            cp.wait_send()

    return pl.pallas_call(
        body, name="allgather_mats", out_shape=[jax.ShapeDtypeStruct(b.shape, b.dtype) for b in bufs],
        in_specs=[HBM_SPEC] * n, out_specs=[HBM_SPEC] * n, input_output_aliases={i: i for i in range(n)},
        scratch_shapes=[pltpu.SemaphoreType.DMA((6 * n,)), pltpu.SemaphoreType.DMA((6 * n,))],
    )(*bufs)


PEER_FLIPS = tuple((fx, fy, fc) for fx in (0, 1) for fy in (0, 1) for fc in (0, 1))[1:]


def exchange_grad_pieces(gs):
    n = len(gs)

    def body(*refs):
        ins, outs = refs[:n], refs[n:2 * n]
        send_sems, recv_sems, local_sems = refs[2 * n:]
        x, y, c, me, sibling, chips = _place()
        my_dev = 4 * x + 2 * y + c
        flip = lambda v, f: 1 - v if f else v
        local, sends = [], []
        for i in range(n):
            cp = pltpu.make_async_copy(_half_rows(ins[i], (me,), c), outs[i].at[my_dev], local_sems.at[i])
            cp.start()
            local.append(cp)
        for r, (fx, fy, fc) in enumerate(PEER_FLIPS):
            px, py, pc = flip(x, fx), flip(y, fy), flip(c, fc)
            for i in range(n):
                cp = pltpu.make_async_remote_copy(
                    src_ref=_half_rows(ins[i], (2 * px + py,), pc), dst_ref=outs[i].at[my_dev], send_sem=send_sems.at[7 * i + r],
                    recv_sem=recv_sems.at[7 * i + r], device_id=(px, py, pc), device_id_type=MESH)
                cp.start()
                sends.append(cp)
        for r, (fx, fy, fc) in enumerate(PEER_FLIPS):
            px, py, pc = flip(x, fx), flip(y, fy), flip(c, fc)
            for i in range(n):
                pltpu.make_async_remote_copy(
                    src_ref=_half_rows(ins[i], (me,), c), dst_ref=outs[i].at[4 * px + 2 * py + pc], send_sem=send_sems.at[7 * i + r],
                    recv_sem=recv_sems.at[7 * i + r], device_id=(px, py, pc), device_id_type=MESH).wait_recv()
        for cp in sends:
            cp.wait_send()
        for cp in local:
            cp.wait()

    return pl.pallas_call(
        body, name="exchange_grad_pieces",
        out_shape=[jax.ShapeDtypeStruct((N_DEV, g.shape[1], g.shape[2] // 2, g.shape[3]), g.dtype) for g in gs],
        in_specs=[HBM_SPEC] * n, out_specs=[HBM_SPEC] * n,
        scratch_shapes=[pltpu.SemaphoreType.DMA((7 * n,)), pltpu.SemaphoreType.DMA((7 * n,)), pltpu.SemaphoreType.DMA((n,))],
    )(*gs)


def sibling_share_halves(fs):
    n = len(fs)

    def body(*refs):
        ins, outs = refs[:n], refs[n:2 * n]
        send_sems, recv_sems = refs[2 * n:]
        x, y, c, me, sibling, chips = _place()
        sends = []
        for i in range(n):
            cp = pltpu.make_async_remote_copy(src_ref=_half_rows(ins[i], (), c), dst_ref=_half_rows(outs[i], (), c),
                                              send_sem=send_sems.at[i], recv_sem=recv_sems.at[i], device_id=sibling, device_id_type=MESH)
            cp.start()
            sends.append(cp)
        for i in range(n):
            pltpu.make_async_remote_copy(src_ref=_half_rows(ins[i], (), c), dst_ref=_half_rows(outs[i], (), 1 - c),
                                         send_sem=send_sems.at[i], recv_sem=recv_sems.at[i], device_id=sibling,
                                         device_id_type=MESH).wait_recv()
        for cp in sends:
            cp.wait_send()

    return pl.pallas_call(
        body, name="sibling_share_halves", out_shape=[jax.ShapeDtypeStruct(f.shape, f.dtype) for f in fs],
        in_specs=[HBM_SPEC] * n, out_specs=[HBM_SPEC] * n, input_output_aliases={i: i for i in range(n)},
        scratch_shapes=[pltpu.SemaphoreType.DMA((n,)), pltpu.SemaphoreType.DMA((n,))],
    )(*fs)


def allgather_small(name, v):
    m_per, n = v.shape

    def body(x_ref, out_ref, send_sems, recv_sems, local_sem):
        x, y, c, _, sibling, chips = _place()
        me = (x, y, c)

        def rows(px, py, pc):
            return out_ref.at[pl.ds((4 * px + 2 * py + pc) * m_per, m_per), :]

        def copy(k, block, to, src=None):
            return pltpu.make_async_remote_copy(src_ref=rows(*block) if src is None else src, dst_ref=rows(*block),
                                                send_sem=send_sems.at[k], recv_sem=recv_sems.at[k], device_id=to, device_id_type=MESH)

        mine = pltpu.make_async_copy(x_ref, rows(*me), local_sem)
        mine.start()
        first = [copy(0, me, sibling, src=x_ref)]
        first += [copy(1 + j, me, (*chip, c), src=x_ref) for j, chip in enumerate(chips)]
        for cp in first:
            cp.start()
        passed = [copy(4 + j, (*chip, c), sibling) for j, chip in enumerate(chips)]
        for j, chip in enumerate(chips):
            copy(1 + j, (*chip, c), me).wait_recv()
            passed[j].start()
        copy(0, sibling, me).wait_recv()
        for j, chip in enumerate(chips):
            copy(4 + j, (*chip, 1 - c), me).wait_recv()
        for cp in first + passed:
            cp.wait_send()
        mine.wait()

    return pl.pallas_call(
        body, name=name, out_shape=jax.ShapeDtypeStruct((N_DEV * m_per, n), v.dtype),
        in_specs=[pl.BlockSpec(memory_space=pltpu.VMEM)], out_specs=pl.BlockSpec(memory_space=pltpu.VMEM),
        scratch_shapes=[pltpu.SemaphoreType.DMA((7,)), pltpu.SemaphoreType.DMA((7,)), pltpu.SemaphoreType.DMA],
        compiler_params=pltpu.CompilerParams(vmem_limit_bytes=VMEM_LIMIT),
    )(v)


SUM_BLOCK_ELEMS = 512 * 1024


def sum_slabs(name, b, into_half=False):
    k, l, h, w = b.shape
    tile = max(t for t in range(8, h + 1, 8) if h % t == 0 and (t % 16 == 0 or t == h) and (t * w <= SUM_BLOCK_ELEMS or t <= 16))
    nt = h // tile
    c = lax.axis_index("c").astype(jnp.int32).reshape(1) if into_half else jnp.zeros((1,), jnp.int32)

    def body(c_ref, b_ref, o_ref):
        acc = b_ref[0, 0].astype(F32)
        for i in range(1, k):
            acc = acc + b_ref[i, 0].astype(F32)
        o_ref[0] = acc

    return pl.pallas_call(
        body, name=name, out_shape=jax.ShapeDtypeStruct((l, 2 * h if into_half else h, w), F32),
        grid_spec=pltpu.PrefetchScalarGridSpec(
            num_scalar_prefetch=1, grid=(l, nt),
            in_specs=[pl.BlockSpec((k, 1, tile, w), lambda a, i, c_ref: (0, a, i, 0))],
            out_specs=pl.BlockSpec((1, tile, w), lambda a, i, c_ref: (a, c_ref[0] * nt + i, 0))),
        compiler_params=_params("parallel", "parallel"),
    )(c, b)


def _adam_block(w, g, m, v):
    m = ADAM_B1 * m + (1.0 - ADAM_B1) * g
    v = ADAM_B2 * v + (1.0 - ADAM_B2) * (g * g)
    m_hat = m / (1.0 - ADAM_B1 ** ADAM_STEP)
    v_hat = v / (1.0 - ADAM_B2 ** ADAM_STEP)
    return -ADAM_LR * (m_hat / (jnp.sqrt(v_hat) + ADAM_EPS) + ADAM_WD * w), m, v


def adamw(name, w, g, m, v):
    shape = w.shape
    cols = shape[-1]
    rows = w.size // cols
    tile = 128 if rows % 128 == 0 else rows
    flat = [a.reshape(rows, cols) for a in (w, g, m, v)]

    def body(w_ref, g_ref, m_ref, v_ref, d_ref, nm_ref, nv_ref):
        d_ref[...], nm_ref[...], nv_ref[...] = _adam_block(w_ref[...], g_ref[...], m_ref[...], v_ref[...])

    blk = pl.BlockSpec((tile, cols), lambda i: (i, 0))
    res = pl.pallas_call(
        body, name=name, grid=(rows // tile,), in_specs=[blk] * 4, out_specs=[blk] * 3,
        out_shape=[jax.ShapeDtypeStruct((rows, cols), F32)] * 3, compiler_params=_params("parallel"),
    )(*flat)
    return tuple(r.reshape(shape) for r in res)


def _pack_small(grads):
    flat = jnp.concatenate([grads[n].reshape(-1) for n in SMALL_NAMES + tuple(n for n, _ in CONVS)])
    n = flat.shape[0]
    total = -(-n // (8 * LANES)) * (8 * LANES)
    return jnp.pad(flat, (0, total - n)).reshape(-1, LANES)


def _unpack_small(v, shapes):
    flat = v.reshape(-1)
    out, off = {}, 0
    for n in SMALL_NAMES + tuple(n for n, _ in CONVS):
        sz = int(np.prod(shapes[n]))
        out[n] = flat[off:off + sz].reshape(shapes[n])
        off += sz
    return out


def kernel(x, ffn1_norm, ffn1_w_gate, ffn1_w_up, ffn1_w_down, mix_norm, w_in, rg_conv_w, rg_conv_b, rg_w_r, rg_b_r, rg_w_i, rg_b_i, rg_lambda, att_q_norm, att_k_norm, dn_conv_w, dn_a_log, dn_dt_bias, dn_out_norm, w_branch, w_out, ffn2_norm, ffn2_w_gate, ffn2_w_up, ffn2_w_down, loss_target, m_ffn1_norm, m_ffn1_w_gate, m_ffn1_w_up, m_ffn1_w_down, m_mix_norm, m_w_in, m_rg_conv_w, m_rg_conv_b, m_rg_w_r, m_rg_b_r, m_rg_w_i, m_rg_b_i, m_rg_lambda, m_att_q_norm, m_att_k_norm, m_dn_conv_w, m_dn_a_log, m_dn_dt_bias, m_dn_out_norm, m_w_branch, m_w_out, m_ffn2_norm, m_ffn2_w_gate, m_ffn2_w_up, m_ffn2_w_down, v_ffn1_norm, v_ffn1_w_gate, v_ffn1_w_up, v_ffn1_w_down, v_mix_norm, v_w_in, v_rg_conv_w, v_rg_conv_b, v_rg_w_r, v_rg_b_r, v_rg_w_i, v_rg_b_i, v_rg_lambda, v_att_q_norm, v_att_k_norm, v_dn_conv_w, v_dn_a_log, v_dn_dt_bias, v_dn_out_norm, v_w_branch, v_w_out, v_ffn2_norm, v_ffn2_w_gate, v_ffn2_w_up, v_ffn2_w_down):
    given = dict(locals())
    small = {n: given[n] for n in SMALL_NAMES}
    n_layers = ffn1_norm.shape[0]
    mat_names = [n for n, _ in MATRICES]
    conv_names = [n for n, _ in CONVS]

    gathered = dict(zip(mat_names, allgather_mats([cast_into_block("cast_" + n, given[n]) for n in mat_names])))
    taps = jnp.concatenate([given[n].reshape(-1) for n in conv_names]).reshape(-1, LANES)
    taps = allgather_small("allgather_taps", taps).reshape(N_CHIPS, 2, -1)[:, 0]
    conv, off = {}, 0
    for n, ax in CONVS:
        sz = given[n].size
        conv[n] = _shard_minor(taps[:, off:off + sz].reshape((N_CHIPS,) + given[n].shape), ax)
        off += sz
    w = [layer_weights(gathered, conv, l) for l in range(n_layers)]
    p = [layer_small(small, l) for l in range(n_layers)]

    loss, (gw, gp, gx) = jax.value_and_grad(local_loss, argnums=(0, 1, 2))(w, p, x[0], loss_target[0])
    loss = lax.psum(loss, ("x", "y", "c"))
    g_mats, g_conv = layer_weight_grads(gw)

    pieces = exchange_grad_pieces([g_mats[n] for n in mat_names])
    halves = [sum_slabs("sum_" + n, b, into_half=True) for n, b in zip(mat_names, pieces)]
    grads = dict(zip(mat_names, sibling_share_halves(halves)))

    g_small = dict(layer_small_grads(gp, small), **g_conv)
    packed_small = _pack_small(g_small)
    slabs = allgather_small("allgather_small", packed_small).reshape(N_DEV, 1, packed_small.shape[0], LANES)
    summed = _unpack_small(sum_slabs("sum_small", slabs)[0], {n: g.shape for n, g in g_small.items()})
    chip = 2 * lax.axis_index("x") + lax.axis_index("y")
    for n in SMALL_NAMES:
        grads[n] = summed[n]
    for n, ax in CONVS:
        s = given[n].shape[ax]
        grads[n] = lax.dynamic_slice_in_dim(summed[n], chip * s, s, axis=ax)

    upd = {n: adamw("adamw_" + n, given[n], grads[n], given["m_" + n], given["v_" + n]) for n in WEIGHT_NAMES}
    return (loss, gx[None], *[grads[n] for n in WEIGHT_NAMES], *[upd[n][0] for n in WEIGHT_NAMES],
            *[upd[n][1] for n in WEIGHT_NAMES], *[upd[n][2] for n in WEIGHT_NAMES])
```

```python
import functools
import math

import jax
import jax.numpy as jnp
import numpy as np
from jax import lax
from jax.experimental import pallas as pl
from jax.experimental.pallas import tpu as pltpu
from jax.experimental.pallas import tpu_sc as plsc

F32 = jnp.float32
BF16 = jnp.bfloat16
MESH = pl.DeviceIdType.MESH

D_MODEL = 1024
FFN_DIM = 2816
RG_C = 8.0
ATT_GROUPS = ((128, 1), (512, 4), (2048, 16))
ATT_HEADS = 12
ATT_HEAD_DIM = 64
ATT_SPAN = 128
DN_HEADS = 8
DN_HEAD_DIM = 128
DN_CHUNK = 64
EPS = 1e-6
NEG_INF = -1e30
N_CHIPS = 4
N_DEV = 8

ADAM_LR, ADAM_B1, ADAM_B2, ADAM_EPS, ADAM_WD, ADAM_STEP = 0.001, 0.9, 0.999, 1e-08, 0.01, 10

LANES = 128
VMEM_LIMIT = 56 * 1024 * 1024


def _params(*sem):
    return pltpu.CompilerParams(dimension_semantics=sem or None, vmem_limit_bytes=VMEM_LIMIT)


def _sigmoid(x):
    return 1.0 / (1.0 + jnp.exp(-x))


def _silu(x):
    return x * _sigmoid(x)


def _softplus(x):
    return jnp.maximum(x, 0.0) + jnp.log(1.0 + jnp.exp(-jnp.abs(x)))


def _gelu(x):
    return 0.5 * x * (1.0 + jnp.tanh(math.sqrt(2.0 / math.pi) * (x + 0.044715 * (x * x * x))))


def _neg_expm1(x):
    series = -x * (1.0 + x * (0.5 + x * (1.0 / 6 + x * (1.0 / 24 + x * (1.0 / 120 + x * (1.0 / 720))))))
    return jnp.where(x > -0.25, series, 1.0 - jnp.exp(x))


def _rms(x, g):
    return x * lax.rsqrt(jnp.mean(x * x, axis=-1, keepdims=True) + EPS) * g


_MM_DIMS = {"nn": (((1,), (0,)), ((), ())), "nt": (((1,), (1,)), ((), ())), "tn": (((0,), (0,)), ((), ()))}


def _split(a):
    hi = a.astype(BF16)
    return hi, (a - hi.astype(F32)).astype(BF16)


def _mxu(a, b, form, passes):
    (ca, cb), _ = _MM_DIMS[form]
    if a.ndim == 3:
        dims = (((ca[0] + 1,), (cb[0] + 1,)), ((0,), (0,)))
    else:
        dims = _MM_DIMS[form]
    dg = lambda p, q: lax.dot_general(p, q, dims, preferred_element_type=F32)
    if passes == 1:
        return dg(a.astype(BF16), b.astype(BF16))
    (a_hi, a_lo), (b_hi, b_lo) = _split(a), _split(b)
    return dg(a_hi, b_hi) + (dg(a_hi, b_lo) + dg(a_lo, b_hi))


@functools.partial(jax.custom_vjp, nondiff_argnums=(2, 3))
def _mm(a, b, form, passes):
    return _mxu(a, b, form, passes)


def _mm_fwd(a, b, form, passes):
    return _mxu(a, b, form, passes), (a, b)


def _mm_bwd(form, passes, res, g):
    a, b = res
    if form == "nn":
        return _mm(g, b, "nt", passes), _mm(a, g, "tn", passes)
    if form == "nt":
        return _mm(g, b, "nn", passes), _mm(g, a, "tn", passes)
    return _mm(b, g, "nt", passes), _mm(a, g, "nn", passes)


_mm.defvjp(_mm_fwd, _mm_bwd)


def _dot(a, b):
    return _mm(a, b, "nn", 1)


def _dot_nt(a, b):
    return _mm(a, b, "nt", 1)


def _dot_tn(a, b):
    return _mm(a, b, "tn", 1)


def _dot3(a, b):
    return _mm(a, b, "nn", 3)


def _rows(shape):
    return lax.broadcasted_iota(jnp.int32, shape, len(shape) - 2)


def _roll_down(x, s, fill):
    return jnp.where(_rows(x.shape) >= s, pltpu.roll(x, s, x.ndim - 2), fill)


def _roll_up(x, s, fill):
    n = x.shape[-2]
    return jnp.where(_rows(x.shape) < n - s, pltpu.roll(x, n - s, x.ndim - 2), fill)


@functools.partial(jax.custom_vjp, nondiff_argnums=(1,))
def _shift(x, s):
    return _roll_down(x, s, 0.0)


def _shift_fwd(x, s):
    return _roll_down(x, s, 0.0), None


def _shift_bwd(s, _, g):
    return (_roll_up(g, s, 0.0),)


_shift.defvjp(_shift_fwd, _shift_bwd)


def _causal_conv(x, w):
    return w[0:1] * _shift(x, 3) + w[1:2] * _shift(x, 2) + w[2:3] * _shift(x, 1) + w[3:4] * x


@jax.custom_vjp
def _lin_scan(a, b):
    return _lin_scan_fwd(a, b)[0]


def _lin_scan_fwd(a, b):
    a0 = a
    s = 1
    while s < a.shape[0]:
        b = a * _roll_down(b, s, 0.0) + b
        a = a * _roll_down(a, s, 1.0)
        s *= 2
    return b, (a0, b)


def _lin_scan_bwd(res, g):
    a, h = res
    c = _roll_up(a, 1, 0.0)
    s = 1
    while s < a.shape[0]:
        g = c * _roll_up(g, s, 0.0) + g
        c = c * _roll_up(c, s, 1.0)
        s *= 2
    return g * _roll_down(h, 1, 0.0), g


_lin_scan.defvjp(_lin_scan_fwd, _lin_scan_bwd)


@jax.custom_vjp
def _cumsum_rows(x):
    s = 1
    while s < x.shape[-2]:
        x = x + _roll_down(x, s, 0.0)
        s *= 2
    return x


def _cumsum_rows_fwd(x):
    return _cumsum_rows(x), None


def _cumsum_rows_bwd(_, g):
    s = 1
    while s < g.shape[-2]:
        g = g + _roll_up(g, s, 0.0)
        s *= 2
    return (g,)


_cumsum_rows.defvjp(_cumsum_rows_fwd, _cumsum_rows_bwd)


def _row_tile(t):
    return 256 if t % 256 == 0 else t


def _rowwise_fwd_call(name, f, rows, pars, tile):
    t = rows[0].shape[0]
    outs = jax.eval_shape(f, *[jax.ShapeDtypeStruct((tile, r.shape[1]), F32) for r in rows],
                          *[jax.ShapeDtypeStruct(p.shape, F32) for p in pars])
    nr, npar = len(rows), len(pars)

    def body(*refs):
        ins = [r[...] for r in refs[:nr + npar]]
        res = f(*ins)
        for o_ref, o in zip(refs[nr + npar:], res):
            o_ref[...] = o.astype(o_ref.dtype)

    return pl.pallas_call(
        body, name=name, grid=(t // tile,),
        in_specs=[pl.BlockSpec((tile, r.shape[1]), lambda i: (i, 0)) for r in rows]
        + [pl.BlockSpec(p.shape, lambda i: (0, 0)) for p in pars],
        out_specs=[pl.BlockSpec((tile, o.shape[1]), lambda i: (i, 0)) for o in outs],
        out_shape=[jax.ShapeDtypeStruct((t, o.shape[1]), F32) for o in outs],
        compiler_params=_params("parallel"),
    )(*rows, *pars)


def _rowwise_bwd_call(name, f, rows, pars, cts, tile):
    t = rows[0].shape[0]
    nr, npar, nct = len(rows), len(pars), len(cts)

    def body(*refs):
        ins = [r[...] for r in refs[:nr + npar]]
        gs = tuple(r[...] for r in refs[nr + npar:nr + npar + nct])
        outs = refs[nr + npar + nct:]
        _, vjp = jax.vjp(f, *ins)
        d = vjp(gs)
        for o_ref, v in zip(outs[:nr], d[:nr]):
            o_ref[...] = v

        @pl.when(pl.program_id(0) == 0)
        def _():
            for o_ref in outs[nr:]:
                o_ref[...] = jnp.zeros_like(o_ref)

        for o_ref, v in zip(outs[nr:], d[nr:]):
            o_ref[...] += v

    res = pl.pallas_call(
        body, name=name, grid=(t // tile,),
        in_specs=[pl.BlockSpec((tile, r.shape[1]), lambda i: (i, 0)) for r in rows]
        + [pl.BlockSpec(p.shape, lambda i: (0, 0)) for p in pars]
        + [pl.BlockSpec((tile, c.shape[1]), lambda i: (i, 0)) for c in cts],
        out_specs=[pl.BlockSpec((tile, r.shape[1]), lambda i: (i, 0)) for r in rows]
        + [pl.BlockSpec(p.shape, lambda i: (0, 0)) for p in pars],
        out_shape=[jax.ShapeDtypeStruct(r.shape, F32) for r in rows]
        + [jax.ShapeDtypeStruct(p.shape, F32) for p in pars],
        compiler_params=_params("arbitrary"),
    )(*rows, *pars, *cts)
    return tuple(res[:nr]), tuple(res[nr:])


def rowwise(name, f, rows, pars=()):
    tile = _row_tile(rows[0].shape[0])

    @jax.custom_vjp
    def op(rows, pars):
        return tuple(_rowwise_fwd_call(name, f, rows, pars, tile))

    def op_fwd(rows, pars):
        return op(rows, pars), (rows, pars)

    def op_bwd(res, cts):
        return _rowwise_bwd_call(name + "_bwd", f, res[0], res[1], tuple(cts), tile)

    op.defvjp(op_fwd, op_bwd)
    return op(tuple(rows), tuple(pars))


MM_TM = 512


def _tile_of(n, cap):
    best = None
    for c in range(LANES, min(n, cap) + 1, LANES):
        if n % c == 0:
            best = c
    return best or n


def _mmc_fwd(name, h, w):
    m, k = h.shape
    j, _, n = w.shape
    tm, tn = MM_TM, _tile_of(n, 1408)

    def body(h_ref, w_ref, o_ref):
        o_ref[0] = _dot(h_ref[...], w_ref[0])

    return pl.pallas_call(
        body, name=name, grid=(m // tm, j, n // tn),
        in_specs=[pl.BlockSpec((tm, k), lambda i, b, c: (i, 0)), pl.BlockSpec((1, k, tn), lambda i, b, c: (b, 0, c))],
        out_specs=pl.BlockSpec((1, tm, tn), lambda i, b, c: (b, i, c)),
        out_shape=jax.ShapeDtypeStruct((j, m, n), F32),
        compiler_params=_params("parallel", "parallel", "parallel"),
    )(h, w)


def _mmc_dh(name, dy, w):
    j, m, n = dy.shape
    k = w.shape[1]
    tm, tn = MM_TM, _tile_of(n, 1408)

    def body(dy_ref, w_ref, o_ref):
        part = _dot_nt(dy_ref[0], w_ref[0])

        @pl.when((pl.program_id(1) == 0) & (pl.program_id(2) == 0))
        def _():
            o_ref[...] = part

        @pl.when((pl.program_id(1) > 0) | (pl.program_id(2) > 0))
        def _():
            o_ref[...] += part

    return pl.pallas_call(
        body, name=name, grid=(m // tm, j, n // tn),
        in_specs=[pl.BlockSpec((1, tm, tn), lambda i, b, c: (b, i, c)), pl.BlockSpec((1, k, tn), lambda i, b, c: (b, 0, c))],
        out_specs=pl.BlockSpec((tm, k), lambda i, b, c: (i, 0)),
        out_shape=jax.ShapeDtypeStruct((m, k), F32),
        compiler_params=_params("parallel", "arbitrary", "arbitrary"),
    )(dy, w)


def _mmc_dw(name, h, dy):
    m, k = h.shape
    j, _, n = dy.shape
    tk, tn = _tile_of(k, 512), _tile_of(n, 1152)

    def body(h_ref, dy_ref, o_ref):
        o_ref[0] = _dot_tn(h_ref[...], dy_ref[0]).astype(BF16)

    return pl.pallas_call(
        body, name=name, grid=(j, k // tk, n // tn),
        in_specs=[pl.BlockSpec((m, tk), lambda b, i, c: (0, i)), pl.BlockSpec((1, m, tn), lambda b, i, c: (b, 0, c))],
        out_specs=pl.BlockSpec((1, tk, tn), lambda b, i, c: (b, i, c)),
        out_shape=jax.ShapeDtypeStruct((j, k, n), BF16),
        compiler_params=_params("parallel", "parallel", "parallel"),
    )(h, dy)


def mm_cols(name, h, w):
    @jax.custom_vjp
    def op(h, w):
        return _mmc_fwd(name, h, w)

    def op_fwd(h, w):
        return op(h, w), (h, w)

    def op_bwd(res, dy):
        h, w = res
        return _mmc_dh(name + "_dh", dy, w), _mmc_dw(name + "_dw", h, dy)

    op.defvjp(op_fwd, op_bwd)
    return op(h, w)


def _mmr_fwd(name, a, w):
    j, m, n = a.shape
    nn = w.shape[2]
    tm, tn = MM_TM, _tile_of(nn, 1024)

    def body(a_ref, w_ref, o_ref):
        part = _dot(a_ref[0], w_ref[0])

        @pl.when(pl.program_id(2) == 0)
        def _():
            o_ref[...] = part

        @pl.when(pl.program_id(2) > 0)
        def _():
            o_ref[...] += part

    return pl.pallas_call(
        body, name=name, grid=(m // tm, nn // tn, j),
        in_specs=[pl.BlockSpec((1, tm, n), lambda i, c, b: (b, i, 0)), pl.BlockSpec((1, n, tn), lambda i, c, b: (b, 0, c))],
        out_specs=pl.BlockSpec((tm, tn), lambda i, c, b: (i, c)),
        out_shape=jax.ShapeDtypeStruct((m, nn), F32),
        compiler_params=_params("parallel", "parallel", "arbitrary"),
    )(a, w)


def _mmr_da(name, dy, w):
    m, nn = dy.shape
    j, n, _ = w.shape
    tm = MM_TM

    def body(dy_ref, w_ref, o_ref):
        o_ref[0] = _dot_nt(dy_ref[...], w_ref[0])

    return pl.pallas_call(
        body, name=name, grid=(m // tm, j),
        in_specs=[pl.BlockSpec((tm, nn), lambda i, b: (i, 0)), pl.BlockSpec((1, n, nn), lambda i, b: (b, 0, 0))],
        out_specs=pl.BlockSpec((1, tm, n), lambda i, b: (b, i, 0)),
        out_shape=jax.ShapeDtypeStruct((j, m, n), F32),
        compiler_params=_params("parallel", "parallel"),
    )(dy, w)


def _mmr_dw(name, a, dy):
    j, m, n = a.shape
    nn = dy.shape[1]
    tn = _tile_of(nn, 512)

    def body(a_ref, dy_ref, o_ref):
        o_ref[0] = _dot_tn(a_ref[0], dy_ref[...]).astype(BF16)

    return pl.pallas_call(
        body, name=name, grid=(j, nn // tn),
        in_specs=[pl.BlockSpec((1, m, n), lambda b, c: (b, 0, 0)), pl.BlockSpec((m, tn), lambda b, c: (0, c))],
        out_specs=pl.BlockSpec((1, n, tn), lambda b, c: (b, 0, c)),
        out_shape=jax.ShapeDtypeStruct((j, n, nn), BF16),
        compiler_params=_params("parallel", "parallel"),
    )(a, dy)


def mm_rows(name, a, w):
    @jax.custom_vjp
    def op(a, w):
        return _mmr_fwd(name, a, w)

    def op_fwd(a, w):
        return op(a, w), (a, w)

    def op_bwd(res, dy):
        a, w = res
        return _mmr_da(name + "_da", dy, w), _mmr_dw(name + "_dw", a, dy)

    op.defvjp(op_fwd, op_bwd)
    return op(a, w)


def _colwise_specs(cols, pars, par_block):
    t = cols[0].shape[0]
    specs = [pl.BlockSpec((t, LANES), lambda j: (0, j)) for _ in cols]
    for p, blk in zip(pars, par_block):
        if blk == "lane":
            specs.append(pl.BlockSpec((p.shape[0], LANES), lambda j: (0, j)))
        else:
            specs.append(pl.BlockSpec((1,) + p.shape[1:], lambda j: (j, 0, 0)))
    return specs


def _colwise_fwd_call(name, f, cols, pars, par_block, n_out):
    t, c = cols[0].shape
    nc, npar = len(cols), len(pars)

    def body(*refs):
        ins = [r[...] for r in refs[:nc]] + [r[...] if b == "lane" else r[0] for r, b in zip(refs[nc:nc + npar], par_block)]
        res = f(*ins)
        for o_ref, o in zip(refs[nc + npar:], res):
            o_ref[...] = o

    return pl.pallas_call(
        body, name=name, grid=(c // LANES,),
        in_specs=_colwise_specs(cols, pars, par_block),
        out_specs=[pl.BlockSpec((t, LANES), lambda j: (0, j)) for _ in range(n_out)],
        out_shape=[jax.ShapeDtypeStruct((t, c), F32) for _ in range(n_out)],
        compiler_params=_params("parallel"),
    )(*cols, *pars)


def _colwise_bwd_call(name, f, cols, pars, par_block, cts):
    t, c = cols[0].shape
    nc, npar, nct = len(cols), len(pars), len(cts)

    def body(*refs):
        ins = [r[...] for r in refs[:nc]] + [r[...] if b == "lane" else r[0] for r, b in zip(refs[nc:nc + npar], par_block)]
        gs = tuple(r[...] for r in refs[nc + npar:nc + npar + nct])
        outs = refs[nc + npar + nct:]
        _, vjp = jax.vjp(f, *ins)
        d = vjp(gs)
        for o_ref, v in zip(outs[:nc], d[:nc]):
            o_ref[...] = v
        for o_ref, v, b in zip(outs[nc:], d[nc:], par_block):
            if b == "lane":
                o_ref[...] = v
            else:
                o_ref[0] = v

    res = pl.pallas_call(
        body, name=name, grid=(c // LANES,),
        in_specs=_colwise_specs(cols, pars, par_block) + [pl.BlockSpec((t, LANES), lambda j: (0, j)) for _ in cts],
        out_specs=_colwise_specs(cols, pars, par_block),
        out_shape=[jax.ShapeDtypeStruct(v.shape, F32) for v in (*cols, *pars)],
        compiler_params=_params("parallel"),
    )(*cols, *pars, *cts)
    return tuple(res[:nc]), tuple(res[nc:])


def colwise(name, f, cols, pars, par_block, n_out):
    @jax.custom_vjp
    def op(cols, pars):
        return tuple(_colwise_fwd_call(name, f, cols, pars, par_block, n_out))

    def op_fwd(cols, pars):
        return op(cols, pars), (cols, pars)

    def op_bwd(res, cts):
        return _colwise_bwd_call(name + "_bwd", f, res[0], res[1], par_block, tuple(cts))

    op.defvjp(op_fwd, op_bwd)
    return op(tuple(cols), tuple(pars))


def _rg_block(x, gate, cw, cb, wr, br, wi, bi, lam):
    xa = _causal_conv(x, cw) + cb
    r = _sigmoid(_dot(xa, wr) + br)
    i = _sigmoid(_dot(xa, wi) + bi)
    log_a = -RG_C * r * _softplus(-lam)
    a = jnp.exp(log_a)
    b = jnp.sqrt(_neg_expm1(2.0 * log_a)) * (i * xa)
    return (_lin_scan(a, b) * _gelu(gate),)


def _dn_conv_block(mode):
    def f(x, cw):
        c = _silu(_causal_conv(x, cw))
        if mode == "v":
            return (c,)
        c = c * lax.rsqrt(jnp.sum(c * c, axis=-1, keepdims=True) + EPS)
        return (c * (DN_HEAD_DIM ** -0.5),) if mode == "q" else (c,)
    return f


def _block_diag(w):
    w = w.reshape(8, 2, 64, 64)
    z = jnp.zeros((8, 64, 64), w.dtype)
    top = jnp.concatenate([w[:, 0], z], axis=2)
    bot = jnp.concatenate([z, w[:, 1]], axis=2)
    return jnp.concatenate([top, bot], axis=1)


DN_HP = 8


def _dn_block(S, qw, kw, vw, gb, h0):
    hp, hd = S.shape[0], DN_HEAD_DIM
    heads = lambda a: jnp.concatenate([a[None, :, j * hd:(j + 1) * hd] for j in range(hp)], axis=0)
    lane = lax.broadcasted_iota(jnp.int32, gb.shape, 1)
    col = lambda i: jnp.sum(jnp.where(lane == i, gb, 0.0), axis=1, keepdims=True)[None]
    beta = jnp.concatenate([col(h0 + j) for j in range(hp)], axis=0)
    g = jnp.concatenate([col(h0 + j + DN_HEADS) for j in range(hp)], axis=0)
    s_new, o = _dn_step(S, heads(qw), heads(kw), heads(vw), beta, g)
    return s_new, jnp.concatenate([o[j:j + 1].reshape(o.shape[1:]) for j in range(hp)], axis=1)


def _dn_step(S, q, k, v, beta, g):
    c = DN_CHUNK
    ri = lax.broadcasted_iota(jnp.int32, (c, c), 0)
    ci = lax.broadcasted_iota(jnp.int32, (c, c), 1)
    incl, strict = ri >= ci, ri > ci
    eye = (ri == ci).astype(F32)
    gam = _cumsum_rows(g)
    gam_row = jnp.sum(jnp.where(ri <= ci, g, 0.0), axis=-2, keepdims=True)
    gam_last = jnp.sum(g, axis=-2, keepdims=True)
    decay = jnp.where(incl, jnp.exp(jnp.where(incl, gam - gam_row, 0.0)), 0.0)
    kb = k * beta
    vb = v * beta
    a = jnp.where(strict, _dot_nt(kb, k) * decay, 0.0)
    p = -a
    tinv = eye + p
    for _ in range(5):
        p = _dot3(p, p)
        tinv = tinv + _dot3(tinv, p)
    e_gam = jnp.exp(gam)
    u0 = _dot3(tinv, vb)
    wk = _dot3(tinv, kb * e_gam)
    qk = jnp.where(incl, _dot_nt(q, k) * decay, 0.0)
    q_dec = q * e_gam
    k_dec = k * jnp.exp(gam_last - gam)
    u = u0 - _dot(wk, S)
    o = _dot(q_dec, S) + _dot(qk, u)
    s_new = S * jnp.exp(gam_last) + _dot_tn(k_dec, u)
    return s_new, o


def _dn_fwd_call(q, k, v, gb):
    t, w = q.shape
    n, hp, hd, c = t // DN_CHUNK, DN_HP, DN_HEAD_DIM, DN_CHUNK

    def body(q_ref, k_ref, v_ref, gb_ref, o_ref, s0_ref, s_scr):
        @pl.when(pl.program_id(1) == 0)
        def _():
            s_scr[...] = jnp.zeros_like(s_scr)

        s_old = s_scr[...]
        s0_ref[:, 0] = s_old
        s_new, o = _dn_block(s_old, q_ref[...], k_ref[...], v_ref[...], gb_ref[...], pl.program_id(0) * hp)
        o_ref[...] = o
        s_scr[...] = s_new

    blk = pl.BlockSpec((c, hp * hd), lambda g, i: (i, g))
    return pl.pallas_call(
        body, name="dn_core", grid=(DN_HEADS // hp, n),
        in_specs=[blk, blk, blk, pl.BlockSpec((c, LANES), lambda g, i: (i, 0))],
        out_specs=[blk, pl.BlockSpec((hp, 1, hd, hd), lambda g, i: (g, i, 0, 0))],
        out_shape=[jax.ShapeDtypeStruct((t, w), F32), jax.ShapeDtypeStruct((DN_HEADS, n, hd, hd), F32)],
        scratch_shapes=[pltpu.VMEM((hp, hd, hd), F32)],
        compiler_params=_params("parallel", "arbitrary"),
    )(q, k, v, gb)


def _dn_bwd_call(q, k, v, gb, s0, do):
    t, w = q.shape
    n, hp, hd, c = t // DN_CHUNK, DN_HP, DN_HEAD_DIM, DN_CHUNK
    ng = DN_HEADS // hp

    def body(q_ref, k_ref, v_ref, gb_ref, s0_ref, do_ref, dq_ref, dk_ref, dv_ref, dgb_ref, ds_scr):
        @pl.when(pl.program_id(1) == 0)
        def _():
            ds_scr[...] = jnp.zeros_like(ds_scr)

        h0 = pl.program_id(0) * hp
        _, vjp = jax.vjp(lambda *a: _dn_block(*a, h0), s0_ref[:, 0], q_ref[...], k_ref[...], v_ref[...], gb_ref[...])
        ds, dq, dk, dv, dgb = vjp((ds_scr[...], do_ref[...]))
        ds_scr[...] = ds
        dq_ref[...], dk_ref[...], dv_ref[...] = dq, dk, dv
        dgb_ref[0] = dgb

    blk = pl.BlockSpec((c, hp * hd), lambda g, i: (n - 1 - i, g))
    res = pl.pallas_call(
        body, name="dn_core_bwd", grid=(ng, n),
        in_specs=[blk, blk, blk, pl.BlockSpec((c, LANES), lambda g, i: (n - 1 - i, 0)),
                  pl.BlockSpec((hp, 1, hd, hd), lambda g, i: (g, n - 1 - i, 0, 0)), blk],
        out_specs=[blk, blk, blk, pl.BlockSpec((1, c, LANES), lambda g, i: (g, n - 1 - i, 0))],
        out_shape=[jax.ShapeDtypeStruct((t, w), F32)] * 3 + [jax.ShapeDtypeStruct((ng, t, LANES), F32)],
        scratch_shapes=[pltpu.VMEM((hp, hd, hd), F32)],
        compiler_params=_params("parallel", "arbitrary"),
    )(q, k, v, gb, s0, do)
    return res[0], res[1], res[2], jnp.sum(res[3], axis=0)


@jax.custom_vjp
def dn_core(q, k, v, gb):
    return _dn_fwd_call(q, k, v, gb)[0]


def _dn_core_fwd(q, k, v, gb):
    o, s0 = _dn_fwd_call(q, k, v, gb)
    return o, (q, k, v, gb, s0)


def _dn_core_bwd(res, do):
    return _dn_bwd_call(*res, do)


dn_core.defvjp(_dn_core_fwd, _dn_core_bwd)


ATT_RB = 4


def _att_block(q, kp, kc, vp, vc, qn, kn, slope, has_prev, dil):
    s = ATT_SPAN
    qh = _rms(q, qn) * (ATT_HEAD_DIM ** -0.5)
    qi = lax.broadcasted_iota(jnp.int32, (s, s), 0)
    kj = lax.broadcasted_iota(jnp.int32, (s, s), 1)
    d_p = qi + s - kj
    d_c = qi - kj
    s_p = _dot_nt(qh, _rms(kp, kn)) - slope * (d_p * dil).astype(F32)
    s_c = _dot_nt(qh, _rms(kc, kn)) - slope * (d_c * dil).astype(F32)
    s_p = jnp.where((d_p <= s) & (has_prev > 0), s_p, NEG_INF)
    s_c = jnp.where(d_c >= 0, s_c, NEG_INF)
    m = lax.stop_gradient(jnp.maximum(jnp.max(s_p, axis=-1, keepdims=True), jnp.max(s_c, axis=-1, keepdims=True)))
    p_p = jnp.exp(s_p - m)
    p_c = jnp.exp(s_c - m)
    den = jnp.sum(p_p, axis=-1, keepdims=True) + jnp.sum(p_c, axis=-1, keepdims=True)
    o = _dot(p_p / den, vp) + _dot(p_c / den, vc)
    lse = m + jnp.log(den)
    return o, jnp.broadcast_to(lse, o.shape)


def _att_specs(dil):
    rb = ATT_RB
    hb = max(1, rb // dil)
    cur = pl.BlockSpec((rb, ATT_SPAN, ATT_HEAD_DIM), lambda i, n: (i, n, 0))
    prev = pl.BlockSpec((rb, ATT_SPAN, ATT_HEAD_DIM), lambda i, n: (i, jnp.maximum(n - 1, 0), 0))
    gain = pl.BlockSpec((hb, 1, ATT_HEAD_DIM), lambda i, n: ((i * rb) // (dil * hb), 0, 0))
    return cur, prev, gain


def _att_slope(group, dil):
    first = pl.program_id(0) * ATT_RB
    if dil == 1:
        head = first + lax.broadcasted_iota(jnp.int32, (ATT_RB, 1, 1), 0)
    else:
        head = jnp.full((1, 1, 1), first // dil, jnp.int32)
    return jnp.exp((head + (4 * group + 1)).astype(F32) * (-8.0 / ATT_HEADS * math.log(2.0)))


def _att_fwd_call(name, group, dil, q, k, v, qn, kn):
    r, l, e = q.shape
    nblk = l // ATT_SPAN
    cur, prev, gain = _att_specs(dil)

    def body(q_ref, kp_ref, kc_ref, vp_ref, vc_ref, qn_ref, kn_ref, o_ref, lse_ref):
        o_ref[...], lse_ref[...] = _att_block(q_ref[...], kp_ref[...], kc_ref[...], vp_ref[...], vc_ref[...], qn_ref[...],
                                              kn_ref[...], _att_slope(group, dil), pl.program_id(1), dil)

    return pl.pallas_call(
        body, name=name, grid=(r // ATT_RB, nblk),
        in_specs=[cur, prev, cur, prev, cur, gain, gain], out_specs=[cur, cur],
        out_shape=[jax.ShapeDtypeStruct(q.shape, F32)] * 2,
        compiler_params=_params("parallel", "arbitrary"),
    )(q, k, k, v, v, qn, kn)


def _att_bwd_call(name, group, dil, q, k, v, qn, kn, do, dlse):
    r, l, e = q.shape
    nblk = l // ATT_SPAN
    cur, prev, gain = _att_specs(dil)
    rows_per_gain = dil * max(1, ATT_RB // dil)

    def body(q_ref, kp_ref, kc_ref, vp_ref, vc_ref, qn_ref, kn_ref, do_ref, dlse_ref,
             dq_ref, dkp_ref, dkc_ref, dvp_ref, dvc_ref, dqn_ref, dkn_ref):
        slope, has_prev = _att_slope(group, dil), pl.program_id(1)
        _, vjp = jax.vjp(lambda *a: _att_block(*a, slope, has_prev, dil), q_ref[...], kp_ref[...], kc_ref[...], vp_ref[...],
                         vc_ref[...], qn_ref[...], kn_ref[...])
        dq, dkp, dkc, dvp, dvc, dqn, dkn = vjp((do_ref[...], dlse_ref[...]))
        dq_ref[...], dkp_ref[...], dkc_ref[...], dvp_ref[...], dvc_ref[...] = dq, dkp, dkc, dvp, dvc

        @pl.when(((pl.program_id(0) * ATT_RB) % rows_per_gain == 0) & (pl.program_id(1) == 0))
        def _():
            dqn_ref[...] = jnp.zeros_like(dqn_ref)
            dkn_ref[...] = jnp.zeros_like(dkn_ref)

        dqn_ref[...] += dqn
        dkn_ref[...] += dkn

    res = pl.pallas_call(
        body, name=name + "_bwd", grid=(r // ATT_RB, nblk),
        in_specs=[cur, prev, cur, prev, cur, gain, gain, cur, cur],
        out_specs=[cur] * 5 + [gain, gain],
        out_shape=[jax.ShapeDtypeStruct(q.shape, F32)] * 5 + [jax.ShapeDtypeStruct(qn.shape, F32)] * 2,
        compiler_params=_params("arbitrary", "arbitrary"),
    )(q, k, k, v, v, qn, kn, do, dlse)
    dq, dkp, dkc, dvp, dvc, dqn, dkn = res
    back = lambda g: jnp.pad(g[:, ATT_SPAN:], ((0, 0), (0, ATT_SPAN), (0, 0)))
    return dq, dkc + back(dkp), dvc + back(dvp), dqn, dkn


def att_group(name, group, dil, q, k, v, qn, kn):
    @jax.custom_vjp
    def op(q, k, v, qn, kn):
        return tuple(_att_fwd_call(name, group, dil, q, k, v, qn, kn))

    def op_fwd(q, k, v, qn, kn):
        return op(q, k, v, qn, kn), (q, k, v, qn, kn)

    def op_bwd(res, cts):
        return _att_bwd_call(name, group, dil, *res, *cts)

    op.defvjp(op_fwd, op_bwd)
    return op(q, k, v, qn, kn)


def _att_mix(o1, o2, o3, l1, l2, l3):
    m = jnp.maximum(jnp.maximum(l1, l2), l3)
    e1, e2, e3 = jnp.exp(l1 - m), jnp.exp(l2 - m), jnp.exp(l3 - m)
    s = e1 + e2 + e3
    return (jnp.concatenate([o1 * (e1 / s), o2 * (e2 / s), o3 * (e3 / s)], axis=1),)


def att_branch(name, pa, qn, kn, groups=ATT_GROUPS):
    t = pa.shape[0]
    hg, e = 4, ATT_HEAD_DIM
    q, k, v = (pa[:, i * 768:(i + 1) * 768].reshape(t, ATT_HEADS, e) for i in range(3))
    outs, lses = [], []
    for g, (window, dil) in enumerate(groups):
        assert window // dil == ATT_SPAN
        l = t // dil
        to_r = lambda a: a[:, hg * g:hg * (g + 1)].reshape(l, dil, hg, e).transpose(2, 1, 0, 3).reshape(hg * dil, l, e)
        back = lambda a: a.reshape(hg, dil, l, e).transpose(2, 1, 0, 3).reshape(t, hg * e)
        o, lse = att_group(f"{name}_att{g}", g, dil, to_r(q), to_r(k), to_r(v),
                           qn[hg * g:hg * (g + 1)].reshape(hg, 1, e), kn[hg * g:hg * (g + 1)].reshape(hg, 1, e))
        outs.append(back(o))
        lses.append(back(lse))
    return rowwise(f"{name}_attmix", _att_mix, outs + lses)[0]


def dn_gates(name, ba, a_log, dt_bias):
    place = lambda p: jnp.pad(p.reshape(1, DN_HEADS), ((0, 0), (DN_HEADS, LANES - 2 * DN_HEADS)))

    def f(x, al, dt):
        lane = lax.broadcasted_iota(jnp.int32, x.shape, 1)
        return (jnp.where(lane < DN_HEADS, _sigmoid(x), -jnp.exp(al) * _softplus(x + dt)),)

    return rowwise(name, f, (ba,), (place(a_log), place(dt_bias)))[0]


def _dn_out(o, z, g):
    parts = []
    for h in range(DN_HEADS):
        sl = slice(h * DN_HEAD_DIM, (h + 1) * DN_HEAD_DIM)
        parts.append(_rms(o[:, sl], g[:, sl]) * _silu(z[:, sl]))
    return (jnp.concatenate(parts, axis=1),)


def _merge(ml, za, zb, zc):
    d = D_MODEL
    return (_sigmoid(ml[:, :d]) * za + _sigmoid(ml[:, d:2 * d]) * zb + _sigmoid(ml[:, 2 * d:]) * zc,)


def _swiglu_act(g, u):
    return (_silu(g) * u,)


def add_norm(name, x, pend, scale, gain):
    if pend is None:
        return x, rowwise(name, lambda a, g: (_rms(a, g),), (x,), (gain,))[0]

    def f(a, b, g):
        s = a + scale * b
        return s, _rms(s, g)

    return rowwise(name, f, (x, pend), (gain,))


def ffn(name, h, wg, wu, wd):
    g, u = mm_cols(name + "_g", h, wg), mm_cols(name + "_u", h, wu)
    j, t, n = g.shape
    a = rowwise(name + "_act", _swiglu_act, (g.reshape(j * t, n), u.reshape(j * t, n)))[0]
    return mm_rows(name + "_d", a.reshape(j, t, n), wd)


W_IN_PIECES = (("rgx", 0, 1024), ("gate", 1024, 1024), ("att", 2048, 2304), ("dq", 4352, 1024), ("dk", 5376, 1024),
               ("dv", 6400, 1024), ("dz", 7424, 1024), ("ba", 8448, 16), ("mrg", 8464, 3072))
RG_PAR_BLOCKS = ("lane", "lane", "blk", "lane", "blk", "lane", "lane")


def mixer(name, u, w, p):
    mm = lambda nm, a, wt: mm_rows(nm, a[None], wt[None])
    pr = {k: mm_cols(f"{name}_in_{k}", u, w["in_" + k][None])[0] for k, _, _ in W_IN_PIECES}
    ya = colwise(name + "_rg", _rg_block, (pr["rgx"], pr["gate"]),
                 (w["rg_conv_w"], p["rg_conv_b"], _block_diag(p["rg_w_r"]), p["rg_b_r"], _block_diag(p["rg_w_i"]),
                  p["rg_b_i"], p["rg_lambda"]), RG_PAR_BLOCKS, 1)[0]
    yb = att_branch(name, pr["att"], p["att_q_norm"], p["att_k_norm"])
    cw = w["dn_conv_w"]
    cq = colwise(name + "_dnq", _dn_conv_block("q"), (pr["dq"],), (cw[:, :1024],), ("lane",), 1)[0]
    ck = colwise(name + "_dnk", _dn_conv_block("k"), (pr["dk"],), (cw[:, 1024:2048],), ("lane",), 1)[0]
    cv = colwise(name + "_dnv", _dn_conv_block("v"), (pr["dv"],), (cw[:, 2048:],), ("lane",), 1)[0]
    gb = dn_gates(name + "_dngate", pr["ba"], p["dn_a_log"], p["dn_dt_bias"])
    o_dn = dn_core(cq, ck, cv, gb)
    yc = rowwise(name + "_dnout", _dn_out, (o_dn, pr["dz"]), (p["dn_out_norm"].reshape(1, D_MODEL),))[0]
    y = rowwise(name + "_merge", _merge, (pr["mrg"], mm(name + "_ba", ya, w["br_a"]), mm(name + "_bb", yb, w["br_b"]),
                                          mm(name + "_bc", yc, w["br_c"])))[0]
    return mm(name + "_out", y, w["w_out"])


def _loss_call(x, pend, target):
    t, d = x.shape
    tile = _row_tile(t)

    def body(x_ref, p_ref, t_ref, loss_ref, g_ref):
        err = x_ref[...] + 0.5 * p_ref[...] - t_ref[...]
        g_ref[...] = err * (1.0 / d)

        @pl.when(pl.program_id(0) == 0)
        def _():
            loss_ref[...] = jnp.zeros_like(loss_ref)

        loss_ref[...] += jnp.full(loss_ref.shape, 0.5 / d, F32) * jnp.sum(err * err)

    blk = pl.BlockSpec((tile, d), lambda i: (i, 0))
    loss, g = pl.pallas_call(
        body, name="loss", grid=(t // tile,), in_specs=[blk, blk, blk],
        out_specs=[pl.BlockSpec((8, LANES), lambda i: (0, 0)), blk],
        out_shape=[jax.ShapeDtypeStruct((8, LANES), F32), jax.ShapeDtypeStruct((t, d), F32)],
        compiler_params=_params("arbitrary"),
    )(x, pend, target)
    return loss[0, 0], g


@jax.custom_vjp
def loss_op(x, pend, target):
    return _loss_call(x, pend, target)[0]


def _loss_fwd(x, pend, target):
    loss, g = _loss_call(x, pend, target)
    return loss, g


def _loss_bwd(g, ct):
    return ct * g, (0.5 * ct) * g, None


loss_op.defvjp(_loss_fwd, _loss_bwd)


def local_loss(w, p, x, target):
    pend, scale = None, 0.0
    for l in range(len(w)):
        n = f"L{l}"
        x, h = add_norm(n + "_n1", x, pend, scale, p[l]["ffn1_norm"])
        pend, scale = ffn(n + "_f1", h, w[l]["ffn1_w_gate"], w[l]["ffn1_w_up"], w[l]["ffn1_w_down"]), 0.5
        x, h = add_norm(n + "_nm", x, pend, scale, p[l]["mix_norm"])
        pend, scale = mixer(n + "_mx", h, w[l], p[l]), 1.0
        x, h = add_norm(n + "_n2", x, pend, scale, p[l]["ffn2_norm"])
        pend, scale = ffn(n + "_f2", h, w[l]["ffn2_w_gate"], w[l]["ffn2_w_up"], w[l]["ffn2_w_down"]), 0.5
    return loss_op(x, pend, target)


WEIGHT_NAMES = ("ffn1_norm", "ffn1_w_gate", "ffn1_w_up", "ffn1_w_down", "mix_norm", "w_in", "rg_conv_w", "rg_conv_b",
                "rg_w_r", "rg_b_r", "rg_w_i", "rg_b_i", "rg_lambda", "att_q_norm", "att_k_norm", "dn_conv_w", "dn_a_log",
                "dn_dt_bias", "dn_out_norm", "w_branch", "w_out", "ffn2_norm", "ffn2_w_gate", "ffn2_w_up", "ffn2_w_down")
MATRICES = (("ffn1_w_gate", 2), ("ffn1_w_up", 2), ("ffn1_w_down", 1), ("w_in", 2), ("w_branch", 1), ("w_out", 1),
            ("ffn2_w_gate", 2), ("ffn2_w_up", 2), ("ffn2_w_down", 1))
CONVS = (("rg_conv_w", 2), ("dn_conv_w", 2))
SHARD_AXIS = dict(MATRICES + CONVS)
SMALL_NAMES = tuple(n for n in WEIGHT_NAMES if n not in SHARD_AXIS)
ROW_PARAMS = ("ffn1_norm", "mix_norm", "rg_conv_b", "rg_b_r", "rg_b_i", "rg_lambda", "ffn2_norm")
FFN_MATS = ("ffn1_w_gate", "ffn1_w_up", "ffn1_w_down", "ffn2_w_gate", "ffn2_w_up", "ffn2_w_down")
W_IN_SHARD = 2884
FIRST_NEEDED = (("ffn1_w_gate", 0), ("ffn1_w_up", 0), ("ffn1_w_down", 0), ("w_in", 0))
LATE_MATS = ("ffn2_w_gate", "ffn2_w_up", "ffn2_w_down", "w_out", "w_branch")
EXCHANGE_GROUPS = (lambda n, l: l == 1 and n in LATE_MATS,
                   lambda n, l: (l == 1) != (n in LATE_MATS),
                   lambda n, l: l == 0 and n not in LATE_MATS)


def _shard_minor(a, axis):
    a = jnp.moveaxis(a, 0, axis)
    return a.reshape(a.shape[:axis] + (N_CHIPS * a.shape[axis + 1],) + a.shape[axis + 2:])


def _w_in_piece(g, off, n):
    s = W_IN_SHARD
    parts = [g[j][:, max(off, j * s) - j * s:min(off + n, (j + 1) * s) - j * s]
             for j in range(N_CHIPS) if max(off, j * s) < min(off + n, (j + 1) * s)]
    return jnp.concatenate(parts, axis=1) if len(parts) > 1 else parts[0]


def _w_in_chip_grad(gl, j):
    s = W_IN_SHARD
    parts = [gl["in_" + k][:, max(off, j * s) - off:min(off + n, (j + 1) * s) - off]
             for k, off, n in W_IN_PIECES if max(off, j * s) < min(off + n, (j + 1) * s)]
    return jnp.concatenate(parts, axis=1)


def layer_weights(g, conv, l):
    w = {n: g[n, l] for n in FFN_MATS}
    w["w_out"] = g["w_out", l].reshape(D_MODEL, D_MODEL)
    for k, off, n in W_IN_PIECES:
        piece = _w_in_piece(g["w_in", l], off, n)
        w["in_" + k] = jnp.pad(piece, ((0, 0), (0, LANES - n))) if n < LANES else piece
    wb = g["w_branch", l].reshape(-1, D_MODEL)
    w["br_a"], w["br_b"], w["br_c"] = wb[:1024], wb[1024:1792], wb[1792:]
    w["rg_conv_w"], w["dn_conv_w"] = conv["rg_conv_w"][l], conv["dn_conv_w"][l]
    return w


def layer_weight_grads(gw):
    out = {}
    for l, gl in enumerate(gw):
        for n in FFN_MATS:
            out[n, l] = gl[n]
        out["w_out", l] = gl["w_out"].reshape(N_CHIPS, -1, D_MODEL)
        out["w_branch", l] = jnp.concatenate([gl["br_a"], gl["br_b"], gl["br_c"]], axis=0).reshape(N_CHIPS, -1, D_MODEL)
        out["w_in", l] = jnp.stack([_w_in_chip_grad(gl, j) for j in range(N_CHIPS)])
    conv = {n: jnp.stack([gl[n] for gl in gw]) for n, _ in CONVS}
    return out, conv


def layer_small(small, l):
    p = {n: small[n][l] for n in SMALL_NAMES}
    for n in ROW_PARAMS:
        p[n] = small[n][l:l + 1]
    return p


def layer_small_grads(gp, small):
    return {n: jnp.stack([g[n] for g in gp]).reshape(small[n].shape) for n in SMALL_NAMES}


HBM_SPEC = pl.BlockSpec(memory_space=pl.ANY)


def _place():
    x, y, c = lax.axis_index("x"), lax.axis_index("y"), lax.axis_index("c")
    other_chips = [(1 - x, y), (x, 1 - y), (1 - x, 1 - y)]
    return x, y, c, 2 * x + y, (x, y, 1 - c), other_chips


def _half_rows(ref, lead, hc):
    hr = ref.shape[-2] // 2
    return ref.at[(*lead, pl.ds(pl.multiple_of(hc * hr, 16), hr), slice(None))]


def _chip_index():
    return (2 * lax.axis_index("x") + lax.axis_index("y")).astype(jnp.int32).reshape(1)


def cast_into_blocks(name, w):
    l, rows, cols = w.shape
    tr = rows // 2

    def body(me_ref, w_ref, *o_refs):
        for a, o_ref in enumerate(o_refs):
            o_ref[...] = w_ref[a:a + 1].astype(BF16)

    return pl.pallas_call(
        body, name=name, out_shape=[jax.ShapeDtypeStruct((N_CHIPS, rows, cols), BF16)] * l,
        grid_spec=pltpu.PrefetchScalarGridSpec(
            num_scalar_prefetch=1, grid=(rows // tr,),
            in_specs=[pl.BlockSpec((l, tr, cols), lambda i, me: (0, i, 0))],
            out_specs=[pl.BlockSpec((1, tr, cols), lambda i, me: (me[0], i, 0))] * l),
        compiler_params=_params("parallel"),
    )(_chip_index(), w)


def _gather_blocks(bufs_in, bufs_out, send_sems, recv_sems):
    n = len(bufs_in)
    x, y, c, me, sibling, chips = _place()

    def copy(s, src, dst, to):
        return pltpu.make_async_remote_copy(src_ref=src, dst_ref=dst, send_sem=send_sems.at[s], recv_sem=recv_sems.at[s],
                                            device_id=to, device_id_type=MESH)

    first, passed = [], []
    for j, (cx, cy) in enumerate(chips):
        for i in range(n):
            cp = copy(6 * i + j, _half_rows(bufs_in[i], (me,), c), _half_rows(bufs_out[i], (me,), c), (cx, cy, c))
            cp.start()
            first.append(cp)
    for j, (cx, cy) in enumerate(chips):
        k = 2 * cx + cy
        for i in range(n):
            copy(6 * i + j, _half_rows(bufs_in[i], (me,), c), _half_rows(bufs_out[i], (k,), c), (cx, cy, c)).wait_recv()
            cp = copy(6 * i + 3 + j, _half_rows(bufs_out[i], (k,), c), _half_rows(bufs_out[i], (k,), c), sibling)
            cp.start()
            passed.append(cp)
    for j, (cx, cy) in enumerate(chips):
        k = 2 * cx + cy
        for i in range(n):
            copy(6 * i + 3 + j, _half_rows(bufs_in[i], (me,), c), _half_rows(bufs_out[i], (k,), 1 - c), sibling).wait_recv()
    for cp in first + passed:
        cp.wait_send()


def _handshake(peers):
    barrier = pltpu.get_barrier_semaphore()
    for p in peers:
        pl.semaphore_signal(barrier, inc=1, device_id=p, device_id_type=MESH)
    pl.semaphore_wait(barrier, len(peers))


def allgather_blocks_sc(name, bufs, collective_id):
    n = len(bufs)
    refs = [jax.new_ref(b, memory_space=pltpu.MemorySpace.HBM) for b in bufs]

    @pl.kernel(mesh=plsc.ScalarSubcoreMesh(axis_name="sequencer", num_cores=1), name=name,
               scratch_types=(pltpu.SemaphoreType.DMA((6 * n,)), pltpu.SemaphoreType.DMA((6 * n,))),
               compiler_params=pltpu.CompilerParams(collective_id=collective_id))
    def launch(send_sems, recv_sems):
        x, y, c, me, sibling, chips = _place()
        _handshake([(cx, cy, c) for cx, cy in chips] + [sibling])
        _gather_blocks(refs, refs, send_sems, recv_sems)

    launch()
    return [jax.freeze(r) for r in refs]


def allgather_mats(bufs):
    n = len(bufs)

    def body(*refs):
        _gather_blocks(refs[:n], refs[n:2 * n], *refs[2 * n:])

    return pl.pallas_call(
        body, name="allgather_mats", out_shape=[jax.ShapeDtypeStruct(b.shape, b.dtype) for b in bufs],
        in_specs=[HBM_SPEC] * n, out_specs=[HBM_SPEC] * n, input_output_aliases={i: i for i in range(n)},
        scratch_shapes=[pltpu.SemaphoreType.DMA((6 * n,)), pltpu.SemaphoreType.DMA((6 * n,))],
    )(*bufs)


PEER_FLIPS = tuple((fx, fy, fc) for fx in (0, 1) for fy in (0, 1) for fc in (0, 1))[1:]


def exchange_pieces_sc(name, gs, collective_id):
    n = len(gs)

    def body(*refs):
        ins, outs = refs[:n], refs[n:2 * n]
        send_sems, recv_sems = refs[2 * n:]
        x, y, c, me, sibling, chips = _place()
        my_dev = 4 * x + 2 * y + c
        flip = lambda v, f: 1 - v if f else v
        peers = [(flip(x, fx), flip(y, fy), flip(c, fc)) for fx, fy, fc in PEER_FLIPS]
        _handshake(peers)
        sends = []
        for r, (px, py, pc) in enumerate(peers):
            for i in range(n):
                cp = pltpu.make_async_remote_copy(
                    src_ref=_half_rows(ins[i], (2 * px + py,), pc), dst_ref=outs[i].at[my_dev], send_sem=send_sems.at[7 * i + r],
                    recv_sem=recv_sems.at[7 * i + r], device_id=(px, py, pc), device_id_type=MESH)
                cp.start()
                sends.append(cp)
        for r, (px, py, pc) in enumerate(peers):
            for i in range(n):
                pltpu.make_async_remote_copy(
                    src_ref=_half_rows(ins[i], (me,), c), dst_ref=outs[i].at[4 * px + 2 * py + pc], send_sem=send_sems.at[7 * i + r],
                    recv_sem=recv_sems.at[7 * i + r], device_id=(px, py, pc), device_id_type=MESH).wait_recv()
        for cp in sends:
            cp.wait_send()

    return pl.kernel(
        body, name=name, mesh=plsc.ScalarSubcoreMesh(axis_name="sequencer", num_cores=1),
        out_type=[jax.ShapeDtypeStruct((N_DEV, g.shape[1] // 2, g.shape[2]), g.dtype) for g in gs],
        scratch_types=[pltpu.SemaphoreType.DMA((7 * n,)), pltpu.SemaphoreType.DMA((7 * n,))],
        compiler_params=pltpu.CompilerParams(collective_id=collective_id),
    )(*gs)


def sibling_share_halves(fs):
    n = len(fs)
    every = (slice(None),)

    def body(*refs):
        ins, outs = refs[:n], refs[n:2 * n]
        send_sems, recv_sems = refs[2 * n:]
        x, y, c, me, sibling, chips = _place()
        sends = []
        for i in range(n):
            cp = pltpu.make_async_remote_copy(src_ref=_half_rows(ins[i], every, c), dst_ref=_half_rows(outs[i], every, c),
                                              send_sem=send_sems.at[i], recv_sem=recv_sems.at[i], device_id=sibling, device_id_type=MESH)
            cp.start()
            sends.append(cp)
        for i in range(n):
            pltpu.make_async_remote_copy(src_ref=_half_rows(ins[i], every, c), dst_ref=_half_rows(outs[i], every, 1 - c),
                                         send_sem=send_sems.at[i], recv_sem=recv_sems.at[i], device_id=sibling,
                                         device_id_type=MESH).wait_recv()
        for cp in sends:
            cp.wait_send()

    return pl.pallas_call(
        body, name="sibling_share_halves", out_shape=[jax.ShapeDtypeStruct(f.shape, f.dtype) for f in fs],
        in_specs=[HBM_SPEC] * n, out_specs=[HBM_SPEC] * n, input_output_aliases={i: i for i in range(n)},
        scratch_shapes=[pltpu.SemaphoreType.DMA((n,)), pltpu.SemaphoreType.DMA((n,))],
    )(*fs)


def allgather_small(name, v):
    m_per, n = v.shape

    def body(x_ref, out_ref, send_sems, recv_sems, local_sem):
        x, y, c, _, sibling, chips = _place()
        me = (x, y, c)

        def rows(px, py, pc):
            return out_ref.at[pl.ds((4 * px + 2 * py + pc) * m_per, m_per), :]

        def copy(k, block, to, src=None):
            return pltpu.make_async_remote_copy(src_ref=rows(*block) if src is None else src, dst_ref=rows(*block),
                                                send_sem=send_sems.at[k], recv_sem=recv_sems.at[k], device_id=to, device_id_type=MESH)

        mine = pltpu.make_async_copy(x_ref, rows(*me), local_sem)
        mine.start()
        first = [copy(0, me, sibling, src=x_ref)]
        first += [copy(1 + j, me, (*chip, c), src=x_ref) for j, chip in enumerate(chips)]
        for cp in first:
            cp.start()
        passed = [copy(4 + j, (*chip, c), sibling) for j, chip in enumerate(chips)]
        for j, chip in enumerate(chips):
            copy(1 + j, (*chip, c), me).wait_recv()
            passed[j].start()
        copy(0, sibling, me).wait_recv()
        for j, chip in enumerate(chips):
            copy(4 + j, (*chip, 1 - c), me).wait_recv()
        for cp in first + passed:
            cp.wait_send()
        mine.wait()

    return pl.pallas_call(
        body, name=name, out_shape=jax.ShapeDtypeStruct((N_DEV * m_per, n), v.dtype),
        in_specs=[pl.BlockSpec(memory_space=pltpu.VMEM)], out_specs=pl.BlockSpec(memory_space=pltpu.VMEM),
        scratch_shapes=[pltpu.SemaphoreType.DMA((7,)), pltpu.SemaphoreType.DMA((7,)), pltpu.SemaphoreType.DMA],
        compiler_params=pltpu.CompilerParams(vmem_limit_bytes=VMEM_LIMIT),
    )(v)


SUM_BLOCK_ELEMS = 512 * 1024


def sum_slabs(name, b):
    k, h, w = b.shape

    def body(b_ref, o_ref):
        acc = b_ref[0].astype(F32)
        for i in range(1, k):
            acc = acc + b_ref[i].astype(F32)
        o_ref[...] = acc

    return pl.pallas_call(
        body, name=name, out_shape=jax.ShapeDtypeStruct((h, w), F32),
        in_specs=[pl.BlockSpec(memory_space=pltpu.VMEM)], out_specs=pl.BlockSpec(memory_space=pltpu.VMEM),
        compiler_params=pltpu.CompilerParams(vmem_limit_bytes=VMEM_LIMIT),
    )(b)


def sum_pieces(name, pieces, gs):
    nl = len(pieces)
    k, h, w = pieces[0].shape
    tile = max(t for t in range(16, h + 1, 16) if h % t == 0 and (t * w <= SUM_BLOCK_ELEMS or t == 16))
    nt = h // tile
    x, y, c = lax.axis_index("x"), lax.axis_index("y"), lax.axis_index("c")
    place = [v.astype(jnp.int32).reshape(1) for v in (c, 2 * x + y, 4 * x + 2 * y + c)]

    assert nl == 2

    def tile_of(l, a, i):
        return i * a if l else i * (1 - a) + (nt - 1) * a

    def body(c_ref, me_ref, dev_ref, *refs):
        p_refs, g_refs, o_ref = refs[:nl], refs[nl:2 * nl], refs[2 * nl]
        my_dev = dev_ref[0]
        for l in range(nl):
            @pl.when(pl.program_id(0) == l)
            def _():
                o_ref[0] = jnp.zeros(o_ref.shape[1:], F32)
                for d in range(k):
                    @pl.when(my_dev == d)
                    def _():
                        o_ref[0] += g_refs[l][0].astype(F32)

                    @pl.when(my_dev != d)
                    def _():
                        o_ref[0] += p_refs[l][d].astype(F32)

    in_specs = [pl.BlockSpec((k, tile, w), functools.partial(lambda l, a, i, cc, me, dev: (0, tile_of(l, a, i), 0), l))
                for l in range(nl)]
    in_specs += [pl.BlockSpec((1, tile, w), functools.partial(lambda l, a, i, cc, me, dev: (me[0], cc[0] * nt + tile_of(l, a, i), 0), l))
                 for l in range(nl)]
    return pl.pallas_call(
        body, name=name, out_shape=jax.ShapeDtypeStruct((nl, 2 * h, w), F32),
        grid_spec=pltpu.PrefetchScalarGridSpec(
            num_scalar_prefetch=3, grid=(nl, nt), in_specs=in_specs,
            out_specs=pl.BlockSpec((1, tile, w), lambda a, i, cc, me, dev: (a, cc[0] * nt + i, 0))),
        compiler_params=_params("arbitrary", "arbitrary"),
    )(*place, *pieces, *gs)


def _adam_block(w, g, m, v):
    m = ADAM_B1 * m + (1.0 - ADAM_B1) * g
    v = ADAM_B2 * v + (1.0 - ADAM_B2) * (g * g)
    m_hat = m / (1.0 - ADAM_B1 ** ADAM_STEP)
    v_hat = v / (1.0 - ADAM_B2 ** ADAM_STEP)
    return -ADAM_LR * (m_hat / (jnp.sqrt(v_hat) + ADAM_EPS) + ADAM_WD * w), m, v


def adamw(name, w, g, m, v):
    shape = w.shape
    cols = shape[-1]
    rows = w.size // cols
    tile = 128 if rows % 128 == 0 else rows
    flat = [a.reshape(rows, cols) for a in (w, g, m, v)]

    def body(w_ref, g_ref, m_ref, v_ref, d_ref, nm_ref, nv_ref):
        d_ref[...], nm_ref[...], nv_ref[...] = _adam_block(w_ref[...], g_ref[...], m_ref[...], v_ref[...])

    blk = pl.BlockSpec((tile, cols), lambda i: (i, 0))
    res = pl.pallas_call(
        body, name=name, grid=(rows // tile,), in_specs=[blk] * 4, out_specs=[blk] * 3,
        out_shape=[jax.ShapeDtypeStruct((rows, cols), F32)] * 3, compiler_params=_params("parallel"),
    )(*flat)
    return tuple(r.reshape(shape) for r in res)


def _pack_small(grads):
    flat = jnp.concatenate([grads[n].reshape(-1) for n in SMALL_NAMES + tuple(n for n, _ in CONVS)])
    n = flat.shape[0]
    total = -(-n // (8 * LANES)) * (8 * LANES)
    return jnp.pad(flat, (0, total - n)).reshape(-1, LANES)


def _unpack_small(v, shapes):
    flat = v.reshape(-1)
    out, off = {}, 0
    for n in SMALL_NAMES + tuple(n for n, _ in CONVS):
        sz = int(np.prod(shapes[n]))
        out[n] = flat[off:off + sz].reshape(shapes[n])
        off += sz
    return out


def kernel(x, ffn1_norm, ffn1_w_gate, ffn1_w_up, ffn1_w_down, mix_norm, w_in, rg_conv_w, rg_conv_b, rg_w_r, rg_b_r, rg_w_i, rg_b_i, rg_lambda, att_q_norm, att_k_norm, dn_conv_w, dn_a_log, dn_dt_bias, dn_out_norm, w_branch, w_out, ffn2_norm, ffn2_w_gate, ffn2_w_up, ffn2_w_down, loss_target, m_ffn1_norm, m_ffn1_w_gate, m_ffn1_w_up, m_ffn1_w_down, m_mix_norm, m_w_in, m_rg_conv_w, m_rg_conv_b, m_rg_w_r, m_rg_b_r, m_rg_w_i, m_rg_b_i, m_rg_lambda, m_att_q_norm, m_att_k_norm, m_dn_conv_w, m_dn_a_log, m_dn_dt_bias, m_dn_out_norm, m_w_branch, m_w_out, m_ffn2_norm, m_ffn2_w_gate, m_ffn2_w_up, m_ffn2_w_down, v_ffn1_norm, v_ffn1_w_gate, v_ffn1_w_up, v_ffn1_w_down, v_mix_norm, v_w_in, v_rg_conv_w, v_rg_conv_b, v_rg_w_r, v_rg_b_r, v_rg_w_i, v_rg_b_i, v_rg_lambda, v_att_q_norm, v_att_k_norm, v_dn_conv_w, v_dn_a_log, v_dn_dt_bias, v_dn_out_norm, v_w_branch, v_w_out, v_ffn2_norm, v_ffn2_w_gate, v_ffn2_w_up, v_ffn2_w_down):
    given = dict(locals())
    small = {n: given[n] for n in SMALL_NAMES}
    n_layers = ffn1_norm.shape[0]
    mat_names = [n for n, _ in MATRICES]
    conv_names = [n for n, _ in CONVS]

    blocks = {}
    for n in mat_names:
        for l, b in enumerate(cast_into_blocks("cast_" + n, given[n])):
            blocks[n, l] = b
    first = [k for k in blocks if k in FIRST_NEEDED]
    later = [k for k in blocks if k not in FIRST_NEEDED]
    gathered = dict(zip(later, allgather_blocks_sc("allgather_later", [blocks[k] for k in later], 1)))
    gathered.update(zip(first, allgather_mats([blocks[k] for k in first])))
    taps = jnp.concatenate([given[n].reshape(-1) for n in conv_names]).reshape(-1, LANES)
    taps = allgather_small("allgather_taps", taps).reshape(N_CHIPS, 2, -1)[:, 0]
    conv, off = {}, 0
    for n, ax in CONVS:
        sz = given[n].size
        conv[n] = _shard_minor(taps[:, off:off + sz].reshape((N_CHIPS,) + given[n].shape), ax)
        off += sz
    w = [layer_weights(gathered, conv, l) for l in range(n_layers)]
    p = [layer_small(small, l) for l in range(n_layers)]

    loss, (gw, gp, gx) = jax.value_and_grad(local_loss, argnums=(0, 1, 2))(w, p, x[0], loss_target[0])
    loss = lax.psum(loss, ("x", "y", "c"))
    g_mats, g_conv = layer_weight_grads(gw)

    pieces = {}
    for i, group in enumerate(EXCHANGE_GROUPS):
        keys = [k for k in g_mats if group(*k)]
        pieces.update(zip(keys, exchange_pieces_sc(f"exchange_{i}", [g_mats[k] for k in keys], 2 + i)))
    halves = [sum_pieces("sum_" + n, [pieces[n, l] for l in range(n_layers)], [g_mats[n, l] for l in range(n_layers)])
              for n in mat_names]
    grads = dict(zip(mat_names, sibling_share_halves(halves)))

    g_small = dict(layer_small_grads(gp, small), **g_conv)
    packed_small = _pack_small(g_small)
    slabs = allgather_small("allgather_small", packed_small).reshape(N_DEV, packed_small.shape[0], LANES)
    summed = _unpack_small(sum_slabs("sum_small", slabs), {n: g.shape for n, g in g_small.items()})
    chip = 2 * lax.axis_index("x") + lax.axis_index("y")
    for n in SMALL_NAMES:
        grads[n] = summed[n]
    for n, ax in CONVS:
        s = given[n].shape[ax]
        grads[n] = lax.dynamic_slice_in_dim(summed[n], chip * s, s, axis=ax)

    upd = {n: adamw("adamw_" + n, given[n], grads[n], given["m_" + n], given["v_" + n]) for n in WEIGHT_NAMES}
    return (loss, gx[None], *[grads[n] for n in WEIGHT_NAMES], *[upd[n][0] for n in WEIGHT_NAMES],
            *[upd[n][1] for n in WEIGHT_NAMES], *[upd[n][2] for n in WEIGHT_NAMES])
```

```python
import functools
import math

import jax
import jax.numpy as jnp
import numpy as np
from jax import lax
from jax.experimental import pallas as pl
from jax.experimental.pallas import tpu as pltpu
from jax.experimental.pallas import tpu_sc as plsc

F32 = jnp.float32
BF16 = jnp.bfloat16
MESH = pl.DeviceIdType.MESH

D_MODEL = 1024
FFN_DIM = 2816
RG_C = 8.0
ATT_GROUPS = ((128, 1), (512, 4), (2048, 16))
ATT_HEADS = 12
ATT_HEAD_DIM = 64
ATT_SPAN = 128
DN_HEADS = 8
DN_HEAD_DIM = 128
DN_CHUNK = 64
EPS = 1e-6
NEG_INF = -1e30
N_CHIPS = 4
N_DEV = 8

ADAM_LR, ADAM_B1, ADAM_B2, ADAM_EPS, ADAM_WD, ADAM_STEP = 0.001, 0.9, 0.999, 1e-08, 0.01, 10

LANES = 128
VMEM_LIMIT = 56 * 1024 * 1024


def _params(*sem):
    return pltpu.CompilerParams(dimension_semantics=sem or None, vmem_limit_bytes=VMEM_LIMIT)


def _sigmoid(x):
    return 1.0 / (1.0 + jnp.exp(-x))


def _silu(x):
    return x * _sigmoid(x)


def _softplus(x):
    return jnp.maximum(x, 0.0) + jnp.log(1.0 + jnp.exp(-jnp.abs(x)))


def _gelu(x):
    return 0.5 * x * (1.0 + jnp.tanh(math.sqrt(2.0 / math.pi) * (x + 0.044715 * (x * x * x))))


def _neg_expm1(x):
    series = -x * (1.0 + x * (0.5 + x * (1.0 / 6 + x * (1.0 / 24 + x * (1.0 / 120 + x * (1.0 / 720))))))
    return jnp.where(x > -0.25, series, 1.0 - jnp.exp(x))


def _rms(x, g):
    return x * lax.rsqrt(jnp.mean(x * x, axis=-1, keepdims=True) + EPS) * g


_MM_DIMS = {"nn": (((1,), (0,)), ((), ())), "nt": (((1,), (1,)), ((), ())), "tn": (((0,), (0,)), ((), ()))}


def _split(a):
    hi = a.astype(BF16)
    return hi, (a - hi.astype(F32)).astype(BF16)


def _mxu(a, b, form, passes):
    (ca, cb), _ = _MM_DIMS[form]
    if a.ndim == 3:
        dims = (((ca[0] + 1,), (cb[0] + 1,)), ((0,), (0,)))
    else:
        dims = _MM_DIMS[form]
    dg = lambda p, q: lax.dot_general(p, q, dims, preferred_element_type=F32)
    if passes == 1:
        return dg(a.astype(BF16), b.astype(BF16))
    (a_hi, a_lo), (b_hi, b_lo) = _split(a), _split(b)
    return dg(a_hi, b_hi) + (dg(a_hi, b_lo) + dg(a_lo, b_hi))


@functools.partial(jax.custom_vjp, nondiff_argnums=(2, 3))
def _mm(a, b, form, passes):
    return _mxu(a, b, form, passes)


def _mm_fwd(a, b, form, passes):
    return _mxu(a, b, form, passes), (a, b)


def _mm_bwd(form, passes, res, g):
    a, b = res
    if form == "nn":
        return _mm(g, b, "nt", passes), _mm(a, g, "tn", passes)
    if form == "nt":
        return _mm(g, b, "nn", passes), _mm(g, a, "tn", passes)
    return _mm(b, g, "nt", passes), _mm(a, g, "nn", passes)


_mm.defvjp(_mm_fwd, _mm_bwd)


def _dot(a, b):
    return _mm(a, b, "nn", 1)


def _dot_nt(a, b):
    return _mm(a, b, "nt", 1)


def _dot_tn(a, b):
    return _mm(a, b, "tn", 1)


def _dot3(a, b):
    return _mm(a, b, "nn", 3)


def _rows(shape):
    return lax.broadcasted_iota(jnp.int32, shape, len(shape) - 2)


def _roll_down(x, s, fill):
    return jnp.where(_rows(x.shape) >= s, pltpu.roll(x, s, x.ndim - 2), fill)


def _roll_up(x, s, fill):
    n = x.shape[-2]
    return jnp.where(_rows(x.shape) < n - s, pltpu.roll(x, n - s, x.ndim - 2), fill)


@functools.partial(jax.custom_vjp, nondiff_argnums=(1,))
def _shift(x, s):
    return _roll_down(x, s, 0.0)


def _shift_fwd(x, s):
    return _roll_down(x, s, 0.0), None


def _shift_bwd(s, _, g):
    return (_roll_up(g, s, 0.0),)


_shift.defvjp(_shift_fwd, _shift_bwd)


def _causal_conv(x, w):
    return w[0:1] * _shift(x, 3) + w[1:2] * _shift(x, 2) + w[2:3] * _shift(x, 1) + w[3:4] * x


@jax.custom_vjp
def _lin_scan(a, b):
    return _lin_scan_fwd(a, b)[0]


def _lin_scan_fwd(a, b):
    a0 = a
    s = 1
    while s < a.shape[0]:
        b = a * _roll_down(b, s, 0.0) + b
        a = a * _roll_down(a, s, 1.0)
        s *= 2
    return b, (a0, b)


def _lin_scan_bwd(res, g):
    a, h = res
    c = _roll_up(a, 1, 0.0)
    s = 1
    while s < a.shape[0]:
        g = c * _roll_up(g, s, 0.0) + g
        c = c * _roll_up(c, s, 1.0)
        s *= 2
    return g * _roll_down(h, 1, 0.0), g


_lin_scan.defvjp(_lin_scan_fwd, _lin_scan_bwd)


@jax.custom_vjp
def _cumsum_rows(x):
    s = 1
    while s < x.shape[-2]:
        x = x + _roll_down(x, s, 0.0)
        s *= 2
    return x


def _cumsum_rows_fwd(x):
    return _cumsum_rows(x), None


def _cumsum_rows_bwd(_, g):
    s = 1
    while s < g.shape[-2]:
        g = g + _roll_up(g, s, 0.0)
        s *= 2
    return (g,)


_cumsum_rows.defvjp(_cumsum_rows_fwd, _cumsum_rows_bwd)


def _row_tile(t):
    return 256 if t % 256 == 0 else t


def _rowwise_fwd_call(name, f, rows, pars, tile):
    t = rows[0].shape[0]
    outs = jax.eval_shape(f, *[jax.ShapeDtypeStruct((tile, r.shape[1]), F32) for r in rows],
                          *[jax.ShapeDtypeStruct(p.shape, F32) for p in pars])
    nr, npar = len(rows), len(pars)

    def body(*refs):
        ins = [r[...] for r in refs[:nr + npar]]
        res = f(*ins)
        for o_ref, o in zip(refs[nr + npar:], res):
            o_ref[...] = o.astype(o_ref.dtype)

    return pl.pallas_call(
        body, name=name, grid=(t // tile,),
        in_specs=[pl.BlockSpec((tile, r.shape[1]), lambda i: (i, 0)) for r in rows]
        + [pl.BlockSpec(p.shape, lambda i: (0, 0)) for p in pars],
        out_specs=[pl.BlockSpec((tile, o.shape[1]), lambda i: (i, 0)) for o in outs],
        out_shape=[jax.ShapeDtypeStruct((t, o.shape[1]), F32) for o in outs],
        compiler_params=_params("parallel"),
    )(*rows, *pars)


def _rowwise_bwd_call(name, f, rows, pars, cts, tile):
    t = rows[0].shape[0]
    nr, npar, nct = len(rows), len(pars), len(cts)

    def body(*refs):
        ins = [r[...] for r in refs[:nr + npar]]
        gs = tuple(r[...] for r in refs[nr + npar:nr + npar + nct])
        outs = refs[nr + npar + nct:]
        _, vjp = jax.vjp(f, *ins)
        d = vjp(gs)
        for o_ref, v in zip(outs[:nr], d[:nr]):
            o_ref[...] = v

        @pl.when(pl.program_id(0) == 0)
        def _():
            for o_ref in outs[nr:]:
                o_ref[...] = jnp.zeros_like(o_ref)

        for o_ref, v in zip(outs[nr:], d[nr:]):
            o_ref[...] += v

    res = pl.pallas_call(
        body, name=name, grid=(t // tile,),
        in_specs=[pl.BlockSpec((tile, r.shape[1]), lambda i: (i, 0)) for r in rows]
        + [pl.BlockSpec(p.shape, lambda i: (0, 0)) for p in pars]
        + [pl.BlockSpec((tile, c.shape[1]), lambda i: (i, 0)) for c in cts],
        out_specs=[pl.BlockSpec((tile, r.shape[1]), lambda i: (i, 0)) for r in rows]
        + [pl.BlockSpec(p.shape, lambda i: (0, 0)) for p in pars],
        out_shape=[jax.ShapeDtypeStruct(r.shape, F32) for r in rows]
        + [jax.ShapeDtypeStruct(p.shape, F32) for p in pars],
        compiler_params=_params("arbitrary"),
    )(*rows, *pars, *cts)
    return tuple(res[:nr]), tuple(res[nr:])


def rowwise(name, f, rows, pars=()):
    tile = _row_tile(rows[0].shape[0])

    @jax.custom_vjp
    def op(rows, pars):
        return tuple(_rowwise_fwd_call(name, f, rows, pars, tile))

    def op_fwd(rows, pars):
        return op(rows, pars), (rows, pars)

    def op_bwd(res, cts):
        return _rowwise_bwd_call(name + "_bwd", f, res[0], res[1], tuple(cts), tile)

    op.defvjp(op_fwd, op_bwd)
    return op(tuple(rows), tuple(pars))


MM_TM = 512


def _tile_of(n, cap):
    best = None
    for c in range(LANES, min(n, cap) + 1, LANES):
        if n % c == 0:
            best = c
    return best or n


def _mmc_fwd(name, h, w):
    m, k = h.shape
    j, _, n = w.shape
    tm, tn = MM_TM, _tile_of(n, 1408)

    def body(h_ref, w_ref, o_ref):
        o_ref[0] = _dot(h_ref[...], w_ref[0])

    return pl.pallas_call(
        body, name=name, grid=(m // tm, j, n // tn),
        in_specs=[pl.BlockSpec((tm, k), lambda i, b, c: (i, 0)), pl.BlockSpec((1, k, tn), lambda i, b, c: (b, 0, c))],
        out_specs=pl.BlockSpec((1, tm, tn), lambda i, b, c: (b, i, c)),
        out_shape=jax.ShapeDtypeStruct((j, m, n), F32),
        compiler_params=_params("parallel", "parallel", "parallel"),
    )(h, w)


def _mmc_dh(name, dy, w):
    j, m, n = dy.shape
    k = w.shape[1]
    tm, tn = MM_TM, _tile_of(n, 1408)

    def body(dy_ref, w_ref, o_ref):
        part = _dot_nt(dy_ref[0], w_ref[0])

        @pl.when((pl.program_id(1) == 0) & (pl.program_id(2) == 0))
        def _():
            o_ref[...] = part

        @pl.when((pl.program_id(1) > 0) | (pl.program_id(2) > 0))
        def _():
            o_ref[...] += part

    return pl.pallas_call(
        body, name=name, grid=(m // tm, j, n // tn),
        in_specs=[pl.BlockSpec((1, tm, tn), lambda i, b, c: (b, i, c)), pl.BlockSpec((1, k, tn), lambda i, b, c: (b, 0, c))],
        out_specs=pl.BlockSpec((tm, k), lambda i, b, c: (i, 0)),
        out_shape=jax.ShapeDtypeStruct((m, k), F32),
        compiler_params=_params("parallel", "arbitrary", "arbitrary"),
    )(dy, w)


def _mmc_dw(name, h, dy):
    m, k = h.shape
    j, _, n = dy.shape
    tk, tn = _tile_of(k, 512), _tile_of(n, 1152)

    def body(h_ref, dy_ref, o_ref):
        o_ref[0] = _dot_tn(h_ref[...], dy_ref[0]).astype(BF16)

    return pl.pallas_call(
        body, name=name, grid=(j, k // tk, n // tn),
        in_specs=[pl.BlockSpec((m, tk), lambda b, i, c: (0, i)), pl.BlockSpec((1, m, tn), lambda b, i, c: (b, 0, c))],
        out_specs=pl.BlockSpec((1, tk, tn), lambda b, i, c: (b, i, c)),
        out_shape=jax.ShapeDtypeStruct((j, k, n), BF16),
        compiler_params=_params("parallel", "parallel", "parallel"),
    )(h, dy)


def mm_cols(name, h, w):
    @jax.custom_vjp
    def op(h, w):
        return _mmc_fwd(name, h, w)

    def op_fwd(h, w):
        return op(h, w), (h, w)

    def op_bwd(res, dy):
        h, w = res
        return _mmc_dh(name + "_dh", dy, w), _mmc_dw(name + "_dw", h, dy)

    op.defvjp(op_fwd, op_bwd)
    return op(h, w)


def mm_cols_t(name, h, wt):
    m, k = h.shape
    j, n, _ = wt.shape
    tm, tk = MM_TM, _tile_of(k, 512)

    def fwd(h, wt):
        def body(h_ref, w_ref, o_ref):
            o_ref[0] = _dot_nt(h_ref[...], w_ref[0])

        return pl.pallas_call(
            body, name=name, grid=(m // tm, j),
            in_specs=[pl.BlockSpec((tm, k), lambda i, b: (i, 0)), pl.BlockSpec((1, n, k), lambda i, b: (b, 0, 0))],
            out_specs=pl.BlockSpec((1, tm, n), lambda i, b: (b, i, 0)),
            out_shape=jax.ShapeDtypeStruct((j, m, n), F32), compiler_params=_params("parallel", "parallel"),
        )(h, wt)

    def dh(dy, wt):
        def body(dy_ref, w_ref, o_ref):
            part = _dot(dy_ref[0], w_ref[0])

            @pl.when(pl.program_id(1) == 0)
            def _():
                o_ref[...] = part

            @pl.when(pl.program_id(1) > 0)
            def _():
                o_ref[...] += part

        return pl.pallas_call(
            body, name=name + "_dh", grid=(m // tm, j),
            in_specs=[pl.BlockSpec((1, tm, n), lambda i, b: (b, i, 0)), pl.BlockSpec((1, n, k), lambda i, b: (b, 0, 0))],
            out_specs=pl.BlockSpec((tm, k), lambda i, b: (i, 0)),
            out_shape=jax.ShapeDtypeStruct((m, k), F32), compiler_params=_params("parallel", "arbitrary"),
        )(dy, wt)

    def dw(h, dy):
        def body(dy_ref, h_ref, o_ref):
            o_ref[0] = _dot_tn(dy_ref[0], h_ref[...]).astype(BF16)

        return pl.pallas_call(
            body, name=name + "_dw", grid=(j, k // tk),
            in_specs=[pl.BlockSpec((1, m, n), lambda b, i: (b, 0, 0)), pl.BlockSpec((m, tk), lambda b, i: (0, i))],
            out_specs=pl.BlockSpec((1, n, tk), lambda b, i: (b, 0, i)),
            out_shape=jax.ShapeDtypeStruct((j, n, k), BF16), compiler_params=_params("parallel", "parallel"),
        )(dy, h)

    @jax.custom_vjp
    def op(h, wt):
        return fwd(h, wt)

    def op_fwd(h, wt):
        return op(h, wt), (h, wt)

    def op_bwd(res, dy):
        return dh(dy, res[1]), dw(res[0], dy)

    op.defvjp(op_fwd, op_bwd)
    return op(h, wt)


def _mmr_fwd(name, a, w):
    j, m, n = a.shape
    nn = w.shape[2]
    tm, tn = MM_TM, _tile_of(nn, 1024)

    def body(a_ref, w_ref, o_ref):
        part = _dot(a_ref[0], w_ref[0])

        @pl.when(pl.program_id(2) == 0)
        def _():
            o_ref[...] = part

        @pl.when(pl.program_id(2) > 0)
        def _():
            o_ref[...] += part

    return pl.pallas_call(
        body, name=name, grid=(m // tm, nn // tn, j),
        in_specs=[pl.BlockSpec((1, tm, n), lambda i, c, b: (b, i, 0)), pl.BlockSpec((1, n, tn), lambda i, c, b: (b, 0, c))],
        out_specs=pl.BlockSpec((tm, tn), lambda i, c, b: (i, c)),
        out_shape=jax.ShapeDtypeStruct((m, nn), F32),
        compiler_params=_params("parallel", "parallel", "arbitrary"),
    )(a, w)


def _mmr_da(name, dy, w):
    m, nn = dy.shape
    j, n, _ = w.shape
    tm = MM_TM

    def body(dy_ref, w_ref, o_ref):
        o_ref[0] = _dot_nt(dy_ref[...], w_ref[0])

    return pl.pallas_call(
        body, name=name, grid=(m // tm, j),
        in_specs=[pl.BlockSpec((tm, nn), lambda i, b: (i, 0)), pl.BlockSpec((1, n, nn), lambda i, b: (b, 0, 0))],
        out_specs=pl.BlockSpec((1, tm, n), lambda i, b: (b, i, 0)),
        out_shape=jax.ShapeDtypeStruct((j, m, n), F32),
        compiler_params=_params("parallel", "parallel"),
    )(dy, w)


def _mmr_dw(name, a, dy):
    j, m, n = a.shape
    nn = dy.shape[1]
    tn = _tile_of(nn, 512)

    def body(a_ref, dy_ref, o_ref):
        o_ref[0] = _dot_tn(a_ref[0], dy_ref[...]).astype(BF16)

    return pl.pallas_call(
        body, name=name, grid=(j, nn // tn),
        in_specs=[pl.BlockSpec((1, m, n), lambda b, c: (b, 0, 0)), pl.BlockSpec((m, tn), lambda b, c: (0, c))],
        out_specs=pl.BlockSpec((1, n, tn), lambda b, c: (b, 0, c)),
        out_shape=jax.ShapeDtypeStruct((j, n, nn), BF16),
        compiler_params=_params("parallel", "parallel"),
    )(a, dy)


def mm_rows(name, a, w):
    @jax.custom_vjp
    def op(a, w):
        return _mmr_fwd(name, a, w)

    def op_fwd(a, w):
        return op(a, w), (a, w)

    def op_bwd(res, dy):
        a, w = res
        return _mmr_da(name + "_da", dy, w), _mmr_dw(name + "_dw", a, dy)

    op.defvjp(op_fwd, op_bwd)
    return op(a, w)


def _colwise_specs(cols, pars, par_block):
    t = cols[0].shape[0]
    specs = [pl.BlockSpec((t, LANES), lambda j: (0, j)) for _ in cols]
    for p, blk in zip(pars, par_block):
        if blk == "lane":
            specs.append(pl.BlockSpec((p.shape[0], LANES), lambda j: (0, j)))
        else:
            specs.append(pl.BlockSpec((1,) + p.shape[1:], lambda j: (j, 0, 0)))
    return specs


def _colwise_fwd_call(name, f, cols, pars, par_block, n_out):
    t, c = cols[0].shape
    nc, npar = len(cols), len(pars)

    def body(*refs):
        ins = [r[...] for r in refs[:nc]] + [r[...] if b == "lane" else r[0] for r, b in zip(refs[nc:nc + npar], par_block)]
        res = f(*ins)
        for o_ref, o in zip(refs[nc + npar:], res):
            o_ref[...] = o

    return pl.pallas_call(
        body, name=name, grid=(c // LANES,),
        in_specs=_colwise_specs(cols, pars, par_block),
        out_specs=[pl.BlockSpec((t, LANES), lambda j: (0, j)) for _ in range(n_out)],
        out_shape=[jax.ShapeDtypeStruct((t, c), F32) for _ in range(n_out)],
        compiler_params=_params("parallel"),
    )(*cols, *pars)


def _colwise_bwd_call(name, f, cols, pars, par_block, cts):
    t, c = cols[0].shape
    nc, npar, nct = len(cols), len(pars), len(cts)

    def body(*refs):
        ins = [r[...] for r in refs[:nc]] + [r[...] if b == "lane" else r[0] for r, b in zip(refs[nc:nc + npar], par_block)]
        gs = tuple(r[...] for r in refs[nc + npar:nc + npar + nct])
        outs = refs[nc + npar + nct:]
        _, vjp = jax.vjp(f, *ins)
        d = vjp(gs)
        for o_ref, v in zip(outs[:nc], d[:nc]):
            o_ref[...] = v
        for o_ref, v, b in zip(outs[nc:], d[nc:], par_block):
            if b == "lane":
                o_ref[...] = v
            else:
                o_ref[0] = v

    res = pl.pallas_call(
        body, name=name, grid=(c // LANES,),
        in_specs=_colwise_specs(cols, pars, par_block) + [pl.BlockSpec((t, LANES), lambda j: (0, j)) for _ in cts],
        out_specs=_colwise_specs(cols, pars, par_block),
        out_shape=[jax.ShapeDtypeStruct(v.shape, F32) for v in (*cols, *pars)],
        compiler_params=_params("parallel"),
    )(*cols, *pars, *cts)
    return tuple(res[:nc]), tuple(res[nc:])


def colwise(name, f, cols, pars, par_block, n_out):
    @jax.custom_vjp
    def op(cols, pars):
        return tuple(_colwise_fwd_call(name, f, cols, pars, par_block, n_out))

    def op_fwd(cols, pars):
        return op(cols, pars), (cols, pars)

    def op_bwd(res, cts):
        return _colwise_bwd_call(name + "_bwd", f, res[0], res[1], par_block, tuple(cts))

    op.defvjp(op_fwd, op_bwd)
    return op(tuple(cols), tuple(pars))


def _rg_block(x, gate, cw, cb, wr, br, wi, bi, lam):
    xa = _causal_conv(x, cw) + cb
    r = _sigmoid(_dot(xa, wr) + br)
    i = _sigmoid(_dot(xa, wi) + bi)
    log_a = -RG_C * r * _softplus(-lam)
    a = jnp.exp(log_a)
    b = jnp.sqrt(_neg_expm1(2.0 * log_a)) * (i * xa)
    return (_lin_scan(a, b) * _gelu(gate),)


def _dn_conv_block(mode):
    def f(x, cw):
        c = _silu(_causal_conv(x, cw))
        if mode == "v":
            return (c,)
        c = c * lax.rsqrt(jnp.sum(c * c, axis=-1, keepdims=True) + EPS)
        return (c * (DN_HEAD_DIM ** -0.5),) if mode == "q" else (c,)
    return f


def _block_diag(w):
    w = w.reshape(8, 2, 64, 64)
    z = jnp.zeros((8, 64, 64), w.dtype)
    top = jnp.concatenate([w[:, 0], z], axis=2)
    bot = jnp.concatenate([z, w[:, 1]], axis=2)
    return jnp.concatenate([top, bot], axis=1)


DN_HP = 8


def _dn_block(S, qw, kw, vw, gb, h0):
    hp, hd = S.shape[0], DN_HEAD_DIM
    heads = lambda a: jnp.concatenate([a[None, :, j * hd:(j + 1) * hd] for j in range(hp)], axis=0)
    lane = lax.broadcasted_iota(jnp.int32, gb.shape, 1)
    col = lambda i: jnp.sum(jnp.where(lane == i, gb, 0.0), axis=1, keepdims=True)[None]
    beta = jnp.concatenate([col(h0 + j) for j in range(hp)], axis=0)
    g = jnp.concatenate([col(h0 + j + DN_HEADS) for j in range(hp)], axis=0)
    s_new, o = _dn_step(S, heads(qw), heads(kw), heads(vw), beta, g)
    return s_new, jnp.concatenate([o[j:j + 1].reshape(o.shape[1:]) for j in range(hp)], axis=1)


def _dn_step(S, q, k, v, beta, g):
    c = DN_CHUNK
    ri = lax.broadcasted_iota(jnp.int32, (c, c), 0)
    ci = lax.broadcasted_iota(jnp.int32, (c, c), 1)
    incl, strict = ri >= ci, ri > ci
    eye = (ri == ci).astype(F32)
    gam = _cumsum_rows(g)
    gam_row = jnp.sum(jnp.where(ri <= ci, g, 0.0), axis=-2, keepdims=True)
    gam_last = jnp.sum(g, axis=-2, keepdims=True)
    decay = jnp.where(incl, jnp.exp(jnp.where(incl, gam - gam_row, 0.0)), 0.0)
    kb = k * beta
    vb = v * beta
    a = jnp.where(strict, _dot_nt(kb, k) * decay, 0.0)
    p = -a
    tinv = eye + p
    for _ in range(5):
        p = _dot3(p, p)
        tinv = tinv + _dot3(tinv, p)
    e_gam = jnp.exp(gam)
    u0 = _dot3(tinv, vb)
    wk = _dot3(tinv, kb * e_gam)
    qk = jnp.where(incl, _dot_nt(q, k) * decay, 0.0)
    q_dec = q * e_gam
    k_dec = k * jnp.exp(gam_last - gam)
    u = u0 - _dot(wk, S)
    o = _dot(q_dec, S) + _dot(qk, u)
    s_new = S * jnp.exp(gam_last) + _dot_tn(k_dec, u)
    return s_new, o


def _dn_fwd_call(q, k, v, gb):
    t, w = q.shape
    n, hp, hd, c = t // DN_CHUNK, DN_HP, DN_HEAD_DIM, DN_CHUNK

    def body(q_ref, k_ref, v_ref, gb_ref, o_ref, s0_ref, s_scr):
        @pl.when(pl.program_id(1) == 0)
        def _():
            s_scr[...] = jnp.zeros_like(s_scr)

        s_old = s_scr[...]
        s0_ref[:, 0] = s_old
        s_new, o = _dn_block(s_old, q_ref[...], k_ref[...], v_ref[...], gb_ref[...], pl.program_id(0) * hp)
        o_ref[...] = o
        s_scr[...] = s_new

    blk = pl.BlockSpec((c, hp * hd), lambda g, i: (i, g))
    return pl.pallas_call(
        body, name="dn_core", grid=(DN_HEADS // hp, n),
        in_specs=[blk, blk, blk, pl.BlockSpec((c, LANES), lambda g, i: (i, 0))],
        out_specs=[blk, pl.BlockSpec((hp, 1, hd, hd), lambda g, i: (g, i, 0, 0))],
        out_shape=[jax.ShapeDtypeStruct((t, w), F32), jax.ShapeDtypeStruct((DN_HEADS, n, hd, hd), F32)],
        scratch_shapes=[pltpu.VMEM((hp, hd, hd), F32)],
        compiler_params=_params("parallel", "arbitrary"),
    )(q, k, v, gb)


def _dn_bwd_call(q, k, v, gb, s0, do):
    t, w = q.shape
    n, hp, hd, c = t // DN_CHUNK, DN_HP, DN_HEAD_DIM, DN_CHUNK
    ng = DN_HEADS // hp

    def body(q_ref, k_ref, v_ref, gb_ref, s0_ref, do_ref, dq_ref, dk_ref, dv_ref, dgb_ref, ds_scr):
        @pl.when(pl.program_id(1) == 0)
        def _():
            ds_scr[...] = jnp.zeros_like(ds_scr)

        h0 = pl.program_id(0) * hp
        _, vjp = jax.vjp(lambda *a: _dn_block(*a, h0), s0_ref[:, 0], q_ref[...], k_ref[...], v_ref[...], gb_ref[...])
        ds, dq, dk, dv, dgb = vjp((ds_scr[...], do_ref[...]))
        ds_scr[...] = ds
        dq_ref[...], dk_ref[...], dv_ref[...] = dq, dk, dv
        dgb_ref[0] = dgb

    blk = pl.BlockSpec((c, hp * hd), lambda g, i: (n - 1 - i, g))
    res = pl.pallas_call(
        body, name="dn_core_bwd", grid=(ng, n),
        in_specs=[blk, blk, blk, pl.BlockSpec((c, LANES), lambda g, i: (n - 1 - i, 0)),
                  pl.BlockSpec((hp, 1, hd, hd), lambda g, i: (g, n - 1 - i, 0, 0)), blk],
        out_specs=[blk, blk, blk, pl.BlockSpec((1, c, LANES), lambda g, i: (g, n - 1 - i, 0))],
        out_shape=[jax.ShapeDtypeStruct((t, w), F32)] * 3 + [jax.ShapeDtypeStruct((ng, t, LANES), F32)],
        scratch_shapes=[pltpu.VMEM((hp, hd, hd), F32)],
        compiler_params=_params("parallel", "arbitrary"),
    )(q, k, v, gb, s0, do)
    return res[0], res[1], res[2], jnp.sum(res[3], axis=0)


@jax.custom_vjp
def dn_core(q, k, v, gb):
    return _dn_fwd_call(q, k, v, gb)[0]


def _dn_core_fwd(q, k, v, gb):
    o, s0 = _dn_fwd_call(q, k, v, gb)
    return o, (q, k, v, gb, s0)


def _dn_core_bwd(res, do):
    return _dn_bwd_call(*res, do)


dn_core.defvjp(_dn_core_fwd, _dn_core_bwd)


ATT_RB = 4


def _att_block(q, kp, kc, vp, vc, qn, kn, slope, has_prev, dil):
    s = ATT_SPAN
    qh = _rms(q, qn) * (ATT_HEAD_DIM ** -0.5)
    qi = lax.broadcasted_iota(jnp.int32, (s, s), 0)
    kj = lax.broadcasted_iota(jnp.int32, (s, s), 1)
    d_p = qi + s - kj
    d_c = qi - kj
    s_p = _dot_nt(qh, _rms(kp, kn)) - slope * (d_p * dil).astype(F32)
    s_c = _dot_nt(qh, _rms(kc, kn)) - slope * (d_c * dil).astype(F32)
    s_p = jnp.where((d_p <= s) & (has_prev > 0), s_p, NEG_INF)
    s_c = jnp.where(d_c >= 0, s_c, NEG_INF)
    m = lax.stop_gradient(jnp.maximum(jnp.max(s_p, axis=-1, keepdims=True), jnp.max(s_c, axis=-1, keepdims=True)))
    p_p = jnp.exp(s_p - m)
    p_c = jnp.exp(s_c - m)
    den = jnp.sum(p_p, axis=-1, keepdims=True) + jnp.sum(p_c, axis=-1, keepdims=True)
    o = _dot(p_p / den, vp) + _dot(p_c / den, vc)
    lse = m + jnp.log(den)
    return o, jnp.broadcast_to(lse, o.shape)


def _att_specs(dil):
    rb = ATT_RB
    hb = max(1, rb // dil)
    cur = pl.BlockSpec((rb, ATT_SPAN, ATT_HEAD_DIM), lambda i, n: (i, n, 0))
    prev = pl.BlockSpec((rb, ATT_SPAN, ATT_HEAD_DIM), lambda i, n: (i, jnp.maximum(n - 1, 0), 0))
    gain = pl.BlockSpec((hb, 1, ATT_HEAD_DIM), lambda i, n: ((i * rb) // (dil * hb), 0, 0))
    return cur, prev, gain


def _att_slope(group, dil):
    first = pl.program_id(0) * ATT_RB
    if dil == 1:
        head = first + lax.broadcasted_iota(jnp.int32, (ATT_RB, 1, 1), 0)
    else:
        head = jnp.full((1, 1, 1), first // dil, jnp.int32)
    return jnp.exp((head + (4 * group + 1)).astype(F32) * (-8.0 / ATT_HEADS * math.log(2.0)))


def _att_fwd_call(name, group, dil, q, k, v, qn, kn):
    r, l, e = q.shape
    nblk = l // ATT_SPAN
    cur, prev, gain = _att_specs(dil)

    def body(q_ref, kp_ref, kc_ref, vp_ref, vc_ref, qn_ref, kn_ref, o_ref, lse_ref):
        o_ref[...], lse_ref[...] = _att_block(q_ref[...], kp_ref[...], kc_ref[...], vp_ref[...], vc_ref[...], qn_ref[...],
                                              kn_ref[...], _att_slope(group, dil), pl.program_id(1), dil)

    return pl.pallas_call(
        body, name=name, grid=(r // ATT_RB, nblk),
        in_specs=[cur, prev, cur, prev, cur, gain, gain], out_specs=[cur, cur],
        out_shape=[jax.ShapeDtypeStruct(q.shape, F32)] * 2,
        compiler_params=_params("parallel", "arbitrary"),
    )(q, k, k, v, v, qn, kn)


def _att_bwd_call(name, group, dil, q, k, v, qn, kn, do, dlse):
    r, l, e = q.shape
    nblk = l // ATT_SPAN
    cur, prev, gain = _att_specs(dil)
    rows_per_gain = dil * max(1, ATT_RB // dil)

    def body(q_ref, kp_ref, kc_ref, vp_ref, vc_ref, qn_ref, kn_ref, do_ref, dlse_ref,
             dq_ref, dkp_ref, dkc_ref, dvp_ref, dvc_ref, dqn_ref, dkn_ref):
        slope, has_prev = _att_slope(group, dil), pl.program_id(1)
        _, vjp = jax.vjp(lambda *a: _att_block(*a, slope, has_prev, dil), q_ref[...], kp_ref[...], kc_ref[...], vp_ref[...],
                         vc_ref[...], qn_ref[...], kn_ref[...])
        dq, dkp, dkc, dvp, dvc, dqn, dkn = vjp((do_ref[...], dlse_ref[...]))
        dq_ref[...], dkp_ref[...], dkc_ref[...], dvp_ref[...], dvc_ref[...] = dq, dkp, dkc, dvp, dvc

        @pl.when(((pl.program_id(0) * ATT_RB) % rows_per_gain == 0) & (pl.program_id(1) == 0))
        def _():
            dqn_ref[...] = jnp.zeros_like(dqn_ref)
            dkn_ref[...] = jnp.zeros_like(dkn_ref)

        dqn_ref[...] += dqn
        dkn_ref[...] += dkn

    res = pl.pallas_call(
        body, name=name + "_bwd", grid=(r // ATT_RB, nblk),
        in_specs=[cur, prev, cur, prev, cur, gain, gain, cur, cur],
        out_specs=[cur] * 5 + [gain, gain],
        out_shape=[jax.ShapeDtypeStruct(q.shape, F32)] * 5 + [jax.ShapeDtypeStruct(qn.shape, F32)] * 2,
        compiler_params=_params("arbitrary", "arbitrary"),
    )(q, k, k, v, v, qn, kn, do, dlse)
    dq, dkp, dkc, dvp, dvc, dqn, dkn = res
    back = lambda g: jnp.pad(g[:, ATT_SPAN:], ((0, 0), (0, ATT_SPAN), (0, 0)))
    return dq, dkc + back(dkp), dvc + back(dvp), dqn, dkn


def att_group(name, group, dil, q, k, v, qn, kn):
    @jax.custom_vjp
    def op(q, k, v, qn, kn):
        return tuple(_att_fwd_call(name, group, dil, q, k, v, qn, kn))

    def op_fwd(q, k, v, qn, kn):
        return op(q, k, v, qn, kn), (q, k, v, qn, kn)

    def op_bwd(res, cts):
        return _att_bwd_call(name, group, dil, *res, *cts)

    op.defvjp(op_fwd, op_bwd)
    return op(q, k, v, qn, kn)


def _att_mix(o1, o2, o3, l1, l2, l3):
    m = jnp.maximum(jnp.maximum(l1, l2), l3)
    e1, e2, e3 = jnp.exp(l1 - m), jnp.exp(l2 - m), jnp.exp(l3 - m)
    s = e1 + e2 + e3
    return (jnp.concatenate([o1 * (e1 / s), o2 * (e2 / s), o3 * (e3 / s)], axis=1),)


def att_branch(name, pa, qn, kn, groups=ATT_GROUPS):
    t = pa.shape[0]
    hg, e = 4, ATT_HEAD_DIM
    q, k, v = (pa[:, i * 768:(i + 1) * 768].reshape(t, ATT_HEADS, e) for i in range(3))
    outs, lses = [], []
    for g, (window, dil) in enumerate(groups):
        assert window // dil == ATT_SPAN
        l = t // dil
        to_r = lambda a: a[:, hg * g:hg * (g + 1)].reshape(l, dil, hg, e).transpose(2, 1, 0, 3).reshape(hg * dil, l, e)
        back = lambda a: a.reshape(hg, dil, l, e).transpose(2, 1, 0, 3).reshape(t, hg * e)
        o, lse = att_group(f"{name}_att{g}", g, dil, to_r(q), to_r(k), to_r(v),
                           qn[hg * g:hg * (g + 1)].reshape(hg, 1, e), kn[hg * g:hg * (g + 1)].reshape(hg, 1, e))
        outs.append(back(o))
        lses.append(back(lse))
    return rowwise(f"{name}_attmix", _att_mix, outs + lses)[0]


def dn_gates(name, ba, a_log, dt_bias):
    place = lambda p: jnp.pad(p.reshape(1, DN_HEADS), ((0, 0), (DN_HEADS, LANES - 2 * DN_HEADS)))

    def f(x, al, dt):
        lane = lax.broadcasted_iota(jnp.int32, x.shape, 1)
        return (jnp.where(lane < DN_HEADS, _sigmoid(x), -jnp.exp(al) * _softplus(x + dt)),)

    return rowwise(name, f, (ba,), (place(a_log), place(dt_bias)))[0]


def _dn_out(o, z, g):
    parts = []
    for h in range(DN_HEADS):
        sl = slice(h * DN_HEAD_DIM, (h + 1) * DN_HEAD_DIM)
        parts.append(_rms(o[:, sl], g[:, sl]) * _silu(z[:, sl]))
    return (jnp.concatenate(parts, axis=1),)


def _merge(ml, za, zb, zc):
    d = D_MODEL
    return (_sigmoid(ml[:, :d]) * za + _sigmoid(ml[:, d:2 * d]) * zb + _sigmoid(ml[:, 2 * d:]) * zc,)


def _swiglu_act(g, u):
    return (_silu(g) * u,)


def add_norm(name, x, pend, scale, gain):
    if pend is None:
        return x, rowwise(name, lambda a, g: (_rms(a, g),), (x,), (gain,))[0]

    def f(a, b, g):
        s = a + scale * b
        return s, _rms(s, g)

    return rowwise(name, f, (x, pend), (gain,))


def ffn(name, h, wg, wu, wd):
    g, u = mm_cols_t(name + "_g", h, wg), mm_cols_t(name + "_u", h, wu)
    j, t, n = g.shape
    a = rowwise(name + "_act", _swiglu_act, (g.reshape(j * t, n), u.reshape(j * t, n)))[0]
    return mm_rows(name + "_d", a.reshape(j, t, n), wd)


W_IN_PIECES = (("rgx", 0, 1024), ("gate", 1024, 1024), ("att", 2048, 2304), ("dq", 4352, 1024), ("dk", 5376, 1024),
               ("dv", 6400, 1024), ("dz", 7424, 1024), ("ba", 8448, 16), ("mrg", 8464, 3072))
RG_PAR_BLOCKS = ("lane", "lane", "blk", "lane", "blk", "lane", "lane")


def mixer(name, u, w, p):
    mm = lambda nm, a, wt: mm_rows(nm, a[None], wt[None])
    pr = {k: mm_cols(f"{name}_in_{k}", u, w["in_" + k][None])[0] for k, _, _ in W_IN_PIECES}
    ya = colwise(name + "_rg", _rg_block, (pr["rgx"], pr["gate"]),
                 (w["rg_conv_w"], p["rg_conv_b"], _block_diag(p["rg_w_r"]), p["rg_b_r"], _block_diag(p["rg_w_i"]),
                  p["rg_b_i"], p["rg_lambda"]), RG_PAR_BLOCKS, 1)[0]
    yb = att_branch(name, pr["att"], p["att_q_norm"], p["att_k_norm"])
    cw = w["dn_conv_w"]
    cq = colwise(name + "_dnq", _dn_conv_block("q"), (pr["dq"],), (cw[:, :1024],), ("lane",), 1)[0]
    ck = colwise(name + "_dnk", _dn_conv_block("k"), (pr["dk"],), (cw[:, 1024:2048],), ("lane",), 1)[0]
    cv = colwise(name + "_dnv", _dn_conv_block("v"), (pr["dv"],), (cw[:, 2048:],), ("lane",), 1)[0]
    gb = dn_gates(name + "_dngate", pr["ba"], p["dn_a_log"], p["dn_dt_bias"])
    o_dn = dn_core(cq, ck, cv, gb)
    yc = rowwise(name + "_dnout", _dn_out, (o_dn, pr["dz"]), (p["dn_out_norm"].reshape(1, D_MODEL),))[0]
    y = rowwise(name + "_merge", _merge, (pr["mrg"], mm(name + "_ba", ya, w["br_a"]), mm(name + "_bb", yb, w["br_b"]),
                                          mm(name + "_bc", yc, w["br_c"])))[0]
    return mm(name + "_out", y, w["w_out"])


def _loss_call(x, pend, target):
    t, d = x.shape
    tile = _row_tile(t)

    def body(x_ref, p_ref, t_ref, loss_ref, g_ref):
        err = x_ref[...] + 0.5 * p_ref[...] - t_ref[...]
        g_ref[...] = err * (1.0 / d)

        @pl.when(pl.program_id(0) == 0)
        def _():
            loss_ref[...] = jnp.zeros_like(loss_ref)

        loss_ref[...] += jnp.full(loss_ref.shape, 0.5 / d, F32) * jnp.sum(err * err)

    blk = pl.BlockSpec((tile, d), lambda i: (i, 0))
    loss, g = pl.pallas_call(
        body, name="loss", grid=(t // tile,), in_specs=[blk, blk, blk],
        out_specs=[pl.BlockSpec((8, LANES), lambda i: (0, 0)), blk],
        out_shape=[jax.ShapeDtypeStruct((8, LANES), F32), jax.ShapeDtypeStruct((t, d), F32)],
        compiler_params=_params("arbitrary"),
    )(x, pend, target)
    return loss[0, 0], g


@jax.custom_vjp
def loss_op(x, pend, target):
    return _loss_call(x, pend, target)[0]


def _loss_fwd(x, pend, target):
    loss, g = _loss_call(x, pend, target)
    return loss, g


def _loss_bwd(g, ct):
    return ct * g, (0.5 * ct) * g, None


loss_op.defvjp(_loss_fwd, _loss_bwd)


def local_loss(w, p, x, target):
    pend, scale = None, 0.0
    for l in range(len(w)):
        n = f"L{l}"
        x, h = add_norm(n + "_n1", x, pend, scale, p[l]["ffn1_norm"])
        pend, scale = ffn(n + "_f1", h, w[l]["ffn1_w_gate"], w[l]["ffn1_w_up"], w[l]["ffn1_w_down"]), 0.5
        x, h = add_norm(n + "_nm", x, pend, scale, p[l]["mix_norm"])
        pend, scale = mixer(n + "_mx", h, w[l], p[l]), 1.0
        x, h = add_norm(n + "_n2", x, pend, scale, p[l]["ffn2_norm"])
        pend, scale = ffn(n + "_f2", h, w[l]["ffn2_w_gate"], w[l]["ffn2_w_up"], w[l]["ffn2_w_down"]), 0.5
    return loss_op(x, pend, target)


WEIGHT_NAMES = ("ffn1_norm", "ffn1_w_gate", "ffn1_w_up", "ffn1_w_down", "mix_norm", "w_in", "rg_conv_w", "rg_conv_b",
                "rg_w_r", "rg_b_r", "rg_w_i", "rg_b_i", "rg_lambda", "att_q_norm", "att_k_norm", "dn_conv_w", "dn_a_log",
                "dn_dt_bias", "dn_out_norm", "w_branch", "w_out", "ffn2_norm", "ffn2_w_gate", "ffn2_w_up", "ffn2_w_down")
MATRICES = (("ffn1_w_gate", 2), ("ffn1_w_up", 2), ("ffn1_w_down", 1), ("w_in", 2), ("w_branch", 1), ("w_out", 1),
            ("ffn2_w_gate", 2), ("ffn2_w_up", 2), ("ffn2_w_down", 1))
CONVS = (("rg_conv_w", 2), ("dn_conv_w", 2))
SHARD_AXIS = dict(MATRICES + CONVS)
SMALL_NAMES = tuple(n for n in WEIGHT_NAMES if n not in SHARD_AXIS)
ROW_PARAMS = ("ffn1_norm", "mix_norm", "rg_conv_b", "rg_b_r", "rg_b_i", "rg_lambda", "ffn2_norm")
FFN_MATS = ("ffn1_w_gate", "ffn1_w_up", "ffn1_w_down", "ffn2_w_gate", "ffn2_w_up", "ffn2_w_down")
TRANSPOSED_MATS = ("ffn1_w_gate", "ffn1_w_up", "ffn2_w_gate", "ffn2_w_up")
W_IN_SHARD = 2884
FIRST_NEEDED = (("ffn1_w_gate", 0), ("ffn1_w_up", 0), ("ffn1_w_down", 0), ("w_in", 0))
LATE_MATS = ("ffn2_w_gate", "ffn2_w_up", "ffn2_w_down", "w_out", "w_branch")
EXCHANGE_GROUPS = (lambda n, l: l == 1 and n in LATE_MATS,
                   lambda n, l: (l == 1) != (n in LATE_MATS),
                   lambda n, l: l == 0 and n == "w_in",
                   lambda n, l: l == 0 and n not in LATE_MATS and n != "w_in")


def _shard_minor(a, axis):
    a = jnp.moveaxis(a, 0, axis)
    return a.reshape(a.shape[:axis] + (N_CHIPS * a.shape[axis + 1],) + a.shape[axis + 2:])


def _w_in_piece(g, off, n):
    s = W_IN_SHARD
    parts = [g[j][:, max(off, j * s) - j * s:min(off + n, (j + 1) * s) - j * s]
             for j in range(N_CHIPS) if max(off, j * s) < min(off + n, (j + 1) * s)]
    return jnp.concatenate(parts, axis=1) if len(parts) > 1 else parts[0]


def _w_in_chip_grad(gl, j):
    s = W_IN_SHARD
    parts = [gl["in_" + k][:, max(off, j * s) - off:min(off + n, (j + 1) * s) - off]
             for k, off, n in W_IN_PIECES if max(off, j * s) < min(off + n, (j + 1) * s)]
    return jnp.concatenate(parts, axis=1)


def layer_weights(g, conv, l):
    w = {n: g[n, l] for n in FFN_MATS}
    w["w_out"] = g["w_out", l].reshape(D_MODEL, D_MODEL)
    for k, off, n in W_IN_PIECES:
        piece = _w_in_piece(g["w_in", l], off, n)
        w["in_" + k] = jnp.pad(piece, ((0, 0), (0, LANES - n))) if n < LANES else piece
    wb = g["w_branch", l].reshape(-1, D_MODEL)
    w["br_a"], w["br_b"], w["br_c"] = wb[:1024], wb[1024:1792], wb[1792:]
    w["rg_conv_w"], w["dn_conv_w"] = conv["rg_conv_w"][l], conv["dn_conv_w"][l]
    return w


def layer_weight_grads(gw):
    out = {}
    for l, gl in enumerate(gw):
        for n in FFN_MATS:
            out[n, l] = gl[n]
        out["w_out", l] = gl["w_out"].reshape(N_CHIPS, -1, D_MODEL)
        out["w_branch", l] = jnp.concatenate([gl["br_a"], gl["br_b"], gl["br_c"]], axis=0).reshape(N_CHIPS, -1, D_MODEL)
        out["w_in", l] = jnp.stack([_w_in_chip_grad(gl, j) for j in range(N_CHIPS)])
    conv = {n: jnp.stack([gl[n] for gl in gw]) for n, _ in CONVS}
    return out, conv


def layer_small(small, l):
    p = {n: small[n][l] for n in SMALL_NAMES}
    for n in ROW_PARAMS:
        p[n] = small[n][l:l + 1]
    return p


def layer_small_grads(gp, small):
    return {n: jnp.stack([g[n] for g in gp]).reshape(small[n].shape) for n in SMALL_NAMES}


HBM_SPEC = pl.BlockSpec(memory_space=pl.ANY)


def _place():
    x, y, c = lax.axis_index("x"), lax.axis_index("y"), lax.axis_index("c")
    other_chips = [(1 - x, y), (x, 1 - y), (1 - x, 1 - y)]
    return x, y, c, 2 * x + y, (x, y, 1 - c), other_chips


def _half_rows(ref, lead, hc):
    hr = ref.shape[-2] // 2
    return ref.at[(*lead, pl.ds(pl.multiple_of(hc * hr, 16), hr), slice(None))]


def _chip_index():
    return (2 * lax.axis_index("x") + lax.axis_index("y")).astype(jnp.int32).reshape(1)


def cast_into_blocks(name, w):
    l, rows, cols = w.shape
    tr = rows // 2

    def body(me_ref, w_ref, *o_refs):
        for a, o_ref in enumerate(o_refs):
            o_ref[...] = w_ref[a:a + 1].astype(BF16)

    return pl.pallas_call(
        body, name=name, out_shape=[jax.ShapeDtypeStruct((N_CHIPS, rows, cols), BF16)] * l,
        grid_spec=pltpu.PrefetchScalarGridSpec(
            num_scalar_prefetch=1, grid=(rows // tr,),
            in_specs=[pl.BlockSpec((l, tr, cols), lambda i, me: (0, i, 0))],
            out_specs=[pl.BlockSpec((1, tr, cols), lambda i, me: (me[0], i, 0))] * l),
        compiler_params=_params("parallel"),
    )(_chip_index(), w)


def _gather_blocks(bufs_in, bufs_out, send_sems, recv_sems):
    n = len(bufs_in)
    x, y, c, me, sibling, chips = _place()

    def copy(s, src, dst, to):
        return pltpu.make_async_remote_copy(src_ref=src, dst_ref=dst, send_sem=send_sems.at[s], recv_sem=recv_sems.at[s],
                                            device_id=to, device_id_type=MESH)

    first, passed = [], []
    for j, (cx, cy) in enumerate(chips):
        for i in range(n):
            cp = copy(6 * i + j, _half_rows(bufs_in[i], (me,), c), _half_rows(bufs_out[i], (me,), c), (cx, cy, c))
            cp.start()
            first.append(cp)
    for j, (cx, cy) in enumerate(chips):
        k = 2 * cx + cy
        for i in range(n):
            copy(6 * i + j, _half_rows(bufs_in[i], (me,), c), _half_rows(bufs_out[i], (k,), c), (cx, cy, c)).wait_recv()
            cp = copy(6 * i + 3 + j, _half_rows(bufs_out[i], (k,), c), _half_rows(bufs_out[i], (k,), c), sibling)
            cp.start()
            passed.append(cp)
    for j, (cx, cy) in enumerate(chips):
        k = 2 * cx + cy
        for i in range(n):
            copy(6 * i + 3 + j, _half_rows(bufs_in[i], (me,), c), _half_rows(bufs_out[i], (k,), 1 - c), sibling).wait_recv()
    for cp in first + passed:
        cp.wait_send()


def _handshake(peers):
    barrier = pltpu.get_barrier_semaphore()
    for p in peers:
        pl.semaphore_signal(barrier, inc=1, device_id=p, device_id_type=MESH)
    pl.semaphore_wait(barrier, len(peers))


def allgather_blocks_sc(name, bufs, collective_id):
    n = len(bufs)
    refs = [jax.new_ref(b, memory_space=pltpu.MemorySpace.HBM) for b in bufs]

    @pl.kernel(mesh=plsc.ScalarSubcoreMesh(axis_name="sequencer", num_cores=1), name=name,
               scratch_types=(pltpu.SemaphoreType.DMA((6 * n,)), pltpu.SemaphoreType.DMA((6 * n,))),
               compiler_params=pltpu.CompilerParams(collective_id=collective_id))
    def launch(send_sems, recv_sems):
        x, y, c, me, sibling, chips = _place()
        _handshake([(cx, cy, c) for cx, cy in chips] + [sibling])
        _gather_blocks(refs, refs, send_sems, recv_sems)

    launch()
    return [jax.freeze(r) for r in refs]


def allgather_mats(bufs):
    n = len(bufs)

    def body(*refs):
        _gather_blocks(refs[:n], refs[n:2 * n], *refs[2 * n:])

    return pl.pallas_call(
        body, name="allgather_mats", out_shape=[jax.ShapeDtypeStruct(b.shape, b.dtype) for b in bufs],
        in_specs=[HBM_SPEC] * n, out_specs=[HBM_SPEC] * n, input_output_aliases={i: i for i in range(n)},
        scratch_shapes=[pltpu.SemaphoreType.DMA((6 * n,)), pltpu.SemaphoreType.DMA((6 * n,))],
    )(*bufs)


PEER_FLIPS = tuple((fx, fy, fc) for fx in (0, 1) for fy in (0, 1) for fc in (0, 1))[1:]


def exchange_pieces_sc(name, gs, collective_id):
    n = len(gs)

    def body(*refs):
        ins, outs = refs[:n], refs[n:2 * n]
        send_sems, recv_sems = refs[2 * n:]
        x, y, c, me, sibling, chips = _place()
        my_dev = 4 * x + 2 * y + c
        flip = lambda v, f: 1 - v if f else v
        peers = [(flip(x, fx), flip(y, fy), flip(c, fc)) for fx, fy, fc in PEER_FLIPS]
        _handshake(peers)
        sends = []
        for r, (px, py, pc) in enumerate(peers):
            for i in range(n):
                cp = pltpu.make_async_remote_copy(
                    src_ref=_half_rows(ins[i], (2 * px + py,), pc), dst_ref=outs[i].at[my_dev], send_sem=send_sems.at[7 * i + r],
                    recv_sem=recv_sems.at[7 * i + r], device_id=(px, py, pc), device_id_type=MESH)
                cp.start()
                sends.append(cp)
        for r, (px, py, pc) in enumerate(peers):
            for i in range(n):
                pltpu.make_async_remote_copy(
                    src_ref=_half_rows(ins[i], (me,), c), dst_ref=outs[i].at[4 * px + 2 * py + pc], send_sem=send_sems.at[7 * i + r],
                    recv_sem=recv_sems.at[7 * i + r], device_id=(px, py, pc), device_id_type=MESH).wait_recv()
        for cp in sends:
            cp.wait_send()

    return pl.kernel(
        body, name=name, mesh=plsc.ScalarSubcoreMesh(axis_name="sequencer", num_cores=1),
        out_type=[jax.ShapeDtypeStruct((N_DEV, g.shape[1] // 2, g.shape[2]), g.dtype) for g in gs],
        scratch_types=[pltpu.SemaphoreType.DMA((7 * n,)), pltpu.SemaphoreType.DMA((7 * n,))],
        compiler_params=pltpu.CompilerParams(collective_id=collective_id),
    )(*gs)


def sibling_share_halves(name, fs):
    n = len(fs)
    every = (slice(None),)

    def body(*refs):
        ins, outs = refs[:n], refs[n:2 * n]
        send_sems, recv_sems = refs[2 * n:]
        x, y, c, me, sibling, chips = _place()
        sends = []
        for i in range(n):
            cp = pltpu.make_async_remote_copy(src_ref=_half_rows(ins[i], every, c), dst_ref=_half_rows(outs[i], every, c),
                                              send_sem=send_sems.at[i], recv_sem=recv_sems.at[i], device_id=sibling, device_id_type=MESH)
            cp.start()
            sends.append(cp)
        for i in range(n):
            pltpu.make_async_remote_copy(src_ref=_half_rows(ins[i], every, c), dst_ref=_half_rows(outs[i], every, 1 - c),
                                         send_sem=send_sems.at[i], recv_sem=recv_sems.at[i], device_id=sibling,
                                         device_id_type=MESH).wait_recv()
        for cp in sends:
            cp.wait_send()

    return pl.pallas_call(
        body, name=name, out_shape=[jax.ShapeDtypeStruct(f.shape, f.dtype) for f in fs],
        in_specs=[HBM_SPEC] * n, out_specs=[HBM_SPEC] * n, input_output_aliases={i: i for i in range(n)},
        scratch_shapes=[pltpu.SemaphoreType.DMA((n,)), pltpu.SemaphoreType.DMA((n,))],
    )(*fs)


def allgather_small(name, v):
    m_per, n = v.shape

    def body(x_ref, out_ref, send_sems, recv_sems, local_sem):
        x, y, c, _, sibling, chips = _place()
        me = (x, y, c)

        def rows(px, py, pc):
            return out_ref.at[pl.ds((4 * px + 2 * py + pc) * m_per, m_per), :]

        def copy(k, block, to, src=None):
            return pltpu.make_async_remote_copy(src_ref=rows(*block) if src is None else src, dst_ref=rows(*block),
                                                send_sem=send_sems.at[k], recv_sem=recv_sems.at[k], device_id=to, device_id_type=MESH)

        mine = pltpu.make_async_copy(x_ref, rows(*me), local_sem)
        mine.start()
        first = [copy(0, me, sibling, src=x_ref)]
        first += [copy(1 + j, me, (*chip, c), src=x_ref) for j, chip in enumerate(chips)]
        for cp in first:
            cp.start()
        passed = [copy(4 + j, (*chip, c), sibling) for j, chip in enumerate(chips)]
        for j, chip in enumerate(chips):
            copy(1 + j, (*chip, c), me).wait_recv()
            passed[j].start()
        copy(0, sibling, me).wait_recv()
        for j, chip in enumerate(chips):
            copy(4 + j, (*chip, 1 - c), me).wait_recv()
        for cp in first + passed:
            cp.wait_send()
        mine.wait()

    return pl.pallas_call(
        body, name=name, out_shape=jax.ShapeDtypeStruct((N_DEV * m_per, n), v.dtype),
        in_specs=[pl.BlockSpec(memory_space=pltpu.VMEM)], out_specs=pl.BlockSpec(memory_space=pltpu.VMEM),
        scratch_shapes=[pltpu.SemaphoreType.DMA((7,)), pltpu.SemaphoreType.DMA((7,)), pltpu.SemaphoreType.DMA],
        compiler_params=pltpu.CompilerParams(vmem_limit_bytes=VMEM_LIMIT),
    )(v)


SUM_BLOCK_ELEMS = 512 * 1024


def sum_slabs(name, b):
    k, h, w = b.shape

    def body(b_ref, o_ref):
        acc = b_ref[0].astype(F32)
        for i in range(1, k):
            acc = acc + b_ref[i].astype(F32)
        o_ref[...] = acc

    return pl.pallas_call(
        body, name=name, out_shape=jax.ShapeDtypeStruct((h, w), F32),
        in_specs=[pl.BlockSpec(memory_space=pltpu.VMEM)], out_specs=pl.BlockSpec(memory_space=pltpu.VMEM),
        compiler_params=pltpu.CompilerParams(vmem_limit_bytes=VMEM_LIMIT),
    )(b)


def sum_pieces(name, pieces, gs):
    nl = len(pieces)
    k, h, w = pieces[0].shape
    tile = max(t for t in range(16, h + 1, 16) if h % t == 0 and (t * w <= SUM_BLOCK_ELEMS or t == 16))
    nt = h // tile
    x, y, c = lax.axis_index("x"), lax.axis_index("y"), lax.axis_index("c")
    place = [v.astype(jnp.int32).reshape(1) for v in (c, 2 * x + y, 4 * x + 2 * y + c)]

    assert nl == 2

    def tile_of(l, a, i):
        return i * a if l else i * (1 - a) + (nt - 1) * a

    def body(c_ref, me_ref, dev_ref, *refs):
        p_refs, g_refs, o_ref = refs[:nl], refs[nl:2 * nl], refs[2 * nl]
        my_dev = dev_ref[0]
        for l in range(nl):
            @pl.when(pl.program_id(0) == l)
            def _():
                o_ref[0] = jnp.zeros(o_ref.shape[1:], F32)
                for d in range(k):
                    @pl.when(my_dev == d)
                    def _():
                        o_ref[0] += g_refs[l][0].astype(F32)

                    @pl.when(my_dev != d)
                    def _():
                        o_ref[0] += p_refs[l][d].astype(F32)

    in_specs = [pl.BlockSpec((k, tile, w), functools.partial(lambda l, a, i, cc, me, dev: (0, tile_of(l, a, i), 0), l))
                for l in range(nl)]
    in_specs += [pl.BlockSpec((1, tile, w), functools.partial(lambda l, a, i, cc, me, dev: (me[0], cc[0] * nt + tile_of(l, a, i), 0), l))
                 for l in range(nl)]
    return pl.pallas_call(
        body, name=name, out_shape=jax.ShapeDtypeStruct((nl, 2 * h, w), F32),
        grid_spec=pltpu.PrefetchScalarGridSpec(
            num_scalar_prefetch=3, grid=(nl, nt), in_specs=in_specs,
            out_specs=pl.BlockSpec((1, tile, w), lambda a, i, cc, me, dev: (a, cc[0] * nt + i, 0))),
        compiler_params=_params("arbitrary", "arbitrary"),
    )(*place, *pieces, *gs)


def _adam_block(w, g, m, v):
    m = ADAM_B1 * m + (1.0 - ADAM_B1) * g
    v = ADAM_B2 * v + (1.0 - ADAM_B2) * (g * g)
    m_hat = m / (1.0 - ADAM_B1 ** ADAM_STEP)
    v_hat = v / (1.0 - ADAM_B2 ** ADAM_STEP)
    return -ADAM_LR * (m_hat / (jnp.sqrt(v_hat) + ADAM_EPS) + ADAM_WD * w), m, v


def adamw(name, w, g, m, v):
    shape = w.shape
    cols = shape[-1]
    rows = w.size // cols
    tile = 128 if rows % 128 == 0 else rows
    flat = [a.reshape(rows, cols) for a in (w, g, m, v)]

    def body(w_ref, g_ref, m_ref, v_ref, d_ref, nm_ref, nv_ref):
        d_ref[...], nm_ref[...], nv_ref[...] = _adam_block(w_ref[...], g_ref[...], m_ref[...], v_ref[...])

    blk = pl.BlockSpec((tile, cols), lambda i: (i, 0))
    res = pl.pallas_call(
        body, name=name, grid=(rows // tile,), in_specs=[blk] * 4, out_specs=[blk] * 3,
        out_shape=[jax.ShapeDtypeStruct((rows, cols), F32)] * 3, compiler_params=_params("parallel"),
    )(*flat)
    return tuple(r.reshape(shape) for r in res)


def _pack_small(grads):
    flat = jnp.concatenate([grads[n].reshape(-1) for n in SMALL_NAMES + tuple(n for n, _ in CONVS)])
    n = flat.shape[0]
    total = -(-n // (8 * LANES)) * (8 * LANES)
    return jnp.pad(flat, (0, total - n)).reshape(-1, LANES)


def _unpack_small(v, shapes):
    flat = v.reshape(-1)
    out, off = {}, 0
    for n in SMALL_NAMES + tuple(n for n, _ in CONVS):
        sz = int(np.prod(shapes[n]))
        out[n] = flat[off:off + sz].reshape(shapes[n])
        off += sz
    return out


def kernel(x, ffn1_norm, ffn1_w_gate, ffn1_w_up, ffn1_w_down, mix_norm, w_in, rg_conv_w, rg_conv_b, rg_w_r, rg_b_r, rg_w_i, rg_b_i, rg_lambda, att_q_norm, att_k_norm, dn_conv_w, dn_a_log, dn_dt_bias, dn_out_norm, w_branch, w_out, ffn2_norm, ffn2_w_gate, ffn2_w_up, ffn2_w_down, loss_target, m_ffn1_norm, m_ffn1_w_gate, m_ffn1_w_up, m_ffn1_w_down, m_mix_norm, m_w_in, m_rg_conv_w, m_rg_conv_b, m_rg_w_r, m_rg_b_r, m_rg_w_i, m_rg_b_i, m_rg_lambda, m_att_q_norm, m_att_k_norm, m_dn_conv_w, m_dn_a_log, m_dn_dt_bias, m_dn_out_norm, m_w_branch, m_w_out, m_ffn2_norm, m_ffn2_w_gate, m_ffn2_w_up, m_ffn2_w_down, v_ffn1_norm, v_ffn1_w_gate, v_ffn1_w_up, v_ffn1_w_down, v_mix_norm, v_w_in, v_rg_conv_w, v_rg_conv_b, v_rg_w_r, v_rg_b_r, v_rg_w_i, v_rg_b_i, v_rg_lambda, v_att_q_norm, v_att_k_norm, v_dn_conv_w, v_dn_a_log, v_dn_dt_bias, v_dn_out_norm, v_w_branch, v_w_out, v_ffn2_norm, v_ffn2_w_gate, v_ffn2_w_up, v_ffn2_w_down):
    given = dict(locals())
    for n in TRANSPOSED_MATS:
        for pre in ("", "m_", "v_"):
            given[pre + n] = jnp.swapaxes(given[pre + n], 1, 2)
    small = {n: given[n] for n in SMALL_NAMES}
    n_layers = ffn1_norm.shape[0]
    mat_names = [n for n, _ in MATRICES]
    conv_names = [n for n, _ in CONVS]

    blocks = {}
    for n in mat_names:
        for l, b in enumerate(cast_into_blocks("cast_" + n, given[n])):
            blocks[n, l] = b
    first = [k for k in blocks if k in FIRST_NEEDED]
    later = [k for k in blocks if k not in FIRST_NEEDED]
    first_blocks, later_blocks = lax.optimization_barrier((allgather_mats([blocks[k] for k in first]), [blocks[k] for k in later]))
    gathered = dict(zip(first, first_blocks))
    gathered.update(zip(later, allgather_blocks_sc("allgather_later", later_blocks, 1)))
    taps = jnp.concatenate([given[n].reshape(-1) for n in conv_names]).reshape(-1, LANES)
    taps = allgather_small("allgather_taps", taps).reshape(N_CHIPS, 2, -1)[:, 0]
    conv, off = {}, 0
    for n, ax in CONVS:
        sz = given[n].size
        conv[n] = _shard_minor(taps[:, off:off + sz].reshape((N_CHIPS,) + given[n].shape), ax)
        off += sz
    w = [layer_weights(gathered, conv, l) for l in range(n_layers)]
    p = [layer_small(small, l) for l in range(n_layers)]

    loss, (gw, gp, gx) = jax.value_and_grad(local_loss, argnums=(0, 1, 2))(w, p, x[0], loss_target[0])
    loss = lax.psum(loss, ("x", "y", "c"))
    g_mats, g_conv = layer_weight_grads(gw)

    pieces = {}
    for i, group in enumerate(EXCHANGE_GROUPS):
        keys = [k for k in g_mats if group(*k)]
        pieces.update(zip(keys, exchange_pieces_sc(f"exchange_{i}", [g_mats[k] for k in keys], 2 + i)))
    halves = {n: sum_pieces("sum_" + n, [pieces[n, l] for l in range(n_layers)], [g_mats[n, l] for l in range(n_layers)])
              for n in mat_names}
    grads = {}
    for tag, names in (("late", [n for n in mat_names if n in LATE_MATS]), ("early", [n for n in mat_names if n not in LATE_MATS])):
        grads.update(zip(names, sibling_share_halves("share_" + tag, [halves[n] for n in names])))

    g_small = dict(layer_small_grads(gp, small), **g_conv)
    packed_small = _pack_small(g_small)
    slabs = allgather_small("allgather_small", packed_small).reshape(N_DEV, packed_small.shape[0], LANES)
    summed = _unpack_small(sum_slabs("sum_small", slabs), {n: g.shape for n, g in g_small.items()})
    chip = 2 * lax.axis_index("x") + lax.axis_index("y")
    for n in SMALL_NAMES:
        grads[n] = summed[n]
    for n, ax in CONVS:
        s = given[n].shape[ax]
        grads[n] = lax.dynamic_slice_in_dim(summed[n], chip * s, s, axis=ax)

    upd = {n: adamw("adamw_" + n, given[n], grads[n], given["m_" + n], given["v_" + n]) for n in WEIGHT_NAMES}
    out = lambda n, a: jnp.swapaxes(a, 1, 2) if n in TRANSPOSED_MATS else a
    return (loss, gx[None], *[out(n, grads[n]) for n in WEIGHT_NAMES], *[out(n, upd[n][0]) for n in WEIGHT_NAMES],
            *[out(n, upd[n][1]) for n in WEIGHT_NAMES], *[out(n, upd[n][2]) for n in WEIGHT_NAMES])
```

```python
import functools
import math

import jax
import jax.numpy as jnp
import numpy as np
from jax import lax
from jax.experimental import pallas as pl
from jax.experimental.pallas import tpu as pltpu
from jax.experimental.pallas import tpu_sc as plsc

F32 = jnp.float32
BF16 = jnp.bfloat16
MESH = pl.DeviceIdType.MESH

D_MODEL = 1024
FFN_DIM = 2816
RG_C = 8.0
ATT_GROUPS = ((128, 1), (512, 4), (2048, 16))
ATT_HEADS = 12
ATT_HEAD_DIM = 64
ATT_SPAN = 128
DN_HEADS = 8
DN_HEAD_DIM = 128
DN_CHUNK = 64
EPS = 1e-6
NEG_INF = -1e30
N_CHIPS = 4
N_DEV = 8

ADAM_LR, ADAM_B1, ADAM_B2, ADAM_EPS, ADAM_WD, ADAM_STEP = 0.001, 0.9, 0.999, 1e-08, 0.01, 10

LANES = 128
VMEM_LIMIT = 56 * 1024 * 1024


def _params(*sem):
    return pltpu.CompilerParams(dimension_semantics=sem or None, vmem_limit_bytes=VMEM_LIMIT)


def _sigmoid(x):
    return 1.0 / (1.0 + jnp.exp(-x))


def _silu(x):
    return x * _sigmoid(x)


def _softplus(x):
    return jnp.maximum(x, 0.0) + jnp.log(1.0 + jnp.exp(-jnp.abs(x)))


def _gelu(x):
    return 0.5 * x * (1.0 + jnp.tanh(math.sqrt(2.0 / math.pi) * (x + 0.044715 * (x * x * x))))


def _neg_expm1(x):
    series = -x * (1.0 + x * (0.5 + x * (1.0 / 6 + x * (1.0 / 24 + x * (1.0 / 120 + x * (1.0 / 720))))))
    return jnp.where(x > -0.25, series, 1.0 - jnp.exp(x))


def _rms(x, g):
    return x * lax.rsqrt(jnp.mean(x * x, axis=-1, keepdims=True) + EPS) * g


_MM_DIMS = {"nn": (((1,), (0,)), ((), ())), "nt": (((1,), (1,)), ((), ())), "tn": (((0,), (0,)), ((), ()))}


def _split(a):
    hi = a.astype(BF16)
    return hi, (a - hi.astype(F32)).astype(BF16)


def _mxu(a, b, form, passes):
    (ca, cb), _ = _MM_DIMS[form]
    if a.ndim == 3:
        dims = (((ca[0] + 1,), (cb[0] + 1,)), ((0,), (0,)))
    else:
        dims = _MM_DIMS[form]
    dg = lambda p, q: lax.dot_general(p, q, dims, preferred_element_type=F32)
    if passes == 1:
        return dg(a.astype(BF16), b.astype(BF16))
    (a_hi, a_lo), (b_hi, b_lo) = _split(a), _split(b)
    return dg(a_hi, b_hi) + (dg(a_hi, b_lo) + dg(a_lo, b_hi))


@functools.partial(jax.custom_vjp, nondiff_argnums=(2, 3))
def _mm(a, b, form, passes):
    return _mxu(a, b, form, passes)


def _mm_fwd(a, b, form, passes):
    return _mxu(a, b, form, passes), (a, b)


def _mm_bwd(form, passes, res, g):
    a, b = res
    if form == "nn":
        return _mm(g, b, "nt", passes), _mm(a, g, "tn", passes)
    if form == "nt":
        return _mm(g, b, "nn", passes), _mm(g, a, "tn", passes)
    return _mm(b, g, "nt", passes), _mm(a, g, "nn", passes)


_mm.defvjp(_mm_fwd, _mm_bwd)


def _dot(a, b):
    return _mm(a, b, "nn", 1)


def _dot_nt(a, b):
    return _mm(a, b, "nt", 1)


def _dot_tn(a, b):
    return _mm(a, b, "tn", 1)


def _dot3(a, b):
    return _mm(a, b, "nn", 3)


def _rows(shape):
    return lax.broadcasted_iota(jnp.int32, shape, len(shape) - 2)


def _roll_down(x, s, fill):
    return jnp.where(_rows(x.shape) >= s, pltpu.roll(x, s, x.ndim - 2), fill)


def _roll_up(x, s, fill):
    n = x.shape[-2]
    return jnp.where(_rows(x.shape) < n - s, pltpu.roll(x, n - s, x.ndim - 2), fill)


@functools.partial(jax.custom_vjp, nondiff_argnums=(1,))
def _shift(x, s):
    return _roll_down(x, s, 0.0)


def _shift_fwd(x, s):
    return _roll_down(x, s, 0.0), None


def _shift_bwd(s, _, g):
    return (_roll_up(g, s, 0.0),)


_shift.defvjp(_shift_fwd, _shift_bwd)


def _causal_conv(x, w):
    return w[0:1] * _shift(x, 3) + w[1:2] * _shift(x, 2) + w[2:3] * _shift(x, 1) + w[3:4] * x


@jax.custom_vjp
def _lin_scan(a, b):
    return _lin_scan_fwd(a, b)[0]


def _lin_scan_fwd(a, b):
    a0 = a
    s = 1
    while s < a.shape[0]:
        b = a * _roll_down(b, s, 0.0) + b
        a = a * _roll_down(a, s, 1.0)
        s *= 2
    return b, (a0, b)


def _lin_scan_bwd(res, g):
    a, h = res
    c = _roll_up(a, 1, 0.0)
    s = 1
    while s < a.shape[0]:
        g = c * _roll_up(g, s, 0.0) + g
        c = c * _roll_up(c, s, 1.0)
        s *= 2
    return g * _roll_down(h, 1, 0.0), g


_lin_scan.defvjp(_lin_scan_fwd, _lin_scan_bwd)


@jax.custom_vjp
def _cumsum_rows(x):
    s = 1
    while s < x.shape[-2]:
        x = x + _roll_down(x, s, 0.0)
        s *= 2
    return x


def _cumsum_rows_fwd(x):
    return _cumsum_rows(x), None


def _cumsum_rows_bwd(_, g):
    s = 1
    while s < g.shape[-2]:
        g = g + _roll_up(g, s, 0.0)
        s *= 2
    return (g,)


_cumsum_rows.defvjp(_cumsum_rows_fwd, _cumsum_rows_bwd)


ROW_BLOCK_BYTES = 14 * 1024 * 1024


def _row_tile(t, width=0):
    for tile in (512, 256):
        if t % tile == 0 and (tile == 256 or tile * width * 4 <= ROW_BLOCK_BYTES):
            return tile
    return t


def _rowwise_fwd_call(name, f, rows, pars, tile):
    t = rows[0].shape[0]
    outs = jax.eval_shape(f, *[jax.ShapeDtypeStruct((tile, r.shape[1]), F32) for r in rows],
                          *[jax.ShapeDtypeStruct(p.shape, F32) for p in pars])
    nr, npar = len(rows), len(pars)

    def body(*refs):
        ins = [r[...] for r in refs[:nr + npar]]
        res = f(*ins)
        for o_ref, o in zip(refs[nr + npar:], res):
            o_ref[...] = o.astype(o_ref.dtype)

    return pl.pallas_call(
        body, name=name, grid=(t // tile,),
        in_specs=[pl.BlockSpec((tile, r.shape[1]), lambda i: (i, 0)) for r in rows]
        + [pl.BlockSpec(p.shape, lambda i: (0, 0)) for p in pars],
        out_specs=[pl.BlockSpec((tile, o.shape[1]), lambda i: (i, 0)) for o in outs],
        out_shape=[jax.ShapeDtypeStruct((t, o.shape[1]), F32) for o in outs],
        compiler_params=_params("parallel"),
    )(*rows, *pars)


def _rowwise_bwd_call(name, f, rows, pars, cts, tile):
    t = rows[0].shape[0]
    nr, npar, nct = len(rows), len(pars), len(cts)

    def body(*refs):
        ins = [r[...] for r in refs[:nr + npar]]
        gs = tuple(r[...] for r in refs[nr + npar:nr + npar + nct])
        outs = refs[nr + npar + nct:]
        _, vjp = jax.vjp(f, *ins)
        d = vjp(gs)
        for o_ref, v in zip(outs[:nr], d[:nr]):
            o_ref[...] = v

        @pl.when(pl.program_id(0) == 0)
        def _():
            for o_ref in outs[nr:]:
                o_ref[...] = jnp.zeros_like(o_ref)

        for o_ref, v in zip(outs[nr:], d[nr:]):
            o_ref[...] += v

    res = pl.pallas_call(
        body, name=name, grid=(t // tile,),
        in_specs=[pl.BlockSpec((tile, r.shape[1]), lambda i: (i, 0)) for r in rows]
        + [pl.BlockSpec(p.shape, lambda i: (0, 0)) for p in pars]
        + [pl.BlockSpec((tile, c.shape[1]), lambda i: (i, 0)) for c in cts],
        out_specs=[pl.BlockSpec((tile, r.shape[1]), lambda i: (i, 0)) for r in rows]
        + [pl.BlockSpec(p.shape, lambda i: (0, 0)) for p in pars],
        out_shape=[jax.ShapeDtypeStruct(r.shape, F32) for r in rows]
        + [jax.ShapeDtypeStruct(p.shape, F32) for p in pars],
        compiler_params=_params("arbitrary"),
    )(*rows, *pars, *cts)
    return tuple(res[:nr]), tuple(res[nr:])


def rowwise(name, f, rows, pars=()):
    outs = jax.eval_shape(f, *[jax.ShapeDtypeStruct((8, r.shape[1]), F32) for r in rows],
                          *[jax.ShapeDtypeStruct(p.shape, F32) for p in pars])
    tile = _row_tile(rows[0].shape[0], 2 * sum(r.shape[1] for r in rows) + sum(o.shape[1] for o in outs))

    @jax.custom_vjp
    def op(rows, pars):
        return tuple(_rowwise_fwd_call(name, f, rows, pars, tile))

    def op_fwd(rows, pars):
        return op(rows, pars), (rows, pars)

    def op_bwd(res, cts):
        return _rowwise_bwd_call(name + "_bwd", f, res[0], res[1], tuple(cts), tile)

    op.defvjp(op_fwd, op_bwd)
    return op(tuple(rows), tuple(pars))


MM_TM = 512


def _tile_of(n, cap):
    best = None
    for c in range(LANES, min(n, cap) + 1, LANES):
        if n % c == 0:
            best = c
    return best or n


def _mmc_fwd(name, h, w):
    m, k = h.shape
    j, _, n = w.shape
    tm, tn = MM_TM, _tile_of(n, 1408)

    def body(h_ref, w_ref, o_ref):
        o_ref[0] = _dot(h_ref[...], w_ref[0])

    return pl.pallas_call(
        body, name=name, grid=(m // tm, j, n // tn),
        in_specs=[pl.BlockSpec((tm, k), lambda i, b, c: (i, 0)), pl.BlockSpec((1, k, tn), lambda i, b, c: (b, 0, c))],
        out_specs=pl.BlockSpec((1, tm, tn), lambda i, b, c: (b, i, c)),
        out_shape=jax.ShapeDtypeStruct((j, m, n), F32),
        compiler_params=_params("parallel", "parallel", "parallel"),
    )(h, w)


def _mmc_dh(name, dy, w):
    j, m, n = dy.shape
    k = w.shape[1]
    tm, tn = MM_TM, _tile_of(n, 1408)

    def body(dy_ref, w_ref, o_ref):
        part = _dot_nt(dy_ref[0], w_ref[0])

        @pl.when((pl.program_id(1) == 0) & (pl.program_id(2) == 0))
        def _():
            o_ref[...] = part

        @pl.when((pl.program_id(1) > 0) | (pl.program_id(2) > 0))
        def _():
            o_ref[...] += part

    return pl.pallas_call(
        body, name=name, grid=(m // tm, j, n // tn),
        in_specs=[pl.BlockSpec((1, tm, tn), lambda i, b, c: (b, i, c)), pl.BlockSpec((1, k, tn), lambda i, b, c: (b, 0, c))],
        out_specs=pl.BlockSpec((tm, k), lambda i, b, c: (i, 0)),
        out_shape=jax.ShapeDtypeStruct((m, k), F32),
        compiler_params=_params("parallel", "arbitrary", "arbitrary"),
    )(dy, w)


def _mmc_dw(name, h, dy):
    m, k = h.shape
    j, _, n = dy.shape
    tk, tn = _tile_of(k, 512), _tile_of(n, 1152)

    def body(h_ref, dy_ref, o_ref):
        o_ref[0] = _dot_tn(h_ref[...], dy_ref[0]).astype(BF16)

    return pl.pallas_call(
        body, name=name, grid=(j, k // tk, n // tn),
        in_specs=[pl.BlockSpec((m, tk), lambda b, i, c: (0, i)), pl.BlockSpec((1, m, tn), lambda b, i, c: (b, 0, c))],
        out_specs=pl.BlockSpec((1, tk, tn), lambda b, i, c: (b, i, c)),
        out_shape=jax.ShapeDtypeStruct((j, k, n), BF16),
        compiler_params=_params("parallel", "parallel", "parallel"),
    )(h, dy)


def mm_cols(name, h, w):
    @jax.custom_vjp
    def op(h, w):
        return _mmc_fwd(name, h, w)

    def op_fwd(h, w):
        return op(h, w), (h, w)

    def op_bwd(res, dy):
        h, w = res
        return _mmc_dh(name + "_dh", dy, w), _mmc_dw(name + "_dw", h, dy)

    op.defvjp(op_fwd, op_bwd)
    return op(h, w)


def _ffn_up(name, h, wt):
    m, k = h.shape
    j, n, _ = wt.shape
    tm = MM_TM

    def body(h_ref, w_ref, o_ref):
        o_ref[0] = _dot_nt(h_ref[...], w_ref[0])

    return pl.pallas_call(
        body, name=name, grid=(m // tm, j),
        in_specs=[pl.BlockSpec((tm, k), lambda i, b: (i, 0)), pl.BlockSpec((1, n, k), lambda i, b: (b, 0, 0))],
        out_specs=pl.BlockSpec((1, tm, n), lambda i, b: (b, i, 0)),
        out_shape=jax.ShapeDtypeStruct((j, m, n), F32), compiler_params=_params("parallel", "parallel"),
    )(h, wt)


def _ffn_down(name, g, u, wd):
    j, m, n = g.shape
    d = wd.shape[2]
    tm = MM_TM

    def body(g_ref, u_ref, w_ref, o_ref):
        part = _dot(_silu(g_ref[0]) * u_ref[0], w_ref[0])

        @pl.when(pl.program_id(1) == 0)
        def _():
            o_ref[...] = part

        @pl.when(pl.program_id(1) > 0)
        def _():
            o_ref[...] += part

    act = pl.BlockSpec((1, tm, n), lambda i, b: (b, i, 0))
    return pl.pallas_call(
        body, name=name, grid=(m // tm, j),
        in_specs=[act, act, pl.BlockSpec((1, n, d), lambda i, b: (b, 0, 0))],
        out_specs=pl.BlockSpec((tm, d), lambda i, b: (i, 0)),
        out_shape=jax.ShapeDtypeStruct((m, d), F32), compiler_params=_params("parallel", "arbitrary"),
    )(g, u, wd)


def _ffn_down_bwd(name, dy, g, u, wd):
    j, m, n = g.shape
    d = wd.shape[2]
    tm = MM_TM

    def body(dy_ref, g_ref, u_ref, w_ref, dg_ref, du_ref):
        da = _dot_nt(dy_ref[...], w_ref[0])
        gv = g_ref[0]
        s = _sigmoid(gv)
        dg_ref[0] = da * u_ref[0] * (s * (1.0 + gv * (1.0 - s)))
        du_ref[0] = da * (gv * s)

    act = pl.BlockSpec((1, tm, n), lambda i, b: (b, i, 0))
    return pl.pallas_call(
        body, name=name, grid=(m // tm, j),
        in_specs=[pl.BlockSpec((tm, d), lambda i, b: (i, 0)), act, act, pl.BlockSpec((1, n, d), lambda i, b: (b, 0, 0))],
        out_specs=[act, act], out_shape=[jax.ShapeDtypeStruct((j, m, n), F32)] * 2,
        compiler_params=_params("parallel", "parallel"),
    )(dy, g, u, wd)


def _ffn_down_dw(name, g, u, dy):
    j, m, n = g.shape
    d = dy.shape[1]
    tn = _tile_of(d, 512)

    def body(g_ref, u_ref, dy_ref, o_ref):
        o_ref[0] = _dot_tn(_silu(g_ref[0]) * u_ref[0], dy_ref[...]).astype(BF16)

    act = pl.BlockSpec((1, m, n), lambda b, c: (b, 0, 0))
    return pl.pallas_call(
        body, name=name, grid=(j, d // tn),
        in_specs=[act, act, pl.BlockSpec((m, tn), lambda b, c: (0, c))],
        out_specs=pl.BlockSpec((1, n, tn), lambda b, c: (b, 0, c)),
        out_shape=jax.ShapeDtypeStruct((j, n, d), BF16), compiler_params=_params("parallel", "parallel"),
    )(g, u, dy)


def _ffn_up_dh(name, dg, du, wg, wu):
    j, m, n = dg.shape
    k = wg.shape[2]
    tm = MM_TM

    def body(dg_ref, du_ref, wg_ref, wu_ref, o_ref):
        part = _dot(dg_ref[0], wg_ref[0]) + _dot(du_ref[0], wu_ref[0])

        @pl.when(pl.program_id(1) == 0)
        def _():
            o_ref[...] = part

        @pl.when(pl.program_id(1) > 0)
        def _():
            o_ref[...] += part

    act = pl.BlockSpec((1, tm, n), lambda i, b: (b, i, 0))
    wsp = pl.BlockSpec((1, n, k), lambda i, b: (b, 0, 0))
    return pl.pallas_call(
        body, name=name, grid=(m // tm, j), in_specs=[act, act, wsp, wsp],
        out_specs=pl.BlockSpec((tm, k), lambda i, b: (i, 0)),
        out_shape=jax.ShapeDtypeStruct((m, k), F32), compiler_params=_params("parallel", "arbitrary"),
    )(dg, du, wg, wu)


def _ffn_up_dw(name, dy, h):
    j, m, n = dy.shape
    k = h.shape[1]
    tk = _tile_of(k, 512)

    def body(dy_ref, h_ref, o_ref):
        o_ref[0] = _dot_tn(dy_ref[0], h_ref[...]).astype(BF16)

    return pl.pallas_call(
        body, name=name, grid=(j, k // tk),
        in_specs=[pl.BlockSpec((1, m, n), lambda b, i: (b, 0, 0)), pl.BlockSpec((m, tk), lambda b, i: (0, i))],
        out_specs=pl.BlockSpec((1, n, tk), lambda b, i: (b, 0, i)),
        out_shape=jax.ShapeDtypeStruct((j, n, k), BF16), compiler_params=_params("parallel", "parallel"),
    )(dy, h)


def ffn(name, h, wg, wu, wd):
    @jax.custom_vjp
    def op(h, wg, wu, wd):
        return _ffn_down(name + "_d", _ffn_up(name + "_g", h, wg), _ffn_up(name + "_u", h, wu), wd)

    def op_fwd(h, wg, wu, wd):
        g, u = _ffn_up(name + "_g", h, wg), _ffn_up(name + "_u", h, wu)
        return _ffn_down(name + "_d", g, u, wd), (h, g, u, wg, wu, wd)

    def op_bwd(res, dy):
        h, g, u, wg, wu, wd = res
        dg, du = _ffn_down_bwd(name + "_d_bwd", dy, g, u, wd)
        return (_ffn_up_dh(name + "_dh", dg, du, wg, wu), _ffn_up_dw(name + "_g_dw", dg, h), _ffn_up_dw(name + "_u_dw", du, h),
                _ffn_down_dw(name + "_d_dw", g, u, dy))

    op.defvjp(op_fwd, op_bwd)
    return op(h, wg, wu, wd)


def _mmr_fwd(name, a, w):
    j, m, n = a.shape
    nn = w.shape[2]
    tm, tn = MM_TM, _tile_of(nn, 1024)

    def body(a_ref, w_ref, o_ref):
        part = _dot(a_ref[0], w_ref[0])

        @pl.when(pl.program_id(2) == 0)
        def _():
            o_ref[...] = part

        @pl.when(pl.program_id(2) > 0)
        def _():
            o_ref[...] += part

    return pl.pallas_call(
        body, name=name, grid=(m // tm, nn // tn, j),
        in_specs=[pl.BlockSpec((1, tm, n), lambda i, c, b: (b, i, 0)), pl.BlockSpec((1, n, tn), lambda i, c, b: (b, 0, c))],
        out_specs=pl.BlockSpec((tm, tn), lambda i, c, b: (i, c)),
        out_shape=jax.ShapeDtypeStruct((m, nn), F32),
        compiler_params=_params("parallel", "parallel", "arbitrary"),
    )(a, w)


def _mmr_da(name, dy, w):
    m, nn = dy.shape
    j, n, _ = w.shape
    tm = MM_TM

    def body(dy_ref, w_ref, o_ref):
        o_ref[0] = _dot_nt(dy_ref[...], w_ref[0])

    return pl.pallas_call(
        body, name=name, grid=(m // tm, j),
        in_specs=[pl.BlockSpec((tm, nn), lambda i, b: (i, 0)), pl.BlockSpec((1, n, nn), lambda i, b: (b, 0, 0))],
        out_specs=pl.BlockSpec((1, tm, n), lambda i, b: (b, i, 0)),
        out_shape=jax.ShapeDtypeStruct((j, m, n), F32),
        compiler_params=_params("parallel", "parallel"),
    )(dy, w)


def _mmr_dw(name, a, dy):
    j, m, n = a.shape
    nn = dy.shape[1]
    tn = _tile_of(nn, 512)

    def body(a_ref, dy_ref, o_ref):
        o_ref[0] = _dot_tn(a_ref[0], dy_ref[...]).astype(BF16)

    return pl.pallas_call(
        body, name=name, grid=(j, nn // tn),
        in_specs=[pl.BlockSpec((1, m, n), lambda b, c: (b, 0, 0)), pl.BlockSpec((m, tn), lambda b, c: (0, c))],
        out_specs=pl.BlockSpec((1, n, tn), lambda b, c: (b, 0, c)),
        out_shape=jax.ShapeDtypeStruct((j, n, nn), BF16),
        compiler_params=_params("parallel", "parallel"),
    )(a, dy)


def mm_rows(name, a, w):
    @jax.custom_vjp
    def op(a, w):
        return _mmr_fwd(name, a, w)

    def op_fwd(a, w):
        return op(a, w), (a, w)

    def op_bwd(res, dy):
        a, w = res
        return _mmr_da(name + "_da", dy, w), _mmr_dw(name + "_dw", a, dy)

    op.defvjp(op_fwd, op_bwd)
    return op(a, w)


def _colwise_specs(cols, pars, par_block):
    t = cols[0].shape[0]
    specs = [pl.BlockSpec((t, LANES), lambda j: (0, j)) for _ in cols]
    for p, blk in zip(pars, par_block):
        if blk == "lane":
            specs.append(pl.BlockSpec((p.shape[0], LANES), lambda j: (0, j)))
        else:
            specs.append(pl.BlockSpec((1,) + p.shape[1:], lambda j: (j, 0, 0)))
    return specs


def _colwise_fwd_call(name, f, cols, pars, par_block, n_out):
    t, c = cols[0].shape
    nc, npar = len(cols), len(pars)

    def body(*refs):
        ins = [r[...] for r in refs[:nc]] + [r[...] if b == "lane" else r[0] for r, b in zip(refs[nc:nc + npar], par_block)]
        res = f(*ins)
        for o_ref, o in zip(refs[nc + npar:], res):
            o_ref[...] = o

    return pl.pallas_call(
        body, name=name, grid=(c // LANES,),
        in_specs=_colwise_specs(cols, pars, par_block),
        out_specs=[pl.BlockSpec((t, LANES), lambda j: (0, j)) for _ in range(n_out)],
        out_shape=[jax.ShapeDtypeStruct((t, c), F32) for _ in range(n_out)],
        compiler_params=_params("parallel"),
    )(*cols, *pars)


def _colwise_bwd_call(name, f, cols, pars, par_block, cts):
    t, c = cols[0].shape
    nc, npar, nct = len(cols), len(pars), len(cts)

    def body(*refs):
        ins = [r[...] for r in refs[:nc]] + [r[...] if b == "lane" else r[0] for r, b in zip(refs[nc:nc + npar], par_block)]
        gs = tuple(r[...] for r in refs[nc + npar:nc + npar + nct])
        outs = refs[nc + npar + nct:]
        _, vjp = jax.vjp(f, *ins)
        d = vjp(gs)
        for o_ref, v in zip(outs[:nc], d[:nc]):
            o_ref[...] = v
        for o_ref, v, b in zip(outs[nc:], d[nc:], par_block):
            if b == "lane":
                o_ref[...] = v
            else:
                o_ref[0] = v

    res = pl.pallas_call(
        body, name=name, grid=(c // LANES,),
        in_specs=_colwise_specs(cols, pars, par_block) + [pl.BlockSpec((t, LANES), lambda j: (0, j)) for _ in cts],
        out_specs=_colwise_specs(cols, pars, par_block),
        out_shape=[jax.ShapeDtypeStruct(v.shape, F32) for v in (*cols, *pars)],
        compiler_params=_params("parallel"),
    )(*cols, *pars, *cts)
    return tuple(res[:nc]), tuple(res[nc:])


def colwise(name, f, cols, pars, par_block, n_out):
    @jax.custom_vjp
    def op(cols, pars):
        return tuple(_colwise_fwd_call(name, f, cols, pars, par_block, n_out))

    def op_fwd(cols, pars):
        return op(cols, pars), (cols, pars)

    def op_bwd(res, cts):
        return _colwise_bwd_call(name + "_bwd", f, res[0], res[1], par_block, tuple(cts))

    op.defvjp(op_fwd, op_bwd)
    return op(tuple(cols), tuple(pars))


def _rg_block(x, gate, cw, cb, wr, br, wi, bi, lam):
    xa = _causal_conv(x, cw) + cb
    r = _sigmoid(_dot(xa, wr) + br)
    i = _sigmoid(_dot(xa, wi) + bi)
    log_a = -RG_C * r * _softplus(-lam)
    a = jnp.exp(log_a)
    b = jnp.sqrt(_neg_expm1(2.0 * log_a)) * (i * xa)
    return (_lin_scan(a, b) * _gelu(gate),)


def _dn_conv_block(mode):
    def f(x, cw):
        c = _silu(_causal_conv(x, cw))
        if mode == "v":
            return (c,)
        c = c * lax.rsqrt(jnp.sum(c * c, axis=-1, keepdims=True) + EPS)
        return (c * (DN_HEAD_DIM ** -0.5),) if mode == "q" else (c,)
    return f


def _block_diag(w):
    w = w.reshape(8, 2, 64, 64)
    z = jnp.zeros((8, 64, 64), w.dtype)
    top = jnp.concatenate([w[:, 0], z], axis=2)
    bot = jnp.concatenate([z, w[:, 1]], axis=2)
    return jnp.concatenate([top, bot], axis=1)


DN_HP = 8


def _dn_block(S, qw, kw, vw, gb, h0):
    hp, hd = S.shape[0], DN_HEAD_DIM
    heads = lambda a: jnp.concatenate([a[None, :, j * hd:(j + 1) * hd] for j in range(hp)], axis=0)
    lane = lax.broadcasted_iota(jnp.int32, gb.shape, 1)
    col = lambda i: jnp.sum(jnp.where(lane == i, gb, 0.0), axis=1, keepdims=True)[None]
    beta = jnp.concatenate([col(h0 + j) for j in range(hp)], axis=0)
    g = jnp.concatenate([col(h0 + j + DN_HEADS) for j in range(hp)], axis=0)
    s_new, o = _dn_step(S, heads(qw), heads(kw), heads(vw), beta, g)
    return s_new, jnp.concatenate([o[j:j + 1].reshape(o.shape[1:]) for j in range(hp)], axis=1)


def _dn_step(S, q, k, v, beta, g):
    c = DN_CHUNK
    ri = lax.broadcasted_iota(jnp.int32, (c, c), 0)
    ci = lax.broadcasted_iota(jnp.int32, (c, c), 1)
    incl, strict = ri >= ci, ri > ci
    eye = (ri == ci).astype(F32)
    gam = _cumsum_rows(g)
    gam_row = jnp.sum(jnp.where(ri <= ci, g, 0.0), axis=-2, keepdims=True)
    gam_last = jnp.sum(g, axis=-2, keepdims=True)
    decay = jnp.where(incl, jnp.exp(jnp.where(incl, gam - gam_row, 0.0)), 0.0)
    kb = k * beta
    vb = v * beta
    a = jnp.where(strict, _dot_nt(kb, k) * decay, 0.0)
    p = -a
    tinv = eye + p
    for _ in range(5):
        p = _dot3(p, p)
        tinv = tinv + _dot3(tinv, p)
    e_gam = jnp.exp(gam)
    u0 = _dot3(tinv, vb)
    wk = _dot3(tinv, kb * e_gam)
    qk = jnp.where(incl, _dot_nt(q, k) * decay, 0.0)
    q_dec = q * e_gam
    k_dec = k * jnp.exp(gam_last - gam)
    u = u0 - _dot(wk, S)
    o = _dot(q_dec, S) + _dot(qk, u)
    s_new = S * jnp.exp(gam_last) + _dot_tn(k_dec, u)
    return s_new, o


def _dn_fwd_call(q, k, v, gb):
    t, w = q.shape
    n, hp, hd, c = t // DN_CHUNK, DN_HP, DN_HEAD_DIM, DN_CHUNK

    def body(q_ref, k_ref, v_ref, gb_ref, o_ref, s0_ref, s_scr):
        @pl.when(pl.program_id(1) == 0)
        def _():
            s_scr[...] = jnp.zeros_like(s_scr)

        s_old = s_scr[...]
        s0_ref[:, 0] = s_old
        s_new, o = _dn_block(s_old, q_ref[...], k_ref[...], v_ref[...], gb_ref[...], pl.program_id(0) * hp)
        o_ref[...] = o
        s_scr[...] = s_new

    blk = pl.BlockSpec((c, hp * hd), lambda g, i: (i, g))
    return pl.pallas_call(
        body, name="dn_core", grid=(DN_HEADS // hp, n),
        in_specs=[blk, blk, blk, pl.BlockSpec((c, LANES), lambda g, i: (i, 0))],
        out_specs=[blk, pl.BlockSpec((hp, 1, hd, hd), lambda g, i: (g, i, 0, 0))],
        out_shape=[jax.ShapeDtypeStruct((t, w), F32), jax.ShapeDtypeStruct((DN_HEADS, n, hd, hd), F32)],
        scratch_shapes=[pltpu.VMEM((hp, hd, hd), F32)],
        compiler_params=_params("parallel", "arbitrary"),
    )(q, k, v, gb)


def _dn_bwd_call(q, k, v, gb, s0, do):
    t, w = q.shape
    n, hp, hd, c = t // DN_CHUNK, DN_HP, DN_HEAD_DIM, DN_CHUNK
    ng = DN_HEADS // hp

    def body(q_ref, k_ref, v_ref, gb_ref, s0_ref, do_ref, dq_ref, dk_ref, dv_ref, dgb_ref, ds_scr):
        @pl.when(pl.program_id(1) == 0)
        def _():
            ds_scr[...] = jnp.zeros_like(ds_scr)

        h0 = pl.program_id(0) * hp
        _, vjp = jax.vjp(lambda *a: _dn_block(*a, h0), s0_ref[:, 0], q_ref[...], k_ref[...], v_ref[...], gb_ref[...])
        ds, dq, dk, dv, dgb = vjp((ds_scr[...], do_ref[...]))
        ds_scr[...] = ds
        dq_ref[...], dk_ref[...], dv_ref[...] = dq, dk, dv
        dgb_ref[0] = dgb

    blk = pl.BlockSpec((c, hp * hd), lambda g, i: (n - 1 - i, g))
    res = pl.pallas_call(
        body, name="dn_core_bwd", grid=(ng, n),
        in_specs=[blk, blk, blk, pl.BlockSpec((c, LANES), lambda g, i: (n - 1 - i, 0)),
                  pl.BlockSpec((hp, 1, hd, hd), lambda g, i: (g, n - 1 - i, 0, 0)), blk],
        out_specs=[blk, blk, blk, pl.BlockSpec((1, c, LANES), lambda g, i: (g, n - 1 - i, 0))],
        out_shape=[jax.ShapeDtypeStruct((t, w), F32)] * 3 + [jax.ShapeDtypeStruct((ng, t, LANES), F32)],
        scratch_shapes=[pltpu.VMEM((hp, hd, hd), F32)],
        compiler_params=_params("parallel", "arbitrary"),
    )(q, k, v, gb, s0, do)
    return res[0], res[1], res[2], jnp.sum(res[3], axis=0)


@jax.custom_vjp
def dn_core(q, k, v, gb):
    return _dn_fwd_call(q, k, v, gb)[0]


def _dn_core_fwd(q, k, v, gb):
    o, s0 = _dn_fwd_call(q, k, v, gb)
    return o, (q, k, v, gb, s0)


def _dn_core_bwd(res, do):
    return _dn_bwd_call(*res, do)


dn_core.defvjp(_dn_core_fwd, _dn_core_bwd)


ATT_RB = 4


def _att_block(q, kp, kc, vp, vc, qn, kn, slope, has_prev, dil):
    s = ATT_SPAN
    qh = _rms(q, qn) * (ATT_HEAD_DIM ** -0.5)
    qi = lax.broadcasted_iota(jnp.int32, (s, s), 0)
    kj = lax.broadcasted_iota(jnp.int32, (s, s), 1)
    d_p = qi + s - kj
    d_c = qi - kj
    s_p = _dot_nt(qh, _rms(kp, kn)) - slope * (d_p * dil).astype(F32)
    s_c = _dot_nt(qh, _rms(kc, kn)) - slope * (d_c * dil).astype(F32)
    s_p = jnp.where((d_p <= s) & (has_prev > 0), s_p, NEG_INF)
    s_c = jnp.where(d_c >= 0, s_c, NEG_INF)
    m = lax.stop_gradient(jnp.maximum(jnp.max(s_p, axis=-1, keepdims=True), jnp.max(s_c, axis=-1, keepdims=True)))
    p_p = jnp.exp(s_p - m)
    p_c = jnp.exp(s_c - m)
    den = jnp.sum(p_p, axis=-1, keepdims=True) + jnp.sum(p_c, axis=-1, keepdims=True)
    o = _dot(p_p / den, vp) + _dot(p_c / den, vc)
    lse = m + jnp.log(den)
    return o, jnp.broadcast_to(lse, o.shape)


def _att_specs(dil):
    rb = ATT_RB
    hb = max(1, rb // dil)
    cur = pl.BlockSpec((rb, ATT_SPAN, ATT_HEAD_DIM), lambda i, n: (i, n, 0))
    prev = pl.BlockSpec((rb, ATT_SPAN, ATT_HEAD_DIM), lambda i, n: (i, jnp.maximum(n - 1, 0), 0))
    gain = pl.BlockSpec((hb, 1, ATT_HEAD_DIM), lambda i, n: ((i * rb) // (dil * hb), 0, 0))
    return cur, prev, gain


def _att_slope(group, dil):
    first = pl.program_id(0) * ATT_RB
    if dil == 1:
        head = first + lax.broadcasted_iota(jnp.int32, (ATT_RB, 1, 1), 0)
    else:
        head = jnp.full((1, 1, 1), first // dil, jnp.int32)
    return jnp.exp((head + (4 * group + 1)).astype(F32) * (-8.0 / ATT_HEADS * math.log(2.0)))


def _att_fwd_call(name, group, dil, q, k, v, qn, kn):
    r, l, e = q.shape
    nblk = l // ATT_SPAN
    cur, prev, gain = _att_specs(dil)

    def body(q_ref, kp_ref, kc_ref, vp_ref, vc_ref, qn_ref, kn_ref, o_ref, lse_ref):
        o_ref[...], lse_ref[...] = _att_block(q_ref[...], kp_ref[...], kc_ref[...], vp_ref[...], vc_ref[...], qn_ref[...],
                                              kn_ref[...], _att_slope(group, dil), pl.program_id(1), dil)

    return pl.pallas_call(
        body, name=name, grid=(r // ATT_RB, nblk),
        in_specs=[cur, prev, cur, prev, cur, gain, gain], out_specs=[cur, cur],
        out_shape=[jax.ShapeDtypeStruct(q.shape, F32)] * 2,
        compiler_params=_params("parallel", "arbitrary"),
    )(q, k, k, v, v, qn, kn)


def _att_bwd_call(name, group, dil, q, k, v, qn, kn, do, dlse):
    r, l, e = q.shape
    nblk = l // ATT_SPAN
    cur, prev, gain = _att_specs(dil)
    rows_per_gain = dil * max(1, ATT_RB // dil)

    def body(q_ref, kp_ref, kc_ref, vp_ref, vc_ref, qn_ref, kn_ref, do_ref, dlse_ref,
             dq_ref, dkp_ref, dkc_ref, dvp_ref, dvc_ref, dqn_ref, dkn_ref):
        slope, has_prev = _att_slope(group, dil), pl.program_id(1)
        _, vjp = jax.vjp(lambda *a: _att_block(*a, slope, has_prev, dil), q_ref[...], kp_ref[...], kc_ref[...], vp_ref[...],
                         vc_ref[...], qn_ref[...], kn_ref[...])
        dq, dkp, dkc, dvp, dvc, dqn, dkn = vjp((do_ref[...], dlse_ref[...]))
        dq_ref[...], dkp_ref[...], dkc_ref[...], dvp_ref[...], dvc_ref[...] = dq, dkp, dkc, dvp, dvc

        @pl.when(((pl.program_id(0) * ATT_RB) % rows_per_gain == 0) & (pl.program_id(1) == 0))
        def _():
            dqn_ref[...] = jnp.zeros_like(dqn_ref)
            dkn_ref[...] = jnp.zeros_like(dkn_ref)

        dqn_ref[...] += dqn
        dkn_ref[...] += dkn

    res = pl.pallas_call(
        body, name=name + "_bwd", grid=(r // ATT_RB, nblk),
        in_specs=[cur, prev, cur, prev, cur, gain, gain, cur, cur],
        out_specs=[cur] * 5 + [gain, gain],
        out_shape=[jax.ShapeDtypeStruct(q.shape, F32)] * 5 + [jax.ShapeDtypeStruct(qn.shape, F32)] * 2,
        compiler_params=_params("arbitrary", "arbitrary"),
    )(q, k, k, v, v, qn, kn, do, dlse)
    dq, dkp, dkc, dvp, dvc, dqn, dkn = res
    back = lambda g: jnp.pad(g[:, ATT_SPAN:], ((0, 0), (0, ATT_SPAN), (0, 0)))
    return dq, dkc + back(dkp), dvc + back(dvp), dqn, dkn


def att_group(name, group, dil, q, k, v, qn, kn):
    @jax.custom_vjp
    def op(q, k, v, qn, kn):
        return tuple(_att_fwd_call(name, group, dil, q, k, v, qn, kn))

    def op_fwd(q, k, v, qn, kn):
        return op(q, k, v, qn, kn), (q, k, v, qn, kn)

    def op_bwd(res, cts):
        return _att_bwd_call(name, group, dil, *res, *cts)

    op.defvjp(op_fwd, op_bwd)
    return op(q, k, v, qn, kn)


def _att_mix(o1, o2, o3, l1, l2, l3):
    m = jnp.maximum(jnp.maximum(l1, l2), l3)
    e1, e2, e3 = jnp.exp(l1 - m), jnp.exp(l2 - m), jnp.exp(l3 - m)
    s = e1 + e2 + e3
    return (jnp.concatenate([o1 * (e1 / s), o2 * (e2 / s), o3 * (e3 / s)], axis=1),)


def att_branch(name, pa, qn, kn, groups=ATT_GROUPS):
    t = pa.shape[0]
    hg, e = 4, ATT_HEAD_DIM
    q, k, v = (pa[:, i * 768:(i + 1) * 768].reshape(t, ATT_HEADS, e) for i in range(3))
    outs, lses = [], []
    for g, (window, dil) in enumerate(groups):
        assert window // dil == ATT_SPAN
        l = t // dil
        to_r = lambda a: a[:, hg * g:hg * (g + 1)].reshape(l, dil, hg, e).transpose(2, 1, 0, 3).reshape(hg * dil, l, e)
        back = lambda a: a.reshape(hg, dil, l, e).transpose(2, 1, 0, 3).reshape(t, hg * e)
        o, lse = att_group(f"{name}_att{g}", g, dil, to_r(q), to_r(k), to_r(v),
                           qn[hg * g:hg * (g + 1)].reshape(hg, 1, e), kn[hg * g:hg * (g + 1)].reshape(hg, 1, e))
        outs.append(back(o))
        lses.append(back(lse))
    return rowwise(f"{name}_attmix", _att_mix, outs + lses)[0]


def dn_gates(name, ba, a_log, dt_bias):
    place = lambda p: jnp.pad(p.reshape(1, DN_HEADS), ((0, 0), (DN_HEADS, LANES - 2 * DN_HEADS)))

    def f(x, al, dt):
        lane = lax.broadcasted_iota(jnp.int32, x.shape, 1)
        return (jnp.where(lane < DN_HEADS, _sigmoid(x), -jnp.exp(al) * _softplus(x + dt)),)

    return rowwise(name, f, (ba,), (place(a_log), place(dt_bias)))[0]


def _dn_out(o, z, g):
    parts = []
    for h in range(DN_HEADS):
        sl = slice(h * DN_HEAD_DIM, (h + 1) * DN_HEAD_DIM)
        parts.append(_rms(o[:, sl], g[:, sl]) * _silu(z[:, sl]))
    return (jnp.concatenate(parts, axis=1),)


def _merge(ml, za, zb, zc):
    d = D_MODEL
    return (_sigmoid(ml[:, :d]) * za + _sigmoid(ml[:, d:2 * d]) * zb + _sigmoid(ml[:, 2 * d:]) * zc,)


def add_norm(name, x, pend, scale, gain):
    if pend is None:
        return x, rowwise(name, lambda a, g: (_rms(a, g),), (x,), (gain,))[0]

    def f(a, b, g):
        s = a + scale * b
        return s, _rms(s, g)

    return rowwise(name, f, (x, pend), (gain,))


W_IN_PIECES = (("rgx", 0, 1024), ("gate", 1024, 1024), ("att", 2048, 2304), ("dq", 4352, 1024), ("dk", 5376, 1024),
               ("dv", 6400, 1024), ("dz", 7424, 1024), ("ba", 8448, 16), ("mrg", 8464, 3072))
RG_PAR_BLOCKS = ("lane", "lane", "blk", "lane", "blk", "lane", "lane")


def mixer(name, u, w, p):
    mm = lambda nm, a, wt: mm_rows(nm, a[None], wt[None])
    pr = {k: mm_cols(f"{name}_in_{k}", u, w["in_" + k][None])[0] for k, _, _ in W_IN_PIECES}
    ya = colwise(name + "_rg", _rg_block, (pr["rgx"], pr["gate"]),
                 (w["rg_conv_w"], p["rg_conv_b"], _block_diag(p["rg_w_r"]), p["rg_b_r"], _block_diag(p["rg_w_i"]),
                  p["rg_b_i"], p["rg_lambda"]), RG_PAR_BLOCKS, 1)[0]
    yb = att_branch(name, pr["att"], p["att_q_norm"], p["att_k_norm"])
    cw = w["dn_conv_w"]
    cq = colwise(name + "_dnq", _dn_conv_block("q"), (pr["dq"],), (cw[:, :1024],), ("lane",), 1)[0]
    ck = colwise(name + "_dnk", _dn_conv_block("k"), (pr["dk"],), (cw[:, 1024:2048],), ("lane",), 1)[0]
    cv = colwise(name + "_dnv", _dn_conv_block("v"), (pr["dv"],), (cw[:, 2048:],), ("lane",), 1)[0]
    gb = dn_gates(name + "_dngate", pr["ba"], p["dn_a_log"], p["dn_dt_bias"])
    o_dn = dn_core(cq, ck, cv, gb)
    yc = rowwise(name + "_dnout", _dn_out, (o_dn, pr["dz"]), (p["dn_out_norm"].reshape(1, D_MODEL),))[0]
    y = rowwise(name + "_merge", _merge, (pr["mrg"], mm(name + "_ba", ya, w["br_a"]), mm(name + "_bb", yb, w["br_b"]),
                                          mm(name + "_bc", yc, w["br_c"])))[0]
    return mm(name + "_out", y, w["w_out"])


def _loss_call(x, pend, target):
    t, d = x.shape
    tile = _row_tile(t)

    def body(x_ref, p_ref, t_ref, loss_ref, g_ref):
        err = x_ref[...] + 0.5 * p_ref[...] - t_ref[...]
        g_ref[...] = err * (1.0 / d)

        @pl.when(pl.program_id(0) == 0)
        def _():
            loss_ref[...] = jnp.zeros_like(loss_ref)

        loss_ref[...] += jnp.full(loss_ref.shape, 0.5 / d, F32) * jnp.sum(err * err)

    blk = pl.BlockSpec((tile, d), lambda i: (i, 0))
    loss, g = pl.pallas_call(
        body, name="loss", grid=(t // tile,), in_specs=[blk, blk, blk],
        out_specs=[pl.BlockSpec((8, LANES), lambda i: (0, 0)), blk],
        out_shape=[jax.ShapeDtypeStruct((8, LANES), F32), jax.ShapeDtypeStruct((t, d), F32)],
        compiler_params=_params("arbitrary"),
    )(x, pend, target)
    return loss[0, 0], g


@jax.custom_vjp
def loss_op(x, pend, target):
    return _loss_call(x, pend, target)[0]


def _loss_fwd(x, pend, target):
    loss, g = _loss_call(x, pend, target)
    return loss, g


def _loss_bwd(g, ct):
    return ct * g, (0.5 * ct) * g, None


loss_op.defvjp(_loss_fwd, _loss_bwd)


def local_loss(w, p, x, target):
    pend, scale = None, 0.0
    for l in range(len(w)):
        n = f"L{l}"
        x, h = add_norm(n + "_n1", x, pend, scale, p[l]["ffn1_norm"])
        pend, scale = ffn(n + "_f1", h, w[l]["ffn1_w_gate"], w[l]["ffn1_w_up"], w[l]["ffn1_w_down"]), 0.5
        x, h = add_norm(n + "_nm", x, pend, scale, p[l]["mix_norm"])
        pend, scale = mixer(n + "_mx", h, w[l], p[l]), 1.0
        x, h = add_norm(n + "_n2", x, pend, scale, p[l]["ffn2_norm"])
        pend, scale = ffn(n + "_f2", h, w[l]["ffn2_w_gate"], w[l]["ffn2_w_up"], w[l]["ffn2_w_down"]), 0.5
    return loss_op(x, pend, target)


WEIGHT_NAMES = ("ffn1_norm", "ffn1_w_gate", "ffn1_w_up", "ffn1_w_down", "mix_norm", "w_in", "rg_conv_w", "rg_conv_b",
                "rg_w_r", "rg_b_r", "rg_w_i", "rg_b_i", "rg_lambda", "att_q_norm", "att_k_norm", "dn_conv_w", "dn_a_log",
                "dn_dt_bias", "dn_out_norm", "w_branch", "w_out", "ffn2_norm", "ffn2_w_gate", "ffn2_w_up", "ffn2_w_down")
MATRICES = (("ffn1_w_gate", 2), ("ffn1_w_up", 2), ("ffn1_w_down", 1), ("w_in", 2), ("w_branch", 1), ("w_out", 1),
            ("ffn2_w_gate", 2), ("ffn2_w_up", 2), ("ffn2_w_down", 1))
CONVS = (("rg_conv_w", 2), ("dn_conv_w", 2))
SHARD_AXIS = dict(MATRICES + CONVS)
SMALL_NAMES = tuple(n for n in WEIGHT_NAMES if n not in SHARD_AXIS)
ROW_PARAMS = ("ffn1_norm", "mix_norm", "rg_conv_b", "rg_b_r", "rg_b_i", "rg_lambda", "ffn2_norm")
FFN_MATS = ("ffn1_w_gate", "ffn1_w_up", "ffn1_w_down", "ffn2_w_gate", "ffn2_w_up", "ffn2_w_down")
TRANSPOSED_MATS = ("ffn1_w_gate", "ffn1_w_up", "ffn2_w_gate", "ffn2_w_up")
W_IN_SHARD = 2884
FIRST_NEEDED = (("ffn1_w_gate", 0), ("ffn1_w_up", 0), ("ffn1_w_down", 0), ("w_in", 0))
LATE_MATS = ("ffn2_w_gate", "ffn2_w_up", "ffn2_w_down", "w_out", "w_branch")
EXCHANGE_GROUPS = (lambda n, l: l == 1 and n in LATE_MATS,
                   lambda n, l: (l == 1) != (n in LATE_MATS),
                   lambda n, l: l == 0 and n == "w_in",
                   lambda n, l: l == 0 and n not in LATE_MATS and n != "w_in")


def _shard_minor(a, axis):
    a = jnp.moveaxis(a, 0, axis)
    return a.reshape(a.shape[:axis] + (N_CHIPS * a.shape[axis + 1],) + a.shape[axis + 2:])


def _w_in_piece(g, off, n):
    s = W_IN_SHARD
    parts = [g[j][:, max(off, j * s) - j * s:min(off + n, (j + 1) * s) - j * s]
             for j in range(N_CHIPS) if max(off, j * s) < min(off + n, (j + 1) * s)]
    return jnp.concatenate(parts, axis=1) if len(parts) > 1 else parts[0]


def _w_in_chip_grad(gl, j):
    s = W_IN_SHARD
    parts = [gl["in_" + k][:, max(off, j * s) - off:min(off + n, (j + 1) * s) - off]
             for k, off, n in W_IN_PIECES if max(off, j * s) < min(off + n, (j + 1) * s)]
    return jnp.concatenate(parts, axis=1)


def layer_weights(g, conv, l):
    w = {n: g[n, l] for n in FFN_MATS}
    w["w_out"] = g["w_out", l].reshape(D_MODEL, D_MODEL)
    for k, off, n in W_IN_PIECES:
        piece = _w_in_piece(g["w_in", l], off, n)
        w["in_" + k] = jnp.pad(piece, ((0, 0), (0, LANES - n))) if n < LANES else piece
    wb = g["w_branch", l].reshape(-1, D_MODEL)
    w["br_a"], w["br_b"], w["br_c"] = wb[:1024], wb[1024:1792], wb[1792:]
    w["rg_conv_w"], w["dn_conv_w"] = conv["rg_conv_w"][l], conv["dn_conv_w"][l]
    return w


def layer_weight_grads(gw):
    out = {}
    for l, gl in enumerate(gw):
        for n in FFN_MATS:
            out[n, l] = gl[n]
        out["w_out", l] = gl["w_out"].reshape(N_CHIPS, -1, D_MODEL)
        out["w_branch", l] = jnp.concatenate([gl["br_a"], gl["br_b"], gl["br_c"]], axis=0).reshape(N_CHIPS, -1, D_MODEL)
        out["w_in", l] = jnp.stack([_w_in_chip_grad(gl, j) for j in range(N_CHIPS)])
    conv = {n: jnp.stack([gl[n] for gl in gw]) for n, _ in CONVS}
    return out, conv


def layer_small(small, l):
    p = {n: small[n][l] for n in SMALL_NAMES}
    for n in ROW_PARAMS:
        p[n] = small[n][l:l + 1]
    return p


def layer_small_grads(gp, small):
    return {n: jnp.stack([g[n] for g in gp]).reshape(small[n].shape) for n in SMALL_NAMES}


HBM_SPEC = pl.BlockSpec(memory_space=pl.ANY)


def _place():
    x, y, c = lax.axis_index("x"), lax.axis_index("y"), lax.axis_index("c")
    other_chips = [(1 - x, y), (x, 1 - y), (1 - x, 1 - y)]
    return x, y, c, 2 * x + y, (x, y, 1 - c), other_chips


def _half_rows(ref, lead, hc):
    hr = ref.shape[-2] // 2
    return ref.at[(*lead, pl.ds(pl.multiple_of(hc * hr, 16), hr), slice(None))]


def _chip_index():
    return (2 * lax.axis_index("x") + lax.axis_index("y")).astype(jnp.int32).reshape(1)


def cast_into_blocks(name, w):
    l, rows, cols = w.shape
    tr = rows // 2

    def body(me_ref, w_ref, *o_refs):
        for a, o_ref in enumerate(o_refs):
            o_ref[...] = w_ref[a:a + 1].astype(BF16)

    return pl.pallas_call(
        body, name=name, out_shape=[jax.ShapeDtypeStruct((N_CHIPS, rows, cols), BF16)] * l,
        grid_spec=pltpu.PrefetchScalarGridSpec(
            num_scalar_prefetch=1, grid=(rows // tr,),
            in_specs=[pl.BlockSpec((l, tr, cols), lambda i, me: (0, i, 0))],
            out_specs=[pl.BlockSpec((1, tr, cols), lambda i, me: (me[0], i, 0))] * l),
        compiler_params=_params("parallel"),
    )(_chip_index(), w)


def _gather_blocks(bufs_in, bufs_out, send_sems, recv_sems):
    n = len(bufs_in)
    x, y, c, me, sibling, chips = _place()

    def copy(s, src, dst, to):
        return pltpu.make_async_remote_copy(src_ref=src, dst_ref=dst, send_sem=send_sems.at[s], recv_sem=recv_sems.at[s],
                                            device_id=to, device_id_type=MESH)

    first, passed = [], []
    for j, (cx, cy) in enumerate(chips):
        for i in range(n):
            cp = copy(6 * i + j, _half_rows(bufs_in[i], (me,), c), _half_rows(bufs_out[i], (me,), c), (cx, cy, c))
            cp.start()
            first.append(cp)
    for j, (cx, cy) in enumerate(chips):
        k = 2 * cx + cy
        for i in range(n):
            copy(6 * i + j, _half_rows(bufs_in[i], (me,), c), _half_rows(bufs_out[i], (k,), c), (cx, cy, c)).wait_recv()
            cp = copy(6 * i + 3 + j, _half_rows(bufs_out[i], (k,), c), _half_rows(bufs_out[i], (k,), c), sibling)
            cp.start()
            passed.append(cp)
    for j, (cx, cy) in enumerate(chips):
        k = 2 * cx + cy
        for i in range(n):
            copy(6 * i + 3 + j, _half_rows(bufs_in[i], (me,), c), _half_rows(bufs_out[i], (k,), 1 - c), sibling).wait_recv()
    for cp in first + passed:
        cp.wait_send()


def _handshake(peers):
    barrier = pltpu.get_barrier_semaphore()
    for p in peers:
        pl.semaphore_signal(barrier, inc=1, device_id=p, device_id_type=MESH)
    pl.semaphore_wait(barrier, len(peers))


def allgather_blocks_sc(name, bufs, collective_id):
    n = len(bufs)
    refs = [jax.new_ref(b, memory_space=pltpu.MemorySpace.HBM) for b in bufs]

    @pl.kernel(mesh=plsc.ScalarSubcoreMesh(axis_name="sequencer", num_cores=1), name=name,
               scratch_types=(pltpu.SemaphoreType.DMA((6 * n,)), pltpu.SemaphoreType.DMA((6 * n,))),
               compiler_params=pltpu.CompilerParams(collective_id=collective_id))
    def launch(send_sems, recv_sems):
        x, y, c, me, sibling, chips = _place()
        _handshake([(cx, cy, c) for cx, cy in chips] + [sibling])
        _gather_blocks(refs, refs, send_sems, recv_sems)

    launch()
    return [jax.freeze(r) for r in refs]


def allgather_mats(bufs):
    n = len(bufs)

    def body(*refs):
        _gather_blocks(refs[:n], refs[n:2 * n], *refs[2 * n:])

    return pl.pallas_call(
        body, name="allgather_mats", out_shape=[jax.ShapeDtypeStruct(b.shape, b.dtype) for b in bufs],
        in_specs=[HBM_SPEC] * n, out_specs=[HBM_SPEC] * n, input_output_aliases={i: i for i in range(n)},
        scratch_shapes=[pltpu.SemaphoreType.DMA((6 * n,)), pltpu.SemaphoreType.DMA((6 * n,))],
    )(*bufs)


PEER_FLIPS = tuple((fx, fy, fc) for fx in (0, 1) for fy in (0, 1) for fc in (0, 1))[1:]


def exchange_pieces_sc(name, gs, collective_id):
    n = len(gs)

    def body(*refs):
        ins, outs = refs[:n], refs[n:2 * n]
        send_sems, recv_sems = refs[2 * n:]
        x, y, c, me, sibling, chips = _place()
        my_dev = 4 * x + 2 * y + c
        flip = lambda v, f: 1 - v if f else v
        peers = [(flip(x, fx), flip(y, fy), flip(c, fc)) for fx, fy, fc in PEER_FLIPS]
        _handshake(peers)
        sends = []
        for r, (px, py, pc) in enumerate(peers):
            for i in range(n):
                cp = pltpu.make_async_remote_copy(
                    src_ref=_half_rows(ins[i], (2 * px + py,), pc), dst_ref=outs[i].at[my_dev], send_sem=send_sems.at[7 * i + r],
                    recv_sem=recv_sems.at[7 * i + r], device_id=(px, py, pc), device_id_type=MESH)
                cp.start()
                sends.append(cp)
        for r, (px, py, pc) in enumerate(peers):
            for i in range(n):
                pltpu.make_async_remote_copy(
                    src_ref=_half_rows(ins[i], (me,), c), dst_ref=outs[i].at[4 * px + 2 * py + pc], send_sem=send_sems.at[7 * i + r],
                    recv_sem=recv_sems.at[7 * i + r], device_id=(px, py, pc), device_id_type=MESH).wait_recv()
        for cp in sends:
            cp.wait_send()

    return pl.kernel(
        body, name=name, mesh=plsc.ScalarSubcoreMesh(axis_name="sequencer", num_cores=1),
        out_type=[jax.ShapeDtypeStruct((N_DEV, g.shape[1] // 2, g.shape[2]), g.dtype) for g in gs],
        scratch_types=[pltpu.SemaphoreType.DMA((7 * n,)), pltpu.SemaphoreType.DMA((7 * n,))],
        compiler_params=pltpu.CompilerParams(collective_id=collective_id),
    )(*gs)


def sibling_share_halves(name, fs):
    n = len(fs)
    every = (slice(None),)

    def body(*refs):
        ins, outs = refs[:n], refs[n:2 * n]
        send_sems, recv_sems = refs[2 * n:]
        x, y, c, me, sibling, chips = _place()
        sends = []
        for i in range(n):
            cp = pltpu.make_async_remote_copy(src_ref=_half_rows(ins[i], every, c), dst_ref=_half_rows(outs[i], every, c),
                                              send_sem=send_sems.at[i], recv_sem=recv_sems.at[i], device_id=sibling, device_id_type=MESH)
            cp.start()
            sends.append(cp)
        for i in range(n):
            pltpu.make_async_remote_copy(src_ref=_half_rows(ins[i], every, c), dst_ref=_half_rows(outs[i], every, 1 - c),
                                         send_sem=send_sems.at[i], recv_sem=recv_sems.at[i], device_id=sibling,
                                         device_id_type=MESH).wait_recv()
        for cp in sends:
            cp.wait_send()

    return pl.pallas_call(
        body, name=name, out_shape=[jax.ShapeDtypeStruct(f.shape, f.dtype) for f in fs],
        in_specs=[HBM_SPEC] * n, out_specs=[HBM_SPEC] * n, input_output_aliases={i: i for i in range(n)},
        scratch_shapes=[pltpu.SemaphoreType.DMA((n,)), pltpu.SemaphoreType.DMA((n,))],
    )(*fs)


def allgather_small(name, v):
    m_per, n = v.shape

    def body(x_ref, out_ref, send_sems, recv_sems, local_sem):
        x, y, c, _, sibling, chips = _place()
        me = (x, y, c)

        def rows(px, py, pc):
            return out_ref.at[pl.ds((4 * px + 2 * py + pc) * m_per, m_per), :]

        def copy(k, block, to, src=None):
            return pltpu.make_async_remote_copy(src_ref=rows(*block) if src is None else src, dst_ref=rows(*block),
                                                send_sem=send_sems.at[k], recv_sem=recv_sems.at[k], device_id=to, device_id_type=MESH)

        mine = pltpu.make_async_copy(x_ref, rows(*me), local_sem)
        mine.start()
        first = [copy(0, me, sibling, src=x_ref)]
        first += [copy(1 + j, me, (*chip, c), src=x_ref) for j, chip in enumerate(chips)]
        for cp in first:
            cp.start()
        passed = [copy(4 + j, (*chip, c), sibling) for j, chip in enumerate(chips)]
        for j, chip in enumerate(chips):
            copy(1 + j, (*chip, c), me).wait_recv()
            passed[j].start()
        copy(0, sibling, me).wait_recv()
        for j, chip in enumerate(chips):
            copy(4 + j, (*chip, 1 - c), me).wait_recv()
        for cp in first + passed:
            cp.wait_send()
        mine.wait()

    return pl.pallas_call(
        body, name=name, out_shape=jax.ShapeDtypeStruct((N_DEV * m_per, n), v.dtype),
        in_specs=[pl.BlockSpec(memory_space=pltpu.VMEM)], out_specs=pl.BlockSpec(memory_space=pltpu.VMEM),
        scratch_shapes=[pltpu.SemaphoreType.DMA((7,)), pltpu.SemaphoreType.DMA((7,)), pltpu.SemaphoreType.DMA],
        compiler_params=pltpu.CompilerParams(vmem_limit_bytes=VMEM_LIMIT),
    )(v)


SUM_BLOCK_ELEMS = 512 * 1024


def sum_slabs(name, b):
    k, h, w = b.shape

    def body(b_ref, o_ref):
        acc = b_ref[0].astype(F32)
        for i in range(1, k):
            acc = acc + b_ref[i].astype(F32)
        o_ref[...] = acc

    return pl.pallas_call(
        body, name=name, out_shape=jax.ShapeDtypeStruct((h, w), F32),
        in_specs=[pl.BlockSpec(memory_space=pltpu.VMEM)], out_specs=pl.BlockSpec(memory_space=pltpu.VMEM),
        compiler_params=pltpu.CompilerParams(vmem_limit_bytes=VMEM_LIMIT),
    )(b)


def sum_pieces(name, pieces, gs):
    nl = len(pieces)
    k, h, w = pieces[0].shape
    tile = max(t for t in range(16, h + 1, 16) if h % t == 0 and (t * w <= SUM_BLOCK_ELEMS or t == 16))
    nt = h // tile
    x, y, c = lax.axis_index("x"), lax.axis_index("y"), lax.axis_index("c")
    place = [v.astype(jnp.int32).reshape(1) for v in (c, 2 * x + y, 4 * x + 2 * y + c)]

    assert nl == 2

    def tile_of(l, a, i):
        return i * a if l else i * (1 - a) + (nt - 1) * a

    def body(c_ref, me_ref, dev_ref, *refs):
        p_refs, g_refs, o_ref = refs[:nl], refs[nl:2 * nl], refs[2 * nl]
        my_dev = dev_ref[0]
        for l in range(nl):
            @pl.when(pl.program_id(0) == l)
            def _():
                o_ref[0] = jnp.zeros(o_ref.shape[1:], F32)
                for d in range(k):
                    @pl.when(my_dev == d)
                    def _():
                        o_ref[0] += g_refs[l][0].astype(F32)

                    @pl.when(my_dev != d)
                    def _():
                        o_ref[0] += p_refs[l][d].astype(F32)

    in_specs = [pl.BlockSpec((k, tile, w), functools.partial(lambda l, a, i, cc, me, dev: (0, tile_of(l, a, i), 0), l))
                for l in range(nl)]
    in_specs += [pl.BlockSpec((1, tile, w), functools.partial(lambda l, a, i, cc, me, dev: (me[0], cc[0] * nt + tile_of(l, a, i), 0), l))
                 for l in range(nl)]
    return pl.pallas_call(
        body, name=name, out_shape=jax.ShapeDtypeStruct((nl, 2 * h, w), F32),
        grid_spec=pltpu.PrefetchScalarGridSpec(
            num_scalar_prefetch=3, grid=(nl, nt), in_specs=in_specs,
            out_specs=pl.BlockSpec((1, tile, w), lambda a, i, cc, me, dev: (a, cc[0] * nt + i, 0))),
        compiler_params=_params("arbitrary", "arbitrary"),
    )(*place, *pieces, *gs)


def _adam_block(w, g, m, v):
    m = ADAM_B1 * m + (1.0 - ADAM_B1) * g
    v = ADAM_B2 * v + (1.0 - ADAM_B2) * (g * g)
    m_hat = m / (1.0 - ADAM_B1 ** ADAM_STEP)
    v_hat = v / (1.0 - ADAM_B2 ** ADAM_STEP)
    return -ADAM_LR * (m_hat / (jnp.sqrt(v_hat) + ADAM_EPS) + ADAM_WD * w), m, v


def adamw(name, w, g, m, v):
    shape = w.shape
    cols = shape[-1]
    rows = w.size // cols
    tile = 128 if rows % 128 == 0 else rows
    flat = [a.reshape(rows, cols) for a in (w, g, m, v)]

    def body(w_ref, g_ref, m_ref, v_ref, d_ref, nm_ref, nv_ref):
        d_ref[...], nm_ref[...], nv_ref[...] = _adam_block(w_ref[...], g_ref[...], m_ref[...], v_ref[...])

    blk = pl.BlockSpec((tile, cols), lambda i: (i, 0))
    res = pl.pallas_call(
        body, name=name, grid=(rows // tile,), in_specs=[blk] * 4, out_specs=[blk] * 3,
        out_shape=[jax.ShapeDtypeStruct((rows, cols), F32)] * 3, compiler_params=_params("parallel"),
    )(*flat)
    return tuple(r.reshape(shape) for r in res)


def _pack_small(grads):
    flat = jnp.concatenate([grads[n].reshape(-1) for n in SMALL_NAMES + tuple(n for n, _ in CONVS)])
    n = flat.shape[0]
    total = -(-n // (8 * LANES)) * (8 * LANES)
    return jnp.pad(flat, (0, total - n)).reshape(-1, LANES)


def _unpack_small(v, shapes):
    flat = v.reshape(-1)
    out, off = {}, 0
    for n in SMALL_NAMES + tuple(n for n, _ in CONVS):
        sz = int(np.prod(shapes[n]))
        out[n] = flat[off:off + sz].reshape(shapes[n])
        off += sz
    return out


def kernel(x, ffn1_norm, ffn1_w_gate, ffn1_w_up, ffn1_w_down, mix_norm, w_in, rg_conv_w, rg_conv_b, rg_w_r, rg_b_r, rg_w_i, rg_b_i, rg_lambda, att_q_norm, att_k_norm, dn_conv_w, dn_a_log, dn_dt_bias, dn_out_norm, w_branch, w_out, ffn2_norm, ffn2_w_gate, ffn2_w_up, ffn2_w_down, loss_target, m_ffn1_norm, m_ffn1_w_gate, m_ffn1_w_up, m_ffn1_w_down, m_mix_norm, m_w_in, m_rg_conv_w, m_rg_conv_b, m_rg_w_r, m_rg_b_r, m_rg_w_i, m_rg_b_i, m_rg_lambda, m_att_q_norm, m_att_k_norm, m_dn_conv_w, m_dn_a_log, m_dn_dt_bias, m_dn_out_norm, m_w_branch, m_w_out, m_ffn2_norm, m_ffn2_w_gate, m_ffn2_w_up, m_ffn2_w_down, v_ffn1_norm, v_ffn1_w_gate, v_ffn1_w_up, v_ffn1_w_down, v_mix_norm, v_w_in, v_rg_conv_w, v_rg_conv_b, v_rg_w_r, v_rg_b_r, v_rg_w_i, v_rg_b_i, v_rg_lambda, v_att_q_norm, v_att_k_norm, v_dn_conv_w, v_dn_a_log, v_dn_dt_bias, v_dn_out_norm, v_w_branch, v_w_out, v_ffn2_norm, v_ffn2_w_gate, v_ffn2_w_up, v_ffn2_w_down):
    given = dict(locals())
    for n in TRANSPOSED_MATS:
        for pre in ("", "m_", "v_"):
            given[pre + n] = jnp.swapaxes(given[pre + n], 1, 2)
    small = {n: given[n] for n in SMALL_NAMES}
    n_layers = ffn1_norm.shape[0]
    mat_names = [n for n, _ in MATRICES]
    conv_names = [n for n, _ in CONVS]

    blocks = {}
    for n in mat_names:
        for l, b in enumerate(cast_into_blocks("cast_" + n, given[n])):
            blocks[n, l] = b
    first = [k for k in blocks if k in FIRST_NEEDED]
    later = [k for k in blocks if k not in FIRST_NEEDED]
    first_blocks, later_blocks = lax.optimization_barrier((allgather_mats([blocks[k] for k in first]), [blocks[k] for k in later]))
    gathered = dict(zip(first, first_blocks))
    gathered.update(zip(later, allgather_blocks_sc("allgather_later", later_blocks, 1)))
    taps = jnp.concatenate([given[n].reshape(-1) for n in conv_names]).reshape(-1, LANES)
    taps = allgather_small("allgather_taps", taps).reshape(N_CHIPS, 2, -1)[:, 0]
    conv, off = {}, 0
    for n, ax in CONVS:
        sz = given[n].size
        conv[n] = _shard_minor(taps[:, off:off + sz].reshape((N_CHIPS,) + given[n].shape), ax)
        off += sz
    w = [layer_weights(gathered, conv, l) for l in range(n_layers)]
    p = [layer_small(small, l) for l in range(n_layers)]

    loss, (gw, gp, gx) = jax.value_and_grad(local_loss, argnums=(0, 1, 2))(w, p, x[0], loss_target[0])
    loss = lax.psum(loss, ("x", "y", "c"))
    g_mats, g_conv = layer_weight_grads(gw)

    pieces = {}
    for i, group in enumerate(EXCHANGE_GROUPS):
        keys = [k for k in g_mats if group(*k)]
        pieces.update(zip(keys, exchange_pieces_sc(f"exchange_{i}", [g_mats[k] for k in keys], 2 + i)))
    halves = {n: sum_pieces("sum_" + n, [pieces[n, l] for l in range(n_layers)], [g_mats[n, l] for l in range(n_layers)])
              for n in mat_names}
    grads = {}
    for tag, names in (("late", [n for n in mat_names if n in LATE_MATS]), ("early", [n for n in mat_names if n not in LATE_MATS])):
        grads.update(zip(names, sibling_share_halves("share_" + tag, [halves[n] for n in names])))

    g_small = dict(layer_small_grads(gp, small), **g_conv)
    packed_small = _pack_small(g_small)
    slabs = allgather_small("allgather_small", packed_small).reshape(N_DEV, packed_small.shape[0], LANES)
    summed = _unpack_small(sum_slabs("sum_small", slabs), {n: g.shape for n, g in g_small.items()})
    chip = 2 * lax.axis_index("x") + lax.axis_index("y")
    for n in SMALL_NAMES:
        grads[n] = summed[n]
    for n, ax in CONVS:
        s = given[n].shape[ax]
        grads[n] = lax.dynamic_slice_in_dim(summed[n], chip * s, s, axis=ax)

    upd = {n: adamw("adamw_" + n, given[n], grads[n], given["m_" + n], given["v_" + n]) for n in WEIGHT_NAMES}
    out = lambda n, a: jnp.swapaxes(a, 1, 2) if n in TRANSPOSED_MATS else a
    return (loss, gx[None], *[out(n, grads[n]) for n in WEIGHT_NAMES], *[out(n, upd[n][0]) for n in WEIGHT_NAMES],
            *[out(n, upd[n][1]) for n in WEIGHT_NAMES], *[out(n, upd[n][2]) for n in WEIGHT_NAMES])
```

```python
import functools
import math

import jax
import jax.numpy as jnp
import numpy as np
from jax import lax
from jax.experimental import pallas as pl
from jax.experimental.pallas import tpu as pltpu
from jax.experimental.pallas import tpu_sc as plsc

F32 = jnp.float32
BF16 = jnp.bfloat16
MESH = pl.DeviceIdType.MESH

D_MODEL = 1024
FFN_DIM = 2816
RG_C = 8.0
ATT_GROUPS = ((128, 1), (512, 4), (2048, 16))
ATT_HEADS = 12
ATT_HEAD_DIM = 64
ATT_SPAN = 128
DN_HEADS = 8
DN_HEAD_DIM = 128
DN_CHUNK = 64
EPS = 1e-6
NEG_INF = -1e30
N_CHIPS = 4
N_DEV = 8

ADAM_LR, ADAM_B1, ADAM_B2, ADAM_EPS, ADAM_WD, ADAM_STEP = 0.001, 0.9, 0.999, 1e-08, 0.01, 10

LANES = 128
VMEM_LIMIT = 56 * 1024 * 1024


def _params(*sem):
    return pltpu.CompilerParams(dimension_semantics=sem or None, vmem_limit_bytes=VMEM_LIMIT)


def _sigmoid(x):
    return 1.0 / (1.0 + jnp.exp(-x))


def _silu(x):
    return x * _sigmoid(x)


def _softplus(x):
    return jnp.maximum(x, 0.0) + jnp.log(1.0 + jnp.exp(-jnp.abs(x)))


def _gelu(x):
    return 0.5 * x * (1.0 + jnp.tanh(math.sqrt(2.0 / math.pi) * (x + 0.044715 * (x * x * x))))


def _neg_expm1(x):
    series = -x * (1.0 + x * (0.5 + x * (1.0 / 6 + x * (1.0 / 24 + x * (1.0 / 120 + x * (1.0 / 720))))))
    return jnp.where(x > -0.25, series, 1.0 - jnp.exp(x))


def _rms(x, g):
    return x * lax.rsqrt(jnp.mean(x * x, axis=-1, keepdims=True) + EPS) * g


_MM_DIMS = {"nn": (((1,), (0,)), ((), ())), "nt": (((1,), (1,)), ((), ())), "tn": (((0,), (0,)), ((), ()))}


def _split(a):
    hi = a.astype(BF16)
    return hi, (a - hi.astype(F32)).astype(BF16)


def _mxu(a, b, form, passes):
    (ca, cb), _ = _MM_DIMS[form]
    if a.ndim == 3:
        dims = (((ca[0] + 1,), (cb[0] + 1,)), ((0,), (0,)))
    else:
        dims = _MM_DIMS[form]
    dg = lambda p, q: lax.dot_general(p, q, dims, preferred_element_type=F32)
    if passes == 1:
        return dg(a.astype(BF16), b.astype(BF16))
    (a_hi, a_lo), (b_hi, b_lo) = _split(a), _split(b)
    return dg(a_hi, b_hi) + (dg(a_hi, b_lo) + dg(a_lo, b_hi))


@functools.partial(jax.custom_vjp, nondiff_argnums=(2, 3))
def _mm(a, b, form, passes):
    return _mxu(a, b, form, passes)


def _mm_fwd(a, b, form, passes):
    return _mxu(a, b, form, passes), (a, b)


def _mm_bwd(form, passes, res, g):
    a, b = res
    if form == "nn":
        return _mm(g, b, "nt", passes), _mm(a, g, "tn", passes)
    if form == "nt":
        return _mm(g, b, "nn", passes), _mm(g, a, "tn", passes)
    return _mm(b, g, "nt", passes), _mm(a, g, "nn", passes)


_mm.defvjp(_mm_fwd, _mm_bwd)


def _dot(a, b):
    return _mm(a, b, "nn", 1)


def _dot_nt(a, b):
    return _mm(a, b, "nt", 1)


def _dot_tn(a, b):
    return _mm(a, b, "tn", 1)


def _dot3(a, b):
    return _mm(a, b, "nn", 3)


def _rows(shape):
    return lax.broadcasted_iota(jnp.int32, shape, len(shape) - 2)


def _roll_down(x, s, fill):
    return jnp.where(_rows(x.shape) >= s, pltpu.roll(x, s, x.ndim - 2), fill)


def _roll_up(x, s, fill):
    n = x.shape[-2]
    return jnp.where(_rows(x.shape) < n - s, pltpu.roll(x, n - s, x.ndim - 2), fill)


@functools.partial(jax.custom_vjp, nondiff_argnums=(1,))
def _shift(x, s):
    return _roll_down(x, s, 0.0)


def _shift_fwd(x, s):
    return _roll_down(x, s, 0.0), None


def _shift_bwd(s, _, g):
    return (_roll_up(g, s, 0.0),)


_shift.defvjp(_shift_fwd, _shift_bwd)


def _causal_conv(x, w):
    return w[0:1] * _shift(x, 3) + w[1:2] * _shift(x, 2) + w[2:3] * _shift(x, 1) + w[3:4] * x


@jax.custom_vjp
def _lin_scan(a, b):
    return _lin_scan_fwd(a, b)[0]


def _lin_scan_fwd(a, b):
    a0 = a
    s = 1
    while s < a.shape[0]:
        b = a * _roll_down(b, s, 0.0) + b
        a = a * _roll_down(a, s, 1.0)
        s *= 2
    return b, (a0, b)


def _lin_scan_bwd(res, g):
    a, h = res
    c = _roll_up(a, 1, 0.0)
    s = 1
    while s < a.shape[0]:
        g = c * _roll_up(g, s, 0.0) + g
        c = c * _roll_up(c, s, 1.0)
        s *= 2
    return g * _roll_down(h, 1, 0.0), g


_lin_scan.defvjp(_lin_scan_fwd, _lin_scan_bwd)


@jax.custom_vjp
def _cumsum_rows(x):
    s = 1
    while s < x.shape[-2]:
        x = x + _roll_down(x, s, 0.0)
        s *= 2
    return x


def _cumsum_rows_fwd(x):
    return _cumsum_rows(x), None


def _cumsum_rows_bwd(_, g):
    s = 1
    while s < g.shape[-2]:
        g = g + _roll_up(g, s, 0.0)
        s *= 2
    return (g,)


_cumsum_rows.defvjp(_cumsum_rows_fwd, _cumsum_rows_bwd)


ROW_BLOCK_BYTES = 14 * 1024 * 1024


def _row_tile(t, width=0):
    for tile in (512, 256):
        if t % tile == 0 and (tile == 256 or tile * width * 4 <= ROW_BLOCK_BYTES):
            return tile
    return t


def _rowwise_fwd_call(name, f, rows, pars, tile):
    t = rows[0].shape[0]
    outs = jax.eval_shape(f, *[jax.ShapeDtypeStruct((tile, r.shape[1]), F32) for r in rows],
                          *[jax.ShapeDtypeStruct(p.shape, F32) for p in pars])
    nr, npar = len(rows), len(pars)

    def body(*refs):
        ins = [r[...] for r in refs[:nr + npar]]
        res = f(*ins)
        for o_ref, o in zip(refs[nr + npar:], res):
            o_ref[...] = o.astype(o_ref.dtype)

    return pl.pallas_call(
        body, name=name, grid=(t // tile,),
        in_specs=[pl.BlockSpec((tile, r.shape[1]), lambda i: (i, 0)) for r in rows]
        + [pl.BlockSpec(p.shape, lambda i: (0, 0)) for p in pars],
        out_specs=[pl.BlockSpec((tile, o.shape[1]), lambda i: (i, 0)) for o in outs],
        out_shape=[jax.ShapeDtypeStruct((t, o.shape[1]), F32) for o in outs],
        compiler_params=_params("parallel"),
    )(*rows, *pars)


def _rowwise_bwd_call(name, f, rows, pars, cts, tile):
    t = rows[0].shape[0]
    nr, npar, nct = len(rows), len(pars), len(cts)

    def body(*refs):
        ins = [r[...] for r in refs[:nr + npar]]
        gs = tuple(r[...] for r in refs[nr + npar:nr + npar + nct])
        outs = refs[nr + npar + nct:]
        _, vjp = jax.vjp(f, *ins)
        d = vjp(gs)
        for o_ref, v in zip(outs[:nr], d[:nr]):
            o_ref[...] = v

        @pl.when(pl.program_id(0) == 0)
        def _():
            for o_ref in outs[nr:]:
                o_ref[...] = jnp.zeros_like(o_ref)

        for o_ref, v in zip(outs[nr:], d[nr:]):
            o_ref[...] += v

    res = pl.pallas_call(
        body, name=name, grid=(t // tile,),
        in_specs=[pl.BlockSpec((tile, r.shape[1]), lambda i: (i, 0)) for r in rows]
        + [pl.BlockSpec(p.shape, lambda i: (0, 0)) for p in pars]
        + [pl.BlockSpec((tile, c.shape[1]), lambda i: (i, 0)) for c in cts],
        out_specs=[pl.BlockSpec((tile, r.shape[1]), lambda i: (i, 0)) for r in rows]
        + [pl.BlockSpec(p.shape, lambda i: (0, 0)) for p in pars],
        out_shape=[jax.ShapeDtypeStruct(r.shape, F32) for r in rows]
        + [jax.ShapeDtypeStruct(p.shape, F32) for p in pars],
        compiler_params=_params("arbitrary"),
    )(*rows, *pars, *cts)
    return tuple(res[:nr]), tuple(res[nr:])


def rowwise(name, f, rows, pars=()):
    outs = jax.eval_shape(f, *[jax.ShapeDtypeStruct((8, r.shape[1]), F32) for r in rows],
                          *[jax.ShapeDtypeStruct(p.shape, F32) for p in pars])
    tile = _row_tile(rows[0].shape[0], 2 * sum(r.shape[1] for r in rows) + sum(o.shape[1] for o in outs))

    @jax.custom_vjp
    def op(rows, pars):
        return tuple(_rowwise_fwd_call(name, f, rows, pars, tile))

    def op_fwd(rows, pars):
        return op(rows, pars), (rows, pars)

    def op_bwd(res, cts):
        return _rowwise_bwd_call(name + "_bwd", f, res[0], res[1], tuple(cts), tile)

    op.defvjp(op_fwd, op_bwd)
    return op(tuple(rows), tuple(pars))


MM_TM = 512


def _tile_of(n, cap):
    best = None
    for c in range(LANES, min(n, cap) + 1, LANES):
        if n % c == 0:
            best = c
    return best or n


def _mmc_fwd(name, h, w):
    m, k = h.shape
    j, _, n = w.shape
    tm, tn = MM_TM, _tile_of(n, 1408)

    def body(h_ref, w_ref, o_ref):
        o_ref[0] = _dot(h_ref[...], w_ref[0])

    return pl.pallas_call(
        body, name=name, grid=(m // tm, j, n // tn),
        in_specs=[pl.BlockSpec((tm, k), lambda i, b, c: (i, 0)), pl.BlockSpec((1, k, tn), lambda i, b, c: (b, 0, c))],
        out_specs=pl.BlockSpec((1, tm, tn), lambda i, b, c: (b, i, c)),
        out_shape=jax.ShapeDtypeStruct((j, m, n), F32),
        compiler_params=_params("parallel", "parallel", "parallel"),
    )(h, w)


def _mmc_dh(name, dy, w):
    j, m, n = dy.shape
    k = w.shape[1]
    tm, tn = MM_TM, _tile_of(n, 1408)

    def body(dy_ref, w_ref, o_ref):
        part = _dot_nt(dy_ref[0], w_ref[0])

        @pl.when((pl.program_id(1) == 0) & (pl.program_id(2) == 0))
        def _():
            o_ref[...] = part

        @pl.when((pl.program_id(1) > 0) | (pl.program_id(2) > 0))
        def _():
            o_ref[...] += part

    return pl.pallas_call(
        body, name=name, grid=(m // tm, j, n // tn),
        in_specs=[pl.BlockSpec((1, tm, tn), lambda i, b, c: (b, i, c)), pl.BlockSpec((1, k, tn), lambda i, b, c: (b, 0, c))],
        out_specs=pl.BlockSpec((tm, k), lambda i, b, c: (i, 0)),
        out_shape=jax.ShapeDtypeStruct((m, k), F32),
        compiler_params=_params("parallel", "arbitrary", "arbitrary"),
    )(dy, w)


def _mmc_dw(name, h, dy):
    m, k = h.shape
    j, _, n = dy.shape
    tk, tn = _tile_of(k, 512), _tile_of(n, 1152)

    def body(h_ref, dy_ref, o_ref):
        o_ref[0] = _dot_tn(h_ref[...], dy_ref[0]).astype(BF16)

    return pl.pallas_call(
        body, name=name, grid=(j, k // tk, n // tn),
        in_specs=[pl.BlockSpec((m, tk), lambda b, i, c: (0, i)), pl.BlockSpec((1, m, tn), lambda b, i, c: (b, 0, c))],
        out_specs=pl.BlockSpec((1, tk, tn), lambda b, i, c: (b, i, c)),
        out_shape=jax.ShapeDtypeStruct((j, k, n), BF16),
        compiler_params=_params("parallel", "parallel", "parallel"),
    )(h, dy)


def mm_cols(name, h, w):
    @jax.custom_vjp
    def op(h, w):
        return _mmc_fwd(name, h, w)

    def op_fwd(h, w):
        return op(h, w), (h, w)

    def op_bwd(res, dy):
        h, w = res
        return _mmc_dh(name + "_dh", dy, w), _mmc_dw(name + "_dw", h, dy)

    op.defvjp(op_fwd, op_bwd)
    return op(h, w)


def _ffn_up(name, h, wt):
    m, k = h.shape
    j, n, _ = wt.shape
    tm = MM_TM

    def body(h_ref, w_ref, o_ref):
        o_ref[0] = _dot_nt(h_ref[...], w_ref[0])

    return pl.pallas_call(
        body, name=name, grid=(m // tm, j),
        in_specs=[pl.BlockSpec((tm, k), lambda i, b: (i, 0)), pl.BlockSpec((1, n, k), lambda i, b: (b, 0, 0))],
        out_specs=pl.BlockSpec((1, tm, n), lambda i, b: (b, i, 0)),
        out_shape=jax.ShapeDtypeStruct((j, m, n), F32), compiler_params=_params("parallel", "parallel"),
    )(h, wt)


def _ffn_down(name, g, u, wd):
    j, m, n = g.shape
    d = wd.shape[2]
    tm = MM_TM

    def body(g_ref, u_ref, w_ref, o_ref):
        part = _dot(_silu(g_ref[0]) * u_ref[0], w_ref[0])

        @pl.when(pl.program_id(1) == 0)
        def _():
            o_ref[...] = part

        @pl.when(pl.program_id(1) > 0)
        def _():
            o_ref[...] += part

    act = pl.BlockSpec((1, tm, n), lambda i, b: (b, i, 0))
    return pl.pallas_call(
        body, name=name, grid=(m // tm, j),
        in_specs=[act, act, pl.BlockSpec((1, n, d), lambda i, b: (b, 0, 0))],
        out_specs=pl.BlockSpec((tm, d), lambda i, b: (i, 0)),
        out_shape=jax.ShapeDtypeStruct((m, d), F32), compiler_params=_params("parallel", "arbitrary"),
    )(g, u, wd)


def _ffn_down_bwd(name, dy, g, u, wd):
    j, m, n = g.shape
    d = wd.shape[2]
    tm = MM_TM

    def body(dy_ref, g_ref, u_ref, w_ref, dg_ref, du_ref):
        da = _dot_nt(dy_ref[...], w_ref[0])
        gv = g_ref[0]
        s = _sigmoid(gv)
        dg_ref[0] = da * u_ref[0] * (s * (1.0 + gv * (1.0 - s)))
        du_ref[0] = da * (gv * s)

    act = pl.BlockSpec((1, tm, n), lambda i, b: (b, i, 0))
    return pl.pallas_call(
        body, name=name, grid=(m // tm, j),
        in_specs=[pl.BlockSpec((tm, d), lambda i, b: (i, 0)), act, act, pl.BlockSpec((1, n, d), lambda i, b: (b, 0, 0))],
        out_specs=[act, act], out_shape=[jax.ShapeDtypeStruct((j, m, n), F32)] * 2,
        compiler_params=_params("parallel", "parallel"),
    )(dy, g, u, wd)


def _ffn_down_dw(name, g, u, dy):
    j, m, n = g.shape
    d = dy.shape[1]
    tn = _tile_of(d, 512)

    def body(g_ref, u_ref, dy_ref, o_ref):
        o_ref[0] = _dot_tn(_silu(g_ref[0]) * u_ref[0], dy_ref[...]).astype(BF16)

    act = pl.BlockSpec((1, m, n), lambda b, c: (b, 0, 0))
    return pl.pallas_call(
        body, name=name, grid=(j, d // tn),
        in_specs=[act, act, pl.BlockSpec((m, tn), lambda b, c: (0, c))],
        out_specs=pl.BlockSpec((1, n, tn), lambda b, c: (b, 0, c)),
        out_shape=jax.ShapeDtypeStruct((j, n, d), BF16), compiler_params=_params("parallel", "parallel"),
    )(g, u, dy)


def _ffn_up_dh(name, dg, du, wg, wu):
    j, m, n = dg.shape
    k = wg.shape[2]
    tm = MM_TM

    def body(dg_ref, du_ref, wg_ref, wu_ref, o_ref):
        part = _dot(dg_ref[0], wg_ref[0]) + _dot(du_ref[0], wu_ref[0])

        @pl.when(pl.program_id(1) == 0)
        def _():
            o_ref[...] = part

        @pl.when(pl.program_id(1) > 0)
        def _():
            o_ref[...] += part

    act = pl.BlockSpec((1, tm, n), lambda i, b: (b, i, 0))
    wsp = pl.BlockSpec((1, n, k), lambda i, b: (b, 0, 0))
    return pl.pallas_call(
        body, name=name, grid=(m // tm, j), in_specs=[act, act, wsp, wsp],
        out_specs=pl.BlockSpec((tm, k), lambda i, b: (i, 0)),
        out_shape=jax.ShapeDtypeStruct((m, k), F32), compiler_params=_params("parallel", "arbitrary"),
    )(dg, du, wg, wu)


def _ffn_up_dw(name, dy, h):
    j, m, n = dy.shape
    k = h.shape[1]
    tk = _tile_of(k, 512)

    def body(dy_ref, h_ref, o_ref):
        o_ref[0] = _dot_tn(dy_ref[0], h_ref[...]).astype(BF16)

    return pl.pallas_call(
        body, name=name, grid=(j, k // tk),
        in_specs=[pl.BlockSpec((1, m, n), lambda b, i: (b, 0, 0)), pl.BlockSpec((m, tk), lambda b, i: (0, i))],
        out_specs=pl.BlockSpec((1, n, tk), lambda b, i: (b, 0, i)),
        out_shape=jax.ShapeDtypeStruct((j, n, k), BF16), compiler_params=_params("parallel", "parallel"),
    )(dy, h)


def ffn(name, h, wg, wu, wd):
    @jax.custom_vjp
    def op(h, wg, wu, wd):
        return _ffn_down(name + "_d", _ffn_up(name + "_g", h, wg), _ffn_up(name + "_u", h, wu), wd)

    def op_fwd(h, wg, wu, wd):
        g, u = _ffn_up(name + "_g", h, wg), _ffn_up(name + "_u", h, wu)
        return _ffn_down(name + "_d", g, u, wd), (h, g, u, wg, wu, wd)

    def op_bwd(res, dy):
        h, g, u, wg, wu, wd = res
        dg, du = _ffn_down_bwd(name + "_d_bwd", dy, g, u, wd)
        return (_ffn_up_dh(name + "_dh", dg, du, wg, wu), _ffn_up_dw(name + "_g_dw", dg, h), _ffn_up_dw(name + "_u_dw", du, h),
                _ffn_down_dw(name + "_d_dw", g, u, dy))

    op.defvjp(op_fwd, op_bwd)
    return op(h, wg, wu, wd)


def _mmr_fwd(name, a, w):
    j, m, n = a.shape
    nn = w.shape[2]
    tm, tn = MM_TM, _tile_of(nn, 1024)

    def body(a_ref, w_ref, o_ref):
        part = _dot(a_ref[0], w_ref[0])

        @pl.when(pl.program_id(2) == 0)
        def _():
            o_ref[...] = part

        @pl.when(pl.program_id(2) > 0)
        def _():
            o_ref[...] += part

    return pl.pallas_call(
        body, name=name, grid=(m // tm, nn // tn, j),
        in_specs=[pl.BlockSpec((1, tm, n), lambda i, c, b: (b, i, 0)), pl.BlockSpec((1, n, tn), lambda i, c, b: (b, 0, c))],
        out_specs=pl.BlockSpec((tm, tn), lambda i, c, b: (i, c)),
        out_shape=jax.ShapeDtypeStruct((m, nn), F32),
        compiler_params=_params("parallel", "parallel", "arbitrary"),
    )(a, w)


def _mmr_da(name, dy, w):
    m, nn = dy.shape
    j, n, _ = w.shape
    tm = MM_TM

    def body(dy_ref, w_ref, o_ref):
        o_ref[0] = _dot_nt(dy_ref[...], w_ref[0])

    return pl.pallas_call(
        body, name=name, grid=(m // tm, j),
        in_specs=[pl.BlockSpec((tm, nn), lambda i, b: (i, 0)), pl.BlockSpec((1, n, nn), lambda i, b: (b, 0, 0))],
        out_specs=pl.BlockSpec((1, tm, n), lambda i, b: (b, i, 0)),
        out_shape=jax.ShapeDtypeStruct((j, m, n), F32),
        compiler_params=_params("parallel", "parallel"),
    )(dy, w)


def _mmr_dw(name, a, dy):
    j, m, n = a.shape
    nn = dy.shape[1]
    tn = _tile_of(nn, 512)

    def body(a_ref, dy_ref, o_ref):
        o_ref[0] = _dot_tn(a_ref[0], dy_ref[...]).astype(BF16)

    return pl.pallas_call(
        body, name=name, grid=(j, nn // tn),
        in_specs=[pl.BlockSpec((1, m, n), lambda b, c: (b, 0, 0)), pl.BlockSpec((m, tn), lambda b, c: (0, c))],
        out_specs=pl.BlockSpec((1, n, tn), lambda b, c: (b, 0, c)),
        out_shape=jax.ShapeDtypeStruct((j, n, nn), BF16),
        compiler_params=_params("parallel", "parallel"),
    )(a, dy)


def mm_rows(name, a, w):
    @jax.custom_vjp
    def op(a, w):
        return _mmr_fwd(name, a, w)

    def op_fwd(a, w):
        return op(a, w), (a, w)

    def op_bwd(res, dy):
        a, w = res
        return _mmr_da(name + "_da", dy, w), _mmr_dw(name + "_dw", a, dy)

    op.defvjp(op_fwd, op_bwd)
    return op(a, w)


def _colwise_specs(cols, pars, par_block):
    t = cols[0].shape[0]
    specs = [pl.BlockSpec((t, LANES), lambda j: (0, j)) for _ in cols]
    for p, blk in zip(pars, par_block):
        if blk == "lane":
            specs.append(pl.BlockSpec((p.shape[0], LANES), lambda j: (0, j)))
        else:
            specs.append(pl.BlockSpec((1,) + p.shape[1:], lambda j: (j, 0, 0)))
    return specs


def _colwise_fwd_call(name, f, cols, pars, par_block, n_out):
    t, c = cols[0].shape
    nc, npar = len(cols), len(pars)

    def body(*refs):
        ins = [r[...] for r in refs[:nc]] + [r[...] if b == "lane" else r[0] for r, b in zip(refs[nc:nc + npar], par_block)]
        res = f(*ins)
        for o_ref, o in zip(refs[nc + npar:], res):
            o_ref[...] = o

    return pl.pallas_call(
        body, name=name, grid=(c // LANES,),
        in_specs=_colwise_specs(cols, pars, par_block),
        out_specs=[pl.BlockSpec((t, LANES), lambda j: (0, j)) for _ in range(n_out)],
        out_shape=[jax.ShapeDtypeStruct((t, c), F32) for _ in range(n_out)],
        compiler_params=_params("parallel"),
    )(*cols, *pars)


def _colwise_bwd_call(name, f, cols, pars, par_block, cts):
    t, c = cols[0].shape
    nc, npar, nct = len(cols), len(pars), len(cts)

    def body(*refs):
        ins = [r[...] for r in refs[:nc]] + [r[...] if b == "lane" else r[0] for r, b in zip(refs[nc:nc + npar], par_block)]
        gs = tuple(r[...] for r in refs[nc + npar:nc + npar + nct])
        outs = refs[nc + npar + nct:]
        _, vjp = jax.vjp(f, *ins)
        d = vjp(gs)
        for o_ref, v in zip(outs[:nc], d[:nc]):
            o_ref[...] = v
        for o_ref, v, b in zip(outs[nc:], d[nc:], par_block):
            if b == "lane":
                o_ref[...] = v
            else:
                o_ref[0] = v

    res = pl.pallas_call(
        body, name=name, grid=(c // LANES,),
        in_specs=_colwise_specs(cols, pars, par_block) + [pl.BlockSpec((t, LANES), lambda j: (0, j)) for _ in cts],
        out_specs=_colwise_specs(cols, pars, par_block),
        out_shape=[jax.ShapeDtypeStruct(v.shape, F32) for v in (*cols, *pars)],
        compiler_params=_params("parallel"),
    )(*cols, *pars, *cts)
    return tuple(res[:nc]), tuple(res[nc:])


def colwise(name, f, cols, pars, par_block, n_out):
    @jax.custom_vjp
    def op(cols, pars):
        return tuple(_colwise_fwd_call(name, f, cols, pars, par_block, n_out))

    def op_fwd(cols, pars):
        return op(cols, pars), (cols, pars)

    def op_bwd(res, cts):
        return _colwise_bwd_call(name + "_bwd", f, res[0], res[1], par_block, tuple(cts))

    op.defvjp(op_fwd, op_bwd)
    return op(tuple(cols), tuple(pars))


def _rg_block(x, gate, cw, cb, wr, br, wi, bi, lam):
    xa = _causal_conv(x, cw) + cb
    r = _sigmoid(_dot(xa, wr) + br)
    i = _sigmoid(_dot(xa, wi) + bi)
    log_a = -RG_C * r * _softplus(-lam)
    a = jnp.exp(log_a)
    b = jnp.sqrt(_neg_expm1(2.0 * log_a)) * (i * xa)
    return (_lin_scan(a, b) * _gelu(gate),)


def _dn_conv_block(mode):
    def f(x, cw):
        c = _silu(_causal_conv(x, cw))
        if mode == "v":
            return (c,)
        c = c * lax.rsqrt(jnp.sum(c * c, axis=-1, keepdims=True) + EPS)
        return (c * (DN_HEAD_DIM ** -0.5),) if mode == "q" else (c,)
    return f


def _block_diag(w):
    w = w.reshape(8, 2, 64, 64)
    z = jnp.zeros((8, 64, 64), w.dtype)
    top = jnp.concatenate([w[:, 0], z], axis=2)
    bot = jnp.concatenate([z, w[:, 1]], axis=2)
    return jnp.concatenate([top, bot], axis=1)


DN_HP = 8


def _dn_block(S, qw, kw, vw, gb, h0):
    hp, hd = S.shape[0], DN_HEAD_DIM
    heads = lambda a: jnp.concatenate([a[None, :, j * hd:(j + 1) * hd] for j in range(hp)], axis=0)
    lane = lax.broadcasted_iota(jnp.int32, gb.shape, 1)
    col = lambda i: jnp.sum(jnp.where(lane == i, gb, 0.0), axis=1, keepdims=True)[None]
    beta = jnp.concatenate([col(h0 + j) for j in range(hp)], axis=0)
    g = jnp.concatenate([col(h0 + j + DN_HEADS) for j in range(hp)], axis=0)
    s_new, o = _dn_step(S, heads(qw), heads(kw), heads(vw), beta, g)
    return s_new, jnp.concatenate([o[j:j + 1].reshape(o.shape[1:]) for j in range(hp)], axis=1)


def _dn_step(S, q, k, v, beta, g):
    c = DN_CHUNK
    ri = lax.broadcasted_iota(jnp.int32, (c, c), 0)
    ci = lax.broadcasted_iota(jnp.int32, (c, c), 1)
    incl, strict = ri >= ci, ri > ci
    eye = (ri == ci).astype(F32)
    gam = _cumsum_rows(g)
    gam_row = jnp.sum(jnp.where(ri <= ci, g, 0.0), axis=-2, keepdims=True)
    gam_last = jnp.sum(g, axis=-2, keepdims=True)
    decay = jnp.where(incl, jnp.exp(jnp.where(incl, gam - gam_row, 0.0)), 0.0)
    kb = k * beta
    vb = v * beta
    a = jnp.where(strict, _dot_nt(kb, k) * decay, 0.0)
    p = -a
    tinv = eye + p
    for _ in range(5):
        p = _dot3(p, p)
        tinv = tinv + _dot3(tinv, p)
    e_gam = jnp.exp(gam)
    u0 = _dot3(tinv, vb)
    wk = _dot3(tinv, kb * e_gam)
    qk = jnp.where(incl, _dot_nt(q, k) * decay, 0.0)
    q_dec = q * e_gam
    k_dec = k * jnp.exp(gam_last - gam)
    u = u0 - _dot(wk, S)
    o = _dot(q_dec, S) + _dot(qk, u)
    s_new = S * jnp.exp(gam_last) + _dot_tn(k_dec, u)
    return s_new, o


def _dn_fwd_call(q, k, v, gb):
    t, w = q.shape
    n, hp, hd, c = t // DN_CHUNK, DN_HP, DN_HEAD_DIM, DN_CHUNK

    def body(q_ref, k_ref, v_ref, gb_ref, o_ref, s0_ref, s_scr):
        @pl.when(pl.program_id(1) == 0)
        def _():
            s_scr[...] = jnp.zeros_like(s_scr)

        s_old = s_scr[...]
        s0_ref[:, 0] = s_old
        s_new, o = _dn_block(s_old, q_ref[...], k_ref[...], v_ref[...], gb_ref[...], pl.program_id(0) * hp)
        o_ref[...] = o
        s_scr[...] = s_new

    blk = pl.BlockSpec((c, hp * hd), lambda g, i: (i, g))
    return pl.pallas_call(
        body, name="dn_core", grid=(DN_HEADS // hp, n),
        in_specs=[blk, blk, blk, pl.BlockSpec((c, LANES), lambda g, i: (i, 0))],
        out_specs=[blk, pl.BlockSpec((hp, 1, hd, hd), lambda g, i: (g, i, 0, 0))],
        out_shape=[jax.ShapeDtypeStruct((t, w), F32), jax.ShapeDtypeStruct((DN_HEADS, n, hd, hd), F32)],
        scratch_shapes=[pltpu.VMEM((hp, hd, hd), F32)],
        compiler_params=_params("parallel", "arbitrary"),
    )(q, k, v, gb)


def _dn_bwd_call(q, k, v, gb, s0, do):
    t, w = q.shape
    n, hp, hd, c = t // DN_CHUNK, DN_HP, DN_HEAD_DIM, DN_CHUNK
    ng = DN_HEADS // hp

    def body(q_ref, k_ref, v_ref, gb_ref, s0_ref, do_ref, dq_ref, dk_ref, dv_ref, dgb_ref, ds_scr):
        @pl.when(pl.program_id(1) == 0)
        def _():
            ds_scr[...] = jnp.zeros_like(ds_scr)

        h0 = pl.program_id(0) * hp
        _, vjp = jax.vjp(lambda *a: _dn_block(*a, h0), s0_ref[:, 0], q_ref[...], k_ref[...], v_ref[...], gb_ref[...])
        ds, dq, dk, dv, dgb = vjp((ds_scr[...], do_ref[...]))
        ds_scr[...] = ds
        dq_ref[...], dk_ref[...], dv_ref[...] = dq, dk, dv
        dgb_ref[0] = dgb

    blk = pl.BlockSpec((c, hp * hd), lambda g, i: (n - 1 - i, g))
    res = pl.pallas_call(
        body, name="dn_core_bwd", grid=(ng, n),
        in_specs=[blk, blk, blk, pl.BlockSpec((c, LANES), lambda g, i: (n - 1 - i, 0)),
                  pl.BlockSpec((hp, 1, hd, hd), lambda g, i: (g, n - 1 - i, 0, 0)), blk],
        out_specs=[blk, blk, blk, pl.BlockSpec((1, c, LANES), lambda g, i: (g, n - 1 - i, 0))],
        out_shape=[jax.ShapeDtypeStruct((t, w), F32)] * 3 + [jax.ShapeDtypeStruct((ng, t, LANES), F32)],
        scratch_shapes=[pltpu.VMEM((hp, hd, hd), F32)],
        compiler_params=_params("parallel", "arbitrary"),
    )(q, k, v, gb, s0, do)
    return res[0], res[1], res[2], jnp.sum(res[3], axis=0)


@jax.custom_vjp
def dn_core(q, k, v, gb):
    return _dn_fwd_call(q, k, v, gb)[0]


def _dn_core_fwd(q, k, v, gb):
    o, s0 = _dn_fwd_call(q, k, v, gb)
    return o, (q, k, v, gb, s0)


def _dn_core_bwd(res, do):
    return _dn_bwd_call(*res, do)


dn_core.defvjp(_dn_core_fwd, _dn_core_bwd)


ATT_GH = 4


def _att_block(q, kp, kc, vp, vc, qn, kn, slope, has_prev, dil):
    s = ATT_SPAN
    qh = _rms(q, qn) * (ATT_HEAD_DIM ** -0.5)
    qi = lax.broadcasted_iota(jnp.int32, (s, s), 0)
    kj = lax.broadcasted_iota(jnp.int32, (s, s), 1)
    d_p = qi + s - kj
    d_c = qi - kj
    s_p = _dot_nt(qh, _rms(kp, kn)) - slope * (d_p * dil).astype(F32)
    s_c = _dot_nt(qh, _rms(kc, kn)) - slope * (d_c * dil).astype(F32)
    s_p = jnp.where((d_p <= s) & (has_prev > 0), s_p, NEG_INF)
    s_c = jnp.where(d_c >= 0, s_c, NEG_INF)
    m = lax.stop_gradient(jnp.maximum(jnp.max(s_p, axis=-1, keepdims=True), jnp.max(s_c, axis=-1, keepdims=True)))
    p_p = jnp.exp(s_p - m)
    p_c = jnp.exp(s_c - m)
    den = jnp.sum(p_p, axis=-1, keepdims=True) + jnp.sum(p_c, axis=-1, keepdims=True)
    o = _dot(p_p / den, vp) + _dot(p_c / den, vc)
    lse = m + jnp.log(den)
    return o, jnp.broadcast_to(lse, o.shape)


def _att_heads(a):
    e = ATT_HEAD_DIM
    return jnp.concatenate([a[None, :, h * e:(h + 1) * e] for h in range(ATT_GH)], axis=0)


def _att_lanes(a):
    return jnp.concatenate([a[h:h + 1].reshape(a.shape[1:]) for h in range(ATT_GH)], axis=1)


def _att_rows(q, kp, kc, vp, vc, qn, kn, group, has_prev, dil):
    head = lax.broadcasted_iota(jnp.int32, (ATT_GH, 1, 1), 0) + (ATT_GH * group + 1)
    slope = jnp.exp(head.astype(F32) * (-8.0 / ATT_HEADS * math.log(2.0)))
    o, lse = _att_block(_att_heads(q), _att_heads(kp), _att_heads(kc), _att_heads(vp), _att_heads(vc), qn, kn, slope, has_prev, dil)
    return _att_lanes(o), _att_lanes(lse)


def _att_specs(group, dil):
    blk = (ATT_SPAN, ATT_GH * ATT_HEAD_DIM)
    cur = lambda which: pl.BlockSpec(blk, lambda r, n: (n, r * 9 + 3 * which + group))
    prev = lambda which: pl.BlockSpec(blk, lambda r, n: (jnp.maximum(n - 1, 0), r * 9 + 3 * which + group))
    out = pl.BlockSpec(blk, lambda r, n: (n, r))
    gain = pl.BlockSpec((ATT_GH, 1, ATT_HEAD_DIM), lambda r, n: (0, 0, 0))
    return [cur(0), prev(1), cur(1), prev(2), cur(2), gain, gain], out, gain


def _att_fwd_call(name, group, dil, pa, qn, kn):
    t = pa.shape[0]
    l = t // dil
    w = ATT_GH * ATT_HEAD_DIM
    ins, out, _ = _att_specs(group, dil)
    pav = pa.reshape(l, dil * pa.shape[1])

    def body(q_ref, kp_ref, kc_ref, vp_ref, vc_ref, qn_ref, kn_ref, o_ref, lse_ref):
        o_ref[...], lse_ref[...] = _att_rows(q_ref[...], kp_ref[...], kc_ref[...], vp_ref[...], vc_ref[...], qn_ref[...],
                                             kn_ref[...], group, pl.program_id(1), dil)

    o, lse = pl.pallas_call(
        body, name=name, grid=(dil, l // ATT_SPAN), in_specs=ins, out_specs=[out, out],
        out_shape=[jax.ShapeDtypeStruct((l, dil * w), F32)] * 2, compiler_params=_params("parallel", "arbitrary"),
    )(pav, pav, pav, pav, pav, qn, kn)
    return o.reshape(t, w), lse.reshape(t, w)


def _att_bwd_call(name, group, dil, pa, qn, kn, do, dlse):
    t = pa.shape[0]
    l = t // dil
    w = ATT_GH * ATT_HEAD_DIM
    ins, out, gain = _att_specs(group, dil)
    pav = pa.reshape(l, dil * pa.shape[1])

    def body(q_ref, kp_ref, kc_ref, vp_ref, vc_ref, qn_ref, kn_ref, do_ref, dlse_ref,
             dq_ref, dkp_ref, dkc_ref, dvp_ref, dvc_ref, dqn_ref, dkn_ref):
        has_prev = pl.program_id(1)
        _, vjp = jax.vjp(lambda *a: _att_rows(*a, group, has_prev, dil), q_ref[...], kp_ref[...], kc_ref[...], vp_ref[...],
                         vc_ref[...], qn_ref[...], kn_ref[...])
        dq, dkp, dkc, dvp, dvc, dqn, dkn = vjp((do_ref[...], dlse_ref[...]))
        dq_ref[...], dkp_ref[...], dkc_ref[...], dvp_ref[...], dvc_ref[...] = dq, dkp, dkc, dvp, dvc

        @pl.when((pl.program_id(0) == 0) & (pl.program_id(1) == 0))
        def _():
            dqn_ref[...] = jnp.zeros_like(dqn_ref)
            dkn_ref[...] = jnp.zeros_like(dkn_ref)

        dqn_ref[...] += dqn
        dkn_ref[...] += dkn

    res = pl.pallas_call(
        body, name=name + "_bwd", grid=(dil, l // ATT_SPAN), in_specs=ins + [out, out],
        out_specs=[out] * 5 + [gain, gain],
        out_shape=[jax.ShapeDtypeStruct((l, dil * w), F32)] * 5 + [jax.ShapeDtypeStruct(qn.shape, F32)] * 2,
        compiler_params=_params("arbitrary", "arbitrary"),
    )(pav, pav, pav, pav, pav, qn, kn, do.reshape(l, dil * w), dlse.reshape(l, dil * w))
    dq, dkp, dkc, dvp, dvc, dqn, dkn = res
    back = lambda g: jnp.pad(g[ATT_SPAN:], ((0, ATT_SPAN), (0, 0)))
    return dq.reshape(t, w), (dkc + back(dkp)).reshape(t, w), (dvc + back(dvp)).reshape(t, w), dqn, dkn


def _att_mix(o1, o2, o3, l1, l2, l3):
    m = jnp.maximum(jnp.maximum(l1, l2), l3)
    e1, e2, e3 = jnp.exp(l1 - m), jnp.exp(l2 - m), jnp.exp(l3 - m)
    s = e1 + e2 + e3
    return (jnp.concatenate([o1 * (e1 / s), o2 * (e2 / s), o3 * (e3 / s)], axis=1),)


def att_branch(name, pa, qn, kn):
    e = ATT_HEAD_DIM
    gains = lambda p, g: p[ATT_GH * g:ATT_GH * (g + 1)].reshape(ATT_GH, 1, e)

    @jax.custom_vjp
    def groups(pa, qn, kn):
        res = [_att_fwd_call(f"{name}_att{g}", g, dil, pa, gains(qn, g), gains(kn, g)) for g, (_, dil) in enumerate(ATT_GROUPS)]
        return tuple(r[0] for r in res) + tuple(r[1] for r in res)

    def groups_fwd(pa, qn, kn):
        return groups(pa, qn, kn), (pa, qn, kn)

    def groups_bwd(res, cts):
        pa, qn, kn = res
        n = len(ATT_GROUPS)
        parts = [_att_bwd_call(f"{name}_att{g}", g, dil, pa, gains(qn, g), gains(kn, g), cts[g], cts[n + g])
                 for g, (_, dil) in enumerate(ATT_GROUPS)]
        d_pa = jnp.concatenate([p[i] for i in range(3) for p in parts], axis=1)
        return (d_pa, jnp.concatenate([p[3] for p in parts]).reshape(qn.shape), jnp.concatenate([p[4] for p in parts]).reshape(kn.shape))

    groups.defvjp(groups_fwd, groups_bwd)
    return rowwise(f"{name}_attmix", _att_mix, groups(pa, qn, kn))[0]


def dn_gates(name, ba, a_log, dt_bias):
    place = lambda p: jnp.pad(p.reshape(1, DN_HEADS), ((0, 0), (DN_HEADS, LANES - 2 * DN_HEADS)))

    def f(x, al, dt):
        lane = lax.broadcasted_iota(jnp.int32, x.shape, 1)
        return (jnp.where(lane < DN_HEADS, _sigmoid(x), -jnp.exp(al) * _softplus(x + dt)),)

    return rowwise(name, f, (ba,), (place(a_log), place(dt_bias)))[0]


def _dn_out(o, z, g):
    parts = []
    for h in range(DN_HEADS):
        sl = slice(h * DN_HEAD_DIM, (h + 1) * DN_HEAD_DIM)
        parts.append(_rms(o[:, sl], g[:, sl]) * _silu(z[:, sl]))
    return (jnp.concatenate(parts, axis=1),)


def _merge(ml, za, zb, zc):
    d = D_MODEL
    return (_sigmoid(ml[:, :d]) * za + _sigmoid(ml[:, d:2 * d]) * zb + _sigmoid(ml[:, 2 * d:]) * zc,)


def add_norm(name, x, pend, scale, gain):
    if pend is None:
        return x, rowwise(name, lambda a, g: (_rms(a, g),), (x,), (gain,))[0]

    def f(a, b, g):
        s = a + scale * b
        return s, _rms(s, g)

    return rowwise(name, f, (x, pend), (gain,))


W_IN_PIECES = (("rgx", 0, 1024), ("gate", 1024, 1024), ("att", 2048, 2304), ("dq", 4352, 1024), ("dk", 5376, 1024),
               ("dv", 6400, 1024), ("dz", 7424, 1024), ("ba", 8448, 16), ("mrg", 8464, 3072))
RG_PAR_BLOCKS = ("lane", "lane", "blk", "lane", "blk", "lane", "lane")


def mixer(name, u, w, p):
    mm = lambda nm, a, wt: mm_rows(nm, a[None], wt[None])
    pr = {k: mm_cols(f"{name}_in_{k}", u, w["in_" + k][None])[0] for k, _, _ in W_IN_PIECES}
    ya = colwise(name + "_rg", _rg_block, (pr["rgx"], pr["gate"]),
                 (w["rg_conv_w"], p["rg_conv_b"], _block_diag(p["rg_w_r"]), p["rg_b_r"], _block_diag(p["rg_w_i"]),
                  p["rg_b_i"], p["rg_lambda"]), RG_PAR_BLOCKS, 1)[0]
    yb = att_branch(name, pr["att"], p["att_q_norm"], p["att_k_norm"])
    cw = w["dn_conv_w"]
    cq = colwise(name + "_dnq", _dn_conv_block("q"), (pr["dq"],), (cw[:, :1024],), ("lane",), 1)[0]
    ck = colwise(name + "_dnk", _dn_conv_block("k"), (pr["dk"],), (cw[:, 1024:2048],), ("lane",), 1)[0]
    cv = colwise(name + "_dnv", _dn_conv_block("v"), (pr["dv"],), (cw[:, 2048:],), ("lane",), 1)[0]
    gb = dn_gates(name + "_dngate", pr["ba"], p["dn_a_log"], p["dn_dt_bias"])
    o_dn = dn_core(cq, ck, cv, gb)
    yc = rowwise(name + "_dnout", _dn_out, (o_dn, pr["dz"]), (p["dn_out_norm"].reshape(1, D_MODEL),))[0]
    y = rowwise(name + "_merge", _merge, (pr["mrg"], mm(name + "_ba", ya, w["br_a"]), mm(name + "_bb", yb, w["br_b"]),
                                          mm(name + "_bc", yc, w["br_c"])))[0]
    return mm(name + "_out", y, w["w_out"])


def _loss_call(x, pend, target):
    t, d = x.shape
    tile = _row_tile(t)

    def body(x_ref, p_ref, t_ref, loss_ref, g_ref):
        err = x_ref[...] + 0.5 * p_ref[...] - t_ref[...]
        g_ref[...] = err * (1.0 / d)

        @pl.when(pl.program_id(0) == 0)
        def _():
            loss_ref[...] = jnp.zeros_like(loss_ref)

        loss_ref[...] += jnp.full(loss_ref.shape, 0.5 / d, F32) * jnp.sum(err * err)

    blk = pl.BlockSpec((tile, d), lambda i: (i, 0))
    loss, g = pl.pallas_call(
        body, name="loss", grid=(t // tile,), in_specs=[blk, blk, blk],
        out_specs=[pl.BlockSpec((8, LANES), lambda i: (0, 0)), blk],
        out_shape=[jax.ShapeDtypeStruct((8, LANES), F32), jax.ShapeDtypeStruct((t, d), F32)],
        compiler_params=_params("arbitrary"),
    )(x, pend, target)
    return loss[0, 0], g


@jax.custom_vjp
def loss_op(x, pend, target):
    return _loss_call(x, pend, target)[0]


def _loss_fwd(x, pend, target):
    loss, g = _loss_call(x, pend, target)
    return loss, g


def _loss_bwd(g, ct):
    return ct * g, (0.5 * ct) * g, None


loss_op.defvjp(_loss_fwd, _loss_bwd)


def local_loss(w, p, x, target):
    pend, scale = None, 0.0
    for l in range(len(w)):
        n = f"L{l}"
        x, h = add_norm(n + "_n1", x, pend, scale, p[l]["ffn1_norm"])
        pend, scale = ffn(n + "_f1", h, w[l]["ffn1_w_gate"], w[l]["ffn1_w_up"], w[l]["ffn1_w_down"]), 0.5
        x, h = add_norm(n + "_nm", x, pend, scale, p[l]["mix_norm"])
        pend, scale = mixer(n + "_mx", h, w[l], p[l]), 1.0
        x, h = add_norm(n + "_n2", x, pend, scale, p[l]["ffn2_norm"])
        pend, scale = ffn(n + "_f2", h, w[l]["ffn2_w_gate"], w[l]["ffn2_w_up"], w[l]["ffn2_w_down"]), 0.5
    return loss_op(x, pend, target)


WEIGHT_NAMES = ("ffn1_norm", "ffn1_w_gate", "ffn1_w_up", "ffn1_w_down", "mix_norm", "w_in", "rg_conv_w", "rg_conv_b",
                "rg_w_r", "rg_b_r", "rg_w_i", "rg_b_i", "rg_lambda", "att_q_norm", "att_k_norm", "dn_conv_w", "dn_a_log",
                "dn_dt_bias", "dn_out_norm", "w_branch", "w_out", "ffn2_norm", "ffn2_w_gate", "ffn2_w_up", "ffn2_w_down")
MATRICES = (("ffn1_w_gate", 2), ("ffn1_w_up", 2), ("ffn1_w_down", 1), ("w_in", 2), ("w_branch", 1), ("w_out", 1),
            ("ffn2_w_gate", 2), ("ffn2_w_up", 2), ("ffn2_w_down", 1))
CONVS = (("rg_conv_w", 2), ("dn_conv_w", 2))
SHARD_AXIS = dict(MATRICES + CONVS)
SMALL_NAMES = tuple(n for n in WEIGHT_NAMES if n not in SHARD_AXIS)
ROW_PARAMS = ("ffn1_norm", "mix_norm", "rg_conv_b", "rg_b_r", "rg_b_i", "rg_lambda", "ffn2_norm")
FFN_MATS = ("ffn1_w_gate", "ffn1_w_up", "ffn1_w_down", "ffn2_w_gate", "ffn2_w_up", "ffn2_w_down")
TRANSPOSED_MATS = ("ffn1_w_gate", "ffn1_w_up", "ffn2_w_gate", "ffn2_w_up")
W_IN_SHARD = 2884
FIRST_NEEDED = (("ffn1_w_gate", 0), ("ffn1_w_up", 0), ("ffn1_w_down", 0), ("w_in", 0))
LATE_MATS = ("ffn2_w_gate", "ffn2_w_up", "ffn2_w_down", "w_out", "w_branch")
EXCHANGE_GROUPS = (lambda n, l: l == 1 and n in LATE_MATS,
                   lambda n, l: (l == 1) != (n in LATE_MATS),
                   lambda n, l: l == 0 and n == "w_in",
                   lambda n, l: l == 0 and n not in LATE_MATS and n != "w_in")


def _shard_minor(a, axis):
    a = jnp.moveaxis(a, 0, axis)
    return a.reshape(a.shape[:axis] + (N_CHIPS * a.shape[axis + 1],) + a.shape[axis + 2:])


def _w_in_piece(g, off, n):
    s = W_IN_SHARD
    parts = [g[j][:, max(off, j * s) - j * s:min(off + n, (j + 1) * s) - j * s]
             for j in range(N_CHIPS) if max(off, j * s) < min(off + n, (j + 1) * s)]
    return jnp.concatenate(parts, axis=1) if len(parts) > 1 else parts[0]


def _w_in_chip_grad(gl, j):
    s = W_IN_SHARD
    parts = [gl["in_" + k][:, max(off, j * s) - off:min(off + n, (j + 1) * s) - off]
             for k, off, n in W_IN_PIECES if max(off, j * s) < min(off + n, (j + 1) * s)]
    return jnp.concatenate(parts, axis=1)


def layer_weights(g, conv, l):
    w = {n: g[n, l] for n in FFN_MATS}
    w["w_out"] = g["w_out", l].reshape(D_MODEL, D_MODEL)
    for k, off, n in W_IN_PIECES:
        piece = _w_in_piece(g["w_in", l], off, n)
        w["in_" + k] = jnp.pad(piece, ((0, 0), (0, LANES - n))) if n < LANES else piece
    wb = g["w_branch", l].reshape(-1, D_MODEL)
    w["br_a"], w["br_b"], w["br_c"] = wb[:1024], wb[1024:1792], wb[1792:]
    w["rg_conv_w"], w["dn_conv_w"] = conv["rg_conv_w"][l], conv["dn_conv_w"][l]
    return w


def layer_weight_grads(gw):
    out = {}
    for l, gl in enumerate(gw):
        for n in FFN_MATS:
            out[n, l] = gl[n]
        out["w_out", l] = gl["w_out"].reshape(N_CHIPS, -1, D_MODEL)
        out["w_branch", l] = jnp.concatenate([gl["br_a"], gl["br_b"], gl["br_c"]], axis=0).reshape(N_CHIPS, -1, D_MODEL)
        out["w_in", l] = jnp.stack([_w_in_chip_grad(gl, j) for j in range(N_CHIPS)])
    conv = {n: jnp.stack([gl[n] for gl in gw]) for n, _ in CONVS}
    return out, conv


def layer_small(small, l):
    p = {n: small[n][l] for n in SMALL_NAMES}
    for n in ROW_PARAMS:
        p[n] = small[n][l:l + 1]
    return p


def layer_small_grads(gp, small):
    return {n: jnp.stack([g[n] for g in gp]).reshape(small[n].shape) for n in SMALL_NAMES}


HBM_SPEC = pl.BlockSpec(memory_space=pl.ANY)


def _place():
    x, y, c = lax.axis_index("x"), lax.axis_index("y"), lax.axis_index("c")
    other_chips = [(1 - x, y), (x, 1 - y), (1 - x, 1 - y)]
    return x, y, c, 2 * x + y, (x, y, 1 - c), other_chips


def _half_rows(ref, lead, hc):
    hr = ref.shape[-2] // 2
    return ref.at[(*lead, pl.ds(pl.multiple_of(hc * hr, 16), hr), slice(None))]


def _chip_index():
    return (2 * lax.axis_index("x") + lax.axis_index("y")).astype(jnp.int32).reshape(1)


def cast_into_blocks(name, w):
    l, rows, cols = w.shape
    tr = rows // 2

    def body(me_ref, w_ref, *o_refs):
        for a, o_ref in enumerate(o_refs):
            o_ref[...] = w_ref[a:a + 1].astype(BF16)

    return pl.pallas_call(
        body, name=name, out_shape=[jax.ShapeDtypeStruct((N_CHIPS, rows, cols), BF16)] * l,
        grid_spec=pltpu.PrefetchScalarGridSpec(
            num_scalar_prefetch=1, grid=(rows // tr,),
            in_specs=[pl.BlockSpec((l, tr, cols), lambda i, me: (0, i, 0))],
            out_specs=[pl.BlockSpec((1, tr, cols), lambda i, me: (me[0], i, 0))] * l),
        compiler_params=_params("parallel"),
    )(_chip_index(), w)


def _gather_blocks(bufs_in, bufs_out, send_sems, recv_sems):
    n = len(bufs_in)
    x, y, c, me, sibling, chips = _place()

    def copy(s, src, dst, to):
        return pltpu.make_async_remote_copy(src_ref=src, dst_ref=dst, send_sem=send_sems.at[s], recv_sem=recv_sems.at[s],
                                            device_id=to, device_id_type=MESH)

    first, passed = [], []
    for j, (cx, cy) in enumerate(chips):
        for i in range(n):
            cp = copy(6 * i + j, _half_rows(bufs_in[i], (me,), c), _half_rows(bufs_out[i], (me,), c), (cx, cy, c))
            cp.start()
            first.append(cp)
    for j, (cx, cy) in enumerate(chips):
        k = 2 * cx + cy
        for i in range(n):
            copy(6 * i + j, _half_rows(bufs_in[i], (me,), c), _half_rows(bufs_out[i], (k,), c), (cx, cy, c)).wait_recv()
            cp = copy(6 * i + 3 + j, _half_rows(bufs_out[i], (k,), c), _half_rows(bufs_out[i], (k,), c), sibling)
            cp.start()
            passed.append(cp)
    for j, (cx, cy) in enumerate(chips):
        k = 2 * cx + cy
        for i in range(n):
            copy(6 * i + 3 + j, _half_rows(bufs_in[i], (me,), c), _half_rows(bufs_out[i], (k,), 1 - c), sibling).wait_recv()
    for cp in first + passed:
        cp.wait_send()


def _handshake(peers):
    barrier = pltpu.get_barrier_semaphore()
    for p in peers:
        pl.semaphore_signal(barrier, inc=1, device_id=p, device_id_type=MESH)
    pl.semaphore_wait(barrier, len(peers))


def allgather_blocks_sc(name, bufs, collective_id):
    n = len(bufs)
    refs = [jax.new_ref(b, memory_space=pltpu.MemorySpace.HBM) for b in bufs]

    @pl.kernel(mesh=plsc.ScalarSubcoreMesh(axis_name="sequencer", num_cores=1), name=name,
               scratch_types=(pltpu.SemaphoreType.DMA((6 * n,)), pltpu.SemaphoreType.DMA((6 * n,))),
               compiler_params=pltpu.CompilerParams(collective_id=collective_id))
    def launch(send_sems, recv_sems):
        x, y, c, me, sibling, chips = _place()
        _handshake([(cx, cy, c) for cx, cy in chips] + [sibling])
        _gather_blocks(refs, refs, send_sems, recv_sems)

    launch()
    return [jax.freeze(r) for r in refs]


def allgather_mats(bufs):
    n = len(bufs)

    def body(*refs):
        _gather_blocks(refs[:n], refs[n:2 * n], *refs[2 * n:])

    return pl.pallas_call(
        body, name="allgather_mats", out_shape=[jax.ShapeDtypeStruct(b.shape, b.dtype) for b in bufs],
        in_specs=[HBM_SPEC] * n, out_specs=[HBM_SPEC] * n, input_output_aliases={i: i for i in range(n)},
        scratch_shapes=[pltpu.SemaphoreType.DMA((6 * n,)), pltpu.SemaphoreType.DMA((6 * n,))],
    )(*bufs)


PEER_FLIPS = tuple((fx, fy, fc) for fx in (0, 1) for fy in (0, 1) for fc in (0, 1))[1:]


def exchange_pieces_sc(name, gs, collective_id):
    n = len(gs)

    def body(*refs):
        ins, outs = refs[:n], refs[n:2 * n]
        send_sems, recv_sems = refs[2 * n:]
        x, y, c, me, sibling, chips = _place()
        my_dev = 4 * x + 2 * y + c
        flip = lambda v, f: 1 - v if f else v
        peers = [(flip(x, fx), flip(y, fy), flip(c, fc)) for fx, fy, fc in PEER_FLIPS]
        _handshake(peers)
        sends = []
        for r, (px, py, pc) in enumerate(peers):
            for i in range(n):
                cp = pltpu.make_async_remote_copy(
                    src_ref=_half_rows(ins[i], (2 * px + py,), pc), dst_ref=outs[i].at[my_dev], send_sem=send_sems.at[7 * i + r],
                    recv_sem=recv_sems.at[7 * i + r], device_id=(px, py, pc), device_id_type=MESH)
                cp.start()
                sends.append(cp)
        for r, (px, py, pc) in enumerate(peers):
            for i in range(n):
                pltpu.make_async_remote_copy(
                    src_ref=_half_rows(ins[i], (me,), c), dst_ref=outs[i].at[4 * px + 2 * py + pc], send_sem=send_sems.at[7 * i + r],
                    recv_sem=recv_sems.at[7 * i + r], device_id=(px, py, pc), device_id_type=MESH).wait_recv()
        for cp in sends:
            cp.wait_send()

    return pl.kernel(
        body, name=name, mesh=plsc.ScalarSubcoreMesh(axis_name="sequencer", num_cores=1),
        out_type=[jax.ShapeDtypeStruct((N_DEV, g.shape[1] // 2, g.shape[2]), g.dtype) for g in gs],
        scratch_types=[pltpu.SemaphoreType.DMA((7 * n,)), pltpu.SemaphoreType.DMA((7 * n,))],
        compiler_params=pltpu.CompilerParams(collective_id=collective_id),
    )(*gs)


def sibling_share_halves(name, fs):
    n = len(fs)
    every = (slice(None),)

    def body(*refs):
        ins, outs = refs[:n], refs[n:2 * n]
        send_sems, recv_sems = refs[2 * n:]
        x, y, c, me, sibling, chips = _place()
        sends = []
        for i in range(n):
            cp = pltpu.make_async_remote_copy(src_ref=_half_rows(ins[i], every, c), dst_ref=_half_rows(outs[i], every, c),
                                              send_sem=send_sems.at[i], recv_sem=recv_sems.at[i], device_id=sibling, device_id_type=MESH)
            cp.start()
            sends.append(cp)
        for i in range(n):
            pltpu.make_async_remote_copy(src_ref=_half_rows(ins[i], every, c), dst_ref=_half_rows(outs[i], every, 1 - c),
                                         send_sem=send_sems.at[i], recv_sem=recv_sems.at[i], device_id=sibling,
                                         device_id_type=MESH).wait_recv()
        for cp in sends:
            cp.wait_send()

    return pl.pallas_call(
        body, name=name, out_shape=[jax.ShapeDtypeStruct(f.shape, f.dtype) for f in fs],
        in_specs=[HBM_SPEC] * n, out_specs=[HBM_SPEC] * n, input_output_aliases={i: i for i in range(n)},
        scratch_shapes=[pltpu.SemaphoreType.DMA((n,)), pltpu.SemaphoreType.DMA((n,))],
    )(*fs)


def allgather_small(name, v):
    m_per, n = v.shape

    def body(x_ref, out_ref, send_sems, recv_sems, local_sem):
        x, y, c, _, sibling, chips = _place()
        me = (x, y, c)

        def rows(px, py, pc):
            return out_ref.at[pl.ds((4 * px + 2 * py + pc) * m_per, m_per), :]

        def copy(k, block, to, src=None):
            return pltpu.make_async_remote_copy(src_ref=rows(*block) if src is None else src, dst_ref=rows(*block),
                                                send_sem=send_sems.at[k], recv_sem=recv_sems.at[k], device_id=to, device_id_type=MESH)

        mine = pltpu.make_async_copy(x_ref, rows(*me), local_sem)
        mine.start()
        first = [copy(0, me, sibling, src=x_ref)]
        first += [copy(1 + j, me, (*chip, c), src=x_ref) for j, chip in enumerate(chips)]
        for cp in first:
            cp.start()
        passed = [copy(4 + j, (*chip, c), sibling) for j, chip in enumerate(chips)]
        for j, chip in enumerate(chips):
            copy(1 + j, (*chip, c), me).wait_recv()
            passed[j].start()
        copy(0, sibling, me).wait_recv()
        for j, chip in enumerate(chips):
            copy(4 + j, (*chip, 1 - c), me).wait_recv()
        for cp in first + passed:
            cp.wait_send()
        mine.wait()

    return pl.pallas_call(
        body, name=name, out_shape=jax.ShapeDtypeStruct((N_DEV * m_per, n), v.dtype),
        in_specs=[pl.BlockSpec(memory_space=pltpu.VMEM)], out_specs=pl.BlockSpec(memory_space=pltpu.VMEM),
        scratch_shapes=[pltpu.SemaphoreType.DMA((7,)), pltpu.SemaphoreType.DMA((7,)), pltpu.SemaphoreType.DMA],
        compiler_params=pltpu.CompilerParams(vmem_limit_bytes=VMEM_LIMIT),
    )(v)


SUM_BLOCK_ELEMS = 512 * 1024


def sum_slabs(name, b):
    k, h, w = b.shape

    def body(b_ref, o_ref):
        acc = b_ref[0].astype(F32)
        for i in range(1, k):
            acc = acc + b_ref[i].astype(F32)
        o_ref[...] = acc

    return pl.pallas_call(
        body, name=name, out_shape=jax.ShapeDtypeStruct((h, w), F32),
        in_specs=[pl.BlockSpec(memory_space=pltpu.VMEM)], out_specs=pl.BlockSpec(memory_space=pltpu.VMEM),
        compiler_params=pltpu.CompilerParams(vmem_limit_bytes=VMEM_LIMIT),
    )(b)


def sum_pieces(name, pieces, gs):
    nl = len(pieces)
    k, h, w = pieces[0].shape
    tile = max(t for t in range(16, h + 1, 16) if h % t == 0 and (t * w <= SUM_BLOCK_ELEMS or t == 16))
    nt = h // tile
    x, y, c = lax.axis_index("x"), lax.axis_index("y"), lax.axis_index("c")
    place = [v.astype(jnp.int32).reshape(1) for v in (c, 2 * x + y, 4 * x + 2 * y + c)]

    assert nl == 2

    def tile_of(l, a, i):
        return i * a if l else i * (1 - a) + (nt - 1) * a

    def body(c_ref, me_ref, dev_ref, *refs):
        p_refs, g_refs, o_ref = refs[:nl], refs[nl:2 * nl], refs[2 * nl]
        my_dev = dev_ref[0]
        for l in range(nl):
            @pl.when(pl.program_id(0) == l)
            def _():
                o_ref[0] = jnp.zeros(o_ref.shape[1:], F32)
                for d in range(k):
                    @pl.when(my_dev == d)
                    def _():
                        o_ref[0] += g_refs[l][0].astype(F32)

                    @pl.when(my_dev != d)
                    def _():
                        o_ref[0] += p_refs[l][d].astype(F32)

    in_specs = [pl.BlockSpec((k, tile, w), functools.partial(lambda l, a, i, cc, me, dev: (0, tile_of(l, a, i), 0), l))
                for l in range(nl)]
    in_specs += [pl.BlockSpec((1, tile, w), functools.partial(lambda l, a, i, cc, me, dev: (me[0], cc[0] * nt + tile_of(l, a, i), 0), l))
                 for l in range(nl)]
    return pl.pallas_call(
        body, name=name, out_shape=jax.ShapeDtypeStruct((nl, 2 * h, w), F32),
        grid_spec=pltpu.PrefetchScalarGridSpec(
            num_scalar_prefetch=3, grid=(nl, nt), in_specs=in_specs,
            out_specs=pl.BlockSpec((1, tile, w), lambda a, i, cc, me, dev: (a, cc[0] * nt + i, 0))),
        compiler_params=_params("arbitrary", "arbitrary"),
    )(*place, *pieces, *gs)


def _adam_block(w, g, m, v):
    m = ADAM_B1 * m + (1.0 - ADAM_B1) * g
    v = ADAM_B2 * v + (1.0 - ADAM_B2) * (g * g)
    m_hat = m / (1.0 - ADAM_B1 ** ADAM_STEP)
    v_hat = v / (1.0 - ADAM_B2 ** ADAM_STEP)
    return -ADAM_LR * (m_hat / (jnp.sqrt(v_hat) + ADAM_EPS) + ADAM_WD * w), m, v


def adamw(name, w, g, m, v):
    shape = w.shape
    cols = shape[-1]
    rows = w.size // cols
    tile = 128 if rows % 128 == 0 else rows
    flat = [a.reshape(rows, cols) for a in (w, g, m, v)]

    def body(w_ref, g_ref, m_ref, v_ref, d_ref, nm_ref, nv_ref):
        d_ref[...], nm_ref[...], nv_ref[...] = _adam_block(w_ref[...], g_ref[...], m_ref[...], v_ref[...])

    blk = pl.BlockSpec((tile, cols), lambda i: (i, 0))
    res = pl.pallas_call(
        body, name=name, grid=(rows // tile,), in_specs=[blk] * 4, out_specs=[blk] * 3,
        out_shape=[jax.ShapeDtypeStruct((rows, cols), F32)] * 3, compiler_params=_params("parallel"),
    )(*flat)
    return tuple(r.reshape(shape) for r in res)


def _pack_small(values):
    flat = jnp.concatenate([v.reshape(-1) for v in values.values()])
    n = flat.shape[0]
    total = -(-n // (8 * LANES)) * (8 * LANES)
    return jnp.pad(flat, (0, total - n)).reshape(-1, LANES)


def _unpack_small(v, shapes):
    flat = v.reshape(-1)
    out, off = {}, 0
    for n, shape in shapes.items():
        sz = int(np.prod(shape))
        out[n] = flat[off:off + sz].reshape(shape)
        off += sz
    return out


def kernel(x, ffn1_norm, ffn1_w_gate, ffn1_w_up, ffn1_w_down, mix_norm, w_in, rg_conv_w, rg_conv_b, rg_w_r, rg_b_r, rg_w_i, rg_b_i, rg_lambda, att_q_norm, att_k_norm, dn_conv_w, dn_a_log, dn_dt_bias, dn_out_norm, w_branch, w_out, ffn2_norm, ffn2_w_gate, ffn2_w_up, ffn2_w_down, loss_target, m_ffn1_norm, m_ffn1_w_gate, m_ffn1_w_up, m_ffn1_w_down, m_mix_norm, m_w_in, m_rg_conv_w, m_rg_conv_b, m_rg_w_r, m_rg_b_r, m_rg_w_i, m_rg_b_i, m_rg_lambda, m_att_q_norm, m_att_k_norm, m_dn_conv_w, m_dn_a_log, m_dn_dt_bias, m_dn_out_norm, m_w_branch, m_w_out, m_ffn2_norm, m_ffn2_w_gate, m_ffn2_w_up, m_ffn2_w_down, v_ffn1_norm, v_ffn1_w_gate, v_ffn1_w_up, v_ffn1_w_down, v_mix_norm, v_w_in, v_rg_conv_w, v_rg_conv_b, v_rg_w_r, v_rg_b_r, v_rg_w_i, v_rg_b_i, v_rg_lambda, v_att_q_norm, v_att_k_norm, v_dn_conv_w, v_dn_a_log, v_dn_dt_bias, v_dn_out_norm, v_w_branch, v_w_out, v_ffn2_norm, v_ffn2_w_gate, v_ffn2_w_up, v_ffn2_w_down):
    given = dict(locals())
    for n in TRANSPOSED_MATS:
        for pre in ("", "m_", "v_"):
            given[pre + n] = jnp.swapaxes(given[pre + n], 1, 2)
    small = {n: given[n] for n in SMALL_NAMES}
    n_layers = ffn1_norm.shape[0]
    mat_names = [n for n, _ in MATRICES]
    conv_names = [n for n, _ in CONVS]

    blocks = {}
    for n in mat_names:
        for l, b in enumerate(cast_into_blocks("cast_" + n, given[n])):
            blocks[n, l] = b
    first = [k for k in blocks if k in FIRST_NEEDED]
    later = [k for k in blocks if k not in FIRST_NEEDED]
    first_blocks, later_blocks = lax.optimization_barrier((allgather_mats([blocks[k] for k in first]), [blocks[k] for k in later]))
    gathered = dict(zip(first, first_blocks))
    gathered.update(zip(later, allgather_blocks_sc("allgather_later", later_blocks, 1)))
    taps = jnp.concatenate([given[n].reshape(-1) for n in conv_names]).reshape(-1, LANES)
    taps = allgather_small("allgather_taps", taps).reshape(N_CHIPS, 2, -1)[:, 0]
    conv, off = {}, 0
    for n, ax in CONVS:
        sz = given[n].size
        conv[n] = _shard_minor(taps[:, off:off + sz].reshape((N_CHIPS,) + given[n].shape), ax)
        off += sz
    w = [layer_weights(gathered, conv, l) for l in range(n_layers)]
    p = [layer_small(small, l) for l in range(n_layers)]

    loss, (gw, gp, gx) = jax.value_and_grad(local_loss, argnums=(0, 1, 2))(w, p, x[0], loss_target[0])
    g_mats, g_conv = layer_weight_grads(gw)

    pieces = {}
    for i, group in enumerate(EXCHANGE_GROUPS):
        keys = [k for k in g_mats if group(*k)]
        pieces.update(zip(keys, exchange_pieces_sc(f"exchange_{i}", [g_mats[k] for k in keys], 2 + i)))
    halves = {n: sum_pieces("sum_" + n, [pieces[n, l] for l in range(n_layers)], [g_mats[n, l] for l in range(n_layers)])
              for n in mat_names}
    grads = {}
    for tag, names in (("late", [n for n in mat_names if n in LATE_MATS]), ("early", [n for n in mat_names if n not in LATE_MATS])):
        grads.update(zip(names, sibling_share_halves("share_" + tag, [halves[n] for n in names])))

    g_small = dict(layer_small_grads(gp, small), **g_conv, loss=loss.reshape(1))
    packed_small = _pack_small(g_small)
    slabs = allgather_small("allgather_small", packed_small).reshape(N_DEV, packed_small.shape[0], LANES)
    summed = _unpack_small(sum_slabs("sum_small", slabs), {n: g.shape for n, g in g_small.items()})
    chip = 2 * lax.axis_index("x") + lax.axis_index("y")
    for n in SMALL_NAMES:
        grads[n] = summed[n]
    for n, ax in CONVS:
        s = given[n].shape[ax]
        grads[n] = lax.dynamic_slice_in_dim(summed[n], chip * s, s, axis=ax)

    upd = {n: adamw("adamw_" + n, given[n], grads[n], given["m_" + n], given["v_" + n]) for n in WEIGHT_NAMES}
    out = lambda n, a: jnp.swapaxes(a, 1, 2) if n in TRANSPOSED_MATS else a
    return (summed["loss"][0], gx[None], *[out(n, grads[n]) for n in WEIGHT_NAMES], *[out(n, upd[n][0]) for n in WEIGHT_NAMES],
            *[out(n, upd[n][1]) for n in WEIGHT_NAMES], *[out(n, upd[n][2]) for n in WEIGHT_NAMES])
```

```python
import functools
import math

import jax
import jax.numpy as jnp
import numpy as np
from jax import lax
from jax.experimental import pallas as pl
from jax.experimental.pallas import tpu as pltpu
from jax.experimental.pallas import tpu_sc as plsc

F32 = jnp.float32
BF16 = jnp.bfloat16
MESH = pl.DeviceIdType.MESH

D_MODEL = 1024
FFN_DIM = 2816
RG_C = 8.0
ATT_GROUPS = ((128, 1), (512, 4), (2048, 16))
ATT_HEADS = 12
ATT_HEAD_DIM = 64
ATT_SPAN = 128
DN_HEADS = 8
DN_HEAD_DIM = 128
DN_CHUNK = 64
EPS = 1e-6
NEG_INF = -1e30
N_CHIPS = 4
N_DEV = 8

ADAM_LR, ADAM_B1, ADAM_B2, ADAM_EPS, ADAM_WD, ADAM_STEP = 0.001, 0.9, 0.999, 1e-08, 0.01, 10

LANES = 128
VMEM_LIMIT = 56 * 1024 * 1024


def _params(*sem):
    return pltpu.CompilerParams(dimension_semantics=sem or None, vmem_limit_bytes=VMEM_LIMIT)


def _sigmoid(x):
    return 1.0 / (1.0 + jnp.exp(-x))


def _silu(x):
    return x * _sigmoid(x)


def _softplus(x):
    return jnp.maximum(x, 0.0) + jnp.log(1.0 + jnp.exp(-jnp.abs(x)))


def _gelu(x):
    return 0.5 * x * (1.0 + jnp.tanh(math.sqrt(2.0 / math.pi) * (x + 0.044715 * (x * x * x))))


def _neg_expm1(x):
    series = -x * (1.0 + x * (0.5 + x * (1.0 / 6 + x * (1.0 / 24 + x * (1.0 / 120 + x * (1.0 / 720))))))
    return jnp.where(x > -0.25, series, 1.0 - jnp.exp(x))


def _rms(x, g):
    return x * lax.rsqrt(jnp.mean(x * x, axis=-1, keepdims=True) + EPS) * g


_MM_DIMS = {"nn": (((1,), (0,)), ((), ())), "nt": (((1,), (1,)), ((), ())), "tn": (((0,), (0,)), ((), ()))}


def _split(a):
    hi = a.astype(BF16)
    return hi, (a - hi.astype(F32)).astype(BF16)


def _mxu(a, b, form, passes):
    (ca, cb), _ = _MM_DIMS[form]
    if a.ndim == 3:
        dims = (((ca[0] + 1,), (cb[0] + 1,)), ((0,), (0,)))
    else:
        dims = _MM_DIMS[form]
    dg = lambda p, q: lax.dot_general(p, q, dims, preferred_element_type=F32)
    if passes == 1:
        return dg(a.astype(BF16), b.astype(BF16))
    (a_hi, a_lo), (b_hi, b_lo) = _split(a), _split(b)
    return dg(a_hi, b_hi) + (dg(a_hi, b_lo) + dg(a_lo, b_hi))


@functools.partial(jax.custom_vjp, nondiff_argnums=(2, 3))
def _mm(a, b, form, passes):
    return _mxu(a, b, form, passes)


def _mm_fwd(a, b, form, passes):
    return _mxu(a, b, form, passes), (a, b)


def _mm_bwd(form, passes, res, g):
    a, b = res
    if form == "nn":
        return _mm(g, b, "nt", passes), _mm(a, g, "tn", passes)
    if form == "nt":
        return _mm(g, b, "nn", passes), _mm(g, a, "tn", passes)
    return _mm(b, g, "nt", passes), _mm(a, g, "nn", passes)


_mm.defvjp(_mm_fwd, _mm_bwd)


def _dot(a, b):
    return _mm(a, b, "nn", 1)


def _dot_nt(a, b):
    return _mm(a, b, "nt", 1)


def _dot_tn(a, b):
    return _mm(a, b, "tn", 1)


def _dot3(a, b):
    return _mm(a, b, "nn", 3)


def _rows(shape):
    return lax.broadcasted_iota(jnp.int32, shape, len(shape) - 2)


def _roll_down(x, s, fill):
    return jnp.where(_rows(x.shape) >= s, pltpu.roll(x, s, x.ndim - 2), fill)


def _roll_up(x, s, fill):
    n = x.shape[-2]
    return jnp.where(_rows(x.shape) < n - s, pltpu.roll(x, n - s, x.ndim - 2), fill)


@functools.partial(jax.custom_vjp, nondiff_argnums=(1,))
def _shift(x, s):
    return _roll_down(x, s, 0.0)


def _shift_fwd(x, s):
    return _roll_down(x, s, 0.0), None


def _shift_bwd(s, _, g):
    return (_roll_up(g, s, 0.0),)


_shift.defvjp(_shift_fwd, _shift_bwd)


def _causal_conv(x, w):
    return w[0:1] * _shift(x, 3) + w[1:2] * _shift(x, 2) + w[2:3] * _shift(x, 1) + w[3:4] * x


@jax.custom_vjp
def _lin_scan(a, b):
    return _lin_scan_fwd(a, b)[0]


def _lin_scan_fwd(a, b):
    a0 = a
    s = 1
    while s < a.shape[0]:
        b = a * _roll_down(b, s, 0.0) + b
        a = a * _roll_down(a, s, 1.0)
        s *= 2
    return b, (a0, b)


def _lin_scan_bwd(res, g):
    a, h = res
    c = _roll_up(a, 1, 0.0)
    s = 1
    while s < a.shape[0]:
        g = c * _roll_up(g, s, 0.0) + g
        c = c * _roll_up(c, s, 1.0)
        s *= 2
    return g * _roll_down(h, 1, 0.0), g


_lin_scan.defvjp(_lin_scan_fwd, _lin_scan_bwd)


@jax.custom_vjp
def _cumsum_rows(x):
    s = 1
    while s < x.shape[-2]:
        x = x + _roll_down(x, s, 0.0)
        s *= 2
    return x


def _cumsum_rows_fwd(x):
    return _cumsum_rows(x), None


def _cumsum_rows_bwd(_, g):
    s = 1
    while s < g.shape[-2]:
        g = g + _roll_up(g, s, 0.0)
        s *= 2
    return (g,)


_cumsum_rows.defvjp(_cumsum_rows_fwd, _cumsum_rows_bwd)


ROW_BLOCK_BYTES = 14 * 1024 * 1024


def _row_tile(t, width=0):
    for tile in (512, 256):
        if t % tile == 0 and (tile == 256 or tile * width * 4 <= ROW_BLOCK_BYTES):
            return tile
    return t


def _rowwise_fwd_call(name, f, rows, pars, tile):
    t = rows[0].shape[0]
    outs = jax.eval_shape(f, *[jax.ShapeDtypeStruct((tile, r.shape[1]), F32) for r in rows],
                          *[jax.ShapeDtypeStruct(p.shape, F32) for p in pars])
    nr, npar = len(rows), len(pars)

    def body(*refs):
        ins = [r[...] for r in refs[:nr + npar]]
        res = f(*ins)
        for o_ref, o in zip(refs[nr + npar:], res):
            o_ref[...] = o.astype(o_ref.dtype)

    return pl.pallas_call(
        body, name=name, grid=(t // tile,),
        in_specs=[pl.BlockSpec((tile, r.shape[1]), lambda i: (i, 0)) for r in rows]
        + [pl.BlockSpec(p.shape, lambda i: (0, 0)) for p in pars],
        out_specs=[pl.BlockSpec((tile, o.shape[1]), lambda i: (i, 0)) for o in outs],
        out_shape=[jax.ShapeDtypeStruct((t, o.shape[1]), F32) for o in outs],
        compiler_params=_params("parallel"),
    )(*rows, *pars)


def _rowwise_bwd_call(name, f, rows, pars, cts, tile):
    t = rows[0].shape[0]
    nr, npar, nct = len(rows), len(pars), len(cts)

    def body(*refs):
        ins = [r[...] for r in refs[:nr + npar]]
        gs = tuple(r[...] for r in refs[nr + npar:nr + npar + nct])
        outs = refs[nr + npar + nct:]
        _, vjp = jax.vjp(f, *ins)
        d = vjp(gs)
        for o_ref, v in zip(outs[:nr], d[:nr]):
            o_ref[...] = v

        @pl.when(pl.program_id(0) == 0)
        def _():
            for o_ref in outs[nr:]:
                o_ref[...] = jnp.zeros_like(o_ref)

        for o_ref, v in zip(outs[nr:], d[nr:]):
            o_ref[...] += v

    res = pl.pallas_call(
        body, name=name, grid=(t // tile,),
        in_specs=[pl.BlockSpec((tile, r.shape[1]), lambda i: (i, 0)) for r in rows]
        + [pl.BlockSpec(p.shape, lambda i: (0, 0)) for p in pars]
        + [pl.BlockSpec((tile, c.shape[1]), lambda i: (i, 0)) for c in cts],
        out_specs=[pl.BlockSpec((tile, r.shape[1]), lambda i: (i, 0)) for r in rows]
        + [pl.BlockSpec(p.shape, lambda i: (0, 0)) for p in pars],
        out_shape=[jax.ShapeDtypeStruct(r.shape, F32) for r in rows]
        + [jax.ShapeDtypeStruct(p.shape, F32) for p in pars],
        compiler_params=_params("arbitrary"),
    )(*rows, *pars, *cts)
    return tuple(res[:nr]), tuple(res[nr:])


def rowwise(name, f, rows, pars=()):
    outs = jax.eval_shape(f, *[jax.ShapeDtypeStruct((8, r.shape[1]), F32) for r in rows],
                          *[jax.ShapeDtypeStruct(p.shape, F32) for p in pars])
    tile = _row_tile(rows[0].shape[0], 2 * sum(r.shape[1] for r in rows) + sum(o.shape[1] for o in outs))

    @jax.custom_vjp
    def op(rows, pars):
        return tuple(_rowwise_fwd_call(name, f, rows, pars, tile))

    def op_fwd(rows, pars):
        return op(rows, pars), (rows, pars)

    def op_bwd(res, cts):
        return _rowwise_bwd_call(name + "_bwd", f, res[0], res[1], tuple(cts), tile)

    op.defvjp(op_fwd, op_bwd)
    return op(tuple(rows), tuple(pars))


MM_TM = 512


def _tile_of(n, cap):
    best = None
    for c in range(LANES, min(n, cap) + 1, LANES):
        if n % c == 0:
            best = c
    return best or n


def _mmc_fwd(name, h, w):
    m, k = h.shape
    j, _, n = w.shape
    tm, tn = MM_TM, _tile_of(n, 1408)

    def body(h_ref, w_ref, o_ref):
        o_ref[0] = _dot(h_ref[...], w_ref[0])

    return pl.pallas_call(
        body, name=name, grid=(m // tm, j, n // tn),
        in_specs=[pl.BlockSpec((tm, k), lambda i, b, c: (i, 0)), pl.BlockSpec((1, k, tn), lambda i, b, c: (b, 0, c))],
        out_specs=pl.BlockSpec((1, tm, tn), lambda i, b, c: (b, i, c)),
        out_shape=jax.ShapeDtypeStruct((j, m, n), F32),
        compiler_params=_params("parallel", "parallel", "parallel"),
    )(h, w)


def _mmc_dh(name, dy, w):
    j, m, n = dy.shape
    k = w.shape[1]
    tm, tn = MM_TM, _tile_of(n, 1408)

    def body(dy_ref, w_ref, o_ref):
        part = _dot_nt(dy_ref[0], w_ref[0])

        @pl.when((pl.program_id(1) == 0) & (pl.program_id(2) == 0))
        def _():
            o_ref[...] = part

        @pl.when((pl.program_id(1) > 0) | (pl.program_id(2) > 0))
        def _():
            o_ref[...] += part

    return pl.pallas_call(
        body, name=name, grid=(m // tm, j, n // tn),
        in_specs=[pl.BlockSpec((1, tm, tn), lambda i, b, c: (b, i, c)), pl.BlockSpec((1, k, tn), lambda i, b, c: (b, 0, c))],
        out_specs=pl.BlockSpec((tm, k), lambda i, b, c: (i, 0)),
        out_shape=jax.ShapeDtypeStruct((m, k), F32),
        compiler_params=_params("parallel", "arbitrary", "arbitrary"),
    )(dy, w)


def _mmc_dw(name, h, dy):
    m, k = h.shape
    j, _, n = dy.shape
    tk, tn = _tile_of(k, 512), _tile_of(n, 1152)

    def body(h_ref, dy_ref, o_ref):
        o_ref[0] = _dot_tn(h_ref[...], dy_ref[0]).astype(BF16)

    return pl.pallas_call(
        body, name=name, grid=(j, k // tk, n // tn),
        in_specs=[pl.BlockSpec((m, tk), lambda b, i, c: (0, i)), pl.BlockSpec((1, m, tn), lambda b, i, c: (b, 0, c))],
        out_specs=pl.BlockSpec((1, tk, tn), lambda b, i, c: (b, i, c)),
        out_shape=jax.ShapeDtypeStruct((j, k, n), BF16),
        compiler_params=_params("parallel", "parallel", "parallel"),
    )(h, dy)


def mm_cols(name, h, w):
    @jax.custom_vjp
    def op(h, w):
        return _mmc_fwd(name, h, w)

    def op_fwd(h, w):
        return op(h, w), (h, w)

    def op_bwd(res, dy):
        h, w = res
        return _mmc_dh(name + "_dh", dy, w), _mmc_dw(name + "_dw", h, dy)

    op.defvjp(op_fwd, op_bwd)
    return op(h, w)


def _ffn_up(name, h, wt):
    m, k = h.shape
    j, n, _ = wt.shape
    tm = MM_TM

    def body(h_ref, w_ref, o_ref):
        o_ref[0] = _dot_nt(h_ref[...], w_ref[0])

    return pl.pallas_call(
        body, name=name, grid=(m // tm, j),
        in_specs=[pl.BlockSpec((tm, k), lambda i, b: (i, 0)), pl.BlockSpec((1, n, k), lambda i, b: (b, 0, 0))],
        out_specs=pl.BlockSpec((1, tm, n), lambda i, b: (b, i, 0)),
        out_shape=jax.ShapeDtypeStruct((j, m, n), F32), compiler_params=_params("parallel", "parallel"),
    )(h, wt)


def _ffn_down(name, g, u, wd):
    j, m, n = g.shape
    d = wd.shape[2]
    tm = MM_TM

    def body(g_ref, u_ref, w_ref, o_ref):
        part = _dot(_silu(g_ref[0]) * u_ref[0], w_ref[0])

        @pl.when(pl.program_id(1) == 0)
        def _():
            o_ref[...] = part

        @pl.when(pl.program_id(1) > 0)
        def _():
            o_ref[...] += part

    act = pl.BlockSpec((1, tm, n), lambda i, b: (b, i, 0))
    return pl.pallas_call(
        body, name=name, grid=(m // tm, j),
        in_specs=[act, act, pl.BlockSpec((1, n, d), lambda i, b: (b, 0, 0))],
        out_specs=pl.BlockSpec((tm, d), lambda i, b: (i, 0)),
        out_shape=jax.ShapeDtypeStruct((m, d), F32), compiler_params=_params("parallel", "arbitrary"),
    )(g, u, wd)


def _ffn_down_bwd(name, dy, g, u, wd):
    j, m, n = g.shape
    d = wd.shape[2]
    tm = MM_TM

    def body(dy_ref, g_ref, u_ref, w_ref, dg_ref, du_ref):
        da = _dot_nt(dy_ref[...], w_ref[0])
        gv = g_ref[0]
        s = _sigmoid(gv)
        dg_ref[0] = da * u_ref[0] * (s * (1.0 + gv * (1.0 - s)))
        du_ref[0] = da * (gv * s)

    act = pl.BlockSpec((1, tm, n), lambda i, b: (b, i, 0))
    return pl.pallas_call(
        body, name=name, grid=(m // tm, j),
        in_specs=[pl.BlockSpec((tm, d), lambda i, b: (i, 0)), act, act, pl.BlockSpec((1, n, d), lambda i, b: (b, 0, 0))],
        out_specs=[act, act], out_shape=[jax.ShapeDtypeStruct((j, m, n), F32)] * 2,
        compiler_params=_params("parallel", "parallel"),
    )(dy, g, u, wd)


def _ffn_down_dw(name, g, u, dy):
    j, m, n = g.shape
    d = dy.shape[1]
    tn = _tile_of(d, 512)

    def body(g_ref, u_ref, dy_ref, o_ref):
        o_ref[0] = _dot_tn(_silu(g_ref[0]) * u_ref[0], dy_ref[...]).astype(BF16)

    act = pl.BlockSpec((1, m, n), lambda b, c: (b, 0, 0))
    return pl.pallas_call(
        body, name=name, grid=(j, d // tn),
        in_specs=[act, act, pl.BlockSpec((m, tn), lambda b, c: (0, c))],
        out_specs=pl.BlockSpec((1, n, tn), lambda b, c: (b, 0, c)),
        out_shape=jax.ShapeDtypeStruct((j, n, d), BF16), compiler_params=_params("parallel", "parallel"),
    )(g, u, dy)


def _ffn_up_dh(name, dg, du, wg, wu):
    j, m, n = dg.shape
    k = wg.shape[2]
    tm = MM_TM

    def body(dg_ref, du_ref, wg_ref, wu_ref, o_ref):
        part = _dot(dg_ref[0], wg_ref[0]) + _dot(du_ref[0], wu_ref[0])

        @pl.when(pl.program_id(1) == 0)
        def _():
            o_ref[...] = part

        @pl.when(pl.program_id(1) > 0)
        def _():
            o_ref[...] += part

    act = pl.BlockSpec((1, tm, n), lambda i, b: (b, i, 0))
    wsp = pl.BlockSpec((1, n, k), lambda i, b: (b, 0, 0))
    return pl.pallas_call(
        body, name=name, grid=(m // tm, j), in_specs=[act, act, wsp, wsp],
        out_specs=pl.BlockSpec((tm, k), lambda i, b: (i, 0)),
        out_shape=jax.ShapeDtypeStruct((m, k), F32), compiler_params=_params("parallel", "arbitrary"),
    )(dg, du, wg, wu)


def _ffn_up_dw(name, dy, h):
    j, m, n = dy.shape
    k = h.shape[1]
    tk = _tile_of(k, 512)

    def body(dy_ref, h_ref, o_ref):
        o_ref[0] = _dot_tn(dy_ref[0], h_ref[...]).astype(BF16)

    return pl.pallas_call(
        body, name=name, grid=(j, k // tk),
        in_specs=[pl.BlockSpec((1, m, n), lambda b, i: (b, 0, 0)), pl.BlockSpec((m, tk), lambda b, i: (0, i))],
        out_specs=pl.BlockSpec((1, n, tk), lambda b, i: (b, 0, i)),
        out_shape=jax.ShapeDtypeStruct((j, n, k), BF16), compiler_params=_params("parallel", "parallel"),
    )(dy, h)


def ffn(name, h, wg, wu, wd):
    @jax.custom_vjp
    def op(h, wg, wu, wd):
        return _ffn_down(name + "_d", _ffn_up(name + "_g", h, wg), _ffn_up(name + "_u", h, wu), wd)

    def op_fwd(h, wg, wu, wd):
        g, u = _ffn_up(name + "_g", h, wg), _ffn_up(name + "_u", h, wu)
        return _ffn_down(name + "_d", g, u, wd), (h, g, u, wg, wu, wd)

    def op_bwd(res, dy):
        h, g, u, wg, wu, wd = res
        dg, du = _ffn_down_bwd(name + "_d_bwd", dy, g, u, wd)
        return (_ffn_up_dh(name + "_dh", dg, du, wg, wu), _ffn_up_dw(name + "_g_dw", dg, h), _ffn_up_dw(name + "_u_dw", du, h),
                _ffn_down_dw(name + "_d_dw", g, u, dy))

    op.defvjp(op_fwd, op_bwd)
    return op(h, wg, wu, wd)


def _mmr_fwd(name, a, w):
    j, m, n = a.shape
    nn = w.shape[2]
    tm, tn = MM_TM, _tile_of(nn, 1024)

    def body(a_ref, w_ref, o_ref):
        part = _dot(a_ref[0], w_ref[0])

        @pl.when(pl.program_id(2) == 0)
        def _():
            o_ref[...] = part

        @pl.when(pl.program_id(2) > 0)
        def _():
            o_ref[...] += part

    return pl.pallas_call(
        body, name=name, grid=(m // tm, nn // tn, j),
        in_specs=[pl.BlockSpec((1, tm, n), lambda i, c, b: (b, i, 0)), pl.BlockSpec((1, n, tn), lambda i, c, b: (b, 0, c))],
        out_specs=pl.BlockSpec((tm, tn), lambda i, c, b: (i, c)),
        out_shape=jax.ShapeDtypeStruct((m, nn), F32),
        compiler_params=_params("parallel", "parallel", "arbitrary"),
    )(a, w)


def _mmr_da(name, dy, w):
    m, nn = dy.shape
    j, n, _ = w.shape
    tm = MM_TM

    def body(dy_ref, w_ref, o_ref):
        o_ref[0] = _dot_nt(dy_ref[...], w_ref[0])

    return pl.pallas_call(
        body, name=name, grid=(m // tm, j),
        in_specs=[pl.BlockSpec((tm, nn), lambda i, b: (i, 0)), pl.BlockSpec((1, n, nn), lambda i, b: (b, 0, 0))],
        out_specs=pl.BlockSpec((1, tm, n), lambda i, b: (b, i, 0)),
        out_shape=jax.ShapeDtypeStruct((j, m, n), F32),
        compiler_params=_params("parallel", "parallel"),
    )(dy, w)


def _mmr_dw(name, a, dy):
    j, m, n = a.shape
    nn = dy.shape[1]
    tn = _tile_of(nn, 512)

    def body(a_ref, dy_ref, o_ref):
        o_ref[0] = _dot_tn(a_ref[0], dy_ref[...]).astype(BF16)

    return pl.pallas_call(
        body, name=name, grid=(j, nn // tn),
        in_specs=[pl.BlockSpec((1, m, n), lambda b, c: (b, 0, 0)), pl.BlockSpec((m, tn), lambda b, c: (0, c))],
        out_specs=pl.BlockSpec((1, n, tn), lambda b, c: (b, 0, c)),
        out_shape=jax.ShapeDtypeStruct((j, n, nn), BF16),
        compiler_params=_params("parallel", "parallel"),
    )(a, dy)


def mm_rows(name, a, w):
    @jax.custom_vjp
    def op(a, w):
        return _mmr_fwd(name, a, w)

    def op_fwd(a, w):
        return op(a, w), (a, w)

    def op_bwd(res, dy):
        a, w = res
        return _mmr_da(name + "_da", dy, w), _mmr_dw(name + "_dw", a, dy)

    op.defvjp(op_fwd, op_bwd)
    return op(a, w)


def _colwise_specs(cols, pars, par_block):
    t = cols[0].shape[0]
    specs = [pl.BlockSpec((t, LANES), lambda j: (0, j)) for _ in cols]
    for p, blk in zip(pars, par_block):
        if blk == "lane":
            specs.append(pl.BlockSpec((p.shape[0], LANES), lambda j: (0, j)))
        else:
            specs.append(pl.BlockSpec((1,) + p.shape[1:], lambda j: (j, 0, 0)))
    return specs


def _colwise_fwd_call(name, f, cols, pars, par_block, n_out):
    t, c = cols[0].shape
    nc, npar = len(cols), len(pars)

    def body(*refs):
        ins = [r[...] for r in refs[:nc]] + [r[...] if b == "lane" else r[0] for r, b in zip(refs[nc:nc + npar], par_block)]
        res = f(*ins)
        for o_ref, o in zip(refs[nc + npar:], res):
            o_ref[...] = o

    return pl.pallas_call(
        body, name=name, grid=(c // LANES,),
        in_specs=_colwise_specs(cols, pars, par_block),
        out_specs=[pl.BlockSpec((t, LANES), lambda j: (0, j)) for _ in range(n_out)],
        out_shape=[jax.ShapeDtypeStruct((t, c), F32) for _ in range(n_out)],
        compiler_params=_params("parallel"),
    )(*cols, *pars)


def _colwise_bwd_call(name, f, cols, pars, par_block, cts):
    t, c = cols[0].shape
    nc, npar, nct = len(cols), len(pars), len(cts)

    def body(*refs):
        ins = [r[...] for r in refs[:nc]] + [r[...] if b == "lane" else r[0] for r, b in zip(refs[nc:nc + npar], par_block)]
        gs = tuple(r[...] for r in refs[nc + npar:nc + npar + nct])
        outs = refs[nc + npar + nct:]
        _, vjp = jax.vjp(f, *ins)
        d = vjp(gs)
        for o_ref, v in zip(outs[:nc], d[:nc]):
            o_ref[...] = v
        for o_ref, v, b in zip(outs[nc:], d[nc:], par_block):
            if b == "lane":
                o_ref[...] = v
            else:
                o_ref[0] = v

    res = pl.pallas_call(
        body, name=name, grid=(c // LANES,),
        in_specs=_colwise_specs(cols, pars, par_block) + [pl.BlockSpec((t, LANES), lambda j: (0, j)) for _ in cts],
        out_specs=_colwise_specs(cols, pars, par_block),
        out_shape=[jax.ShapeDtypeStruct(v.shape, F32) for v in (*cols, *pars)],
        compiler_params=_params("parallel"),
    )(*cols, *pars, *cts)
    return tuple(res[:nc]), tuple(res[nc:])


def colwise(name, f, cols, pars, par_block, n_out):
    @jax.custom_vjp
    def op(cols, pars):
        return tuple(_colwise_fwd_call(name, f, cols, pars, par_block, n_out))

    def op_fwd(cols, pars):
        return op(cols, pars), (cols, pars)

    def op_bwd(res, cts):
        return _colwise_bwd_call(name + "_bwd", f, res[0], res[1], par_block, tuple(cts))

    op.defvjp(op_fwd, op_bwd)
    return op(tuple(cols), tuple(pars))


def _rg_block(x, gate, cw, cb, wr, br, wi, bi, lam):
    xa = _causal_conv(x, cw) + cb
    r = _sigmoid(_dot(xa, wr) + br)
    i = _sigmoid(_dot(xa, wi) + bi)
    log_a = -RG_C * r * _softplus(-lam)
    a = jnp.exp(log_a)
    b = jnp.sqrt(_neg_expm1(2.0 * log_a)) * (i * xa)
    return (_lin_scan(a, b) * _gelu(gate),)


def _dn_conv_block(mode):
    def f(x, cw):
        c = _silu(_causal_conv(x, cw))
        if mode == "v":
            return (c,)
        c = c * lax.rsqrt(jnp.sum(c * c, axis=-1, keepdims=True) + EPS)
        return (c * (DN_HEAD_DIM ** -0.5),) if mode == "q" else (c,)
    return f


def _block_diag(w):
    w = w.reshape(8, 2, 64, 64)
    z = jnp.zeros((8, 64, 64), w.dtype)
    top = jnp.concatenate([w[:, 0], z], axis=2)
    bot = jnp.concatenate([z, w[:, 1]], axis=2)
    return jnp.concatenate([top, bot], axis=1)


DN_HP = 8


def _dn_block(S, qw, kw, vw, gb, h0):
    hp, hd = S.shape[0], DN_HEAD_DIM
    heads = lambda a: jnp.concatenate([a[None, :, j * hd:(j + 1) * hd] for j in range(hp)], axis=0)
    lane = lax.broadcasted_iota(jnp.int32, gb.shape, 1)
    col = lambda i: jnp.sum(jnp.where(lane == i, gb, 0.0), axis=1, keepdims=True)[None]
    beta = jnp.concatenate([col(h0 + j) for j in range(hp)], axis=0)
    g = jnp.concatenate([col(h0 + j + DN_HEADS) for j in range(hp)], axis=0)
    s_new, o = _dn_step(S, heads(qw), heads(kw), heads(vw), beta, g)
    return s_new, jnp.concatenate([o[j:j + 1].reshape(o.shape[1:]) for j in range(hp)], axis=1)


def _dn_step(S, q, k, v, beta, g):
    c = DN_CHUNK
    ri = lax.broadcasted_iota(jnp.int32, (c, c), 0)
    ci = lax.broadcasted_iota(jnp.int32, (c, c), 1)
    incl, strict = ri >= ci, ri > ci
    eye = (ri == ci).astype(F32)
    gam = _cumsum_rows(g)
    gam_row = jnp.sum(jnp.where(ri <= ci, g, 0.0), axis=-2, keepdims=True)
    gam_last = jnp.sum(g, axis=-2, keepdims=True)
    decay = jnp.where(incl, jnp.exp(jnp.where(incl, gam - gam_row, 0.0)), 0.0)
    kb = k * beta
    vb = v * beta
    a = jnp.where(strict, _dot_nt(kb, k) * decay, 0.0)
    p = -a
    tinv = eye + p
    for _ in range(5):
        p = _dot3(p, p)
        tinv = tinv + _dot3(tinv, p)
    e_gam = jnp.exp(gam)
    u0 = _dot3(tinv, vb)
    wk = _dot3(tinv, kb * e_gam)
    qk = jnp.where(incl, _dot_nt(q, k) * decay, 0.0)
    q_dec = q * e_gam
    k_dec = k * jnp.exp(gam_last - gam)
    u = u0 - _dot(wk, S)
    o = _dot(q_dec, S) + _dot(qk, u)
    s_new = S * jnp.exp(gam_last) + _dot_tn(k_dec, u)
    return s_new, o


def _dn_fwd_call(q, k, v, gb):
    t, w = q.shape
    n, hp, hd, c = t // DN_CHUNK, DN_HP, DN_HEAD_DIM, DN_CHUNK

    def body(q_ref, k_ref, v_ref, gb_ref, o_ref, s0_ref, s_scr):
        @pl.when(pl.program_id(1) == 0)
        def _():
            s_scr[...] = jnp.zeros_like(s_scr)

        s_old = s_scr[...]
        s0_ref[:, 0] = s_old
        s_new, o = _dn_block(s_old, q_ref[...], k_ref[...], v_ref[...], gb_ref[...], pl.program_id(0) * hp)
        o_ref[...] = o
        s_scr[...] = s_new

    blk = pl.BlockSpec((c, hp * hd), lambda g, i: (i, g))
    return pl.pallas_call(
        body, name="dn_core", grid=(DN_HEADS // hp, n),
        in_specs=[blk, blk, blk, pl.BlockSpec((c, LANES), lambda g, i: (i, 0))],
        out_specs=[blk, pl.BlockSpec((hp, 1, hd, hd), lambda g, i: (g, i, 0, 0))],
        out_shape=[jax.ShapeDtypeStruct((t, w), F32), jax.ShapeDtypeStruct((DN_HEADS, n, hd, hd), F32)],
        scratch_shapes=[pltpu.VMEM((hp, hd, hd), F32)],
        compiler_params=_params("parallel", "arbitrary"),
    )(q, k, v, gb)


def _dn_bwd_call(q, k, v, gb, s0, do):
    t, w = q.shape
    n, hp, hd, c = t // DN_CHUNK, DN_HP, DN_HEAD_DIM, DN_CHUNK
    ng = DN_HEADS // hp

    def body(q_ref, k_ref, v_ref, gb_ref, s0_ref, do_ref, dq_ref, dk_ref, dv_ref, dgb_ref, ds_scr):
        @pl.when(pl.program_id(1) == 0)
        def _():
            ds_scr[...] = jnp.zeros_like(ds_scr)

        h0 = pl.program_id(0) * hp
        _, vjp = jax.vjp(lambda *a: _dn_block(*a, h0), s0_ref[:, 0], q_ref[...], k_ref[...], v_ref[...], gb_ref[...])
        ds, dq, dk, dv, dgb = vjp((ds_scr[...], do_ref[...]))
        ds_scr[...] = ds
        dq_ref[...], dk_ref[...], dv_ref[...] = dq, dk, dv
        dgb_ref[0] = dgb

    blk = pl.BlockSpec((c, hp * hd), lambda g, i: (n - 1 - i, g))
    res = pl.pallas_call(
        body, name="dn_core_bwd", grid=(ng, n),
        in_specs=[blk, blk, blk, pl.BlockSpec((c, LANES), lambda g, i: (n - 1 - i, 0)),
                  pl.BlockSpec((hp, 1, hd, hd), lambda g, i: (g, n - 1 - i, 0, 0)), blk],
        out_specs=[blk, blk, blk, pl.BlockSpec((1, c, LANES), lambda g, i: (g, n - 1 - i, 0))],
        out_shape=[jax.ShapeDtypeStruct((t, w), F32)] * 3 + [jax.ShapeDtypeStruct((ng, t, LANES), F32)],
        scratch_shapes=[pltpu.VMEM((hp, hd, hd), F32)],
        compiler_params=_params("parallel", "arbitrary"),
    )(q, k, v, gb, s0, do)
    return res[0], res[1], res[2], jnp.sum(res[3], axis=0)


@jax.custom_vjp
def dn_core(q, k, v, gb):
    return _dn_fwd_call(q, k, v, gb)[0]


def _dn_core_fwd(q, k, v, gb):
    o, s0 = _dn_fwd_call(q, k, v, gb)
    return o, (q, k, v, gb, s0)


def _dn_core_bwd(res, do):
    return _dn_bwd_call(*res, do)


dn_core.defvjp(_dn_core_fwd, _dn_core_bwd)


ATT_GH = 4


def _att_block(q, kp, kc, vp, vc, qn, kn, slope, has_prev, dil):
    s = ATT_SPAN
    qh = _rms(q, qn) * (ATT_HEAD_DIM ** -0.5)
    qi = lax.broadcasted_iota(jnp.int32, (s, s), 0)
    kj = lax.broadcasted_iota(jnp.int32, (s, s), 1)
    d_p = qi + s - kj
    d_c = qi - kj
    s_p = _dot_nt(qh, _rms(kp, kn)) - slope * (d_p * dil).astype(F32)
    s_c = _dot_nt(qh, _rms(kc, kn)) - slope * (d_c * dil).astype(F32)
    s_p = jnp.where((d_p <= s) & (has_prev > 0), s_p, NEG_INF)
    s_c = jnp.where(d_c >= 0, s_c, NEG_INF)
    m = lax.stop_gradient(jnp.maximum(jnp.max(s_p, axis=-1, keepdims=True), jnp.max(s_c, axis=-1, keepdims=True)))
    p_p = jnp.exp(s_p - m)
    p_c = jnp.exp(s_c - m)
    den = jnp.sum(p_p, axis=-1, keepdims=True) + jnp.sum(p_c, axis=-1, keepdims=True)
    o = _dot(p_p / den, vp) + _dot(p_c / den, vc)
    lse = m + jnp.log(den)
    return o, jnp.broadcast_to(lse, o.shape)


def _att_heads(a):
    e = ATT_HEAD_DIM
    return jnp.concatenate([a[None, :, h * e:(h + 1) * e] for h in range(ATT_GH)], axis=0)


def _att_lanes(a):
    return jnp.concatenate([a[h:h + 1].reshape(a.shape[1:]) for h in range(ATT_GH)], axis=1)


def _att_rows(q, kp, kc, vp, vc, qn, kn, group, has_prev, dil):
    head = lax.broadcasted_iota(jnp.int32, (ATT_GH, 1, 1), 0) + (ATT_GH * group + 1)
    slope = jnp.exp(head.astype(F32) * (-8.0 / ATT_HEADS * math.log(2.0)))
    o, lse = _att_block(_att_heads(q), _att_heads(kp), _att_heads(kc), _att_heads(vp), _att_heads(vc), qn, kn, slope, has_prev, dil)
    return _att_lanes(o), _att_lanes(lse)


def _att_specs(group, dil):
    blk = (ATT_SPAN, ATT_GH * ATT_HEAD_DIM)
    cur = lambda which: pl.BlockSpec(blk, lambda r, n: (n, r * 9 + 3 * which + group))
    prev = lambda which: pl.BlockSpec(blk, lambda r, n: (jnp.maximum(n - 1, 0), r * 9 + 3 * which + group))
    out = pl.BlockSpec(blk, lambda r, n: (n, r))
    gain = pl.BlockSpec((ATT_GH, 1, ATT_HEAD_DIM), lambda r, n: (0, 0, 0))
    return [cur(0), prev(1), cur(1), prev(2), cur(2), gain, gain], out, gain


def _att_fwd_call(name, group, dil, pa, qn, kn):
    t = pa.shape[0]
    l = t // dil
    w = ATT_GH * ATT_HEAD_DIM
    ins, out, _ = _att_specs(group, dil)
    pav = pa.reshape(l, dil * pa.shape[1])

    def body(q_ref, kp_ref, kc_ref, vp_ref, vc_ref, qn_ref, kn_ref, o_ref, lse_ref):
        o_ref[...], lse_ref[...] = _att_rows(q_ref[...], kp_ref[...], kc_ref[...], vp_ref[...], vc_ref[...], qn_ref[...],
                                             kn_ref[...], group, pl.program_id(1), dil)

    o, lse = pl.pallas_call(
        body, name=name, grid=(dil, l // ATT_SPAN), in_specs=ins, out_specs=[out, out],
        out_shape=[jax.ShapeDtypeStruct((l, dil * w), F32)] * 2, compiler_params=_params("parallel", "arbitrary"),
    )(pav, pav, pav, pav, pav, qn, kn)
    return o.reshape(t, w), lse.reshape(t, w)


def _att_bwd_call(name, group, dil, pa, qn, kn, do, dlse):
    t = pa.shape[0]
    l = t // dil
    w = ATT_GH * ATT_HEAD_DIM
    ins, out, gain = _att_specs(group, dil)
    pav = pa.reshape(l, dil * pa.shape[1])

    def body(q_ref, kp_ref, kc_ref, vp_ref, vc_ref, qn_ref, kn_ref, do_ref, dlse_ref,
             dq_ref, dkp_ref, dkc_ref, dvp_ref, dvc_ref, dqn_ref, dkn_ref):
        has_prev = pl.program_id(1)
        _, vjp = jax.vjp(lambda *a: _att_rows(*a, group, has_prev, dil), q_ref[...], kp_ref[...], kc_ref[...], vp_ref[...],
                         vc_ref[...], qn_ref[...], kn_ref[...])
        dq, dkp, dkc, dvp, dvc, dqn, dkn = vjp((do_ref[...], dlse_ref[...]))
        dq_ref[...], dkp_ref[...], dkc_ref[...], dvp_ref[...], dvc_ref[...] = dq, dkp, dkc, dvp, dvc

        @pl.when((pl.program_id(0) == 0) & (pl.program_id(1) == 0))
        def _():
            dqn_ref[...] = jnp.zeros_like(dqn_ref)
            dkn_ref[...] = jnp.zeros_like(dkn_ref)

        dqn_ref[...] += dqn
        dkn_ref[...] += dkn

    res = pl.pallas_call(
        body, name=name + "_bwd", grid=(dil, l // ATT_SPAN), in_specs=ins + [out, out],
        out_specs=[out] * 5 + [gain, gain],
        out_shape=[jax.ShapeDtypeStruct((l, dil * w), F32)] * 5 + [jax.ShapeDtypeStruct(qn.shape, F32)] * 2,
        compiler_params=_params("arbitrary", "arbitrary"),
    )(pav, pav, pav, pav, pav, qn, kn, do.reshape(l, dil * w), dlse.reshape(l, dil * w))
    dq, dkp, dkc, dvp, dvc, dqn, dkn = res
    back = lambda g: jnp.pad(g[ATT_SPAN:], ((0, ATT_SPAN), (0, 0)))
    return dq.reshape(t, w), (dkc + back(dkp)).reshape(t, w), (dvc + back(dvp)).reshape(t, w), dqn, dkn


def _att_mix(o1, o2, o3, l1, l2, l3):
    m = jnp.maximum(jnp.maximum(l1, l2), l3)
    e1, e2, e3 = jnp.exp(l1 - m), jnp.exp(l2 - m), jnp.exp(l3 - m)
    s = e1 + e2 + e3
    return (jnp.concatenate([o1 * (e1 / s), o2 * (e2 / s), o3 * (e3 / s)], axis=1),)


def att_branch(name, pa, qn, kn):
    e = ATT_HEAD_DIM
    gains = lambda p, g: p[ATT_GH * g:ATT_GH * (g + 1)].reshape(ATT_GH, 1, e)

    @jax.custom_vjp
    def groups(pa, qn, kn):
        res = [_att_fwd_call(f"{name}_att{g}", g, dil, pa, gains(qn, g), gains(kn, g)) for g, (_, dil) in enumerate(ATT_GROUPS)]
        return tuple(r[0] for r in res) + tuple(r[1] for r in res)

    def groups_fwd(pa, qn, kn):
        return groups(pa, qn, kn), (pa, qn, kn)

    def groups_bwd(res, cts):
        pa, qn, kn = res
        n = len(ATT_GROUPS)
        parts = [_att_bwd_call(f"{name}_att{g}", g, dil, pa, gains(qn, g), gains(kn, g), cts[g], cts[n + g])
                 for g, (_, dil) in enumerate(ATT_GROUPS)]
        d_pa = jnp.concatenate([p[i] for i in range(3) for p in parts], axis=1)
        return (d_pa, jnp.concatenate([p[3] for p in parts]).reshape(qn.shape), jnp.concatenate([p[4] for p in parts]).reshape(kn.shape))

    groups.defvjp(groups_fwd, groups_bwd)
    return rowwise(f"{name}_attmix", _att_mix, groups(pa, qn, kn))[0]


def dn_gates(name, ba, a_log, dt_bias):
    place = lambda p: jnp.pad(p.reshape(1, DN_HEADS), ((0, 0), (DN_HEADS, LANES - 2 * DN_HEADS)))

    def f(x, al, dt):
        lane = lax.broadcasted_iota(jnp.int32, x.shape, 1)
        return (jnp.where(lane < DN_HEADS, _sigmoid(x), -jnp.exp(al) * _softplus(x + dt)),)

    return rowwise(name, f, (ba,), (place(a_log), place(dt_bias)))[0]


def _dn_out(o, z, g):
    parts = []
    for h in range(DN_HEADS):
        sl = slice(h * DN_HEAD_DIM, (h + 1) * DN_HEAD_DIM)
        parts.append(_rms(o[:, sl], g[:, sl]) * _silu(z[:, sl]))
    return (jnp.concatenate(parts, axis=1),)


def _merge(ml, za, zb, zc):
    d = D_MODEL
    return (_sigmoid(ml[:, :d]) * za + _sigmoid(ml[:, d:2 * d]) * zb + _sigmoid(ml[:, 2 * d:]) * zc,)


def add_norm(name, x, pend, scale, gain):
    if pend is None:
        return x, rowwise(name, lambda a, g: (_rms(a, g),), (x,), (gain,))[0]

    def f(a, b, g):
        s = a + scale * b
        return s, _rms(s, g)

    return rowwise(name, f, (x, pend), (gain,))


W_IN_PIECES = (("rgx", 0, 1024), ("gate", 1024, 1024), ("att", 2048, 2304), ("dq", 4352, 1024), ("dk", 5376, 1024),
               ("dv", 6400, 1024), ("dz", 7424, 1024), ("ba", 8448, 16), ("mrg", 8464, 3072))
RG_PAR_BLOCKS = ("lane", "lane", "blk", "lane", "blk", "lane", "lane")


def mixer(name, u, w, p):
    mm = lambda nm, a, wt: mm_rows(nm, a[None], wt[None])
    pr = {k: mm_cols(f"{name}_in_{k}", u, w["in_" + k][None])[0] for k, _, _ in W_IN_PIECES}
    ya = colwise(name + "_rg", _rg_block, (pr["rgx"], pr["gate"]),
                 (w["rg_conv_w"], p["rg_conv_b"], _block_diag(p["rg_w_r"]), p["rg_b_r"], _block_diag(p["rg_w_i"]),
                  p["rg_b_i"], p["rg_lambda"]), RG_PAR_BLOCKS, 1)[0]
    yb = att_branch(name, pr["att"], p["att_q_norm"], p["att_k_norm"])
    cw = w["dn_conv_w"]
    cq = colwise(name + "_dnq", _dn_conv_block("q"), (pr["dq"],), (cw[:, :1024],), ("lane",), 1)[0]
    ck = colwise(name + "_dnk", _dn_conv_block("k"), (pr["dk"],), (cw[:, 1024:2048],), ("lane",), 1)[0]
    cv = colwise(name + "_dnv", _dn_conv_block("v"), (pr["dv"],), (cw[:, 2048:],), ("lane",), 1)[0]
    gb = dn_gates(name + "_dngate", pr["ba"], p["dn_a_log"], p["dn_dt_bias"])
    o_dn = dn_core(cq, ck, cv, gb)
    yc = rowwise(name + "_dnout", _dn_out, (o_dn, pr["dz"]), (p["dn_out_norm"].reshape(1, D_MODEL),))[0]
    y = rowwise(name + "_merge", _merge, (pr["mrg"], mm(name + "_ba", ya, w["br_a"]), mm(name + "_bb", yb, w["br_b"]),
                                          mm(name + "_bc", yc, w["br_c"])))[0]
    return mm(name + "_out", y, w["w_out"])


def _loss_call(x, pend, target):
    t, d = x.shape
    tile = _row_tile(t)

    def body(x_ref, p_ref, t_ref, loss_ref, g_ref):
        err = x_ref[...] + 0.5 * p_ref[...] - t_ref[...]
        g_ref[...] = err * (1.0 / d)

        @pl.when(pl.program_id(0) == 0)
        def _():
            loss_ref[...] = jnp.zeros_like(loss_ref)

        loss_ref[...] += jnp.full(loss_ref.shape, 0.5 / d, F32) * jnp.sum(err * err)

    blk = pl.BlockSpec((tile, d), lambda i: (i, 0))
    loss, g = pl.pallas_call(
        body, name="loss", grid=(t // tile,), in_specs=[blk, blk, blk],
        out_specs=[pl.BlockSpec((8, LANES), lambda i: (0, 0)), blk],
        out_shape=[jax.ShapeDtypeStruct((8, LANES), F32), jax.ShapeDtypeStruct((t, d), F32)],
        compiler_params=_params("arbitrary"),
    )(x, pend, target)
    return loss[0, 0], g


@jax.custom_vjp
def loss_op(x, pend, target):
    return _loss_call(x, pend, target)[0]


def _loss_fwd(x, pend, target):
    loss, g = _loss_call(x, pend, target)
    return loss, g


def _loss_bwd(g, ct):
    return ct * g, (0.5 * ct) * g, None


loss_op.defvjp(_loss_fwd, _loss_bwd)


def local_loss(w, p, x, target):
    pend, scale = None, 0.0
    for l in range(len(w)):
        n = f"L{l}"
        x, h = add_norm(n + "_n1", x, pend, scale, p[l]["ffn1_norm"])
        pend, scale = ffn(n + "_f1", h, w[l]["ffn1_w_gate"], w[l]["ffn1_w_up"], w[l]["ffn1_w_down"]), 0.5
        x, h = add_norm(n + "_nm", x, pend, scale, p[l]["mix_norm"])
        pend, scale = mixer(n + "_mx", h, w[l], p[l]), 1.0
        x, h = add_norm(n + "_n2", x, pend, scale, p[l]["ffn2_norm"])
        pend, scale = ffn(n + "_f2", h, w[l]["ffn2_w_gate"], w[l]["ffn2_w_up"], w[l]["ffn2_w_down"]), 0.5
    return loss_op(x, pend, target)


WEIGHT_NAMES = ("ffn1_norm", "ffn1_w_gate", "ffn1_w_up", "ffn1_w_down", "mix_norm", "w_in", "rg_conv_w", "rg_conv_b",
                "rg_w_r", "rg_b_r", "rg_w_i", "rg_b_i", "rg_lambda", "att_q_norm", "att_k_norm", "dn_conv_w", "dn_a_log",
                "dn_dt_bias", "dn_out_norm", "w_branch", "w_out", "ffn2_norm", "ffn2_w_gate", "ffn2_w_up", "ffn2_w_down")
MATRICES = (("ffn1_w_gate", 2), ("ffn1_w_up", 2), ("ffn1_w_down", 1), ("w_in", 2), ("w_branch", 1), ("w_out", 1),
            ("ffn2_w_gate", 2), ("ffn2_w_up", 2), ("ffn2_w_down", 1))
CONVS = (("rg_conv_w", 2), ("dn_conv_w", 2))
SHARD_AXIS = dict(MATRICES + CONVS)
SMALL_NAMES = tuple(n for n in WEIGHT_NAMES if n not in SHARD_AXIS)
ROW_PARAMS = ("ffn1_norm", "mix_norm", "rg_conv_b", "rg_b_r", "rg_b_i", "rg_lambda", "ffn2_norm")
FFN_MATS = ("ffn1_w_gate", "ffn1_w_up", "ffn1_w_down", "ffn2_w_gate", "ffn2_w_up", "ffn2_w_down")
TRANSPOSED_MATS = ("ffn1_w_gate", "ffn1_w_up", "ffn2_w_gate", "ffn2_w_up")
W_IN_SHARD = 2884
GATHER_ORDER = ((("ffn1_w_gate", 0), ("ffn1_w_up", 0), ("ffn1_w_down", 0)),
                (("w_in", 0),),
                None)
GATHER_IDS = (1, 6, 7)
LATE_MATS = ("ffn2_w_gate", "ffn2_w_up", "ffn2_w_down", "w_out", "w_branch")
EXCHANGE_GROUPS = (lambda n, l: l == 1 and n in LATE_MATS,
                   lambda n, l: (l == 1) != (n in LATE_MATS),
                   lambda n, l: l == 0 and n == "w_in",
                   lambda n, l: l == 0 and n not in LATE_MATS and n != "w_in")


def _shard_minor(a, axis):
    a = jnp.moveaxis(a, 0, axis)
    return a.reshape(a.shape[:axis] + (N_CHIPS * a.shape[axis + 1],) + a.shape[axis + 2:])


def _w_in_piece(g, off, n):
    s = W_IN_SHARD
    parts = [g[j][:, max(off, j * s) - j * s:min(off + n, (j + 1) * s) - j * s]
             for j in range(N_CHIPS) if max(off, j * s) < min(off + n, (j + 1) * s)]
    return jnp.concatenate(parts, axis=1) if len(parts) > 1 else parts[0]


def _w_in_chip_grad(gl, j):
    s = W_IN_SHARD
    parts = [gl["in_" + k][:, max(off, j * s) - off:min(off + n, (j + 1) * s) - off]
             for k, off, n in W_IN_PIECES if max(off, j * s) < min(off + n, (j + 1) * s)]
    return jnp.concatenate(parts, axis=1)


def layer_weights(g, conv, l):
    w = {n: g[n, l] for n in FFN_MATS}
    w["w_out"] = g["w_out", l].reshape(D_MODEL, D_MODEL)
    for k, off, n in W_IN_PIECES:
        piece = _w_in_piece(g["w_in", l], off, n)
        w["in_" + k] = jnp.pad(piece, ((0, 0), (0, LANES - n))) if n < LANES else piece
    wb = g["w_branch", l].reshape(-1, D_MODEL)
    w["br_a"], w["br_b"], w["br_c"] = wb[:1024], wb[1024:1792], wb[1792:]
    w["rg_conv_w"], w["dn_conv_w"] = conv["rg_conv_w"][l], conv["dn_conv_w"][l]
    return w


def layer_weight_grads(gw):
    out = {}
    for l, gl in enumerate(gw):
        for n in FFN_MATS:
            out[n, l] = gl[n]
        out["w_out", l] = gl["w_out"].reshape(N_CHIPS, -1, D_MODEL)
        out["w_branch", l] = jnp.concatenate([gl["br_a"], gl["br_b"], gl["br_c"]], axis=0).reshape(N_CHIPS, -1, D_MODEL)
        out["w_in", l] = jnp.stack([_w_in_chip_grad(gl, j) for j in range(N_CHIPS)])
    conv = {n: jnp.stack([gl[n] for gl in gw]) for n, _ in CONVS}
    return out, conv


def layer_small(small, l):
    p = {n: small[n][l] for n in SMALL_NAMES}
    for n in ROW_PARAMS:
        p[n] = small[n][l:l + 1]
    return p


def layer_small_grads(gp, small):
    return {n: jnp.stack([g[n] for g in gp]).reshape(small[n].shape) for n in SMALL_NAMES}


HBM_SPEC = pl.BlockSpec(memory_space=pl.ANY)


def _place():
    x, y, c = lax.axis_index("x"), lax.axis_index("y"), lax.axis_index("c")
    other_chips = [(1 - x, y), (x, 1 - y), (1 - x, 1 - y)]
    return x, y, c, 2 * x + y, (x, y, 1 - c), other_chips


def _half_rows(ref, lead, hc):
    hr = ref.shape[-2] // 2
    return ref.at[(*lead, pl.ds(pl.multiple_of(hc * hr, 16), hr), slice(None))]


def _chip_index():
    return (2 * lax.axis_index("x") + lax.axis_index("y")).astype(jnp.int32).reshape(1)


def cast_into_blocks(name, w):
    l, rows, cols = w.shape
    tr = rows // 2

    def body(me_ref, w_ref, *o_refs):
        for a, o_ref in enumerate(o_refs):
            o_ref[...] = w_ref[a:a + 1].astype(BF16)

    return pl.pallas_call(
        body, name=name, out_shape=[jax.ShapeDtypeStruct((N_CHIPS, rows, cols), BF16)] * l,
        grid_spec=pltpu.PrefetchScalarGridSpec(
            num_scalar_prefetch=1, grid=(rows // tr,),
            in_specs=[pl.BlockSpec((l, tr, cols), lambda i, me: (0, i, 0))],
            out_specs=[pl.BlockSpec((1, tr, cols), lambda i, me: (me[0], i, 0))] * l),
        compiler_params=_params("parallel"),
    )(_chip_index(), w)


def _gather_blocks(bufs_in, bufs_out, send_sems, recv_sems):
    n = len(bufs_in)
    x, y, c, me, sibling, chips = _place()

    def copy(s, src, dst, to):
        return pltpu.make_async_remote_copy(src_ref=src, dst_ref=dst, send_sem=send_sems.at[s], recv_sem=recv_sems.at[s],
                                            device_id=to, device_id_type=MESH)

    first, passed = [], []
    for j, (cx, cy) in enumerate(chips):
        for i in range(n):
            cp = copy(6 * i + j, _half_rows(bufs_in[i], (me,), c), _half_rows(bufs_out[i], (me,), c), (cx, cy, c))
            cp.start()
            first.append(cp)
    for j, (cx, cy) in enumerate(chips):
        k = 2 * cx + cy
        for i in range(n):
            copy(6 * i + j, _half_rows(bufs_in[i], (me,), c), _half_rows(bufs_out[i], (k,), c), (cx, cy, c)).wait_recv()
            cp = copy(6 * i + 3 + j, _half_rows(bufs_out[i], (k,), c), _half_rows(bufs_out[i], (k,), c), sibling)
            cp.start()
            passed.append(cp)
    for j, (cx, cy) in enumerate(chips):
        k = 2 * cx + cy
        for i in range(n):
            copy(6 * i + 3 + j, _half_rows(bufs_in[i], (me,), c), _half_rows(bufs_out[i], (k,), 1 - c), sibling).wait_recv()
    for cp in first + passed:
        cp.wait_send()


def _handshake(peers):
    barrier = pltpu.get_barrier_semaphore()
    for p in peers:
        pl.semaphore_signal(barrier, inc=1, device_id=p, device_id_type=MESH)
    pl.semaphore_wait(barrier, len(peers))


def allgather_blocks_sc(name, bufs, collective_id):
    n = len(bufs)
    refs = [jax.new_ref(b, memory_space=pltpu.MemorySpace.HBM) for b in bufs]

    @pl.kernel(mesh=plsc.ScalarSubcoreMesh(axis_name="sequencer", num_cores=1), name=name,
               scratch_types=(pltpu.SemaphoreType.DMA((6 * n,)), pltpu.SemaphoreType.DMA((6 * n,))),
               compiler_params=pltpu.CompilerParams(collective_id=collective_id))
    def launch(send_sems, recv_sems):
        x, y, c, me, sibling, chips = _place()
        _handshake([(cx, cy, c) for cx, cy in chips] + [sibling])
        _gather_blocks(refs, refs, send_sems, recv_sems)

    launch()
    return [jax.freeze(r) for r in refs]


PEER_FLIPS = tuple((fx, fy, fc) for fx in (0, 1) for fy in (0, 1) for fc in (0, 1))[1:]


def exchange_pieces_sc(name, gs, collective_id):
    n = len(gs)

    def body(*refs):
        ins, outs = refs[:n], refs[n:2 * n]
        send_sems, recv_sems = refs[2 * n:]
        x, y, c, me, sibling, chips = _place()
        my_dev = 4 * x + 2 * y + c
        flip = lambda v, f: 1 - v if f else v
        peers = [(flip(x, fx), flip(y, fy), flip(c, fc)) for fx, fy, fc in PEER_FLIPS]
        _handshake(peers)
        sends = []
        for r, (px, py, pc) in enumerate(peers):
            for i in range(n):
                cp = pltpu.make_async_remote_copy(
                    src_ref=_half_rows(ins[i], (2 * px + py,), pc), dst_ref=outs[i].at[my_dev], send_sem=send_sems.at[7 * i + r],
                    recv_sem=recv_sems.at[7 * i + r], device_id=(px, py, pc), device_id_type=MESH)
                cp.start()
                sends.append(cp)
        for r, (px, py, pc) in enumerate(peers):
            for i in range(n):
                pltpu.make_async_remote_copy(
                    src_ref=_half_rows(ins[i], (me,), c), dst_ref=outs[i].at[4 * px + 2 * py + pc], send_sem=send_sems.at[7 * i + r],
                    recv_sem=recv_sems.at[7 * i + r], device_id=(px, py, pc), device_id_type=MESH).wait_recv()
        for cp in sends:
            cp.wait_send()

    return pl.kernel(
        body, name=name, mesh=plsc.ScalarSubcoreMesh(axis_name="sequencer", num_cores=1),
        out_type=[jax.ShapeDtypeStruct((N_DEV, g.shape[1] // 2, g.shape[2]), g.dtype) for g in gs],
        scratch_types=[pltpu.SemaphoreType.DMA((7 * n,)), pltpu.SemaphoreType.DMA((7 * n,))],
        compiler_params=pltpu.CompilerParams(collective_id=collective_id),
    )(*gs)


def sibling_share_halves(name, fs):
    n = len(fs)
    every = (slice(None),)

    def body(*refs):
        ins, outs = refs[:n], refs[n:2 * n]
        send_sems, recv_sems = refs[2 * n:]
        x, y, c, me, sibling, chips = _place()
        sends = []
        for i in range(n):
            cp = pltpu.make_async_remote_copy(src_ref=_half_rows(ins[i], every, c), dst_ref=_half_rows(outs[i], every, c),
                                              send_sem=send_sems.at[i], recv_sem=recv_sems.at[i], device_id=sibling, device_id_type=MESH)
            cp.start()
            sends.append(cp)
        for i in range(n):
            pltpu.make_async_remote_copy(src_ref=_half_rows(ins[i], every, c), dst_ref=_half_rows(outs[i], every, 1 - c),
                                         send_sem=send_sems.at[i], recv_sem=recv_sems.at[i], device_id=sibling,
                                         device_id_type=MESH).wait_recv()
        for cp in sends:
            cp.wait_send()

    return pl.pallas_call(
        body, name=name, out_shape=[jax.ShapeDtypeStruct(f.shape, f.dtype) for f in fs],
        in_specs=[HBM_SPEC] * n, out_specs=[HBM_SPEC] * n, input_output_aliases={i: i for i in range(n)},
        scratch_shapes=[pltpu.SemaphoreType.DMA((n,)), pltpu.SemaphoreType.DMA((n,))],
    )(*fs)


def allgather_small_sc(name, v, collective_id):
    def body(v_ref, out_ref, send_sems, recv_sems, local_sem):
        x, y, c, me, sibling, chips = _place()
        my_dev = 4 * x + 2 * y + c
        flip = lambda a, f: 1 - a if f else a
        peers = [(flip(x, fx), flip(y, fy), flip(c, fc)) for fx, fy, fc in PEER_FLIPS]
        _handshake(peers)
        mine = pltpu.make_async_copy(v_ref, out_ref.at[my_dev], local_sem)
        mine.start()
        sends = []
        for r, peer in enumerate(peers):
            cp = pltpu.make_async_remote_copy(src_ref=v_ref, dst_ref=out_ref.at[my_dev], send_sem=send_sems.at[r],
                                              recv_sem=recv_sems.at[r], device_id=peer, device_id_type=MESH)
            cp.start()
            sends.append(cp)
        for r, (px, py, pc) in enumerate(peers):
            pltpu.make_async_remote_copy(src_ref=v_ref, dst_ref=out_ref.at[4 * px + 2 * py + pc], send_sem=send_sems.at[r],
                                         recv_sem=recv_sems.at[r], device_id=(px, py, pc), device_id_type=MESH).wait_recv()
        for cp in sends:
            cp.wait_send()
        mine.wait()

    return pl.kernel(
        body, name=name, mesh=plsc.ScalarSubcoreMesh(axis_name="sequencer", num_cores=1),
        out_type=jax.ShapeDtypeStruct((N_DEV,) + v.shape, v.dtype),
        scratch_types=[pltpu.SemaphoreType.DMA((7,)), pltpu.SemaphoreType.DMA((7,)), pltpu.SemaphoreType.DMA],
        compiler_params=pltpu.CompilerParams(collective_id=collective_id),
    )(v)


SUM_BLOCK_ELEMS = 512 * 1024


def sum_slabs(name, b):
    k, h, w = b.shape

    def body(b_ref, o_ref):
        acc = b_ref[0].astype(F32)
        for i in range(1, k):
            acc = acc + b_ref[i].astype(F32)
        o_ref[...] = acc

    return pl.pallas_call(
        body, name=name, out_shape=jax.ShapeDtypeStruct((h, w), F32),
        in_specs=[pl.BlockSpec(memory_space=pltpu.VMEM)], out_specs=pl.BlockSpec(memory_space=pltpu.VMEM),
        compiler_params=pltpu.CompilerParams(vmem_limit_bytes=VMEM_LIMIT),
    )(b)


def sum_pieces(name, pieces, gs):
    nl = len(pieces)
    k, h, w = pieces[0].shape
    tile = max(t for t in range(16, h + 1, 16) if h % t == 0 and (t * w <= SUM_BLOCK_ELEMS or t == 16))
    nt = h // tile
    x, y, c = lax.axis_index("x"), lax.axis_index("y"), lax.axis_index("c")
    place = [v.astype(jnp.int32).reshape(1) for v in (c, 2 * x + y, 4 * x + 2 * y + c)]

    assert nl == 2

    def tile_of(l, a, i):
        return i * a if l else i * (1 - a) + (nt - 1) * a

    def body(c_ref, me_ref, dev_ref, *refs):
        p_refs, g_refs, o_ref = refs[:nl], refs[nl:2 * nl], refs[2 * nl]
        my_dev = dev_ref[0]
        for l in range(nl):
            @pl.when(pl.program_id(0) == l)
            def _():
                o_ref[0] = jnp.zeros(o_ref.shape[1:], F32)
                for d in range(k):
                    @pl.when(my_dev == d)
                    def _():
                        o_ref[0] += g_refs[l][0].astype(F32)

                    @pl.when(my_dev != d)
                    def _():
                        o_ref[0] += p_refs[l][d].astype(F32)

    in_specs = [pl.BlockSpec((k, tile, w), functools.partial(lambda l, a, i, cc, me, dev: (0, tile_of(l, a, i), 0), l))
                for l in range(nl)]
    in_specs += [pl.BlockSpec((1, tile, w), functools.partial(lambda l, a, i, cc, me, dev: (me[0], cc[0] * nt + tile_of(l, a, i), 0), l))
                 for l in range(nl)]
    return pl.pallas_call(
        body, name=name, out_shape=jax.ShapeDtypeStruct((nl, 2 * h, w), F32),
        grid_spec=pltpu.PrefetchScalarGridSpec(
            num_scalar_prefetch=3, grid=(nl, nt), in_specs=in_specs,
            out_specs=pl.BlockSpec((1, tile, w), lambda a, i, cc, me, dev: (a, cc[0] * nt + i, 0))),
        compiler_params=_params("arbitrary", "arbitrary"),
    )(*place, *pieces, *gs)


def _adam_block(w, g, m, v):
    m = ADAM_B1 * m + (1.0 - ADAM_B1) * g
    v = ADAM_B2 * v + (1.0 - ADAM_B2) * (g * g)
    m_hat = m / (1.0 - ADAM_B1 ** ADAM_STEP)
    v_hat = v / (1.0 - ADAM_B2 ** ADAM_STEP)
    return -ADAM_LR * (m_hat / (jnp.sqrt(v_hat) + ADAM_EPS) + ADAM_WD * w), m, v


def adamw(name, w, g, m, v):
    shape = w.shape
    cols = shape[-1]
    rows = w.size // cols
    tile = 128 if rows % 128 == 0 else rows
    flat = [a.reshape(rows, cols) for a in (w, g, m, v)]

    def body(w_ref, g_ref, m_ref, v_ref, d_ref, nm_ref, nv_ref):
        d_ref[...], nm_ref[...], nv_ref[...] = _adam_block(w_ref[...], g_ref[...], m_ref[...], v_ref[...])

    blk = pl.BlockSpec((tile, cols), lambda i: (i, 0))
    res = pl.pallas_call(
        body, name=name, grid=(rows // tile,), in_specs=[blk] * 4, out_specs=[blk] * 3,
        out_shape=[jax.ShapeDtypeStruct((rows, cols), F32)] * 3, compiler_params=_params("parallel"),
    )(*flat)
    return tuple(r.reshape(shape) for r in res)


def _pack_small(values):
    flat = jnp.concatenate([v.reshape(-1) for v in values.values()])
    n = flat.shape[0]
    total = -(-n // (8 * LANES)) * (8 * LANES)
    return jnp.pad(flat, (0, total - n)).reshape(-1, LANES)


def _unpack_small(v, shapes):
    flat = v.reshape(-1)
    out, off = {}, 0
    for n, shape in shapes.items():
        sz = int(np.prod(shape))
        out[n] = flat[off:off + sz].reshape(shape)
        off += sz
    return out


def kernel(x, ffn1_norm, ffn1_w_gate, ffn1_w_up, ffn1_w_down, mix_norm, w_in, rg_conv_w, rg_conv_b, rg_w_r, rg_b_r, rg_w_i, rg_b_i, rg_lambda, att_q_norm, att_k_norm, dn_conv_w, dn_a_log, dn_dt_bias, dn_out_norm, w_branch, w_out, ffn2_norm, ffn2_w_gate, ffn2_w_up, ffn2_w_down, loss_target, m_ffn1_norm, m_ffn1_w_gate, m_ffn1_w_up, m_ffn1_w_down, m_mix_norm, m_w_in, m_rg_conv_w, m_rg_conv_b, m_rg_w_r, m_rg_b_r, m_rg_w_i, m_rg_b_i, m_rg_lambda, m_att_q_norm, m_att_k_norm, m_dn_conv_w, m_dn_a_log, m_dn_dt_bias, m_dn_out_norm, m_w_branch, m_w_out, m_ffn2_norm, m_ffn2_w_gate, m_ffn2_w_up, m_ffn2_w_down, v_ffn1_norm, v_ffn1_w_gate, v_ffn1_w_up, v_ffn1_w_down, v_mix_norm, v_w_in, v_rg_conv_w, v_rg_conv_b, v_rg_w_r, v_rg_b_r, v_rg_w_i, v_rg_b_i, v_rg_lambda, v_att_q_norm, v_att_k_norm, v_dn_conv_w, v_dn_a_log, v_dn_dt_bias, v_dn_out_norm, v_w_branch, v_w_out, v_ffn2_norm, v_ffn2_w_gate, v_ffn2_w_up, v_ffn2_w_down):
    given = dict(locals())
    for n in TRANSPOSED_MATS:
        for pre in ("", "m_", "v_"):
            given[pre + n] = jnp.swapaxes(given[pre + n], 1, 2)
    small = {n: given[n] for n in SMALL_NAMES}
    n_layers = ffn1_norm.shape[0]
    mat_names = [n for n, _ in MATRICES]
    conv_names = [n for n, _ in CONVS]

    blocks = {}
    for n in mat_names:
        for l, b in enumerate(cast_into_blocks("cast_" + n, given[n])):
            blocks[n, l] = b
    gathered, done = {}, []
    for i, wanted in enumerate(GATHER_ORDER):
        keys = [k for k in blocks if k not in gathered and (wanted is None or k in wanted)]
        bufs, _ = lax.optimization_barrier(([blocks[k] for k in keys], done))
        done = allgather_blocks_sc(f"allgather_{i}", bufs, GATHER_IDS[i])
        gathered.update(zip(keys, done))
    taps = jnp.concatenate([given[n].reshape(-1) for n in conv_names]).reshape(-1, LANES)
    taps = allgather_small_sc("allgather_taps", taps, 8).reshape(N_CHIPS, 2, -1)[:, 0]
    conv, off = {}, 0
    for n, ax in CONVS:
        sz = given[n].size
        conv[n] = _shard_minor(taps[:, off:off + sz].reshape((N_CHIPS,) + given[n].shape), ax)
        off += sz
    w = [layer_weights(gathered, conv, l) for l in range(n_layers)]
    p = [layer_small(small, l) for l in range(n_layers)]

    loss, (gw, gp, gx) = jax.value_and_grad(local_loss, argnums=(0, 1, 2))(w, p, x[0], loss_target[0])
    g_mats, g_conv = layer_weight_grads(gw)

    pieces = {}
    for i, group in enumerate(EXCHANGE_GROUPS):
        keys = [k for k in g_mats if group(*k)]
        pieces.update(zip(keys, exchange_pieces_sc(f"exchange_{i}", [g_mats[k] for k in keys], 2 + i)))
    halves = {n: sum_pieces("sum_" + n, [pieces[n, l] for l in range(n_layers)], [g_mats[n, l] for l in range(n_layers)])
              for n in mat_names}
    grads = {}
    for tag, names in (("late", [n for n in mat_names if n in LATE_MATS]), ("early", [n for n in mat_names if n not in LATE_MATS])):
        grads.update(zip(names, sibling_share_halves("share_" + tag, [halves[n] for n in names])))

    g_small = dict(layer_small_grads(gp, small), **g_conv, loss=loss.reshape(1))
    packed_small = _pack_small(g_small)
    slabs = allgather_small_sc("allgather_small", packed_small, 9)
    summed =_unpack_small(sum_slabs("sum_small", slabs), {n: g.shape for n, g in g_small.items()})
    chip = 2 * lax.axis_index("x") + lax.axis_index("y")
    for n in SMALL_NAMES:
        grads[n] = summed[n]
    for n, ax in CONVS:
        s = given[n].shape[ax]
        grads[n] = lax.dynamic_slice_in_dim(summed[n], chip * s, s, axis=ax)

    upd = {n: adamw("adamw_" + n, given[n], grads[n], given["m_" + n], given["v_" + n]) for n in WEIGHT_NAMES}
    out = lambda n, a: jnp.swapaxes(a, 1, 2) if n in TRANSPOSED_MATS else a
    return (summed["loss"][0], gx[None], *[out(n, grads[n]) for n in WEIGHT_NAMES], *[out(n, upd[n][0]) for n in WEIGHT_NAMES],
            *[out(n, upd[n][1]) for n in WEIGHT_NAMES], *[out(n, upd[n][2]) for n in WEIGHT_NAMES])
```

```python
import functools
import math

import jax
import jax.numpy as jnp
import numpy as np
from jax import lax
from jax.experimental import pallas as pl
from jax.experimental.pallas import tpu as pltpu
from jax.experimental.pallas import tpu_sc as plsc

F32 = jnp.float32
BF16 = jnp.bfloat16
MESH = pl.DeviceIdType.MESH

D_MODEL = 1024
FFN_DIM = 2816
RG_C = 8.0
ATT_GROUPS = ((128, 1), (512, 4), (2048, 16))
ATT_HEADS = 12
ATT_HEAD_DIM = 64
ATT_SPAN = 128
DN_HEADS = 8
DN_HEAD_DIM = 128
DN_CHUNK = 64
EPS = 1e-6
NEG_INF = -1e30
N_CHIPS = 4
N_DEV = 8

ADAM_LR, ADAM_B1, ADAM_B2, ADAM_EPS, ADAM_WD, ADAM_STEP = 0.001, 0.9, 0.999, 1e-08, 0.01, 10

LANES = 128
VMEM_LIMIT = 56 * 1024 * 1024


def _params(*sem):
    return pltpu.CompilerParams(dimension_semantics=sem or None, vmem_limit_bytes=VMEM_LIMIT)


def _sigmoid(x):
    return 1.0 / (1.0 + jnp.exp(-x))


def _silu(x):
    return x * _sigmoid(x)


def _softplus(x):
    return jnp.maximum(x, 0.0) + jnp.log(1.0 + jnp.exp(-jnp.abs(x)))


def _gelu(x):
    return 0.5 * x * (1.0 + jnp.tanh(math.sqrt(2.0 / math.pi) * (x + 0.044715 * (x * x * x))))


def _neg_expm1(x):
    series = -x * (1.0 + x * (0.5 + x * (1.0 / 6 + x * (1.0 / 24 + x * (1.0 / 120 + x * (1.0 / 720))))))
    return jnp.where(x > -0.25, series, 1.0 - jnp.exp(x))


def _rms(x, g):
    return x * lax.rsqrt(jnp.mean(x * x, axis=-1, keepdims=True) + EPS) * g


_MM_DIMS = {"nn": (((1,), (0,)), ((), ())), "nt": (((1,), (1,)), ((), ())), "tn": (((0,), (0,)), ((), ()))}


def _split(a):
    hi = a.astype(BF16)
    return hi, (a - hi.astype(F32)).astype(BF16)


def _mxu(a, b, form, passes):
    (ca, cb), _ = _MM_DIMS[form]
    if a.ndim == 3:
        dims = (((ca[0] + 1,), (cb[0] + 1,)), ((0,), (0,)))
    else:
        dims = _MM_DIMS[form]
    dg = lambda p, q: lax.dot_general(p, q, dims, preferred_element_type=F32)
    if passes == 1:
        return dg(a.astype(BF16), b.astype(BF16))
    (a_hi, a_lo), (b_hi, b_lo) = _split(a), _split(b)
    return dg(a_hi, b_hi) + (dg(a_hi, b_lo) + dg(a_lo, b_hi))


@functools.partial(jax.custom_vjp, nondiff_argnums=(2, 3))
def _mm(a, b, form, passes):
    return _mxu(a, b, form, passes)


def _mm_fwd(a, b, form, passes):
    return _mxu(a, b, form, passes), (a, b)


def _mm_bwd(form, passes, res, g):
    a, b = res
    if form == "nn":
        return _mm(g, b, "nt", passes), _mm(a, g, "tn", passes)
    if form == "nt":
        return _mm(g, b, "nn", passes), _mm(g, a, "tn", passes)
    return _mm(b, g, "nt", passes), _mm(a, g, "nn", passes)


_mm.defvjp(_mm_fwd, _mm_bwd)


def _dot(a, b):
    return _mm(a, b, "nn", 1)


def _dot_nt(a, b):
    return _mm(a, b, "nt", 1)


def _dot_tn(a, b):
    return _mm(a, b, "tn", 1)


def _dot3(a, b):
    return _mm(a, b, "nn", 3)


def _rows(shape):
    return lax.broadcasted_iota(jnp.int32, shape, len(shape) - 2)


def _roll_down(x, s, fill):
    return jnp.where(_rows(x.shape) >= s, pltpu.roll(x, s, x.ndim - 2), fill)


def _roll_up(x, s, fill):
    n = x.shape[-2]
    return jnp.where(_rows(x.shape) < n - s, pltpu.roll(x, n - s, x.ndim - 2), fill)


@functools.partial(jax.custom_vjp, nondiff_argnums=(1,))
def _shift(x, s):
    return _roll_down(x, s, 0.0)


def _shift_fwd(x, s):
    return _roll_down(x, s, 0.0), None


def _shift_bwd(s, _, g):
    return (_roll_up(g, s, 0.0),)


_shift.defvjp(_shift_fwd, _shift_bwd)


def _causal_conv(x, w):
    return w[0:1] * _shift(x, 3) + w[1:2] * _shift(x, 2) + w[2:3] * _shift(x, 1) + w[3:4] * x


@jax.custom_vjp
def _lin_scan(a, b):
    return _lin_scan_fwd(a, b)[0]


def _lin_scan_fwd(a, b):
    a0 = a
    s = 1
    while s < a.shape[0]:
        b = a * _roll_down(b, s, 0.0) + b
        a = a * _roll_down(a, s, 1.0)
        s *= 2
    return b, (a0, b)


def _lin_scan_bwd(res, g):
    a, h = res
    c = _roll_up(a, 1, 0.0)
    s = 1
    while s < a.shape[0]:
        g = c * _roll_up(g, s, 0.0) + g
        c = c * _roll_up(c, s, 1.0)
        s *= 2
    return g * _roll_down(h, 1, 0.0), g


_lin_scan.defvjp(_lin_scan_fwd, _lin_scan_bwd)


@jax.custom_vjp
def _cumsum_rows(x):
    s = 1
    while s < x.shape[-2]:
        x = x + _roll_down(x, s, 0.0)
        s *= 2
    return x


def _cumsum_rows_fwd(x):
    return _cumsum_rows(x), None


def _cumsum_rows_bwd(_, g):
    s = 1
    while s < g.shape[-2]:
        g = g + _roll_up(g, s, 0.0)
        s *= 2
    return (g,)


_cumsum_rows.defvjp(_cumsum_rows_fwd, _cumsum_rows_bwd)


ROW_BLOCK_BYTES = 14 * 1024 * 1024


def _row_tile(t, width=0):
    for tile in (512, 256):
        if t % tile == 0 and (tile == 256 or tile * width * 4 <= ROW_BLOCK_BYTES):
            return tile
    return t


def _rowwise_fwd_call(name, f, rows, pars, tile):
    t = rows[0].shape[0]
    outs = jax.eval_shape(f, *[jax.ShapeDtypeStruct((tile, r.shape[1]), F32) for r in rows],
                          *[jax.ShapeDtypeStruct(p.shape, F32) for p in pars])
    nr, npar = len(rows), len(pars)

    def body(*refs):
        ins = [r[...] for r in refs[:nr + npar]]
        res = f(*ins)
        for o_ref, o in zip(refs[nr + npar:], res):
            o_ref[...] = o.astype(o_ref.dtype)

    return pl.pallas_call(
        body, name=name, grid=(t // tile,),
        in_specs=[pl.BlockSpec((tile, r.shape[1]), lambda i: (i, 0)) for r in rows]
        + [pl.BlockSpec(p.shape, lambda i: (0, 0)) for p in pars],
        out_specs=[pl.BlockSpec((tile, o.shape[1]), lambda i: (i, 0)) for o in outs],
        out_shape=[jax.ShapeDtypeStruct((t, o.shape[1]), F32) for o in outs],
        compiler_params=_params("parallel"),
    )(*rows, *pars)


def _rowwise_bwd_call(name, f, rows, pars, cts, tile):
    t = rows[0].shape[0]
    nr, npar, nct = len(rows), len(pars), len(cts)

    def body(*refs):
        ins = [r[...] for r in refs[:nr + npar]]
        gs = tuple(r[...] for r in refs[nr + npar:nr + npar + nct])
        outs = refs[nr + npar + nct:]
        _, vjp = jax.vjp(f, *ins)
        d = vjp(gs)
        for o_ref, v in zip(outs[:nr], d[:nr]):
            o_ref[...] = v

        @pl.when(pl.program_id(0) == 0)
        def _():
            for o_ref in outs[nr:]:
                o_ref[...] = jnp.zeros_like(o_ref)

        for o_ref, v in zip(outs[nr:], d[nr:]):
            o_ref[...] += v

    res = pl.pallas_call(
        body, name=name, grid=(t // tile,),
        in_specs=[pl.BlockSpec((tile, r.shape[1]), lambda i: (i, 0)) for r in rows]
        + [pl.BlockSpec(p.shape, lambda i: (0, 0)) for p in pars]
        + [pl.BlockSpec((tile, c.shape[1]), lambda i: (i, 0)) for c in cts],
        out_specs=[pl.BlockSpec((tile, r.shape[1]), lambda i: (i, 0)) for r in rows]
        + [pl.BlockSpec(p.shape, lambda i: (0, 0)) for p in pars],
        out_shape=[jax.ShapeDtypeStruct(r.shape, F32) for r in rows]
        + [jax.ShapeDtypeStruct(p.shape, F32) for p in pars],
        compiler_params=_params("arbitrary"),
    )(*rows, *pars, *cts)
    return tuple(res[:nr]), tuple(res[nr:])


def rowwise(name, f, rows, pars=()):
    outs = jax.eval_shape(f, *[jax.ShapeDtypeStruct((8, r.shape[1]), F32) for r in rows],
                          *[jax.ShapeDtypeStruct(p.shape, F32) for p in pars])
    tile = _row_tile(rows[0].shape[0], 2 * sum(r.shape[1] for r in rows) + sum(o.shape[1] for o in outs))

    @jax.custom_vjp
    def op(rows, pars):
        return tuple(_rowwise_fwd_call(name, f, rows, pars, tile))

    def op_fwd(rows, pars):
        return op(rows, pars), (rows, pars)

    def op_bwd(res, cts):
        return _rowwise_bwd_call(name + "_bwd", f, res[0], res[1], tuple(cts), tile)

    op.defvjp(op_fwd, op_bwd)
    return op(tuple(rows), tuple(pars))


MM_TM = 512


def _tile_of(n, cap):
    best = None
    for c in range(LANES, min(n, cap) + 1, LANES):
        if n % c == 0:
            best = c
    return best or n


def _mmc_fwd(name, h, w):
    m, k = h.shape
    j, _, n = w.shape
    tm, tn = MM_TM, _tile_of(n, 1408)

    def body(h_ref, w_ref, o_ref):
        o_ref[0] = _dot(h_ref[...], w_ref[0])

    return pl.pallas_call(
        body, name=name, grid=(m // tm, j, n // tn),
        in_specs=[pl.BlockSpec((tm, k), lambda i, b, c: (i, 0)), pl.BlockSpec((1, k, tn), lambda i, b, c: (b, 0, c))],
        out_specs=pl.BlockSpec((1, tm, tn), lambda i, b, c: (b, i, c)),
        out_shape=jax.ShapeDtypeStruct((j, m, n), F32),
        compiler_params=_params("parallel", "parallel", "parallel"),
    )(h, w)


def _mmc_dw(name, h, dy):
    m, k = h.shape
    j, _, n = dy.shape
    tk, tn = _tile_of(k, 512), _tile_of(n, 1152)

    def body(h_ref, dy_ref, o_ref):
        o_ref[0] = _dot_tn(h_ref[...], dy_ref[0]).astype(BF16)

    return pl.pallas_call(
        body, name=name, grid=(j, k // tk, n // tn),
        in_specs=[pl.BlockSpec((m, tk), lambda b, i, c: (0, i)), pl.BlockSpec((1, m, tn), lambda b, i, c: (b, 0, c))],
        out_specs=pl.BlockSpec((1, tk, tn), lambda b, i, c: (b, i, c)),
        out_shape=jax.ShapeDtypeStruct((j, k, n), BF16),
        compiler_params=_params("parallel", "parallel", "parallel"),
    )(h, dy)


PROJ_GROUP_COLS = 4608


def _proj_dh(name, dys, ws, acc):
    m, k = dys[0].shape[0], ws[0].shape[0]
    n, tm = len(dys), 256

    def body(*refs):
        dy_refs, w_refs, rest = refs[:n], refs[n:2 * n], refs[2 * n:]
        total = _dot_nt(dy_refs[0][...], w_refs[0][...])
        for dy_ref, w_ref in zip(dy_refs[1:], w_refs[1:]):
            total = total + _dot_nt(dy_ref[...], w_ref[...])
        if acc is not None:
            total = total + rest[0][...]
        rest[-1][...] = total

    row = lambda width: pl.BlockSpec((tm, width), lambda i: (i, 0))
    return pl.pallas_call(
        body, name=name, grid=(m // tm,),
        in_specs=[row(d.shape[1]) for d in dys] + [pl.BlockSpec(w.shape, lambda i: (0, 0)) for w in ws] + ([row(k)] if acc is not None else []),
        out_specs=row(k), out_shape=jax.ShapeDtypeStruct((m, k), F32), compiler_params=_params("parallel"),
    )(*dys, *ws, *([acc] if acc is not None else []))


def project_in(name, h, ws):
    keys = list(ws)

    @jax.custom_vjp
    def op(h, ws):
        return {p: _mmc_fwd(f"{name}_{p}", h, ws[p][None])[0] for p in keys}

    def op_fwd(h, ws):
        return op(h, ws), (h, ws)

    def op_bwd(res, dys):
        h, ws = res
        groups, cols = [[]], 0
        for p in keys:
            if groups[-1] and cols + ws[p].shape[1] > PROJ_GROUP_COLS:
                groups.append([])
                cols = 0
            groups[-1].append(p)
            cols += ws[p].shape[1]
        dh = None
        for i, group in enumerate(groups):
            dh = _proj_dh(f"{name}_dh{i}", [dys[p] for p in group], [ws[p] for p in group], dh)
        return dh, {p: _mmc_dw(f"{name}_{p}_dw", h, dys[p][None])[0] for p in keys}

    op.defvjp(op_fwd, op_bwd)
    return op(h, ws)


def _ffn_up(name, h, wt):
    m, k = h.shape
    j, n, _ = wt.shape
    tm = MM_TM

    def body(h_ref, w_ref, o_ref):
        o_ref[0] = _dot_nt(h_ref[...], w_ref[0])

    return pl.pallas_call(
        body, name=name, grid=(m // tm, j),
        in_specs=[pl.BlockSpec((tm, k), lambda i, b: (i, 0)), pl.BlockSpec((1, n, k), lambda i, b: (b, 0, 0))],
        out_specs=pl.BlockSpec((1, tm, n), lambda i, b: (b, i, 0)),
        out_shape=jax.ShapeDtypeStruct((j, m, n), F32), compiler_params=_params("parallel", "parallel"),
    )(h, wt)


def _ffn_down(name, g, u, wd):
    j, m, n = g.shape
    d = wd.shape[2]
    tm = MM_TM

    def body(g_ref, u_ref, w_ref, o_ref):
        part = _dot(_silu(g_ref[0]) * u_ref[0], w_ref[0])

        @pl.when(pl.program_id(1) == 0)
        def _():
            o_ref[...] = part

        @pl.when(pl.program_id(1) > 0)
        def _():
            o_ref[...] += part

    act = pl.BlockSpec((1, tm, n), lambda i, b: (b, i, 0))
    return pl.pallas_call(
        body, name=name, grid=(m // tm, j),
        in_specs=[act, act, pl.BlockSpec((1, n, d), lambda i, b: (b, 0, 0))],
        out_specs=pl.BlockSpec((tm, d), lambda i, b: (i, 0)),
        out_shape=jax.ShapeDtypeStruct((m, d), F32), compiler_params=_params("parallel", "arbitrary"),
    )(g, u, wd)


def _ffn_down_bwd(name, dy, g, u, wd):
    j, m, n = g.shape
    d = wd.shape[2]
    tm = MM_TM

    def body(dy_ref, g_ref, u_ref, w_ref, dg_ref, du_ref):
        da = _dot_nt(dy_ref[...], w_ref[0])
        gv = g_ref[0]
        s = _sigmoid(gv)
        dg_ref[0] = da * u_ref[0] * (s * (1.0 + gv * (1.0 - s)))
        du_ref[0] = da * (gv * s)

    act = pl.BlockSpec((1, tm, n), lambda i, b: (b, i, 0))
    return pl.pallas_call(
        body, name=name, grid=(m // tm, j),
        in_specs=[pl.BlockSpec((tm, d), lambda i, b: (i, 0)), act, act, pl.BlockSpec((1, n, d), lambda i, b: (b, 0, 0))],
        out_specs=[act, act], out_shape=[jax.ShapeDtypeStruct((j, m, n), F32)] * 2,
        compiler_params=_params("parallel", "parallel"),
    )(dy, g, u, wd)


def _ffn_down_dw(name, g, u, dy):
    j, m, n = g.shape
    d = dy.shape[1]
    tn = _tile_of(d, 512)

    def body(g_ref, u_ref, dy_ref, o_ref):
        o_ref[0] = _dot_tn(_silu(g_ref[0]) * u_ref[0], dy_ref[...]).astype(BF16)

    act = pl.BlockSpec((1, m, n), lambda b, c: (b, 0, 0))
    return pl.pallas_call(
        body, name=name, grid=(j, d // tn),
        in_specs=[act, act, pl.BlockSpec((m, tn), lambda b, c: (0, c))],
        out_specs=pl.BlockSpec((1, n, tn), lambda b, c: (b, 0, c)),
        out_shape=jax.ShapeDtypeStruct((j, n, d), BF16), compiler_params=_params("parallel", "parallel"),
    )(g, u, dy)


def _ffn_up_dh(name, dg, du, wg, wu):
    j, m, n = dg.shape
    k = wg.shape[2]
    tm = MM_TM

    def body(dg_ref, du_ref, wg_ref, wu_ref, o_ref):
        part = _dot(dg_ref[0], wg_ref[0]) + _dot(du_ref[0], wu_ref[0])

        @pl.when(pl.program_id(1) == 0)
        def _():
            o_ref[...] = part

        @pl.when(pl.program_id(1) > 0)
        def _():
            o_ref[...] += part

    act = pl.BlockSpec((1, tm, n), lambda i, b: (b, i, 0))
    wsp = pl.BlockSpec((1, n, k), lambda i, b: (b, 0, 0))
    return pl.pallas_call(
        body, name=name, grid=(m // tm, j), in_specs=[act, act, wsp, wsp],
        out_specs=pl.BlockSpec((tm, k), lambda i, b: (i, 0)),
        out_shape=jax.ShapeDtypeStruct((m, k), F32), compiler_params=_params("parallel", "arbitrary"),
    )(dg, du, wg, wu)


def _ffn_up_dw(name, dy, h):
    j, m, n = dy.shape
    k = h.shape[1]
    tk = _tile_of(k, 512)

    def body(dy_ref, h_ref, o_ref):
        o_ref[0] = _dot_tn(dy_ref[0], h_ref[...]).astype(BF16)

    return pl.pallas_call(
        body, name=name, grid=(j, k // tk),
        in_specs=[pl.BlockSpec((1, m, n), lambda b, i: (b, 0, 0)), pl.BlockSpec((m, tk), lambda b, i: (0, i))],
        out_specs=pl.BlockSpec((1, n, tk), lambda b, i: (b, 0, i)),
        out_shape=jax.ShapeDtypeStruct((j, n, k), BF16), compiler_params=_params("parallel", "parallel"),
    )(dy, h)


def ffn(name, h, wg, wu, wd):
    @jax.custom_vjp
    def op(h, wg, wu, wd):
        return _ffn_down(name + "_d", _ffn_up(name + "_g", h, wg), _ffn_up(name + "_u", h, wu), wd)

    def op_fwd(h, wg, wu, wd):
        g, u = _ffn_up(name + "_g", h, wg), _ffn_up(name + "_u", h, wu)
        return _ffn_down(name + "_d", g, u, wd), (h, g, u, wg, wu, wd)

    def op_bwd(res, dy):
        h, g, u, wg, wu, wd = res
        dg, du = _ffn_down_bwd(name + "_d_bwd", dy, g, u, wd)
        return (_ffn_up_dh(name + "_dh", dg, du, wg, wu), _ffn_up_dw(name + "_g_dw", dg, h), _ffn_up_dw(name + "_u_dw", du, h),
                _ffn_down_dw(name + "_d_dw", g, u, dy))

    op.defvjp(op_fwd, op_bwd)
    return op(h, wg, wu, wd)


def _mmr_fwd(name, a, w):
    j, m, n = a.shape
    nn = w.shape[2]
    tm, tn = MM_TM, _tile_of(nn, 1024)

    def body(a_ref, w_ref, o_ref):
        part = _dot(a_ref[0], w_ref[0])

        @pl.when(pl.program_id(2) == 0)
        def _():
            o_ref[...] = part

        @pl.when(pl.program_id(2) > 0)
        def _():
            o_ref[...] += part

    return pl.pallas_call(
        body, name=name, grid=(m // tm, nn // tn, j),
        in_specs=[pl.BlockSpec((1, tm, n), lambda i, c, b: (b, i, 0)), pl.BlockSpec((1, n, tn), lambda i, c, b: (b, 0, c))],
        out_specs=pl.BlockSpec((tm, tn), lambda i, c, b: (i, c)),
        out_shape=jax.ShapeDtypeStruct((m, nn), F32),
        compiler_params=_params("parallel", "parallel", "arbitrary"),
    )(a, w)


def _mmr_da(name, dy, w):
    m, nn = dy.shape
    j, n, _ = w.shape
    tm = MM_TM

    def body(dy_ref, w_ref, o_ref):
        o_ref[0] = _dot_nt(dy_ref[...], w_ref[0])

    return pl.pallas_call(
        body, name=name, grid=(m // tm, j),
        in_specs=[pl.BlockSpec((tm, nn), lambda i, b: (i, 0)), pl.BlockSpec((1, n, nn), lambda i, b: (b, 0, 0))],
        out_specs=pl.BlockSpec((1, tm, n), lambda i, b: (b, i, 0)),
        out_shape=jax.ShapeDtypeStruct((j, m, n), F32),
        compiler_params=_params("parallel", "parallel"),
    )(dy, w)


def _mmr_dw(name, a, dy):
    j, m, n = a.shape
    nn = dy.shape[1]
    tn = _tile_of(nn, 512)

    def body(a_ref, dy_ref, o_ref):
        o_ref[0] = _dot_tn(a_ref[0], dy_ref[...]).astype(BF16)

    return pl.pallas_call(
        body, name=name, grid=(j, nn // tn),
        in_specs=[pl.BlockSpec((1, m, n), lambda b, c: (b, 0, 0)), pl.BlockSpec((m, tn), lambda b, c: (0, c))],
        out_specs=pl.BlockSpec((1, n, tn), lambda b, c: (b, 0, c)),
        out_shape=jax.ShapeDtypeStruct((j, n, nn), BF16),
        compiler_params=_params("parallel", "parallel"),
    )(a, dy)


def mm_rows(name, a, w):
    @jax.custom_vjp
    def op(a, w):
        return _mmr_fwd(name, a, w)

    def op_fwd(a, w):
        return op(a, w), (a, w)

    def op_bwd(res, dy):
        a, w = res
        return _mmr_da(name + "_da", dy, w), _mmr_dw(name + "_dw", a, dy)

    op.defvjp(op_fwd, op_bwd)
    return op(a, w)


def _colwise_specs(cols, pars, par_block):
    t = cols[0].shape[0]
    specs = [pl.BlockSpec((t, LANES), lambda j: (0, j)) for _ in cols]
    for p, blk in zip(pars, par_block):
        if blk == "lane":
            specs.append(pl.BlockSpec((p.shape[0], LANES), lambda j: (0, j)))
        else:
            specs.append(pl.BlockSpec((1,) + p.shape[1:], lambda j: (j, 0, 0)))
    return specs


def _colwise_fwd_call(name, f, cols, pars, par_block, n_out):
    t, c = cols[0].shape
    nc, npar = len(cols), len(pars)

    def body(*refs):
        ins = [r[...] for r in refs[:nc]] + [r[...] if b == "lane" else r[0] for r, b in zip(refs[nc:nc + npar], par_block)]
        res = f(*ins)
        for o_ref, o in zip(refs[nc + npar:], res):
            o_ref[...] = o

    return pl.pallas_call(
        body, name=name, grid=(c // LANES,),
        in_specs=_colwise_specs(cols, pars, par_block),
        out_specs=[pl.BlockSpec((t, LANES), lambda j: (0, j)) for _ in range(n_out)],
        out_shape=[jax.ShapeDtypeStruct((t, c), F32) for _ in range(n_out)],
        compiler_params=_params("parallel"),
    )(*cols, *pars)


def _colwise_bwd_call(name, f, cols, pars, par_block, cts):
    t, c = cols[0].shape
    nc, npar, nct = len(cols), len(pars), len(cts)

    def body(*refs):
        ins = [r[...] for r in refs[:nc]] + [r[...] if b == "lane" else r[0] for r, b in zip(refs[nc:nc + npar], par_block)]
        gs = tuple(r[...] for r in refs[nc + npar:nc + npar + nct])
        outs = refs[nc + npar + nct:]
        _, vjp = jax.vjp(f, *ins)
        d = vjp(gs)
        for o_ref, v in zip(outs[:nc], d[:nc]):
            o_ref[...] = v
        for o_ref, v, b in zip(outs[nc:], d[nc:], par_block):
            if b == "lane":
                o_ref[...] = v
            else:
                o_ref[0] = v

    res = pl.pallas_call(
        body, name=name, grid=(c // LANES,),
        in_specs=_colwise_specs(cols, pars, par_block) + [pl.BlockSpec((t, LANES), lambda j: (0, j)) for _ in cts],
        out_specs=_colwise_specs(cols, pars, par_block),
        out_shape=[jax.ShapeDtypeStruct(v.shape, F32) for v in (*cols, *pars)],
        compiler_params=_params("parallel"),
    )(*cols, *pars, *cts)
    return tuple(res[:nc]), tuple(res[nc:])


def colwise(name, f, cols, pars, par_block, n_out):
    @jax.custom_vjp
    def op(cols, pars):
        return tuple(_colwise_fwd_call(name, f, cols, pars, par_block, n_out))

    def op_fwd(cols, pars):
        return op(cols, pars), (cols, pars)

    def op_bwd(res, cts):
        return _colwise_bwd_call(name + "_bwd", f, res[0], res[1], par_block, tuple(cts))

    op.defvjp(op_fwd, op_bwd)
    return op(tuple(cols), tuple(pars))


def _rg_block(x, gate, cw, cb, wr, br, wi, bi, lam):
    xa = _causal_conv(x, cw) + cb
    r = _sigmoid(_dot(xa, wr) + br)
    i = _sigmoid(_dot(xa, wi) + bi)
    log_a = -RG_C * r * _softplus(-lam)
    a = jnp.exp(log_a)
    b = jnp.sqrt(_neg_expm1(2.0 * log_a)) * (i * xa)
    return (_lin_scan(a, b) * _gelu(gate),)


def _dn_conv_block(mode):
    def f(x, cw):
        c = _silu(_causal_conv(x, cw))
        if mode == "v":
            return (c,)
        c = c * lax.rsqrt(jnp.sum(c * c, axis=-1, keepdims=True) + EPS)
        return (c * (DN_HEAD_DIM ** -0.5),) if mode == "q" else (c,)
    return f


def _block_diag(w):
    w = w.reshape(8, 2, 64, 64)
    z = jnp.zeros((8, 64, 64), w.dtype)
    top = jnp.concatenate([w[:, 0], z], axis=2)
    bot = jnp.concatenate([z, w[:, 1]], axis=2)
    return jnp.concatenate([top, bot], axis=1)


DN_HP = 8


def _dn_block(S, qw, kw, vw, gb, h0):
    hp, hd = S.shape[0], DN_HEAD_DIM
    heads = lambda a: jnp.concatenate([a[None, :, j * hd:(j + 1) * hd] for j in range(hp)], axis=0)
    lane = lax.broadcasted_iota(jnp.int32, gb.shape, 1)
    col = lambda i: jnp.sum(jnp.where(lane == i, gb, 0.0), axis=1, keepdims=True)[None]
    beta = jnp.concatenate([col(h0 + j) for j in range(hp)], axis=0)
    g = jnp.concatenate([col(h0 + j + DN_HEADS) for j in range(hp)], axis=0)
    s_new, o = _dn_step(S, heads(qw), heads(kw), heads(vw), beta, g)
    return s_new, jnp.concatenate([o[j:j + 1].reshape(o.shape[1:]) for j in range(hp)], axis=1)


def _dn_step(S, q, k, v, beta, g):
    c = DN_CHUNK
    ri = lax.broadcasted_iota(jnp.int32, (c, c), 0)
    ci = lax.broadcasted_iota(jnp.int32, (c, c), 1)
    incl, strict = ri >= ci, ri > ci
    eye = (ri == ci).astype(F32)
    gam = _cumsum_rows(g)
    gam_row = jnp.sum(jnp.where(ri <= ci, g, 0.0), axis=-2, keepdims=True)
    gam_last = jnp.sum(g, axis=-2, keepdims=True)
    decay = jnp.where(incl, jnp.exp(jnp.where(incl, gam - gam_row, 0.0)), 0.0)
    kb = k * beta
    vb = v * beta
    a = jnp.where(strict, _dot_nt(kb, k) * decay, 0.0)
    p = -a
    tinv = eye + p
    for _ in range(5):
        p = _dot3(p, p)
        tinv = tinv + _dot3(tinv, p)
    e_gam = jnp.exp(gam)
    u0 = _dot3(tinv, vb)
    wk = _dot3(tinv, kb * e_gam)
    qk = jnp.where(incl, _dot_nt(q, k) * decay, 0.0)
    q_dec = q * e_gam
    k_dec = k * jnp.exp(gam_last - gam)
    u = u0 - _dot(wk, S)
    o = _dot(q_dec, S) + _dot(qk, u)
    s_new = S * jnp.exp(gam_last) + _dot_tn(k_dec, u)
    return s_new, o


def _dn_fwd_call(q, k, v, gb):
    t, w = q.shape
    n, hp, hd, c = t // DN_CHUNK, DN_HP, DN_HEAD_DIM, DN_CHUNK

    def body(q_ref, k_ref, v_ref, gb_ref, o_ref, s0_ref, s_scr):
        @pl.when(pl.program_id(1) == 0)
        def _():
            s_scr[...] = jnp.zeros_like(s_scr)

        s_old = s_scr[...]
        s0_ref[:, 0] = s_old
        s_new, o = _dn_block(s_old, q_ref[...], k_ref[...], v_ref[...], gb_ref[...], pl.program_id(0) * hp)
        o_ref[...] = o
        s_scr[...] = s_new

    blk = pl.BlockSpec((c, hp * hd), lambda g, i: (i, g))
    return pl.pallas_call(
        body, name="dn_core", grid=(DN_HEADS // hp, n),
        in_specs=[blk, blk, blk, pl.BlockSpec((c, LANES), lambda g, i: (i, 0))],
        out_specs=[blk, pl.BlockSpec((hp, 1, hd, hd), lambda g, i: (g, i, 0, 0))],
        out_shape=[jax.ShapeDtypeStruct((t, w), F32), jax.ShapeDtypeStruct((DN_HEADS, n, hd, hd), F32)],
        scratch_shapes=[pltpu.VMEM((hp, hd, hd), F32)],
        compiler_params=_params("parallel", "arbitrary"),
    )(q, k, v, gb)


def _dn_bwd_call(q, k, v, gb, s0, do):
    t, w = q.shape
    n, hp, hd, c = t // DN_CHUNK, DN_HP, DN_HEAD_DIM, DN_CHUNK
    ng = DN_HEADS // hp

    def body(q_ref, k_ref, v_ref, gb_ref, s0_ref, do_ref, dq_ref, dk_ref, dv_ref, dgb_ref, ds_scr):
        @pl.when(pl.program_id(1) == 0)
        def _():
            ds_scr[...] = jnp.zeros_like(ds_scr)

        h0 = pl.program_id(0) * hp
        _, vjp = jax.vjp(lambda *a: _dn_block(*a, h0), s0_ref[:, 0], q_ref[...], k_ref[...], v_ref[...], gb_ref[...])
        ds, dq, dk, dv, dgb = vjp((ds_scr[...], do_ref[...]))
        ds_scr[...] = ds
        dq_ref[...], dk_ref[...], dv_ref[...] = dq, dk, dv
        dgb_ref[0] = dgb

    blk = pl.BlockSpec((c, hp * hd), lambda g, i: (n - 1 - i, g))
    res = pl.pallas_call(
        body, name="dn_core_bwd", grid=(ng, n),
        in_specs=[blk, blk, blk, pl.BlockSpec((c, LANES), lambda g, i: (n - 1 - i, 0)),
                  pl.BlockSpec((hp, 1, hd, hd), lambda g, i: (g, n - 1 - i, 0, 0)), blk],
        out_specs=[blk, blk, blk, pl.BlockSpec((1, c, LANES), lambda g, i: (g, n - 1 - i, 0))],
        out_shape=[jax.ShapeDtypeStruct((t, w), F32)] * 3 + [jax.ShapeDtypeStruct((ng, t, LANES), F32)],
        scratch_shapes=[pltpu.VMEM((hp, hd, hd), F32)],
        compiler_params=_params("parallel", "arbitrary"),
    )(q, k, v, gb, s0, do)
    return res[0], res[1], res[2], jnp.sum(res[3], axis=0)


@jax.custom_vjp
def dn_core(q, k, v, gb):
    return _dn_fwd_call(q, k, v, gb)[0]


def _dn_core_fwd(q, k, v, gb):
    o, s0 = _dn_fwd_call(q, k, v, gb)
    return o, (q, k, v, gb, s0)


def _dn_core_bwd(res, do):
    return _dn_bwd_call(*res, do)


dn_core.defvjp(_dn_core_fwd, _dn_core_bwd)


ATT_GH = 4


def _att_block(q, kp, kc, vp, vc, qn, kn, slope, has_prev, dil):
    s = ATT_SPAN
    qh = _rms(q, qn) * (ATT_HEAD_DIM ** -0.5)
    qi = lax.broadcasted_iota(jnp.int32, (s, s), 0)
    kj = lax.broadcasted_iota(jnp.int32, (s, s), 1)
    d_p = qi + s - kj
    d_c = qi - kj
    s_p = _dot_nt(qh, _rms(kp, kn)) - slope * (d_p * dil).astype(F32)
    s_c = _dot_nt(qh, _rms(kc, kn)) - slope * (d_c * dil).astype(F32)
    s_p = jnp.where((d_p <= s) & (has_prev > 0), s_p, NEG_INF)
    s_c = jnp.where(d_c >= 0, s_c, NEG_INF)
    m = lax.stop_gradient(jnp.maximum(jnp.max(s_p, axis=-1, keepdims=True), jnp.max(s_c, axis=-1, keepdims=True)))
    p_p = jnp.exp(s_p - m)
    p_c = jnp.exp(s_c - m)
    den = jnp.sum(p_p, axis=-1, keepdims=True) + jnp.sum(p_c, axis=-1, keepdims=True)
    o = _dot(p_p / den, vp) + _dot(p_c / den, vc)
    lse = m + jnp.log(den)
    return o, jnp.broadcast_to(lse, o.shape)


def _att_heads(a):
    e = ATT_HEAD_DIM
    return jnp.concatenate([a[None, :, h * e:(h + 1) * e] for h in range(ATT_GH)], axis=0)


def _att_lanes(a):
    return jnp.concatenate([a[h:h + 1].reshape(a.shape[1:]) for h in range(ATT_GH)], axis=1)


def _att_rows(q, kp, kc, vp, vc, qn, kn, group, has_prev, dil):
    head = lax.broadcasted_iota(jnp.int32, (ATT_GH, 1, 1), 0) + (ATT_GH * group + 1)
    slope = jnp.exp(head.astype(F32) * (-8.0 / ATT_HEADS * math.log(2.0)))
    o, lse = _att_block(_att_heads(q), _att_heads(kp), _att_heads(kc), _att_heads(vp), _att_heads(vc), qn, kn, slope, has_prev, dil)
    return _att_lanes(o), _att_lanes(lse)


def _att_specs(group, dil):
    blk = (ATT_SPAN, ATT_GH * ATT_HEAD_DIM)
    cur = lambda which: pl.BlockSpec(blk, lambda r, n: (n, r * 9 + 3 * which + group))
    prev = lambda which: pl.BlockSpec(blk, lambda r, n: (jnp.maximum(n - 1, 0), r * 9 + 3 * which + group))
    out = pl.BlockSpec(blk, lambda r, n: (n, r))
    gain = pl.BlockSpec((ATT_GH, 1, ATT_HEAD_DIM), lambda r, n: (0, 0, 0))
    return [cur(0), prev(1), cur(1), prev(2), cur(2), gain, gain], out, gain


def _att_fwd_call(name, group, dil, pa, qn, kn):
    t = pa.shape[0]
    l = t // dil
    w = ATT_GH * ATT_HEAD_DIM
    ins, out, _ = _att_specs(group, dil)
    pav = pa.reshape(l, dil * pa.shape[1])

    def body(q_ref, kp_ref, kc_ref, vp_ref, vc_ref, qn_ref, kn_ref, o_ref, lse_ref):
        o_ref[...], lse_ref[...] = _att_rows(q_ref[...], kp_ref[...], kc_ref[...], vp_ref[...], vc_ref[...], qn_ref[...],
                                             kn_ref[...], group, pl.program_id(1), dil)

    o, lse = pl.pallas_call(
        body, name=name, grid=(dil, l // ATT_SPAN), in_specs=ins, out_specs=[out, out],
        out_shape=[jax.ShapeDtypeStruct((l, dil * w), F32)] * 2, compiler_params=_params("parallel", "arbitrary"),
    )(pav, pav, pav, pav, pav, qn, kn)
    return o.reshape(t, w), lse.reshape(t, w)


def _att_bwd_call(name, group, dil, pa, qn, kn, do, dlse):
    t = pa.shape[0]
    l = t // dil
    w = ATT_GH * ATT_HEAD_DIM
    ins, out, gain = _att_specs(group, dil)
    pav = pa.reshape(l, dil * pa.shape[1])

    def body(q_ref, kp_ref, kc_ref, vp_ref, vc_ref, qn_ref, kn_ref, do_ref, dlse_ref,
             dq_ref, dkp_ref, dkc_ref, dvp_ref, dvc_ref, dqn_ref, dkn_ref):
        has_prev = pl.program_id(1)
        _, vjp = jax.vjp(lambda *a: _att_rows(*a, group, has_prev, dil), q_ref[...], kp_ref[...], kc_ref[...], vp_ref[...],
                         vc_ref[...], qn_ref[...], kn_ref[...])
        dq, dkp, dkc, dvp, dvc, dqn, dkn = vjp((do_ref[...], dlse_ref[...]))
        dq_ref[...], dkp_ref[...], dkc_ref[...], dvp_ref[...], dvc_ref[...] = dq, dkp, dkc, dvp, dvc

        @pl.when((pl.program_id(0) == 0) & (pl.program_id(1) == 0))
        def _():
            dqn_ref[...] = jnp.zeros_like(dqn_ref)
            dkn_ref[...] = jnp.zeros_like(dkn_ref)

        dqn_ref[...] += dqn
        dkn_ref[...] += dkn

    res = pl.pallas_call(
        body, name=name + "_bwd", grid=(dil, l // ATT_SPAN), in_specs=ins + [out, out],
        out_specs=[out] * 5 + [gain, gain],
        out_shape=[jax.ShapeDtypeStruct((l, dil * w), F32)] * 5 + [jax.ShapeDtypeStruct(qn.shape, F32)] * 2,
        compiler_params=_params("arbitrary", "arbitrary"),
    )(pav, pav, pav, pav, pav, qn, kn, do.reshape(l, dil * w), dlse.reshape(l, dil * w))
    dq, dkp, dkc, dvp, dvc, dqn, dkn = res
    back = lambda g: jnp.pad(g[ATT_SPAN:], ((0, ATT_SPAN), (0, 0)))
    return dq.reshape(t, w), (dkc + back(dkp)).reshape(t, w), (dvc + back(dvp)).reshape(t, w), dqn, dkn


def _att_mix(o1, o2, o3, l1, l2, l3):
    m = jnp.maximum(jnp.maximum(l1, l2), l3)
    e1, e2, e3 = jnp.exp(l1 - m), jnp.exp(l2 - m), jnp.exp(l3 - m)
    s = e1 + e2 + e3
    return (jnp.concatenate([o1 * (e1 / s), o2 * (e2 / s), o3 * (e3 / s)], axis=1),)


def att_branch(name, pa, qn, kn):
    e = ATT_HEAD_DIM
    gains = lambda p, g: p[ATT_GH * g:ATT_GH * (g + 1)].reshape(ATT_GH, 1, e)

    @jax.custom_vjp
    def groups(pa, qn, kn):
        res = [_att_fwd_call(f"{name}_att{g}", g, dil, pa, gains(qn, g), gains(kn, g)) for g, (_, dil) in enumerate(ATT_GROUPS)]
        return tuple(r[0] for r in res) + tuple(r[1] for r in res)

    def groups_fwd(pa, qn, kn):
        return groups(pa, qn, kn), (pa, qn, kn)

    def groups_bwd(res, cts):
        pa, qn, kn = res
        n = len(ATT_GROUPS)
        parts = [_att_bwd_call(f"{name}_att{g}", g, dil, pa, gains(qn, g), gains(kn, g), cts[g], cts[n + g])
                 for g, (_, dil) in enumerate(ATT_GROUPS)]
        d_pa = jnp.concatenate([p[i] for i in range(3) for p in parts], axis=1)
        return (d_pa, jnp.concatenate([p[3] for p in parts]).reshape(qn.shape), jnp.concatenate([p[4] for p in parts]).reshape(kn.shape))

    groups.defvjp(groups_fwd, groups_bwd)
    return rowwise(f"{name}_attmix", _att_mix, groups(pa, qn, kn))[0]


def dn_gates(name, ba, a_log, dt_bias):
    place = lambda p: jnp.pad(p.reshape(1, DN_HEADS), ((0, 0), (DN_HEADS, LANES - 2 * DN_HEADS)))

    def f(x, al, dt):
        lane = lax.broadcasted_iota(jnp.int32, x.shape, 1)
        return (jnp.where(lane < DN_HEADS, _sigmoid(x), -jnp.exp(al) * _softplus(x + dt)),)

    return rowwise(name, f, (ba,), (place(a_log), place(dt_bias)))[0]


def _dn_out(o, z, g):
    parts = []
    for h in range(DN_HEADS):
        sl = slice(h * DN_HEAD_DIM, (h + 1) * DN_HEAD_DIM)
        parts.append(_rms(o[:, sl], g[:, sl]) * _silu(z[:, sl]))
    return (jnp.concatenate(parts, axis=1),)


def _merge(ml, za, zb, zc):
    d = D_MODEL
    return (_sigmoid(ml[:, :d]) * za + _sigmoid(ml[:, d:2 * d]) * zb + _sigmoid(ml[:, 2 * d:]) * zc,)


def add_norm(name, x, pend, scale, gain):
    if pend is None:
        return x, rowwise(name, lambda a, g: (_rms(a, g),), (x,), (gain,))[0]

    def f(a, b, g):
        s = a + scale * b
        return s, _rms(s, g)

    return rowwise(name, f, (x, pend), (gain,))


W_IN_PIECES = (("rgx", 0, 1024), ("gate", 1024, 1024), ("att", 2048, 2304), ("dq", 4352, 1024), ("dk", 5376, 1024),
               ("dv", 6400, 1024), ("dz", 7424, 1024), ("ba", 8448, 16), ("mrg", 8464, 3072))
RG_PAR_BLOCKS = ("lane", "lane", "blk", "lane", "blk", "lane", "lane")


def mixer(name, u, w, p):
    mm = lambda nm, a, wt: mm_rows(nm, a[None], wt[None])
    pr = project_in(name + "_in", u, {k: w["in_" + k] for k, _, _ in W_IN_PIECES})
    ya = colwise(name + "_rg", _rg_block, (pr["rgx"], pr["gate"]),
                 (w["rg_conv_w"], p["rg_conv_b"], _block_diag(p["rg_w_r"]), p["rg_b_r"], _block_diag(p["rg_w_i"]),
                  p["rg_b_i"], p["rg_lambda"]), RG_PAR_BLOCKS, 1)[0]
    yb = att_branch(name, pr["att"], p["att_q_norm"], p["att_k_norm"])
    cw = w["dn_conv_w"]
    cq = colwise(name + "_dnq", _dn_conv_block("q"), (pr["dq"],), (cw[:, :1024],), ("lane",), 1)[0]
    ck = colwise(name + "_dnk", _dn_conv_block("k"), (pr["dk"],), (cw[:, 1024:2048],), ("lane",), 1)[0]
    cv = colwise(name + "_dnv", _dn_conv_block("v"), (pr["dv"],), (cw[:, 2048:],), ("lane",), 1)[0]
    gb = dn_gates(name + "_dngate", pr["ba"], p["dn_a_log"], p["dn_dt_bias"])
    o_dn = dn_core(cq, ck, cv, gb)
    yc = rowwise(name + "_dnout", _dn_out, (o_dn, pr["dz"]), (p["dn_out_norm"].reshape(1, D_MODEL),))[0]
    y = rowwise(name + "_merge", _merge, (pr["mrg"], mm(name + "_ba", ya, w["br_a"]), mm(name + "_bb", yb, w["br_b"]),
                                          mm(name + "_bc", yc, w["br_c"])))[0]
    return mm(name + "_out", y, w["w_out"])


def _loss_call(x, pend, target):
    t, d = x.shape
    tile = _row_tile(t)

    def body(x_ref, p_ref, t_ref, loss_ref, g_ref):
        err = x_ref[...] + 0.5 * p_ref[...] - t_ref[...]
        g_ref[...] = err * (1.0 / d)

        @pl.when(pl.program_id(0) == 0)
        def _():
            loss_ref[...] = jnp.zeros_like(loss_ref)

        loss_ref[...] += jnp.full(loss_ref.shape, 0.5 / d, F32) * jnp.sum(err * err)

    blk = pl.BlockSpec((tile, d), lambda i: (i, 0))
    loss, g = pl.pallas_call(
        body, name="loss", grid=(t // tile,), in_specs=[blk, blk, blk],
        out_specs=[pl.BlockSpec((8, LANES), lambda i: (0, 0)), blk],
        out_shape=[jax.ShapeDtypeStruct((8, LANES), F32), jax.ShapeDtypeStruct((t, d), F32)],
        compiler_params=_params("arbitrary"),
    )(x, pend, target)
    return loss[0, 0], g


@jax.custom_vjp
def loss_op(x, pend, target):
    return _loss_call(x, pend, target)[0]


def _loss_fwd(x, pend, target):
    loss, g = _loss_call(x, pend, target)
    return loss, g


def _loss_bwd(g, ct):
    return ct * g, (0.5 * ct) * g, None


loss_op.defvjp(_loss_fwd, _loss_bwd)


def local_loss(w, p, x, target):
    pend, scale = None, 0.0
    for l in range(len(w)):
        n = f"L{l}"
        x, h = add_norm(n + "_n1", x, pend, scale, p[l]["ffn1_norm"])
        pend, scale = ffn(n + "_f1", h, w[l]["ffn1_w_gate"], w[l]["ffn1_w_up"], w[l]["ffn1_w_down"]), 0.5
        x, h = add_norm(n + "_nm", x, pend, scale, p[l]["mix_norm"])
        pend, scale = mixer(n + "_mx", h, w[l], p[l]), 1.0
        x, h = add_norm(n + "_n2", x, pend, scale, p[l]["ffn2_norm"])
        pend, scale = ffn(n + "_f2", h, w[l]["ffn2_w_gate"], w[l]["ffn2_w_up"], w[l]["ffn2_w_down"]), 0.5
    return loss_op(x, pend, target)


WEIGHT_NAMES = ("ffn1_norm", "ffn1_w_gate", "ffn1_w_up", "ffn1_w_down", "mix_norm", "w_in", "rg_conv_w", "rg_conv_b",
                "rg_w_r", "rg_b_r", "rg_w_i", "rg_b_i", "rg_lambda", "att_q_norm", "att_k_norm", "dn_conv_w", "dn_a_log",
                "dn_dt_bias", "dn_out_norm", "w_branch", "w_out", "ffn2_norm", "ffn2_w_gate", "ffn2_w_up", "ffn2_w_down")
MATRICES = (("ffn1_w_gate", 2), ("ffn1_w_up", 2), ("ffn1_w_down", 1), ("w_in", 2), ("w_branch", 1), ("w_out", 1),
            ("ffn2_w_gate", 2), ("ffn2_w_up", 2), ("ffn2_w_down", 1))
CONVS = (("rg_conv_w", 2), ("dn_conv_w", 2))
SHARD_AXIS = dict(MATRICES + CONVS)
SMALL_NAMES = tuple(n for n in WEIGHT_NAMES if n not in SHARD_AXIS)
ROW_PARAMS = ("ffn1_norm", "mix_norm", "rg_conv_b", "rg_b_r", "rg_b_i", "rg_lambda", "ffn2_norm")
FFN_MATS = ("ffn1_w_gate", "ffn1_w_up", "ffn1_w_down", "ffn2_w_gate", "ffn2_w_up", "ffn2_w_down")
TRANSPOSED_MATS = ("ffn1_w_gate", "ffn1_w_up", "ffn2_w_gate", "ffn2_w_up")
W_IN_SHARD = 2884
GATHER_ORDER = ((("ffn1_w_gate", 0), ("ffn1_w_up", 0), ("ffn1_w_down", 0)),
                (("w_in", 0),),
                None)
GATHER_IDS = (1, 6, 7)
LATE_MATS = ("ffn2_w_gate", "ffn2_w_up", "ffn2_w_down", "w_out", "w_branch")
EXCHANGE_GROUPS = (lambda n, l: l == 1 and n in LATE_MATS,
                   lambda n, l: (l == 1) != (n in LATE_MATS),
                   lambda n, l: l == 0 and n == "w_in",
                   lambda n, l: l == 0 and n not in LATE_MATS and n != "w_in")


def _shard_minor(a, axis):
    a = jnp.moveaxis(a, 0, axis)
    return a.reshape(a.shape[:axis] + (N_CHIPS * a.shape[axis + 1],) + a.shape[axis + 2:])


def _w_in_piece(g, off, n):
    s = W_IN_SHARD
    parts = [g[j][:, max(off, j * s) - j * s:min(off + n, (j + 1) * s) - j * s]
             for j in range(N_CHIPS) if max(off, j * s) < min(off + n, (j + 1) * s)]
    return jnp.concatenate(parts, axis=1) if len(parts) > 1 else parts[0]


def _w_in_chip_grad(gl, j):
    s = W_IN_SHARD
    parts = [gl["in_" + k][:, max(off, j * s) - off:min(off + n, (j + 1) * s) - off]
             for k, off, n in W_IN_PIECES if max(off, j * s) < min(off + n, (j + 1) * s)]
    return jnp.concatenate(parts, axis=1)


def layer_weights(g, conv, l):
    w = {n: g[n, l] for n in FFN_MATS}
    w["w_out"] = g["w_out", l].reshape(D_MODEL, D_MODEL)
    for k, off, n in W_IN_PIECES:
        piece = _w_in_piece(g["w_in", l], off, n)
        w["in_" + k] = jnp.pad(piece, ((0, 0), (0, LANES - n))) if n < LANES else piece
    wb = g["w_branch", l].reshape(-1, D_MODEL)
    w["br_a"], w["br_b"], w["br_c"] = wb[:1024], wb[1024:1792], wb[1792:]
    w["rg_conv_w"], w["dn_conv_w"] = conv["rg_conv_w"][l], conv["dn_conv_w"][l]
    return w


def layer_weight_grads(gw):
    out = {}
    for l, gl in enumerate(gw):
        for n in FFN_MATS:
            out[n, l] = gl[n]
        out["w_out", l] = gl["w_out"].reshape(N_CHIPS, -1, D_MODEL)
        out["w_branch", l] = jnp.concatenate([gl["br_a"], gl["br_b"], gl["br_c"]], axis=0).reshape(N_CHIPS, -1, D_MODEL)
        out["w_in", l] = jnp.stack([_w_in_chip_grad(gl, j) for j in range(N_CHIPS)])
    conv = {n: jnp.stack([gl[n] for gl in gw]) for n, _ in CONVS}
    return out, conv


def layer_small(small, l):
    p = {n: small[n][l] for n in SMALL_NAMES}
    for n in ROW_PARAMS:
        p[n] = small[n][l:l + 1]
    return p


def layer_small_grads(gp, small):
    return {n: jnp.stack([g[n] for g in gp]).reshape(small[n].shape) for n in SMALL_NAMES}


HBM_SPEC = pl.BlockSpec(memory_space=pl.ANY)


def _place():
    x, y, c = lax.axis_index("x"), lax.axis_index("y"), lax.axis_index("c")
    other_chips = [(1 - x, y), (x, 1 - y), (1 - x, 1 - y)]
    return x, y, c, 2 * x + y, (x, y, 1 - c), other_chips


def _half_rows(ref, lead, hc):
    hr = ref.shape[-2] // 2
    return ref.at[(*lead, pl.ds(pl.multiple_of(hc * hr, 16), hr), slice(None))]


def _chip_index():
    return (2 * lax.axis_index("x") + lax.axis_index("y")).astype(jnp.int32).reshape(1)


def cast_into_blocks(name, w):
    l, rows, cols = w.shape
    tr = rows // 2

    def body(me_ref, w_ref, *o_refs):
        for a, o_ref in enumerate(o_refs):
            o_ref[...] = w_ref[a:a + 1].astype(BF16)

    return pl.pallas_call(
        body, name=name, out_shape=[jax.ShapeDtypeStruct((N_CHIPS, rows, cols), BF16)] * l,
        grid_spec=pltpu.PrefetchScalarGridSpec(
            num_scalar_prefetch=1, grid=(rows // tr,),
            in_specs=[pl.BlockSpec((l, tr, cols), lambda i, me: (0, i, 0))],
            out_specs=[pl.BlockSpec((1, tr, cols), lambda i, me: (me[0], i, 0))] * l),
        compiler_params=_params("parallel"),
    )(_chip_index(), w)


def _gather_blocks(bufs_in, bufs_out, send_sems, recv_sems):
    n = len(bufs_in)
    x, y, c, me, sibling, chips = _place()

    def copy(s, src, dst, to):
        return pltpu.make_async_remote_copy(src_ref=src, dst_ref=dst, send_sem=send_sems.at[s], recv_sem=recv_sems.at[s],
                                            device_id=to, device_id_type=MESH)

    first, passed = [], []
    for j, (cx, cy) in enumerate(chips):
        for i in range(n):
            cp = copy(6 * i + j, _half_rows(bufs_in[i], (me,), c), _half_rows(bufs_out[i], (me,), c), (cx, cy, c))
            cp.start()
            first.append(cp)
    for j, (cx, cy) in enumerate(chips):
        k = 2 * cx + cy
        for i in range(n):
            copy(6 * i + j, _half_rows(bufs_in[i], (me,), c), _half_rows(bufs_out[i], (k,), c), (cx, cy, c)).wait_recv()
            cp = copy(6 * i + 3 + j, _half_rows(bufs_out[i], (k,), c), _half_rows(bufs_out[i], (k,), c), sibling)
            cp.start()
            passed.append(cp)
    for j, (cx, cy) in enumerate(chips):
        k = 2 * cx + cy
        for i in range(n):
            copy(6 * i + 3 + j, _half_rows(bufs_in[i], (me,), c), _half_rows(bufs_out[i], (k,), 1 - c), sibling).wait_recv()
    for cp in first + passed:
        cp.wait_send()


def _handshake(peers):
    barrier = pltpu.get_barrier_semaphore()
    for p in peers:
        pl.semaphore_signal(barrier, inc=1, device_id=p, device_id_type=MESH)
    pl.semaphore_wait(barrier, len(peers))


def allgather_blocks_sc(name, bufs, collective_id):
    n = len(bufs)
    refs = [jax.new_ref(b, memory_space=pltpu.MemorySpace.HBM) for b in bufs]

    @pl.kernel(mesh=plsc.ScalarSubcoreMesh(axis_name="sequencer", num_cores=1), name=name,
               scratch_types=(pltpu.SemaphoreType.DMA((6 * n,)), pltpu.SemaphoreType.DMA((6 * n,))),
               compiler_params=pltpu.CompilerParams(collective_id=collective_id))
    def launch(send_sems, recv_sems):
        x, y, c, me, sibling, chips = _place()
        _handshake([(cx, cy, c) for cx, cy in chips] + [sibling])
        _gather_blocks(refs, refs, send_sems, recv_sems)

    launch()
    return [jax.freeze(r) for r in refs]


PEER_FLIPS = tuple((fx, fy, fc) for fx in (0, 1) for fy in (0, 1) for fc in (0, 1))[1:]


def exchange_pieces_sc(name, gs, collective_id):
    n = len(gs)

    def body(*refs):
        ins, outs = refs[:n], refs[n:2 * n]
        send_sems, recv_sems = refs[2 * n:]
        x, y, c, me, sibling, chips = _place()
        my_dev = 4 * x + 2 * y + c
        flip = lambda v, f: 1 - v if f else v
        peers = [(flip(x, fx), flip(y, fy), flip(c, fc)) for fx, fy, fc in PEER_FLIPS]
        _handshake(peers)
        sends = []
        for r, (px, py, pc) in enumerate(peers):
            for i in range(n):
                cp = pltpu.make_async_remote_copy(
                    src_ref=_half_rows(ins[i], (2 * px + py,), pc), dst_ref=outs[i].at[my_dev], send_sem=send_sems.at[7 * i + r],
                    recv_sem=recv_sems.at[7 * i + r], device_id=(px, py, pc), device_id_type=MESH)
                cp.start()
                sends.append(cp)
        for r, (px, py, pc) in enumerate(peers):
            for i in range(n):
                pltpu.make_async_remote_copy(
                    src_ref=_half_rows(ins[i], (me,), c), dst_ref=outs[i].at[4 * px + 2 * py + pc], send_sem=send_sems.at[7 * i + r],
                    recv_sem=recv_sems.at[7 * i + r], device_id=(px, py, pc), device_id_type=MESH).wait_recv()
        for cp in sends:
            cp.wait_send()

    return pl.kernel(
        body, name=name, mesh=plsc.ScalarSubcoreMesh(axis_name="sequencer", num_cores=1),
        out_type=[jax.ShapeDtypeStruct((N_DEV, g.shape[1] // 2, g.shape[2]), g.dtype) for g in gs],
        scratch_types=[pltpu.SemaphoreType.DMA((7 * n,)), pltpu.SemaphoreType.DMA((7 * n,))],
        compiler_params=pltpu.CompilerParams(collective_id=collective_id),
    )(*gs)


def sibling_share_halves(name, fs):
    n = len(fs)
    every = (slice(None),)

    def body(*refs):
        ins, outs = refs[:n], refs[n:2 * n]
        send_sems, recv_sems = refs[2 * n:]
        x, y, c, me, sibling, chips = _place()
        sends = []
        for i in range(n):
            cp = pltpu.make_async_remote_copy(src_ref=_half_rows(ins[i], every, c), dst_ref=_half_rows(outs[i], every, c),
                                              send_sem=send_sems.at[i], recv_sem=recv_sems.at[i], device_id=sibling, device_id_type=MESH)
            cp.start()
            sends.append(cp)
        for i in range(n):
            pltpu.make_async_remote_copy(src_ref=_half_rows(ins[i], every, c), dst_ref=_half_rows(outs[i], every, 1 - c),
                                         send_sem=send_sems.at[i], recv_sem=recv_sems.at[i], device_id=sibling,
                                         device_id_type=MESH).wait_recv()
        for cp in sends:
            cp.wait_send()

    return pl.pallas_call(
        body, name=name, out_shape=[jax.ShapeDtypeStruct(f.shape, f.dtype) for f in fs],
        in_specs=[HBM_SPEC] * n, out_specs=[HBM_SPEC] * n, input_output_aliases={i: i for i in range(n)},
        scratch_shapes=[pltpu.SemaphoreType.DMA((n,)), pltpu.SemaphoreType.DMA((n,))],
    )(*fs)


def allgather_small_sc(name, v, collective_id):
    def body(v_ref, out_ref, send_sems, recv_sems, local_sem):
        x, y, c, me, sibling, chips = _place()
        my_dev = 4 * x + 2 * y + c
        flip = lambda a, f: 1 - a if f else a
        peers = [(flip(x, fx), flip(y, fy), flip(c, fc)) for fx, fy, fc in PEER_FLIPS]
        _handshake(peers)
        mine = pltpu.make_async_copy(v_ref, out_ref.at[my_dev], local_sem)
        mine.start()
        sends = []
        for r, peer in enumerate(peers):
            cp = pltpu.make_async_remote_copy(src_ref=v_ref, dst_ref=out_ref.at[my_dev], send_sem=send_sems.at[r],
                                              recv_sem=recv_sems.at[r], device_id=peer, device_id_type=MESH)
            cp.start()
            sends.append(cp)
        for r, (px, py, pc) in enumerate(peers):
            pltpu.make_async_remote_copy(src_ref=v_ref, dst_ref=out_ref.at[4 * px + 2 * py + pc], send_sem=send_sems.at[r],
                                         recv_sem=recv_sems.at[r], device_id=(px, py, pc), device_id_type=MESH).wait_recv()
        for cp in sends:
            cp.wait_send()
        mine.wait()

    return pl.kernel(
        body, name=name, mesh=plsc.ScalarSubcoreMesh(axis_name="sequencer", num_cores=1),
        out_type=jax.ShapeDtypeStruct((N_DEV,) + v.shape, v.dtype),
        scratch_types=[pltpu.SemaphoreType.DMA((7,)), pltpu.SemaphoreType.DMA((7,)), pltpu.SemaphoreType.DMA],
        compiler_params=pltpu.CompilerParams(collective_id=collective_id),
    )(v)


SUM_BLOCK_ELEMS = 512 * 1024


def sum_slabs(name, b):
    k, h, w = b.shape

    def body(b_ref, o_ref):
        acc = b_ref[0].astype(F32)
        for i in range(1, k):
            acc = acc + b_ref[i].astype(F32)
        o_ref[...] = acc

    return pl.pallas_call(
        body, name=name, out_shape=jax.ShapeDtypeStruct((h, w), F32),
        in_specs=[pl.BlockSpec(memory_space=pltpu.VMEM)], out_specs=pl.BlockSpec(memory_space=pltpu.VMEM),
        compiler_params=pltpu.CompilerParams(vmem_limit_bytes=VMEM_LIMIT),
    )(b)


def sum_pieces(name, pieces, gs):
    nl = len(pieces)
    k, h, w = pieces[0].shape
    tile = max(t for t in range(16, h + 1, 16) if h % t == 0 and (t * w <= SUM_BLOCK_ELEMS or t == 16))
    nt = h // tile
    x, y, c = lax.axis_index("x"), lax.axis_index("y"), lax.axis_index("c")
    place = [v.astype(jnp.int32).reshape(1) for v in (c, 2 * x + y, 4 * x + 2 * y + c)]

    assert nl == 2

    def tile_of(l, a, i):
        return i * a if l else i * (1 - a) + (nt - 1) * a

    def body(c_ref, me_ref, dev_ref, *refs):
        p_refs, g_refs, o_ref = refs[:nl], refs[nl:2 * nl], refs[2 * nl]
        my_dev = dev_ref[0]
        for l in range(nl):
            @pl.when(pl.program_id(0) == l)
            def _():
                o_ref[0] = jnp.zeros(o_ref.shape[1:], F32)
                for d in range(k):
                    @pl.when(my_dev == d)
                    def _():
                        o_ref[0] += g_refs[l][0].astype(F32)

                    @pl.when(my_dev != d)
                    def _():
                        o_ref[0] += p_refs[l][d].astype(F32)

    in_specs = [pl.BlockSpec((k, tile, w), functools.partial(lambda l, a, i, cc, me, dev: (0, tile_of(l, a, i), 0), l))
                for l in range(nl)]
    in_specs += [pl.BlockSpec((1, tile, w), functools.partial(lambda l, a, i, cc, me, dev: (me[0], cc[0] * nt + tile_of(l, a, i), 0), l))
                 for l in range(nl)]
    return pl.pallas_call(
        body, name=name, out_shape=jax.ShapeDtypeStruct((nl, 2 * h, w), F32),
        grid_spec=pltpu.PrefetchScalarGridSpec(
            num_scalar_prefetch=3, grid=(nl, nt), in_specs=in_specs,
            out_specs=pl.BlockSpec((1, tile, w), lambda a, i, cc, me, dev: (a, cc[0] * nt + i, 0))),
        compiler_params=_params("arbitrary", "arbitrary"),
    )(*place, *pieces, *gs)


def _adam_block(w, g, m, v):
    m = ADAM_B1 * m + (1.0 - ADAM_B1) * g
    v = ADAM_B2 * v + (1.0 - ADAM_B2) * (g * g)
    m_hat = m / (1.0 - ADAM_B1 ** ADAM_STEP)
    v_hat = v / (1.0 - ADAM_B2 ** ADAM_STEP)
    return -ADAM_LR * (m_hat / (jnp.sqrt(v_hat) + ADAM_EPS) + ADAM_WD * w), m, v


def adamw(name, w, g, m, v):
    shape = w.shape
    cols = shape[-1]
    rows = w.size // cols
    tile = 128 if rows % 128 == 0 else rows
    flat = [a.reshape(rows, cols) for a in (w, g, m, v)]

    def body(w_ref, g_ref, m_ref, v_ref, d_ref, nm_ref, nv_ref):
        d_ref[...], nm_ref[...], nv_ref[...] = _adam_block(w_ref[...], g_ref[...], m_ref[...], v_ref[...])

    blk = pl.BlockSpec((tile, cols), lambda i: (i, 0))
    res = pl.pallas_call(
        body, name=name, grid=(rows // tile,), in_specs=[blk] * 4, out_specs=[blk] * 3,
        out_shape=[jax.ShapeDtypeStruct((rows, cols), F32)] * 3, compiler_params=_params("parallel"),
    )(*flat)
    return tuple(r.reshape(shape) for r in res)


def _pack_small(values):
    flat = jnp.concatenate([v.reshape(-1) for v in values.values()])
    n = flat.shape[0]
    total = -(-n // (8 * LANES)) * (8 * LANES)
    return jnp.pad(flat, (0, total - n)).reshape(-1, LANES)


def _unpack_small(v, shapes):
    flat = v.reshape(-1)
    out, off = {}, 0
    for n, shape in shapes.items():
        sz = int(np.prod(shape))
        out[n] = flat[off:off + sz].reshape(shape)
        off += sz
    return out


def kernel(x, ffn1_norm, ffn1_w_gate, ffn1_w_up, ffn1_w_down, mix_norm, w_in, rg_conv_w, rg_conv_b, rg_w_r, rg_b_r, rg_w_i, rg_b_i, rg_lambda, att_q_norm, att_k_norm, dn_conv_w, dn_a_log, dn_dt_bias, dn_out_norm, w_branch, w_out, ffn2_norm, ffn2_w_gate, ffn2_w_up, ffn2_w_down, loss_target, m_ffn1_norm, m_ffn1_w_gate, m_ffn1_w_up, m_ffn1_w_down, m_mix_norm, m_w_in, m_rg_conv_w, m_rg_conv_b, m_rg_w_r, m_rg_b_r, m_rg_w_i, m_rg_b_i, m_rg_lambda, m_att_q_norm, m_att_k_norm, m_dn_conv_w, m_dn_a_log, m_dn_dt_bias, m_dn_out_norm, m_w_branch, m_w_out, m_ffn2_norm, m_ffn2_w_gate, m_ffn2_w_up, m_ffn2_w_down, v_ffn1_norm, v_ffn1_w_gate, v_ffn1_w_up, v_ffn1_w_down, v_mix_norm, v_w_in, v_rg_conv_w, v_rg_conv_b, v_rg_w_r, v_rg_b_r, v_rg_w_i, v_rg_b_i, v_rg_lambda, v_att_q_norm, v_att_k_norm, v_dn_conv_w, v_dn_a_log, v_dn_dt_bias, v_dn_out_norm, v_w_branch, v_w_out, v_ffn2_norm, v_ffn2_w_gate, v_ffn2_w_up, v_ffn2_w_down):
    given = dict(locals())
    for n in TRANSPOSED_MATS:
        for pre in ("", "m_", "v_"):
            given[pre + n] = jnp.swapaxes(given[pre + n], 1, 2)
    small = {n: given[n] for n in SMALL_NAMES}
    n_layers = ffn1_norm.shape[0]
    mat_names = [n for n, _ in MATRICES]
    conv_names = [n for n, _ in CONVS]

    blocks = {}
    for n in mat_names:
        for l, b in enumerate(cast_into_blocks("cast_" + n, given[n])):
            blocks[n, l] = b
    gathered, done = {}, []
    for i, wanted in enumerate(GATHER_ORDER):
        keys = [k for k in blocks if k not in gathered and (wanted is None or k in wanted)]
        bufs, _ = lax.optimization_barrier(([blocks[k] for k in keys], done))
        done = allgather_blocks_sc(f"allgather_{i}", bufs, GATHER_IDS[i])
        gathered.update(zip(keys, done))
    taps = jnp.concatenate([given[n].reshape(-1) for n in conv_names]).reshape(-1, LANES)
    taps = allgather_small_sc("allgather_taps", taps, 8).reshape(N_CHIPS, 2, -1)[:, 0]
    conv, off = {}, 0
    for n, ax in CONVS:
        sz = given[n].size
        conv[n] = _shard_minor(taps[:, off:off + sz].reshape((N_CHIPS,) + given[n].shape), ax)
        off += sz
    w = [layer_weights(gathered, conv, l) for l in range(n_layers)]
    p = [layer_small(small, l) for l in range(n_layers)]

    loss, (gw, gp, gx) = jax.value_and_grad(local_loss, argnums=(0, 1, 2))(w, p, x[0], loss_target[0])
    g_mats, g_conv = layer_weight_grads(gw)

    pieces = {}
    for i, group in enumerate(EXCHANGE_GROUPS):
        keys = [k for k in g_mats if group(*k)]
        pieces.update(zip(keys, exchange_pieces_sc(f"exchange_{i}", [g_mats[k] for k in keys], 2 + i)))
    halves = {n: sum_pieces("sum_" + n, [pieces[n, l] for l in range(n_layers)], [g_mats[n, l] for l in range(n_layers)])
              for n in mat_names}
    grads = {}
    for tag, names in (("late", [n for n in mat_names if n in LATE_MATS]), ("early", [n for n in mat_names if n not in LATE_MATS])):
        grads.update(zip(names, sibling_share_halves("share_" + tag, [halves[n] for n in names])))

    g_small = dict(layer_small_grads(gp, small), **g_conv, loss=loss.reshape(1))
    packed_small = _pack_small(g_small)
    slabs = allgather_small_sc("allgather_small", packed_small, 9)
    summed =_unpack_small(sum_slabs("sum_small", slabs), {n: g.shape for n, g in g_small.items()})
    chip = 2 * lax.axis_index("x") + lax.axis_index("y")
    for n in SMALL_NAMES:
        grads[n] = summed[n]
    for n, ax in CONVS:
        s = given[n].shape[ax]
        grads[n] = lax.dynamic_slice_in_dim(summed[n], chip * s, s, axis=ax)

    upd = {n: adamw("adamw_" + n, given[n], grads[n], given["m_" + n], given["v_" + n]) for n in WEIGHT_NAMES}
    out = lambda n, a: jnp.swapaxes(a, 1, 2) if n in TRANSPOSED_MATS else a
    return (summed["loss"][0], gx[None], *[out(n, grads[n]) for n in WEIGHT_NAMES], *[out(n, upd[n][0]) for n in WEIGHT_NAMES],
            *[out(n, upd[n][1]) for n in WEIGHT_NAMES], *[out(n, upd[n][2]) for n in WEIGHT_NAMES])
```

```python
import functools
import math

import jax
import jax.numpy as jnp
import numpy as np
from jax import lax
from jax.experimental import pallas as pl
from jax.experimental.pallas import tpu as pltpu
from jax.experimental.pallas import tpu_sc as plsc

F32 = jnp.float32
BF16 = jnp.bfloat16
MESH = pl.DeviceIdType.MESH

D_MODEL = 1024
FFN_DIM = 2816
RG_C = 8.0
ATT_GROUPS = ((128, 1), (512, 4), (2048, 16))
ATT_HEADS = 12
ATT_HEAD_DIM = 64
ATT_SPAN = 128
DN_HEADS = 8
DN_HEAD_DIM = 128
DN_CHUNK = 64
EPS = 1e-6
NEG_INF = -1e30
N_CHIPS = 4
N_DEV = 8

ADAM_LR, ADAM_B1, ADAM_B2, ADAM_EPS, ADAM_WD, ADAM_STEP = 0.001, 0.9, 0.999, 1e-08, 0.01, 10

LANES = 128
VMEM_LIMIT = 56 * 1024 * 1024


def _params(*sem):
    return pltpu.CompilerParams(dimension_semantics=sem or None, vmem_limit_bytes=VMEM_LIMIT)


def _sigmoid(x):
    return 1.0 / (1.0 + jnp.exp(-x))


def _silu(x):
    return x * _sigmoid(x)


def _softplus(x):
    return jnp.maximum(x, 0.0) + jnp.log(1.0 + jnp.exp(-jnp.abs(x)))


def _gelu(x):
    return 0.5 * x * (1.0 + jnp.tanh(math.sqrt(2.0 / math.pi) * (x + 0.044715 * (x * x * x))))


def _neg_expm1(x):
    series = -x * (1.0 + x * (0.5 + x * (1.0 / 6 + x * (1.0 / 24 + x * (1.0 / 120 + x * (1.0 / 720))))))
    return jnp.where(x > -0.25, series, 1.0 - jnp.exp(x))


def _rms(x, g):
    return x * lax.rsqrt(jnp.mean(x * x, axis=-1, keepdims=True) + EPS) * g


_MM_DIMS = {"nn": (((1,), (0,)), ((), ())), "nt": (((1,), (1,)), ((), ())), "tn": (((0,), (0,)), ((), ()))}


def _split(a):
    hi = a.astype(BF16)
    return hi, (a - hi.astype(F32)).astype(BF16)


def _mxu(a, b, form, passes):
    (ca, cb), _ = _MM_DIMS[form]
    if a.ndim == 3:
        dims = (((ca[0] + 1,), (cb[0] + 1,)), ((0,), (0,)))
    else:
        dims = _MM_DIMS[form]
    dg = lambda p, q: lax.dot_general(p, q, dims, preferred_element_type=F32)
    if passes == 1:
        return dg(a.astype(BF16), b.astype(BF16))
    (a_hi, a_lo), (b_hi, b_lo) = _split(a), _split(b)
    return dg(a_hi, b_hi) + (dg(a_hi, b_lo) + dg(a_lo, b_hi))


@functools.partial(jax.custom_vjp, nondiff_argnums=(2, 3))
def _mm(a, b, form, passes):
    return _mxu(a, b, form, passes)


def _mm_fwd(a, b, form, passes):
    return _mxu(a, b, form, passes), (a, b)


def _mm_bwd(form, passes, res, g):
    a, b = res
    if form == "nn":
        return _mm(g, b, "nt", passes), _mm(a, g, "tn", passes)
    if form == "nt":
        return _mm(g, b, "nn", passes), _mm(g, a, "tn", passes)
    return _mm(b, g, "nt", passes), _mm(a, g, "nn", passes)


_mm.defvjp(_mm_fwd, _mm_bwd)


def _dot(a, b):
    return _mm(a, b, "nn", 1)


def _dot_nt(a, b):
    return _mm(a, b, "nt", 1)


def _dot_tn(a, b):
    return _mm(a, b, "tn", 1)


def _dot3(a, b):
    return _mm(a, b, "nn", 3)


def _rows(shape):
    return lax.broadcasted_iota(jnp.int32, shape, len(shape) - 2)


def _roll_down(x, s, fill):
    return jnp.where(_rows(x.shape) >= s, pltpu.roll(x, s, x.ndim - 2), fill)


def _roll_up(x, s, fill):
    n = x.shape[-2]
    return jnp.where(_rows(x.shape) < n - s, pltpu.roll(x, n - s, x.ndim - 2), fill)


@functools.partial(jax.custom_vjp, nondiff_argnums=(1,))
def _shift(x, s):
    return _roll_down(x, s, 0.0)


def _shift_fwd(x, s):
    return _roll_down(x, s, 0.0), None


def _shift_bwd(s, _, g):
    return (_roll_up(g, s, 0.0),)


_shift.defvjp(_shift_fwd, _shift_bwd)


def _causal_conv(x, w):
    return w[0:1] * _shift(x, 3) + w[1:2] * _shift(x, 2) + w[2:3] * _shift(x, 1) + w[3:4] * x


@jax.custom_vjp
def _lin_scan(a, b):
    return _lin_scan_fwd(a, b)[0]


def _lin_scan_fwd(a, b):
    a0 = a
    s = 1
    while s < a.shape[0]:
        b = a * _roll_down(b, s, 0.0) + b
        a = a * _roll_down(a, s, 1.0)
        s *= 2
    return b, (a0, b)


def _lin_scan_bwd(res, g):
    a, h = res
    c = _roll_up(a, 1, 0.0)
    s = 1
    while s < a.shape[0]:
        g = c * _roll_up(g, s, 0.0) + g
        c = c * _roll_up(c, s, 1.0)
        s *= 2
    return g * _roll_down(h, 1, 0.0), g


_lin_scan.defvjp(_lin_scan_fwd, _lin_scan_bwd)


@jax.custom_vjp
def _cumsum_rows(x):
    s = 1
    while s < x.shape[-2]:
        x = x + _roll_down(x, s, 0.0)
        s *= 2
    return x


def _cumsum_rows_fwd(x):
    return _cumsum_rows(x), None


def _cumsum_rows_bwd(_, g):
    s = 1
    while s < g.shape[-2]:
        g = g + _roll_up(g, s, 0.0)
        s *= 2
    return (g,)


_cumsum_rows.defvjp(_cumsum_rows_fwd, _cumsum_rows_bwd)


ROW_BLOCK_BYTES = 14 * 1024 * 1024


def _row_tile(t, width=0):
    for tile in (512, 256):
        if t % tile == 0 and (tile == 256 or tile * width * 4 <= ROW_BLOCK_BYTES):
            return tile
    return t


def _rowwise_fwd_call(name, f, rows, pars, tile):
    t = rows[0].shape[0]
    outs = jax.eval_shape(f, *[jax.ShapeDtypeStruct((tile, r.shape[1]), F32) for r in rows],
                          *[jax.ShapeDtypeStruct(p.shape, F32) for p in pars])
    nr, npar = len(rows), len(pars)

    def body(*refs):
        ins = [r[...] for r in refs[:nr + npar]]
        res = f(*ins)
        for o_ref, o in zip(refs[nr + npar:], res):
            o_ref[...] = o.astype(o_ref.dtype)

    return pl.pallas_call(
        body, name=name, grid=(t // tile,),
        in_specs=[pl.BlockSpec((tile, r.shape[1]), lambda i: (i, 0)) for r in rows]
        + [pl.BlockSpec(p.shape, lambda i: (0, 0)) for p in pars],
        out_specs=[pl.BlockSpec((tile, o.shape[1]), lambda i: (i, 0)) for o in outs],
        out_shape=[jax.ShapeDtypeStruct((t, o.shape[1]), F32) for o in outs],
        compiler_params=_params("parallel"),
    )(*rows, *pars)


def _rowwise_bwd_call(name, f, rows, pars, cts, tile):
    t = rows[0].shape[0]
    nr, npar, nct = len(rows), len(pars), len(cts)

    def body(*refs):
        ins = [r[...] for r in refs[:nr + npar]]
        gs = tuple(r[...] for r in refs[nr + npar:nr + npar + nct])
        outs = refs[nr + npar + nct:]
        _, vjp = jax.vjp(f, *ins)
        d = vjp(gs)
        for o_ref, v in zip(outs[:nr], d[:nr]):
            o_ref[...] = v

        @pl.when(pl.program_id(0) == 0)
        def _():
            for o_ref in outs[nr:]:
                o_ref[...] = jnp.zeros_like(o_ref)

        for o_ref, v in zip(outs[nr:], d[nr:]):
            o_ref[...] += v

    res = pl.pallas_call(
        body, name=name, grid=(t // tile,),
        in_specs=[pl.BlockSpec((tile, r.shape[1]), lambda i: (i, 0)) for r in rows]
        + [pl.BlockSpec(p.shape, lambda i: (0, 0)) for p in pars]
        + [pl.BlockSpec((tile, c.shape[1]), lambda i: (i, 0)) for c in cts],
        out_specs=[pl.BlockSpec((tile, r.shape[1]), lambda i: (i, 0)) for r in rows]
        + [pl.BlockSpec(p.shape, lambda i: (0, 0)) for p in pars],
        out_shape=[jax.ShapeDtypeStruct(r.shape, F32) for r in rows]
        + [jax.ShapeDtypeStruct(p.shape, F32) for p in pars],
        compiler_params=_params("arbitrary"),
    )(*rows, *pars, *cts)
    return tuple(res[:nr]), tuple(res[nr:])


def rowwise(name, f, rows, pars=()):
    outs = jax.eval_shape(f, *[jax.ShapeDtypeStruct((8, r.shape[1]), F32) for r in rows],
                          *[jax.ShapeDtypeStruct(p.shape, F32) for p in pars])
    tile = _row_tile(rows[0].shape[0], 2 * sum(r.shape[1] for r in rows) + sum(o.shape[1] for o in outs))

    @jax.custom_vjp
    def op(rows, pars):
        return tuple(_rowwise_fwd_call(name, f, rows, pars, tile))

    def op_fwd(rows, pars):
        return op(rows, pars), (rows, pars)

    def op_bwd(res, cts):
        return _rowwise_bwd_call(name + "_bwd", f, res[0], res[1], tuple(cts), tile)

    op.defvjp(op_fwd, op_bwd)
    return op(tuple(rows), tuple(pars))


MM_TM = 512


def _tile_of(n, cap):
    best = None
    for c in range(LANES, min(n, cap) + 1, LANES):
        if n % c == 0:
            best = c
    return best or n


def _mmc_fwd(name, h, w):
    m, k = h.shape
    j, _, n = w.shape
    tm, tn = MM_TM, _tile_of(n, 1408)

    def body(h_ref, w_ref, o_ref):
        o_ref[0] = _dot(h_ref[...], w_ref[0])

    return pl.pallas_call(
        body, name=name, grid=(m // tm, j, n // tn),
        in_specs=[pl.BlockSpec((tm, k), lambda i, b, c: (i, 0)), pl.BlockSpec((1, k, tn), lambda i, b, c: (b, 0, c))],
        out_specs=pl.BlockSpec((1, tm, tn), lambda i, b, c: (b, i, c)),
        out_shape=jax.ShapeDtypeStruct((j, m, n), F32),
        compiler_params=_params("parallel", "parallel", "parallel"),
    )(h, w)


def _mmc_dw(name, h, dy):
    m, k = h.shape
    j, _, n = dy.shape
    tk, tn = _tile_of(k, 512), _tile_of(n, 1152)

    def body(h_ref, dy_ref, o_ref):
        o_ref[0] = _dot_tn(h_ref[...], dy_ref[0]).astype(BF16)

    return pl.pallas_call(
        body, name=name, grid=(j, k // tk, n // tn),
        in_specs=[pl.BlockSpec((m, tk), lambda b, i, c: (0, i)), pl.BlockSpec((1, m, tn), lambda b, i, c: (b, 0, c))],
        out_specs=pl.BlockSpec((1, tk, tn), lambda b, i, c: (b, i, c)),
        out_shape=jax.ShapeDtypeStruct((j, k, n), BF16),
        compiler_params=_params("parallel", "parallel", "parallel"),
    )(h, dy)


PROJ_GROUP_COLS = 4608


def _proj_dh(name, dys, ws, acc):
    m, k = dys[0].shape[0], ws[0].shape[0]
    n, tm = len(dys), 256

    def body(*refs):
        dy_refs, w_refs, rest = refs[:n], refs[n:2 * n], refs[2 * n:]
        total = _dot_nt(dy_refs[0][...], w_refs[0][...])
        for dy_ref, w_ref in zip(dy_refs[1:], w_refs[1:]):
            total = total + _dot_nt(dy_ref[...], w_ref[...])
        if acc is not None:
            total = total + rest[0][...]
        rest[-1][...] = total

    row = lambda width: pl.BlockSpec((tm, width), lambda i: (i, 0))
    return pl.pallas_call(
        body, name=name, grid=(m // tm,),
        in_specs=[row(d.shape[1]) for d in dys] + [pl.BlockSpec(w.shape, lambda i: (0, 0)) for w in ws] + ([row(k)] if acc is not None else []),
        out_specs=row(k), out_shape=jax.ShapeDtypeStruct((m, k), F32), compiler_params=_params("parallel"),
    )(*dys, *ws, *([acc] if acc is not None else []))


def project_in(name, h, ws):
    keys = list(ws)

    @jax.custom_vjp
    def op(h, ws):
        return {p: _mmc_fwd(f"{name}_{p}", h, ws[p][None])[0] for p in keys}

    def op_fwd(h, ws):
        return op(h, ws), (h, ws)

    def op_bwd(res, dys):
        h, ws = res
        groups, cols = [[]], 0
        for p in keys:
            if groups[-1] and cols + ws[p].shape[1] > PROJ_GROUP_COLS:
                groups.append([])
                cols = 0
            groups[-1].append(p)
            cols += ws[p].shape[1]
        dh = None
        for i, group in enumerate(groups):
            dh = _proj_dh(f"{name}_dh{i}", [dys[p] for p in group], [ws[p] for p in group], dh)
        return dh, {p: _mmc_dw(f"{name}_{p}_dw", h, dys[p][None])[0] for p in keys}

    op.defvjp(op_fwd, op_bwd)
    return op(h, ws)


def _ffn_up(name, h, wt):
    m, k = h.shape
    j, n, _ = wt.shape
    tm = MM_TM

    def body(h_ref, w_ref, o_ref):
        o_ref[0] = _dot_nt(h_ref[...], w_ref[0])

    return pl.pallas_call(
        body, name=name, grid=(m // tm, j),
        in_specs=[pl.BlockSpec((tm, k), lambda i, b: (i, 0)), pl.BlockSpec((1, n, k), lambda i, b: (b, 0, 0))],
        out_specs=pl.BlockSpec((1, tm, n), lambda i, b: (b, i, 0)),
        out_shape=jax.ShapeDtypeStruct((j, m, n), F32), compiler_params=_params("parallel", "parallel"),
    )(h, wt)


def _ffn_down(name, g, u, wd):
    j, m, n = g.shape
    d = wd.shape[2]
    tm = MM_TM

    def body(g_ref, u_ref, w_ref, o_ref):
        part = _dot(_silu(g_ref[0]) * u_ref[0], w_ref[0])

        @pl.when(pl.program_id(1) == 0)
        def _():
            o_ref[...] = part

        @pl.when(pl.program_id(1) > 0)
        def _():
            o_ref[...] += part

    act = pl.BlockSpec((1, tm, n), lambda i, b: (b, i, 0))
    return pl.pallas_call(
        body, name=name, grid=(m // tm, j),
        in_specs=[act, act, pl.BlockSpec((1, n, d), lambda i, b: (b, 0, 0))],
        out_specs=pl.BlockSpec((tm, d), lambda i, b: (i, 0)),
        out_shape=jax.ShapeDtypeStruct((m, d), F32), compiler_params=_params("parallel", "arbitrary"),
    )(g, u, wd)


def _ffn_down_bwd(name, dy, g, u, wd):
    j, m, n = g.shape
    d = wd.shape[2]
    tm = MM_TM

    def body(dy_ref, g_ref, u_ref, w_ref, dg_ref, du_ref):
        da = _dot_nt(dy_ref[...], w_ref[0])
        gv = g_ref[0]
        s = _sigmoid(gv)
        dg_ref[0] = da * u_ref[0] * (s * (1.0 + gv * (1.0 - s)))
        du_ref[0] = da * (gv * s)

    act = pl.BlockSpec((1, tm, n), lambda i, b: (b, i, 0))
    return pl.pallas_call(
        body, name=name, grid=(m // tm, j),
        in_specs=[pl.BlockSpec((tm, d), lambda i, b: (i, 0)), act, act, pl.BlockSpec((1, n, d), lambda i, b: (b, 0, 0))],
        out_specs=[act, act], out_shape=[jax.ShapeDtypeStruct((j, m, n), F32)] * 2,
        compiler_params=_params("parallel", "parallel"),
    )(dy, g, u, wd)


def _ffn_down_dw(name, g, u, dy):
    j, m, n = g.shape
    d = dy.shape[1]
    tn = _tile_of(d, 512)

    def body(g_ref, u_ref, dy_ref, o_ref):
        o_ref[0] = _dot_tn(_silu(g_ref[0]) * u_ref[0], dy_ref[...]).astype(BF16)

    act = pl.BlockSpec((1, m, n), lambda b, c: (b, 0, 0))
    return pl.pallas_call(
        body, name=name, grid=(j, d // tn),
        in_specs=[act, act, pl.BlockSpec((m, tn), lambda b, c: (0, c))],
        out_specs=pl.BlockSpec((1, n, tn), lambda b, c: (b, 0, c)),
        out_shape=jax.ShapeDtypeStruct((j, n, d), BF16), compiler_params=_params("parallel", "parallel"),
    )(g, u, dy)


def _ffn_up_dh(name, dg, du, wg, wu):
    j, m, n = dg.shape
    k = wg.shape[2]
    tm = MM_TM

    def body(dg_ref, du_ref, wg_ref, wu_ref, o_ref):
        part = _dot(dg_ref[0], wg_ref[0]) + _dot(du_ref[0], wu_ref[0])

        @pl.when(pl.program_id(1) == 0)
        def _():
            o_ref[...] = part

        @pl.when(pl.program_id(1) > 0)
        def _():
            o_ref[...] += part

    act = pl.BlockSpec((1, tm, n), lambda i, b: (b, i, 0))
    wsp = pl.BlockSpec((1, n, k), lambda i, b: (b, 0, 0))
    return pl.pallas_call(
        body, name=name, grid=(m // tm, j), in_specs=[act, act, wsp, wsp],
        out_specs=pl.BlockSpec((tm, k), lambda i, b: (i, 0)),
        out_shape=jax.ShapeDtypeStruct((m, k), F32), compiler_params=_params("parallel", "arbitrary"),
    )(dg, du, wg, wu)


def _ffn_up_dw(name, dy, h):
    j, m, n = dy.shape
    k = h.shape[1]
    tk = _tile_of(k, 512)

    def body(dy_ref, h_ref, o_ref):
        o_ref[0] = _dot_tn(dy_ref[0], h_ref[...]).astype(BF16)

    return pl.pallas_call(
        body, name=name, grid=(j, k // tk),
        in_specs=[pl.BlockSpec((1, m, n), lambda b, i: (b, 0, 0)), pl.BlockSpec((m, tk), lambda b, i: (0, i))],
        out_specs=pl.BlockSpec((1, n, tk), lambda b, i: (b, 0, i)),
        out_shape=jax.ShapeDtypeStruct((j, n, k), BF16), compiler_params=_params("parallel", "parallel"),
    )(dy, h)


def ffn(name, h, wg, wu, wd):
    @jax.custom_vjp
    def op(h, wg, wu, wd):
        return _ffn_down(name + "_d", _ffn_up(name + "_g", h, wg), _ffn_up(name + "_u", h, wu), wd)

    def op_fwd(h, wg, wu, wd):
        g, u = _ffn_up(name + "_g", h, wg), _ffn_up(name + "_u", h, wu)
        return _ffn_down(name + "_d", g, u, wd), (h, g, u, wg, wu, wd)

    def op_bwd(res, dy):
        h, g, u, wg, wu, wd = res
        dg, du = _ffn_down_bwd(name + "_d_bwd", dy, g, u, wd)
        return (_ffn_up_dh(name + "_dh", dg, du, wg, wu), _ffn_up_dw(name + "_g_dw", dg, h), _ffn_up_dw(name + "_u_dw", du, h),
                _ffn_down_dw(name + "_d_dw", g, u, dy))

    op.defvjp(op_fwd, op_bwd)
    return op(h, wg, wu, wd)


def _mmr_fwd(name, a, w):
    j, m, n = a.shape
    nn = w.shape[2]
    tm, tn = MM_TM, _tile_of(nn, 1024)

    def body(a_ref, w_ref, o_ref):
        part = _dot(a_ref[0], w_ref[0])

        @pl.when(pl.program_id(2) == 0)
        def _():
            o_ref[...] = part

        @pl.when(pl.program_id(2) > 0)
        def _():
            o_ref[...] += part

    return pl.pallas_call(
        body, name=name, grid=(m // tm, nn // tn, j),
        in_specs=[pl.BlockSpec((1, tm, n), lambda i, c, b: (b, i, 0)), pl.BlockSpec((1, n, tn), lambda i, c, b: (b, 0, c))],
        out_specs=pl.BlockSpec((tm, tn), lambda i, c, b: (i, c)),
        out_shape=jax.ShapeDtypeStruct((m, nn), F32),
        compiler_params=_params("parallel", "parallel", "arbitrary"),
    )(a, w)


def _mmr_da(name, dy, w):
    m, nn = dy.shape
    j, n, _ = w.shape
    tm = MM_TM

    def body(dy_ref, w_ref, o_ref):
        o_ref[0] = _dot_nt(dy_ref[...], w_ref[0])

    return pl.pallas_call(
        body, name=name, grid=(m // tm, j),
        in_specs=[pl.BlockSpec((tm, nn), lambda i, b: (i, 0)), pl.BlockSpec((1, n, nn), lambda i, b: (b, 0, 0))],
        out_specs=pl.BlockSpec((1, tm, n), lambda i, b: (b, i, 0)),
        out_shape=jax.ShapeDtypeStruct((j, m, n), F32),
        compiler_params=_params("parallel", "parallel"),
    )(dy, w)


def _mmr_dw(name, a, dy):
    j, m, n = a.shape
    nn = dy.shape[1]
    tn = _tile_of(nn, 512)

    def body(a_ref, dy_ref, o_ref):
        o_ref[0] = _dot_tn(a_ref[0], dy_ref[...]).astype(BF16)

    return pl.pallas_call(
        body, name=name, grid=(j, nn // tn),
        in_specs=[pl.BlockSpec((1, m, n), lambda b, c: (b, 0, 0)), pl.BlockSpec((m, tn), lambda b, c: (0, c))],
        out_specs=pl.BlockSpec((1, n, tn), lambda b, c: (b, 0, c)),
        out_shape=jax.ShapeDtypeStruct((j, n, nn), BF16),
        compiler_params=_params("parallel", "parallel"),
    )(a, dy)


def mm_rows(name, a, w):
    @jax.custom_vjp
    def op(a, w):
        return _mmr_fwd(name, a, w)

    def op_fwd(a, w):
        return op(a, w), (a, w)

    def op_bwd(res, dy):
        a, w = res
        return _mmr_da(name + "_da", dy, w), _mmr_dw(name + "_dw", a, dy)

    op.defvjp(op_fwd, op_bwd)
    return op(a, w)


def _colwise_specs(cols, pars, par_block):
    t = cols[0].shape[0]
    specs = [pl.BlockSpec((t, LANES), lambda j: (0, j)) for _ in cols]
    for p, blk in zip(pars, par_block):
        if blk == "lane":
            specs.append(pl.BlockSpec((p.shape[0], LANES), lambda j: (0, j)))
        else:
            specs.append(pl.BlockSpec((1,) + p.shape[1:], lambda j: (j, 0, 0)))
    return specs


def _colwise_fwd_call(name, f, cols, pars, par_block, n_out):
    t, c = cols[0].shape
    nc, npar = len(cols), len(pars)

    def body(*refs):
        ins = [r[...] for r in refs[:nc]] + [r[...] if b == "lane" else r[0] for r, b in zip(refs[nc:nc + npar], par_block)]
        res = f(*ins)
        for o_ref, o in zip(refs[nc + npar:], res):
            o_ref[...] = o

    return pl.pallas_call(
        body, name=name, grid=(c // LANES,),
        in_specs=_colwise_specs(cols, pars, par_block),
        out_specs=[pl.BlockSpec((t, LANES), lambda j: (0, j)) for _ in range(n_out)],
        out_shape=[jax.ShapeDtypeStruct((t, c), F32) for _ in range(n_out)],
        compiler_params=_params("parallel"),
    )(*cols, *pars)


def _colwise_bwd_call(name, f, cols, pars, par_block, cts):
    t, c = cols[0].shape
    nc, npar, nct = len(cols), len(pars), len(cts)

    def body(*refs):
        ins = [r[...] for r in refs[:nc]] + [r[...] if b == "lane" else r[0] for r, b in zip(refs[nc:nc + npar], par_block)]
        gs = tuple(r[...] for r in refs[nc + npar:nc + npar + nct])
        outs = refs[nc + npar + nct:]
        _, vjp = jax.vjp(f, *ins)
        d = vjp(gs)
        for o_ref, v in zip(outs[:nc], d[:nc]):
            o_ref[...] = v
        for o_ref, v, b in zip(outs[nc:], d[nc:], par_block):
            if b == "lane":
                o_ref[...] = v
            else:
                o_ref[0] = v

    res = pl.pallas_call(
        body, name=name, grid=(c // LANES,),
        in_specs=_colwise_specs(cols, pars, par_block) + [pl.BlockSpec((t, LANES), lambda j: (0, j)) for _ in cts],
        out_specs=_colwise_specs(cols, pars, par_block),
        out_shape=[jax.ShapeDtypeStruct(v.shape, F32) for v in (*cols, *pars)],
        compiler_params=_params("parallel"),
    )(*cols, *pars, *cts)
    return tuple(res[:nc]), tuple(res[nc:])


def colwise(name, f, cols, pars, par_block, n_out):
    @jax.custom_vjp
    def op(cols, pars):
        return tuple(_colwise_fwd_call(name, f, cols, pars, par_block, n_out))

    def op_fwd(cols, pars):
        return op(cols, pars), (cols, pars)

    def op_bwd(res, cts):
        return _colwise_bwd_call(name + "_bwd", f, res[0], res[1], par_block, tuple(cts))

    op.defvjp(op_fwd, op_bwd)
    return op(tuple(cols), tuple(pars))


def _rg_block(x, gate, cw, cb, wr, br, wi, bi, lam):
    xa = _causal_conv(x, cw) + cb
    r = _sigmoid(_dot(xa, wr) + br)
    i = _sigmoid(_dot(xa, wi) + bi)
    log_a = -RG_C * r * _softplus(-lam)
    a = jnp.exp(log_a)
    b = jnp.sqrt(_neg_expm1(2.0 * log_a)) * (i * xa)
    return (_lin_scan(a, b) * _gelu(gate),)


def _dn_conv_block(mode):
    def f(x, cw):
        c = _silu(_causal_conv(x, cw))
        if mode == "v":
            return (c,)
        c = c * lax.rsqrt(jnp.sum(c * c, axis=-1, keepdims=True) + EPS)
        return (c * (DN_HEAD_DIM ** -0.5),) if mode == "q" else (c,)
    return f


def _block_diag(w):
    w = w.reshape(8, 2, 64, 64)
    z = jnp.zeros((8, 64, 64), w.dtype)
    top = jnp.concatenate([w[:, 0], z], axis=2)
    bot = jnp.concatenate([z, w[:, 1]], axis=2)
    return jnp.concatenate([top, bot], axis=1)


DN_HP = 8


def _dn_block(S, qw, kw, vw, gb, h0):
    hp, hd = S.shape[0], DN_HEAD_DIM
    heads = lambda a: jnp.concatenate([a[None, :, j * hd:(j + 1) * hd] for j in range(hp)], axis=0)
    lane = lax.broadcasted_iota(jnp.int32, gb.shape, 1)
    col = lambda i: jnp.sum(jnp.where(lane == i, gb, 0.0), axis=1, keepdims=True)[None]
    beta = jnp.concatenate([col(h0 + j) for j in range(hp)], axis=0)
    g = jnp.concatenate([col(h0 + j + DN_HEADS) for j in range(hp)], axis=0)
    s_new, o = _dn_step(S, heads(qw), heads(kw), heads(vw), beta, g)
    return s_new, jnp.concatenate([o[j:j + 1].reshape(o.shape[1:]) for j in range(hp)], axis=1)


def _dn_step(S, q, k, v, beta, g):
    c = DN_CHUNK
    ri = lax.broadcasted_iota(jnp.int32, (c, c), 0)
    ci = lax.broadcasted_iota(jnp.int32, (c, c), 1)
    incl, strict = ri >= ci, ri > ci
    eye = (ri == ci).astype(F32)
    gam = _cumsum_rows(g)
    gam_row = jnp.sum(jnp.where(ri <= ci, g, 0.0), axis=-2, keepdims=True)
    gam_last = jnp.sum(g, axis=-2, keepdims=True)
    decay = jnp.where(incl, jnp.exp(jnp.where(incl, gam - gam_row, 0.0)), 0.0)
    kb = k * beta
    vb = v * beta
    a = jnp.where(strict, _dot_nt(kb, k) * decay, 0.0)
    p = -a
    tinv = eye + p
    for _ in range(5):
        p = _dot3(p, p)
        tinv = tinv + _dot3(tinv, p)
    e_gam = jnp.exp(gam)
    u0 = _dot3(tinv, vb)
    wk = _dot3(tinv, kb * e_gam)
    qk = jnp.where(incl, _dot_nt(q, k) * decay, 0.0)
    q_dec = q * e_gam
    k_dec = k * jnp.exp(gam_last - gam)
    u = u0 - _dot(wk, S)
    o = _dot(q_dec, S) + _dot(qk, u)
    s_new = S * jnp.exp(gam_last) + _dot_tn(k_dec, u)
    return s_new, o


def _dn_fwd_call(q, k, v, gb):
    t, w = q.shape
    n, hp, hd, c = t // DN_CHUNK, DN_HP, DN_HEAD_DIM, DN_CHUNK

    def body(q_ref, k_ref, v_ref, gb_ref, o_ref, s0_ref, s_scr):
        @pl.when(pl.program_id(1) == 0)
        def _():
            s_scr[...] = jnp.zeros_like(s_scr)

        s_old = s_scr[...]
        s0_ref[:, 0] = s_old
        s_new, o = _dn_block(s_old, q_ref[...], k_ref[...], v_ref[...], gb_ref[...], pl.program_id(0) * hp)
        o_ref[...] = o
        s_scr[...] = s_new

    blk = pl.BlockSpec((c, hp * hd), lambda g, i: (i, g))
    return pl.pallas_call(
        body, name="dn_core", grid=(DN_HEADS // hp, n),
        in_specs=[blk, blk, blk, pl.BlockSpec((c, LANES), lambda g, i: (i, 0))],
        out_specs=[blk, pl.BlockSpec((hp, 1, hd, hd), lambda g, i: (g, i, 0, 0))],
        out_shape=[jax.ShapeDtypeStruct((t, w), F32), jax.ShapeDtypeStruct((DN_HEADS, n, hd, hd), F32)],
        scratch_shapes=[pltpu.VMEM((hp, hd, hd), F32)],
        compiler_params=_params("parallel", "arbitrary"),
    )(q, k, v, gb)


def _dn_bwd_call(q, k, v, gb, s0, do):
    t, w = q.shape
    n, hp, hd, c = t // DN_CHUNK, DN_HP, DN_HEAD_DIM, DN_CHUNK
    ng = DN_HEADS // hp

    def body(q_ref, k_ref, v_ref, gb_ref, s0_ref, do_ref, dq_ref, dk_ref, dv_ref, dgb_ref, ds_scr):
        @pl.when(pl.program_id(1) == 0)
        def _():
            ds_scr[...] = jnp.zeros_like(ds_scr)

        h0 = pl.program_id(0) * hp
        _, vjp = jax.vjp(lambda *a: _dn_block(*a, h0), s0_ref[:, 0], q_ref[...], k_ref[...], v_ref[...], gb_ref[...])
        ds, dq, dk, dv, dgb = vjp((ds_scr[...], do_ref[...]))
        ds_scr[...] = ds
        dq_ref[...], dk_ref[...], dv_ref[...] = dq, dk, dv
        dgb_ref[0] = dgb

    blk = pl.BlockSpec((c, hp * hd), lambda g, i: (n - 1 - i, g))
    res = pl.pallas_call(
        body, name="dn_core_bwd", grid=(ng, n),
        in_specs=[blk, blk, blk, pl.BlockSpec((c, LANES), lambda g, i: (n - 1 - i, 0)),
                  pl.BlockSpec((hp, 1, hd, hd), lambda g, i: (g, n - 1 - i, 0, 0)), blk],
        out_specs=[blk, blk, blk, pl.BlockSpec((1, c, LANES), lambda g, i: (g, n - 1 - i, 0))],
        out_shape=[jax.ShapeDtypeStruct((t, w), F32)] * 3 + [jax.ShapeDtypeStruct((ng, t, LANES), F32)],
        scratch_shapes=[pltpu.VMEM((hp, hd, hd), F32)],
        compiler_params=_params("parallel", "arbitrary"),
    )(q, k, v, gb, s0, do)
    return res[0], res[1], res[2], jnp.sum(res[3], axis=0)


@jax.custom_vjp
def dn_core(q, k, v, gb):
    return _dn_fwd_call(q, k, v, gb)[0]


def _dn_core_fwd(q, k, v, gb):
    o, s0 = _dn_fwd_call(q, k, v, gb)
    return o, (q, k, v, gb, s0)


def _dn_core_bwd(res, do):
    return _dn_bwd_call(*res, do)


dn_core.defvjp(_dn_core_fwd, _dn_core_bwd)


ATT_GH = 4


def _att_block(q, kp, kc, vp, vc, qn, kn, slope, has_prev, dil):
    s = ATT_SPAN
    qh = _rms(q, qn) * (ATT_HEAD_DIM ** -0.5)
    qi = lax.broadcasted_iota(jnp.int32, (s, s), 0)
    kj = lax.broadcasted_iota(jnp.int32, (s, s), 1)
    d_p = qi + s - kj
    d_c = qi - kj
    s_p = _dot_nt(qh, _rms(kp, kn)) - slope * (d_p * dil).astype(F32)
    s_c = _dot_nt(qh, _rms(kc, kn)) - slope * (d_c * dil).astype(F32)
    s_p = jnp.where((d_p <= s) & (has_prev > 0), s_p, NEG_INF)
    s_c = jnp.where(d_c >= 0, s_c, NEG_INF)
    m = lax.stop_gradient(jnp.maximum(jnp.max(s_p, axis=-1, keepdims=True), jnp.max(s_c, axis=-1, keepdims=True)))
    p_p = jnp.exp(s_p - m)
    p_c = jnp.exp(s_c - m)
    den = jnp.sum(p_p, axis=-1, keepdims=True) + jnp.sum(p_c, axis=-1, keepdims=True)
    o = _dot(p_p / den, vp) + _dot(p_c / den, vc)
    lse = m + jnp.log(den)
    return o, jnp.broadcast_to(lse, o.shape)


def _att_heads(a):
    e = ATT_HEAD_DIM
    return jnp.concatenate([a[None, :, h * e:(h + 1) * e] for h in range(ATT_GH)], axis=0)


def _att_lanes(a):
    return jnp.concatenate([a[h:h + 1].reshape(a.shape[1:]) for h in range(ATT_GH)], axis=1)


def _att_rows(q, kp, kc, vp, vc, qn, kn, group, has_prev, dil):
    head = lax.broadcasted_iota(jnp.int32, (ATT_GH, 1, 1), 0) + (ATT_GH * group + 1)
    slope = jnp.exp(head.astype(F32) * (-8.0 / ATT_HEADS * math.log(2.0)))
    o, lse = _att_block(_att_heads(q), _att_heads(kp), _att_heads(kc), _att_heads(vp), _att_heads(vc), qn, kn, slope, has_prev, dil)
    return _att_lanes(o), _att_lanes(lse)


def _att_specs(group, dil):
    blk = (ATT_SPAN, ATT_GH * ATT_HEAD_DIM)
    cur = lambda which: pl.BlockSpec(blk, lambda r, n: (n, r * 9 + 3 * which + group))
    prev = lambda which: pl.BlockSpec(blk, lambda r, n: (jnp.maximum(n - 1, 0), r * 9 + 3 * which + group))
    out = pl.BlockSpec(blk, lambda r, n: (n, r))
    gain = pl.BlockSpec((ATT_GH, 1, ATT_HEAD_DIM), lambda r, n: (0, 0, 0))
    return [cur(0), prev(1), cur(1), prev(2), cur(2), gain, gain], out, gain


def _att_fwd_call(name, group, dil, pa, qn, kn):
    t = pa.shape[0]
    l = t // dil
    w = ATT_GH * ATT_HEAD_DIM
    ins, out, _ = _att_specs(group, dil)
    pav = pa.reshape(l, dil * pa.shape[1])

    def body(q_ref, kp_ref, kc_ref, vp_ref, vc_ref, qn_ref, kn_ref, o_ref, lse_ref):
        o_ref[...], lse_ref[...] = _att_rows(q_ref[...], kp_ref[...], kc_ref[...], vp_ref[...], vc_ref[...], qn_ref[...],
                                             kn_ref[...], group, pl.program_id(1), dil)

    o, lse = pl.pallas_call(
        body, name=name, grid=(dil, l // ATT_SPAN), in_specs=ins, out_specs=[out, out],
        out_shape=[jax.ShapeDtypeStruct((l, dil * w), F32)] * 2, compiler_params=_params("parallel", "arbitrary"),
    )(pav, pav, pav, pav, pav, qn, kn)
    return o.reshape(t, w), lse.reshape(t, w)


def _att_bwd_call(name, group, dil, pa, qn, kn, do, dlse):
    t = pa.shape[0]
    l = t // dil
    w = ATT_GH * ATT_HEAD_DIM
    ins, out, gain = _att_specs(group, dil)
    pav = pa.reshape(l, dil * pa.shape[1])

    def body(q_ref, kp_ref, kc_ref, vp_ref, vc_ref, qn_ref, kn_ref, do_ref, dlse_ref,
             dq_ref, dkp_ref, dkc_ref, dvp_ref, dvc_ref, dqn_ref, dkn_ref):
        has_prev = pl.program_id(1)
        _, vjp = jax.vjp(lambda *a: _att_rows(*a, group, has_prev, dil), q_ref[...], kp_ref[...], kc_ref[...], vp_ref[...],
                         vc_ref[...], qn_ref[...], kn_ref[...])
        dq, dkp, dkc, dvp, dvc, dqn, dkn = vjp((do_ref[...], dlse_ref[...]))
        dq_ref[...], dkp_ref[...], dkc_ref[...], dvp_ref[...], dvc_ref[...] = dq, dkp, dkc, dvp, dvc

        @pl.when((pl.program_id(0) == 0) & (pl.program_id(1) == 0))
        def _():
            dqn_ref[...] = jnp.zeros_like(dqn_ref)
            dkn_ref[...] = jnp.zeros_like(dkn_ref)

        dqn_ref[...] += dqn
        dkn_ref[...] += dkn

    res = pl.pallas_call(
        body, name=name + "_bwd", grid=(dil, l // ATT_SPAN), in_specs=ins + [out, out],
        out_specs=[out] * 5 + [gain, gain],
        out_shape=[jax.ShapeDtypeStruct((l, dil * w), F32)] * 5 + [jax.ShapeDtypeStruct(qn.shape, F32)] * 2,
        compiler_params=_params("arbitrary", "arbitrary"),
    )(pav, pav, pav, pav, pav, qn, kn, do.reshape(l, dil * w), dlse.reshape(l, dil * w))
    dq, dkp, dkc, dvp, dvc, dqn, dkn = res
    back = lambda g: jnp.pad(g[ATT_SPAN:], ((0, ATT_SPAN), (0, 0)))
    return dq.reshape(t, w), (dkc + back(dkp)).reshape(t, w), (dvc + back(dvp)).reshape(t, w), dqn, dkn


def _att_mix(o1, o2, o3, l1, l2, l3):
    m = jnp.maximum(jnp.maximum(l1, l2), l3)
    e1, e2, e3 = jnp.exp(l1 - m), jnp.exp(l2 - m), jnp.exp(l3 - m)
    s = e1 + e2 + e3
    return (jnp.concatenate([o1 * (e1 / s), o2 * (e2 / s), o3 * (e3 / s)], axis=1),)


def att_branch(name, pa, qn, kn):
    e = ATT_HEAD_DIM
    gains = lambda p, g: p[ATT_GH * g:ATT_GH * (g + 1)].reshape(ATT_GH, 1, e)

    @jax.custom_vjp
    def groups(pa, qn, kn):
        res = [_att_fwd_call(f"{name}_att{g}", g, dil, pa, gains(qn, g), gains(kn, g)) for g, (_, dil) in enumerate(ATT_GROUPS)]
        return tuple(r[0] for r in res) + tuple(r[1] for r in res)

    def groups_fwd(pa, qn, kn):
        return groups(pa, qn, kn), (pa, qn, kn)

    def groups_bwd(res, cts):
        pa, qn, kn = res
        n = len(ATT_GROUPS)
        parts = [_att_bwd_call(f"{name}_att{g}", g, dil, pa, gains(qn, g), gains(kn, g), cts[g], cts[n + g])
                 for g, (_, dil) in enumerate(ATT_GROUPS)]
        d_pa = jnp.concatenate([p[i] for i in range(3) for p in parts], axis=1)
        return (d_pa, jnp.concatenate([p[3] for p in parts]).reshape(qn.shape), jnp.concatenate([p[4] for p in parts]).reshape(kn.shape))

    groups.defvjp(groups_fwd, groups_bwd)
    return rowwise(f"{name}_attmix", _att_mix, groups(pa, qn, kn))[0]


def dn_gates(name, ba, a_log, dt_bias):
    place = lambda p: jnp.pad(p.reshape(1, DN_HEADS), ((0, 0), (DN_HEADS, LANES - 2 * DN_HEADS)))

    def f(x, al, dt):
        lane = lax.broadcasted_iota(jnp.int32, x.shape, 1)
        return (jnp.where(lane < DN_HEADS, _sigmoid(x), -jnp.exp(al) * _softplus(x + dt)),)

    return rowwise(name, f, (ba,), (place(a_log), place(dt_bias)))[0]


def _dn_out(o, z, g):
    parts = []
    for h in range(DN_HEADS):
        sl = slice(h * DN_HEAD_DIM, (h + 1) * DN_HEAD_DIM)
        parts.append(_rms(o[:, sl], g[:, sl]) * _silu(z[:, sl]))
    return (jnp.concatenate(parts, axis=1),)


def _merge(ml, za, zb, zc):
    d = D_MODEL
    return (_sigmoid(ml[:, :d]) * za + _sigmoid(ml[:, d:2 * d]) * zb + _sigmoid(ml[:, 2 * d:]) * zc,)


def add_norm(name, x, pend, scale, gain):
    if pend is None:
        return x, rowwise(name, lambda a, g: (_rms(a, g),), (x,), (gain,))[0]

    def f(a, b, g):
        s = a + scale * b
        return s, _rms(s, g)

    return rowwise(name, f, (x, pend), (gain,))


W_IN_PIECES = (("rgx", 0, 1024), ("gate", 1024, 1024), ("att", 2048, 2304), ("dq", 4352, 1024), ("dk", 5376, 1024),
               ("dv", 6400, 1024), ("dz", 7424, 1024), ("ba", 8448, 16), ("mrg", 8464, 3072))
RG_PAR_BLOCKS = ("lane", "lane", "blk", "lane", "blk", "lane", "lane")


def mixer(name, u, w, p):
    mm = lambda nm, a, wt: mm_rows(nm, a[None], wt[None])
    pr = project_in(name + "_in", u, {k: w["in_" + k] for k, _, _ in W_IN_PIECES})
    ya = colwise(name + "_rg", _rg_block, (pr["rgx"], pr["gate"]),
                 (w["rg_conv_w"], p["rg_conv_b"], _block_diag(p["rg_w_r"]), p["rg_b_r"], _block_diag(p["rg_w_i"]),
                  p["rg_b_i"], p["rg_lambda"]), RG_PAR_BLOCKS, 1)[0]
    yb = att_branch(name, pr["att"], p["att_q_norm"], p["att_k_norm"])
    cw = w["dn_conv_w"]
    cq = colwise(name + "_dnq", _dn_conv_block("q"), (pr["dq"],), (cw[:, :1024],), ("lane",), 1)[0]
    ck = colwise(name + "_dnk", _dn_conv_block("k"), (pr["dk"],), (cw[:, 1024:2048],), ("lane",), 1)[0]
    cv = colwise(name + "_dnv", _dn_conv_block("v"), (pr["dv"],), (cw[:, 2048:],), ("lane",), 1)[0]
    gb = dn_gates(name + "_dngate", pr["ba"], p["dn_a_log"], p["dn_dt_bias"])
    o_dn = dn_core(cq, ck, cv, gb)
    yc = rowwise(name + "_dnout", _dn_out, (o_dn, pr["dz"]), (p["dn_out_norm"].reshape(1, D_MODEL),))[0]
    y = rowwise(name + "_merge", _merge, (pr["mrg"], mm(name + "_ba", ya, w["br_a"]), mm(name + "_bb", yb, w["br_b"]),
                                          mm(name + "_bc", yc, w["br_c"])))[0]
    return mm(name + "_out", y, w["w_out"])


def _loss_call(x, pend, target):
    t, d = x.shape
    tile = _row_tile(t)

    def body(x_ref, p_ref, t_ref, loss_ref, g_ref):
        err = x_ref[...] + 0.5 * p_ref[...] - t_ref[...]
        g_ref[...] = err * (1.0 / d)

        @pl.when(pl.program_id(0) == 0)
        def _():
            loss_ref[...] = jnp.zeros_like(loss_ref)

        loss_ref[...] += jnp.full(loss_ref.shape, 0.5 / d, F32) * jnp.sum(err * err)

    blk = pl.BlockSpec((tile, d), lambda i: (i, 0))
    loss, g = pl.pallas_call(
        body, name="loss", grid=(t // tile,), in_specs=[blk, blk, blk],
        out_specs=[pl.BlockSpec((8, LANES), lambda i: (0, 0)), blk],
        out_shape=[jax.ShapeDtypeStruct((8, LANES), F32), jax.ShapeDtypeStruct((t, d), F32)],
        compiler_params=_params("arbitrary"),
    )(x, pend, target)
    return loss[0, 0], g


@jax.custom_vjp
def loss_op(x, pend, target):
    return _loss_call(x, pend, target)[0]


def _loss_fwd(x, pend, target):
    loss, g = _loss_call(x, pend, target)
    return loss, g


def _loss_bwd(g, ct):
    return ct * g, (0.5 * ct) * g, None


loss_op.defvjp(_loss_fwd, _loss_bwd)


def first_ffn(wg, wu, wd, gain, x):
    x, h = add_norm("L0_n1", x, None, 0.0, gain)
    return x, ffn("L0_f1", h, wg, wu, wd)


def rest_of_step(g, conv, p, x, pend, target):
    scale = 0.5
    w = [split_layer({n: g[n, l] for n, _ in MATRICES if (n, l) in g}, {n: conv[n][l] for n, _ in CONVS}) for l in range(len(p))]
    for l in range(len(p)):
        n = f"L{l}"
        if l > 0:
            x, h = add_norm(n + "_n1", x, pend, scale, p[l]["ffn1_norm"])
            pend, scale = ffn(n + "_f1", h, w[l]["ffn1_w_gate"], w[l]["ffn1_w_up"], w[l]["ffn1_w_down"]), 0.5
        x, h = add_norm(n + "_nm", x, pend, scale, p[l]["mix_norm"])
        pend, scale = mixer(n + "_mx", h, w[l], p[l]), 1.0
        x, h = add_norm(n + "_n2", x, pend, scale, p[l]["ffn2_norm"])
        pend, scale = ffn(n + "_f2", h, w[l]["ffn2_w_gate"], w[l]["ffn2_w_up"], w[l]["ffn2_w_down"]), 0.5
    return loss_op(x, pend, target)


WEIGHT_NAMES = ("ffn1_norm", "ffn1_w_gate", "ffn1_w_up", "ffn1_w_down", "mix_norm", "w_in", "rg_conv_w", "rg_conv_b",
                "rg_w_r", "rg_b_r", "rg_w_i", "rg_b_i", "rg_lambda", "att_q_norm", "att_k_norm", "dn_conv_w", "dn_a_log",
                "dn_dt_bias", "dn_out_norm", "w_branch", "w_out", "ffn2_norm", "ffn2_w_gate", "ffn2_w_up", "ffn2_w_down")
MATRICES = (("ffn1_w_gate", 2), ("ffn1_w_up", 2), ("ffn1_w_down", 1), ("w_in", 2), ("w_branch", 1), ("w_out", 1),
            ("ffn2_w_gate", 2), ("ffn2_w_up", 2), ("ffn2_w_down", 1))
CONVS = (("rg_conv_w", 2), ("dn_conv_w", 2))
SHARD_AXIS = dict(MATRICES + CONVS)
SMALL_NAMES = tuple(n for n in WEIGHT_NAMES if n not in SHARD_AXIS)
ROW_PARAMS = ("ffn1_norm", "mix_norm", "rg_conv_b", "rg_b_r", "rg_b_i", "rg_lambda", "ffn2_norm")
FFN_MATS = ("ffn1_w_gate", "ffn1_w_up", "ffn1_w_down", "ffn2_w_gate", "ffn2_w_up", "ffn2_w_down")
TRANSPOSED_MATS = ("ffn1_w_gate", "ffn1_w_up", "ffn2_w_gate", "ffn2_w_up")
W_IN_SHARD = 2884
GATHER_ORDER = ((("ffn1_w_gate", 0), ("ffn1_w_up", 0), ("ffn1_w_down", 0)),
                (("w_in", 0),),
                None)
GATHER_IDS = (1, 6, 7)
LATE_MATS = ("ffn2_w_gate", "ffn2_w_up", "ffn2_w_down", "w_out", "w_branch")
EXCHANGE_GROUPS = (lambda n, l: l == 1 and n in LATE_MATS,
                   lambda n, l: (l == 1) != (n in LATE_MATS),
                   lambda n, l: l == 0 and n == "w_in",
                   lambda n, l: l == 0 and n not in LATE_MATS and n != "w_in")


def _shard_minor(a, axis):
    a = jnp.moveaxis(a, 0, axis)
    return a.reshape(a.shape[:axis] + (N_CHIPS * a.shape[axis + 1],) + a.shape[axis + 2:])


def _w_in_piece(g, off, n):
    s = W_IN_SHARD
    parts = [g[j][:, max(off, j * s) - j * s:min(off + n, (j + 1) * s) - j * s]
             for j in range(N_CHIPS) if max(off, j * s) < min(off + n, (j + 1) * s)]
    return jnp.concatenate(parts, axis=1) if len(parts) > 1 else parts[0]


def _w_in_chip_grad(gl, j):
    s = W_IN_SHARD
    parts = [gl["in_" + k][:, max(off, j * s) - off:min(off + n, (j + 1) * s) - off]
             for k, off, n in W_IN_PIECES if max(off, j * s) < min(off + n, (j + 1) * s)]
    return jnp.concatenate(parts, axis=1)


def _layer_weights(g, conv):
    w = {n: g[n] for n in FFN_MATS if n in g}
    w["w_out"] = g["w_out"].reshape(D_MODEL, D_MODEL)
    for k, off, n in W_IN_PIECES:
        piece = _w_in_piece(g["w_in"], off, n)
        w["in_" + k] = jnp.pad(piece, ((0, 0), (0, LANES - n))) if n < LANES else piece
    wb = g["w_branch"].reshape(-1, D_MODEL)
    w["br_a"], w["br_b"], w["br_c"] = wb[:1024], wb[1024:1792], wb[1792:]
    return dict(w, **conv)


def _layer_weight_grads(gl):
    out = {n: gl[n] for n in FFN_MATS if n in gl}
    out["w_out"] = gl["w_out"].reshape(N_CHIPS, -1, D_MODEL)
    out["w_branch"] = jnp.concatenate([gl["br_a"], gl["br_b"], gl["br_c"]], axis=0).reshape(N_CHIPS, -1, D_MODEL)
    out["w_in"] = jnp.stack([_w_in_chip_grad(gl, j) for j in range(N_CHIPS)])
    return out, {n: gl[n] for n, _ in CONVS}


@jax.custom_vjp
def split_layer(g, conv):
    return _layer_weights(g, conv)


split_layer.defvjp(lambda g, conv: (_layer_weights(g, conv), None), lambda _, gw: _layer_weight_grads(gw))


def layer_small(small, l):
    p = {n: small[n][l] for n in SMALL_NAMES}
    for n in ROW_PARAMS:
        p[n] = small[n][l:l + 1]
    return p


def layer_small_grads(gp, small):
    return {n: jnp.stack([g[n] for g in gp]).reshape(small[n].shape) for n in SMALL_NAMES}


HBM_SPEC = pl.BlockSpec(memory_space=pl.ANY)


def _place():
    x, y, c = lax.axis_index("x"), lax.axis_index("y"), lax.axis_index("c")
    other_chips = [(1 - x, y), (x, 1 - y), (1 - x, 1 - y)]
    return x, y, c, 2 * x + y, (x, y, 1 - c), other_chips


def _half_rows(ref, lead, hc):
    hr = ref.shape[-2] // 2
    return ref.at[(*lead, pl.ds(pl.multiple_of(hc * hr, 16), hr), slice(None))]


def _chip_index():
    return (2 * lax.axis_index("x") + lax.axis_index("y")).astype(jnp.int32).reshape(1)


def cast_into_blocks(name, w):
    l, rows, cols = w.shape
    tr = rows // 2

    def body(me_ref, w_ref, *o_refs):
        for a, o_ref in enumerate(o_refs):
            o_ref[...] = w_ref[a:a + 1].astype(BF16)

    return pl.pallas_call(
        body, name=name, out_shape=[jax.ShapeDtypeStruct((N_CHIPS, rows, cols), BF16)] * l,
        grid_spec=pltpu.PrefetchScalarGridSpec(
            num_scalar_prefetch=1, grid=(rows // tr,),
            in_specs=[pl.BlockSpec((l, tr, cols), lambda i, me: (0, i, 0))],
            out_specs=[pl.BlockSpec((1, tr, cols), lambda i, me: (me[0], i, 0))] * l),
        compiler_params=_params("parallel"),
    )(_chip_index(), w)


def _gather_blocks(bufs_in, bufs_out, send_sems, recv_sems):
    n = len(bufs_in)
    x, y, c, me, sibling, chips = _place()

    def copy(s, src, dst, to):
        return pltpu.make_async_remote_copy(src_ref=src, dst_ref=dst, send_sem=send_sems.at[s], recv_sem=recv_sems.at[s],
                                            device_id=to, device_id_type=MESH)

    first, passed = [], []
    for j, (cx, cy) in enumerate(chips):
        for i in range(n):
            cp = copy(6 * i + j, _half_rows(bufs_in[i], (me,), c), _half_rows(bufs_out[i], (me,), c), (cx, cy, c))
            cp.start()
            first.append(cp)
    for j, (cx, cy) in enumerate(chips):
        k = 2 * cx + cy
        for i in range(n):
            copy(6 * i + j, _half_rows(bufs_in[i], (me,), c), _half_rows(bufs_out[i], (k,), c), (cx, cy, c)).wait_recv()
            cp = copy(6 * i + 3 + j, _half_rows(bufs_out[i], (k,), c), _half_rows(bufs_out[i], (k,), c), sibling)
            cp.start()
            passed.append(cp)
    for j, (cx, cy) in enumerate(chips):
        k = 2 * cx + cy
        for i in range(n):
            copy(6 * i + 3 + j, _half_rows(bufs_in[i], (me,), c), _half_rows(bufs_out[i], (k,), 1 - c), sibling).wait_recv()
    for cp in first + passed:
        cp.wait_send()


def _handshake(peers):
    barrier = pltpu.get_barrier_semaphore()
    for p in peers:
        pl.semaphore_signal(barrier, inc=1, device_id=p, device_id_type=MESH)
    pl.semaphore_wait(barrier, len(peers))


def allgather_blocks_sc(name, bufs, collective_id):
    n = len(bufs)
    refs = [jax.new_ref(b, memory_space=pltpu.MemorySpace.HBM) for b in bufs]

    @pl.kernel(mesh=plsc.ScalarSubcoreMesh(axis_name="sequencer", num_cores=1), name=name,
               scratch_types=(pltpu.SemaphoreType.DMA((6 * n,)), pltpu.SemaphoreType.DMA((6 * n,))),
               compiler_params=pltpu.CompilerParams(collective_id=collective_id))
    def launch(send_sems, recv_sems):
        x, y, c, me, sibling, chips = _place()
        _handshake([(cx, cy, c) for cx, cy in chips] + [sibling])
        _gather_blocks(refs, refs, send_sems, recv_sems)

    launch()
    return [jax.freeze(r) for r in refs]


PEER_FLIPS = tuple((fx, fy, fc) for fx in (0, 1) for fy in (0, 1) for fc in (0, 1))[1:]


def exchange_pieces_sc(name, gs, collective_id):
    n = len(gs)

    def body(*refs):
        ins, outs = refs[:n], refs[n:2 * n]
        send_sems, recv_sems = refs[2 * n:]
        x, y, c, me, sibling, chips = _place()
        my_dev = 4 * x + 2 * y + c
        flip = lambda v, f: 1 - v if f else v
        peers = [(flip(x, fx), flip(y, fy), flip(c, fc)) for fx, fy, fc in PEER_FLIPS]
        _handshake(peers)
        sends = []
        for r, (px, py, pc) in enumerate(peers):
            for i in range(n):
                cp = pltpu.make_async_remote_copy(
                    src_ref=_half_rows(ins[i], (2 * px + py,), pc), dst_ref=outs[i].at[my_dev], send_sem=send_sems.at[7 * i + r],
                    recv_sem=recv_sems.at[7 * i + r], device_id=(px, py, pc), device_id_type=MESH)
                cp.start()
                sends.append(cp)
        for r, (px, py, pc) in enumerate(peers):
            for i in range(n):
                pltpu.make_async_remote_copy(
                    src_ref=_half_rows(ins[i], (me,), c), dst_ref=outs[i].at[4 * px + 2 * py + pc], send_sem=send_sems.at[7 * i + r],
                    recv_sem=recv_sems.at[7 * i + r], device_id=(px, py, pc), device_id_type=MESH).wait_recv()
        for cp in sends:
            cp.wait_send()

    return pl.kernel(
        body, name=name, mesh=plsc.ScalarSubcoreMesh(axis_name="sequencer", num_cores=1),
        out_type=[jax.ShapeDtypeStruct((N_DEV, g.shape[1] // 2, g.shape[2]), g.dtype) for g in gs],
        scratch_types=[pltpu.SemaphoreType.DMA((7 * n,)), pltpu.SemaphoreType.DMA((7 * n,))],
        compiler_params=pltpu.CompilerParams(collective_id=collective_id),
    )(*gs)


def sibling_share_halves(name, fs):
    n = len(fs)
    every = (slice(None),)

    def body(*refs):
        ins, outs = refs[:n], refs[n:2 * n]
        send_sems, recv_sems = refs[2 * n:]
        x, y, c, me, sibling, chips = _place()
        sends = []
        for i in range(n):
            cp = pltpu.make_async_remote_copy(src_ref=_half_rows(ins[i], every, c), dst_ref=_half_rows(outs[i], every, c),
                                              send_sem=send_sems.at[i], recv_sem=recv_sems.at[i], device_id=sibling, device_id_type=MESH)
            cp.start()
            sends.append(cp)
        for i in range(n):
            pltpu.make_async_remote_copy(src_ref=_half_rows(ins[i], every, c), dst_ref=_half_rows(outs[i], every, 1 - c),
                                         send_sem=send_sems.at[i], recv_sem=recv_sems.at[i], device_id=sibling,
                                         device_id_type=MESH).wait_recv()
        for cp in sends:
            cp.wait_send()

    return pl.pallas_call(
        body, name=name, out_shape=[jax.ShapeDtypeStruct(f.shape, f.dtype) for f in fs],
        in_specs=[HBM_SPEC] * n, out_specs=[HBM_SPEC] * n, input_output_aliases={i: i for i in range(n)},
        scratch_shapes=[pltpu.SemaphoreType.DMA((n,)), pltpu.SemaphoreType.DMA((n,))],
    )(*fs)


def allgather_small_sc(name, v, collective_id):
    def body(v_ref, out_ref, send_sems, recv_sems, local_sem):
        x, y, c, me, sibling, chips = _place()
        my_dev = 4 * x + 2 * y + c
        flip = lambda a, f: 1 - a if f else a
        peers = [(flip(x, fx), flip(y, fy), flip(c, fc)) for fx, fy, fc in PEER_FLIPS]
        _handshake(peers)
        mine = pltpu.make_async_copy(v_ref, out_ref.at[my_dev], local_sem)
        mine.start()
        sends = []
        for r, peer in enumerate(peers):
            cp = pltpu.make_async_remote_copy(src_ref=v_ref, dst_ref=out_ref.at[my_dev], send_sem=send_sems.at[r],
                                              recv_sem=recv_sems.at[r], device_id=peer, device_id_type=MESH)
            cp.start()
            sends.append(cp)
        for r, (px, py, pc) in enumerate(peers):
            pltpu.make_async_remote_copy(src_ref=v_ref, dst_ref=out_ref.at[4 * px + 2 * py + pc], send_sem=send_sems.at[r],
                                         recv_sem=recv_sems.at[r], device_id=(px, py, pc), device_id_type=MESH).wait_recv()
        for cp in sends:
            cp.wait_send()
        mine.wait()

    return pl.kernel(
        body, name=name, mesh=plsc.ScalarSubcoreMesh(axis_name="sequencer", num_cores=1),
        out_type=jax.ShapeDtypeStruct((N_DEV,) + v.shape, v.dtype),
        scratch_types=[pltpu.SemaphoreType.DMA((7,)), pltpu.SemaphoreType.DMA((7,)), pltpu.SemaphoreType.DMA],
        compiler_params=pltpu.CompilerParams(collective_id=collective_id),
    )(v)


SUM_BLOCK_ELEMS = 512 * 1024


def sum_slabs(name, b):
    k, h, w = b.shape

    def body(b_ref, o_ref):
        acc = b_ref[0].astype(F32)
        for i in range(1, k):
            acc = acc + b_ref[i].astype(F32)
        o_ref[...] = acc

    return pl.pallas_call(
        body, name=name, out_shape=jax.ShapeDtypeStruct((h, w), F32),
        in_specs=[pl.BlockSpec(memory_space=pltpu.VMEM)], out_specs=pl.BlockSpec(memory_space=pltpu.VMEM),
        compiler_params=pltpu.CompilerParams(vmem_limit_bytes=VMEM_LIMIT),
    )(b)


def sum_pieces(name, pieces, gs):
    nl = len(pieces)
    k, h, w = pieces[0].shape
    tile = max(t for t in range(16, h + 1, 16) if h % t == 0 and (t * w <= SUM_BLOCK_ELEMS or t == 16))
    nt = h // tile
    x, y, c = lax.axis_index("x"), lax.axis_index("y"), lax.axis_index("c")
    place = [v.astype(jnp.int32).reshape(1) for v in (c, 2 * x + y, 4 * x + 2 * y + c)]

    assert nl == 2

    def tile_of(l, a, i):
        return i * a if l else i * (1 - a) + (nt - 1) * a

    def body(c_ref, me_ref, dev_ref, *refs):
        p_refs, g_refs, o_ref = refs[:nl], refs[nl:2 * nl], refs[2 * nl]
        my_dev = dev_ref[0]
        for l in range(nl):
            @pl.when(pl.program_id(0) == l)
            def _():
                o_ref[0] = jnp.zeros(o_ref.shape[1:], F32)
                for d in range(k):
                    @pl.when(my_dev == d)
                    def _():
                        o_ref[0] += g_refs[l][0].astype(F32)

                    @pl.when(my_dev != d)
                    def _():
                        o_ref[0] += p_refs[l][d].astype(F32)

    in_specs = [pl.BlockSpec((k, tile, w), functools.partial(lambda l, a, i, cc, me, dev: (0, tile_of(l, a, i), 0), l))
                for l in range(nl)]
    in_specs += [pl.BlockSpec((1, tile, w), functools.partial(lambda l, a, i, cc, me, dev: (me[0], cc[0] * nt + tile_of(l, a, i), 0), l))
                 for l in range(nl)]
    return pl.pallas_call(
        body, name=name, out_shape=jax.ShapeDtypeStruct((nl, 2 * h, w), F32),
        grid_spec=pltpu.PrefetchScalarGridSpec(
            num_scalar_prefetch=3, grid=(nl, nt), in_specs=in_specs,
            out_specs=pl.BlockSpec((1, tile, w), lambda a, i, cc, me, dev: (a, cc[0] * nt + i, 0))),
        compiler_params=_params("arbitrary", "arbitrary"),
    )(*place, *pieces, *gs)


def _adam_block(w, g, m, v):
    m = ADAM_B1 * m + (1.0 - ADAM_B1) * g
    v = ADAM_B2 * v + (1.0 - ADAM_B2) * (g * g)
    m_hat = m / (1.0 - ADAM_B1 ** ADAM_STEP)
    v_hat = v / (1.0 - ADAM_B2 ** ADAM_STEP)
    return -ADAM_LR * (m_hat / (jnp.sqrt(v_hat) + ADAM_EPS) + ADAM_WD * w), m, v


def adamw(name, w, g, m, v):
    shape = w.shape
    cols = shape[-1]
    rows = w.size // cols
    tile = 128 if rows % 128 == 0 else rows
    flat = [a.reshape(rows, cols) for a in (w, g, m, v)]

    def body(w_ref, g_ref, m_ref, v_ref, d_ref, nm_ref, nv_ref):
        d_ref[...], nm_ref[...], nv_ref[...] = _adam_block(w_ref[...], g_ref[...], m_ref[...], v_ref[...])

    blk = pl.BlockSpec((tile, cols), lambda i: (i, 0))
    res = pl.pallas_call(
        body, name=name, grid=(rows // tile,), in_specs=[blk] * 4, out_specs=[blk] * 3,
        out_shape=[jax.ShapeDtypeStruct((rows, cols), F32)] * 3, compiler_params=_params("parallel"),
    )(*flat)
    return tuple(r.reshape(shape) for r in res)


def _pack_small(values):
    flat = jnp.concatenate([v.reshape(-1) for v in values.values()])
    n = flat.shape[0]
    total = -(-n // (8 * LANES)) * (8 * LANES)
    return jnp.pad(flat, (0, total - n)).reshape(-1, LANES)


def _unpack_small(v, shapes):
    flat = v.reshape(-1)
    out, off = {}, 0
    for n, shape in shapes.items():
        sz = int(np.prod(shape))
        out[n] = flat[off:off + sz].reshape(shape)
        off += sz
    return out


def kernel(x, ffn1_norm, ffn1_w_gate, ffn1_w_up, ffn1_w_down, mix_norm, w_in, rg_conv_w, rg_conv_b, rg_w_r, rg_b_r, rg_w_i, rg_b_i, rg_lambda, att_q_norm, att_k_norm, dn_conv_w, dn_a_log, dn_dt_bias, dn_out_norm, w_branch, w_out, ffn2_norm, ffn2_w_gate, ffn2_w_up, ffn2_w_down, loss_target, m_ffn1_norm, m_ffn1_w_gate, m_ffn1_w_up, m_ffn1_w_down, m_mix_norm, m_w_in, m_rg_conv_w, m_rg_conv_b, m_rg_w_r, m_rg_b_r, m_rg_w_i, m_rg_b_i, m_rg_lambda, m_att_q_norm, m_att_k_norm, m_dn_conv_w, m_dn_a_log, m_dn_dt_bias, m_dn_out_norm, m_w_branch, m_w_out, m_ffn2_norm, m_ffn2_w_gate, m_ffn2_w_up, m_ffn2_w_down, v_ffn1_norm, v_ffn1_w_gate, v_ffn1_w_up, v_ffn1_w_down, v_mix_norm, v_w_in, v_rg_conv_w, v_rg_conv_b, v_rg_w_r, v_rg_b_r, v_rg_w_i, v_rg_b_i, v_rg_lambda, v_att_q_norm, v_att_k_norm, v_dn_conv_w, v_dn_a_log, v_dn_dt_bias, v_dn_out_norm, v_w_branch, v_w_out, v_ffn2_norm, v_ffn2_w_gate, v_ffn2_w_up, v_ffn2_w_down):
    given = dict(locals())
    for n in TRANSPOSED_MATS:
        for pre in ("", "m_", "v_"):
            given[pre + n] = jnp.swapaxes(given[pre + n], 1, 2)
    small = {n: given[n] for n in SMALL_NAMES}
    n_layers = ffn1_norm.shape[0]
    mat_names = [n for n, _ in MATRICES]
    conv_names = [n for n, _ in CONVS]

    blocks = {}
    for n in mat_names:
        for l, b in enumerate(cast_into_blocks("cast_" + n, given[n])):
            blocks[n, l] = b
    first, done = {}, []
    for i, wanted in enumerate(GATHER_ORDER[:-1]):
        bufs, _ = lax.optimization_barrier(([blocks[k] for k in wanted], done))
        done = allgather_blocks_sc(f"allgather_{i}", bufs, GATHER_IDS[i])
        first.update(zip(wanted, done))
    rest = {k: b for k, b in blocks.items() if k not in first}
    taps = jnp.concatenate([given[n].reshape(-1) for n in conv_names]).reshape(-1, LANES)
    taps = allgather_small_sc("allgather_taps", taps, 8).reshape(N_CHIPS, 2, -1)[:, 0]
    conv, off = {}, 0
    for n, ax in CONVS:
        sz = given[n].size
        conv[n] = _shard_minor(taps[:, off:off + sz].reshape((N_CHIPS,) + given[n].shape), ax)
        off += sz
    p = [layer_small(small, l) for l in range(n_layers)]

    ffn1_keys = GATHER_ORDER[0]
    (x1, pend), first_vjp = jax.vjp(first_ffn, *[first[k] for k in ffn1_keys], p[0]["ffn1_norm"], x[0])
    keys = list(rest)
    bufs, pend, w_in0 = lax.optimization_barrier(([rest[k] for k in keys], pend, first["w_in", 0]))
    gathered = dict(zip(keys, allgather_blocks_sc("allgather_2", bufs, GATHER_IDS[2])))
    gathered["w_in", 0] = w_in0
    loss, (g_mats, g_conv, gp, gx1, gpend) = jax.value_and_grad(rest_of_step, argnums=(0, 1, 2, 3, 4))(
        gathered, conv, p, x1, pend, loss_target[0])
    *g_ffn1, gp[0]["ffn1_norm"], gx = first_vjp((gx1, gpend))
    g_mats.update(zip(ffn1_keys, g_ffn1))

    pieces = {}
    for i, group in enumerate(EXCHANGE_GROUPS):
        keys = [k for k in g_mats if group(*k)]
        pieces.update(zip(keys, exchange_pieces_sc(f"exchange_{i}", [g_mats[k] for k in keys], 2 + i)))
    halves = {n: sum_pieces("sum_" + n, [pieces[n, l] for l in range(n_layers)], [g_mats[n, l] for l in range(n_layers)])
              for n in mat_names}
    grads = {}
    for tag, names in (("late", [n for n in mat_names if n in LATE_MATS]), ("early", [n for n in mat_names if n not in LATE_MATS])):
        grads.update(zip(names, sibling_share_halves("share_" + tag, [halves[n] for n in names])))

    g_small = dict(layer_small_grads(gp, small), **g_conv, loss=loss.reshape(1))
    packed_small = _pack_small(g_small)
    slabs = allgather_small_sc("allgather_small", packed_small, 9)
    summed =_unpack_small(sum_slabs("sum_small", slabs), {n: g.shape for n, g in g_small.items()})
    chip = 2 * lax.axis_index("x") + lax.axis_index("y")
    for n in SMALL_NAMES:
        grads[n] = summed[n]
    for n, ax in CONVS:
        s = given[n].shape[ax]
        grads[n] = lax.dynamic_slice_in_dim(summed[n], chip * s, s, axis=ax)

    upd = {n: adamw("adamw_" + n, given[n], grads[n], given["m_" + n], given["v_" + n]) for n in WEIGHT_NAMES}
    out = lambda n, a: jnp.swapaxes(a, 1, 2) if n in TRANSPOSED_MATS else a
    return (summed["loss"][0], gx[None], *[out(n, grads[n]) for n in WEIGHT_NAMES], *[out(n, upd[n][0]) for n in WEIGHT_NAMES],
            *[out(n, upd[n][1]) for n in WEIGHT_NAMES], *[out(n, upd[n][2]) for n in WEIGHT_NAMES])
```

```python
import functools
import math

import jax
import jax.numpy as jnp
import numpy as np
from jax import lax
from jax.experimental import pallas as pl
from jax.experimental.pallas import tpu as pltpu
from jax.experimental.pallas import tpu_sc as plsc

F32 = jnp.float32
BF16 = jnp.bfloat16
MESH = pl.DeviceIdType.MESH

D_MODEL = 1024
FFN_DIM = 2816
RG_C = 8.0
ATT_GROUPS = ((128, 1), (512, 4), (2048, 16))
ATT_HEADS = 12
ATT_HEAD_DIM = 64
ATT_SPAN = 128
DN_HEADS = 8
DN_HEAD_DIM = 128
DN_CHUNK = 64
EPS = 1e-6
NEG_INF = -1e30
N_CHIPS = 4
N_DEV = 8

ADAM_LR, ADAM_B1, ADAM_B2, ADAM_EPS, ADAM_WD, ADAM_STEP = 0.001, 0.9, 0.999, 1e-08, 0.01, 10

LANES = 128
VMEM_LIMIT = 56 * 1024 * 1024


def _params(*sem):
    return pltpu.CompilerParams(dimension_semantics=sem or None, vmem_limit_bytes=VMEM_LIMIT)


def _sigmoid(x):
    return 1.0 / (1.0 + jnp.exp(-x))


def _silu(x):
    return x * _sigmoid(x)


def _softplus(x):
    return jnp.maximum(x, 0.0) + jnp.log(1.0 + jnp.exp(-jnp.abs(x)))


def _gelu(x):
    return 0.5 * x * (1.0 + jnp.tanh(math.sqrt(2.0 / math.pi) * (x + 0.044715 * (x * x * x))))


def _neg_expm1(x):
    series = -x * (1.0 + x * (0.5 + x * (1.0 / 6 + x * (1.0 / 24 + x * (1.0 / 120 + x * (1.0 / 720))))))
    return jnp.where(x > -0.25, series, 1.0 - jnp.exp(x))


def _rms(x, g):
    return x * lax.rsqrt(jnp.mean(x * x, axis=-1, keepdims=True) + EPS) * g


_MM_DIMS = {"nn": (((1,), (0,)), ((), ())), "nt": (((1,), (1,)), ((), ())), "tn": (((0,), (0,)), ((), ()))}


def _split(a):
    hi = a.astype(BF16)
    return hi, (a - hi.astype(F32)).astype(BF16)


def _mxu(a, b, form, passes):
    (ca, cb), _ = _MM_DIMS[form]
    if a.ndim == 3:
        dims = (((ca[0] + 1,), (cb[0] + 1,)), ((0,), (0,)))
    else:
        dims = _MM_DIMS[form]
    dg = lambda p, q: lax.dot_general(p, q, dims, preferred_element_type=F32)
    if passes == 1:
        return dg(a.astype(BF16), b.astype(BF16))
    (a_hi, a_lo), (b_hi, b_lo) = _split(a), _split(b)
    return dg(a_hi, b_hi) + (dg(a_hi, b_lo) + dg(a_lo, b_hi))


@functools.partial(jax.custom_vjp, nondiff_argnums=(2, 3))
def _mm(a, b, form, passes):
    return _mxu(a, b, form, passes)


def _mm_fwd(a, b, form, passes):
    return _mxu(a, b, form, passes), (a, b)


def _mm_bwd(form, passes, res, g):
    a, b = res
    if form == "nn":
        return _mm(g, b, "nt", passes), _mm(a, g, "tn", passes)
    if form == "nt":
        return _mm(g, b, "nn", passes), _mm(g, a, "tn", passes)
    return _mm(b, g, "nt", passes), _mm(a, g, "nn", passes)


_mm.defvjp(_mm_fwd, _mm_bwd)


def _dot(a, b):
    return _mm(a, b, "nn", 1)


def _dot_nt(a, b):
    return _mm(a, b, "nt", 1)


def _dot_tn(a, b):
    return _mm(a, b, "tn", 1)


def _dot3(a, b):
    return _mm(a, b, "nn", 3)


def _rows(shape):
    return lax.broadcasted_iota(jnp.int32, shape, len(shape) - 2)


def _roll_down(x, s, fill):
    return jnp.where(_rows(x.shape) >= s, pltpu.roll(x, s, x.ndim - 2), fill)


def _roll_up(x, s, fill):
    n = x.shape[-2]
    return jnp.where(_rows(x.shape) < n - s, pltpu.roll(x, n - s, x.ndim - 2), fill)


@functools.partial(jax.custom_vjp, nondiff_argnums=(1,))
def _shift(x, s):
    return _roll_down(x, s, 0.0)


def _shift_fwd(x, s):
    return _roll_down(x, s, 0.0), None


def _shift_bwd(s, _, g):
    return (_roll_up(g, s, 0.0),)


_shift.defvjp(_shift_fwd, _shift_bwd)


def _causal_conv(x, w):
    return w[0:1] * _shift(x, 3) + w[1:2] * _shift(x, 2) + w[2:3] * _shift(x, 1) + w[3:4] * x


@jax.custom_vjp
def _lin_scan(a, b):
    return _lin_scan_fwd(a, b)[0]


def _lin_scan_fwd(a, b):
    a0 = a
    s = 1
    while s < a.shape[0]:
        b = a * _roll_down(b, s, 0.0) + b
        a = a * _roll_down(a, s, 1.0)
        s *= 2
    return b, (a0, b)


def _lin_scan_bwd(res, g):
    a, h = res
    c = _roll_up(a, 1, 0.0)
    s = 1
    while s < a.shape[0]:
        g = c * _roll_up(g, s, 0.0) + g
        c = c * _roll_up(c, s, 1.0)
        s *= 2
    return g * _roll_down(h, 1, 0.0), g


_lin_scan.defvjp(_lin_scan_fwd, _lin_scan_bwd)


@jax.custom_vjp
def _cumsum_rows(x):
    s = 1
    while s < x.shape[-2]:
        x = x + _roll_down(x, s, 0.0)
        s *= 2
    return x


def _cumsum_rows_fwd(x):
    return _cumsum_rows(x), None


def _cumsum_rows_bwd(_, g):
    s = 1
    while s < g.shape[-2]:
        g = g + _roll_up(g, s, 0.0)
        s *= 2
    return (g,)


_cumsum_rows.defvjp(_cumsum_rows_fwd, _cumsum_rows_bwd)


ROW_BLOCK_BYTES = 14 * 1024 * 1024


def _row_tile(t, width=0):
    for tile in (512, 256):
        if t % tile == 0 and (tile == 256 or tile * width * 4 <= ROW_BLOCK_BYTES):
            return tile
    return t


def _rowwise_fwd_call(name, f, rows, pars, tile):
    t = rows[0].shape[0]
    outs = jax.eval_shape(f, *[jax.ShapeDtypeStruct((tile, r.shape[1]), F32) for r in rows],
                          *[jax.ShapeDtypeStruct(p.shape, F32) for p in pars])
    nr, npar = len(rows), len(pars)

    def body(*refs):
        ins = [r[...] for r in refs[:nr + npar]]
        res = f(*ins)
        for o_ref, o in zip(refs[nr + npar:], res):
            o_ref[...] = o.astype(o_ref.dtype)

    return pl.pallas_call(
        body, name=name, grid=(t // tile,),
        in_specs=[pl.BlockSpec((tile, r.shape[1]), lambda i: (i, 0)) for r in rows]
        + [pl.BlockSpec(p.shape, lambda i: (0, 0)) for p in pars],
        out_specs=[pl.BlockSpec((tile, o.shape[1]), lambda i: (i, 0)) for o in outs],
        out_shape=[jax.ShapeDtypeStruct((t, o.shape[1]), F32) for o in outs],
        compiler_params=_params("parallel"),
    )(*rows, *pars)


def _rowwise_bwd_call(name, f, rows, pars, cts, tile):
    t = rows[0].shape[0]
    nr, npar, nct = len(rows), len(pars), len(cts)

    def body(*refs):
        ins = [r[...] for r in refs[:nr + npar]]
        gs = tuple(r[...] for r in refs[nr + npar:nr + npar + nct])
        outs = refs[nr + npar + nct:]
        _, vjp = jax.vjp(f, *ins)
        d = vjp(gs)
        for o_ref, v in zip(outs[:nr], d[:nr]):
            o_ref[...] = v

        @pl.when(pl.program_id(0) == 0)
        def _():
            for o_ref in outs[nr:]:
                o_ref[...] = jnp.zeros_like(o_ref)

        for o_ref, v in zip(outs[nr:], d[nr:]):
            o_ref[...] += v

    res = pl.pallas_call(
        body, name=name, grid=(t // tile,),
        in_specs=[pl.BlockSpec((tile, r.shape[1]), lambda i: (i, 0)) for r in rows]
        + [pl.BlockSpec(p.shape, lambda i: (0, 0)) for p in pars]
        + [pl.BlockSpec((tile, c.shape[1]), lambda i: (i, 0)) for c in cts],
        out_specs=[pl.BlockSpec((tile, r.shape[1]), lambda i: (i, 0)) for r in rows]
        + [pl.BlockSpec(p.shape, lambda i: (0, 0)) for p in pars],
        out_shape=[jax.ShapeDtypeStruct(r.shape, F32) for r in rows]
        + [jax.ShapeDtypeStruct(p.shape, F32) for p in pars],
        compiler_params=_params("arbitrary"),
    )(*rows, *pars, *cts)
    return tuple(res[:nr]), tuple(res[nr:])


def rowwise(name, f, rows, pars=()):
    outs = jax.eval_shape(f, *[jax.ShapeDtypeStruct((8, r.shape[1]), F32) for r in rows],
                          *[jax.ShapeDtypeStruct(p.shape, F32) for p in pars])
    tile = _row_tile(rows[0].shape[0], 2 * sum(r.shape[1] for r in rows) + sum(o.shape[1] for o in outs))

    @jax.custom_vjp
    def op(rows, pars):
        return tuple(_rowwise_fwd_call(name, f, rows, pars, tile))

    def op_fwd(rows, pars):
        return op(rows, pars), (rows, pars)

    def op_bwd(res, cts):
        return _rowwise_bwd_call(name + "_bwd", f, res[0], res[1], tuple(cts), tile)

    op.defvjp(op_fwd, op_bwd)
    return op(tuple(rows), tuple(pars))


MM_TM = 512


def _tile_of(n, cap):
    best = None
    for c in range(LANES, min(n, cap) + 1, LANES):
        if n % c == 0:
            best = c
    return best or n


def _mmc_fwd(name, h, w):
    m, k = h.shape
    j, _, n = w.shape
    tm, tn = MM_TM, _tile_of(n, 1408)

    def body(h_ref, w_ref, o_ref):
        o_ref[0] = _dot(h_ref[...], w_ref[0])

    return pl.pallas_call(
        body, name=name, grid=(m // tm, j, n // tn),
        in_specs=[pl.BlockSpec((tm, k), lambda i, b, c: (i, 0)), pl.BlockSpec((1, k, tn), lambda i, b, c: (b, 0, c))],
        out_specs=pl.BlockSpec((1, tm, tn), lambda i, b, c: (b, i, c)),
        out_shape=jax.ShapeDtypeStruct((j, m, n), F32),
        compiler_params=_params("parallel", "parallel", "parallel"),
    )(h, w)


def _mmc_dw(name, h, dy):
    m, k = h.shape
    j, _, n = dy.shape
    tk, tn = _tile_of(k, 512), _tile_of(n, 1152)

    def body(h_ref, dy_ref, o_ref):
        o_ref[0] = _dot_tn(h_ref[...], dy_ref[0]).astype(BF16)

    return pl.pallas_call(
        body, name=name, grid=(j, k // tk, n // tn),
        in_specs=[pl.BlockSpec((m, tk), lambda b, i, c: (0, i)), pl.BlockSpec((1, m, tn), lambda b, i, c: (b, 0, c))],
        out_specs=pl.BlockSpec((1, tk, tn), lambda b, i, c: (b, i, c)),
        out_shape=jax.ShapeDtypeStruct((j, k, n), BF16),
        compiler_params=_params("parallel", "parallel", "parallel"),
    )(h, dy)


PROJ_GROUP_COLS = 4608


def _proj_dh(name, dys, ws, acc):
    m, k = dys[0].shape[0], ws[0].shape[0]
    n, tm = len(dys), 256

    def body(*refs):
        dy_refs, w_refs, rest = refs[:n], refs[n:2 * n], refs[2 * n:]
        total = _dot_nt(dy_refs[0][...], w_refs[0][...])
        for dy_ref, w_ref in zip(dy_refs[1:], w_refs[1:]):
            total = total + _dot_nt(dy_ref[...], w_ref[...])
        if acc is not None:
            total = total + rest[0][...]
        rest[-1][...] = total

    row = lambda width: pl.BlockSpec((tm, width), lambda i: (i, 0))
    return pl.pallas_call(
        body, name=name, grid=(m // tm,),
        in_specs=[row(d.shape[1]) for d in dys] + [pl.BlockSpec(w.shape, lambda i: (0, 0)) for w in ws] + ([row(k)] if acc is not None else []),
        out_specs=row(k), out_shape=jax.ShapeDtypeStruct((m, k), F32), compiler_params=_params("parallel"),
    )(*dys, *ws, *([acc] if acc is not None else []))


def project_in(name, h, ws):
    keys = list(ws)

    @jax.custom_vjp
    def op(h, ws):
        return {p: _mmc_fwd(f"{name}_{p}", h, ws[p][None])[0] for p in keys}

    def op_fwd(h, ws):
        return op(h, ws), (h, ws)

    def op_bwd(res, dys):
        h, ws = res
        groups, cols = [[]], 0
        for p in keys:
            if groups[-1] and cols + ws[p].shape[1] > PROJ_GROUP_COLS:
                groups.append([])
                cols = 0
            groups[-1].append(p)
            cols += ws[p].shape[1]
        dh = None
        for i, group in enumerate(groups):
            dh = _proj_dh(f"{name}_dh{i}", [dys[p] for p in group], [ws[p] for p in group], dh)
        return dh, {p: _mmc_dw(f"{name}_{p}_dw", h, dys[p][None])[0] for p in keys}

    op.defvjp(op_fwd, op_bwd)
    return op(h, ws)


def _ffn_up(name, h, wt):
    m, k = h.shape
    j, n, _ = wt.shape
    tm = MM_TM

    def body(h_ref, w_ref, o_ref):
        o_ref[0] = _dot_nt(h_ref[...], w_ref[0])

    return pl.pallas_call(
        body, name=name, grid=(m // tm, j),
        in_specs=[pl.BlockSpec((tm, k), lambda i, b: (i, 0)), pl.BlockSpec((1, n, k), lambda i, b: (b, 0, 0))],
        out_specs=pl.BlockSpec((1, tm, n), lambda i, b: (b, i, 0)),
        out_shape=jax.ShapeDtypeStruct((j, m, n), F32), compiler_params=_params("parallel", "parallel"),
    )(h, wt)


def _ffn_down(name, g, u, wd):
    j, m, n = g.shape
    d = wd.shape[2]
    tm = MM_TM

    def body(g_ref, u_ref, w_ref, o_ref):
        part = _dot(_silu(g_ref[0]) * u_ref[0], w_ref[0])

        @pl.when(pl.program_id(1) == 0)
        def _():
            o_ref[...] = part

        @pl.when(pl.program_id(1) > 0)
        def _():
            o_ref[...] += part

    act = pl.BlockSpec((1, tm, n), lambda i, b: (b, i, 0))
    return pl.pallas_call(
        body, name=name, grid=(m // tm, j),
        in_specs=[act, act, pl.BlockSpec((1, n, d), lambda i, b: (b, 0, 0))],
        out_specs=pl.BlockSpec((tm, d), lambda i, b: (i, 0)),
        out_shape=jax.ShapeDtypeStruct((m, d), F32), compiler_params=_params("parallel", "arbitrary"),
    )(g, u, wd)


def _ffn_down_bwd(name, dy, g, u, wd):
    j, m, n = g.shape
    d = wd.shape[2]
    tm = MM_TM

    def body(dy_ref, g_ref, u_ref, w_ref, dg_ref, du_ref):
        da = _dot_nt(dy_ref[...], w_ref[0])
        gv = g_ref[0]
        s = _sigmoid(gv)
        dg_ref[0] = da * u_ref[0] * (s * (1.0 + gv * (1.0 - s)))
        du_ref[0] = da * (gv * s)

    act = pl.BlockSpec((1, tm, n), lambda i, b: (b, i, 0))
    return pl.pallas_call(
        body, name=name, grid=(m // tm, j),
        in_specs=[pl.BlockSpec((tm, d), lambda i, b: (i, 0)), act, act, pl.BlockSpec((1, n, d), lambda i, b: (b, 0, 0))],
        out_specs=[act, act], out_shape=[jax.ShapeDtypeStruct((j, m, n), F32)] * 2,
        compiler_params=_params("parallel", "parallel"),
    )(dy, g, u, wd)


def _ffn_down_dw(name, g, u, dy):
    j, m, n = g.shape
    d = dy.shape[1]
    tn = _tile_of(d, 512)

    def body(g_ref, u_ref, dy_ref, o_ref):
        o_ref[0] = _dot_tn(_silu(g_ref[0]) * u_ref[0], dy_ref[...]).astype(BF16)

    act = pl.BlockSpec((1, m, n), lambda b, c: (b, 0, 0))
    return pl.pallas_call(
        body, name=name, grid=(j, d // tn),
        in_specs=[act, act, pl.BlockSpec((m, tn), lambda b, c: (0, c))],
        out_specs=pl.BlockSpec((1, n, tn), lambda b, c: (b, 0, c)),
        out_shape=jax.ShapeDtypeStruct((j, n, d), BF16), compiler_params=_params("parallel", "parallel"),
    )(g, u, dy)


def _ffn_up_dh(name, dg, du, wg, wu):
    j, m, n = dg.shape
    k = wg.shape[2]
    tm = MM_TM

    def body(dg_ref, du_ref, wg_ref, wu_ref, o_ref):
        part = _dot(dg_ref[0], wg_ref[0]) + _dot(du_ref[0], wu_ref[0])

        @pl.when(pl.program_id(1) == 0)
        def _():
            o_ref[...] = part

        @pl.when(pl.program_id(1) > 0)
        def _():
            o_ref[...] += part

    act = pl.BlockSpec((1, tm, n), lambda i, b: (b, i, 0))
    wsp = pl.BlockSpec((1, n, k), lambda i, b: (b, 0, 0))
    return pl.pallas_call(
        body, name=name, grid=(m // tm, j), in_specs=[act, act, wsp, wsp],
        out_specs=pl.BlockSpec((tm, k), lambda i, b: (i, 0)),
        out_shape=jax.ShapeDtypeStruct((m, k), F32), compiler_params=_params("parallel", "arbitrary"),
    )(dg, du, wg, wu)


def _ffn_up_dw(name, dy, h):
    j, m, n = dy.shape
    k = h.shape[1]
    tk = _tile_of(k, 512)

    def body(dy_ref, h_ref, o_ref):
        o_ref[0] = _dot_tn(dy_ref[0], h_ref[...]).astype(BF16)

    return pl.pallas_call(
        body, name=name, grid=(j, k // tk),
        in_specs=[pl.BlockSpec((1, m, n), lambda b, i: (b, 0, 0)), pl.BlockSpec((m, tk), lambda b, i: (0, i))],
        out_specs=pl.BlockSpec((1, n, tk), lambda b, i: (b, 0, i)),
        out_shape=jax.ShapeDtypeStruct((j, n, k), BF16), compiler_params=_params("parallel", "parallel"),
    )(dy, h)


def ffn(name, h, wg, wu, wd):
    @jax.custom_vjp
    def op(h, wg, wu, wd):
        return _ffn_down(name + "_d", _ffn_up(name + "_g", h, wg), _ffn_up(name + "_u", h, wu), wd)

    def op_fwd(h, wg, wu, wd):
        g, u = _ffn_up(name + "_g", h, wg), _ffn_up(name + "_u", h, wu)
        return _ffn_down(name + "_d", g, u, wd), (h, g, u, wg, wu, wd)

    def op_bwd(res, dy):
        h, g, u, wg, wu, wd = res
        dg, du = _ffn_down_bwd(name + "_d_bwd", dy, g, u, wd)
        return (_ffn_up_dh(name + "_dh", dg, du, wg, wu), _ffn_up_dw(name + "_g_dw", dg, h), _ffn_up_dw(name + "_u_dw", du, h),
                _ffn_down_dw(name + "_d_dw", g, u, dy))

    op.defvjp(op_fwd, op_bwd)
    return op(h, wg, wu, wd)


def _mmr_fwd(name, a, w):
    j, m, n = a.shape
    nn = w.shape[2]
    tm, tn = MM_TM, _tile_of(nn, 1024)

    def body(a_ref, w_ref, o_ref):
        part = _dot(a_ref[0], w_ref[0])

        @pl.when(pl.program_id(2) == 0)
        def _():
            o_ref[...] = part

        @pl.when(pl.program_id(2) > 0)
        def _():
            o_ref[...] += part

    return pl.pallas_call(
        body, name=name, grid=(m // tm, nn // tn, j),
        in_specs=[pl.BlockSpec((1, tm, n), lambda i, c, b: (b, i, 0)), pl.BlockSpec((1, n, tn), lambda i, c, b: (b, 0, c))],
        out_specs=pl.BlockSpec((tm, tn), lambda i, c, b: (i, c)),
        out_shape=jax.ShapeDtypeStruct((m, nn), F32),
        compiler_params=_params("parallel", "parallel", "arbitrary"),
    )(a, w)


def _mmr_da(name, dy, w):
    m, nn = dy.shape
    j, n, _ = w.shape
    tm = MM_TM

    def body(dy_ref, w_ref, o_ref):
        o_ref[0] = _dot_nt(dy_ref[...], w_ref[0])

    return pl.pallas_call(
        body, name=name, grid=(m // tm, j),
        in_specs=[pl.BlockSpec((tm, nn), lambda i, b: (i, 0)), pl.BlockSpec((1, n, nn), lambda i, b: (b, 0, 0))],
        out_specs=pl.BlockSpec((1, tm, n), lambda i, b: (b, i, 0)),
        out_shape=jax.ShapeDtypeStruct((j, m, n), F32),
        compiler_params=_params("parallel", "parallel"),
    )(dy, w)


def _mmr_dw(name, a, dy):
    j, m, n = a.shape
    nn = dy.shape[1]
    tn = _tile_of(nn, 512)

    def body(a_ref, dy_ref, o_ref):
        o_ref[0] = _dot_tn(a_ref[0], dy_ref[...]).astype(BF16)

    return pl.pallas_call(
        body, name=name, grid=(j, nn // tn),
        in_specs=[pl.BlockSpec((1, m, n), lambda b, c: (b, 0, 0)), pl.BlockSpec((m, tn), lambda b, c: (0, c))],
        out_specs=pl.BlockSpec((1, n, tn), lambda b, c: (b, 0, c)),
        out_shape=jax.ShapeDtypeStruct((j, n, nn), BF16),
        compiler_params=_params("parallel", "parallel"),
    )(a, dy)


def mm_rows(name, a, w):
    @jax.custom_vjp
    def op(a, w):
        return _mmr_fwd(name, a, w)

    def op_fwd(a, w):
        return op(a, w), (a, w)

    def op_bwd(res, dy):
        a, w = res
        return _mmr_da(name + "_da", dy, w), _mmr_dw(name + "_dw", a, dy)

    op.defvjp(op_fwd, op_bwd)
    return op(a, w)


def _colwise_specs(cols, pars, par_block):
    t = cols[0].shape[0]
    specs = [pl.BlockSpec((t, LANES), lambda j: (0, j)) for _ in cols]
    for p, blk in zip(pars, par_block):
        if blk == "lane":
            specs.append(pl.BlockSpec((p.shape[0], LANES), lambda j: (0, j)))
        else:
            specs.append(pl.BlockSpec((1,) + p.shape[1:], lambda j: (j, 0, 0)))
    return specs


def _colwise_fwd_call(name, f, cols, pars, par_block, n_out):
    t, c = cols[0].shape
    nc, npar = len(cols), len(pars)

    def body(*refs):
        ins = [r[...] for r in refs[:nc]] + [r[...] if b == "lane" else r[0] for r, b in zip(refs[nc:nc + npar], par_block)]
        res = f(*ins)
        for o_ref, o in zip(refs[nc + npar:], res):
            o_ref[...] = o

    return pl.pallas_call(
        body, name=name, grid=(c // LANES,),
        in_specs=_colwise_specs(cols, pars, par_block),
        out_specs=[pl.BlockSpec((t, LANES), lambda j: (0, j)) for _ in range(n_out)],
        out_shape=[jax.ShapeDtypeStruct((t, c), F32) for _ in range(n_out)],
        compiler_params=_params("parallel"),
    )(*cols, *pars)


def _colwise_bwd_call(name, f, cols, pars, par_block, cts):
    t, c = cols[0].shape
    nc, npar, nct = len(cols), len(pars), len(cts)

    def body(*refs):
        ins = [r[...] for r in refs[:nc]] + [r[...] if b == "lane" else r[0] for r, b in zip(refs[nc:nc + npar], par_block)]
        gs = tuple(r[...] for r in refs[nc + npar:nc + npar + nct])
        outs = refs[nc + npar + nct:]
        _, vjp = jax.vjp(f, *ins)
        d = vjp(gs)
        for o_ref, v in zip(outs[:nc], d[:nc]):
            o_ref[...] = v
        for o_ref, v, b in zip(outs[nc:], d[nc:], par_block):
            if b == "lane":
                o_ref[...] = v
            else:
                o_ref[0] = v

    res = pl.pallas_call(
        body, name=name, grid=(c // LANES,),
        in_specs=_colwise_specs(cols, pars, par_block) + [pl.BlockSpec((t, LANES), lambda j: (0, j)) for _ in cts],
        out_specs=_colwise_specs(cols, pars, par_block),
        out_shape=[jax.ShapeDtypeStruct(v.shape, F32) for v in (*cols, *pars)],
        compiler_params=_params("parallel"),
    )(*cols, *pars, *cts)
    return tuple(res[:nc]), tuple(res[nc:])


def colwise(name, f, cols, pars, par_block, n_out):
    @jax.custom_vjp
    def op(cols, pars):
        return tuple(_colwise_fwd_call(name, f, cols, pars, par_block, n_out))

    def op_fwd(cols, pars):
        return op(cols, pars), (cols, pars)

    def op_bwd(res, cts):
        return _colwise_bwd_call(name + "_bwd", f, res[0], res[1], par_block, tuple(cts))

    op.defvjp(op_fwd, op_bwd)
    return op(tuple(cols), tuple(pars))


def _rg_block(x, gate, cw, cb, wr, br, wi, bi, lam):
    xa = _causal_conv(x, cw) + cb
    r = _sigmoid(_dot(xa, wr) + br)
    i = _sigmoid(_dot(xa, wi) + bi)
    log_a = -RG_C * r * _softplus(-lam)
    a = jnp.exp(log_a)
    b = jnp.sqrt(_neg_expm1(2.0 * log_a)) * (i * xa)
    return (_lin_scan(a, b) * _gelu(gate),)


def _dn_conv_block(mode):
    def f(x, cw):
        c = _silu(_causal_conv(x, cw))
        if mode == "v":
            return (c,)
        c = c * lax.rsqrt(jnp.sum(c * c, axis=-1, keepdims=True) + EPS)
        return (c * (DN_HEAD_DIM ** -0.5),) if mode == "q" else (c,)
    return f


def _block_diag(w):
    w = w.reshape(8, 2, 64, 64)
    z = jnp.zeros((8, 64, 64), w.dtype)
    top = jnp.concatenate([w[:, 0], z], axis=2)
    bot = jnp.concatenate([z, w[:, 1]], axis=2)
    return jnp.concatenate([top, bot], axis=1)


DN_HP = 8


def _dn_block(S, qw, kw, vw, gb, h0):
    hp, hd = S.shape[0], DN_HEAD_DIM
    heads = lambda a: jnp.concatenate([a[None, :, j * hd:(j + 1) * hd] for j in range(hp)], axis=0)
    lane = lax.broadcasted_iota(jnp.int32, gb.shape, 1)
    col = lambda i: jnp.sum(jnp.where(lane == i, gb, 0.0), axis=1, keepdims=True)[None]
    beta = jnp.concatenate([col(h0 + j) for j in range(hp)], axis=0)
    g = jnp.concatenate([col(h0 + j + DN_HEADS) for j in range(hp)], axis=0)
    s_new, o = _dn_step(S, heads(qw), heads(kw), heads(vw), beta, g)
    return s_new, jnp.concatenate([o[j:j + 1].reshape(o.shape[1:]) for j in range(hp)], axis=1)


def _dn_step(S, q, k, v, beta, g):
    c = DN_CHUNK
    ri = lax.broadcasted_iota(jnp.int32, (c, c), 0)
    ci = lax.broadcasted_iota(jnp.int32, (c, c), 1)
    incl, strict = ri >= ci, ri > ci
    eye = (ri == ci).astype(F32)
    gam = _cumsum_rows(g)
    gam_row = jnp.sum(jnp.where(ri <= ci, g, 0.0), axis=-2, keepdims=True)
    gam_last = jnp.sum(g, axis=-2, keepdims=True)
    decay = jnp.where(incl, jnp.exp(jnp.where(incl, gam - gam_row, 0.0)), 0.0)
    kb = k * beta
    vb = v * beta
    a = jnp.where(strict, _dot_nt(kb, k) * decay, 0.0)
    p = -a
    tinv = eye + p
    for _ in range(5):
        p = _dot3(p, p)
        tinv = tinv + _dot3(tinv, p)
    e_gam = jnp.exp(gam)
    u0 = _dot3(tinv, vb)
    wk = _dot3(tinv, kb * e_gam)
    qk = jnp.where(incl, _dot_nt(q, k) * decay, 0.0)
    q_dec = q * e_gam
    k_dec = k * jnp.exp(gam_last - gam)
    u = u0 - _dot(wk, S)
    o = _dot(q_dec, S) + _dot(qk, u)
    s_new = S * jnp.exp(gam_last) + _dot_tn(k_dec, u)
    return s_new, o


def _dn_fwd_call(q, k, v, gb):
    t, w = q.shape
    n, hp, hd, c = t // DN_CHUNK, DN_HP, DN_HEAD_DIM, DN_CHUNK

    def body(q_ref, k_ref, v_ref, gb_ref, o_ref, s0_ref, s_scr):
        @pl.when(pl.program_id(1) == 0)
        def _():
            s_scr[...] = jnp.zeros_like(s_scr)

        s_old = s_scr[...]
        s0_ref[:, 0] = s_old
        s_new, o = _dn_block(s_old, q_ref[...], k_ref[...], v_ref[...], gb_ref[...], pl.program_id(0) * hp)
        o_ref[...] = o
        s_scr[...] = s_new

    blk = pl.BlockSpec((c, hp * hd), lambda g, i: (i, g))
    return pl.pallas_call(
        body, name="dn_core", grid=(DN_HEADS // hp, n),
        in_specs=[blk, blk, blk, pl.BlockSpec((c, LANES), lambda g, i: (i, 0))],
        out_specs=[blk, pl.BlockSpec((hp, 1, hd, hd), lambda g, i: (g, i, 0, 0))],
        out_shape=[jax.ShapeDtypeStruct((t, w), F32), jax.ShapeDtypeStruct((DN_HEADS, n, hd, hd), F32)],
        scratch_shapes=[pltpu.VMEM((hp, hd, hd), F32)],
        compiler_params=_params("parallel", "arbitrary"),
    )(q, k, v, gb)


def _dn_bwd_call(q, k, v, gb, s0, do):
    t, w = q.shape
    n, hp, hd, c = t // DN_CHUNK, DN_HP, DN_HEAD_DIM, DN_CHUNK
    ng = DN_HEADS // hp

    def body(q_ref, k_ref, v_ref, gb_ref, s0_ref, do_ref, dq_ref, dk_ref, dv_ref, dgb_ref, ds_scr):
        @pl.when(pl.program_id(1) == 0)
        def _():
            ds_scr[...] = jnp.zeros_like(ds_scr)

        h0 = pl.program_id(0) * hp
        _, vjp = jax.vjp(lambda *a: _dn_block(*a, h0), s0_ref[:, 0], q_ref[...], k_ref[...], v_ref[...], gb_ref[...])
        ds, dq, dk, dv, dgb = vjp((ds_scr[...], do_ref[...]))
        ds_scr[...] = ds
        dq_ref[...], dk_ref[...], dv_ref[...] = dq, dk, dv
        dgb_ref[0] = dgb

    blk = pl.BlockSpec((c, hp * hd), lambda g, i: (n - 1 - i, g))
    res = pl.pallas_call(
        body, name="dn_core_bwd", grid=(ng, n),
        in_specs=[blk, blk, blk, pl.BlockSpec((c, LANES), lambda g, i: (n - 1 - i, 0)),
                  pl.BlockSpec((hp, 1, hd, hd), lambda g, i: (g, n - 1 - i, 0, 0)), blk],
        out_specs=[blk, blk, blk, pl.BlockSpec((1, c, LANES), lambda g, i: (g, n - 1 - i, 0))],
        out_shape=[jax.ShapeDtypeStruct((t, w), F32)] * 3 + [jax.ShapeDtypeStruct((ng, t, LANES), F32)],
        scratch_shapes=[pltpu.VMEM((hp, hd, hd), F32)],
        compiler_params=_params("parallel", "arbitrary"),
    )(q, k, v, gb, s0, do)
    return res[0], res[1], res[2], jnp.sum(res[3], axis=0)


@jax.custom_vjp
def dn_core(q, k, v, gb):
    return _dn_fwd_call(q, k, v, gb)[0]


def _dn_core_fwd(q, k, v, gb):
    o, s0 = _dn_fwd_call(q, k, v, gb)
    return o, (q, k, v, gb, s0)


def _dn_core_bwd(res, do):
    return _dn_bwd_call(*res, do)


dn_core.defvjp(_dn_core_fwd, _dn_core_bwd)


ATT_GH = 4


def _att_block(q, kp, kc, vp, vc, qn, kn, slope, has_prev, dil):
    s = ATT_SPAN
    qh = _rms(q, qn) * (ATT_HEAD_DIM ** -0.5)
    qi = lax.broadcasted_iota(jnp.int32, (s, s), 0)
    kj = lax.broadcasted_iota(jnp.int32, (s, s), 1)
    d_p = qi + s - kj
    d_c = qi - kj
    s_p = _dot_nt(qh, _rms(kp, kn)) - slope * (d_p * dil).astype(F32)
    s_c = _dot_nt(qh, _rms(kc, kn)) - slope * (d_c * dil).astype(F32)
    s_p = jnp.where((d_p <= s) & (has_prev > 0), s_p, NEG_INF)
    s_c = jnp.where(d_c >= 0, s_c, NEG_INF)
    m = lax.stop_gradient(jnp.maximum(jnp.max(s_p, axis=-1, keepdims=True), jnp.max(s_c, axis=-1, keepdims=True)))
    p_p = jnp.exp(s_p - m)
    p_c = jnp.exp(s_c - m)
    den = jnp.sum(p_p, axis=-1, keepdims=True) + jnp.sum(p_c, axis=-1, keepdims=True)
    o = _dot(p_p / den, vp) + _dot(p_c / den, vc)
    lse = m + jnp.log(den)
    return o, jnp.broadcast_to(lse, o.shape)


def _att_heads(a):
    e = ATT_HEAD_DIM
    return jnp.concatenate([a[None, :, h * e:(h + 1) * e] for h in range(ATT_GH)], axis=0)


def _att_lanes(a):
    return jnp.concatenate([a[h:h + 1].reshape(a.shape[1:]) for h in range(ATT_GH)], axis=1)


def _att_rows(q, kp, kc, vp, vc, qn, kn, group, has_prev, dil):
    head = lax.broadcasted_iota(jnp.int32, (ATT_GH, 1, 1), 0) + (ATT_GH * group + 1)
    slope = jnp.exp(head.astype(F32) * (-8.0 / ATT_HEADS * math.log(2.0)))
    o, lse = _att_block(_att_heads(q), _att_heads(kp), _att_heads(kc), _att_heads(vp), _att_heads(vc), qn, kn, slope, has_prev, dil)
    return _att_lanes(o), _att_lanes(lse)


def _att_specs(group, dil):
    blk = (ATT_SPAN, ATT_GH * ATT_HEAD_DIM)
    cur = lambda which: pl.BlockSpec(blk, lambda r, n: (n, r * 9 + 3 * which + group))
    prev = lambda which: pl.BlockSpec(blk, lambda r, n: (jnp.maximum(n - 1, 0), r * 9 + 3 * which + group))
    out = pl.BlockSpec(blk, lambda r, n: (n, r))
    gain = pl.BlockSpec((ATT_GH, 1, ATT_HEAD_DIM), lambda r, n: (0, 0, 0))
    return [cur(0), prev(1), cur(1), prev(2), cur(2), gain, gain], out, gain


def _att_fwd_call(name, group, dil, pa, qn, kn):
    t = pa.shape[0]
    l = t // dil
    w = ATT_GH * ATT_HEAD_DIM
    ins, out, _ = _att_specs(group, dil)
    pav = pa.reshape(l, dil * pa.shape[1])

    def body(q_ref, kp_ref, kc_ref, vp_ref, vc_ref, qn_ref, kn_ref, o_ref, lse_ref):
        o_ref[...], lse_ref[...] = _att_rows(q_ref[...], kp_ref[...], kc_ref[...], vp_ref[...], vc_ref[...], qn_ref[...],
                                             kn_ref[...], group, pl.program_id(1), dil)

    o, lse = pl.pallas_call(
        body, name=name, grid=(dil, l // ATT_SPAN), in_specs=ins, out_specs=[out, out],
        out_shape=[jax.ShapeDtypeStruct((l, dil * w), F32)] * 2, compiler_params=_params("parallel", "arbitrary"),
    )(pav, pav, pav, pav, pav, qn, kn)
    return o.reshape(t, w), lse.reshape(t, w)


def _att_bwd_call(name, group, dil, pa, qn, kn, do, dlse):
    t = pa.shape[0]
    l = t // dil
    w = ATT_GH * ATT_HEAD_DIM
    ins, out, gain = _att_specs(group, dil)
    pav = pa.reshape(l, dil * pa.shape[1])

    def body(q_ref, kp_ref, kc_ref, vp_ref, vc_ref, qn_ref, kn_ref, do_ref, dlse_ref,
             dq_ref, dkp_ref, dkc_ref, dvp_ref, dvc_ref, dqn_ref, dkn_ref):
        has_prev = pl.program_id(1)
        _, vjp = jax.vjp(lambda *a: _att_rows(*a, group, has_prev, dil), q_ref[...], kp_ref[...], kc_ref[...], vp_ref[...],
                         vc_ref[...], qn_ref[...], kn_ref[...])
        dq, dkp, dkc, dvp, dvc, dqn, dkn = vjp((do_ref[...], dlse_ref[...]))
        dq_ref[...], dkp_ref[...], dkc_ref[...], dvp_ref[...], dvc_ref[...] = dq, dkp, dkc, dvp, dvc

        @pl.when((pl.program_id(0) == 0) & (pl.program_id(1) == 0))
        def _():
            dqn_ref[...] = jnp.zeros_like(dqn_ref)
            dkn_ref[...] = jnp.zeros_like(dkn_ref)

        dqn_ref[...] += dqn
        dkn_ref[...] += dkn

    res = pl.pallas_call(
        body, name=name + "_bwd", grid=(dil, l // ATT_SPAN), in_specs=ins + [out, out],
        out_specs=[out] * 5 + [gain, gain],
        out_shape=[jax.ShapeDtypeStruct((l, dil * w), F32)] * 5 + [jax.ShapeDtypeStruct(qn.shape, F32)] * 2,
        compiler_params=_params("arbitrary", "arbitrary"),
    )(pav, pav, pav, pav, pav, qn, kn, do.reshape(l, dil * w), dlse.reshape(l, dil * w))
    dq, dkp, dkc, dvp, dvc, dqn, dkn = res
    back = lambda g: jnp.pad(g[ATT_SPAN:], ((0, ATT_SPAN), (0, 0)))
    return dq.reshape(t, w), (dkc + back(dkp)).reshape(t, w), (dvc + back(dvp)).reshape(t, w), dqn, dkn


def _att_mix(o1, o2, o3, l1, l2, l3):
    m = jnp.maximum(jnp.maximum(l1, l2), l3)
    e1, e2, e3 = jnp.exp(l1 - m), jnp.exp(l2 - m), jnp.exp(l3 - m)
    s = e1 + e2 + e3
    return (jnp.concatenate([o1 * (e1 / s), o2 * (e2 / s), o3 * (e3 / s)], axis=1),)


def att_branch(name, pa, qn, kn):
    e = ATT_HEAD_DIM
    gains = lambda p, g: p[ATT_GH * g:ATT_GH * (g + 1)].reshape(ATT_GH, 1, e)

    @jax.custom_vjp
    def groups(pa, qn, kn):
        res = [_att_fwd_call(f"{name}_att{g}", g, dil, pa, gains(qn, g), gains(kn, g)) for g, (_, dil) in enumerate(ATT_GROUPS)]
        return tuple(r[0] for r in res) + tuple(r[1] for r in res)

    def groups_fwd(pa, qn, kn):
        return groups(pa, qn, kn), (pa, qn, kn)

    def groups_bwd(res, cts):
        pa, qn, kn = res
        n = len(ATT_GROUPS)
        parts = [_att_bwd_call(f"{name}_att{g}", g, dil, pa, gains(qn, g), gains(kn, g), cts[g], cts[n + g])
                 for g, (_, dil) in enumerate(ATT_GROUPS)]
        d_pa = jnp.concatenate([p[i] for i in range(3) for p in parts], axis=1)
        return (d_pa, jnp.concatenate([p[3] for p in parts]).reshape(qn.shape), jnp.concatenate([p[4] for p in parts]).reshape(kn.shape))

    groups.defvjp(groups_fwd, groups_bwd)
    return rowwise(f"{name}_attmix", _att_mix, groups(pa, qn, kn))[0]


def dn_gates(name, ba, a_log, dt_bias):
    place = lambda p: jnp.pad(p.reshape(1, DN_HEADS), ((0, 0), (DN_HEADS, LANES - 2 * DN_HEADS)))

    def f(x, al, dt):
        lane = lax.broadcasted_iota(jnp.int32, x.shape, 1)
        return (jnp.where(lane < DN_HEADS, _sigmoid(x), -jnp.exp(al) * _softplus(x + dt)),)

    return rowwise(name, f, (ba,), (place(a_log), place(dt_bias)))[0]


def _dn_out(o, z, g):
    parts = []
    for h in range(DN_HEADS):
        sl = slice(h * DN_HEAD_DIM, (h + 1) * DN_HEAD_DIM)
        parts.append(_rms(o[:, sl], g[:, sl]) * _silu(z[:, sl]))
    return (jnp.concatenate(parts, axis=1),)


def _merge(ml, za, zb, zc):
    d = D_MODEL
    return (_sigmoid(ml[:, :d]) * za + _sigmoid(ml[:, d:2 * d]) * zb + _sigmoid(ml[:, 2 * d:]) * zc,)


def add_norm(name, x, pend, scale, gain):
    if pend is None:
        return x, rowwise(name, lambda a, g: (_rms(a, g),), (x,), (gain,))[0]

    def f(a, b, g):
        s = a + scale * b
        return s, _rms(s, g)

    return rowwise(name, f, (x, pend), (gain,))


W_IN_PIECES = (("rgx", 0, 1024), ("gate", 1024, 1024), ("att", 2048, 2304), ("dq", 4352, 1024), ("dk", 5376, 1024),
               ("dv", 6400, 1024), ("dz", 7424, 1024), ("ba", 8448, 16), ("mrg", 8464, 3072))
RG_PAR_BLOCKS = ("lane", "lane", "blk", "lane", "blk", "lane", "lane")


def mixer(name, u, w, p):
    mm = lambda nm, a, wt: mm_rows(nm, a[None], wt[None])
    pr = project_in(name + "_in", u, {k: w["in_" + k] for k, _, _ in W_IN_PIECES})
    ya = colwise(name + "_rg", _rg_block, (pr["rgx"], pr["gate"]),
                 (w["rg_conv_w"], p["rg_conv_b"], _block_diag(p["rg_w_r"]), p["rg_b_r"], _block_diag(p["rg_w_i"]),
                  p["rg_b_i"], p["rg_lambda"]), RG_PAR_BLOCKS, 1)[0]
    yb = att_branch(name, pr["att"], p["att_q_norm"], p["att_k_norm"])
    cw = w["dn_conv_w"]
    cq = colwise(name + "_dnq", _dn_conv_block("q"), (pr["dq"],), (cw[:, :1024],), ("lane",), 1)[0]
    ck = colwise(name + "_dnk", _dn_conv_block("k"), (pr["dk"],), (cw[:, 1024:2048],), ("lane",), 1)[0]
    cv = colwise(name + "_dnv", _dn_conv_block("v"), (pr["dv"],), (cw[:, 2048:],), ("lane",), 1)[0]
    gb = dn_gates(name + "_dngate", pr["ba"], p["dn_a_log"], p["dn_dt_bias"])
    o_dn = dn_core(cq, ck, cv, gb)
    yc = rowwise(name + "_dnout", _dn_out, (o_dn, pr["dz"]), (p["dn_out_norm"].reshape(1, D_MODEL),))[0]
    y = rowwise(name + "_merge", _merge, (pr["mrg"], mm(name + "_ba", ya, w["br_a"]), mm(name + "_bb", yb, w["br_b"]),
                                          mm(name + "_bc", yc, w["br_c"])))[0]
    return mm(name + "_out", y, w["w_out"])


def _loss_call(x, pend, target):
    t, d = x.shape
    tile = _row_tile(t)

    def body(x_ref, p_ref, t_ref, loss_ref, g_ref):
        err = x_ref[...] + 0.5 * p_ref[...] - t_ref[...]
        g_ref[...] = err * (1.0 / d)

        @pl.when(pl.program_id(0) == 0)
        def _():
            loss_ref[...] = jnp.zeros_like(loss_ref)

        loss_ref[...] += jnp.full(loss_ref.shape, 0.5 / d, F32) * jnp.sum(err * err)

    blk = pl.BlockSpec((tile, d), lambda i: (i, 0))
    loss, g = pl.pallas_call(
        body, name="loss", grid=(t // tile,), in_specs=[blk, blk, blk],
        out_specs=[pl.BlockSpec((8, LANES), lambda i: (0, 0)), blk],
        out_shape=[jax.ShapeDtypeStruct((8, LANES), F32), jax.ShapeDtypeStruct((t, d), F32)],
        compiler_params=_params("arbitrary"),
    )(x, pend, target)
    return loss[0, 0], g


@jax.custom_vjp
def loss_op(x, pend, target):
    return _loss_call(x, pend, target)[0]


def _loss_fwd(x, pend, target):
    loss, g = _loss_call(x, pend, target)
    return loss, g


def _loss_bwd(g, ct):
    return ct * g, (0.5 * ct) * g, None


loss_op.defvjp(_loss_fwd, _loss_bwd)


def first_ffn(wg, wu, wd, gain, x):
    x, h = add_norm("L0_n1", x, None, 0.0, gain)
    return x, ffn("L0_f1", h, wg, wu, wd)


def rest_of_step(g, conv, p, x, pend, target):
    scale = 0.5
    w = [split_layer({n: g[n, l] for n, _ in MATRICES if (n, l) in g}, {n: conv[n][l] for n, _ in CONVS}) for l in range(len(p))]
    for l in range(len(p)):
        n = f"L{l}"
        if l > 0:
            x, h = add_norm(n + "_n1", x, pend, scale, p[l]["ffn1_norm"])
            pend, scale = ffn(n + "_f1", h, w[l]["ffn1_w_gate"], w[l]["ffn1_w_up"], w[l]["ffn1_w_down"]), 0.5
        x, h = add_norm(n + "_nm", x, pend, scale, p[l]["mix_norm"])
        pend, scale = mixer(n + "_mx", h, w[l], p[l]), 1.0
        x, h = add_norm(n + "_n2", x, pend, scale, p[l]["ffn2_norm"])
        pend, scale = ffn(n + "_f2", h, w[l]["ffn2_w_gate"], w[l]["ffn2_w_up"], w[l]["ffn2_w_down"]), 0.5
    return loss_op(x, pend, target)


WEIGHT_NAMES = ("ffn1_norm", "ffn1_w_gate", "ffn1_w_up", "ffn1_w_down", "mix_norm", "w_in", "rg_conv_w", "rg_conv_b",
                "rg_w_r", "rg_b_r", "rg_w_i", "rg_b_i", "rg_lambda", "att_q_norm", "att_k_norm", "dn_conv_w", "dn_a_log",
                "dn_dt_bias", "dn_out_norm", "w_branch", "w_out", "ffn2_norm", "ffn2_w_gate", "ffn2_w_up", "ffn2_w_down")
MATRICES = (("ffn1_w_gate", 2), ("ffn1_w_up", 2), ("ffn1_w_down", 1), ("w_in", 2), ("w_branch", 1), ("w_out", 1),
            ("ffn2_w_gate", 2), ("ffn2_w_up", 2), ("ffn2_w_down", 1))
CONVS = (("rg_conv_w", 2), ("dn_conv_w", 2))
SHARD_AXIS = dict(MATRICES + CONVS)
SMALL_NAMES = tuple(n for n in WEIGHT_NAMES if n not in SHARD_AXIS)
ROW_PARAMS = ("ffn1_norm", "mix_norm", "rg_conv_b", "rg_b_r", "rg_b_i", "rg_lambda", "ffn2_norm")
FFN_MATS = ("ffn1_w_gate", "ffn1_w_up", "ffn1_w_down", "ffn2_w_gate", "ffn2_w_up", "ffn2_w_down")
TRANSPOSED_MATS = ("ffn1_w_gate", "ffn1_w_up", "ffn2_w_gate", "ffn2_w_up")
W_IN_SHARD = 2884
GATHER_ORDER = ((("ffn1_w_gate", 0), ("ffn1_w_up", 0), ("ffn1_w_down", 0)),
                (("w_in", 0), ("w_branch", 0), ("w_out", 0)),
                None)
GATHER_IDS = (1, 6, 7)
LATE_MATS = ("ffn2_w_gate", "ffn2_w_up", "ffn2_w_down", "w_out", "w_branch")
EXCHANGE_GROUPS = (lambda n, l: l == 1 and n in LATE_MATS,
                   lambda n, l: (l == 1) != (n in LATE_MATS),
                   lambda n, l: l == 0 and n == "w_in",
                   lambda n, l: l == 0 and n not in LATE_MATS and n != "w_in")


def _shard_minor(a, axis):
    a = jnp.moveaxis(a, 0, axis)
    return a.reshape(a.shape[:axis] + (N_CHIPS * a.shape[axis + 1],) + a.shape[axis + 2:])


def _w_in_piece(g, off, n):
    s = W_IN_SHARD
    parts = [g[j][:, max(off, j * s) - j * s:min(off + n, (j + 1) * s) - j * s]
             for j in range(N_CHIPS) if max(off, j * s) < min(off + n, (j + 1) * s)]
    return jnp.concatenate(parts, axis=1) if len(parts) > 1 else parts[0]


def _w_in_chip_grad(gl, j):
    s = W_IN_SHARD
    parts = [gl["in_" + k][:, max(off, j * s) - off:min(off + n, (j + 1) * s) - off]
             for k, off, n in W_IN_PIECES if max(off, j * s) < min(off + n, (j + 1) * s)]
    return jnp.concatenate(parts, axis=1)


def _layer_weights(g, conv):
    w = {n: g[n] for n in FFN_MATS if n in g}
    w["w_out"] = g["w_out"].reshape(D_MODEL, D_MODEL)
    for k, off, n in W_IN_PIECES:
        piece = _w_in_piece(g["w_in"], off, n)
        w["in_" + k] = jnp.pad(piece, ((0, 0), (0, LANES - n))) if n < LANES else piece
    wb = g["w_branch"].reshape(-1, D_MODEL)
    w["br_a"], w["br_b"], w["br_c"] = wb[:1024], wb[1024:1792], wb[1792:]
    return dict(w, **conv)


def _layer_weight_grads(gl):
    out = {n: gl[n] for n in FFN_MATS if n in gl}
    out["w_out"] = gl["w_out"].reshape(N_CHIPS, -1, D_MODEL)
    out["w_branch"] = jnp.concatenate([gl["br_a"], gl["br_b"], gl["br_c"]], axis=0).reshape(N_CHIPS, -1, D_MODEL)
    out["w_in"] = jnp.stack([_w_in_chip_grad(gl, j) for j in range(N_CHIPS)])
    return out, {n: gl[n] for n, _ in CONVS}


@jax.custom_vjp
def split_layer(g, conv):
    return _layer_weights(g, conv)


split_layer.defvjp(lambda g, conv: (_layer_weights(g, conv), None), lambda _, gw: _layer_weight_grads(gw))


def layer_small(small, l):
    p = {n: small[n][l] for n in SMALL_NAMES}
    for n in ROW_PARAMS:
        p[n] = small[n][l:l + 1]
    return p


def layer_small_grads(gp, small):
    return {n: jnp.stack([g[n] for g in gp]).reshape(small[n].shape) for n in SMALL_NAMES}


HBM_SPEC = pl.BlockSpec(memory_space=pl.ANY)


def _place():
    x, y, c = lax.axis_index("x"), lax.axis_index("y"), lax.axis_index("c")
    other_chips = [(1 - x, y), (x, 1 - y), (1 - x, 1 - y)]
    return x, y, c, 2 * x + y, (x, y, 1 - c), other_chips


def _half_rows(ref, lead, hc):
    hr = ref.shape[-2] // 2
    return ref.at[(*lead, pl.ds(pl.multiple_of(hc * hr, 16), hr), slice(None))]


def _chip_index():
    return (2 * lax.axis_index("x") + lax.axis_index("y")).astype(jnp.int32).reshape(1)


def cast_into_blocks(name, w):
    l, rows, cols = w.shape
    tr = rows // 2

    def body(me_ref, w_ref, *o_refs):
        for a, o_ref in enumerate(o_refs):
            o_ref[...] = w_ref[a:a + 1].astype(BF16)

    return pl.pallas_call(
        body, name=name, out_shape=[jax.ShapeDtypeStruct((N_CHIPS, rows, cols), BF16)] * l,
        grid_spec=pltpu.PrefetchScalarGridSpec(
            num_scalar_prefetch=1, grid=(rows // tr,),
            in_specs=[pl.BlockSpec((l, tr, cols), lambda i, me: (0, i, 0))],
            out_specs=[pl.BlockSpec((1, tr, cols), lambda i, me: (me[0], i, 0))] * l),
        compiler_params=_params("parallel"),
    )(_chip_index(), w)


def _gather_blocks(bufs_in, bufs_out, send_sems, recv_sems):
    n = len(bufs_in)
    x, y, c, me, sibling, chips = _place()

    def copy(s, src, dst, to):
        return pltpu.make_async_remote_copy(src_ref=src, dst_ref=dst, send_sem=send_sems.at[s], recv_sem=recv_sems.at[s],
                                            device_id=to, device_id_type=MESH)

    first, passed = [], []
    for j, (cx, cy) in enumerate(chips):
        for i in range(n):
            cp = copy(6 * i + j, _half_rows(bufs_in[i], (me,), c), _half_rows(bufs_out[i], (me,), c), (cx, cy, c))
            cp.start()
            first.append(cp)
    for j, (cx, cy) in enumerate(chips):
        k = 2 * cx + cy
        for i in range(n):
            copy(6 * i + j, _half_rows(bufs_in[i], (me,), c), _half_rows(bufs_out[i], (k,), c), (cx, cy, c)).wait_recv()
            cp = copy(6 * i + 3 + j, _half_rows(bufs_out[i], (k,), c), _half_rows(bufs_out[i], (k,), c), sibling)
            cp.start()
            passed.append(cp)
    for j, (cx, cy) in enumerate(chips):
        k = 2 * cx + cy
        for i in range(n):
            copy(6 * i + 3 + j, _half_rows(bufs_in[i], (me,), c), _half_rows(bufs_out[i], (k,), 1 - c), sibling).wait_recv()
    for cp in first + passed:
        cp.wait_send()


def _handshake(peers):
    barrier = pltpu.get_barrier_semaphore()
    for p in peers:
        pl.semaphore_signal(barrier, inc=1, device_id=p, device_id_type=MESH)
    pl.semaphore_wait(barrier, len(peers))


def allgather_blocks_sc(name, bufs, collective_id):
    n = len(bufs)
    refs = [jax.new_ref(b, memory_space=pltpu.MemorySpace.HBM) for b in bufs]

    @pl.kernel(mesh=plsc.ScalarSubcoreMesh(axis_name="sequencer", num_cores=1), name=name,
               scratch_types=(pltpu.SemaphoreType.DMA((6 * n,)), pltpu.SemaphoreType.DMA((6 * n,))),
               compiler_params=pltpu.CompilerParams(collective_id=collective_id))
    def launch(send_sems, recv_sems):
        x, y, c, me, sibling, chips = _place()
        _handshake([(cx, cy, c) for cx, cy in chips] + [sibling])
        _gather_blocks(refs, refs, send_sems, recv_sems)

    launch()
    return [jax.freeze(r) for r in refs]


PEER_FLIPS = tuple((fx, fy, fc) for fx in (0, 1) for fy in (0, 1) for fc in (0, 1))[1:]


def exchange_pieces_sc(name, gs, collective_id):
    n = len(gs)

    def body(*refs):
        ins, outs = refs[:n], refs[n:2 * n]
        send_sems, recv_sems = refs[2 * n:]
        x, y, c, me, sibling, chips = _place()
        my_dev = 4 * x + 2 * y + c
        flip = lambda v, f: 1 - v if f else v
        peers = [(flip(x, fx), flip(y, fy), flip(c, fc)) for fx, fy, fc in PEER_FLIPS]
        _handshake(peers)
        sends = []
        for r, (px, py, pc) in enumerate(peers):
            for i in range(n):
                cp = pltpu.make_async_remote_copy(
                    src_ref=_half_rows(ins[i], (2 * px + py,), pc), dst_ref=outs[i].at[my_dev], send_sem=send_sems.at[7 * i + r],
                    recv_sem=recv_sems.at[7 * i + r], device_id=(px, py, pc), device_id_type=MESH)
                cp.start()
                sends.append(cp)
        for r, (px, py, pc) in enumerate(peers):
            for i in range(n):
                pltpu.make_async_remote_copy(
                    src_ref=_half_rows(ins[i], (me,), c), dst_ref=outs[i].at[4 * px + 2 * py + pc], send_sem=send_sems.at[7 * i + r],
                    recv_sem=recv_sems.at[7 * i + r], device_id=(px, py, pc), device_id_type=MESH).wait_recv()
        for cp in sends:
            cp.wait_send()

    return pl.kernel(
        body, name=name, mesh=plsc.ScalarSubcoreMesh(axis_name="sequencer", num_cores=1),
        out_type=[jax.ShapeDtypeStruct((N_DEV, g.shape[1] // 2, g.shape[2]), g.dtype) for g in gs],
        scratch_types=[pltpu.SemaphoreType.DMA((7 * n,)), pltpu.SemaphoreType.DMA((7 * n,))],
        compiler_params=pltpu.CompilerParams(collective_id=collective_id),
    )(*gs)


def sibling_share_halves(name, fs):
    n = len(fs)
    every = (slice(None),)

    def body(*refs):
        ins, outs = refs[:n], refs[n:2 * n]
        send_sems, recv_sems = refs[2 * n:]
        x, y, c, me, sibling, chips = _place()
        sends = []
        for i in range(n):
            cp = pltpu.make_async_remote_copy(src_ref=_half_rows(ins[i], every, c), dst_ref=_half_rows(outs[i], every, c),
                                              send_sem=send_sems.at[i], recv_sem=recv_sems.at[i], device_id=sibling, device_id_type=MESH)
            cp.start()
            sends.append(cp)
        for i in range(n):
            pltpu.make_async_remote_copy(src_ref=_half_rows(ins[i], every, c), dst_ref=_half_rows(outs[i], every, 1 - c),
                                         send_sem=send_sems.at[i], recv_sem=recv_sems.at[i], device_id=sibling,
                                         device_id_type=MESH).wait_recv()
        for cp in sends:
            cp.wait_send()

    return pl.pallas_call(
        body, name=name, out_shape=[jax.ShapeDtypeStruct(f.shape, f.dtype) for f in fs],
        in_specs=[HBM_SPEC] * n, out_specs=[HBM_SPEC] * n, input_output_aliases={i: i for i in range(n)},
        scratch_shapes=[pltpu.SemaphoreType.DMA((n,)), pltpu.SemaphoreType.DMA((n,))],
    )(*fs)


def allgather_small_sc(name, v, collective_id):
    def body(v_ref, out_ref, send_sems, recv_sems, local_sem):
        x, y, c, me, sibling, chips = _place()
        my_dev = 4 * x + 2 * y + c
        flip = lambda a, f: 1 - a if f else a
        peers = [(flip(x, fx), flip(y, fy), flip(c, fc)) for fx, fy, fc in PEER_FLIPS]
        _handshake(peers)
        mine = pltpu.make_async_copy(v_ref, out_ref.at[my_dev], local_sem)
        mine.start()
        sends = []
        for r, peer in enumerate(peers):
            cp = pltpu.make_async_remote_copy(src_ref=v_ref, dst_ref=out_ref.at[my_dev], send_sem=send_sems.at[r],
                                              recv_sem=recv_sems.at[r], device_id=peer, device_id_type=MESH)
            cp.start()
            sends.append(cp)
        for r, (px, py, pc) in enumerate(peers):
            pltpu.make_async_remote_copy(src_ref=v_ref, dst_ref=out_ref.at[4 * px + 2 * py + pc], send_sem=send_sems.at[r],
                                         recv_sem=recv_sems.at[r], device_id=(px, py, pc), device_id_type=MESH).wait_recv()
        for cp in sends:
            cp.wait_send()
        mine.wait()

    return pl.kernel(
        body, name=name, mesh=plsc.ScalarSubcoreMesh(axis_name="sequencer", num_cores=1),
        out_type=jax.ShapeDtypeStruct((N_DEV,) + v.shape, v.dtype),
        scratch_types=[pltpu.SemaphoreType.DMA((7,)), pltpu.SemaphoreType.DMA((7,)), pltpu.SemaphoreType.DMA],
        compiler_params=pltpu.CompilerParams(collective_id=collective_id),
    )(v)


SUM_BLOCK_ELEMS = 512 * 1024


def sum_slabs(name, b):
    k, h, w = b.shape

    def body(b_ref, o_ref):
        acc = b_ref[0].astype(F32)
        for i in range(1, k):
            acc = acc + b_ref[i].astype(F32)
        o_ref[...] = acc

    return pl.pallas_call(
        body, name=name, out_shape=jax.ShapeDtypeStruct((h, w), F32),
        in_specs=[pl.BlockSpec(memory_space=pltpu.VMEM)], out_specs=pl.BlockSpec(memory_space=pltpu.VMEM),
        compiler_params=pltpu.CompilerParams(vmem_limit_bytes=VMEM_LIMIT),
    )(b)


def sum_pieces(name, pieces, gs):
    nl = len(pieces)
    k, h, w = pieces[0].shape
    tile = max(t for t in range(16, h + 1, 16) if h % t == 0 and (t * w <= SUM_BLOCK_ELEMS or t == 16))
    nt = h // tile
    x, y, c = lax.axis_index("x"), lax.axis_index("y"), lax.axis_index("c")
    place = [v.astype(jnp.int32).reshape(1) for v in (c, 2 * x + y, 4 * x + 2 * y + c)]

    assert nl == 2

    def tile_of(l, a, i):
        return i * a if l else i * (1 - a) + (nt - 1) * a

    def body(c_ref, me_ref, dev_ref, *refs):
        p_refs, g_refs, o_ref = refs[:nl], refs[nl:2 * nl], refs[2 * nl]
        my_dev = dev_ref[0]
        for l in range(nl):
            @pl.when(pl.program_id(0) == l)
            def _():
                o_ref[0] = jnp.zeros(o_ref.shape[1:], F32)
                for d in range(k):
                    @pl.when(my_dev == d)
                    def _():
                        o_ref[0] += g_refs[l][0].astype(F32)

                    @pl.when(my_dev != d)
                    def _():
                        o_ref[0] += p_refs[l][d].astype(F32)

    in_specs = [pl.BlockSpec((k, tile, w), functools.partial(lambda l, a, i, cc, me, dev: (0, tile_of(l, a, i), 0), l))
                for l in range(nl)]
    in_specs += [pl.BlockSpec((1, tile, w), functools.partial(lambda l, a, i, cc, me, dev: (me[0], cc[0] * nt + tile_of(l, a, i), 0), l))
                 for l in range(nl)]
    return pl.pallas_call(
        body, name=name, out_shape=jax.ShapeDtypeStruct((nl, 2 * h, w), F32),
        grid_spec=pltpu.PrefetchScalarGridSpec(
            num_scalar_prefetch=3, grid=(nl, nt), in_specs=in_specs,
            out_specs=pl.BlockSpec((1, tile, w), lambda a, i, cc, me, dev: (a, cc[0] * nt + i, 0))),
        compiler_params=_params("arbitrary", "arbitrary"),
    )(*place, *pieces, *gs)


def _adam_block(w, g, m, v):
    m = ADAM_B1 * m + (1.0 - ADAM_B1) * g
    v = ADAM_B2 * v + (1.0 - ADAM_B2) * (g * g)
    m_hat = m / (1.0 - ADAM_B1 ** ADAM_STEP)
    v_hat = v / (1.0 - ADAM_B2 ** ADAM_STEP)
    return -ADAM_LR * (m_hat / (jnp.sqrt(v_hat) + ADAM_EPS) + ADAM_WD * w), m, v


def adamw(name, w, g, m, v):
    shape = w.shape
    cols = shape[-1]
    rows = w.size // cols
    tile = 128 if rows % 128 == 0 else rows
    flat = [a.reshape(rows, cols) for a in (w, g, m, v)]

    def body(w_ref, g_ref, m_ref, v_ref, d_ref, nm_ref, nv_ref):
        d_ref[...], nm_ref[...], nv_ref[...] = _adam_block(w_ref[...], g_ref[...], m_ref[...], v_ref[...])

    blk = pl.BlockSpec((tile, cols), lambda i: (i, 0))
    res = pl.pallas_call(
        body, name=name, grid=(rows // tile,), in_specs=[blk] * 4, out_specs=[blk] * 3,
        out_shape=[jax.ShapeDtypeStruct((rows, cols), F32)] * 3, compiler_params=_params("parallel"),
    )(*flat)
    return tuple(r.reshape(shape) for r in res)


def _pack_small(values):
    flat = jnp.concatenate([v.reshape(-1) for v in values.values()])
    n = flat.shape[0]
    total = -(-n // (8 * LANES)) * (8 * LANES)
    return jnp.pad(flat, (0, total - n)).reshape(-1, LANES)


def _unpack_small(v, shapes):
    flat = v.reshape(-1)
    out, off = {}, 0
    for n, shape in shapes.items():
        sz = int(np.prod(shape))
        out[n] = flat[off:off + sz].reshape(shape)
        off += sz
    return out


def kernel(x, ffn1_norm, ffn1_w_gate, ffn1_w_up, ffn1_w_down, mix_norm, w_in, rg_conv_w, rg_conv_b, rg_w_r, rg_b_r, rg_w_i, rg_b_i, rg_lambda, att_q_norm, att_k_norm, dn_conv_w, dn_a_log, dn_dt_bias, dn_out_norm, w_branch, w_out, ffn2_norm, ffn2_w_gate, ffn2_w_up, ffn2_w_down, loss_target, m_ffn1_norm, m_ffn1_w_gate, m_ffn1_w_up, m_ffn1_w_down, m_mix_norm, m_w_in, m_rg_conv_w, m_rg_conv_b, m_rg_w_r, m_rg_b_r, m_rg_w_i, m_rg_b_i, m_rg_lambda, m_att_q_norm, m_att_k_norm, m_dn_conv_w, m_dn_a_log, m_dn_dt_bias, m_dn_out_norm, m_w_branch, m_w_out, m_ffn2_norm, m_ffn2_w_gate, m_ffn2_w_up, m_ffn2_w_down, v_ffn1_norm, v_ffn1_w_gate, v_ffn1_w_up, v_ffn1_w_down, v_mix_norm, v_w_in, v_rg_conv_w, v_rg_conv_b, v_rg_w_r, v_rg_b_r, v_rg_w_i, v_rg_b_i, v_rg_lambda, v_att_q_norm, v_att_k_norm, v_dn_conv_w, v_dn_a_log, v_dn_dt_bias, v_dn_out_norm, v_w_branch, v_w_out, v_ffn2_norm, v_ffn2_w_gate, v_ffn2_w_up, v_ffn2_w_down):
    given = dict(locals())
    for n in TRANSPOSED_MATS:
        for pre in ("", "m_", "v_"):
            given[pre + n] = jnp.swapaxes(given[pre + n], 1, 2)
    small = {n: given[n] for n in SMALL_NAMES}
    n_layers = ffn1_norm.shape[0]
    mat_names = [n for n, _ in MATRICES]
    conv_names = [n for n, _ in CONVS]

    blocks = {}
    for n in mat_names:
        for l, b in enumerate(cast_into_blocks("cast_" + n, given[n])):
            blocks[n, l] = b
    first, done = {}, []
    for i, wanted in enumerate(GATHER_ORDER[:-1]):
        bufs, _ = lax.optimization_barrier(([blocks[k] for k in wanted], done))
        done = allgather_blocks_sc(f"allgather_{i}", bufs, GATHER_IDS[i])
        first.update(zip(wanted, done))
    rest = {k: b for k, b in blocks.items() if k not in first}
    taps = jnp.concatenate([given[n].reshape(-1) for n in conv_names]).reshape(-1, LANES)
    taps = allgather_small_sc("allgather_taps", taps, 8).reshape(N_CHIPS, 2, -1)[:, 0]
    conv, off = {}, 0
    for n, ax in CONVS:
        sz = given[n].size
        conv[n] = _shard_minor(taps[:, off:off + sz].reshape((N_CHIPS,) + given[n].shape), ax)
        off += sz
    p = [layer_small(small, l) for l in range(n_layers)]

    ffn1_keys = GATHER_ORDER[0]
    (x1, pend), first_vjp = jax.vjp(first_ffn, *[first[k] for k in ffn1_keys], p[0]["ffn1_norm"], x[0])
    keys = list(rest)
    bufs, pend, second = lax.optimization_barrier(([rest[k] for k in keys], pend, [first[k] for k in GATHER_ORDER[1]]))
    gathered = dict(zip(keys, allgather_blocks_sc("allgather_2", bufs, GATHER_IDS[2])))
    gathered.update(zip(GATHER_ORDER[1], second))
    loss, (g_mats, g_conv, gp, gx1, gpend) = jax.value_and_grad(rest_of_step, argnums=(0, 1, 2, 3, 4))(
        gathered, conv, p, x1, pend, loss_target[0])
    *g_ffn1, gp[0]["ffn1_norm"], gx = first_vjp((gx1, gpend))
    g_mats.update(zip(ffn1_keys, g_ffn1))

    pieces = {}
    for i, group in enumerate(EXCHANGE_GROUPS):
        keys = [k for k in g_mats if group(*k)]
        pieces.update(zip(keys, exchange_pieces_sc(f"exchange_{i}", [g_mats[k] for k in keys], 2 + i)))
    halves = {n: sum_pieces("sum_" + n, [pieces[n, l] for l in range(n_layers)], [g_mats[n, l] for l in range(n_layers)])
              for n in mat_names}
    grads = {}
    for tag, names in (("late", [n for n in mat_names if n in LATE_MATS]), ("early", [n for n in mat_names if n not in LATE_MATS])):
        grads.update(zip(names, sibling_share_halves("share_" + tag, [halves[n] for n in names])))

    g_small = dict(layer_small_grads(gp, small), **g_conv, loss=loss.reshape(1))
    packed_small = _pack_small(g_small)
    slabs = allgather_small_sc("allgather_small", packed_small, 9)
    summed =_unpack_small(sum_slabs("sum_small", slabs), {n: g.shape for n, g in g_small.items()})
    chip = 2 * lax.axis_index("x") + lax.axis_index("y")
    for n in SMALL_NAMES:
        grads[n] = summed[n]
    for n, ax in CONVS:
        s = given[n].shape[ax]
        grads[n] = lax.dynamic_slice_in_dim(summed[n], chip * s, s, axis=ax)

    upd = {n: adamw("adamw_" + n, given[n], grads[n], given["m_" + n], given["v_" + n]) for n in WEIGHT_NAMES}
    out = lambda n, a: jnp.swapaxes(a, 1, 2) if n in TRANSPOSED_MATS else a
    return (summed["loss"][0], gx[None], *[out(n, grads[n]) for n in WEIGHT_NAMES], *[out(n, upd[n][0]) for n in WEIGHT_NAMES],
            *[out(n, upd[n][1]) for n in WEIGHT_NAMES], *[out(n, upd[n][2]) for n in WEIGHT_NAMES])
```

```python
import functools
import math

import jax
import jax.numpy as jnp
import numpy as np
from jax import lax
from jax.experimental import pallas as pl
from jax.experimental.pallas import tpu as pltpu
from jax.experimental.pallas import tpu_sc as plsc

F32 = jnp.float32
BF16 = jnp.bfloat16
MESH = pl.DeviceIdType.MESH

D_MODEL = 1024
FFN_DIM = 2816
RG_C = 8.0
ATT_GROUPS = ((128, 1), (512, 4), (2048, 16))
ATT_HEADS = 12
ATT_HEAD_DIM = 64
ATT_SPAN = 128
DN_HEADS = 8
DN_HEAD_DIM = 128
DN_CHUNK = 64
EPS = 1e-6
NEG_INF = -1e30
N_CHIPS = 4
N_DEV = 8

ADAM_LR, ADAM_B1, ADAM_B2, ADAM_EPS, ADAM_WD, ADAM_STEP = 0.001, 0.9, 0.999, 1e-08, 0.01, 10

LANES = 128
VMEM_LIMIT = 56 * 1024 * 1024


def _params(*sem):
    return pltpu.CompilerParams(dimension_semantics=sem or None, vmem_limit_bytes=VMEM_LIMIT)


def _sigmoid(x):
    return 1.0 / (1.0 + jnp.exp(-x))


def _silu(x):
    return x * _sigmoid(x)


def _softplus(x):
    return jnp.maximum(x, 0.0) + jnp.log(1.0 + jnp.exp(-jnp.abs(x)))


def _gelu(x):
    return 0.5 * x * (1.0 + jnp.tanh(math.sqrt(2.0 / math.pi) * (x + 0.044715 * (x * x * x))))


def _neg_expm1(x):
    series = -x * (1.0 + x * (0.5 + x * (1.0 / 6 + x * (1.0 / 24 + x * (1.0 / 120 + x * (1.0 / 720))))))
    return jnp.where(x > -0.25, series, 1.0 - jnp.exp(x))


def _rms(x, g):
    return x * lax.rsqrt(jnp.mean(x * x, axis=-1, keepdims=True) + EPS) * g


_MM_DIMS = {"nn": (((1,), (0,)), ((), ())), "nt": (((1,), (1,)), ((), ())), "tn": (((0,), (0,)), ((), ()))}


def _split(a):
    hi = a.astype(BF16)
    return hi, (a - hi.astype(F32)).astype(BF16)


def _mxu(a, b, form, passes):
    (ca, cb), _ = _MM_DIMS[form]
    if a.ndim == 3:
        dims = (((ca[0] + 1,), (cb[0] + 1,)), ((0,), (0,)))
    else:
        dims = _MM_DIMS[form]
    dg = lambda p, q: lax.dot_general(p, q, dims, preferred_element_type=F32)
    if passes == 1:
        return dg(a.astype(BF16), b.astype(BF16))
    (a_hi, a_lo), (b_hi, b_lo) = _split(a), _split(b)
    return dg(a_hi, b_hi) + (dg(a_hi, b_lo) + dg(a_lo, b_hi))


@functools.partial(jax.custom_vjp, nondiff_argnums=(2, 3))
def _mm(a, b, form, passes):
    return _mxu(a, b, form, passes)


def _mm_fwd(a, b, form, passes):
    return _mxu(a, b, form, passes), (a, b)


def _mm_bwd(form, passes, res, g):
    a, b = res
    if form == "nn":
        return _mm(g, b, "nt", passes), _mm(a, g, "tn", passes)
    if form == "nt":
        return _mm(g, b, "nn", passes), _mm(g, a, "tn", passes)
    return _mm(b, g, "nt", passes), _mm(a, g, "nn", passes)


_mm.defvjp(_mm_fwd, _mm_bwd)


def _dot(a, b):
    return _mm(a, b, "nn", 1)


def _dot_nt(a, b):
    return _mm(a, b, "nt", 1)


def _dot_tn(a, b):
    return _mm(a, b, "tn", 1)


def _dot3(a, b):
    return _mm(a, b, "nn", 3)


def _rows(shape):
    return lax.broadcasted_iota(jnp.int32, shape, len(shape) - 2)


def _roll_down(x, s, fill):
    return jnp.where(_rows(x.shape) >= s, pltpu.roll(x, s, x.ndim - 2), fill)


def _roll_up(x, s, fill):
    n = x.shape[-2]
    return jnp.where(_rows(x.shape) < n - s, pltpu.roll(x, n - s, x.ndim - 2), fill)


@functools.partial(jax.custom_vjp, nondiff_argnums=(1,))
def _shift(x, s):
    return _roll_down(x, s, 0.0)


def _shift_fwd(x, s):
    return _roll_down(x, s, 0.0), None


def _shift_bwd(s, _, g):
    return (_roll_up(g, s, 0.0),)


_shift.defvjp(_shift_fwd, _shift_bwd)


def _causal_conv(x, w):
    return w[0:1] * _shift(x, 3) + w[1:2] * _shift(x, 2) + w[2:3] * _shift(x, 1) + w[3:4] * x


@jax.custom_vjp
def _lin_scan(a, b):
    return _lin_scan_fwd(a, b)[0]


def _lin_scan_fwd(a, b):
    a0 = a
    s = 1
    while s < a.shape[0]:
        b = a * _roll_down(b, s, 0.0) + b
        a = a * _roll_down(a, s, 1.0)
        s *= 2
    return b, (a0, b)


def _lin_scan_bwd(res, g):
    a, h = res
    c = _roll_up(a, 1, 0.0)
    s = 1
    while s < a.shape[0]:
        g = c * _roll_up(g, s, 0.0) + g
        c = c * _roll_up(c, s, 1.0)
        s *= 2
    return g * _roll_down(h, 1, 0.0), g


_lin_scan.defvjp(_lin_scan_fwd, _lin_scan_bwd)


@jax.custom_vjp
def _cumsum_rows(x):
    s = 1
    while s < x.shape[-2]:
        x = x + _roll_down(x, s, 0.0)
        s *= 2
    return x


def _cumsum_rows_fwd(x):
    return _cumsum_rows(x), None


def _cumsum_rows_bwd(_, g):
    s = 1
    while s < g.shape[-2]:
        g = g + _roll_up(g, s, 0.0)
        s *= 2
    return (g,)


_cumsum_rows.defvjp(_cumsum_rows_fwd, _cumsum_rows_bwd)


ROW_BLOCK_BYTES = 14 * 1024 * 1024


def _row_tile(t, width=0):
    for tile in (512, 256):
        if t % tile == 0 and (tile == 256 or tile * width * 4 <= ROW_BLOCK_BYTES):
            return tile
    return t


def _rowwise_fwd_call(name, f, rows, pars, tile):
    t = rows[0].shape[0]
    outs = jax.eval_shape(f, *[jax.ShapeDtypeStruct((tile, r.shape[1]), F32) for r in rows],
                          *[jax.ShapeDtypeStruct(p.shape, F32) for p in pars])
    nr, npar = len(rows), len(pars)

    def body(*refs):
        ins = [r[...] for r in refs[:nr + npar]]
        res = f(*ins)
        for o_ref, o in zip(refs[nr + npar:], res):
            o_ref[...] = o.astype(o_ref.dtype)

    return pl.pallas_call(
        body, name=name, grid=(t // tile,),
        in_specs=[pl.BlockSpec((tile, r.shape[1]), lambda i: (i, 0)) for r in rows]
        + [pl.BlockSpec(p.shape, lambda i: (0, 0)) for p in pars],
        out_specs=[pl.BlockSpec((tile, o.shape[1]), lambda i: (i, 0)) for o in outs],
        out_shape=[jax.ShapeDtypeStruct((t, o.shape[1]), F32) for o in outs],
        compiler_params=_params("parallel"),
    )(*rows, *pars)


def _rowwise_bwd_call(name, f, rows, pars, cts, tile):
    t = rows[0].shape[0]
    nr, npar, nct = len(rows), len(pars), len(cts)

    def body(*refs):
        ins = [r[...] for r in refs[:nr + npar]]
        gs = tuple(r[...] for r in refs[nr + npar:nr + npar + nct])
        outs = refs[nr + npar + nct:]
        _, vjp = jax.vjp(f, *ins)
        d = vjp(gs)
        for o_ref, v in zip(outs[:nr], d[:nr]):
            o_ref[...] = v

        @pl.when(pl.program_id(0) == 0)
        def _():
            for o_ref in outs[nr:]:
                o_ref[...] = jnp.zeros_like(o_ref)

        for o_ref, v in zip(outs[nr:], d[nr:]):
            o_ref[...] += v

    res = pl.pallas_call(
        body, name=name, grid=(t // tile,),
        in_specs=[pl.BlockSpec((tile, r.shape[1]), lambda i: (i, 0)) for r in rows]
        + [pl.BlockSpec(p.shape, lambda i: (0, 0)) for p in pars]
        + [pl.BlockSpec((tile, c.shape[1]), lambda i: (i, 0)) for c in cts],
        out_specs=[pl.BlockSpec((tile, r.shape[1]), lambda i: (i, 0)) for r in rows]
        + [pl.BlockSpec(p.shape, lambda i: (0, 0)) for p in pars],
        out_shape=[jax.ShapeDtypeStruct(r.shape, F32) for r in rows]
        + [jax.ShapeDtypeStruct(p.shape, F32) for p in pars],
        compiler_params=_params("arbitrary"),
    )(*rows, *pars, *cts)
    return tuple(res[:nr]), tuple(res[nr:])


def rowwise(name, f, rows, pars=()):
    outs = jax.eval_shape(f, *[jax.ShapeDtypeStruct((8, r.shape[1]), F32) for r in rows],
                          *[jax.ShapeDtypeStruct(p.shape, F32) for p in pars])
    tile = _row_tile(rows[0].shape[0], 2 * sum(r.shape[1] for r in rows) + sum(o.shape[1] for o in outs))

    @jax.custom_vjp
    def op(rows, pars):
        return tuple(_rowwise_fwd_call(name, f, rows, pars, tile))

    def op_fwd(rows, pars):
        return op(rows, pars), (rows, pars)

    def op_bwd(res, cts):
        return _rowwise_bwd_call(name + "_bwd", f, res[0], res[1], tuple(cts), tile)

    op.defvjp(op_fwd, op_bwd)
    return op(tuple(rows), tuple(pars))


MM_TM = 512


def _tile_of(n, cap):
    best = None
    for c in range(LANES, min(n, cap) + 1, LANES):
        if n % c == 0:
            best = c
    return best or n


def _mmc_fwd(name, h, w):
    m, k = h.shape
    j, _, n = w.shape
    tm, tn = MM_TM, _tile_of(n, 1408)

    def body(h_ref, w_ref, o_ref):
        o_ref[0] = _dot(h_ref[...], w_ref[0])

    return pl.pallas_call(
        body, name=name, grid=(m // tm, j, n // tn),
        in_specs=[pl.BlockSpec((tm, k), lambda i, b, c: (i, 0)), pl.BlockSpec((1, k, tn), lambda i, b, c: (b, 0, c))],
        out_specs=pl.BlockSpec((1, tm, tn), lambda i, b, c: (b, i, c)),
        out_shape=jax.ShapeDtypeStruct((j, m, n), F32),
        compiler_params=_params("parallel", "parallel", "parallel"),
    )(h, w)


def _mmc_dw(name, h, dy):
    m, k = h.shape
    j, _, n = dy.shape
    tk, tn = _tile_of(k, 512), _tile_of(n, 1152)

    def body(h_ref, dy_ref, o_ref):
        o_ref[0] = _dot_tn(h_ref[...], dy_ref[0]).astype(BF16)

    return pl.pallas_call(
        body, name=name, grid=(j, k // tk, n // tn),
        in_specs=[pl.BlockSpec((m, tk), lambda b, i, c: (0, i)), pl.BlockSpec((1, m, tn), lambda b, i, c: (b, 0, c))],
        out_specs=pl.BlockSpec((1, tk, tn), lambda b, i, c: (b, i, c)),
        out_shape=jax.ShapeDtypeStruct((j, k, n), BF16),
        compiler_params=_params("parallel", "parallel", "parallel"),
    )(h, dy)


PROJ_GROUP_COLS = 4608


def _proj_dh(name, dys, ws, acc):
    m, k = dys[0].shape[0], ws[0].shape[0]
    n, tm = len(dys), 256

    def body(*refs):
        dy_refs, w_refs, rest = refs[:n], refs[n:2 * n], refs[2 * n:]
        total = _dot_nt(dy_refs[0][...], w_refs[0][...])
        for dy_ref, w_ref in zip(dy_refs[1:], w_refs[1:]):
            total = total + _dot_nt(dy_ref[...], w_ref[...])
        if acc is not None:
            total = total + rest[0][...]
        rest[-1][...] = total

    row = lambda width: pl.BlockSpec((tm, width), lambda i: (i, 0))
    return pl.pallas_call(
        body, name=name, grid=(m // tm,),
        in_specs=[row(d.shape[1]) for d in dys] + [pl.BlockSpec(w.shape, lambda i: (0, 0)) for w in ws] + ([row(k)] if acc is not None else []),
        out_specs=row(k), out_shape=jax.ShapeDtypeStruct((m, k), F32), compiler_params=_params("parallel"),
    )(*dys, *ws, *([acc] if acc is not None else []))


def project_in(name, h, ws):
    keys = list(ws)

    @jax.custom_vjp
    def op(h, ws):
        return {p: _mmc_fwd(f"{name}_{p}", h, ws[p][None])[0] for p in keys}

    def op_fwd(h, ws):
        return op(h, ws), (h, ws)

    def op_bwd(res, dys):
        h, ws = res
        groups, cols = [[]], 0
        for p in keys:
            if groups[-1] and cols + ws[p].shape[1] > PROJ_GROUP_COLS:
                groups.append([])
                cols = 0
            groups[-1].append(p)
            cols += ws[p].shape[1]
        dh = None
        for i, group in enumerate(groups):
            dh = _proj_dh(f"{name}_dh{i}", [dys[p] for p in group], [ws[p] for p in group], dh)
        return dh, {p: _mmc_dw(f"{name}_{p}_dw", h, dys[p][None])[0] for p in keys}

    op.defvjp(op_fwd, op_bwd)
    return op(h, ws)


def _ffn_up(name, h, wt):
    m, k = h.shape
    j, n, _ = wt.shape
    tm = MM_TM

    def body(h_ref, w_ref, o_ref):
        o_ref[0] = _dot_nt(h_ref[...], w_ref[0])

    return pl.pallas_call(
        body, name=name, grid=(m // tm, j),
        in_specs=[pl.BlockSpec((tm, k), lambda i, b: (i, 0)), pl.BlockSpec((1, n, k), lambda i, b: (b, 0, 0))],
        out_specs=pl.BlockSpec((1, tm, n), lambda i, b: (b, i, 0)),
        out_shape=jax.ShapeDtypeStruct((j, m, n), F32), compiler_params=_params("parallel", "parallel"),
    )(h, wt)


def _ffn_down(name, g, u, wd):
    j, m, n = g.shape
    d = wd.shape[2]
    tm = MM_TM

    def body(g_ref, u_ref, w_ref, o_ref):
        part = _dot(_silu(g_ref[0]) * u_ref[0], w_ref[0])

        @pl.when(pl.program_id(1) == 0)
        def _():
            o_ref[...] = part

        @pl.when(pl.program_id(1) > 0)
        def _():
            o_ref[...] += part

    act = pl.BlockSpec((1, tm, n), lambda i, b: (b, i, 0))
    return pl.pallas_call(
        body, name=name, grid=(m // tm, j),
        in_specs=[act, act, pl.BlockSpec((1, n, d), lambda i, b: (b, 0, 0))],
        out_specs=pl.BlockSpec((tm, d), lambda i, b: (i, 0)),
        out_shape=jax.ShapeDtypeStruct((m, d), F32), compiler_params=_params("parallel", "arbitrary"),
    )(g, u, wd)


def _ffn_down_bwd(name, dy, g, u, wd):
    j, m, n = g.shape
    d = wd.shape[2]
    tm = MM_TM

    def body(dy_ref, g_ref, u_ref, w_ref, dg_ref, du_ref):
        da = _dot_nt(dy_ref[...], w_ref[0])
        gv = g_ref[0]
        s = _sigmoid(gv)
        dg_ref[0] = da * u_ref[0] * (s * (1.0 + gv * (1.0 - s)))
        du_ref[0] = da * (gv * s)

    act = pl.BlockSpec((1, tm, n), lambda i, b: (b, i, 0))
    return pl.pallas_call(
        body, name=name, grid=(m // tm, j),
        in_specs=[pl.BlockSpec((tm, d), lambda i, b: (i, 0)), act, act, pl.BlockSpec((1, n, d), lambda i, b: (b, 0, 0))],
        out_specs=[act, act], out_shape=[jax.ShapeDtypeStruct((j, m, n), F32)] * 2,
        compiler_params=_params("parallel", "parallel"),
    )(dy, g, u, wd)


def _ffn_down_dw(name, g, u, dy):
    j, m, n = g.shape
    d = dy.shape[1]
    tn = _tile_of(d, 512)

    def body(g_ref, u_ref, dy_ref, o_ref):
        o_ref[0] = _dot_tn(_silu(g_ref[0]) * u_ref[0], dy_ref[...]).astype(BF16)

    act = pl.BlockSpec((1, m, n), lambda b, c: (b, 0, 0))
    return pl.pallas_call(
        body, name=name, grid=(j, d // tn),
        in_specs=[act, act, pl.BlockSpec((m, tn), lambda b, c: (0, c))],
        out_specs=pl.BlockSpec((1, n, tn), lambda b, c: (b, 0, c)),
        out_shape=jax.ShapeDtypeStruct((j, n, d), BF16), compiler_params=_params("parallel", "parallel"),
    )(g, u, dy)


def _ffn_up_dh(name, dg, du, wg, wu):
    j, m, n = dg.shape
    k = wg.shape[2]
    tm = MM_TM

    def body(dg_ref, du_ref, wg_ref, wu_ref, o_ref):
        part = _dot(dg_ref[0], wg_ref[0]) + _dot(du_ref[0], wu_ref[0])

        @pl.when(pl.program_id(1) == 0)
        def _():
            o_ref[...] = part

        @pl.when(pl.program_id(1) > 0)
        def _():
            o_ref[...] += part

    act = pl.BlockSpec((1, tm, n), lambda i, b: (b, i, 0))
    wsp = pl.BlockSpec((1, n, k), lambda i, b: (b, 0, 0))
    return pl.pallas_call(
        body, name=name, grid=(m // tm, j), in_specs=[act, act, wsp, wsp],
        out_specs=pl.BlockSpec((tm, k), lambda i, b: (i, 0)),
        out_shape=jax.ShapeDtypeStruct((m, k), F32), compiler_params=_params("parallel", "arbitrary"),
    )(dg, du, wg, wu)


def _ffn_up_dw(name, dy, h):
    j, m, n = dy.shape
    k = h.shape[1]
    tk = _tile_of(k, 512)

    def body(dy_ref, h_ref, o_ref):
        o_ref[0] = _dot_tn(dy_ref[0], h_ref[...]).astype(BF16)

    return pl.pallas_call(
        body, name=name, grid=(j, k // tk),
        in_specs=[pl.BlockSpec((1, m, n), lambda b, i: (b, 0, 0)), pl.BlockSpec((m, tk), lambda b, i: (0, i))],
        out_specs=pl.BlockSpec((1, n, tk), lambda b, i: (b, 0, i)),
        out_shape=jax.ShapeDtypeStruct((j, n, k), BF16), compiler_params=_params("parallel", "parallel"),
    )(dy, h)


def ffn(name, h, wg, wu, wd):
    @jax.custom_vjp
    def op(h, wg, wu, wd):
        return _ffn_down(name + "_d", _ffn_up(name + "_g", h, wg), _ffn_up(name + "_u", h, wu), wd)

    def op_fwd(h, wg, wu, wd):
        g, u = _ffn_up(name + "_g", h, wg), _ffn_up(name + "_u", h, wu)
        return _ffn_down(name + "_d", g, u, wd), (h, g, u, wg, wu, wd)

    def op_bwd(res, dy):
        h, g, u, wg, wu, wd = res
        dg, du = _ffn_down_bwd(name + "_d_bwd", dy, g, u, wd)
        return (_ffn_up_dh(name + "_dh", dg, du, wg, wu), _ffn_up_dw(name + "_g_dw", dg, h), _ffn_up_dw(name + "_u_dw", du, h),
                _ffn_down_dw(name + "_d_dw", g, u, dy))

    op.defvjp(op_fwd, op_bwd)
    return op(h, wg, wu, wd)


def _mmr_fwd(name, a, w):
    j, m, n = a.shape
    nn = w.shape[2]
    tm, tn = MM_TM, _tile_of(nn, 1024)

    def body(a_ref, w_ref, o_ref):
        part = _dot(a_ref[0], w_ref[0])

        @pl.when(pl.program_id(2) == 0)
        def _():
            o_ref[...] = part

        @pl.when(pl.program_id(2) > 0)
        def _():
            o_ref[...] += part

    return pl.pallas_call(
        body, name=name, grid=(m // tm, nn // tn, j),
        in_specs=[pl.BlockSpec((1, tm, n), lambda i, c, b: (b, i, 0)), pl.BlockSpec((1, n, tn), lambda i, c, b: (b, 0, c))],
        out_specs=pl.BlockSpec((tm, tn), lambda i, c, b: (i, c)),
        out_shape=jax.ShapeDtypeStruct((m, nn), F32),
        compiler_params=_params("parallel", "parallel", "arbitrary"),
    )(a, w)


def _mmr_da(name, dy, w):
    m, nn = dy.shape
    j, n, _ = w.shape
    tm = MM_TM

    def body(dy_ref, w_ref, o_ref):
        o_ref[0] = _dot_nt(dy_ref[...], w_ref[0])

    return pl.pallas_call(
        body, name=name, grid=(m // tm, j),
        in_specs=[pl.BlockSpec((tm, nn), lambda i, b: (i, 0)), pl.BlockSpec((1, n, nn), lambda i, b: (b, 0, 0))],
        out_specs=pl.BlockSpec((1, tm, n), lambda i, b: (b, i, 0)),
        out_shape=jax.ShapeDtypeStruct((j, m, n), F32),
        compiler_params=_params("parallel", "parallel"),
    )(dy, w)


def _mmr_dw(name, a, dy):
    j, m, n = a.shape
    nn = dy.shape[1]
    tn = _tile_of(nn, 512)

    def body(a_ref, dy_ref, o_ref):
        o_ref[0] = _dot_tn(a_ref[0], dy_ref[...]).astype(BF16)

    return pl.pallas_call(
        body, name=name, grid=(j, nn // tn),
        in_specs=[pl.BlockSpec((1, m, n), lambda b, c: (b, 0, 0)), pl.BlockSpec((m, tn), lambda b, c: (0, c))],
        out_specs=pl.BlockSpec((1, n, tn), lambda b, c: (b, 0, c)),
        out_shape=jax.ShapeDtypeStruct((j, n, nn), BF16),
        compiler_params=_params("parallel", "parallel"),
    )(a, dy)


def mm_rows(name, a, w):
    @jax.custom_vjp
    def op(a, w):
        return _mmr_fwd(name, a, w)

    def op_fwd(a, w):
        return op(a, w), (a, w)

    def op_bwd(res, dy):
        a, w = res
        return _mmr_da(name + "_da", dy, w), _mmr_dw(name + "_dw", a, dy)

    op.defvjp(op_fwd, op_bwd)
    return op(a, w)


def _colwise_specs(cols, pars, par_block):
    t = cols[0].shape[0]
    specs = [pl.BlockSpec((t, LANES), lambda j: (0, j)) for _ in cols]
    for p, blk in zip(pars, par_block):
        if blk == "lane":
            specs.append(pl.BlockSpec((p.shape[0], LANES), lambda j: (0, j)))
        else:
            specs.append(pl.BlockSpec((1,) + p.shape[1:], lambda j: (j, 0, 0)))
    return specs


def _colwise_fwd_call(name, f, cols, pars, par_block, n_out):
    t, c = cols[0].shape
    nc, npar = len(cols), len(pars)

    def body(*refs):
        ins = [r[...] for r in refs[:nc]] + [r[...] if b == "lane" else r[0] for r, b in zip(refs[nc:nc + npar], par_block)]
        res = f(*ins)
        for o_ref, o in zip(refs[nc + npar:], res):
            o_ref[...] = o

    return pl.pallas_call(
        body, name=name, grid=(c // LANES,),
        in_specs=_colwise_specs(cols, pars, par_block),
        out_specs=[pl.BlockSpec((t, LANES), lambda j: (0, j)) for _ in range(n_out)],
        out_shape=[jax.ShapeDtypeStruct((t, c), F32) for _ in range(n_out)],
        compiler_params=_params("parallel"),
    )(*cols, *pars)


def _colwise_bwd_call(name, f, cols, pars, par_block, cts):
    t, c = cols[0].shape
    nc, npar, nct = len(cols), len(pars), len(cts)

    def body(*refs):
        ins = [r[...] for r in refs[:nc]] + [r[...] if b == "lane" else r[0] for r, b in zip(refs[nc:nc + npar], par_block)]
        gs = tuple(r[...] for r in refs[nc + npar:nc + npar + nct])
        outs = refs[nc + npar + nct:]
        _, vjp = jax.vjp(f, *ins)
        d = vjp(gs)
        for o_ref, v in zip(outs[:nc], d[:nc]):
            o_ref[...] = v
        for o_ref, v, b in zip(outs[nc:], d[nc:], par_block):
            if b == "lane":
                o_ref[...] = v
            else:
                o_ref[0] = v

    res = pl.pallas_call(
        body, name=name, grid=(c // LANES,),
        in_specs=_colwise_specs(cols, pars, par_block) + [pl.BlockSpec((t, LANES), lambda j: (0, j)) for _ in cts],
        out_specs=_colwise_specs(cols, pars, par_block),
        out_shape=[jax.ShapeDtypeStruct(v.shape, F32) for v in (*cols, *pars)],
        compiler_params=_params("parallel"),
    )(*cols, *pars, *cts)
    return tuple(res[:nc]), tuple(res[nc:])


def colwise(name, f, cols, pars, par_block, n_out):
    @jax.custom_vjp
    def op(cols, pars):
        return tuple(_colwise_fwd_call(name, f, cols, pars, par_block, n_out))

    def op_fwd(cols, pars):
        return op(cols, pars), (cols, pars)

    def op_bwd(res, cts):
        return _colwise_bwd_call(name + "_bwd", f, res[0], res[1], par_block, tuple(cts))

    op.defvjp(op_fwd, op_bwd)
    return op(tuple(cols), tuple(pars))


def _rg_block(x, gate, cw, cb, wr, br, wi, bi, lam):
    xa = _causal_conv(x, cw) + cb
    r = _sigmoid(_dot(xa, wr) + br)
    i = _sigmoid(_dot(xa, wi) + bi)
    log_a = -RG_C * r * _softplus(-lam)
    a = jnp.exp(log_a)
    b = jnp.sqrt(_neg_expm1(2.0 * log_a)) * (i * xa)
    return (_lin_scan(a, b) * _gelu(gate),)


def _dn_conv_block(mode):
    def f(x, cw):
        c = _silu(_causal_conv(x, cw))
        if mode == "v":
            return (c,)
        c = c * lax.rsqrt(jnp.sum(c * c, axis=-1, keepdims=True) + EPS)
        return (c * (DN_HEAD_DIM ** -0.5),) if mode == "q" else (c,)
    return f


def _block_diag(w):
    w = w.reshape(8, 2, 64, 64)
    z = jnp.zeros((8, 64, 64), w.dtype)
    top = jnp.concatenate([w[:, 0], z], axis=2)
    bot = jnp.concatenate([z, w[:, 1]], axis=2)
    return jnp.concatenate([top, bot], axis=1)


DN_HP = 8


def _dn_block(S, qw, kw, vw, gb, h0):
    hp, hd = S.shape[0], DN_HEAD_DIM
    heads = lambda a: jnp.concatenate([a[None, :, j * hd:(j + 1) * hd] for j in range(hp)], axis=0)
    lane = lax.broadcasted_iota(jnp.int32, gb.shape, 1)
    col = lambda i: jnp.sum(jnp.where(lane == i, gb, 0.0), axis=1, keepdims=True)[None]
    beta = jnp.concatenate([col(h0 + j) for j in range(hp)], axis=0)
    g = jnp.concatenate([col(h0 + j + DN_HEADS) for j in range(hp)], axis=0)
    s_new, o = _dn_step(S, heads(qw), heads(kw), heads(vw), beta, g)
    return s_new, jnp.concatenate([o[j:j + 1].reshape(o.shape[1:]) for j in range(hp)], axis=1)


@jax.custom_vjp
def _unit_lower_inverse(a):
    c = a.shape[-1]
    eye = (lax.broadcasted_iota(jnp.int32, (c, c), 0) == lax.broadcasted_iota(jnp.int32, (c, c), 1)).astype(F32)
    p = -a
    tinv = eye + p
    for _ in range(5):
        p = _dot3(p, p)
        tinv = tinv + _dot3(tinv, p)
    return tinv


def _unit_lower_inverse_fwd(a):
    t = _unit_lower_inverse(a)
    return t, t


def _unit_lower_inverse_bwd(t, g):
    return (-_mm(_mm(t, g, "tn", 3), t, "nt", 3),)


_unit_lower_inverse.defvjp(_unit_lower_inverse_fwd, _unit_lower_inverse_bwd)


def _dn_step(S, q, k, v, beta, g):
    c = DN_CHUNK
    ri = lax.broadcasted_iota(jnp.int32, (c, c), 0)
    ci = lax.broadcasted_iota(jnp.int32, (c, c), 1)
    incl, strict = ri >= ci, ri > ci
    gam = _cumsum_rows(g)
    gam_row = jnp.sum(jnp.where(ri <= ci, g, 0.0), axis=-2, keepdims=True)
    gam_last = jnp.sum(g, axis=-2, keepdims=True)
    decay = jnp.where(incl, jnp.exp(jnp.where(incl, gam - gam_row, 0.0)), 0.0)
    kb = k * beta
    vb = v * beta
    a = jnp.where(strict, _dot_nt(kb, k) * decay, 0.0)
    tinv = _unit_lower_inverse(a)
    e_gam = jnp.exp(gam)
    u0 = _dot3(tinv, vb)
    wk = _dot3(tinv, kb * e_gam)
    qk = jnp.where(incl, _dot_nt(q, k) * decay, 0.0)
    q_dec = q * e_gam
    k_dec = k * jnp.exp(gam_last - gam)
    u = u0 - _dot(wk, S)
    o = _dot(q_dec, S) + _dot(qk, u)
    s_new = S * jnp.exp(gam_last) + _dot_tn(k_dec, u)
    return s_new, o


def _dn_fwd_call(q, k, v, gb):
    t, w = q.shape
    n, hp, hd, c = t // DN_CHUNK, DN_HP, DN_HEAD_DIM, DN_CHUNK

    def body(q_ref, k_ref, v_ref, gb_ref, o_ref, s0_ref, s_scr):
        @pl.when(pl.program_id(1) == 0)
        def _():
            s_scr[...] = jnp.zeros_like(s_scr)

        s_old = s_scr[...]
        s0_ref[:, 0] = s_old
        s_new, o = _dn_block(s_old, q_ref[...], k_ref[...], v_ref[...], gb_ref[...], pl.program_id(0) * hp)
        o_ref[...] = o
        s_scr[...] = s_new

    blk = pl.BlockSpec((c, hp * hd), lambda g, i: (i, g))
    return pl.pallas_call(
        body, name="dn_core", grid=(DN_HEADS // hp, n),
        in_specs=[blk, blk, blk, pl.BlockSpec((c, LANES), lambda g, i: (i, 0))],
        out_specs=[blk, pl.BlockSpec((hp, 1, hd, hd), lambda g, i: (g, i, 0, 0))],
        out_shape=[jax.ShapeDtypeStruct((t, w), F32), jax.ShapeDtypeStruct((DN_HEADS, n, hd, hd), F32)],
        scratch_shapes=[pltpu.VMEM((hp, hd, hd), F32)],
        compiler_params=_params("parallel", "arbitrary"),
    )(q, k, v, gb)


def _dn_bwd_call(q, k, v, gb, s0, do):
    t, w = q.shape
    n, hp, hd, c = t // DN_CHUNK, DN_HP, DN_HEAD_DIM, DN_CHUNK
    ng = DN_HEADS // hp

    def body(q_ref, k_ref, v_ref, gb_ref, s0_ref, do_ref, dq_ref, dk_ref, dv_ref, dgb_ref, ds_scr):
        @pl.when(pl.program_id(1) == 0)
        def _():
            ds_scr[...] = jnp.zeros_like(ds_scr)

        h0 = pl.program_id(0) * hp
        _, vjp = jax.vjp(lambda *a: _dn_block(*a, h0), s0_ref[:, 0], q_ref[...], k_ref[...], v_ref[...], gb_ref[...])
        ds, dq, dk, dv, dgb = vjp((ds_scr[...], do_ref[...]))
        ds_scr[...] = ds
        dq_ref[...], dk_ref[...], dv_ref[...] = dq, dk, dv
        dgb_ref[0] = dgb

    blk = pl.BlockSpec((c, hp * hd), lambda g, i: (n - 1 - i, g))
    res = pl.pallas_call(
        body, name="dn_core_bwd", grid=(ng, n),
        in_specs=[blk, blk, blk, pl.BlockSpec((c, LANES), lambda g, i: (n - 1 - i, 0)),
                  pl.BlockSpec((hp, 1, hd, hd), lambda g, i: (g, n - 1 - i, 0, 0)), blk],
        out_specs=[blk, blk, blk, pl.BlockSpec((1, c, LANES), lambda g, i: (g, n - 1 - i, 0))],
        out_shape=[jax.ShapeDtypeStruct((t, w), F32)] * 3 + [jax.ShapeDtypeStruct((ng, t, LANES), F32)],
        scratch_shapes=[pltpu.VMEM((hp, hd, hd), F32)],
        compiler_params=_params("parallel", "arbitrary"),
    )(q, k, v, gb, s0, do)
    return res[0], res[1], res[2], jnp.sum(res[3], axis=0)


@jax.custom_vjp
def dn_core(q, k, v, gb):
    return _dn_fwd_call(q, k, v, gb)[0]


def _dn_core_fwd(q, k, v, gb):
    o, s0 = _dn_fwd_call(q, k, v, gb)
    return o, (q, k, v, gb, s0)


def _dn_core_bwd(res, do):
    return _dn_bwd_call(*res, do)


dn_core.defvjp(_dn_core_fwd, _dn_core_bwd)


ATT_GH = 4


def _att_block(q, kp, kc, vp, vc, qn, kn, slope, has_prev, dil):
    s = ATT_SPAN
    qh = _rms(q, qn) * (ATT_HEAD_DIM ** -0.5)
    qi = lax.broadcasted_iota(jnp.int32, (s, s), 0)
    kj = lax.broadcasted_iota(jnp.int32, (s, s), 1)
    d_p = qi + s - kj
    d_c = qi - kj
    s_p = _dot_nt(qh, _rms(kp, kn)) - slope * (d_p * dil).astype(F32)
    s_c = _dot_nt(qh, _rms(kc, kn)) - slope * (d_c * dil).astype(F32)
    s_p = jnp.where((d_p <= s) & (has_prev > 0), s_p, NEG_INF)
    s_c = jnp.where(d_c >= 0, s_c, NEG_INF)
    m = lax.stop_gradient(jnp.maximum(jnp.max(s_p, axis=-1, keepdims=True), jnp.max(s_c, axis=-1, keepdims=True)))
    p_p = jnp.exp(s_p - m)
    p_c = jnp.exp(s_c - m)
    den = jnp.sum(p_p, axis=-1, keepdims=True) + jnp.sum(p_c, axis=-1, keepdims=True)
    o = _dot(p_p / den, vp) + _dot(p_c / den, vc)
    lse = m + jnp.log(den)
    return o, jnp.broadcast_to(lse, o.shape)


def _att_heads(a):
    e = ATT_HEAD_DIM
    return jnp.concatenate([a[None, :, h * e:(h + 1) * e] for h in range(ATT_GH)], axis=0)


def _att_lanes(a):
    return jnp.concatenate([a[h:h + 1].reshape(a.shape[1:]) for h in range(ATT_GH)], axis=1)


def _att_rows(q, kp, kc, vp, vc, qn, kn, group, has_prev, dil):
    head = lax.broadcasted_iota(jnp.int32, (ATT_GH, 1, 1), 0) + (ATT_GH * group + 1)
    slope = jnp.exp(head.astype(F32) * (-8.0 / ATT_HEADS * math.log(2.0)))
    o, lse = _att_block(_att_heads(q), _att_heads(kp), _att_heads(kc), _att_heads(vp), _att_heads(vc), qn, kn, slope, has_prev, dil)
    return _att_lanes(o), _att_lanes(lse)


def _att_specs(group, dil):
    blk = (ATT_SPAN, ATT_GH * ATT_HEAD_DIM)
    cur = lambda which: pl.BlockSpec(blk, lambda r, n: (n, r * 9 + 3 * which + group))
    prev = lambda which: pl.BlockSpec(blk, lambda r, n: (jnp.maximum(n - 1, 0), r * 9 + 3 * which + group))
    out = pl.BlockSpec(blk, lambda r, n: (n, r))
    gain = pl.BlockSpec((ATT_GH, 1, ATT_HEAD_DIM), lambda r, n: (0, 0, 0))
    return [cur(0), prev(1), cur(1), prev(2), cur(2), gain, gain], out, gain


def _att_fwd_call(name, group, dil, pa, qn, kn):
    t = pa.shape[0]
    l = t // dil
    w = ATT_GH * ATT_HEAD_DIM
    ins, out, _ = _att_specs(group, dil)
    pav = pa.reshape(l, dil * pa.shape[1])

    def body(q_ref, kp_ref, kc_ref, vp_ref, vc_ref, qn_ref, kn_ref, o_ref, lse_ref):
        o_ref[...], lse_ref[...] = _att_rows(q_ref[...], kp_ref[...], kc_ref[...], vp_ref[...], vc_ref[...], qn_ref[...],
                                             kn_ref[...], group, pl.program_id(1), dil)

    o, lse = pl.pallas_call(
        body, name=name, grid=(dil, l // ATT_SPAN), in_specs=ins, out_specs=[out, out],
        out_shape=[jax.ShapeDtypeStruct((l, dil * w), F32)] * 2, compiler_params=_params("parallel", "arbitrary"),
    )(pav, pav, pav, pav, pav, qn, kn)
    return o.reshape(t, w), lse.reshape(t, w)


def _att_bwd_call(name, group, dil, pa, qn, kn, do, dlse):
    t = pa.shape[0]
    l = t // dil
    w = ATT_GH * ATT_HEAD_DIM
    ins, out, gain = _att_specs(group, dil)
    pav = pa.reshape(l, dil * pa.shape[1])

    def body(q_ref, kp_ref, kc_ref, vp_ref, vc_ref, qn_ref, kn_ref, do_ref, dlse_ref,
             dq_ref, dkp_ref, dkc_ref, dvp_ref, dvc_ref, dqn_ref, dkn_ref):
        has_prev = pl.program_id(1)
        _, vjp = jax.vjp(lambda *a: _att_rows(*a, group, has_prev, dil), q_ref[...], kp_ref[...], kc_ref[...], vp_ref[...],
                         vc_ref[...], qn_ref[...], kn_ref[...])
        dq, dkp, dkc, dvp, dvc, dqn, dkn = vjp((do_ref[...], dlse_ref[...]))
        dq_ref[...], dkp_ref[...], dkc_ref[...], dvp_ref[...], dvc_ref[...] = dq, dkp, dkc, dvp, dvc

        @pl.when((pl.program_id(0) == 0) & (pl.program_id(1) == 0))
        def _():
            dqn_ref[...] = jnp.zeros_like(dqn_ref)
            dkn_ref[...] = jnp.zeros_like(dkn_ref)

        dqn_ref[...] += dqn
        dkn_ref[...] += dkn

    res = pl.pallas_call(
        body, name=name + "_bwd", grid=(dil, l // ATT_SPAN), in_specs=ins + [out, out],
        out_specs=[out] * 5 + [gain, gain],
        out_shape=[jax.ShapeDtypeStruct((l, dil * w), F32)] * 5 + [jax.ShapeDtypeStruct(qn.shape, F32)] * 2,
        compiler_params=_params("arbitrary", "arbitrary"),
    )(pav, pav, pav, pav, pav, qn, kn, do.reshape(l, dil * w), dlse.reshape(l, dil * w))
    dq, dkp, dkc, dvp, dvc, dqn, dkn = res
    back = lambda g: jnp.pad(g[ATT_SPAN:], ((0, ATT_SPAN), (0, 0)))
    return dq.reshape(t, w), (dkc + back(dkp)).reshape(t, w), (dvc + back(dvp)).reshape(t, w), dqn, dkn


def _att_mix(o1, o2, o3, l1, l2, l3):
    m = jnp.maximum(jnp.maximum(l1, l2), l3)
    e1, e2, e3 = jnp.exp(l1 - m), jnp.exp(l2 - m), jnp.exp(l3 - m)
    s = e1 + e2 + e3
    return (jnp.concatenate([o1 * (e1 / s), o2 * (e2 / s), o3 * (e3 / s)], axis=1),)


def att_branch(name, pa, qn, kn):
    e = ATT_HEAD_DIM
    gains = lambda p, g: p[ATT_GH * g:ATT_GH * (g + 1)].reshape(ATT_GH, 1, e)

    @jax.custom_vjp
    def groups(pa, qn, kn):
        res = [_att_fwd_call(f"{name}_att{g}", g, dil, pa, gains(qn, g), gains(kn, g)) for g, (_, dil) in enumerate(ATT_GROUPS)]
        return tuple(r[0] for r in res) + tuple(r[1] for r in res)

    def groups_fwd(pa, qn, kn):
        return groups(pa, qn, kn), (pa, qn, kn)

    def groups_bwd(res, cts):
        pa, qn, kn = res
        n = len(ATT_GROUPS)
        parts = [_att_bwd_call(f"{name}_att{g}", g, dil, pa, gains(qn, g), gains(kn, g), cts[g], cts[n + g])
                 for g, (_, dil) in enumerate(ATT_GROUPS)]
        d_pa = jnp.concatenate([p[i] for i in range(3) for p in parts], axis=1)
        return (d_pa, jnp.concatenate([p[3] for p in parts]).reshape(qn.shape), jnp.concatenate([p[4] for p in parts]).reshape(kn.shape))

    groups.defvjp(groups_fwd, groups_bwd)
    return rowwise(f"{name}_attmix", _att_mix, groups(pa, qn, kn))[0]


def dn_gates(name, ba, a_log, dt_bias):
    place = lambda p: jnp.pad(p.reshape(1, DN_HEADS), ((0, 0), (DN_HEADS, LANES - 2 * DN_HEADS)))

    def f(x, al, dt):
        lane = lax.broadcasted_iota(jnp.int32, x.shape, 1)
        return (jnp.where(lane < DN_HEADS, _sigmoid(x), -jnp.exp(al) * _softplus(x + dt)),)

    return rowwise(name, f, (ba,), (place(a_log), place(dt_bias)))[0]


def _dn_out(o, z, g):
    parts = []
    for h in range(DN_HEADS):
        sl = slice(h * DN_HEAD_DIM, (h + 1) * DN_HEAD_DIM)
        parts.append(_rms(o[:, sl], g[:, sl]) * _silu(z[:, sl]))
    return (jnp.concatenate(parts, axis=1),)


def _merge(ml, za, zb, zc):
    d = D_MODEL
    return (_sigmoid(ml[:, :d]) * za + _sigmoid(ml[:, d:2 * d]) * zb + _sigmoid(ml[:, 2 * d:]) * zc,)


def add_norm(name, x, pend, scale, gain):
    if pend is None:
        return x, rowwise(name, lambda a, g: (_rms(a, g),), (x,), (gain,))[0]

    def f(a, b, g):
        s = a + scale * b
        return s, _rms(s, g)

    return rowwise(name, f, (x, pend), (gain,))


W_IN_PIECES = (("rgx", 0, 1024), ("gate", 1024, 1024), ("att", 2048, 2304), ("dq", 4352, 1024), ("dk", 5376, 1024),
               ("dv", 6400, 1024), ("dz", 7424, 1024), ("ba", 8448, 16), ("mrg", 8464, 3072))
RG_PAR_BLOCKS = ("lane", "lane", "blk", "lane", "blk", "lane", "lane")


def mixer(name, u, w, p):
    mm = lambda nm, a, wt: mm_rows(nm, a[None], wt[None])
    pr = project_in(name + "_in", u, {k: w["in_" + k] for k, _, _ in W_IN_PIECES})
    ya = colwise(name + "_rg", _rg_block, (pr["rgx"], pr["gate"]),
                 (w["rg_conv_w"], p["rg_conv_b"], _block_diag(p["rg_w_r"]), p["rg_b_r"], _block_diag(p["rg_w_i"]),
                  p["rg_b_i"], p["rg_lambda"]), RG_PAR_BLOCKS, 1)[0]
    yb = att_branch(name, pr["att"], p["att_q_norm"], p["att_k_norm"])
    cw = w["dn_conv_w"]
    cq = colwise(name + "_dnq", _dn_conv_block("q"), (pr["dq"],), (cw[:, :1024],), ("lane",), 1)[0]
    ck = colwise(name + "_dnk", _dn_conv_block("k"), (pr["dk"],), (cw[:, 1024:2048],), ("lane",), 1)[0]
    cv = colwise(name + "_dnv", _dn_conv_block("v"), (pr["dv"],), (cw[:, 2048:],), ("lane",), 1)[0]
    gb = dn_gates(name + "_dngate", pr["ba"], p["dn_a_log"], p["dn_dt_bias"])
    o_dn = dn_core(cq, ck, cv, gb)
    yc = rowwise(name + "_dnout", _dn_out, (o_dn, pr["dz"]), (p["dn_out_norm"].reshape(1, D_MODEL),))[0]
    y = rowwise(name + "_merge", _merge, (pr["mrg"], mm(name + "_ba", ya, w["br_a"]), mm(name + "_bb", yb, w["br_b"]),
                                          mm(name + "_bc", yc, w["br_c"])))[0]
    return mm(name + "_out", y, w["w_out"])


def _loss_call(x, pend, target):
    t, d = x.shape
    tile = _row_tile(t)

    def body(x_ref, p_ref, t_ref, loss_ref, g_ref):
        err = x_ref[...] + 0.5 * p_ref[...] - t_ref[...]
        g_ref[...] = err * (1.0 / d)

        @pl.when(pl.program_id(0) == 0)
        def _():
            loss_ref[...] = jnp.zeros_like(loss_ref)

        loss_ref[...] += jnp.full(loss_ref.shape, 0.5 / d, F32) * jnp.sum(err * err)

    blk = pl.BlockSpec((tile, d), lambda i: (i, 0))
    loss, g = pl.pallas_call(
        body, name="loss", grid=(t // tile,), in_specs=[blk, blk, blk],
        out_specs=[pl.BlockSpec((8, LANES), lambda i: (0, 0)), blk],
        out_shape=[jax.ShapeDtypeStruct((8, LANES), F32), jax.ShapeDtypeStruct((t, d), F32)],
        compiler_params=_params("arbitrary"),
    )(x, pend, target)
    return loss[0, 0], g


@jax.custom_vjp
def loss_op(x, pend, target):
    return _loss_call(x, pend, target)[0]


def _loss_fwd(x, pend, target):
    loss, g = _loss_call(x, pend, target)
    return loss, g


def _loss_bwd(g, ct):
    return ct * g, (0.5 * ct) * g, None


loss_op.defvjp(_loss_fwd, _loss_bwd)


def first_ffn(wg, wu, wd, gain, x):
    x, h = add_norm("L0_n1", x, None, 0.0, gain)
    return x, ffn("L0_f1", h, wg, wu, wd)


def rest_of_step(g, conv, p, x, pend, target):
    scale = 0.5
    w = [split_layer({n: g[n, l] for n, _ in MATRICES if (n, l) in g}, {n: conv[n][l] for n, _ in CONVS}) for l in range(len(p))]
    for l in range(len(p)):
        n = f"L{l}"
        if l > 0:
            x, h = add_norm(n + "_n1", x, pend, scale, p[l]["ffn1_norm"])
            pend, scale = ffn(n + "_f1", h, w[l]["ffn1_w_gate"], w[l]["ffn1_w_up"], w[l]["ffn1_w_down"]), 0.5
        x, h = add_norm(n + "_nm", x, pend, scale, p[l]["mix_norm"])
        pend, scale = mixer(n + "_mx", h, w[l], p[l]), 1.0
        x, h = add_norm(n + "_n2", x, pend, scale, p[l]["ffn2_norm"])
        pend, scale = ffn(n + "_f2", h, w[l]["ffn2_w_gate"], w[l]["ffn2_w_up"], w[l]["ffn2_w_down"]), 0.5
    return loss_op(x, pend, target)


WEIGHT_NAMES = ("ffn1_norm", "ffn1_w_gate", "ffn1_w_up", "ffn1_w_down", "mix_norm", "w_in", "rg_conv_w", "rg_conv_b",
                "rg_w_r", "rg_b_r", "rg_w_i", "rg_b_i", "rg_lambda", "att_q_norm", "att_k_norm", "dn_conv_w", "dn_a_log",
                "dn_dt_bias", "dn_out_norm", "w_branch", "w_out", "ffn2_norm", "ffn2_w_gate", "ffn2_w_up", "ffn2_w_down")
MATRICES = (("ffn1_w_gate", 2), ("ffn1_w_up", 2), ("ffn1_w_down", 1), ("w_in", 2), ("w_branch", 1), ("w_out", 1),
            ("ffn2_w_gate", 2), ("ffn2_w_up", 2), ("ffn2_w_down", 1))
CONVS = (("rg_conv_w", 2), ("dn_conv_w", 2))
SHARD_AXIS = dict(MATRICES + CONVS)
SMALL_NAMES = tuple(n for n in WEIGHT_NAMES if n not in SHARD_AXIS)
ROW_PARAMS = ("ffn1_norm", "mix_norm", "rg_conv_b", "rg_b_r", "rg_b_i", "rg_lambda", "ffn2_norm")
FFN_MATS = ("ffn1_w_gate", "ffn1_w_up", "ffn1_w_down", "ffn2_w_gate", "ffn2_w_up", "ffn2_w_down")
TRANSPOSED_MATS = ("ffn1_w_gate", "ffn1_w_up", "ffn2_w_gate", "ffn2_w_up")
W_IN_SHARD = 2884
GATHER_ORDER = ((("ffn1_w_gate", 0), ("ffn1_w_up", 0), ("ffn1_w_down", 0)),
                (("w_in", 0), ("w_branch", 0), ("w_out", 0)),
                None)
GATHER_IDS = (1, 6, 7)
LATE_MATS = ("ffn2_w_gate", "ffn2_w_up", "ffn2_w_down", "w_out", "w_branch")
EXCHANGE_GROUPS = (lambda n, l: l == 1 and n in LATE_MATS,
                   lambda n, l: (l == 1) != (n in LATE_MATS),
                   lambda n, l: l == 0 and n == "w_in",
                   lambda n, l: l == 0 and n not in LATE_MATS and n != "w_in")


def _shard_minor(a, axis):
    a = jnp.moveaxis(a, 0, axis)
    return a.reshape(a.shape[:axis] + (N_CHIPS * a.shape[axis + 1],) + a.shape[axis + 2:])


def _w_in_piece(g, off, n):
    s = W_IN_SHARD
    parts = [g[j][:, max(off, j * s) - j * s:min(off + n, (j + 1) * s) - j * s]
             for j in range(N_CHIPS) if max(off, j * s) < min(off + n, (j + 1) * s)]
    return jnp.concatenate(parts, axis=1) if len(parts) > 1 else parts[0]


def _w_in_chip_grad(gl, j):
    s = W_IN_SHARD
    parts = [gl["in_" + k][:, max(off, j * s) - off:min(off + n, (j + 1) * s) - off]
             for k, off, n in W_IN_PIECES if max(off, j * s) < min(off + n, (j + 1) * s)]
    return jnp.concatenate(parts, axis=1)


def _layer_weights(g, conv):
    w = {n: g[n] for n in FFN_MATS if n in g}
    w["w_out"] = g["w_out"].reshape(D_MODEL, D_MODEL)
    for k, off, n in W_IN_PIECES:
        piece = _w_in_piece(g["w_in"], off, n)
        w["in_" + k] = jnp.pad(piece, ((0, 0), (0, LANES - n))) if n < LANES else piece
    wb = g["w_branch"].reshape(-1, D_MODEL)
    w["br_a"], w["br_b"], w["br_c"] = wb[:1024], wb[1024:1792], wb[1792:]
    return dict(w, **conv)


def _layer_weight_grads(gl):
    out = {n: gl[n] for n in FFN_MATS if n in gl}
    out["w_out"] = gl["w_out"].reshape(N_CHIPS, -1, D_MODEL)
    out["w_branch"] = jnp.concatenate([gl["br_a"], gl["br_b"], gl["br_c"]], axis=0).reshape(N_CHIPS, -1, D_MODEL)
    out["w_in"] = jnp.stack([_w_in_chip_grad(gl, j) for j in range(N_CHIPS)])
    return out, {n: gl[n] for n, _ in CONVS}


@jax.custom_vjp
def split_layer(g, conv):
    return _layer_weights(g, conv)


split_layer.defvjp(lambda g, conv: (_layer_weights(g, conv), None), lambda _, gw: _layer_weight_grads(gw))


def layer_small(small, l):
    p = {n: small[n][l] for n in SMALL_NAMES}
    for n in ROW_PARAMS:
        p[n] = small[n][l:l + 1]
    return p


def layer_small_grads(gp, small):
    return {n: jnp.stack([g[n] for g in gp]).reshape(small[n].shape) for n in SMALL_NAMES}


HBM_SPEC = pl.BlockSpec(memory_space=pl.ANY)


def _place():
    x, y, c = lax.axis_index("x"), lax.axis_index("y"), lax.axis_index("c")
    other_chips = [(1 - x, y), (x, 1 - y), (1 - x, 1 - y)]
    return x, y, c, 2 * x + y, (x, y, 1 - c), other_chips


def _half_rows(ref, lead, hc):
    hr = ref.shape[-2] // 2
    return ref.at[(*lead, pl.ds(pl.multiple_of(hc * hr, 16), hr), slice(None))]


def _chip_index():
    return (2 * lax.axis_index("x") + lax.axis_index("y")).astype(jnp.int32).reshape(1)


def cast_into_blocks(name, w):
    l, rows, cols = w.shape
    tr = rows // 2

    def body(me_ref, w_ref, *o_refs):
        for a, o_ref in enumerate(o_refs):
            o_ref[...] = w_ref[a:a + 1].astype(BF16)

    return pl.pallas_call(
        body, name=name, out_shape=[jax.ShapeDtypeStruct((N_CHIPS, rows, cols), BF16)] * l,
        grid_spec=pltpu.PrefetchScalarGridSpec(
            num_scalar_prefetch=1, grid=(rows // tr,),
            in_specs=[pl.BlockSpec((l, tr, cols), lambda i, me: (0, i, 0))],
            out_specs=[pl.BlockSpec((1, tr, cols), lambda i, me: (me[0], i, 0))] * l),
        compiler_params=_params("parallel"),
    )(_chip_index(), w)


def _gather_blocks(bufs_in, bufs_out, send_sems, recv_sems):
    n = len(bufs_in)
    x, y, c, me, sibling, chips = _place()

    def copy(s, src, dst, to):
        return pltpu.make_async_remote_copy(src_ref=src, dst_ref=dst, send_sem=send_sems.at[s], recv_sem=recv_sems.at[s],
                                            device_id=to, device_id_type=MESH)

    first, passed = [], []
    for j, (cx, cy) in enumerate(chips):
        for i in range(n):
            cp = copy(6 * i + j, _half_rows(bufs_in[i], (me,), c), _half_rows(bufs_out[i], (me,), c), (cx, cy, c))
            cp.start()
            first.append(cp)
    for j, (cx, cy) in enumerate(chips):
        k = 2 * cx + cy
        for i in range(n):
            copy(6 * i + j, _half_rows(bufs_in[i], (me,), c), _half_rows(bufs_out[i], (k,), c), (cx, cy, c)).wait_recv()
            cp = copy(6 * i + 3 + j, _half_rows(bufs_out[i], (k,), c), _half_rows(bufs_out[i], (k,), c), sibling)
            cp.start()
            passed.append(cp)
    for j, (cx, cy) in enumerate(chips):
        k = 2 * cx + cy
        for i in range(n):
            copy(6 * i + 3 + j, _half_rows(bufs_in[i], (me,), c), _half_rows(bufs_out[i], (k,), 1 - c), sibling).wait_recv()
    for cp in first + passed:
        cp.wait_send()


def _handshake(peers):
    barrier = pltpu.get_barrier_semaphore()
    for p in peers:
        pl.semaphore_signal(barrier, inc=1, device_id=p, device_id_type=MESH)
    pl.semaphore_wait(barrier, len(peers))


def allgather_blocks_sc(name, bufs, collective_id):
    n = len(bufs)
    refs = [jax.new_ref(b, memory_space=pltpu.MemorySpace.HBM) for b in bufs]

    @pl.kernel(mesh=plsc.ScalarSubcoreMesh(axis_name="sequencer", num_cores=1), name=name,
               scratch_types=(pltpu.SemaphoreType.DMA((6 * n,)), pltpu.SemaphoreType.DMA((6 * n,))),
               compiler_params=pltpu.CompilerParams(collective_id=collective_id))
    def launch(send_sems, recv_sems):
        x, y, c, me, sibling, chips = _place()
        _handshake([(cx, cy, c) for cx, cy in chips] + [sibling])
        _gather_blocks(refs, refs, send_sems, recv_sems)

    launch()
    return [jax.freeze(r) for r in refs]


PEER_FLIPS = tuple((fx, fy, fc) for fx in (0, 1) for fy in (0, 1) for fc in (0, 1))[1:]


def exchange_pieces_sc(name, gs, collective_id):
    n = len(gs)

    def body(*refs):
        ins, outs = refs[:n], refs[n:2 * n]
        send_sems, recv_sems = refs[2 * n:]
        x, y, c, me, sibling, chips = _place()
        my_dev = 4 * x + 2 * y + c
        flip = lambda v, f: 1 - v if f else v
        peers = [(flip(x, fx), flip(y, fy), flip(c, fc)) for fx, fy, fc in PEER_FLIPS]
        _handshake(peers)
        sends = []
        for r, (px, py, pc) in enumerate(peers):
            for i in range(n):
                cp = pltpu.make_async_remote_copy(
                    src_ref=_half_rows(ins[i], (2 * px + py,), pc), dst_ref=outs[i].at[my_dev], send_sem=send_sems.at[7 * i + r],
                    recv_sem=recv_sems.at[7 * i + r], device_id=(px, py, pc), device_id_type=MESH)
                cp.start()
                sends.append(cp)
        for r, (px, py, pc) in enumerate(peers):
            for i in range(n):
                pltpu.make_async_remote_copy(
                    src_ref=_half_rows(ins[i], (me,), c), dst_ref=outs[i].at[4 * px + 2 * py + pc], send_sem=send_sems.at[7 * i + r],
                    recv_sem=recv_sems.at[7 * i + r], device_id=(px, py, pc), device_id_type=MESH).wait_recv()
        for cp in sends:
            cp.wait_send()

    return pl.kernel(
        body, name=name, mesh=plsc.ScalarSubcoreMesh(axis_name="sequencer", num_cores=1),
        out_type=[jax.ShapeDtypeStruct((N_DEV, g.shape[1] // 2, g.shape[2]), g.dtype) for g in gs],
        scratch_types=[pltpu.SemaphoreType.DMA((7 * n,)), pltpu.SemaphoreType.DMA((7 * n,))],
        compiler_params=pltpu.CompilerParams(collective_id=collective_id),
    )(*gs)


def sibling_share_halves(name, fs):
    n = len(fs)
    every = (slice(None),)

    def body(*refs):
        ins, outs = refs[:n], refs[n:2 * n]
        send_sems, recv_sems = refs[2 * n:]
        x, y, c, me, sibling, chips = _place()
        sends = []
        for i in range(n):
            cp = pltpu.make_async_remote_copy(src_ref=_half_rows(ins[i], every, c), dst_ref=_half_rows(outs[i], every, c),
                                              send_sem=send_sems.at[i], recv_sem=recv_sems.at[i], device_id=sibling, device_id_type=MESH)
            cp.start()
            sends.append(cp)
        for i in range(n):
            pltpu.make_async_remote_copy(src_ref=_half_rows(ins[i], every, c), dst_ref=_half_rows(outs[i], every, 1 - c),
                                         send_sem=send_sems.at[i], recv_sem=recv_sems.at[i], device_id=sibling,
                                         device_id_type=MESH).wait_recv()
        for cp in sends:
            cp.wait_send()

    return pl.pallas_call(
        body, name=name, out_shape=[jax.ShapeDtypeStruct(f.shape, f.dtype) for f in fs],
        in_specs=[HBM_SPEC] * n, out_specs=[HBM_SPEC] * n, input_output_aliases={i: i for i in range(n)},
        scratch_shapes=[pltpu.SemaphoreType.DMA((n,)), pltpu.SemaphoreType.DMA((n,))],
    )(*fs)


def allgather_small_sc(name, v, collective_id):
    def body(v_ref, out_ref, send_sems, recv_sems, local_sem):
        x, y, c, me, sibling, chips = _place()
        my_dev = 4 * x + 2 * y + c
        flip = lambda a, f: 1 - a if f else a
        peers = [(flip(x, fx), flip(y, fy), flip(c, fc)) for fx, fy, fc in PEER_FLIPS]
        _handshake(peers)
        mine = pltpu.make_async_copy(v_ref, out_ref.at[my_dev], local_sem)
        mine.start()
        sends = []
        for r, peer in enumerate(peers):
            cp = pltpu.make_async_remote_copy(src_ref=v_ref, dst_ref=out_ref.at[my_dev], send_sem=send_sems.at[r],
                                              recv_sem=recv_sems.at[r], device_id=peer, device_id_type=MESH)
            cp.start()
            sends.append(cp)
        for r, (px, py, pc) in enumerate(peers):
            pltpu.make_async_remote_copy(src_ref=v_ref, dst_ref=out_ref.at[4 * px + 2 * py + pc], send_sem=send_sems.at[r],
                                         recv_sem=recv_sems.at[r], device_id=(px, py, pc), device_id_type=MESH).wait_recv()
        for cp in sends:
            cp.wait_send()
        mine.wait()

    return pl.kernel(
        body, name=name, mesh=plsc.ScalarSubcoreMesh(axis_name="sequencer", num_cores=1),
        out_type=jax.ShapeDtypeStruct((N_DEV,) + v.shape, v.dtype),
        scratch_types=[pltpu.SemaphoreType.DMA((7,)), pltpu.SemaphoreType.DMA((7,)), pltpu.SemaphoreType.DMA],
        compiler_params=pltpu.CompilerParams(collective_id=collective_id),
    )(v)


SUM_BLOCK_ELEMS = 512 * 1024


def sum_slabs(name, b):
    k, h, w = b.shape

    def body(b_ref, o_ref):
        acc = b_ref[0].astype(F32)
        for i in range(1, k):
            acc = acc + b_ref[i].astype(F32)
        o_ref[...] = acc

    return pl.pallas_call(
        body, name=name, out_shape=jax.ShapeDtypeStruct((h, w), F32),
        in_specs=[pl.BlockSpec(memory_space=pltpu.VMEM)], out_specs=pl.BlockSpec(memory_space=pltpu.VMEM),
        compiler_params=pltpu.CompilerParams(vmem_limit_bytes=VMEM_LIMIT),
    )(b)


def sum_pieces(name, pieces, gs):
    nl = len(pieces)
    k, h, w = pieces[0].shape
    tile = max(t for t in range(16, h + 1, 16) if h % t == 0 and (t * w <= SUM_BLOCK_ELEMS or t == 16))
    nt = h // tile
    x, y, c = lax.axis_index("x"), lax.axis_index("y"), lax.axis_index("c")
    place = [v.astype(jnp.int32).reshape(1) for v in (c, 2 * x + y, 4 * x + 2 * y + c)]

    assert nl == 2

    def tile_of(l, a, i):
        return i * a if l else i * (1 - a) + (nt - 1) * a

    def body(c_ref, me_ref, dev_ref, *refs):
        p_refs, g_refs, o_ref = refs[:nl], refs[nl:2 * nl], refs[2 * nl]
        my_dev = dev_ref[0]
        for l in range(nl):
            @pl.when(pl.program_id(0) == l)
            def _():
                o_ref[0] = jnp.zeros(o_ref.shape[1:], F32)
                for d in range(k):
                    @pl.when(my_dev == d)
                    def _():
                        o_ref[0] += g_refs[l][0].astype(F32)

                    @pl.when(my_dev != d)
                    def _():
                        o_ref[0] += p_refs[l][d].astype(F32)

    in_specs = [pl.BlockSpec((k, tile, w), functools.partial(lambda l, a, i, cc, me, dev: (0, tile_of(l, a, i), 0), l))
                for l in range(nl)]
    in_specs += [pl.BlockSpec((1, tile, w), functools.partial(lambda l, a, i, cc, me, dev: (me[0], cc[0] * nt + tile_of(l, a, i), 0), l))
                 for l in range(nl)]
    return pl.pallas_call(
        body, name=name, out_shape=jax.ShapeDtypeStruct((nl, 2 * h, w), F32),
        grid_spec=pltpu.PrefetchScalarGridSpec(
            num_scalar_prefetch=3, grid=(nl, nt), in_specs=in_specs,
            out_specs=pl.BlockSpec((1, tile, w), lambda a, i, cc, me, dev: (a, cc[0] * nt + i, 0))),
        compiler_params=_params("arbitrary", "arbitrary"),
    )(*place, *pieces, *gs)


def _adam_block(w, g, m, v):
    m = ADAM_B1 * m + (1.0 - ADAM_B1) * g
    v = ADAM_B2 * v + (1.0 - ADAM_B2) * (g * g)
    m_hat = m / (1.0 - ADAM_B1 ** ADAM_STEP)
    v_hat = v / (1.0 - ADAM_B2 ** ADAM_STEP)
    return -ADAM_LR * (m_hat / (jnp.sqrt(v_hat) + ADAM_EPS) + ADAM_WD * w), m, v


def adamw(name, w, g, m, v):
    shape = w.shape
    cols = shape[-1]
    rows = w.size // cols
    tile = 128 if rows % 128 == 0 else rows
    flat = [a.reshape(rows, cols) for a in (w, g, m, v)]

    def body(w_ref, g_ref, m_ref, v_ref, d_ref, nm_ref, nv_ref):
        d_ref[...], nm_ref[...], nv_ref[...] = _adam_block(w_ref[...], g_ref[...], m_ref[...], v_ref[...])

    blk = pl.BlockSpec((tile, cols), lambda i: (i, 0))
    res = pl.pallas_call(
        body, name=name, grid=(rows // tile,), in_specs=[blk] * 4, out_specs=[blk] * 3,
        out_shape=[jax.ShapeDtypeStruct((rows, cols), F32)] * 3, compiler_params=_params("parallel"),
    )(*flat)
    return tuple(r.reshape(shape) for r in res)


def _pack_small(values):
    flat = jnp.concatenate([v.reshape(-1) for v in values.values()])
    n = flat.shape[0]
    total = -(-n // (8 * LANES)) * (8 * LANES)
    return jnp.pad(flat, (0, total - n)).reshape(-1, LANES)


def _unpack_small(v, shapes):
    flat = v.reshape(-1)
    out, off = {}, 0
    for n, shape in shapes.items():
        sz = int(np.prod(shape))
        out[n] = flat[off:off + sz].reshape(shape)
        off += sz
    return out


def kernel(x, ffn1_norm, ffn1_w_gate, ffn1_w_up, ffn1_w_down, mix_norm, w_in, rg_conv_w, rg_conv_b, rg_w_r, rg_b_r, rg_w_i, rg_b_i, rg_lambda, att_q_norm, att_k_norm, dn_conv_w, dn_a_log, dn_dt_bias, dn_out_norm, w_branch, w_out, ffn2_norm, ffn2_w_gate, ffn2_w_up, ffn2_w_down, loss_target, m_ffn1_norm, m_ffn1_w_gate, m_ffn1_w_up, m_ffn1_w_down, m_mix_norm, m_w_in, m_rg_conv_w, m_rg_conv_b, m_rg_w_r, m_rg_b_r, m_rg_w_i, m_rg_b_i, m_rg_lambda, m_att_q_norm, m_att_k_norm, m_dn_conv_w, m_dn_a_log, m_dn_dt_bias, m_dn_out_norm, m_w_branch, m_w_out, m_ffn2_norm, m_ffn2_w_gate, m_ffn2_w_up, m_ffn2_w_down, v_ffn1_norm, v_ffn1_w_gate, v_ffn1_w_up, v_ffn1_w_down, v_mix_norm, v_w_in, v_rg_conv_w, v_rg_conv_b, v_rg_w_r, v_rg_b_r, v_rg_w_i, v_rg_b_i, v_rg_lambda, v_att_q_norm, v_att_k_norm, v_dn_conv_w, v_dn_a_log, v_dn_dt_bias, v_dn_out_norm, v_w_branch, v_w_out, v_ffn2_norm, v_ffn2_w_gate, v_ffn2_w_up, v_ffn2_w_down):
    given = dict(locals())
    for n in TRANSPOSED_MATS:
        for pre in ("", "m_", "v_"):
            given[pre + n] = jnp.swapaxes(given[pre + n], 1, 2)
    small = {n: given[n] for n in SMALL_NAMES}
    n_layers = ffn1_norm.shape[0]
    mat_names = [n for n, _ in MATRICES]
    conv_names = [n for n, _ in CONVS]

    blocks = {}
    for n in mat_names:
        for l, b in enumerate(cast_into_blocks("cast_" + n, given[n])):
            blocks[n, l] = b
    first, done = {}, []
    for i, wanted in enumerate(GATHER_ORDER[:-1]):
        bufs, _ = lax.optimization_barrier(([blocks[k] for k in wanted], done))
        done = allgather_blocks_sc(f"allgather_{i}", bufs, GATHER_IDS[i])
        first.update(zip(wanted, done))
    rest = {k: b for k, b in blocks.items() if k not in first}
    taps = jnp.concatenate([given[n].reshape(-1) for n in conv_names]).reshape(-1, LANES)
    taps = allgather_small_sc("allgather_taps", taps, 8).reshape(N_CHIPS, 2, -1)[:, 0]
    conv, off = {}, 0
    for n, ax in CONVS:
        sz = given[n].size
        conv[n] = _shard_minor(taps[:, off:off + sz].reshape((N_CHIPS,) + given[n].shape), ax)
        off += sz
    p = [layer_small(small, l) for l in range(n_layers)]

    ffn1_keys = GATHER_ORDER[0]
    (x1, pend), first_vjp = jax.vjp(first_ffn, *[first[k] for k in ffn1_keys], p[0]["ffn1_norm"], x[0])
    keys = list(rest)
    bufs, pend, second = lax.optimization_barrier(([rest[k] for k in keys], pend, [first[k] for k in GATHER_ORDER[1]]))
    gathered = dict(zip(keys, allgather_blocks_sc("allgather_2", bufs, GATHER_IDS[2])))
    gathered.update(zip(GATHER_ORDER[1], second))
    loss, (g_mats, g_conv, gp, gx1, gpend) = jax.value_and_grad(rest_of_step, argnums=(0, 1, 2, 3, 4))(
        gathered, conv, p, x1, pend, loss_target[0])
    *g_ffn1, gp[0]["ffn1_norm"], gx = first_vjp((gx1, gpend))
    g_mats.update(zip(ffn1_keys, g_ffn1))

    pieces = {}
    for i, group in enumerate(EXCHANGE_GROUPS):
        keys = [k for k in g_mats if group(*k)]
        pieces.update(zip(keys, exchange_pieces_sc(f"exchange_{i}", [g_mats[k] for k in keys], 2 + i)))
    halves = {n: sum_pieces("sum_" + n, [pieces[n, l] for l in range(n_layers)], [g_mats[n, l] for l in range(n_layers)])
              for n in mat_names}
    grads = {}
    for tag, names in (("late", [n for n in mat_names if n in LATE_MATS]), ("early", [n for n in mat_names if n not in LATE_MATS])):
        grads.update(zip(names, sibling_share_halves("share_" + tag, [halves[n] for n in names])))

    g_small = dict(layer_small_grads(gp, small), **g_conv, loss=loss.reshape(1))
    packed_small = _pack_small(g_small)
    slabs = allgather_small_sc("allgather_small", packed_small, 9)
    summed =_unpack_small(sum_slabs("sum_small", slabs), {n: g.shape for n, g in g_small.items()})
    chip = 2 * lax.axis_index("x") + lax.axis_index("y")
    for n in SMALL_NAMES:
        grads[n] = summed[n]
    for n, ax in CONVS:
        s = given[n].shape[ax]
        grads[n] = lax.dynamic_slice_in_dim(summed[n], chip * s, s, axis=ax)

    upd = {n: adamw("adamw_" + n, given[n], grads[n], given["m_" + n], given["v_" + n]) for n in WEIGHT_NAMES}
    out = lambda n, a: jnp.swapaxes(a, 1, 2) if n in TRANSPOSED_MATS else a
    return (summed["loss"][0], gx[None], *[out(n, grads[n]) for n in WEIGHT_NAMES], *[out(n, upd[n][0]) for n in WEIGHT_NAMES],
            *[out(n, upd[n][1]) for n in WEIGHT_NAMES], *[out(n, upd[n][2]) for n in WEIGHT_NAMES])
```

```python
import functools
import math

import jax
import jax.numpy as jnp
import numpy as np
from jax import lax
from jax.experimental import pallas as pl
from jax.experimental.pallas import tpu as pltpu
from jax.experimental.pallas import tpu_sc as plsc

F32 = jnp.float32
BF16 = jnp.bfloat16
MESH = pl.DeviceIdType.MESH

D_MODEL = 1024
FFN_DIM = 2816
RG_C = 8.0
ATT_GROUPS = ((128, 1), (512, 4), (2048, 16))
ATT_HEADS = 12
ATT_HEAD_DIM = 64
ATT_SPAN = 128
DN_HEADS = 8
DN_HEAD_DIM = 128
DN_CHUNK = 64
EPS = 1e-6
NEG_INF = -1e30
N_CHIPS = 4
N_DEV = 8

ADAM_LR, ADAM_B1, ADAM_B2, ADAM_EPS, ADAM_WD, ADAM_STEP = 0.001, 0.9, 0.999, 1e-08, 0.01, 10

LANES = 128
VMEM_LIMIT = 56 * 1024 * 1024


def _params(*sem):
    return pltpu.CompilerParams(dimension_semantics=sem or None, vmem_limit_bytes=VMEM_LIMIT)


def _sigmoid(x):
    return 1.0 / (1.0 + jnp.exp(-x))


def _silu(x):
    return x * _sigmoid(x)


def _softplus(x):
    return jnp.maximum(x, 0.0) + jnp.log(1.0 + jnp.exp(-jnp.abs(x)))


def _gelu(x):
    return 0.5 * x * (1.0 + jnp.tanh(math.sqrt(2.0 / math.pi) * (x + 0.044715 * (x * x * x))))


def _neg_expm1(x):
    series = -x * (1.0 + x * (0.5 + x * (1.0 / 6 + x * (1.0 / 24 + x * (1.0 / 120 + x * (1.0 / 720))))))
    return jnp.where(x > -0.25, series, 1.0 - jnp.exp(x))


def _rms(x, g):
    return x * lax.rsqrt(jnp.mean(x * x, axis=-1, keepdims=True) + EPS) * g


_MM_DIMS = {"nn": (((1,), (0,)), ((), ())), "nt": (((1,), (1,)), ((), ())), "tn": (((0,), (0,)), ((), ()))}


def _split(a):
    hi = a.astype(BF16)
    return hi, (a - hi.astype(F32)).astype(BF16)


def _mxu(a, b, form, passes):
    (ca, cb), _ = _MM_DIMS[form]
    if a.ndim == 3:
        dims = (((ca[0] + 1,), (cb[0] + 1,)), ((0,), (0,)))
    else:
        dims = _MM_DIMS[form]
    dg = lambda p, q: lax.dot_general(p, q, dims, preferred_element_type=F32)
    if passes == 1:
        return dg(a.astype(BF16), b.astype(BF16))
    (a_hi, a_lo), (b_hi, b_lo) = _split(a), _split(b)
    return dg(a_hi, b_hi) + (dg(a_hi, b_lo) + dg(a_lo, b_hi))


@functools.partial(jax.custom_vjp, nondiff_argnums=(2, 3))
def _mm(a, b, form, passes):
    return _mxu(a, b, form, passes)


def _mm_fwd(a, b, form, passes):
    return _mxu(a, b, form, passes), (a, b)


def _mm_bwd(form, passes, res, g):
    a, b = res
    if form == "nn":
        return _mm(g, b, "nt", passes), _mm(a, g, "tn", passes)
    if form == "nt":
        return _mm(g, b, "nn", passes), _mm(g, a, "tn", passes)
    return _mm(b, g, "nt", passes), _mm(a, g, "nn", passes)


_mm.defvjp(_mm_fwd, _mm_bwd)


def _dot(a, b):
    return _mm(a, b, "nn", 1)


def _dot_nt(a, b):
    return _mm(a, b, "nt", 1)


def _dot_tn(a, b):
    return _mm(a, b, "tn", 1)


def _dot3(a, b):
    return _mm(a, b, "nn", 3)


def _rows(shape):
    return lax.broadcasted_iota(jnp.int32, shape, len(shape) - 2)


def _roll_down(x, s, fill):
    return jnp.where(_rows(x.shape) >= s, pltpu.roll(x, s, x.ndim - 2), fill)


def _roll_up(x, s, fill):
    n = x.shape[-2]
    return jnp.where(_rows(x.shape) < n - s, pltpu.roll(x, n - s, x.ndim - 2), fill)


@functools.partial(jax.custom_vjp, nondiff_argnums=(1,))
def _shift(x, s):
    return _roll_down(x, s, 0.0)


def _shift_fwd(x, s):
    return _roll_down(x, s, 0.0), None


def _shift_bwd(s, _, g):
    return (_roll_up(g, s, 0.0),)


_shift.defvjp(_shift_fwd, _shift_bwd)


def _causal_conv(x, w):
    return w[0:1] * _shift(x, 3) + w[1:2] * _shift(x, 2) + w[2:3] * _shift(x, 1) + w[3:4] * x


@jax.custom_vjp
def _lin_scan(a, b):
    return _lin_scan_fwd(a, b)[0]


def _lin_scan_fwd(a, b):
    a0 = a
    s = 1
    while s < a.shape[0]:
        b = a * _roll_down(b, s, 0.0) + b
        a = a * _roll_down(a, s, 1.0)
        s *= 2
    return b, (a0, b)


def _lin_scan_bwd(res, g):
    a, h = res
    c = _roll_up(a, 1, 0.0)
    s = 1
    while s < a.shape[0]:
        g = c * _roll_up(g, s, 0.0) + g
        c = c * _roll_up(c, s, 1.0)
        s *= 2
    return g * _roll_down(h, 1, 0.0), g


_lin_scan.defvjp(_lin_scan_fwd, _lin_scan_bwd)


@jax.custom_vjp
def _cumsum_rows(x):
    s = 1
    while s < x.shape[-2]:
        x = x + _roll_down(x, s, 0.0)
        s *= 2
    return x


def _cumsum_rows_fwd(x):
    return _cumsum_rows(x), None


def _cumsum_rows_bwd(_, g):
    s = 1
    while s < g.shape[-2]:
        g = g + _roll_up(g, s, 0.0)
        s *= 2
    return (g,)


_cumsum_rows.defvjp(_cumsum_rows_fwd, _cumsum_rows_bwd)


ROW_BLOCK_BYTES = 14 * 1024 * 1024


def _row_tile(t, width=0):
    for tile in (512, 256):
        if t % tile == 0 and (tile == 256 or tile * width * 4 <= ROW_BLOCK_BYTES):
            return tile
    return t


def _rowwise_fwd_call(name, f, rows, pars, tile):
    t = rows[0].shape[0]
    outs = jax.eval_shape(f, *[jax.ShapeDtypeStruct((tile, r.shape[1]), F32) for r in rows],
                          *[jax.ShapeDtypeStruct(p.shape, F32) for p in pars])
    nr, npar = len(rows), len(pars)

    def body(*refs):
        ins = [r[...] for r in refs[:nr + npar]]
        res = f(*ins)
        for o_ref, o in zip(refs[nr + npar:], res):
            o_ref[...] = o.astype(o_ref.dtype)

    return pl.pallas_call(
        body, name=name, grid=(t // tile,),
        in_specs=[pl.BlockSpec((tile, r.shape[1]), lambda i: (i, 0)) for r in rows]
        + [pl.BlockSpec(p.shape, lambda i: (0, 0)) for p in pars],
        out_specs=[pl.BlockSpec((tile, o.shape[1]), lambda i: (i, 0)) for o in outs],
        out_shape=[jax.ShapeDtypeStruct((t, o.shape[1]), F32) for o in outs],
        compiler_params=_params("parallel"),
    )(*rows, *pars)


def _rowwise_bwd_call(name, f, rows, pars, cts, tile):
    t = rows[0].shape[0]
    nr, npar, nct = len(rows), len(pars), len(cts)

    def body(*refs):
        ins = [r[...] for r in refs[:nr + npar]]
        gs = tuple(r[...] for r in refs[nr + npar:nr + npar + nct])
        outs = refs[nr + npar + nct:]
        _, vjp = jax.vjp(f, *ins)
        d = vjp(gs)
        for o_ref, v in zip(outs[:nr], d[:nr]):
            o_ref[...] = v

        @pl.when(pl.program_id(0) == 0)
        def _():
            for o_ref in outs[nr:]:
                o_ref[...] = jnp.zeros_like(o_ref)

        for o_ref, v in zip(outs[nr:], d[nr:]):
            o_ref[...] += v

    res = pl.pallas_call(
        body, name=name, grid=(t // tile,),
        in_specs=[pl.BlockSpec((tile, r.shape[1]), lambda i: (i, 0)) for r in rows]
        + [pl.BlockSpec(p.shape, lambda i: (0, 0)) for p in pars]
        + [pl.BlockSpec((tile, c.shape[1]), lambda i: (i, 0)) for c in cts],
        out_specs=[pl.BlockSpec((tile, r.shape[1]), lambda i: (i, 0)) for r in rows]
        + [pl.BlockSpec(p.shape, lambda i: (0, 0)) for p in pars],
        out_shape=[jax.ShapeDtypeStruct(r.shape, F32) for r in rows]
        + [jax.ShapeDtypeStruct(p.shape, F32) for p in pars],
        compiler_params=_params("arbitrary"),
    )(*rows, *pars, *cts)
    return tuple(res[:nr]), tuple(res[nr:])


def rowwise(name, f, rows, pars=()):
    outs = jax.eval_shape(f, *[jax.ShapeDtypeStruct((8, r.shape[1]), F32) for r in rows],
                          *[jax.ShapeDtypeStruct(p.shape, F32) for p in pars])
    tile = _row_tile(rows[0].shape[0], 2 * sum(r.shape[1] for r in rows) + sum(o.shape[1] for o in outs))

    @jax.custom_vjp
    def op(rows, pars):
        return tuple(_rowwise_fwd_call(name, f, rows, pars, tile))

    def op_fwd(rows, pars):
        return op(rows, pars), (rows, pars)

    def op_bwd(res, cts):
        return _rowwise_bwd_call(name + "_bwd", f, res[0], res[1], tuple(cts), tile)

    op.defvjp(op_fwd, op_bwd)
    return op(tuple(rows), tuple(pars))


MM_TM = 512


def _tile_of(n, cap):
    best = None
    for c in range(LANES, min(n, cap) + 1, LANES):
        if n % c == 0:
            best = c
    return best or n


def _mmc_fwd(name, h, w):
    m, k = h.shape
    j, _, n = w.shape
    tm, tn = MM_TM, _tile_of(n, 1408)

    def body(h_ref, w_ref, o_ref):
        o_ref[0] = _dot(h_ref[...], w_ref[0])

    return pl.pallas_call(
        body, name=name, grid=(m // tm, j, n // tn),
        in_specs=[pl.BlockSpec((tm, k), lambda i, b, c: (i, 0)), pl.BlockSpec((1, k, tn), lambda i, b, c: (b, 0, c))],
        out_specs=pl.BlockSpec((1, tm, tn), lambda i, b, c: (b, i, c)),
        out_shape=jax.ShapeDtypeStruct((j, m, n), F32),
        compiler_params=_params("parallel", "parallel", "parallel"),
    )(h, w)


def _mmc_dw(name, h, dy):
    m, k = h.shape
    j, _, n = dy.shape
    tk, tn = _tile_of(k, 512), _tile_of(n, 1152)

    def body(h_ref, dy_ref, o_ref):
        o_ref[0] = _dot_tn(h_ref[...], dy_ref[0]).astype(BF16)

    return pl.pallas_call(
        body, name=name, grid=(j, k // tk, n // tn),
        in_specs=[pl.BlockSpec((m, tk), lambda b, i, c: (0, i)), pl.BlockSpec((1, m, tn), lambda b, i, c: (b, 0, c))],
        out_specs=pl.BlockSpec((1, tk, tn), lambda b, i, c: (b, i, c)),
        out_shape=jax.ShapeDtypeStruct((j, k, n), BF16),
        compiler_params=_params("parallel", "parallel", "parallel"),
    )(h, dy)


PROJ_GROUP_COLS = 4608


def _proj_dh(name, dys, ws, acc):
    m, k = dys[0].shape[0], ws[0].shape[0]
    n, tm = len(dys), 256

    def body(*refs):
        dy_refs, w_refs, rest = refs[:n], refs[n:2 * n], refs[2 * n:]
        total = _dot_nt(dy_refs[0][...], w_refs[0][...])
        for dy_ref, w_ref in zip(dy_refs[1:], w_refs[1:]):
            total = total + _dot_nt(dy_ref[...], w_ref[...])
        if acc is not None:
            total = total + rest[0][...]
        rest[-1][...] = total

    row = lambda width: pl.BlockSpec((tm, width), lambda i: (i, 0))
    return pl.pallas_call(
        body, name=name, grid=(m // tm,),
        in_specs=[row(d.shape[1]) for d in dys] + [pl.BlockSpec(w.shape, lambda i: (0, 0)) for w in ws] + ([row(k)] if acc is not None else []),
        out_specs=row(k), out_shape=jax.ShapeDtypeStruct((m, k), F32), compiler_params=_params("parallel"),
    )(*dys, *ws, *([acc] if acc is not None else []))


def project_in(name, h, ws):
    keys = list(ws)

    @jax.custom_vjp
    def op(h, ws):
        return {p: _mmc_fwd(f"{name}_{p}", h, ws[p][None])[0] for p in keys}

    def op_fwd(h, ws):
        return op(h, ws), (h, ws)

    def op_bwd(res, dys):
        h, ws = res
        groups, cols = [[]], 0
        for p in keys:
            if groups[-1] and cols + ws[p].shape[1] > PROJ_GROUP_COLS:
                groups.append([])
                cols = 0
            groups[-1].append(p)
            cols += ws[p].shape[1]
        dh = None
        for i, group in enumerate(groups):
            dh = _proj_dh(f"{name}_dh{i}", [dys[p] for p in group], [ws[p] for p in group], dh)
        return dh, {p: _mmc_dw(f"{name}_{p}_dw", h, dys[p][None])[0] for p in keys}

    op.defvjp(op_fwd, op_bwd)
    return op(h, ws)


def _ffn_up(name, h, wt):
    m, k = h.shape
    j, n, _ = wt.shape
    tm = MM_TM

    def body(h_ref, w_ref, o_ref):
        o_ref[0] = _dot_nt(h_ref[...], w_ref[0])

    return pl.pallas_call(
        body, name=name, grid=(m // tm, j),
        in_specs=[pl.BlockSpec((tm, k), lambda i, b: (i, 0)), pl.BlockSpec((1, n, k), lambda i, b: (b, 0, 0))],
        out_specs=pl.BlockSpec((1, tm, n), lambda i, b: (b, i, 0)),
        out_shape=jax.ShapeDtypeStruct((j, m, n), F32), compiler_params=_params("parallel", "parallel"),
    )(h, wt)


def _ffn_down(name, g, u, wd):
    j, m, n = g.shape
    d = wd.shape[2]
    tm = MM_TM

    def body(g_ref, u_ref, w_ref, o_ref):
        part = _dot(_silu(g_ref[0]) * u_ref[0], w_ref[0])

        @pl.when(pl.program_id(1) == 0)
        def _():
            o_ref[...] = part

        @pl.when(pl.program_id(1) > 0)
        def _():
            o_ref[...] += part

    act = pl.BlockSpec((1, tm, n), lambda i, b: (b, i, 0))
    return pl.pallas_call(
        body, name=name, grid=(m // tm, j),
        in_specs=[act, act, pl.BlockSpec((1, n, d), lambda i, b: (b, 0, 0))],
        out_specs=pl.BlockSpec((tm, d), lambda i, b: (i, 0)),
        out_shape=jax.ShapeDtypeStruct((m, d), F32), compiler_params=_params("parallel", "arbitrary"),
    )(g, u, wd)


def _ffn_down_bwd(name, dy, g, u, wd):
    j, m, n = g.shape
    d = wd.shape[2]
    tm = MM_TM

    def body(dy_ref, g_ref, u_ref, w_ref, dg_ref, du_ref):
        da = _dot_nt(dy_ref[...], w_ref[0])
        gv = g_ref[0]
        s = _sigmoid(gv)
        dg_ref[0] = da * u_ref[0] * (s * (1.0 + gv * (1.0 - s)))
        du_ref[0] = da * (gv * s)

    act = pl.BlockSpec((1, tm, n), lambda i, b: (b, i, 0))
    return pl.pallas_call(
        body, name=name, grid=(m // tm, j),
        in_specs=[pl.BlockSpec((tm, d), lambda i, b: (i, 0)), act, act, pl.BlockSpec((1, n, d), lambda i, b: (b, 0, 0))],
        out_specs=[act, act], out_shape=[jax.ShapeDtypeStruct((j, m, n), F32)] * 2,
        compiler_params=_params("parallel", "parallel"),
    )(dy, g, u, wd)


def _ffn_down_dw(name, g, u, dy):
    j, m, n = g.shape
    d = dy.shape[1]
    tn = _tile_of(d, 512)

    def body(g_ref, u_ref, dy_ref, o_ref):
        o_ref[0] = _dot_tn(_silu(g_ref[0]) * u_ref[0], dy_ref[...]).astype(BF16)

    act = pl.BlockSpec((1, m, n), lambda b, c: (b, 0, 0))
    return pl.pallas_call(
        body, name=name, grid=(j, d // tn),
        in_specs=[act, act, pl.BlockSpec((m, tn), lambda b, c: (0, c))],
        out_specs=pl.BlockSpec((1, n, tn), lambda b, c: (b, 0, c)),
        out_shape=jax.ShapeDtypeStruct((j, n, d), BF16), compiler_params=_params("parallel", "parallel"),
    )(g, u, dy)


def _ffn_up_dh(name, dg, du, wg, wu):
    j, m, n = dg.shape
    k = wg.shape[2]
    tm = MM_TM

    def body(dg_ref, du_ref, wg_ref, wu_ref, o_ref):
        part = _dot(dg_ref[0], wg_ref[0]) + _dot(du_ref[0], wu_ref[0])

        @pl.when(pl.program_id(1) == 0)
        def _():
            o_ref[...] = part

        @pl.when(pl.program_id(1) > 0)
        def _():
            o_ref[...] += part

    act = pl.BlockSpec((1, tm, n), lambda i, b: (b, i, 0))
    wsp = pl.BlockSpec((1, n, k), lambda i, b: (b, 0, 0))
    return pl.pallas_call(
        body, name=name, grid=(m // tm, j), in_specs=[act, act, wsp, wsp],
        out_specs=pl.BlockSpec((tm, k), lambda i, b: (i, 0)),
        out_shape=jax.ShapeDtypeStruct((m, k), F32), compiler_params=_params("parallel", "arbitrary"),
    )(dg, du, wg, wu)


def _ffn_up_dw(name, dy, h):
    j, m, n = dy.shape
    k = h.shape[1]
    tk = _tile_of(k, 512)

    def body(dy_ref, h_ref, o_ref):
        o_ref[0] = _dot_tn(dy_ref[0], h_ref[...]).astype(BF16)

    return pl.pallas_call(
        body, name=name, grid=(j, k // tk),
        in_specs=[pl.BlockSpec((1, m, n), lambda b, i: (b, 0, 0)), pl.BlockSpec((m, tk), lambda b, i: (0, i))],
        out_specs=pl.BlockSpec((1, n, tk), lambda b, i: (b, 0, i)),
        out_shape=jax.ShapeDtypeStruct((j, n, k), BF16), compiler_params=_params("parallel", "parallel"),
    )(dy, h)


def ffn(name, h, wg, wu, wd):
    @jax.custom_vjp
    def op(h, wg, wu, wd):
        return _ffn_down(name + "_d", _ffn_up(name + "_g", h, wg), _ffn_up(name + "_u", h, wu), wd)

    def op_fwd(h, wg, wu, wd):
        g, u = _ffn_up(name + "_g", h, wg), _ffn_up(name + "_u", h, wu)
        return _ffn_down(name + "_d", g, u, wd), (h, g, u, wg, wu, wd)

    def op_bwd(res, dy):
        h, g, u, wg, wu, wd = res
        dg, du = _ffn_down_bwd(name + "_d_bwd", dy, g, u, wd)
        return (_ffn_up_dh(name + "_dh", dg, du, wg, wu), _ffn_up_dw(name + "_g_dw", dg, h), _ffn_up_dw(name + "_u_dw", du, h),
                _ffn_down_dw(name + "_d_dw", g, u, dy))

    op.defvjp(op_fwd, op_bwd)
    return op(h, wg, wu, wd)


def _mmr_fwd(name, a, w):
    j, m, n = a.shape
    nn = w.shape[2]
    tm, tn = MM_TM, _tile_of(nn, 1024)

    def body(a_ref, w_ref, o_ref):
        part = _dot(a_ref[0], w_ref[0])

        @pl.when(pl.program_id(2) == 0)
        def _():
            o_ref[...] = part

        @pl.when(pl.program_id(2) > 0)
        def _():
            o_ref[...] += part

    return pl.pallas_call(
        body, name=name, grid=(m // tm, nn // tn, j),
        in_specs=[pl.BlockSpec((1, tm, n), lambda i, c, b: (b, i, 0)), pl.BlockSpec((1, n, tn), lambda i, c, b: (b, 0, c))],
        out_specs=pl.BlockSpec((tm, tn), lambda i, c, b: (i, c)),
        out_shape=jax.ShapeDtypeStruct((m, nn), F32),
        compiler_params=_params("parallel", "parallel", "arbitrary"),
    )(a, w)


def _mmr_da(name, dy, w):
    m, nn = dy.shape
    j, n, _ = w.shape
    tm = MM_TM

    def body(dy_ref, w_ref, o_ref):
        o_ref[0] = _dot_nt(dy_ref[...], w_ref[0])

    return pl.pallas_call(
        body, name=name, grid=(m // tm, j),
        in_specs=[pl.BlockSpec((tm, nn), lambda i, b: (i, 0)), pl.BlockSpec((1, n, nn), lambda i, b: (b, 0, 0))],
        out_specs=pl.BlockSpec((1, tm, n), lambda i, b: (b, i, 0)),
        out_shape=jax.ShapeDtypeStruct((j, m, n), F32),
        compiler_params=_params("parallel", "parallel"),
    )(dy, w)


def _mmr_dw(name, a, dy):
    j, m, n = a.shape
    nn = dy.shape[1]
    tn = _tile_of(nn, 512)

    def body(a_ref, dy_ref, o_ref):
        o_ref[0] = _dot_tn(a_ref[0], dy_ref[...]).astype(BF16)

    return pl.pallas_call(
        body, name=name, grid=(j, nn // tn),
        in_specs=[pl.BlockSpec((1, m, n), lambda b, c: (b, 0, 0)), pl.BlockSpec((m, tn), lambda b, c: (0, c))],
        out_specs=pl.BlockSpec((1, n, tn), lambda b, c: (b, 0, c)),
        out_shape=jax.ShapeDtypeStruct((j, n, nn), BF16),
        compiler_params=_params("parallel", "parallel"),
    )(a, dy)


def mm_rows(name, a, w):
    @jax.custom_vjp
    def op(a, w):
        return _mmr_fwd(name, a, w)

    def op_fwd(a, w):
        return op(a, w), (a, w)

    def op_bwd(res, dy):
        a, w = res
        return _mmr_da(name + "_da", dy, w), _mmr_dw(name + "_dw", a, dy)

    op.defvjp(op_fwd, op_bwd)
    return op(a, w)


def _colwise_specs(cols, pars, par_block):
    t = cols[0].shape[0]
    specs = [pl.BlockSpec((t, LANES), lambda j: (0, j)) for _ in cols]
    for p, blk in zip(pars, par_block):
        if blk == "lane":
            specs.append(pl.BlockSpec((p.shape[0], LANES), lambda j: (0, j)))
        else:
            specs.append(pl.BlockSpec((1,) + p.shape[1:], lambda j: (j, 0, 0)))
    return specs


def _colwise_fwd_call(name, f, cols, pars, par_block, n_out):
    t, c = cols[0].shape
    nc, npar = len(cols), len(pars)

    def body(*refs):
        ins = [r[...] for r in refs[:nc]] + [r[...] if b == "lane" else r[0] for r, b in zip(refs[nc:nc + npar], par_block)]
        res = f(*ins)
        for o_ref, o in zip(refs[nc + npar:], res):
            o_ref[...] = o

    return pl.pallas_call(
        body, name=name, grid=(c // LANES,),
        in_specs=_colwise_specs(cols, pars, par_block),
        out_specs=[pl.BlockSpec((t, LANES), lambda j: (0, j)) for _ in range(n_out)],
        out_shape=[jax.ShapeDtypeStruct((t, c), F32) for _ in range(n_out)],
        compiler_params=_params("parallel"),
    )(*cols, *pars)


def _colwise_bwd_call(name, f, cols, pars, par_block, cts):
    t, c = cols[0].shape
    nc, npar, nct = len(cols), len(pars), len(cts)

    def body(*refs):
        ins = [r[...] for r in refs[:nc]] + [r[...] if b == "lane" else r[0] for r, b in zip(refs[nc:nc + npar], par_block)]
        gs = tuple(r[...] for r in refs[nc + npar:nc + npar + nct])
        outs = refs[nc + npar + nct:]
        _, vjp = jax.vjp(f, *ins)
        d = vjp(gs)
        for o_ref, v in zip(outs[:nc], d[:nc]):
            o_ref[...] = v
        for o_ref, v, b in zip(outs[nc:], d[nc:], par_block):
            if b == "lane":
                o_ref[...] = v
            else:
                o_ref[0] = v

    res = pl.pallas_call(
        body, name=name, grid=(c // LANES,),
        in_specs=_colwise_specs(cols, pars, par_block) + [pl.BlockSpec((t, LANES), lambda j: (0, j)) for _ in cts],
        out_specs=_colwise_specs(cols, pars, par_block),
        out_shape=[jax.ShapeDtypeStruct(v.shape, F32) for v in (*cols, *pars)],
        compiler_params=_params("parallel"),
    )(*cols, *pars, *cts)
    return tuple(res[:nc]), tuple(res[nc:])


def colwise(name, f, cols, pars, par_block, n_out):
    @jax.custom_vjp
    def op(cols, pars):
        return tuple(_colwise_fwd_call(name, f, cols, pars, par_block, n_out))

    def op_fwd(cols, pars):
        return op(cols, pars), (cols, pars)

    def op_bwd(res, cts):
        return _colwise_bwd_call(name + "_bwd", f, res[0], res[1], par_block, tuple(cts))

    op.defvjp(op_fwd, op_bwd)
    return op(tuple(cols), tuple(pars))


def _rg_block(x, gate, cw, cb, wr, br, wi, bi, lam):
    xa = _causal_conv(x, cw) + cb
    r = _sigmoid(_dot(xa, wr) + br)
    i = _sigmoid(_dot(xa, wi) + bi)
    log_a = -RG_C * r * _softplus(-lam)
    a = jnp.exp(log_a)
    b = jnp.sqrt(_neg_expm1(2.0 * log_a)) * (i * xa)
    return (_lin_scan(a, b) * _gelu(gate),)


def _dn_conv_block(mode):
    def f(x, cw):
        c = _silu(_causal_conv(x, cw))
        if mode == "v":
            return (c,)
        c = c * lax.rsqrt(jnp.sum(c * c, axis=-1, keepdims=True) + EPS)
        return (c * (DN_HEAD_DIM ** -0.5),) if mode == "q" else (c,)
    return f


def _block_diag(w):
    w = w.reshape(8, 2, 64, 64)
    z = jnp.zeros((8, 64, 64), w.dtype)
    top = jnp.concatenate([w[:, 0], z], axis=2)
    bot = jnp.concatenate([z, w[:, 1]], axis=2)
    return jnp.concatenate([top, bot], axis=1)


DN_HP = 8


def _dn_block(S, qw, kw, vw, gb, h0, tinv=None):
    hp, hd = S.shape[0], DN_HEAD_DIM
    heads = lambda a: jnp.concatenate([a[None, :, j * hd:(j + 1) * hd] for j in range(hp)], axis=0)
    lane = lax.broadcasted_iota(jnp.int32, gb.shape, 1)
    col = lambda i: jnp.sum(jnp.where(lane == i, gb, 0.0), axis=1, keepdims=True)[None]
    beta = jnp.concatenate([col(h0 + j) for j in range(hp)], axis=0)
    g = jnp.concatenate([col(h0 + j + DN_HEADS) for j in range(hp)], axis=0)
    s_new, o, tinv = _dn_step(S, heads(qw), heads(kw), heads(vw), beta, g, tinv)
    return s_new, jnp.concatenate([o[j:j + 1].reshape(o.shape[1:]) for j in range(hp)], axis=1), tinv


@jax.custom_vjp
def _unit_lower_inverse(a):
    c = a.shape[-1]
    eye = (lax.broadcasted_iota(jnp.int32, (c, c), 0) == lax.broadcasted_iota(jnp.int32, (c, c), 1)).astype(F32)
    p = -a
    tinv = eye + p
    for _ in range(5):
        p = _dot3(p, p)
        tinv = tinv + _dot3(tinv, p)
    return tinv


def _unit_lower_inverse_fwd(a):
    t = _unit_lower_inverse(a)
    return t, t


def _unit_lower_inverse_bwd(t, g):
    return (-_mm(_mm(t, g, "tn", 3), t, "nt", 3),)


_unit_lower_inverse.defvjp(_unit_lower_inverse_fwd, _unit_lower_inverse_bwd)


@jax.custom_vjp
def _known_inverse(a, t):
    return t


_known_inverse.defvjp(lambda a, t: (t, t), lambda t, g: (_unit_lower_inverse_bwd(t, g)[0], jnp.zeros_like(t)))


def _dn_step(S, q, k, v, beta, g, tinv=None):
    c = DN_CHUNK
    ri = lax.broadcasted_iota(jnp.int32, (c, c), 0)
    ci = lax.broadcasted_iota(jnp.int32, (c, c), 1)
    incl, strict = ri >= ci, ri > ci
    gam = _cumsum_rows(g)
    gam_row = jnp.sum(jnp.where(ri <= ci, g, 0.0), axis=-2, keepdims=True)
    gam_last = jnp.sum(g, axis=-2, keepdims=True)
    decay = jnp.where(incl, jnp.exp(jnp.where(incl, gam - gam_row, 0.0)), 0.0)
    kb = k * beta
    vb = v * beta
    a = jnp.where(strict, _dot_nt(kb, k) * decay, 0.0)
    tinv = _unit_lower_inverse(a) if tinv is None else _known_inverse(a, tinv)
    e_gam = jnp.exp(gam)
    u0 = _dot3(tinv, vb)
    wk = _dot3(tinv, kb * e_gam)
    qk = jnp.where(incl, _dot_nt(q, k) * decay, 0.0)
    q_dec = q * e_gam
    k_dec = k * jnp.exp(gam_last - gam)
    u = u0 - _dot(wk, S)
    o = _dot(q_dec, S) + _dot(qk, u)
    s_new = S * jnp.exp(gam_last) + _dot_tn(k_dec, u)
    return s_new, o, tinv


def _dn_fwd_call(q, k, v, gb):
    t, w = q.shape
    n, hp, hd, c = t // DN_CHUNK, DN_HP, DN_HEAD_DIM, DN_CHUNK

    def body(q_ref, k_ref, v_ref, gb_ref, o_ref, s0_ref, ti_ref, s_scr):
        @pl.when(pl.program_id(1) == 0)
        def _():
            s_scr[...] = jnp.zeros_like(s_scr)

        s_old = s_scr[...]
        s0_ref[:, 0] = s_old
        s_new, o, tinv = _dn_block(s_old, q_ref[...], k_ref[...], v_ref[...], gb_ref[...], pl.program_id(0) * hp)
        o_ref[...] = o
        ti_ref[:, 0] = tinv
        s_scr[...] = s_new

    blk = pl.BlockSpec((c, hp * hd), lambda g, i: (i, g))
    return pl.pallas_call(
        body, name="dn_core", grid=(DN_HEADS // hp, n),
        in_specs=[blk, blk, blk, pl.BlockSpec((c, LANES), lambda g, i: (i, 0))],
        out_specs=[blk, pl.BlockSpec((hp, 1, hd, hd), lambda g, i: (g, i, 0, 0)), pl.BlockSpec((hp, 1, c, c), lambda g, i: (g, i, 0, 0))],
        out_shape=[jax.ShapeDtypeStruct((t, w), F32), jax.ShapeDtypeStruct((DN_HEADS, n, hd, hd), F32),
                   jax.ShapeDtypeStruct((DN_HEADS, n, c, c), F32)],
        scratch_shapes=[pltpu.VMEM((hp, hd, hd), F32)],
        compiler_params=_params("parallel", "arbitrary"),
    )(q, k, v, gb)


def _dn_bwd_call(q, k, v, gb, s0, ti, do):
    t, w = q.shape
    n, hp, hd, c = t // DN_CHUNK, DN_HP, DN_HEAD_DIM, DN_CHUNK
    ng = DN_HEADS // hp

    def body(q_ref, k_ref, v_ref, gb_ref, s0_ref, ti_ref, do_ref, dq_ref, dk_ref, dv_ref, dgb_ref, ds_scr):
        @pl.when(pl.program_id(1) == 0)
        def _():
            ds_scr[...] = jnp.zeros_like(ds_scr)

        h0, tinv = pl.program_id(0) * hp, ti_ref[:, 0]
        _, vjp = jax.vjp(lambda *a: _dn_block(*a, h0, tinv)[:2], s0_ref[:, 0], q_ref[...], k_ref[...], v_ref[...], gb_ref[...])
        ds, dq, dk, dv, dgb = vjp((ds_scr[...], do_ref[...]))
        ds_scr[...] = ds
        dq_ref[...], dk_ref[...], dv_ref[...] = dq, dk, dv
        dgb_ref[0] = dgb

    blk = pl.BlockSpec((c, hp * hd), lambda g, i: (n - 1 - i, g))
    res = pl.pallas_call(
        body, name="dn_core_bwd", grid=(ng, n),
        in_specs=[blk, blk, blk, pl.BlockSpec((c, LANES), lambda g, i: (n - 1 - i, 0)),
                  pl.BlockSpec((hp, 1, hd, hd), lambda g, i: (g, n - 1 - i, 0, 0)),
                  pl.BlockSpec((hp, 1, c, c), lambda g, i: (g, n - 1 - i, 0, 0)), blk],
        out_specs=[blk, blk, blk, pl.BlockSpec((1, c, LANES), lambda g, i: (g, n - 1 - i, 0))],
        out_shape=[jax.ShapeDtypeStruct((t, w), F32)] * 3 + [jax.ShapeDtypeStruct((ng, t, LANES), F32)],
        scratch_shapes=[pltpu.VMEM((hp, hd, hd), F32)],
        compiler_params=_params("parallel", "arbitrary"),
    )(q, k, v, gb, s0, ti, do)
    return res[0], res[1], res[2], jnp.sum(res[3], axis=0)


@jax.custom_vjp
def dn_core(q, k, v, gb):
    return _dn_fwd_call(q, k, v, gb)[0]


def _dn_core_fwd(q, k, v, gb):
    o, s0, ti = _dn_fwd_call(q, k, v, gb)
    return o, (q, k, v, gb, s0, ti)


def _dn_core_bwd(res, do):
    return _dn_bwd_call(*res, do)


dn_core.defvjp(_dn_core_fwd, _dn_core_bwd)


ATT_GH = 4


def _att_block(q, kp, kc, vp, vc, qn, kn, slope, has_prev, dil):
    s = ATT_SPAN
    qh = _rms(q, qn) * (ATT_HEAD_DIM ** -0.5)
    qi = lax.broadcasted_iota(jnp.int32, (s, s), 0)
    kj = lax.broadcasted_iota(jnp.int32, (s, s), 1)
    d_p = qi + s - kj
    d_c = qi - kj
    s_p = _dot_nt(qh, _rms(kp, kn)) - slope * (d_p * dil).astype(F32)
    s_c = _dot_nt(qh, _rms(kc, kn)) - slope * (d_c * dil).astype(F32)
    s_p = jnp.where((d_p <= s) & (has_prev > 0), s_p, NEG_INF)
    s_c = jnp.where(d_c >= 0, s_c, NEG_INF)
    m = lax.stop_gradient(jnp.maximum(jnp.max(s_p, axis=-1, keepdims=True), jnp.max(s_c, axis=-1, keepdims=True)))
    p_p = jnp.exp(s_p - m)
    p_c = jnp.exp(s_c - m)
    den = jnp.sum(p_p, axis=-1, keepdims=True) + jnp.sum(p_c, axis=-1, keepdims=True)
    o = _dot(p_p / den, vp) + _dot(p_c / den, vc)
    lse = m + jnp.log(den)
    return o, jnp.broadcast_to(lse, o.shape)


def _att_heads(a):
    e = ATT_HEAD_DIM
    return jnp.concatenate([a[None, :, h * e:(h + 1) * e] for h in range(ATT_GH)], axis=0)


def _att_lanes(a):
    return jnp.concatenate([a[h:h + 1].reshape(a.shape[1:]) for h in range(ATT_GH)], axis=1)


def _att_rows(q, kp, kc, vp, vc, qn, kn, group, has_prev, dil):
    head = lax.broadcasted_iota(jnp.int32, (ATT_GH, 1, 1), 0) + (ATT_GH * group + 1)
    slope = jnp.exp(head.astype(F32) * (-8.0 / ATT_HEADS * math.log(2.0)))
    o, lse = _att_block(_att_heads(q), _att_heads(kp), _att_heads(kc), _att_heads(vp), _att_heads(vc), qn, kn, slope, has_prev, dil)
    return _att_lanes(o), _att_lanes(lse)


def _att_specs(group, dil):
    blk = (ATT_SPAN, ATT_GH * ATT_HEAD_DIM)
    cur = lambda which: pl.BlockSpec(blk, lambda r, n: (n, r * 9 + 3 * which + group))
    prev = lambda which: pl.BlockSpec(blk, lambda r, n: (jnp.maximum(n - 1, 0), r * 9 + 3 * which + group))
    out = pl.BlockSpec(blk, lambda r, n: (n, r))
    gain = pl.BlockSpec((ATT_GH, 1, ATT_HEAD_DIM), lambda r, n: (0, 0, 0))
    return [cur(0), prev(1), cur(1), prev(2), cur(2), gain, gain], out, gain


def _att_fwd_call(name, group, dil, pa, qn, kn):
    t = pa.shape[0]
    l = t // dil
    w = ATT_GH * ATT_HEAD_DIM
    ins, out, _ = _att_specs(group, dil)
    pav = pa.reshape(l, dil * pa.shape[1])

    def body(q_ref, kp_ref, kc_ref, vp_ref, vc_ref, qn_ref, kn_ref, o_ref, lse_ref):
        o_ref[...], lse_ref[...] = _att_rows(q_ref[...], kp_ref[...], kc_ref[...], vp_ref[...], vc_ref[...], qn_ref[...],
                                             kn_ref[...], group, pl.program_id(1), dil)

    o, lse = pl.pallas_call(
        body, name=name, grid=(dil, l // ATT_SPAN), in_specs=ins, out_specs=[out, out],
        out_shape=[jax.ShapeDtypeStruct((l, dil * w), F32)] * 2, compiler_params=_params("parallel", "arbitrary"),
    )(pav, pav, pav, pav, pav, qn, kn)
    return o.reshape(t, w), lse.reshape(t, w)


def _att_bwd_call(name, group, dil, pa, qn, kn, do, dlse):
    t = pa.shape[0]
    l = t // dil
    w = ATT_GH * ATT_HEAD_DIM
    ins, out, gain = _att_specs(group, dil)
    pav = pa.reshape(l, dil * pa.shape[1])

    def body(q_ref, kp_ref, kc_ref, vp_ref, vc_ref, qn_ref, kn_ref, do_ref, dlse_ref,
             dq_ref, dkp_ref, dkc_ref, dvp_ref, dvc_ref, dqn_ref, dkn_ref):
        has_prev = pl.program_id(1)
        _, vjp = jax.vjp(lambda *a: _att_rows(*a, group, has_prev, dil), q_ref[...], kp_ref[...], kc_ref[...], vp_ref[...],
                         vc_ref[...], qn_ref[...], kn_ref[...])
        dq, dkp, dkc, dvp, dvc, dqn, dkn = vjp((do_ref[...], dlse_ref[...]))
        dq_ref[...], dkp_ref[...], dkc_ref[...], dvp_ref[...], dvc_ref[...] = dq, dkp, dkc, dvp, dvc

        @pl.when((pl.program_id(0) == 0) & (pl.program_id(1) == 0))
        def _():
            dqn_ref[...] = jnp.zeros_like(dqn_ref)
            dkn_ref[...] = jnp.zeros_like(dkn_ref)

        dqn_ref[...] += dqn
        dkn_ref[...] += dkn

    res = pl.pallas_call(
        body, name=name + "_bwd", grid=(dil, l // ATT_SPAN), in_specs=ins + [out, out],
        out_specs=[out] * 5 + [gain, gain],
        out_shape=[jax.ShapeDtypeStruct((l, dil * w), F32)] * 5 + [jax.ShapeDtypeStruct(qn.shape, F32)] * 2,
        compiler_params=_params("arbitrary", "arbitrary"),
    )(pav, pav, pav, pav, pav, qn, kn, do.reshape(l, dil * w), dlse.reshape(l, dil * w))
    dq, dkp, dkc, dvp, dvc, dqn, dkn = res
    back = lambda g: jnp.pad(g[ATT_SPAN:], ((0, ATT_SPAN), (0, 0)))
    return dq.reshape(t, w), (dkc + back(dkp)).reshape(t, w), (dvc + back(dvp)).reshape(t, w), dqn, dkn


def _att_mix(o1, o2, o3, l1, l2, l3):
    m = jnp.maximum(jnp.maximum(l1, l2), l3)
    e1, e2, e3 = jnp.exp(l1 - m), jnp.exp(l2 - m), jnp.exp(l3 - m)
    s = e1 + e2 + e3
    return (jnp.concatenate([o1 * (e1 / s), o2 * (e2 / s), o3 * (e3 / s)], axis=1),)


def att_branch(name, pa, qn, kn):
    e = ATT_HEAD_DIM
    gains = lambda p, g: p[ATT_GH * g:ATT_GH * (g + 1)].reshape(ATT_GH, 1, e)

    @jax.custom_vjp
    def groups(pa, qn, kn):
        res = [_att_fwd_call(f"{name}_att{g}", g, dil, pa, gains(qn, g), gains(kn, g)) for g, (_, dil) in enumerate(ATT_GROUPS)]
        return tuple(r[0] for r in res) + tuple(r[1] for r in res)

    def groups_fwd(pa, qn, kn):
        return groups(pa, qn, kn), (pa, qn, kn)

    def groups_bwd(res, cts):
        pa, qn, kn = res
        n = len(ATT_GROUPS)
        parts = [_att_bwd_call(f"{name}_att{g}", g, dil, pa, gains(qn, g), gains(kn, g), cts[g], cts[n + g])
                 for g, (_, dil) in enumerate(ATT_GROUPS)]
        d_pa = jnp.concatenate([p[i] for i in range(3) for p in parts], axis=1)
        return (d_pa, jnp.concatenate([p[3] for p in parts]).reshape(qn.shape), jnp.concatenate([p[4] for p in parts]).reshape(kn.shape))

    groups.defvjp(groups_fwd, groups_bwd)
    return rowwise(f"{name}_attmix", _att_mix, groups(pa, qn, kn))[0]


def dn_gates(name, ba, a_log, dt_bias):
    place = lambda p: jnp.pad(p.reshape(1, DN_HEADS), ((0, 0), (DN_HEADS, LANES - 2 * DN_HEADS)))

    def f(x, al, dt):
        lane = lax.broadcasted_iota(jnp.int32, x.shape, 1)
        return (jnp.where(lane < DN_HEADS, _sigmoid(x), -jnp.exp(al) * _softplus(x + dt)),)

    return rowwise(name, f, (ba,), (place(a_log), place(dt_bias)))[0]


def _dn_out(o, z, g):
    parts = []
    for h in range(DN_HEADS):
        sl = slice(h * DN_HEAD_DIM, (h + 1) * DN_HEAD_DIM)
        parts.append(_rms(o[:, sl], g[:, sl]) * _silu(z[:, sl]))
    return (jnp.concatenate(parts, axis=1),)


def _merge(ml, za, zb, zc):
    d = D_MODEL
    return (_sigmoid(ml[:, :d]) * za + _sigmoid(ml[:, d:2 * d]) * zb + _sigmoid(ml[:, 2 * d:]) * zc,)


def add_norm(name, x, pend, scale, gain):
    if pend is None:
        return x, rowwise(name, lambda a, g: (_rms(a, g),), (x,), (gain,))[0]

    def f(a, b, g):
        s = a + scale * b
        return s, _rms(s, g)

    return rowwise(name, f, (x, pend), (gain,))


W_IN_PIECES = (("rgx", 0, 1024), ("gate", 1024, 1024), ("att", 2048, 2304), ("dq", 4352, 1024), ("dk", 5376, 1024),
               ("dv", 6400, 1024), ("dz", 7424, 1024), ("ba", 8448, 16), ("mrg", 8464, 3072))
RG_PAR_BLOCKS = ("lane", "lane", "blk", "lane", "blk", "lane", "lane")


def mixer(name, u, w, p):
    mm = lambda nm, a, wt: mm_rows(nm, a[None], wt[None])
    pr = project_in(name + "_in", u, {k: w["in_" + k] for k, _, _ in W_IN_PIECES})
    ya = colwise(name + "_rg", _rg_block, (pr["rgx"], pr["gate"]),
                 (w["rg_conv_w"], p["rg_conv_b"], _block_diag(p["rg_w_r"]), p["rg_b_r"], _block_diag(p["rg_w_i"]),
                  p["rg_b_i"], p["rg_lambda"]), RG_PAR_BLOCKS, 1)[0]
    yb = att_branch(name, pr["att"], p["att_q_norm"], p["att_k_norm"])
    cw = w["dn_conv_w"]
    cq = colwise(name + "_dnq", _dn_conv_block("q"), (pr["dq"],), (cw[:, :1024],), ("lane",), 1)[0]
    ck = colwise(name + "_dnk", _dn_conv_block("k"), (pr["dk"],), (cw[:, 1024:2048],), ("lane",), 1)[0]
    cv = colwise(name + "_dnv", _dn_conv_block("v"), (pr["dv"],), (cw[:, 2048:],), ("lane",), 1)[0]
    gb = dn_gates(name + "_dngate", pr["ba"], p["dn_a_log"], p["dn_dt_bias"])
    o_dn = dn_core(cq, ck, cv, gb)
    yc = rowwise(name + "_dnout", _dn_out, (o_dn, pr["dz"]), (p["dn_out_norm"].reshape(1, D_MODEL),))[0]
    y = rowwise(name + "_merge", _merge, (pr["mrg"], mm(name + "_ba", ya, w["br_a"]), mm(name + "_bb", yb, w["br_b"]),
                                          mm(name + "_bc", yc, w["br_c"])))[0]
    return mm(name + "_out", y, w["w_out"])


def _loss_call(x, pend, target):
    t, d = x.shape
    tile = _row_tile(t)

    def body(x_ref, p_ref, t_ref, loss_ref, g_ref):
        err = x_ref[...] + 0.5 * p_ref[...] - t_ref[...]
        g_ref[...] = err * (1.0 / d)

        @pl.when(pl.program_id(0) == 0)
        def _():
            loss_ref[...] = jnp.zeros_like(loss_ref)

        loss_ref[...] += jnp.full(loss_ref.shape, 0.5 / d, F32) * jnp.sum(err * err)

    blk = pl.BlockSpec((tile, d), lambda i: (i, 0))
    loss, g = pl.pallas_call(
        body, name="loss", grid=(t // tile,), in_specs=[blk, blk, blk],
        out_specs=[pl.BlockSpec((8, LANES), lambda i: (0, 0)), blk],
        out_shape=[jax.ShapeDtypeStruct((8, LANES), F32), jax.ShapeDtypeStruct((t, d), F32)],
        compiler_params=_params("arbitrary"),
    )(x, pend, target)
    return loss[0, 0], g


@jax.custom_vjp
def loss_op(x, pend, target):
    return _loss_call(x, pend, target)[0]


def _loss_fwd(x, pend, target):
    loss, g = _loss_call(x, pend, target)
    return loss, g


def _loss_bwd(g, ct):
    return ct * g, (0.5 * ct) * g, None


loss_op.defvjp(_loss_fwd, _loss_bwd)


def first_ffn(wg, wu, wd, gain, x):
    x, h = add_norm("L0_n1", x, None, 0.0, gain)
    return x, ffn("L0_f1", h, wg, wu, wd)


def rest_of_step(g, conv, p, x, pend, target):
    scale = 0.5
    w = [split_layer({n: g[n, l] for n, _ in MATRICES if (n, l) in g}, {n: conv[n][l] for n, _ in CONVS}) for l in range(len(p))]
    for l in range(len(p)):
        n = f"L{l}"
        if l > 0:
            x, h = add_norm(n + "_n1", x, pend, scale, p[l]["ffn1_norm"])
            pend, scale = ffn(n + "_f1", h, w[l]["ffn1_w_gate"], w[l]["ffn1_w_up"], w[l]["ffn1_w_down"]), 0.5
        x, h = add_norm(n + "_nm", x, pend, scale, p[l]["mix_norm"])
        pend, scale = mixer(n + "_mx", h, w[l], p[l]), 1.0
        x, h = add_norm(n + "_n2", x, pend, scale, p[l]["ffn2_norm"])
        pend, scale = ffn(n + "_f2", h, w[l]["ffn2_w_gate"], w[l]["ffn2_w_up"], w[l]["ffn2_w_down"]), 0.5
    return loss_op(x, pend, target)


WEIGHT_NAMES = ("ffn1_norm", "ffn1_w_gate", "ffn1_w_up", "ffn1_w_down", "mix_norm", "w_in", "rg_conv_w", "rg_conv_b",
                "rg_w_r", "rg_b_r", "rg_w_i", "rg_b_i", "rg_lambda", "att_q_norm", "att_k_norm", "dn_conv_w", "dn_a_log",
                "dn_dt_bias", "dn_out_norm", "w_branch", "w_out", "ffn2_norm", "ffn2_w_gate", "ffn2_w_up", "ffn2_w_down")
MATRICES = (("ffn1_w_gate", 2), ("ffn1_w_up", 2), ("ffn1_w_down", 1), ("w_in", 2), ("w_branch", 1), ("w_out", 1),
            ("ffn2_w_gate", 2), ("ffn2_w_up", 2), ("ffn2_w_down", 1))
CONVS = (("rg_conv_w", 2), ("dn_conv_w", 2))
SHARD_AXIS = dict(MATRICES + CONVS)
SMALL_NAMES = tuple(n for n in WEIGHT_NAMES if n not in SHARD_AXIS)
ROW_PARAMS = ("ffn1_norm", "mix_norm", "rg_conv_b", "rg_b_r", "rg_b_i", "rg_lambda", "ffn2_norm")
FFN_MATS = ("ffn1_w_gate", "ffn1_w_up", "ffn1_w_down", "ffn2_w_gate", "ffn2_w_up", "ffn2_w_down")
TRANSPOSED_MATS = ("ffn1_w_gate", "ffn1_w_up", "ffn2_w_gate", "ffn2_w_up")
W_IN_SHARD = 2884
GATHER_ORDER = ((("ffn1_w_gate", 0), ("ffn1_w_up", 0), ("ffn1_w_down", 0)),
                (("w_in", 0), ("w_branch", 0), ("w_out", 0)),
                None)
GATHER_IDS = (1, 6, 7)
LATE_MATS = ("ffn2_w_gate", "ffn2_w_up", "ffn2_w_down", "w_out", "w_branch")
EXCHANGE_GROUPS = (lambda n, l: l == 1 and n in LATE_MATS,
                   lambda n, l: (l == 1) != (n in LATE_MATS),
                   lambda n, l: l == 0 and n == "w_in",
                   lambda n, l: l == 0 and n not in LATE_MATS and n != "w_in")


def _shard_minor(a, axis):
    a = jnp.moveaxis(a, 0, axis)
    return a.reshape(a.shape[:axis] + (N_CHIPS * a.shape[axis + 1],) + a.shape[axis + 2:])


def _w_in_piece(g, off, n):
    s = W_IN_SHARD
    parts = [g[j][:, max(off, j * s) - j * s:min(off + n, (j + 1) * s) - j * s]
             for j in range(N_CHIPS) if max(off, j * s) < min(off + n, (j + 1) * s)]
    return jnp.concatenate(parts, axis=1) if len(parts) > 1 else parts[0]


def _w_in_chip_grad(gl, j):
    s = W_IN_SHARD
    parts = [gl["in_" + k][:, max(off, j * s) - off:min(off + n, (j + 1) * s) - off]
             for k, off, n in W_IN_PIECES if max(off, j * s) < min(off + n, (j + 1) * s)]
    return jnp.concatenate(parts, axis=1)


def _layer_weights(g, conv):
    w = {n: g[n] for n in FFN_MATS if n in g}
    w["w_out"] = g["w_out"].reshape(D_MODEL, D_MODEL)
    for k, off, n in W_IN_PIECES:
        piece = _w_in_piece(g["w_in"], off, n)
        w["in_" + k] = jnp.pad(piece, ((0, 0), (0, LANES - n))) if n < LANES else piece
    wb = g["w_branch"].reshape(-1, D_MODEL)
    w["br_a"], w["br_b"], w["br_c"] = wb[:1024], wb[1024:1792], wb[1792:]
    return dict(w, **conv)


def _layer_weight_grads(gl):
    out = {n: gl[n] for n in FFN_MATS if n in gl}
    out["w_out"] = gl["w_out"].reshape(N_CHIPS, -1, D_MODEL)
    out["w_branch"] = jnp.concatenate([gl["br_a"], gl["br_b"], gl["br_c"]], axis=0).reshape(N_CHIPS, -1, D_MODEL)
    out["w_in"] = jnp.stack([_w_in_chip_grad(gl, j) for j in range(N_CHIPS)])
    return out, {n: gl[n] for n, _ in CONVS}


@jax.custom_vjp
def split_layer(g, conv):
    return _layer_weights(g, conv)


split_layer.defvjp(lambda g, conv: (_layer_weights(g, conv), None), lambda _, gw: _layer_weight_grads(gw))


def layer_small(small, l):
    p = {n: small[n][l] for n in SMALL_NAMES}
    for n in ROW_PARAMS:
        p[n] = small[n][l:l + 1]
    return p


def layer_small_grads(gp, small):
    return {n: jnp.stack([g[n] for g in gp]).reshape(small[n].shape) for n in SMALL_NAMES}


HBM_SPEC = pl.BlockSpec(memory_space=pl.ANY)


def _place():
    x, y, c = lax.axis_index("x"), lax.axis_index("y"), lax.axis_index("c")
    other_chips = [(1 - x, y), (x, 1 - y), (1 - x, 1 - y)]
    return x, y, c, 2 * x + y, (x, y, 1 - c), other_chips


def _half_rows(ref, lead, hc):
    hr = ref.shape[-2] // 2
    return ref.at[(*lead, pl.ds(pl.multiple_of(hc * hr, 16), hr), slice(None))]


def _chip_index():
    return (2 * lax.axis_index("x") + lax.axis_index("y")).astype(jnp.int32).reshape(1)


def cast_into_blocks(name, w):
    l, rows, cols = w.shape
    tr = rows // 2

    def body(me_ref, w_ref, *o_refs):
        for a, o_ref in enumerate(o_refs):
            o_ref[...] = w_ref[a:a + 1].astype(BF16)

    return pl.pallas_call(
        body, name=name, out_shape=[jax.ShapeDtypeStruct((N_CHIPS, rows, cols), BF16)] * l,
        grid_spec=pltpu.PrefetchScalarGridSpec(
            num_scalar_prefetch=1, grid=(rows // tr,),
            in_specs=[pl.BlockSpec((l, tr, cols), lambda i, me: (0, i, 0))],
            out_specs=[pl.BlockSpec((1, tr, cols), lambda i, me: (me[0], i, 0))] * l),
        compiler_params=_params("parallel"),
    )(_chip_index(), w)


def _gather_blocks(bufs_in, bufs_out, send_sems, recv_sems):
    n = len(bufs_in)
    x, y, c, me, sibling, chips = _place()

    def copy(s, src, dst, to):
        return pltpu.make_async_remote_copy(src_ref=src, dst_ref=dst, send_sem=send_sems.at[s], recv_sem=recv_sems.at[s],
                                            device_id=to, device_id_type=MESH)

    first, passed = [], []
    for j, (cx, cy) in enumerate(chips):
        for i in range(n):
            cp = copy(6 * i + j, _half_rows(bufs_in[i], (me,), c), _half_rows(bufs_out[i], (me,), c), (cx, cy, c))
            cp.start()
            first.append(cp)
    for j, (cx, cy) in enumerate(chips):
        k = 2 * cx + cy
        for i in range(n):
            copy(6 * i + j, _half_rows(bufs_in[i], (me,), c), _half_rows(bufs_out[i], (k,), c), (cx, cy, c)).wait_recv()
            cp = copy(6 * i + 3 + j, _half_rows(bufs_out[i], (k,), c), _half_rows(bufs_out[i], (k,), c), sibling)
            cp.start()
            passed.append(cp)
    for j, (cx, cy) in enumerate(chips):
        k = 2 * cx + cy
        for i in range(n):
            copy(6 * i + 3 + j, _half_rows(bufs_in[i], (me,), c), _half_rows(bufs_out[i], (k,), 1 - c), sibling).wait_recv()
    for cp in first + passed:
        cp.wait_send()


def _handshake(peers):
    barrier = pltpu.get_barrier_semaphore()
    for p in peers:
        pl.semaphore_signal(barrier, inc=1, device_id=p, device_id_type=MESH)
    pl.semaphore_wait(barrier, len(peers))


def allgather_blocks_sc(name, bufs, collective_id):
    n = len(bufs)
    refs = [jax.new_ref(b, memory_space=pltpu.MemorySpace.HBM) for b in bufs]

    @pl.kernel(mesh=plsc.ScalarSubcoreMesh(axis_name="sequencer", num_cores=1), name=name,
               scratch_types=(pltpu.SemaphoreType.DMA((6 * n,)), pltpu.SemaphoreType.DMA((6 * n,))),
               compiler_params=pltpu.CompilerParams(collective_id=collective_id))
    def launch(send_sems, recv_sems):
        x, y, c, me, sibling, chips = _place()
        _handshake([(cx, cy, c) for cx, cy in chips] + [sibling])
        _gather_blocks(refs, refs, send_sems, recv_sems)

    launch()
    return [jax.freeze(r) for r in refs]


PEER_FLIPS = tuple((fx, fy, fc) for fx in (0, 1) for fy in (0, 1) for fc in (0, 1))[1:]


def exchange_pieces_sc(name, gs, collective_id):
    n = len(gs)

    def body(*refs):
        ins, outs = refs[:n], refs[n:2 * n]
        send_sems, recv_sems = refs[2 * n:]
        x, y, c, me, sibling, chips = _place()
        my_dev = 4 * x + 2 * y + c
        flip = lambda v, f: 1 - v if f else v
        peers = [(flip(x, fx), flip(y, fy), flip(c, fc)) for fx, fy, fc in PEER_FLIPS]
        _handshake(peers)
        sends = []
        for r, (px, py, pc) in enumerate(peers):
            for i in range(n):
                cp = pltpu.make_async_remote_copy(
                    src_ref=_half_rows(ins[i], (2 * px + py,), pc), dst_ref=outs[i].at[my_dev], send_sem=send_sems.at[7 * i + r],
                    recv_sem=recv_sems.at[7 * i + r], device_id=(px, py, pc), device_id_type=MESH)
                cp.start()
                sends.append(cp)
        for r, (px, py, pc) in enumerate(peers):
            for i in range(n):
                pltpu.make_async_remote_copy(
                    src_ref=_half_rows(ins[i], (me,), c), dst_ref=outs[i].at[4 * px + 2 * py + pc], send_sem=send_sems.at[7 * i + r],
                    recv_sem=recv_sems.at[7 * i + r], device_id=(px, py, pc), device_id_type=MESH).wait_recv()
        for cp in sends:
            cp.wait_send()

    return pl.kernel(
        body, name=name, mesh=plsc.ScalarSubcoreMesh(axis_name="sequencer", num_cores=1),
        out_type=[jax.ShapeDtypeStruct((N_DEV, g.shape[1] // 2, g.shape[2]), g.dtype) for g in gs],
        scratch_types=[pltpu.SemaphoreType.DMA((7 * n,)), pltpu.SemaphoreType.DMA((7 * n,))],
        compiler_params=pltpu.CompilerParams(collective_id=collective_id),
    )(*gs)


def sibling_share_halves(name, fs):
    n = len(fs)
    every = (slice(None),)

    def body(*refs):
        ins, outs = refs[:n], refs[n:2 * n]
        send_sems, recv_sems = refs[2 * n:]
        x, y, c, me, sibling, chips = _place()
        sends = []
        for i in range(n):
            cp = pltpu.make_async_remote_copy(src_ref=_half_rows(ins[i], every, c), dst_ref=_half_rows(outs[i], every, c),
                                              send_sem=send_sems.at[i], recv_sem=recv_sems.at[i], device_id=sibling, device_id_type=MESH)
            cp.start()
            sends.append(cp)
        for i in range(n):
            pltpu.make_async_remote_copy(src_ref=_half_rows(ins[i], every, c), dst_ref=_half_rows(outs[i], every, 1 - c),
                                         send_sem=send_sems.at[i], recv_sem=recv_sems.at[i], device_id=sibling,
                                         device_id_type=MESH).wait_recv()
        for cp in sends:
            cp.wait_send()

    return pl.pallas_call(
        body, name=name, out_shape=[jax.ShapeDtypeStruct(f.shape, f.dtype) for f in fs],
        in_specs=[HBM_SPEC] * n, out_specs=[HBM_SPEC] * n, input_output_aliases={i: i for i in range(n)},
        scratch_shapes=[pltpu.SemaphoreType.DMA((n,)), pltpu.SemaphoreType.DMA((n,))],
    )(*fs)


def allgather_small_sc(name, v, collective_id):
    def body(v_ref, out_ref, send_sems, recv_sems, local_sem):
        x, y, c, me, sibling, chips = _place()
        my_dev = 4 * x + 2 * y + c
        flip = lambda a, f: 1 - a if f else a
        peers = [(flip(x, fx), flip(y, fy), flip(c, fc)) for fx, fy, fc in PEER_FLIPS]
        _handshake(peers)
        mine = pltpu.make_async_copy(v_ref, out_ref.at[my_dev], local_sem)
        mine.start()
        sends = []
        for r, peer in enumerate(peers):
            cp = pltpu.make_async_remote_copy(src_ref=v_ref, dst_ref=out_ref.at[my_dev], send_sem=send_sems.at[r],
                                              recv_sem=recv_sems.at[r], device_id=peer, device_id_type=MESH)
            cp.start()
            sends.append(cp)
        for r, (px, py, pc) in enumerate(peers):
            pltpu.make_async_remote_copy(src_ref=v_ref, dst_ref=out_ref.at[4 * px + 2 * py + pc], send_sem=send_sems.at[r],
                                         recv_sem=recv_sems.at[r], device_id=(px, py, pc), device_id_type=MESH).wait_recv()
        for cp in sends:
            cp.wait_send()
        mine.wait()

    return pl.kernel(
        body, name=name, mesh=plsc.ScalarSubcoreMesh(axis_name="sequencer", num_cores=1),
        out_type=jax.ShapeDtypeStruct((N_DEV,) + v.shape, v.dtype),
        scratch_types=[pltpu.SemaphoreType.DMA((7,)), pltpu.SemaphoreType.DMA((7,)), pltpu.SemaphoreType.DMA],
        compiler_params=pltpu.CompilerParams(collective_id=collective_id),
    )(v)


SUM_BLOCK_ELEMS = 512 * 1024


def sum_slabs(name, b):
    k, h, w = b.shape

    def body(b_ref, o_ref):
        acc = b_ref[0].astype(F32)
        for i in range(1, k):
            acc = acc + b_ref[i].astype(F32)
        o_ref[...] = acc

    return pl.pallas_call(
        body, name=name, out_shape=jax.ShapeDtypeStruct((h, w), F32),
        in_specs=[pl.BlockSpec(memory_space=pltpu.VMEM)], out_specs=pl.BlockSpec(memory_space=pltpu.VMEM),
        compiler_params=pltpu.CompilerParams(vmem_limit_bytes=VMEM_LIMIT),
    )(b)


def sum_pieces(name, pieces, gs):
    nl = len(pieces)
    k, h, w = pieces[0].shape
    tile = max(t for t in range(16, h + 1, 16) if h % t == 0 and (t * w <= SUM_BLOCK_ELEMS or t == 16))
    nt = h // tile
    x, y, c = lax.axis_index("x"), lax.axis_index("y"), lax.axis_index("c")
    place = [v.astype(jnp.int32).reshape(1) for v in (c, 2 * x + y, 4 * x + 2 * y + c)]

    assert nl == 2

    def tile_of(l, a, i):
        return i * a if l else i * (1 - a) + (nt - 1) * a

    def body(c_ref, me_ref, dev_ref, *refs):
        p_refs, g_refs, o_ref = refs[:nl], refs[nl:2 * nl], refs[2 * nl]
        my_dev = dev_ref[0]
        for l in range(nl):
            @pl.when(pl.program_id(0) == l)
            def _():
                o_ref[0] = jnp.zeros(o_ref.shape[1:], F32)
                for d in range(k):
                    @pl.when(my_dev == d)
                    def _():
                        o_ref[0] += g_refs[l][0].astype(F32)

                    @pl.when(my_dev != d)
                    def _():
                        o_ref[0] += p_refs[l][d].astype(F32)

    in_specs = [pl.BlockSpec((k, tile, w), functools.partial(lambda l, a, i, cc, me, dev: (0, tile_of(l, a, i), 0), l))
                for l in range(nl)]
    in_specs += [pl.BlockSpec((1, tile, w), functools.partial(lambda l, a, i, cc, me, dev: (me[0], cc[0] * nt + tile_of(l, a, i), 0), l))
                 for l in range(nl)]
    return pl.pallas_call(
        body, name=name, out_shape=jax.ShapeDtypeStruct((nl, 2 * h, w), F32),
        grid_spec=pltpu.PrefetchScalarGridSpec(
            num_scalar_prefetch=3, grid=(nl, nt), in_specs=in_specs,
            out_specs=pl.BlockSpec((1, tile, w), lambda a, i, cc, me, dev: (a, cc[0] * nt + i, 0))),
        compiler_params=_params("arbitrary", "arbitrary"),
    )(*place, *pieces, *gs)


def _adam_block(w, g, m, v):
    m = ADAM_B1 * m + (1.0 - ADAM_B1) * g
    v = ADAM_B2 * v + (1.0 - ADAM_B2) * (g * g)
    m_hat = m / (1.0 - ADAM_B1 ** ADAM_STEP)
    v_hat = v / (1.0 - ADAM_B2 ** ADAM_STEP)
    return -ADAM_LR * (m_hat / (jnp.sqrt(v_hat) + ADAM_EPS) + ADAM_WD * w), m, v


def adamw(name, w, g, m, v):
    shape = w.shape
    cols = shape[-1]
    rows = w.size // cols
    tile = 128 if rows % 128 == 0 else rows
    flat = [a.reshape(rows, cols) for a in (w, g, m, v)]

    def body(w_ref, g_ref, m_ref, v_ref, d_ref, nm_ref, nv_ref):
        d_ref[...], nm_ref[...], nv_ref[...] = _adam_block(w_ref[...], g_ref[...], m_ref[...], v_ref[...])

    blk = pl.BlockSpec((tile, cols), lambda i: (i, 0))
    res = pl.pallas_call(
        body, name=name, grid=(rows // tile,), in_specs=[blk] * 4, out_specs=[blk] * 3,
        out_shape=[jax.ShapeDtypeStruct((rows, cols), F32)] * 3, compiler_params=_params("parallel"),
    )(*flat)
    return tuple(r.reshape(shape) for r in res)


def _pack_small(values):
    flat = jnp.concatenate([v.reshape(-1) for v in values.values()])
    n = flat.shape[0]
    total = -(-n // (8 * LANES)) * (8 * LANES)
    return jnp.pad(flat, (0, total - n)).reshape(-1, LANES)


def _unpack_small(v, shapes):
    flat = v.reshape(-1)
    out, off = {}, 0
    for n, shape in shapes.items():
        sz = int(np.prod(shape))
        out[n] = flat[off:off + sz].reshape(shape)
        off += sz
    return out


def kernel(x, ffn1_norm, ffn1_w_gate, ffn1_w_up, ffn1_w_down, mix_norm, w_in, rg_conv_w, rg_conv_b, rg_w_r, rg_b_r, rg_w_i, rg_b_i, rg_lambda, att_q_norm, att_k_norm, dn_conv_w, dn_a_log, dn_dt_bias, dn_out_norm, w_branch, w_out, ffn2_norm, ffn2_w_gate, ffn2_w_up, ffn2_w_down, loss_target, m_ffn1_norm, m_ffn1_w_gate, m_ffn1_w_up, m_ffn1_w_down, m_mix_norm, m_w_in, m_rg_conv_w, m_rg_conv_b, m_rg_w_r, m_rg_b_r, m_rg_w_i, m_rg_b_i, m_rg_lambda, m_att_q_norm, m_att_k_norm, m_dn_conv_w, m_dn_a_log, m_dn_dt_bias, m_dn_out_norm, m_w_branch, m_w_out, m_ffn2_norm, m_ffn2_w_gate, m_ffn2_w_up, m_ffn2_w_down, v_ffn1_norm, v_ffn1_w_gate, v_ffn1_w_up, v_ffn1_w_down, v_mix_norm, v_w_in, v_rg_conv_w, v_rg_conv_b, v_rg_w_r, v_rg_b_r, v_rg_w_i, v_rg_b_i, v_rg_lambda, v_att_q_norm, v_att_k_norm, v_dn_conv_w, v_dn_a_log, v_dn_dt_bias, v_dn_out_norm, v_w_branch, v_w_out, v_ffn2_norm, v_ffn2_w_gate, v_ffn2_w_up, v_ffn2_w_down):
    given = dict(locals())
    for n in TRANSPOSED_MATS:
        for pre in ("", "m_", "v_"):
            given[pre + n] = jnp.swapaxes(given[pre + n], 1, 2)
    small = {n: given[n] for n in SMALL_NAMES}
    n_layers = ffn1_norm.shape[0]
    mat_names = [n for n, _ in MATRICES]
    conv_names = [n for n, _ in CONVS]

    blocks = {}
    for n in mat_names:
        for l, b in enumerate(cast_into_blocks("cast_" + n, given[n])):
            blocks[n, l] = b
    first, done = {}, []
    for i, wanted in enumerate(GATHER_ORDER[:-1]):
        bufs, _ = lax.optimization_barrier(([blocks[k] for k in wanted], done))
        done = allgather_blocks_sc(f"allgather_{i}", bufs, GATHER_IDS[i])
        first.update(zip(wanted, done))
    rest = {k: b for k, b in blocks.items() if k not in first}
    taps = jnp.concatenate([given[n].reshape(-1) for n in conv_names]).reshape(-1, LANES)
    taps = allgather_small_sc("allgather_taps", taps, 8).reshape(N_CHIPS, 2, -1)[:, 0]
    conv, off = {}, 0
    for n, ax in CONVS:
        sz = given[n].size
        conv[n] = _shard_minor(taps[:, off:off + sz].reshape((N_CHIPS,) + given[n].shape), ax)
        off += sz
    p = [layer_small(small, l) for l in range(n_layers)]

    ffn1_keys = GATHER_ORDER[0]
    (x1, pend), first_vjp = jax.vjp(first_ffn, *[first[k] for k in ffn1_keys], p[0]["ffn1_norm"], x[0])
    keys = list(rest)
    bufs, pend, second = lax.optimization_barrier(([rest[k] for k in keys], pend, [first[k] for k in GATHER_ORDER[1]]))
    gathered = dict(zip(keys, allgather_blocks_sc("allgather_2", bufs, GATHER_IDS[2])))
    gathered.update(zip(GATHER_ORDER[1], second))
    loss, (g_mats, g_conv, gp, gx1, gpend) = jax.value_and_grad(rest_of_step, argnums=(0, 1, 2, 3, 4))(
        gathered, conv, p, x1, pend, loss_target[0])
    *g_ffn1, gp[0]["ffn1_norm"], gx = first_vjp((gx1, gpend))
    g_mats.update(zip(ffn1_keys, g_ffn1))

    pieces = {}
    for i, group in enumerate(EXCHANGE_GROUPS):
        keys = [k for k in g_mats if group(*k)]
        pieces.update(zip(keys, exchange_pieces_sc(f"exchange_{i}", [g_mats[k] for k in keys], 2 + i)))
    halves = {n: sum_pieces("sum_" + n, [pieces[n, l] for l in range(n_layers)], [g_mats[n, l] for l in range(n_layers)])
              for n in mat_names}
    grads = {}
    for tag, names in (("late", [n for n in mat_names if n in LATE_MATS]), ("early", [n for n in mat_names if n not in LATE_MATS])):
        grads.update(zip(names, sibling_share_halves("share_" + tag, [halves[n] for n in names])))

    g_small = dict(layer_small_grads(gp, small), **g_conv, loss=loss.reshape(1))
    packed_small = _pack_small(g_small)
    slabs = allgather_small_sc("allgather_small", packed_small, 9)
    summed =_unpack_small(sum_slabs("sum_small", slabs), {n: g.shape for n, g in g_small.items()})
    chip = 2 * lax.axis_index("x") + lax.axis_index("y")
    for n in SMALL_NAMES:
        grads[n] = summed[n]
    for n, ax in CONVS:
        s = given[n].shape[ax]
        grads[n] = lax.dynamic_slice_in_dim(summed[n], chip * s, s, axis=ax)

    upd = {n: adamw("adamw_" + n, given[n], grads[n], given["m_" + n], given["v_" + n]) for n in WEIGHT_NAMES}
    out = lambda n, a: jnp.swapaxes(a, 1, 2) if n in TRANSPOSED_MATS else a
    return (summed["loss"][0], gx[None], *[out(n, grads[n]) for n in WEIGHT_NAMES], *[out(n, upd[n][0]) for n in WEIGHT_NAMES],
            *[out(n, upd[n][1]) for n in WEIGHT_NAMES], *[out(n, upd[n][2]) for n in WEIGHT_NAMES])
```

```python
import functools
import math

import jax
import jax.numpy as jnp
import numpy as np
from jax import lax
from jax.experimental import pallas as pl
from jax.experimental.pallas import tpu as pltpu
from jax.experimental.pallas import tpu_sc as plsc

F32 = jnp.float32
BF16 = jnp.bfloat16
MESH = pl.DeviceIdType.MESH

D_MODEL = 1024
FFN_DIM = 2816
RG_C = 8.0
ATT_GROUPS = ((128, 1), (512, 4), (2048, 16))
ATT_HEADS = 12
ATT_HEAD_DIM = 64
ATT_SPAN = 128
DN_HEADS = 8
DN_HEAD_DIM = 128
DN_CHUNK = 64
EPS = 1e-6
NEG_INF = -1e30
N_CHIPS = 4
N_DEV = 8

ADAM_LR, ADAM_B1, ADAM_B2, ADAM_EPS, ADAM_WD, ADAM_STEP = 0.001, 0.9, 0.999, 1e-08, 0.01, 10

LANES = 128
VMEM_LIMIT = 56 * 1024 * 1024


def _params(*sem):
    return pltpu.CompilerParams(dimension_semantics=sem or None, vmem_limit_bytes=VMEM_LIMIT)


def _sigmoid(x):
    return 1.0 / (1.0 + jnp.exp(-x))


def _silu(x):
    return x * _sigmoid(x)


def _softplus(x):
    return jnp.maximum(x, 0.0) + jnp.log(1.0 + jnp.exp(-jnp.abs(x)))


def _gelu(x):
    return 0.5 * x * (1.0 + jnp.tanh(math.sqrt(2.0 / math.pi) * (x + 0.044715 * (x * x * x))))


def _neg_expm1(x):
    series = -x * (1.0 + x * (0.5 + x * (1.0 / 6 + x * (1.0 / 24 + x * (1.0 / 120 + x * (1.0 / 720))))))
    return jnp.where(x > -0.25, series, 1.0 - jnp.exp(x))


def _rms(x, g):
    return x * lax.rsqrt(jnp.mean(x * x, axis=-1, keepdims=True) + EPS) * g


_MM_DIMS = {"nn": (((1,), (0,)), ((), ())), "nt": (((1,), (1,)), ((), ())), "tn": (((0,), (0,)), ((), ()))}


def _split(a):
    hi = a.astype(BF16)
    return hi, (a - hi.astype(F32)).astype(BF16)


def _mxu(a, b, form, passes):
    (ca, cb), _ = _MM_DIMS[form]
    if a.ndim == 3:
        dims = (((ca[0] + 1,), (cb[0] + 1,)), ((0,), (0,)))
    else:
        dims = _MM_DIMS[form]
    dg = lambda p, q: lax.dot_general(p, q, dims, preferred_element_type=F32)
    if passes == 1:
        return dg(a.astype(BF16), b.astype(BF16))
    (a_hi, a_lo), (b_hi, b_lo) = _split(a), _split(b)
    return dg(a_hi, b_hi) + (dg(a_hi, b_lo) + dg(a_lo, b_hi))


@functools.partial(jax.custom_vjp, nondiff_argnums=(2, 3))
def _mm(a, b, form, passes):
    return _mxu(a, b, form, passes)


def _mm_fwd(a, b, form, passes):
    return _mxu(a, b, form, passes), (a, b)


def _mm_bwd(form, passes, res, g):
    a, b = res
    if form == "nn":
        return _mm(g, b, "nt", passes), _mm(a, g, "tn", passes)
    if form == "nt":
        return _mm(g, b, "nn", passes), _mm(g, a, "tn", passes)
    return _mm(b, g, "nt", passes), _mm(a, g, "nn", passes)


_mm.defvjp(_mm_fwd, _mm_bwd)


def _dot(a, b):
    return _mm(a, b, "nn", 1)


def _dot_nt(a, b):
    return _mm(a, b, "nt", 1)


def _dot_tn(a, b):
    return _mm(a, b, "tn", 1)


def _dot3(a, b):
    return _mm(a, b, "nn", 3)


def _rows(shape):
    return lax.broadcasted_iota(jnp.int32, shape, len(shape) - 2)


def _roll_down(x, s, fill):
    return jnp.where(_rows(x.shape) >= s, pltpu.roll(x, s, x.ndim - 2), fill)


def _roll_up(x, s, fill):
    n = x.shape[-2]
    return jnp.where(_rows(x.shape) < n - s, pltpu.roll(x, n - s, x.ndim - 2), fill)


@functools.partial(jax.custom_vjp, nondiff_argnums=(1,))
def _shift(x, s):
    return _roll_down(x, s, 0.0)


def _shift_fwd(x, s):
    return _roll_down(x, s, 0.0), None


def _shift_bwd(s, _, g):
    return (_roll_up(g, s, 0.0),)


_shift.defvjp(_shift_fwd, _shift_bwd)


def _causal_conv(x, w):
    return w[0:1] * _shift(x, 3) + w[1:2] * _shift(x, 2) + w[2:3] * _shift(x, 1) + w[3:4] * x


@jax.custom_vjp
def _lin_scan(a, b):
    return _lin_scan_fwd(a, b)[0]


def _lin_scan_fwd(a, b):
    a0 = a
    s = 1
    while s < a.shape[0]:
        b = a * _roll_down(b, s, 0.0) + b
        a = a * _roll_down(a, s, 1.0)
        s *= 2
    return b, (a0, b)


def _lin_scan_bwd(res, g):
    a, h = res
    c = _roll_up(a, 1, 0.0)
    s = 1
    while s < a.shape[0]:
        g = c * _roll_up(g, s, 0.0) + g
        c = c * _roll_up(c, s, 1.0)
        s *= 2
    return g * _roll_down(h, 1, 0.0), g


_lin_scan.defvjp(_lin_scan_fwd, _lin_scan_bwd)


@jax.custom_vjp
def _cumsum_rows(x):
    s = 1
    while s < x.shape[-2]:
        x = x + _roll_down(x, s, 0.0)
        s *= 2
    return x


def _cumsum_rows_fwd(x):
    return _cumsum_rows(x), None


def _cumsum_rows_bwd(_, g):
    s = 1
    while s < g.shape[-2]:
        g = g + _roll_up(g, s, 0.0)
        s *= 2
    return (g,)


_cumsum_rows.defvjp(_cumsum_rows_fwd, _cumsum_rows_bwd)


ROW_BLOCK_BYTES = 14 * 1024 * 1024


def _row_tile(t, width=0):
    for tile in (512, 256):
        if t % tile == 0 and (tile == 256 or tile * width * 4 <= ROW_BLOCK_BYTES):
            return tile
    return t


def _rowwise_fwd_call(name, f, rows, pars, tile):
    t = rows[0].shape[0]
    outs = jax.eval_shape(f, *[jax.ShapeDtypeStruct((tile, r.shape[1]), F32) for r in rows],
                          *[jax.ShapeDtypeStruct(p.shape, F32) for p in pars])
    nr, npar = len(rows), len(pars)

    def body(*refs):
        ins = [r[...] for r in refs[:nr + npar]]
        res = f(*ins)
        for o_ref, o in zip(refs[nr + npar:], res):
            o_ref[...] = o.astype(o_ref.dtype)

    return pl.pallas_call(
        body, name=name, grid=(t // tile,),
        in_specs=[pl.BlockSpec((tile, r.shape[1]), lambda i: (i, 0)) for r in rows]
        + [pl.BlockSpec(p.shape, lambda i: (0, 0)) for p in pars],
        out_specs=[pl.BlockSpec((tile, o.shape[1]), lambda i: (i, 0)) for o in outs],
        out_shape=[jax.ShapeDtypeStruct((t, o.shape[1]), F32) for o in outs],
        compiler_params=_params("parallel"),
    )(*rows, *pars)


def _rowwise_bwd_call(name, f, rows, pars, cts, tile):
    t = rows[0].shape[0]
    nr, npar, nct = len(rows), len(pars), len(cts)

    def body(*refs):
        ins = [r[...] for r in refs[:nr + npar]]
        gs = tuple(r[...] for r in refs[nr + npar:nr + npar + nct])
        outs = refs[nr + npar + nct:]
        _, vjp = jax.vjp(f, *ins)
        d = vjp(gs)
        for o_ref, v in zip(outs[:nr], d[:nr]):
            o_ref[...] = v

        @pl.when(pl.program_id(0) == 0)
        def _():
            for o_ref in outs[nr:]:
                o_ref[...] = jnp.zeros_like(o_ref)

        for o_ref, v in zip(outs[nr:], d[nr:]):
            o_ref[...] += v

    res = pl.pallas_call(
        body, name=name, grid=(t // tile,),
        in_specs=[pl.BlockSpec((tile, r.shape[1]), lambda i: (i, 0)) for r in rows]
        + [pl.BlockSpec(p.shape, lambda i: (0, 0)) for p in pars]
        + [pl.BlockSpec((tile, c.shape[1]), lambda i: (i, 0)) for c in cts],
        out_specs=[pl.BlockSpec((tile, r.shape[1]), lambda i: (i, 0)) for r in rows]
        + [pl.BlockSpec(p.shape, lambda i: (0, 0)) for p in pars],
        out_shape=[jax.ShapeDtypeStruct(r.shape, F32) for r in rows]
        + [jax.ShapeDtypeStruct(p.shape, F32) for p in pars],
        compiler_params=_params("arbitrary"),
    )(*rows, *pars, *cts)
    return tuple(res[:nr]), tuple(res[nr:])


def rowwise(name, f, rows, pars=()):
    outs = jax.eval_shape(f, *[jax.ShapeDtypeStruct((8, r.shape[1]), F32) for r in rows],
                          *[jax.ShapeDtypeStruct(p.shape, F32) for p in pars])
    tile = _row_tile(rows[0].shape[0], 2 * sum(r.shape[1] for r in rows) + sum(o.shape[1] for o in outs))

    @jax.custom_vjp
    def op(rows, pars):
        return tuple(_rowwise_fwd_call(name, f, rows, pars, tile))

    def op_fwd(rows, pars):
        return op(rows, pars), (rows, pars)

    def op_bwd(res, cts):
        return _rowwise_bwd_call(name + "_bwd", f, res[0], res[1], tuple(cts), tile)

    op.defvjp(op_fwd, op_bwd)
    return op(tuple(rows), tuple(pars))


MM_TM = 512


def _tile_of(n, cap):
    best = None
    for c in range(LANES, min(n, cap) + 1, LANES):
        if n % c == 0:
            best = c
    return best or n


def _mmc_dw(name, h, dy):
    m, k = h.shape
    j, _, n = dy.shape
    tk, tn = _tile_of(k, 512), _tile_of(n, 1152)

    def body(h_ref, dy_ref, o_ref):
        o_ref[0] = _dot_tn(h_ref[...], dy_ref[0]).astype(BF16)

    return pl.pallas_call(
        body, name=name, grid=(j, k // tk, n // tn),
        in_specs=[pl.BlockSpec((m, tk), lambda b, i, c: (0, i)), pl.BlockSpec((1, m, tn), lambda b, i, c: (b, 0, c))],
        out_specs=pl.BlockSpec((1, tk, tn), lambda b, i, c: (b, i, c)),
        out_shape=jax.ShapeDtypeStruct((j, k, n), BF16),
        compiler_params=_params("parallel", "parallel", "parallel"),
    )(h, dy)


PROJ_GROUP_COLS = 4608


def _proj_dh(name, dys, ws, acc):
    m, k = dys[0].shape[0], ws[0].shape[0]
    n, tm = len(dys), 256

    def body(*refs):
        dy_refs, w_refs, rest = refs[:n], refs[n:2 * n], refs[2 * n:]
        total = _dot_nt(dy_refs[0][...], w_refs[0][...])
        for dy_ref, w_ref in zip(dy_refs[1:], w_refs[1:]):
            total = total + _dot_nt(dy_ref[...], w_ref[...])
        if acc is not None:
            total = total + rest[0][...]
        rest[-1][...] = total

    row = lambda width: pl.BlockSpec((tm, width), lambda i: (i, 0))
    return pl.pallas_call(
        body, name=name, grid=(m // tm,),
        in_specs=[row(d.shape[1]) for d in dys] + [pl.BlockSpec(w.shape, lambda i: (0, 0)) for w in ws] + ([row(k)] if acc is not None else []),
        out_specs=row(k), out_shape=jax.ShapeDtypeStruct((m, k), F32), compiler_params=_params("parallel"),
    )(*dys, *ws, *([acc] if acc is not None else []))


def _proj_fwd(name, h, ws):
    m, k = h.shape
    n, tm = len(ws), 256

    def body(h_ref, *refs):
        hv = h_ref[...].astype(BF16)
        for w_ref, o_ref in zip(refs[:n], refs[n:]):
            o_ref[...] = _dot(hv, w_ref[...])

    row = lambda width: pl.BlockSpec((tm, width), lambda i: (i, 0))
    return pl.pallas_call(
        body, name=name, grid=(m // tm,),
        in_specs=[row(k)] + [pl.BlockSpec(w.shape, lambda i: (0, 0)) for w in ws],
        out_specs=[row(w.shape[1]) for w in ws],
        out_shape=[jax.ShapeDtypeStruct((m, w.shape[1]), F32) for w in ws], compiler_params=_params("parallel"),
    )(h, *ws)


def project_in(name, h, ws):
    keys = list(ws)
    groups, cols = [[]], 0
    for p in keys:
        if groups[-1] and cols + ws[p].shape[1] > PROJ_GROUP_COLS:
            groups.append([])
            cols = 0
        groups[-1].append(p)
        cols += ws[p].shape[1]

    @jax.custom_vjp
    def op(h, ws):
        out = {}
        for i, group in enumerate(groups):
            out.update(zip(group, _proj_fwd(f"{name}_{i}", h, [ws[p] for p in group])))
        return out

    def op_fwd(h, ws):
        return op(h, ws), (h, ws)

    def op_bwd(res, dys):
        h, ws = res
        dh = None
        for i, group in enumerate(groups):
            dh = _proj_dh(f"{name}_dh{i}", [dys[p] for p in group], [ws[p] for p in group], dh)
        return dh, {p: _mmc_dw(f"{name}_{p}_dw", h, dys[p][None])[0] for p in keys}

    op.defvjp(op_fwd, op_bwd)
    return op(h, ws)


def _ffn_up(name, h, wt):
    m, k = h.shape
    j, n, _ = wt.shape
    tm = MM_TM

    def body(h_ref, w_ref, o_ref):
        o_ref[0] = _dot_nt(h_ref[...], w_ref[0])

    return pl.pallas_call(
        body, name=name, grid=(m // tm, j),
        in_specs=[pl.BlockSpec((tm, k), lambda i, b: (i, 0)), pl.BlockSpec((1, n, k), lambda i, b: (b, 0, 0))],
        out_specs=pl.BlockSpec((1, tm, n), lambda i, b: (b, i, 0)),
        out_shape=jax.ShapeDtypeStruct((j, m, n), F32), compiler_params=_params("parallel", "parallel"),
    )(h, wt)


def _ffn_down(name, g, u, wd):
    j, m, n = g.shape
    d = wd.shape[2]
    tm = MM_TM

    def body(g_ref, u_ref, w_ref, o_ref):
        part = _dot(_silu(g_ref[0]) * u_ref[0], w_ref[0])

        @pl.when(pl.program_id(1) == 0)
        def _():
            o_ref[...] = part

        @pl.when(pl.program_id(1) > 0)
        def _():
            o_ref[...] += part

    act = pl.BlockSpec((1, tm, n), lambda i, b: (b, i, 0))
    return pl.pallas_call(
        body, name=name, grid=(m // tm, j),
        in_specs=[act, act, pl.BlockSpec((1, n, d), lambda i, b: (b, 0, 0))],
        out_specs=pl.BlockSpec((tm, d), lambda i, b: (i, 0)),
        out_shape=jax.ShapeDtypeStruct((m, d), F32), compiler_params=_params("parallel", "arbitrary"),
    )(g, u, wd)


def _ffn_down_bwd(name, dy, g, u, wd):
    j, m, n = g.shape
    d = wd.shape[2]
    tm = MM_TM

    def body(dy_ref, g_ref, u_ref, w_ref, dg_ref, du_ref):
        da = _dot_nt(dy_ref[...], w_ref[0])
        gv = g_ref[0]
        s = _sigmoid(gv)
        dg_ref[0] = da * u_ref[0] * (s * (1.0 + gv * (1.0 - s)))
        du_ref[0] = da * (gv * s)

    act = pl.BlockSpec((1, tm, n), lambda i, b: (b, i, 0))
    return pl.pallas_call(
        body, name=name, grid=(m // tm, j),
        in_specs=[pl.BlockSpec((tm, d), lambda i, b: (i, 0)), act, act, pl.BlockSpec((1, n, d), lambda i, b: (b, 0, 0))],
        out_specs=[act, act], out_shape=[jax.ShapeDtypeStruct((j, m, n), F32)] * 2,
        compiler_params=_params("parallel", "parallel"),
    )(dy, g, u, wd)


def _ffn_down_dw(name, g, u, dy):
    j, m, n = g.shape
    d = dy.shape[1]
    tn = _tile_of(d, 512)

    def body(g_ref, u_ref, dy_ref, o_ref):
        o_ref[0] = _dot_tn(_silu(g_ref[0]) * u_ref[0], dy_ref[...]).astype(BF16)

    act = pl.BlockSpec((1, m, n), lambda b, c: (b, 0, 0))
    return pl.pallas_call(
        body, name=name, grid=(j, d // tn),
        in_specs=[act, act, pl.BlockSpec((m, tn), lambda b, c: (0, c))],
        out_specs=pl.BlockSpec((1, n, tn), lambda b, c: (b, 0, c)),
        out_shape=jax.ShapeDtypeStruct((j, n, d), BF16), compiler_params=_params("parallel", "parallel"),
    )(g, u, dy)


def _ffn_up_dh(name, dg, du, wg, wu):
    j, m, n = dg.shape
    k = wg.shape[2]
    tm = MM_TM

    def body(dg_ref, du_ref, wg_ref, wu_ref, o_ref):
        part = _dot(dg_ref[0], wg_ref[0]) + _dot(du_ref[0], wu_ref[0])

        @pl.when(pl.program_id(1) == 0)
        def _():
            o_ref[...] = part

        @pl.when(pl.program_id(1) > 0)
        def _():
            o_ref[...] += part

    act = pl.BlockSpec((1, tm, n), lambda i, b: (b, i, 0))
    wsp = pl.BlockSpec((1, n, k), lambda i, b: (b, 0, 0))
    return pl.pallas_call(
        body, name=name, grid=(m // tm, j), in_specs=[act, act, wsp, wsp],
        out_specs=pl.BlockSpec((tm, k), lambda i, b: (i, 0)),
        out_shape=jax.ShapeDtypeStruct((m, k), F32), compiler_params=_params("parallel", "arbitrary"),
    )(dg, du, wg, wu)


def _ffn_up_dw(name, dy, h):
    j, m, n = dy.shape
    k = h.shape[1]
    tk = _tile_of(k, 512)

    def body(dy_ref, h_ref, o_ref):
        o_ref[0] = _dot_tn(dy_ref[0], h_ref[...]).astype(BF16)

    return pl.pallas_call(
        body, name=name, grid=(j, k // tk),
        in_specs=[pl.BlockSpec((1, m, n), lambda b, i: (b, 0, 0)), pl.BlockSpec((m, tk), lambda b, i: (0, i))],
        out_specs=pl.BlockSpec((1, n, tk), lambda b, i: (b, 0, i)),
        out_shape=jax.ShapeDtypeStruct((j, n, k), BF16), compiler_params=_params("parallel", "parallel"),
    )(dy, h)


def ffn(name, h, wg, wu, wd):
    @jax.custom_vjp
    def op(h, wg, wu, wd):
        return _ffn_down(name + "_d", _ffn_up(name + "_g", h, wg), _ffn_up(name + "_u", h, wu), wd)

    def op_fwd(h, wg, wu, wd):
        g, u = _ffn_up(name + "_g", h, wg), _ffn_up(name + "_u", h, wu)
        return _ffn_down(name + "_d", g, u, wd), (h, g, u, wg, wu, wd)

    def op_bwd(res, dy):
        h, g, u, wg, wu, wd = res
        dg, du = _ffn_down_bwd(name + "_d_bwd", dy, g, u, wd)
        return (_ffn_up_dh(name + "_dh", dg, du, wg, wu), _ffn_up_dw(name + "_g_dw", dg, h), _ffn_up_dw(name + "_u_dw", du, h),
                _ffn_down_dw(name + "_d_dw", g, u, dy))

    op.defvjp(op_fwd, op_bwd)
    return op(h, wg, wu, wd)


def _mmr_fwd(name, a, w):
    j, m, n = a.shape
    nn = w.shape[2]
    tm, tn = MM_TM, _tile_of(nn, 1024)

    def body(a_ref, w_ref, o_ref):
        part = _dot(a_ref[0], w_ref[0])

        @pl.when(pl.program_id(2) == 0)
        def _():
            o_ref[...] = part

        @pl.when(pl.program_id(2) > 0)
        def _():
            o_ref[...] += part

    return pl.pallas_call(
        body, name=name, grid=(m // tm, nn // tn, j),
        in_specs=[pl.BlockSpec((1, tm, n), lambda i, c, b: (b, i, 0)), pl.BlockSpec((1, n, tn), lambda i, c, b: (b, 0, c))],
        out_specs=pl.BlockSpec((tm, tn), lambda i, c, b: (i, c)),
        out_shape=jax.ShapeDtypeStruct((m, nn), F32),
        compiler_params=_params("parallel", "parallel", "arbitrary"),
    )(a, w)


def _mmr_da(name, dy, w):
    m, nn = dy.shape
    j, n, _ = w.shape
    tm = MM_TM

    def body(dy_ref, w_ref, o_ref):
        o_ref[0] = _dot_nt(dy_ref[...], w_ref[0])

    return pl.pallas_call(
        body, name=name, grid=(m // tm, j),
        in_specs=[pl.BlockSpec((tm, nn), lambda i, b: (i, 0)), pl.BlockSpec((1, n, nn), lambda i, b: (b, 0, 0))],
        out_specs=pl.BlockSpec((1, tm, n), lambda i, b: (b, i, 0)),
        out_shape=jax.ShapeDtypeStruct((j, m, n), F32),
        compiler_params=_params("parallel", "parallel"),
    )(dy, w)


def _mmr_dw(name, a, dy):
    j, m, n = a.shape
    nn = dy.shape[1]
    tn = _tile_of(nn, 512)

    def body(a_ref, dy_ref, o_ref):
        o_ref[0] = _dot_tn(a_ref[0], dy_ref[...]).astype(BF16)

    return pl.pallas_call(
        body, name=name, grid=(j, nn // tn),
        in_specs=[pl.BlockSpec((1, m, n), lambda b, c: (b, 0, 0)), pl.BlockSpec((m, tn), lambda b, c: (0, c))],
        out_specs=pl.BlockSpec((1, n, tn), lambda b, c: (b, 0, c)),
        out_shape=jax.ShapeDtypeStruct((j, n, nn), BF16),
        compiler_params=_params("parallel", "parallel"),
    )(a, dy)


def mm_rows(name, a, w):
    @jax.custom_vjp
    def op(a, w):
        return _mmr_fwd(name, a, w)

    def op_fwd(a, w):
        return op(a, w), (a, w)

    def op_bwd(res, dy):
        a, w = res
        return _mmr_da(name + "_da", dy, w), _mmr_dw(name + "_dw", a, dy)

    op.defvjp(op_fwd, op_bwd)
    return op(a, w)


def _colwise_specs(cols, pars, par_block):
    t = cols[0].shape[0]
    specs = [pl.BlockSpec((t, LANES), lambda j: (0, j)) for _ in cols]
    for p, blk in zip(pars, par_block):
        if blk == "lane":
            specs.append(pl.BlockSpec((p.shape[0], LANES), lambda j: (0, j)))
        else:
            specs.append(pl.BlockSpec((1,) + p.shape[1:], lambda j: (j, 0, 0)))
    return specs


def _colwise_fwd_call(name, f, cols, pars, par_block, n_out):
    t, c = cols[0].shape
    nc, npar = len(cols), len(pars)

    def body(*refs):
        ins = [r[...] for r in refs[:nc]] + [r[...] if b == "lane" else r[0] for r, b in zip(refs[nc:nc + npar], par_block)]
        res = f(*ins)
        for o_ref, o in zip(refs[nc + npar:], res):
            o_ref[...] = o

    return pl.pallas_call(
        body, name=name, grid=(c // LANES,),
        in_specs=_colwise_specs(cols, pars, par_block),
        out_specs=[pl.BlockSpec((t, LANES), lambda j: (0, j)) for _ in range(n_out)],
        out_shape=[jax.ShapeDtypeStruct((t, c), F32) for _ in range(n_out)],
        compiler_params=_params("parallel"),
    )(*cols, *pars)


def _colwise_bwd_call(name, f, cols, pars, par_block, cts):
    t, c = cols[0].shape
    nc, npar, nct = len(cols), len(pars), len(cts)

    def body(*refs):
        ins = [r[...] for r in refs[:nc]] + [r[...] if b == "lane" else r[0] for r, b in zip(refs[nc:nc + npar], par_block)]
        gs = tuple(r[...] for r in refs[nc + npar:nc + npar + nct])
        outs = refs[nc + npar + nct:]
        _, vjp = jax.vjp(f, *ins)
        d = vjp(gs)
        for o_ref, v in zip(outs[:nc], d[:nc]):
            o_ref[...] = v
        for o_ref, v, b in zip(outs[nc:], d[nc:], par_block):
            if b == "lane":
                o_ref[...] = v
            else:
                o_ref[0] = v

    res = pl.pallas_call(
        body, name=name, grid=(c // LANES,),
        in_specs=_colwise_specs(cols, pars, par_block) + [pl.BlockSpec((t, LANES), lambda j: (0, j)) for _ in cts],
        out_specs=_colwise_specs(cols, pars, par_block),
        out_shape=[jax.ShapeDtypeStruct(v.shape, F32) for v in (*cols, *pars)],
        compiler_params=_params("parallel"),
    )(*cols, *pars, *cts)
    return tuple(res[:nc]), tuple(res[nc:])


def colwise(name, f, cols, pars, par_block, n_out):
    @jax.custom_vjp
    def op(cols, pars):
        return tuple(_colwise_fwd_call(name, f, cols, pars, par_block, n_out))

    def op_fwd(cols, pars):
        return op(cols, pars), (cols, pars)

    def op_bwd(res, cts):
        return _colwise_bwd_call(name + "_bwd", f, res[0], res[1], par_block, tuple(cts))

    op.defvjp(op_fwd, op_bwd)
    return op(tuple(cols), tuple(pars))


def _rg_block(x, gate, cw, cb, wr, br, wi, bi, lam):
    xa = _causal_conv(x, cw) + cb
    r = _sigmoid(_dot(xa, wr) + br)
    i = _sigmoid(_dot(xa, wi) + bi)
    log_a = -RG_C * r * _softplus(-lam)
    a = jnp.exp(log_a)
    b = jnp.sqrt(_neg_expm1(2.0 * log_a)) * (i * xa)
    return (_lin_scan(a, b) * _gelu(gate),)


def _dn_conv_block(mode):
    def f(x, cw):
        c = _silu(_causal_conv(x, cw))
        if mode == "v":
            return (c,)
        c = c * lax.rsqrt(jnp.sum(c * c, axis=-1, keepdims=True) + EPS)
        return (c * (DN_HEAD_DIM ** -0.5),) if mode == "q" else (c,)
    return f


def _block_diag(w):
    w = w.reshape(8, 2, 64, 64)
    z = jnp.zeros((8, 64, 64), w.dtype)
    top = jnp.concatenate([w[:, 0], z], axis=2)
    bot = jnp.concatenate([z, w[:, 1]], axis=2)
    return jnp.concatenate([top, bot], axis=1)


DN_HP = 8


def _dn_block(S, qw, kw, vw, gb, h0, tinv=None):
    hp, hd = S.shape[0], DN_HEAD_DIM
    heads = lambda a: jnp.concatenate([a[None, :, j * hd:(j + 1) * hd] for j in range(hp)], axis=0)
    lane = lax.broadcasted_iota(jnp.int32, gb.shape, 1)
    col = lambda i: jnp.sum(jnp.where(lane == i, gb, 0.0), axis=1, keepdims=True)[None]
    beta = jnp.concatenate([col(h0 + j) for j in range(hp)], axis=0)
    g = jnp.concatenate([col(h0 + j + DN_HEADS) for j in range(hp)], axis=0)
    s_new, o, tinv = _dn_step(S, heads(qw), heads(kw), heads(vw), beta, g, tinv)
    return s_new, jnp.concatenate([o[j:j + 1].reshape(o.shape[1:]) for j in range(hp)], axis=1), tinv


@jax.custom_vjp
def _unit_lower_inverse(a):
    c = a.shape[-1]
    eye = (lax.broadcasted_iota(jnp.int32, (c, c), 0) == lax.broadcasted_iota(jnp.int32, (c, c), 1)).astype(F32)
    p = -a
    tinv = eye + p
    for _ in range(5):
        p = _dot3(p, p)
        tinv = tinv + _dot3(tinv, p)
    return tinv


def _unit_lower_inverse_fwd(a):
    t = _unit_lower_inverse(a)
    return t, t


def _unit_lower_inverse_bwd(t, g):
    return (-_mm(_mm(t, g, "tn", 3), t, "nt", 3),)


_unit_lower_inverse.defvjp(_unit_lower_inverse_fwd, _unit_lower_inverse_bwd)


@jax.custom_vjp
def _known_inverse(a, t):
    return t


_known_inverse.defvjp(lambda a, t: (t, t), lambda t, g: (_unit_lower_inverse_bwd(t, g)[0], jnp.zeros_like(t)))


def _dn_step(S, q, k, v, beta, g, tinv=None):
    c = DN_CHUNK
    ri = lax.broadcasted_iota(jnp.int32, (c, c), 0)
    ci = lax.broadcasted_iota(jnp.int32, (c, c), 1)
    incl, strict = ri >= ci, ri > ci
    gam = _cumsum_rows(g)
    gam_row = jnp.sum(jnp.where(ri <= ci, g, 0.0), axis=-2, keepdims=True)
    gam_last = jnp.sum(g, axis=-2, keepdims=True)
    decay = jnp.where(incl, jnp.exp(jnp.where(incl, gam - gam_row, 0.0)), 0.0)
    kb = k * beta
    vb = v * beta
    a = jnp.where(strict, _dot_nt(kb, k) * decay, 0.0)
    tinv = _unit_lower_inverse(a) if tinv is None else _known_inverse(a, tinv)
    e_gam = jnp.exp(gam)
    u0 = _dot3(tinv, vb)
    wk = _dot3(tinv, kb * e_gam)
    qk = jnp.where(incl, _dot_nt(q, k) * decay, 0.0)
    q_dec = q * e_gam
    k_dec = k * jnp.exp(gam_last - gam)
    u = u0 - _dot(wk, S)
    o = _dot(q_dec, S) + _dot(qk, u)
    s_new = S * jnp.exp(gam_last) + _dot_tn(k_dec, u)
    return s_new, o, tinv


def _dn_fwd_call(q, k, v, gb):
    t, w = q.shape
    n, hp, hd, c = t // DN_CHUNK, DN_HP, DN_HEAD_DIM, DN_CHUNK

    def body(q_ref, k_ref, v_ref, gb_ref, o_ref, s0_ref, ti_ref, s_scr):
        @pl.when(pl.program_id(1) == 0)
        def _():
            s_scr[...] = jnp.zeros_like(s_scr)

        s_old = s_scr[...]
        s0_ref[:, 0] = s_old
        s_new, o, tinv = _dn_block(s_old, q_ref[...], k_ref[...], v_ref[...], gb_ref[...], pl.program_id(0) * hp)
        o_ref[...] = o
        ti_ref[:, 0] = tinv
        s_scr[...] = s_new

    blk = pl.BlockSpec((c, hp * hd), lambda g, i: (i, g))
    return pl.pallas_call(
        body, name="dn_core", grid=(DN_HEADS // hp, n),
        in_specs=[blk, blk, blk, pl.BlockSpec((c, LANES), lambda g, i: (i, 0))],
        out_specs=[blk, pl.BlockSpec((hp, 1, hd, hd), lambda g, i: (g, i, 0, 0)), pl.BlockSpec((hp, 1, c, c), lambda g, i: (g, i, 0, 0))],
        out_shape=[jax.ShapeDtypeStruct((t, w), F32), jax.ShapeDtypeStruct((DN_HEADS, n, hd, hd), F32),
                   jax.ShapeDtypeStruct((DN_HEADS, n, c, c), F32)],
        scratch_shapes=[pltpu.VMEM((hp, hd, hd), F32)],
        compiler_params=_params("parallel", "arbitrary"),
    )(q, k, v, gb)


def _dn_bwd_call(q, k, v, gb, s0, ti, do):
    t, w = q.shape
    n, hp, hd, c = t // DN_CHUNK, DN_HP, DN_HEAD_DIM, DN_CHUNK
    ng = DN_HEADS // hp

    def body(q_ref, k_ref, v_ref, gb_ref, s0_ref, ti_ref, do_ref, dq_ref, dk_ref, dv_ref, dgb_ref, ds_scr):
        @pl.when(pl.program_id(1) == 0)
        def _():
            ds_scr[...] = jnp.zeros_like(ds_scr)

        h0, tinv = pl.program_id(0) * hp, ti_ref[:, 0]
        _, vjp = jax.vjp(lambda *a: _dn_block(*a, h0, tinv)[:2], s0_ref[:, 0], q_ref[...], k_ref[...], v_ref[...], gb_ref[...])
        ds, dq, dk, dv, dgb = vjp((ds_scr[...], do_ref[...]))
        ds_scr[...] = ds
        dq_ref[...], dk_ref[...], dv_ref[...] = dq, dk, dv
        dgb_ref[0] = dgb

    blk = pl.BlockSpec((c, hp * hd), lambda g, i: (n - 1 - i, g))
    res = pl.pallas_call(
        body, name="dn_core_bwd", grid=(ng, n),
        in_specs=[blk, blk, blk, pl.BlockSpec((c, LANES), lambda g, i: (n - 1 - i, 0)),
                  pl.BlockSpec((hp, 1, hd, hd), lambda g, i: (g, n - 1 - i, 0, 0)),
                  pl.BlockSpec((hp, 1, c, c), lambda g, i: (g, n - 1 - i, 0, 0)), blk],
        out_specs=[blk, blk, blk, pl.BlockSpec((1, c, LANES), lambda g, i: (g, n - 1 - i, 0))],
        out_shape=[jax.ShapeDtypeStruct((t, w), F32)] * 3 + [jax.ShapeDtypeStruct((ng, t, LANES), F32)],
        scratch_shapes=[pltpu.VMEM((hp, hd, hd), F32)],
        compiler_params=_params("parallel", "arbitrary"),
    )(q, k, v, gb, s0, ti, do)
    return res[0], res[1], res[2], jnp.sum(res[3], axis=0)


@jax.custom_vjp
def dn_core(q, k, v, gb):
    return _dn_fwd_call(q, k, v, gb)[0]


def _dn_core_fwd(q, k, v, gb):
    o, s0, ti = _dn_fwd_call(q, k, v, gb)
    return o, (q, k, v, gb, s0, ti)


def _dn_core_bwd(res, do):
    return _dn_bwd_call(*res, do)


dn_core.defvjp(_dn_core_fwd, _dn_core_bwd)


ATT_GH = 4


def _att_block(q, kp, kc, vp, vc, qn, kn, slope, has_prev, dil):
    s = ATT_SPAN
    qh = _rms(q, qn) * (ATT_HEAD_DIM ** -0.5)
    qi = lax.broadcasted_iota(jnp.int32, (s, s), 0)
    kj = lax.broadcasted_iota(jnp.int32, (s, s), 1)
    d_p = qi + s - kj
    d_c = qi - kj
    s_p = _dot_nt(qh, _rms(kp, kn)) - slope * (d_p * dil).astype(F32)
    s_c = _dot_nt(qh, _rms(kc, kn)) - slope * (d_c * dil).astype(F32)
    s_p = jnp.where((d_p <= s) & (has_prev > 0), s_p, NEG_INF)
    s_c = jnp.where(d_c >= 0, s_c, NEG_INF)
    m = lax.stop_gradient(jnp.maximum(jnp.max(s_p, axis=-1, keepdims=True), jnp.max(s_c, axis=-1, keepdims=True)))
    p_p = jnp.exp(s_p - m)
    p_c = jnp.exp(s_c - m)
    den = jnp.sum(p_p, axis=-1, keepdims=True) + jnp.sum(p_c, axis=-1, keepdims=True)
    o = _dot(p_p / den, vp) + _dot(p_c / den, vc)
    lse = m + jnp.log(den)
    return o, jnp.broadcast_to(lse, o.shape)


def _att_heads(a):
    e = ATT_HEAD_DIM
    return jnp.concatenate([a[None, :, h * e:(h + 1) * e] for h in range(ATT_GH)], axis=0)


def _att_lanes(a):
    return jnp.concatenate([a[h:h + 1].reshape(a.shape[1:]) for h in range(ATT_GH)], axis=1)


def _att_rows(q, kp, kc, vp, vc, qn, kn, group, has_prev, dil):
    head = lax.broadcasted_iota(jnp.int32, (ATT_GH, 1, 1), 0) + (ATT_GH * group + 1)
    slope = jnp.exp(head.astype(F32) * (-8.0 / ATT_HEADS * math.log(2.0)))
    o, lse = _att_block(_att_heads(q), _att_heads(kp), _att_heads(kc), _att_heads(vp), _att_heads(vc), qn, kn, slope, has_prev, dil)
    return _att_lanes(o), _att_lanes(lse)


def _att_specs(group, dil):
    blk = (ATT_SPAN, ATT_GH * ATT_HEAD_DIM)
    cur = lambda which: pl.BlockSpec(blk, lambda r, n: (n, r * 9 + 3 * which + group))
    prev = lambda which: pl.BlockSpec(blk, lambda r, n: (jnp.maximum(n - 1, 0), r * 9 + 3 * which + group))
    out = pl.BlockSpec(blk, lambda r, n: (n, r))
    gain = pl.BlockSpec((ATT_GH, 1, ATT_HEAD_DIM), lambda r, n: (0, 0, 0))
    return [cur(0), prev(1), cur(1), prev(2), cur(2), gain, gain], out, gain


def _att_fwd_call(name, group, dil, pa, qn, kn):
    t = pa.shape[0]
    l = t // dil
    w = ATT_GH * ATT_HEAD_DIM
    ins, out, _ = _att_specs(group, dil)
    pav = pa.reshape(l, dil * pa.shape[1])

    def body(q_ref, kp_ref, kc_ref, vp_ref, vc_ref, qn_ref, kn_ref, o_ref, lse_ref):
        o_ref[...], lse_ref[...] = _att_rows(q_ref[...], kp_ref[...], kc_ref[...], vp_ref[...], vc_ref[...], qn_ref[...],
                                             kn_ref[...], group, pl.program_id(1), dil)

    o, lse = pl.pallas_call(
        body, name=name, grid=(dil, l // ATT_SPAN), in_specs=ins, out_specs=[out, out],
        out_shape=[jax.ShapeDtypeStruct((l, dil * w), F32)] * 2, compiler_params=_params("parallel", "arbitrary"),
    )(pav, pav, pav, pav, pav, qn, kn)
    return o.reshape(t, w), lse.reshape(t, w)


def _att_bwd_call(name, group, dil, pa, qn, kn, do, dlse):
    t = pa.shape[0]
    l = t // dil
    w = ATT_GH * ATT_HEAD_DIM
    ins, out, gain = _att_specs(group, dil)
    pav = pa.reshape(l, dil * pa.shape[1])

    def body(q_ref, kp_ref, kc_ref, vp_ref, vc_ref, qn_ref, kn_ref, do_ref, dlse_ref,
             dq_ref, dkp_ref, dkc_ref, dvp_ref, dvc_ref, dqn_ref, dkn_ref):
        has_prev = pl.program_id(1)
        _, vjp = jax.vjp(lambda *a: _att_rows(*a, group, has_prev, dil), q_ref[...], kp_ref[...], kc_ref[...], vp_ref[...],
                         vc_ref[...], qn_ref[...], kn_ref[...])
        dq, dkp, dkc, dvp, dvc, dqn, dkn = vjp((do_ref[...], dlse_ref[...]))
        dq_ref[...], dkp_ref[...], dkc_ref[...], dvp_ref[...], dvc_ref[...] = dq, dkp, dkc, dvp, dvc

        @pl.when((pl.program_id(0) == 0) & (pl.program_id(1) == 0))
        def _():
            dqn_ref[...] = jnp.zeros_like(dqn_ref)
            dkn_ref[...] = jnp.zeros_like(dkn_ref)

        dqn_ref[...] += dqn
        dkn_ref[...] += dkn

    res = pl.pallas_call(
        body, name=name + "_bwd", grid=(dil, l // ATT_SPAN), in_specs=ins + [out, out],
        out_specs=[out] * 5 + [gain, gain],
        out_shape=[jax.ShapeDtypeStruct((l, dil * w), F32)] * 5 + [jax.ShapeDtypeStruct(qn.shape, F32)] * 2,
        compiler_params=_params("arbitrary", "arbitrary"),
    )(pav, pav, pav, pav, pav, qn, kn, do.reshape(l, dil * w), dlse.reshape(l, dil * w))
    dq, dkp, dkc, dvp, dvc, dqn, dkn = res
    back = lambda g: jnp.pad(g[ATT_SPAN:], ((0, ATT_SPAN), (0, 0)))
    return dq.reshape(t, w), (dkc + back(dkp)).reshape(t, w), (dvc + back(dvp)).reshape(t, w), dqn, dkn


def _att_mix(o1, o2, o3, l1, l2, l3):
    m = jnp.maximum(jnp.maximum(l1, l2), l3)
    e1, e2, e3 = jnp.exp(l1 - m), jnp.exp(l2 - m), jnp.exp(l3 - m)
    s = e1 + e2 + e3
    return (jnp.concatenate([o1 * (e1 / s), o2 * (e2 / s), o3 * (e3 / s)], axis=1),)


def att_branch(name, pa, qn, kn):
    e = ATT_HEAD_DIM
    gains = lambda p, g: p[ATT_GH * g:ATT_GH * (g + 1)].reshape(ATT_GH, 1, e)

    @jax.custom_vjp
    def groups(pa, qn, kn):
        res = [_att_fwd_call(f"{name}_att{g}", g, dil, pa, gains(qn, g), gains(kn, g)) for g, (_, dil) in enumerate(ATT_GROUPS)]
        return tuple(r[0] for r in res) + tuple(r[1] for r in res)

    def groups_fwd(pa, qn, kn):
        return groups(pa, qn, kn), (pa, qn, kn)

    def groups_bwd(res, cts):
        pa, qn, kn = res
        n = len(ATT_GROUPS)
        parts = [_att_bwd_call(f"{name}_att{g}", g, dil, pa, gains(qn, g), gains(kn, g), cts[g], cts[n + g])
                 for g, (_, dil) in enumerate(ATT_GROUPS)]
        d_pa = jnp.concatenate([p[i] for i in range(3) for p in parts], axis=1)
        return (d_pa, jnp.concatenate([p[3] for p in parts]).reshape(qn.shape), jnp.concatenate([p[4] for p in parts]).reshape(kn.shape))

    groups.defvjp(groups_fwd, groups_bwd)
    return rowwise(f"{name}_attmix", _att_mix, groups(pa, qn, kn))[0]


def dn_gates(name, ba, a_log, dt_bias):
    place = lambda p: jnp.pad(p.reshape(1, DN_HEADS), ((0, 0), (DN_HEADS, LANES - 2 * DN_HEADS)))

    def f(x, al, dt):
        lane = lax.broadcasted_iota(jnp.int32, x.shape, 1)
        return (jnp.where(lane < DN_HEADS, _sigmoid(x), -jnp.exp(al) * _softplus(x + dt)),)

    return rowwise(name, f, (ba,), (place(a_log), place(dt_bias)))[0]


def _dn_out(o, z, g):
    parts = []
    for h in range(DN_HEADS):
        sl = slice(h * DN_HEAD_DIM, (h + 1) * DN_HEAD_DIM)
        parts.append(_rms(o[:, sl], g[:, sl]) * _silu(z[:, sl]))
    return (jnp.concatenate(parts, axis=1),)


def _merge(ml, za, zb, zc):
    d = D_MODEL
    return (_sigmoid(ml[:, :d]) * za + _sigmoid(ml[:, d:2 * d]) * zb + _sigmoid(ml[:, 2 * d:]) * zc,)


def add_norm(name, x, pend, scale, gain):
    if pend is None:
        return x, rowwise(name, lambda a, g: (_rms(a, g),), (x,), (gain,))[0]

    def f(a, b, g):
        s = a + scale * b
        return s, _rms(s, g)

    return rowwise(name, f, (x, pend), (gain,))


W_IN_PIECES = (("rgx", 0, 1024), ("gate", 1024, 1024), ("att", 2048, 2304), ("dq", 4352, 1024), ("dk", 5376, 1024),
               ("dv", 6400, 1024), ("dz", 7424, 1024), ("ba", 8448, 16), ("mrg", 8464, 3072))
RG_PAR_BLOCKS = ("lane", "lane", "blk", "lane", "blk", "lane", "lane")


def mixer(name, u, w, p):
    mm = lambda nm, a, wt: mm_rows(nm, a[None], wt[None])
    pr = project_in(name + "_in", u, {k: w["in_" + k] for k, _, _ in W_IN_PIECES})
    ya = colwise(name + "_rg", _rg_block, (pr["rgx"], pr["gate"]),
                 (w["rg_conv_w"], p["rg_conv_b"], _block_diag(p["rg_w_r"]), p["rg_b_r"], _block_diag(p["rg_w_i"]),
                  p["rg_b_i"], p["rg_lambda"]), RG_PAR_BLOCKS, 1)[0]
    yb = att_branch(name, pr["att"], p["att_q_norm"], p["att_k_norm"])
    cw = w["dn_conv_w"]
    cq = colwise(name + "_dnq", _dn_conv_block("q"), (pr["dq"],), (cw[:, :1024],), ("lane",), 1)[0]
    ck = colwise(name + "_dnk", _dn_conv_block("k"), (pr["dk"],), (cw[:, 1024:2048],), ("lane",), 1)[0]
    cv = colwise(name + "_dnv", _dn_conv_block("v"), (pr["dv"],), (cw[:, 2048:],), ("lane",), 1)[0]
    gb = dn_gates(name + "_dngate", pr["ba"], p["dn_a_log"], p["dn_dt_bias"])
    o_dn = dn_core(cq, ck, cv, gb)
    yc = rowwise(name + "_dnout", _dn_out, (o_dn, pr["dz"]), (p["dn_out_norm"].reshape(1, D_MODEL),))[0]
    y = rowwise(name + "_merge", _merge, (pr["mrg"], mm(name + "_ba", ya, w["br_a"]), mm(name + "_bb", yb, w["br_b"]),
                                          mm(name + "_bc", yc, w["br_c"])))[0]
    return mm(name + "_out", y, w["w_out"])


def _loss_call(x, pend, target):
    t, d = x.shape
    tile = _row_tile(t)

    def body(x_ref, p_ref, t_ref, loss_ref, g_ref):
        err = x_ref[...] + 0.5 * p_ref[...] - t_ref[...]
        g_ref[...] = err * (1.0 / d)

        @pl.when(pl.program_id(0) == 0)
        def _():
            loss_ref[...] = jnp.zeros_like(loss_ref)

        loss_ref[...] += jnp.full(loss_ref.shape, 0.5 / d, F32) * jnp.sum(err * err)

    blk = pl.BlockSpec((tile, d), lambda i: (i, 0))
    loss, g = pl.pallas_call(
        body, name="loss", grid=(t // tile,), in_specs=[blk, blk, blk],
        out_specs=[pl.BlockSpec((8, LANES), lambda i: (0, 0)), blk],
        out_shape=[jax.ShapeDtypeStruct((8, LANES), F32), jax.ShapeDtypeStruct((t, d), F32)],
        compiler_params=_params("arbitrary"),
    )(x, pend, target)
    return loss[0, 0], g


@jax.custom_vjp
def loss_op(x, pend, target):
    return _loss_call(x, pend, target)[0]


def _loss_fwd(x, pend, target):
    loss, g = _loss_call(x, pend, target)
    return loss, g


def _loss_bwd(g, ct):
    return ct * g, (0.5 * ct) * g, None


loss_op.defvjp(_loss_fwd, _loss_bwd)


def first_ffn(wg, wu, wd, gain, x):
    x, h = add_norm("L0_n1", x, None, 0.0, gain)
    return x, ffn("L0_f1", h, wg, wu, wd)


def rest_of_step(g, conv, p, x, pend, target):
    scale = 0.5
    w = [split_layer({n: g[n, l] for n, _ in MATRICES if (n, l) in g}, {n: conv[n][l] for n, _ in CONVS}) for l in range(len(p))]
    for l in range(len(p)):
        n = f"L{l}"
        if l > 0:
            x, h = add_norm(n + "_n1", x, pend, scale, p[l]["ffn1_norm"])
            pend, scale = ffn(n + "_f1", h, w[l]["ffn1_w_gate"], w[l]["ffn1_w_up"], w[l]["ffn1_w_down"]), 0.5
        x, h = add_norm(n + "_nm", x, pend, scale, p[l]["mix_norm"])
        pend, scale = mixer(n + "_mx", h, w[l], p[l]), 1.0
        x, h = add_norm(n + "_n2", x, pend, scale, p[l]["ffn2_norm"])
        pend, scale = ffn(n + "_f2", h, w[l]["ffn2_w_gate"], w[l]["ffn2_w_up"], w[l]["ffn2_w_down"]), 0.5
    return loss_op(x, pend, target)


WEIGHT_NAMES = ("ffn1_norm", "ffn1_w_gate", "ffn1_w_up", "ffn1_w_down", "mix_norm", "w_in", "rg_conv_w", "rg_conv_b",
                "rg_w_r", "rg_b_r", "rg_w_i", "rg_b_i", "rg_lambda", "att_q_norm", "att_k_norm", "dn_conv_w", "dn_a_log",
                "dn_dt_bias", "dn_out_norm", "w_branch", "w_out", "ffn2_norm", "ffn2_w_gate", "ffn2_w_up", "ffn2_w_down")
MATRICES = (("ffn1_w_gate", 2), ("ffn1_w_up", 2), ("ffn1_w_down", 1), ("w_in", 2), ("w_branch", 1), ("w_out", 1),
            ("ffn2_w_gate", 2), ("ffn2_w_up", 2), ("ffn2_w_down", 1))
CONVS = (("rg_conv_w", 2), ("dn_conv_w", 2))
SHARD_AXIS = dict(MATRICES + CONVS)
SMALL_NAMES = tuple(n for n in WEIGHT_NAMES if n not in SHARD_AXIS)
ROW_PARAMS = ("ffn1_norm", "mix_norm", "rg_conv_b", "rg_b_r", "rg_b_i", "rg_lambda", "ffn2_norm")
FFN_MATS = ("ffn1_w_gate", "ffn1_w_up", "ffn1_w_down", "ffn2_w_gate", "ffn2_w_up", "ffn2_w_down")
TRANSPOSED_MATS = ("ffn1_w_gate", "ffn1_w_up", "ffn2_w_gate", "ffn2_w_up")
W_IN_SHARD = 2884
GATHER_ORDER = ((("ffn1_w_gate", 0), ("ffn1_w_up", 0), ("ffn1_w_down", 0)),
                (("w_in", 0), ("w_branch", 0), ("w_out", 0)),
                None)
GATHER_IDS = (1, 6, 7)
LATE_MATS = ("ffn2_w_gate", "ffn2_w_up", "ffn2_w_down", "w_out", "w_branch")
EXCHANGE_GROUPS = (lambda n, l: l == 1 and n in LATE_MATS,
                   lambda n, l: (l == 1) != (n in LATE_MATS),
                   lambda n, l: l == 0 and n == "w_in",
                   lambda n, l: l == 0 and n not in LATE_MATS and n != "w_in")


def _shard_minor(a, axis):
    a = jnp.moveaxis(a, 0, axis)
    return a.reshape(a.shape[:axis] + (N_CHIPS * a.shape[axis + 1],) + a.shape[axis + 2:])


def _w_in_piece(g, off, n):
    s = W_IN_SHARD
    parts = [g[j][:, max(off, j * s) - j * s:min(off + n, (j + 1) * s) - j * s]
             for j in range(N_CHIPS) if max(off, j * s) < min(off + n, (j + 1) * s)]
    return jnp.concatenate(parts, axis=1) if len(parts) > 1 else parts[0]


def _w_in_chip_grad(gl, j):
    s = W_IN_SHARD
    parts = [gl["in_" + k][:, max(off, j * s) - off:min(off + n, (j + 1) * s) - off]
             for k, off, n in W_IN_PIECES if max(off, j * s) < min(off + n, (j + 1) * s)]
    return jnp.concatenate(parts, axis=1)


def _layer_weights(g, conv):
    w = {n: g[n] for n in FFN_MATS if n in g}
    w["w_out"] = g["w_out"].reshape(D_MODEL, D_MODEL)
    for k, off, n in W_IN_PIECES:
        piece = _w_in_piece(g["w_in"], off, n)
        w["in_" + k] = jnp.pad(piece, ((0, 0), (0, LANES - n))) if n < LANES else piece
    wb = g["w_branch"].reshape(-1, D_MODEL)
    w["br_a"], w["br_b"], w["br_c"] = wb[:1024], wb[1024:1792], wb[1792:]
    return dict(w, **conv)


def _layer_weight_grads(gl):
    out = {n: gl[n] for n in FFN_MATS if n in gl}
    out["w_out"] = gl["w_out"].reshape(N_CHIPS, -1, D_MODEL)
    out["w_branch"] = jnp.concatenate([gl["br_a"], gl["br_b"], gl["br_c"]], axis=0).reshape(N_CHIPS, -1, D_MODEL)
    out["w_in"] = jnp.stack([_w_in_chip_grad(gl, j) for j in range(N_CHIPS)])
    return out, {n: gl[n] for n, _ in CONVS}


@jax.custom_vjp
def split_layer(g, conv):
    return _layer_weights(g, conv)


split_layer.defvjp(lambda g, conv: (_layer_weights(g, conv), None), lambda _, gw: _layer_weight_grads(gw))


def layer_small(small, l):
    p = {n: small[n][l] for n in SMALL_NAMES}
    for n in ROW_PARAMS:
        p[n] = small[n][l:l + 1]
    return p


def layer_small_grads(gp, small):
    return {n: jnp.stack([g[n] for g in gp]).reshape(small[n].shape) for n in SMALL_NAMES}


HBM_SPEC = pl.BlockSpec(memory_space=pl.ANY)


def _place():
    x, y, c = lax.axis_index("x"), lax.axis_index("y"), lax.axis_index("c")
    other_chips = [(1 - x, y), (x, 1 - y), (1 - x, 1 - y)]
    return x, y, c, 2 * x + y, (x, y, 1 - c), other_chips


def _half_rows(ref, lead, hc):
    hr = ref.shape[-2] // 2
    return ref.at[(*lead, pl.ds(pl.multiple_of(hc * hr, 16), hr), slice(None))]


def _chip_index():
    return (2 * lax.axis_index("x") + lax.axis_index("y")).astype(jnp.int32).reshape(1)


def cast_into_blocks(name, w):
    l, rows, cols = w.shape
    tr = rows // 2

    def body(me_ref, w_ref, *o_refs):
        for a, o_ref in enumerate(o_refs):
            o_ref[...] = w_ref[a:a + 1].astype(BF16)

    return pl.pallas_call(
        body, name=name, out_shape=[jax.ShapeDtypeStruct((N_CHIPS, rows, cols), BF16)] * l,
        grid_spec=pltpu.PrefetchScalarGridSpec(
            num_scalar_prefetch=1, grid=(rows // tr,),
            in_specs=[pl.BlockSpec((l, tr, cols), lambda i, me: (0, i, 0))],
            out_specs=[pl.BlockSpec((1, tr, cols), lambda i, me: (me[0], i, 0))] * l),
        compiler_params=_params("parallel"),
    )(_chip_index(), w)


def _gather_blocks(bufs_in, bufs_out, send_sems, recv_sems):
    n = len(bufs_in)
    x, y, c, me, sibling, chips = _place()

    def copy(s, src, dst, to):
        return pltpu.make_async_remote_copy(src_ref=src, dst_ref=dst, send_sem=send_sems.at[s], recv_sem=recv_sems.at[s],
                                            device_id=to, device_id_type=MESH)

    first, passed = [], []
    for j, (cx, cy) in enumerate(chips):
        for i in range(n):
            cp = copy(6 * i + j, _half_rows(bufs_in[i], (me,), c), _half_rows(bufs_out[i], (me,), c), (cx, cy, c))
            cp.start()
            first.append(cp)
    for j, (cx, cy) in enumerate(chips):
        k = 2 * cx + cy
        for i in range(n):
            copy(6 * i + j, _half_rows(bufs_in[i], (me,), c), _half_rows(bufs_out[i], (k,), c), (cx, cy, c)).wait_recv()
            cp = copy(6 * i + 3 + j, _half_rows(bufs_out[i], (k,), c), _half_rows(bufs_out[i], (k,), c), sibling)
            cp.start()
            passed.append(cp)
    for j, (cx, cy) in enumerate(chips):
        k = 2 * cx + cy
        for i in range(n):
            copy(6 * i + 3 + j, _half_rows(bufs_in[i], (me,), c), _half_rows(bufs_out[i], (k,), 1 - c), sibling).wait_recv()
    for cp in first + passed:
        cp.wait_send()


def _handshake(peers):
    barrier = pltpu.get_barrier_semaphore()
    for p in peers:
        pl.semaphore_signal(barrier, inc=1, device_id=p, device_id_type=MESH)
    pl.semaphore_wait(barrier, len(peers))


def allgather_blocks_sc(name, bufs, collective_id):
    n = len(bufs)
    refs = [jax.new_ref(b, memory_space=pltpu.MemorySpace.HBM) for b in bufs]

    @pl.kernel(mesh=plsc.ScalarSubcoreMesh(axis_name="sequencer", num_cores=1), name=name,
               scratch_types=(pltpu.SemaphoreType.DMA((6 * n,)), pltpu.SemaphoreType.DMA((6 * n,))),
               compiler_params=pltpu.CompilerParams(collective_id=collective_id))
    def launch(send_sems, recv_sems):
        x, y, c, me, sibling, chips = _place()
        _handshake([(cx, cy, c) for cx, cy in chips] + [sibling])
        _gather_blocks(refs, refs, send_sems, recv_sems)

    launch()
    return [jax.freeze(r) for r in refs]


PEER_FLIPS = tuple((fx, fy, fc) for fx in (0, 1) for fy in (0, 1) for fc in (0, 1))[1:]


def exchange_pieces_sc(name, gs, collective_id):
    n = len(gs)

    def body(*refs):
        ins, outs = refs[:n], refs[n:2 * n]
        send_sems, recv_sems = refs[2 * n:]
        x, y, c, me, sibling, chips = _place()
        my_dev = 4 * x + 2 * y + c
        flip = lambda v, f: 1 - v if f else v
        peers = [(flip(x, fx), flip(y, fy), flip(c, fc)) for fx, fy, fc in PEER_FLIPS]
        _handshake(peers)
        sends = []
        for r, (px, py, pc) in enumerate(peers):
            for i in range(n):
                cp = pltpu.make_async_remote_copy(
                    src_ref=_half_rows(ins[i], (2 * px + py,), pc), dst_ref=outs[i].at[my_dev], send_sem=send_sems.at[7 * i + r],
                    recv_sem=recv_sems.at[7 * i + r], device_id=(px, py, pc), device_id_type=MESH)
                cp.start()
                sends.append(cp)
        for r, (px, py, pc) in enumerate(peers):
            for i in range(n):
                pltpu.make_async_remote_copy(
                    src_ref=_half_rows(ins[i], (me,), c), dst_ref=outs[i].at[4 * px + 2 * py + pc], send_sem=send_sems.at[7 * i + r],
                    recv_sem=recv_sems.at[7 * i + r], device_id=(px, py, pc), device_id_type=MESH).wait_recv()
        for cp in sends:
            cp.wait_send()

    return pl.kernel(
        body, name=name, mesh=plsc.ScalarSubcoreMesh(axis_name="sequencer", num_cores=1),
        out_type=[jax.ShapeDtypeStruct((N_DEV, g.shape[1] // 2, g.shape[2]), g.dtype) for g in gs],
        scratch_types=[pltpu.SemaphoreType.DMA((7 * n,)), pltpu.SemaphoreType.DMA((7 * n,))],
        compiler_params=pltpu.CompilerParams(collective_id=collective_id),
    )(*gs)


def sibling_share_halves(name, fs):
    n = len(fs)
    every = (slice(None),)

    def body(*refs):
        ins, outs = refs[:n], refs[n:2 * n]
        send_sems, recv_sems = refs[2 * n:]
        x, y, c, me, sibling, chips = _place()
        sends = []
        for i in range(n):
            cp = pltpu.make_async_remote_copy(src_ref=_half_rows(ins[i], every, c), dst_ref=_half_rows(outs[i], every, c),
                                              send_sem=send_sems.at[i], recv_sem=recv_sems.at[i], device_id=sibling, device_id_type=MESH)
            cp.start()
            sends.append(cp)
        for i in range(n):
            pltpu.make_async_remote_copy(src_ref=_half_rows(ins[i], every, c), dst_ref=_half_rows(outs[i], every, 1 - c),
                                         send_sem=send_sems.at[i], recv_sem=recv_sems.at[i], device_id=sibling,
                                         device_id_type=MESH).wait_recv()
        for cp in sends:
            cp.wait_send()

    return pl.pallas_call(
        body, name=name, out_shape=[jax.ShapeDtypeStruct(f.shape, f.dtype) for f in fs],
        in_specs=[HBM_SPEC] * n, out_specs=[HBM_SPEC] * n, input_output_aliases={i: i for i in range(n)},
        scratch_shapes=[pltpu.SemaphoreType.DMA((n,)), pltpu.SemaphoreType.DMA((n,))],
    )(*fs)


def allgather_small_sc(name, v, collective_id):
    def body(v_ref, out_ref, send_sems, recv_sems, local_sem):
        x, y, c, me, sibling, chips = _place()
        my_dev = 4 * x + 2 * y + c
        flip = lambda a, f: 1 - a if f else a
        peers = [(flip(x, fx), flip(y, fy), flip(c, fc)) for fx, fy, fc in PEER_FLIPS]
        _handshake(peers)
        mine = pltpu.make_async_copy(v_ref, out_ref.at[my_dev], local_sem)
        mine.start()
        sends = []
        for r, peer in enumerate(peers):
            cp = pltpu.make_async_remote_copy(src_ref=v_ref, dst_ref=out_ref.at[my_dev], send_sem=send_sems.at[r],
                                              recv_sem=recv_sems.at[r], device_id=peer, device_id_type=MESH)
            cp.start()
            sends.append(cp)
        for r, (px, py, pc) in enumerate(peers):
            pltpu.make_async_remote_copy(src_ref=v_ref, dst_ref=out_ref.at[4 * px + 2 * py + pc], send_sem=send_sems.at[r],
                                         recv_sem=recv_sems.at[r], device_id=(px, py, pc), device_id_type=MESH).wait_recv()
        for cp in sends:
            cp.wait_send()
        mine.wait()

    return pl.kernel(
        body, name=name, mesh=plsc.ScalarSubcoreMesh(axis_name="sequencer", num_cores=1),
        out_type=jax.ShapeDtypeStruct((N_DEV,) + v.shape, v.dtype),
        scratch_types=[pltpu.SemaphoreType.DMA((7,)), pltpu.SemaphoreType.DMA((7,)), pltpu.SemaphoreType.DMA],
        compiler_params=pltpu.CompilerParams(collective_id=collective_id),
    )(v)


SUM_BLOCK_ELEMS = 512 * 1024


def sum_slabs(name, b):
    k, h, w = b.shape

    def body(b_ref, o_ref):
        acc = b_ref[0].astype(F32)
        for i in range(1, k):
            acc = acc + b_ref[i].astype(F32)
        o_ref[...] = acc

    return pl.pallas_call(
        body, name=name, out_shape=jax.ShapeDtypeStruct((h, w), F32),
        in_specs=[pl.BlockSpec(memory_space=pltpu.VMEM)], out_specs=pl.BlockSpec(memory_space=pltpu.VMEM),
        compiler_params=pltpu.CompilerParams(vmem_limit_bytes=VMEM_LIMIT),
    )(b)


def sum_pieces(name, pieces, gs):
    nl = len(pieces)
    k, h, w = pieces[0].shape
    tile = max(t for t in range(16, h + 1, 16) if h % t == 0 and (t * w <= SUM_BLOCK_ELEMS or t == 16))
    nt = h // tile
    x, y, c = lax.axis_index("x"), lax.axis_index("y"), lax.axis_index("c")
    place = [v.astype(jnp.int32).reshape(1) for v in (c, 2 * x + y, 4 * x + 2 * y + c)]

    assert nl == 2

    def tile_of(l, a, i):
        return i * a if l else i * (1 - a) + (nt - 1) * a

    def body(c_ref, me_ref, dev_ref, *refs):
        p_refs, g_refs, o_ref = refs[:nl], refs[nl:2 * nl], refs[2 * nl]
        my_dev = dev_ref[0]
        for l in range(nl):
            @pl.when(pl.program_id(0) == l)
            def _():
                o_ref[0] = jnp.zeros(o_ref.shape[1:], F32)
                for d in range(k):
                    @pl.when(my_dev == d)
                    def _():
                        o_ref[0] += g_refs[l][0].astype(F32)

                    @pl.when(my_dev != d)
                    def _():
                        o_ref[0] += p_refs[l][d].astype(F32)

    in_specs = [pl.BlockSpec((k, tile, w), functools.partial(lambda l, a, i, cc, me, dev: (0, tile_of(l, a, i), 0), l))
                for l in range(nl)]
    in_specs += [pl.BlockSpec((1, tile, w), functools.partial(lambda l, a, i, cc, me, dev: (me[0], cc[0] * nt + tile_of(l, a, i), 0), l))
                 for l in range(nl)]
    return pl.pallas_call(
        body, name=name, out_shape=jax.ShapeDtypeStruct((nl, 2 * h, w), F32),
        grid_spec=pltpu.PrefetchScalarGridSpec(
            num_scalar_prefetch=3, grid=(nl, nt), in_specs=in_specs,
            out_specs=pl.BlockSpec((1, tile, w), lambda a, i, cc, me, dev: (a, cc[0] * nt + i, 0))),
        compiler_params=_params("arbitrary", "arbitrary"),
    )(*place, *pieces, *gs)


def _adam_block(w, g, m, v):
    m = ADAM_B1 * m + (1.0 - ADAM_B1) * g
    v = ADAM_B2 * v + (1.0 - ADAM_B2) * (g * g)
    m_hat = m / (1.0 - ADAM_B1 ** ADAM_STEP)
    v_hat = v / (1.0 - ADAM_B2 ** ADAM_STEP)
    return -ADAM_LR * (m_hat / (jnp.sqrt(v_hat) + ADAM_EPS) + ADAM_WD * w), m, v


def adamw(name, w, g, m, v):
    shape = w.shape
    cols = shape[-1]
    rows = w.size // cols
    tile = 128 if rows % 128 == 0 else rows
    flat = [a.reshape(rows, cols) for a in (w, g, m, v)]

    def body(w_ref, g_ref, m_ref, v_ref, d_ref, nm_ref, nv_ref):
        d_ref[...], nm_ref[...], nv_ref[...] = _adam_block(w_ref[...], g_ref[...], m_ref[...], v_ref[...])

    blk = pl.BlockSpec((tile, cols), lambda i: (i, 0))
    res = pl.pallas_call(
        body, name=name, grid=(rows // tile,), in_specs=[blk] * 4, out_specs=[blk] * 3,
        out_shape=[jax.ShapeDtypeStruct((rows, cols), F32)] * 3, compiler_params=_params("parallel"),
    )(*flat)
    return tuple(r.reshape(shape) for r in res)


def _pack_small(values):
    flat = jnp.concatenate([v.reshape(-1) for v in values.values()])
    n = flat.shape[0]
    total = -(-n // (8 * LANES)) * (8 * LANES)
    return jnp.pad(flat, (0, total - n)).reshape(-1, LANES)


def _unpack_small(v, shapes):
    flat = v.reshape(-1)
    out, off = {}, 0
    for n, shape in shapes.items():
        sz = int(np.prod(shape))
        out[n] = flat[off:off + sz].reshape(shape)
        off += sz
    return out


def kernel(x, ffn1_norm, ffn1_w_gate, ffn1_w_up, ffn1_w_down, mix_norm, w_in, rg_conv_w, rg_conv_b, rg_w_r, rg_b_r, rg_w_i, rg_b_i, rg_lambda, att_q_norm, att_k_norm, dn_conv_w, dn_a_log, dn_dt_bias, dn_out_norm, w_branch, w_out, ffn2_norm, ffn2_w_gate, ffn2_w_up, ffn2_w_down, loss_target, m_ffn1_norm, m_ffn1_w_gate, m_ffn1_w_up, m_ffn1_w_down, m_mix_norm, m_w_in, m_rg_conv_w, m_rg_conv_b, m_rg_w_r, m_rg_b_r, m_rg_w_i, m_rg_b_i, m_rg_lambda, m_att_q_norm, m_att_k_norm, m_dn_conv_w, m_dn_a_log, m_dn_dt_bias, m_dn_out_norm, m_w_branch, m_w_out, m_ffn2_norm, m_ffn2_w_gate, m_ffn2_w_up, m_ffn2_w_down, v_ffn1_norm, v_ffn1_w_gate, v_ffn1_w_up, v_ffn1_w_down, v_mix_norm, v_w_in, v_rg_conv_w, v_rg_conv_b, v_rg_w_r, v_rg_b_r, v_rg_w_i, v_rg_b_i, v_rg_lambda, v_att_q_norm, v_att_k_norm, v_dn_conv_w, v_dn_a_log, v_dn_dt_bias, v_dn_out_norm, v_w_branch, v_w_out, v_ffn2_norm, v_ffn2_w_gate, v_ffn2_w_up, v_ffn2_w_down):
    given = dict(locals())
    for n in TRANSPOSED_MATS:
        for pre in ("", "m_", "v_"):
            given[pre + n] = jnp.swapaxes(given[pre + n], 1, 2)
    small = {n: given[n] for n in SMALL_NAMES}
    n_layers = ffn1_norm.shape[0]
    mat_names = [n for n, _ in MATRICES]
    conv_names = [n for n, _ in CONVS]

    blocks = {}
    for n in mat_names:
        for l, b in enumerate(cast_into_blocks("cast_" + n, given[n])):
            blocks[n, l] = b
    first, done = {}, []
    for i, wanted in enumerate(GATHER_ORDER[:-1]):
        bufs, _ = lax.optimization_barrier(([blocks[k] for k in wanted], done))
        done = allgather_blocks_sc(f"allgather_{i}", bufs, GATHER_IDS[i])
        first.update(zip(wanted, done))
    rest = {k: b for k, b in blocks.items() if k not in first}
    taps = jnp.concatenate([given[n].reshape(-1) for n in conv_names]).reshape(-1, LANES)
    taps = allgather_small_sc("allgather_taps", taps, 8).reshape(N_CHIPS, 2, -1)[:, 0]
    conv, off = {}, 0
    for n, ax in CONVS:
        sz = given[n].size
        conv[n] = _shard_minor(taps[:, off:off + sz].reshape((N_CHIPS,) + given[n].shape), ax)
        off += sz
    p = [layer_small(small, l) for l in range(n_layers)]

    ffn1_keys = GATHER_ORDER[0]
    (x1, pend), first_vjp = jax.vjp(first_ffn, *[first[k] for k in ffn1_keys], p[0]["ffn1_norm"], x[0])
    keys = list(rest)
    bufs, pend, second = lax.optimization_barrier(([rest[k] for k in keys], pend, [first[k] for k in GATHER_ORDER[1]]))
    gathered = dict(zip(keys, allgather_blocks_sc("allgather_2", bufs, GATHER_IDS[2])))
    gathered.update(zip(GATHER_ORDER[1], second))
    loss, (g_mats, g_conv, gp, gx1, gpend) = jax.value_and_grad(rest_of_step, argnums=(0, 1, 2, 3, 4))(
        gathered, conv, p, x1, pend, loss_target[0])
    *g_ffn1, gp[0]["ffn1_norm"], gx = first_vjp((gx1, gpend))
    g_mats.update(zip(ffn1_keys, g_ffn1))

    pieces = {}
    for i, group in enumerate(EXCHANGE_GROUPS):
        keys = [k for k in g_mats if group(*k)]
        pieces.update(zip(keys, exchange_pieces_sc(f"exchange_{i}", [g_mats[k] for k in keys], 2 + i)))
    halves = {n: sum_pieces("sum_" + n, [pieces[n, l] for l in range(n_layers)], [g_mats[n, l] for l in range(n_layers)])
              for n in mat_names}
    grads = {}
    for tag, names in (("late", [n for n in mat_names if n in LATE_MATS]), ("early", [n for n in mat_names if n not in LATE_MATS])):
        grads.update(zip(names, sibling_share_halves("share_" + tag, [halves[n] for n in names])))

    g_small = dict(layer_small_grads(gp, small), **g_conv, loss=loss.reshape(1))
    packed_small = _pack_small(g_small)
    slabs = allgather_small_sc("allgather_small", packed_small, 9)
    summed =_unpack_small(sum_slabs("sum_small", slabs), {n: g.shape for n, g in g_small.items()})
    chip = 2 * lax.axis_index("x") + lax.axis_index("y")
    for n in SMALL_NAMES:
        grads[n] = summed[n]
    for n, ax in CONVS:
        s = given[n].shape[ax]
        grads[n] = lax.dynamic_slice_in_dim(summed[n], chip * s, s, axis=ax)

    upd = {n: adamw("adamw_" + n, given[n], grads[n], given["m_" + n], given["v_" + n]) for n in WEIGHT_NAMES}
    out = lambda n, a: jnp.swapaxes(a, 1, 2) if n in TRANSPOSED_MATS else a
    return (summed["loss"][0], gx[None], *[out(n, grads[n]) for n in WEIGHT_NAMES], *[out(n, upd[n][0]) for n in WEIGHT_NAMES],
            *[out(n, upd[n][1]) for n in WEIGHT_NAMES], *[out(n, upd[n][2]) for n in WEIGHT_NAMES])
```

```python
import functools
import math

import jax
import jax.numpy as jnp
import numpy as np
from jax import lax
from jax.experimental import pallas as pl
from jax.experimental.pallas import tpu as pltpu
from jax.experimental.pallas import tpu_sc as plsc

F32 = jnp.float32
BF16 = jnp.bfloat16
MESH = pl.DeviceIdType.MESH

D_MODEL = 1024
FFN_DIM = 2816
RG_C = 8.0
ATT_GROUPS = ((128, 1), (512, 4), (2048, 16))
ATT_HEADS = 12
ATT_HEAD_DIM = 64
ATT_SPAN = 128
DN_HEADS = 8
DN_HEAD_DIM = 128
DN_CHUNK = 64
EPS = 1e-6
NEG_INF = -1e30
N_CHIPS = 4
N_DEV = 8

ADAM_LR, ADAM_B1, ADAM_B2, ADAM_EPS, ADAM_WD, ADAM_STEP = 0.001, 0.9, 0.999, 1e-08, 0.01, 10

LANES = 128
VMEM_LIMIT = 56 * 1024 * 1024


def _params(*sem):
    return pltpu.CompilerParams(dimension_semantics=sem or None, vmem_limit_bytes=VMEM_LIMIT)


def _sigmoid(x):
    return 1.0 / (1.0 + jnp.exp(-x))


def _silu(x):
    return x * _sigmoid(x)


def _softplus(x):
    return jnp.maximum(x, 0.0) + jnp.log(1.0 + jnp.exp(-jnp.abs(x)))


def _gelu(x):
    return 0.5 * x * (1.0 + jnp.tanh(math.sqrt(2.0 / math.pi) * (x + 0.044715 * (x * x * x))))


def _neg_expm1(x):
    series = -x * (1.0 + x * (0.5 + x * (1.0 / 6 + x * (1.0 / 24 + x * (1.0 / 120 + x * (1.0 / 720))))))
    return jnp.where(x > -0.25, series, 1.0 - jnp.exp(x))


def _rms(x, g):
    return x * lax.rsqrt(jnp.mean(x * x, axis=-1, keepdims=True) + EPS) * g


_MM_DIMS = {"nn": (((1,), (0,)), ((), ())), "nt": (((1,), (1,)), ((), ())), "tn": (((0,), (0,)), ((), ()))}


def _split(a):
    hi = a.astype(BF16)
    return hi, (a - hi.astype(F32)).astype(BF16)


def _mxu(a, b, form, passes):
    (ca, cb), _ = _MM_DIMS[form]
    if a.ndim == 3:
        dims = (((ca[0] + 1,), (cb[0] + 1,)), ((0,), (0,)))
    else:
        dims = _MM_DIMS[form]
    dg = lambda p, q: lax.dot_general(p, q, dims, preferred_element_type=F32)
    if passes == 1:
        return dg(a.astype(BF16), b.astype(BF16))
    (a_hi, a_lo), (b_hi, b_lo) = _split(a), _split(b)
    return dg(a_hi, b_hi) + (dg(a_hi, b_lo) + dg(a_lo, b_hi))


@functools.partial(jax.custom_vjp, nondiff_argnums=(2, 3))
def _mm(a, b, form, passes):
    return _mxu(a, b, form, passes)


def _mm_fwd(a, b, form, passes):
    return _mxu(a, b, form, passes), (a, b)


def _mm_bwd(form, passes, res, g):
    a, b = res
    if form == "nn":
        return _mm(g, b, "nt", passes), _mm(a, g, "tn", passes)
    if form == "nt":
        return _mm(g, b, "nn", passes), _mm(g, a, "tn", passes)
    return _mm(b, g, "nt", passes), _mm(a, g, "nn", passes)


_mm.defvjp(_mm_fwd, _mm_bwd)


def _dot(a, b):
    return _mm(a, b, "nn", 1)


def _dot_nt(a, b):
    return _mm(a, b, "nt", 1)


def _dot_tn(a, b):
    return _mm(a, b, "tn", 1)


def _dot3(a, b):
    return _mm(a, b, "nn", 3)


def _rows(shape):
    return lax.broadcasted_iota(jnp.int32, shape, len(shape) - 2)


def _roll_down(x, s, fill):
    return jnp.where(_rows(x.shape) >= s, pltpu.roll(x, s, x.ndim - 2), fill)


def _roll_up(x, s, fill):
    n = x.shape[-2]
    return jnp.where(_rows(x.shape) < n - s, pltpu.roll(x, n - s, x.ndim - 2), fill)


@functools.partial(jax.custom_vjp, nondiff_argnums=(1,))
def _shift(x, s):
    return _roll_down(x, s, 0.0)


def _shift_fwd(x, s):
    return _roll_down(x, s, 0.0), None


def _shift_bwd(s, _, g):
    return (_roll_up(g, s, 0.0),)


_shift.defvjp(_shift_fwd, _shift_bwd)


def _causal_conv(x, w):
    return w[0:1] * _shift(x, 3) + w[1:2] * _shift(x, 2) + w[2:3] * _shift(x, 1) + w[3:4] * x


@jax.custom_vjp
def _lin_scan(a, b):
    return _lin_scan_fwd(a, b)[0]


def _lin_scan_fwd(a, b):
    a0 = a
    s = 1
    while s < a.shape[0]:
        b = a * _roll_down(b, s, 0.0) + b
        a = a * _roll_down(a, s, 1.0)
        s *= 2
    return b, (a0, b)


def _lin_scan_bwd(res, g):
    a, h = res
    c = _roll_up(a, 1, 0.0)
    s = 1
    while s < a.shape[0]:
        g = c * _roll_up(g, s, 0.0) + g
        c = c * _roll_up(c, s, 1.0)
        s *= 2
    return g * _roll_down(h, 1, 0.0), g


_lin_scan.defvjp(_lin_scan_fwd, _lin_scan_bwd)


@jax.custom_vjp
def _cumsum_rows(x):
    s = 1
    while s < x.shape[-2]:
        x = x + _roll_down(x, s, 0.0)
        s *= 2
    return x


def _cumsum_rows_fwd(x):
    return _cumsum_rows(x), None


def _cumsum_rows_bwd(_, g):
    s = 1
    while s < g.shape[-2]:
        g = g + _roll_up(g, s, 0.0)
        s *= 2
    return (g,)


_cumsum_rows.defvjp(_cumsum_rows_fwd, _cumsum_rows_bwd)


ROW_BLOCK_BYTES = 14 * 1024 * 1024


def _row_tile(t, width=0):
    for tile in (512, 256):
        if t % tile == 0 and (tile == 256 or tile * width * 4 <= ROW_BLOCK_BYTES):
            return tile
    return t


def _rowwise_fwd_call(name, f, rows, pars, tile):
    t = rows[0].shape[0]
    outs = jax.eval_shape(f, *[jax.ShapeDtypeStruct((tile, r.shape[1]), F32) for r in rows],
                          *[jax.ShapeDtypeStruct(p.shape, F32) for p in pars])
    nr, npar = len(rows), len(pars)

    def body(*refs):
        ins = [r[...] for r in refs[:nr + npar]]
        res = f(*ins)
        for o_ref, o in zip(refs[nr + npar:], res):
            o_ref[...] = o.astype(o_ref.dtype)

    return pl.pallas_call(
        body, name=name, grid=(t // tile,),
        in_specs=[pl.BlockSpec((tile, r.shape[1]), lambda i: (i, 0)) for r in rows]
        + [pl.BlockSpec(p.shape, lambda i: (0, 0)) for p in pars],
        out_specs=[pl.BlockSpec((tile, o.shape[1]), lambda i: (i, 0)) for o in outs],
        out_shape=[jax.ShapeDtypeStruct((t, o.shape[1]), F32) for o in outs],
        compiler_params=_params("parallel"),
    )(*rows, *pars)


def _rowwise_bwd_call(name, f, rows, pars, cts, tile):
    t = rows[0].shape[0]
    nr, npar, nct = len(rows), len(pars), len(cts)

    def body(*refs):
        ins = [r[...] for r in refs[:nr + npar]]
        gs = tuple(r[...] for r in refs[nr + npar:nr + npar + nct])
        outs = refs[nr + npar + nct:]
        _, vjp = jax.vjp(f, *ins)
        d = vjp(gs)
        for o_ref, v in zip(outs[:nr], d[:nr]):
            o_ref[...] = v

        @pl.when(pl.program_id(0) == 0)
        def _():
            for o_ref in outs[nr:]:
                o_ref[...] = jnp.zeros_like(o_ref)

        for o_ref, v in zip(outs[nr:], d[nr:]):
            o_ref[...] += v

    res = pl.pallas_call(
        body, name=name, grid=(t // tile,),
        in_specs=[pl.BlockSpec((tile, r.shape[1]), lambda i: (i, 0)) for r in rows]
        + [pl.BlockSpec(p.shape, lambda i: (0, 0)) for p in pars]
        + [pl.BlockSpec((tile, c.shape[1]), lambda i: (i, 0)) for c in cts],
        out_specs=[pl.BlockSpec((tile, r.shape[1]), lambda i: (i, 0)) for r in rows]
        + [pl.BlockSpec(p.shape, lambda i: (0, 0)) for p in pars],
        out_shape=[jax.ShapeDtypeStruct(r.shape, F32) for r in rows]
        + [jax.ShapeDtypeStruct(p.shape, F32) for p in pars],
        compiler_params=_params("arbitrary"),
    )(*rows, *pars, *cts)
    return tuple(res[:nr]), tuple(res[nr:])


def rowwise(name, f, rows, pars=()):
    outs = jax.eval_shape(f, *[jax.ShapeDtypeStruct((8, r.shape[1]), F32) for r in rows],
                          *[jax.ShapeDtypeStruct(p.shape, F32) for p in pars])
    tile = _row_tile(rows[0].shape[0], 2 * sum(r.shape[1] for r in rows) + sum(o.shape[1] for o in outs))

    @jax.custom_vjp
    def op(rows, pars):
        return tuple(_rowwise_fwd_call(name, f, rows, pars, tile))

    def op_fwd(rows, pars):
        return op(rows, pars), (rows, pars)

    def op_bwd(res, cts):
        return _rowwise_bwd_call(name + "_bwd", f, res[0], res[1], tuple(cts), tile)

    op.defvjp(op_fwd, op_bwd)
    return op(tuple(rows), tuple(pars))


MM_TM = 512


def _tile_of(n, cap):
    best = None
    for c in range(LANES, min(n, cap) + 1, LANES):
        if n % c == 0:
            best = c
    return best or n


def _proj_dw(name, h, dys):
    m, k = h.shape
    n, tm = len(dys), 256
    steps = m // tm

    def body(h_ref, *refs):
        dy_refs, o_refs, accs = refs[:n], refs[n:2 * n], refs[2 * n:]
        ht = jnp.transpose(h_ref[...]).astype(BF16)
        first = pl.program_id(0) == 0
        for dy_ref, acc in zip(dy_refs, accs):
            for c0 in range(0, acc.shape[1], 1024):
                cols = slice(c0, min(c0 + 1024, acc.shape[1]))
                part = _dot(ht, dy_ref[:, cols])

                @pl.when(first)
                def _():
                    acc[:, cols] = part

                @pl.when(jnp.logical_not(first))
                def _():
                    acc[:, cols] += part

        @pl.when(pl.program_id(0) == steps - 1)
        def _():
            for o_ref, acc in zip(o_refs, accs):
                o_ref[...] = acc[...].astype(BF16)

    row = lambda width: pl.BlockSpec((tm, width), lambda i: (i, 0))
    return pl.pallas_call(
        body, name=name, grid=(steps,),
        in_specs=[row(k)] + [row(d.shape[1]) for d in dys],
        out_specs=[pl.BlockSpec((k, d.shape[1]), lambda i: (0, 0)) for d in dys],
        out_shape=[jax.ShapeDtypeStruct((k, d.shape[1]), BF16) for d in dys],
        scratch_shapes=[pltpu.VMEM((k, d.shape[1]), F32) for d in dys],
        compiler_params=_params("arbitrary"),
    )(h, *dys)


PROJ_GROUP_COLS = 4608


def _proj_dh(name, dys, ws, acc):
    m, k = dys[0].shape[0], ws[0].shape[0]
    n, tm = len(dys), 256

    def body(*refs):
        dy_refs, w_refs, rest = refs[:n], refs[n:2 * n], refs[2 * n:]
        total = _dot_nt(dy_refs[0][...], w_refs[0][...])
        for dy_ref, w_ref in zip(dy_refs[1:], w_refs[1:]):
            total = total + _dot_nt(dy_ref[...], w_ref[...])
        if acc is not None:
            total = total + rest[0][...]
        rest[-1][...] = total

    row = lambda width: pl.BlockSpec((tm, width), lambda i: (i, 0))
    return pl.pallas_call(
        body, name=name, grid=(m // tm,),
        in_specs=[row(d.shape[1]) for d in dys] + [pl.BlockSpec(w.shape, lambda i: (0, 0)) for w in ws] + ([row(k)] if acc is not None else []),
        out_specs=row(k), out_shape=jax.ShapeDtypeStruct((m, k), F32), compiler_params=_params("parallel"),
    )(*dys, *ws, *([acc] if acc is not None else []))


def _proj_fwd(name, h, ws):
    m, k = h.shape
    n, tm = len(ws), 256

    def body(h_ref, *refs):
        hv = h_ref[...].astype(BF16)
        for w_ref, o_ref in zip(refs[:n], refs[n:]):
            o_ref[...] = _dot(hv, w_ref[...])

    row = lambda width: pl.BlockSpec((tm, width), lambda i: (i, 0))
    return pl.pallas_call(
        body, name=name, grid=(m // tm,),
        in_specs=[row(k)] + [pl.BlockSpec(w.shape, lambda i: (0, 0)) for w in ws],
        out_specs=[row(w.shape[1]) for w in ws],
        out_shape=[jax.ShapeDtypeStruct((m, w.shape[1]), F32) for w in ws], compiler_params=_params("parallel"),
    )(h, *ws)


def project_in(name, h, ws):
    keys = list(ws)
    groups, cols = [[]], 0
    for p in keys:
        if groups[-1] and cols + ws[p].shape[1] > PROJ_GROUP_COLS:
            groups.append([])
            cols = 0
        groups[-1].append(p)
        cols += ws[p].shape[1]

    @jax.custom_vjp
    def op(h, ws):
        out = {}
        for i, group in enumerate(groups):
            out.update(zip(group, _proj_fwd(f"{name}_{i}", h, [ws[p] for p in group])))
        return out

    def op_fwd(h, ws):
        return op(h, ws), (h, ws)

    def op_bwd(res, dys):
        h, ws = res
        dh, dws = None, {}
        for i, group in enumerate(groups):
            dh = _proj_dh(f"{name}_dh{i}", [dys[p] for p in group], [ws[p] for p in group], dh)
            dws.update(zip(group, _proj_dw(f"{name}_dw{i}", h, [dys[p] for p in group])))
        return dh, dws

    op.defvjp(op_fwd, op_bwd)
    return op(h, ws)


def _ffn_up(name, h, wt):
    m, k = h.shape
    j, n, _ = wt.shape
    tm = MM_TM

    def body(h_ref, w_ref, o_ref):
        o_ref[0] = _dot_nt(h_ref[...], w_ref[0])

    return pl.pallas_call(
        body, name=name, grid=(m // tm, j),
        in_specs=[pl.BlockSpec((tm, k), lambda i, b: (i, 0)), pl.BlockSpec((1, n, k), lambda i, b: (b, 0, 0))],
        out_specs=pl.BlockSpec((1, tm, n), lambda i, b: (b, i, 0)),
        out_shape=jax.ShapeDtypeStruct((j, m, n), F32), compiler_params=_params("parallel", "parallel"),
    )(h, wt)


def _ffn_down(name, g, u, wd):
    j, m, n = g.shape
    d = wd.shape[2]
    tm = MM_TM

    def body(g_ref, u_ref, w_ref, o_ref):
        part = _dot(_silu(g_ref[0]) * u_ref[0], w_ref[0])

        @pl.when(pl.program_id(1) == 0)
        def _():
            o_ref[...] = part

        @pl.when(pl.program_id(1) > 0)
        def _():
            o_ref[...] += part

    act = pl.BlockSpec((1, tm, n), lambda i, b: (b, i, 0))
    return pl.pallas_call(
        body, name=name, grid=(m // tm, j),
        in_specs=[act, act, pl.BlockSpec((1, n, d), lambda i, b: (b, 0, 0))],
        out_specs=pl.BlockSpec((tm, d), lambda i, b: (i, 0)),
        out_shape=jax.ShapeDtypeStruct((m, d), F32), compiler_params=_params("parallel", "arbitrary"),
    )(g, u, wd)


def _ffn_down_bwd(name, dy, g, u, wd):
    j, m, n = g.shape
    d = wd.shape[2]
    tm = MM_TM

    def body(dy_ref, g_ref, u_ref, w_ref, dg_ref, du_ref):
        da = _dot_nt(dy_ref[...], w_ref[0])
        gv = g_ref[0]
        s = _sigmoid(gv)
        dg_ref[0] = da * u_ref[0] * (s * (1.0 + gv * (1.0 - s)))
        du_ref[0] = da * (gv * s)

    act = pl.BlockSpec((1, tm, n), lambda i, b: (b, i, 0))
    return pl.pallas_call(
        body, name=name, grid=(m // tm, j),
        in_specs=[pl.BlockSpec((tm, d), lambda i, b: (i, 0)), act, act, pl.BlockSpec((1, n, d), lambda i, b: (b, 0, 0))],
        out_specs=[act, act], out_shape=[jax.ShapeDtypeStruct((j, m, n), F32)] * 2,
        compiler_params=_params("parallel", "parallel"),
    )(dy, g, u, wd)


def _ffn_down_dw(name, g, u, dy):
    j, m, n = g.shape
    d = dy.shape[1]
    tn = _tile_of(d, 512)

    def body(g_ref, u_ref, dy_ref, o_ref):
        o_ref[0] = _dot_tn(_silu(g_ref[0]) * u_ref[0], dy_ref[...]).astype(BF16)

    act = pl.BlockSpec((1, m, n), lambda b, c: (b, 0, 0))
    return pl.pallas_call(
        body, name=name, grid=(j, d // tn),
        in_specs=[act, act, pl.BlockSpec((m, tn), lambda b, c: (0, c))],
        out_specs=pl.BlockSpec((1, n, tn), lambda b, c: (b, 0, c)),
        out_shape=jax.ShapeDtypeStruct((j, n, d), BF16), compiler_params=_params("parallel", "parallel"),
    )(g, u, dy)


def _ffn_up_dh(name, dg, du, wg, wu):
    j, m, n = dg.shape
    k = wg.shape[2]
    tm = MM_TM

    def body(dg_ref, du_ref, wg_ref, wu_ref, o_ref):
        part = _dot(dg_ref[0], wg_ref[0]) + _dot(du_ref[0], wu_ref[0])

        @pl.when(pl.program_id(1) == 0)
        def _():
            o_ref[...] = part

        @pl.when(pl.program_id(1) > 0)
        def _():
            o_ref[...] += part

    act = pl.BlockSpec((1, tm, n), lambda i, b: (b, i, 0))
    wsp = pl.BlockSpec((1, n, k), lambda i, b: (b, 0, 0))
    return pl.pallas_call(
        body, name=name, grid=(m // tm, j), in_specs=[act, act, wsp, wsp],
        out_specs=pl.BlockSpec((tm, k), lambda i, b: (i, 0)),
        out_shape=jax.ShapeDtypeStruct((m, k), F32), compiler_params=_params("parallel", "arbitrary"),
    )(dg, du, wg, wu)


def _ffn_up_dw(name, dy, h):
    j, m, n = dy.shape
    k = h.shape[1]
    tk = _tile_of(k, 512)

    def body(dy_ref, h_ref, o_ref):
        o_ref[0] = _dot_tn(dy_ref[0], h_ref[...]).astype(BF16)

    return pl.pallas_call(
        body, name=name, grid=(j, k // tk),
        in_specs=[pl.BlockSpec((1, m, n), lambda b, i: (b, 0, 0)), pl.BlockSpec((m, tk), lambda b, i: (0, i))],
        out_specs=pl.BlockSpec((1, n, tk), lambda b, i: (b, 0, i)),
        out_shape=jax.ShapeDtypeStruct((j, n, k), BF16), compiler_params=_params("parallel", "parallel"),
    )(dy, h)


def ffn(name, h, wg, wu, wd):
    @jax.custom_vjp
    def op(h, wg, wu, wd):
        return _ffn_down(name + "_d", _ffn_up(name + "_g", h, wg), _ffn_up(name + "_u", h, wu), wd)

    def op_fwd(h, wg, wu, wd):
        g, u = _ffn_up(name + "_g", h, wg), _ffn_up(name + "_u", h, wu)
        return _ffn_down(name + "_d", g, u, wd), (h, g, u, wg, wu, wd)

    def op_bwd(res, dy):
        h, g, u, wg, wu, wd = res
        dg, du = _ffn_down_bwd(name + "_d_bwd", dy, g, u, wd)
        return (_ffn_up_dh(name + "_dh", dg, du, wg, wu), _ffn_up_dw(name + "_g_dw", dg, h), _ffn_up_dw(name + "_u_dw", du, h),
                _ffn_down_dw(name + "_d_dw", g, u, dy))

    op.defvjp(op_fwd, op_bwd)
    return op(h, wg, wu, wd)


def _mmr_fwd(name, a, w):
    j, m, n = a.shape
    nn = w.shape[2]
    tm, tn = MM_TM, _tile_of(nn, 1024)

    def body(a_ref, w_ref, o_ref):
        part = _dot(a_ref[0], w_ref[0])

        @pl.when(pl.program_id(2) == 0)
        def _():
            o_ref[...] = part

        @pl.when(pl.program_id(2) > 0)
        def _():
            o_ref[...] += part

    return pl.pallas_call(
        body, name=name, grid=(m // tm, nn // tn, j),
        in_specs=[pl.BlockSpec((1, tm, n), lambda i, c, b: (b, i, 0)), pl.BlockSpec((1, n, tn), lambda i, c, b: (b, 0, c))],
        out_specs=pl.BlockSpec((tm, tn), lambda i, c, b: (i, c)),
        out_shape=jax.ShapeDtypeStruct((m, nn), F32),
        compiler_params=_params("parallel", "parallel", "arbitrary"),
    )(a, w)


def _mmr_da(name, dy, w):
    m, nn = dy.shape
    j, n, _ = w.shape
    tm = MM_TM

    def body(dy_ref, w_ref, o_ref):
        o_ref[0] = _dot_nt(dy_ref[...], w_ref[0])

    return pl.pallas_call(
        body, name=name, grid=(m // tm, j),
        in_specs=[pl.BlockSpec((tm, nn), lambda i, b: (i, 0)), pl.BlockSpec((1, n, nn), lambda i, b: (b, 0, 0))],
        out_specs=pl.BlockSpec((1, tm, n), lambda i, b: (b, i, 0)),
        out_shape=jax.ShapeDtypeStruct((j, m, n), F32),
        compiler_params=_params("parallel", "parallel"),
    )(dy, w)


def _mmr_dw(name, a, dy):
    j, m, n = a.shape
    nn = dy.shape[1]
    tn = _tile_of(nn, 512)

    def body(a_ref, dy_ref, o_ref):
        o_ref[0] = _dot_tn(a_ref[0], dy_ref[...]).astype(BF16)

    return pl.pallas_call(
        body, name=name, grid=(j, nn // tn),
        in_specs=[pl.BlockSpec((1, m, n), lambda b, c: (b, 0, 0)), pl.BlockSpec((m, tn), lambda b, c: (0, c))],
        out_specs=pl.BlockSpec((1, n, tn), lambda b, c: (b, 0, c)),
        out_shape=jax.ShapeDtypeStruct((j, n, nn), BF16),
        compiler_params=_params("parallel", "parallel"),
    )(a, dy)


def mm_rows(name, a, w):
    @jax.custom_vjp
    def op(a, w):
        return _mmr_fwd(name, a, w)

    def op_fwd(a, w):
        return op(a, w), (a, w)

    def op_bwd(res, dy):
        a, w = res
        return _mmr_da(name + "_da", dy, w), _mmr_dw(name + "_dw", a, dy)

    op.defvjp(op_fwd, op_bwd)
    return op(a, w)


def _colwise_specs(cols, pars, par_block):
    t = cols[0].shape[0]
    specs = [pl.BlockSpec((t, LANES), lambda j: (0, j)) for _ in cols]
    for p, blk in zip(pars, par_block):
        if blk == "lane":
            specs.append(pl.BlockSpec((p.shape[0], LANES), lambda j: (0, j)))
        else:
            specs.append(pl.BlockSpec((1,) + p.shape[1:], lambda j: (j, 0, 0)))
    return specs


def _colwise_fwd_call(name, f, cols, pars, par_block, n_out):
    t, c = cols[0].shape
    nc, npar = len(cols), len(pars)

    def body(*refs):
        ins = [r[...] for r in refs[:nc]] + [r[...] if b == "lane" else r[0] for r, b in zip(refs[nc:nc + npar], par_block)]
        res = f(*ins)
        for o_ref, o in zip(refs[nc + npar:], res):
            o_ref[...] = o

    return pl.pallas_call(
        body, name=name, grid=(c // LANES,),
        in_specs=_colwise_specs(cols, pars, par_block),
        out_specs=[pl.BlockSpec((t, LANES), lambda j: (0, j)) for _ in range(n_out)],
        out_shape=[jax.ShapeDtypeStruct((t, c), F32) for _ in range(n_out)],
        compiler_params=_params("parallel"),
    )(*cols, *pars)


def _colwise_bwd_call(name, f, cols, pars, par_block, cts):
    t, c = cols[0].shape
    nc, npar, nct = len(cols), len(pars), len(cts)

    def body(*refs):
        ins = [r[...] for r in refs[:nc]] + [r[...] if b == "lane" else r[0] for r, b in zip(refs[nc:nc + npar], par_block)]
        gs = tuple(r[...] for r in refs[nc + npar:nc + npar + nct])
        outs = refs[nc + npar + nct:]
        _, vjp = jax.vjp(f, *ins)
        d = vjp(gs)
        for o_ref, v in zip(outs[:nc], d[:nc]):
            o_ref[...] = v
        for o_ref, v, b in zip(outs[nc:], d[nc:], par_block):
            if b == "lane":
                o_ref[...] = v
            else:
                o_ref[0] = v

    res = pl.pallas_call(
        body, name=name, grid=(c // LANES,),
        in_specs=_colwise_specs(cols, pars, par_block) + [pl.BlockSpec((t, LANES), lambda j: (0, j)) for _ in cts],
        out_specs=_colwise_specs(cols, pars, par_block),
        out_shape=[jax.ShapeDtypeStruct(v.shape, F32) for v in (*cols, *pars)],
        compiler_params=_params("parallel"),
    )(*cols, *pars, *cts)
    return tuple(res[:nc]), tuple(res[nc:])


def colwise(name, f, cols, pars, par_block, n_out):
    @jax.custom_vjp
    def op(cols, pars):
        return tuple(_colwise_fwd_call(name, f, cols, pars, par_block, n_out))

    def op_fwd(cols, pars):
        return op(cols, pars), (cols, pars)

    def op_bwd(res, cts):
        return _colwise_bwd_call(name + "_bwd", f, res[0], res[1], par_block, tuple(cts))

    op.defvjp(op_fwd, op_bwd)
    return op(tuple(cols), tuple(pars))


def _rg_block(x, gate, cw, cb, wr, br, wi, bi, lam):
    xa = _causal_conv(x, cw) + cb
    r = _sigmoid(_dot(xa, wr) + br)
    i = _sigmoid(_dot(xa, wi) + bi)
    log_a = -RG_C * r * _softplus(-lam)
    a = jnp.exp(log_a)
    b = jnp.sqrt(_neg_expm1(2.0 * log_a)) * (i * xa)
    return (_lin_scan(a, b) * _gelu(gate),)


def _dn_conv_block(mode):
    def f(x, cw):
        c = _silu(_causal_conv(x, cw))
        if mode == "v":
            return (c,)
        c = c * lax.rsqrt(jnp.sum(c * c, axis=-1, keepdims=True) + EPS)
        return (c * (DN_HEAD_DIM ** -0.5),) if mode == "q" else (c,)
    return f


def _block_diag(w):
    w = w.reshape(8, 2, 64, 64)
    z = jnp.zeros((8, 64, 64), w.dtype)
    top = jnp.concatenate([w[:, 0], z], axis=2)
    bot = jnp.concatenate([z, w[:, 1]], axis=2)
    return jnp.concatenate([top, bot], axis=1)


DN_HP = 8


def _dn_block(S, qw, kw, vw, gb, h0, tinv=None):
    hp, hd = S.shape[0], DN_HEAD_DIM
    heads = lambda a: jnp.concatenate([a[None, :, j * hd:(j + 1) * hd] for j in range(hp)], axis=0)
    lane = lax.broadcasted_iota(jnp.int32, gb.shape, 1)
    col = lambda i: jnp.sum(jnp.where(lane == i, gb, 0.0), axis=1, keepdims=True)[None]
    beta = jnp.concatenate([col(h0 + j) for j in range(hp)], axis=0)
    g = jnp.concatenate([col(h0 + j + DN_HEADS) for j in range(hp)], axis=0)
    s_new, o, tinv = _dn_step(S, heads(qw), heads(kw), heads(vw), beta, g, tinv)
    return s_new, jnp.concatenate([o[j:j + 1].reshape(o.shape[1:]) for j in range(hp)], axis=1), tinv


@jax.custom_vjp
def _unit_lower_inverse(a):
    c = a.shape[-1]
    eye = (lax.broadcasted_iota(jnp.int32, (c, c), 0) == lax.broadcasted_iota(jnp.int32, (c, c), 1)).astype(F32)
    p = -a
    tinv = eye + p
    for _ in range(5):
        p = _dot3(p, p)
        tinv = tinv + _dot3(tinv, p)
    return tinv


def _unit_lower_inverse_fwd(a):
    t = _unit_lower_inverse(a)
    return t, t


def _unit_lower_inverse_bwd(t, g):
    return (-_mm(_mm(t, g, "tn", 3), t, "nt", 3),)


_unit_lower_inverse.defvjp(_unit_lower_inverse_fwd, _unit_lower_inverse_bwd)


@jax.custom_vjp
def _known_inverse(a, t):
    return t


_known_inverse.defvjp(lambda a, t: (t, t), lambda t, g: (_unit_lower_inverse_bwd(t, g)[0], jnp.zeros_like(t)))


def _dn_step(S, q, k, v, beta, g, tinv=None):
    c = DN_CHUNK
    ri = lax.broadcasted_iota(jnp.int32, (c, c), 0)
    ci = lax.broadcasted_iota(jnp.int32, (c, c), 1)
    incl, strict = ri >= ci, ri > ci
    gam = _cumsum_rows(g)
    gam_row = jnp.sum(jnp.where(ri <= ci, g, 0.0), axis=-2, keepdims=True)
    gam_last = jnp.sum(g, axis=-2, keepdims=True)
    decay = jnp.where(incl, jnp.exp(jnp.where(incl, gam - gam_row, 0.0)), 0.0)
    kb = k * beta
    vb = v * beta
    a = jnp.where(strict, _dot_nt(kb, k) * decay, 0.0)
    tinv = _unit_lower_inverse(a) if tinv is None else _known_inverse(a, tinv)
    e_gam = jnp.exp(gam)
    u0 = _dot3(tinv, vb)
    wk = _dot3(tinv, kb * e_gam)
    qk = jnp.where(incl, _dot_nt(q, k) * decay, 0.0)
    q_dec = q * e_gam
    k_dec = k * jnp.exp(gam_last - gam)
    u = u0 - _dot(wk, S)
    o = _dot(q_dec, S) + _dot(qk, u)
    s_new = S * jnp.exp(gam_last) + _dot_tn(k_dec, u)
    return s_new, o, tinv


def _dn_fwd_call(q, k, v, gb):
    t, w = q.shape
    n, hp, hd, c = t // DN_CHUNK, DN_HP, DN_HEAD_DIM, DN_CHUNK

    def body(q_ref, k_ref, v_ref, gb_ref, o_ref, s0_ref, ti_ref, s_scr):
        @pl.when(pl.program_id(1) == 0)
        def _():
            s_scr[...] = jnp.zeros_like(s_scr)

        s_old = s_scr[...]
        s0_ref[:, 0] = s_old
        s_new, o, tinv = _dn_block(s_old, q_ref[...], k_ref[...], v_ref[...], gb_ref[...], pl.program_id(0) * hp)
        o_ref[...] = o
        ti_ref[:, 0] = tinv
        s_scr[...] = s_new

    blk = pl.BlockSpec((c, hp * hd), lambda g, i: (i, g))
    return pl.pallas_call(
        body, name="dn_core", grid=(DN_HEADS // hp, n),
        in_specs=[blk, blk, blk, pl.BlockSpec((c, LANES), lambda g, i: (i, 0))],
        out_specs=[blk, pl.BlockSpec((hp, 1, hd, hd), lambda g, i: (g, i, 0, 0)), pl.BlockSpec((hp, 1, c, c), lambda g, i: (g, i, 0, 0))],
        out_shape=[jax.ShapeDtypeStruct((t, w), F32), jax.ShapeDtypeStruct((DN_HEADS, n, hd, hd), F32),
                   jax.ShapeDtypeStruct((DN_HEADS, n, c, c), F32)],
        scratch_shapes=[pltpu.VMEM((hp, hd, hd), F32)],
        compiler_params=_params("parallel", "arbitrary"),
    )(q, k, v, gb)


def _dn_bwd_call(q, k, v, gb, s0, ti, do):
    t, w = q.shape
    n, hp, hd, c = t // DN_CHUNK, DN_HP, DN_HEAD_DIM, DN_CHUNK
    ng = DN_HEADS // hp

    def body(q_ref, k_ref, v_ref, gb_ref, s0_ref, ti_ref, do_ref, dq_ref, dk_ref, dv_ref, dgb_ref, ds_scr):
        @pl.when(pl.program_id(1) == 0)
        def _():
            ds_scr[...] = jnp.zeros_like(ds_scr)

        h0, tinv = pl.program_id(0) * hp, ti_ref[:, 0]
        _, vjp = jax.vjp(lambda *a: _dn_block(*a, h0, tinv)[:2], s0_ref[:, 0], q_ref[...], k_ref[...], v_ref[...], gb_ref[...])
        ds, dq, dk, dv, dgb = vjp((ds_scr[...], do_ref[...]))
        ds_scr[...] = ds
        dq_ref[...], dk_ref[...], dv_ref[...] = dq, dk, dv
        dgb_ref[0] = dgb

    blk = pl.BlockSpec((c, hp * hd), lambda g, i: (n - 1 - i, g))
    res = pl.pallas_call(
        body, name="dn_core_bwd", grid=(ng, n),
        in_specs=[blk, blk, blk, pl.BlockSpec((c, LANES), lambda g, i: (n - 1 - i, 0)),
                  pl.BlockSpec((hp, 1, hd, hd), lambda g, i: (g, n - 1 - i, 0, 0)),
                  pl.BlockSpec((hp, 1, c, c), lambda g, i: (g, n - 1 - i, 0, 0)), blk],
        out_specs=[blk, blk, blk, pl.BlockSpec((1, c, LANES), lambda g, i: (g, n - 1 - i, 0))],
        out_shape=[jax.ShapeDtypeStruct((t, w), F32)] * 3 + [jax.ShapeDtypeStruct((ng, t, LANES), F32)],
        scratch_shapes=[pltpu.VMEM((hp, hd, hd), F32)],
        compiler_params=_params("parallel", "arbitrary"),
    )(q, k, v, gb, s0, ti, do)
    return res[0], res[1], res[2], jnp.sum(res[3], axis=0)


@jax.custom_vjp
def dn_core(q, k, v, gb):
    return _dn_fwd_call(q, k, v, gb)[0]


def _dn_core_fwd(q, k, v, gb):
    o, s0, ti = _dn_fwd_call(q, k, v, gb)
    return o, (q, k, v, gb, s0, ti)


def _dn_core_bwd(res, do):
    return _dn_bwd_call(*res, do)


dn_core.defvjp(_dn_core_fwd, _dn_core_bwd)


ATT_GH = 4


def _att_block(q, kp, kc, vp, vc, qn, kn, slope, has_prev, dil):
    s = ATT_SPAN
    qh = _rms(q, qn) * (ATT_HEAD_DIM ** -0.5)
    qi = lax.broadcasted_iota(jnp.int32, (s, s), 0)
    kj = lax.broadcasted_iota(jnp.int32, (s, s), 1)
    d_p = qi + s - kj
    d_c = qi - kj
    s_p = _dot_nt(qh, _rms(kp, kn)) - slope * (d_p * dil).astype(F32)
    s_c = _dot_nt(qh, _rms(kc, kn)) - slope * (d_c * dil).astype(F32)
    s_p = jnp.where((d_p <= s) & (has_prev > 0), s_p, NEG_INF)
    s_c = jnp.where(d_c >= 0, s_c, NEG_INF)
    m = lax.stop_gradient(jnp.maximum(jnp.max(s_p, axis=-1, keepdims=True), jnp.max(s_c, axis=-1, keepdims=True)))
    p_p = jnp.exp(s_p - m)
    p_c = jnp.exp(s_c - m)
    den = jnp.sum(p_p, axis=-1, keepdims=True) + jnp.sum(p_c, axis=-1, keepdims=True)
    o = _dot(p_p / den, vp) + _dot(p_c / den, vc)
    lse = m + jnp.log(den)
    return o, jnp.broadcast_to(lse, o.shape)


def _att_heads(a):
    e = ATT_HEAD_DIM
    return jnp.concatenate([a[None, :, h * e:(h + 1) * e] for h in range(ATT_GH)], axis=0)


def _att_lanes(a):
    return jnp.concatenate([a[h:h + 1].reshape(a.shape[1:]) for h in range(ATT_GH)], axis=1)


def _att_rows(q, kp, kc, vp, vc, qn, kn, group, has_prev, dil):
    head = lax.broadcasted_iota(jnp.int32, (ATT_GH, 1, 1), 0) + (ATT_GH * group + 1)
    slope = jnp.exp(head.astype(F32) * (-8.0 / ATT_HEADS * math.log(2.0)))
    o, lse = _att_block(_att_heads(q), _att_heads(kp), _att_heads(kc), _att_heads(vp), _att_heads(vc), qn, kn, slope, has_prev, dil)
    return _att_lanes(o), _att_lanes(lse)


def _att_specs(group, dil):
    blk = (ATT_SPAN, ATT_GH * ATT_HEAD_DIM)
    cur = lambda which: pl.BlockSpec(blk, lambda r, n: (n, r * 9 + 3 * which + group))
    prev = lambda which: pl.BlockSpec(blk, lambda r, n: (jnp.maximum(n - 1, 0), r * 9 + 3 * which + group))
    out = pl.BlockSpec(blk, lambda r, n: (n, r))
    gain = pl.BlockSpec((ATT_GH, 1, ATT_HEAD_DIM), lambda r, n: (0, 0, 0))
    return [cur(0), prev(1), cur(1), prev(2), cur(2), gain, gain], out, gain


def _att_fwd_call(name, group, dil, pa, qn, kn):
    t = pa.shape[0]
    l = t // dil
    w = ATT_GH * ATT_HEAD_DIM
    ins, out, _ = _att_specs(group, dil)
    pav = pa.reshape(l, dil * pa.shape[1])

    def body(q_ref, kp_ref, kc_ref, vp_ref, vc_ref, qn_ref, kn_ref, o_ref, lse_ref):
        o_ref[...], lse_ref[...] = _att_rows(q_ref[...], kp_ref[...], kc_ref[...], vp_ref[...], vc_ref[...], qn_ref[...],
                                             kn_ref[...], group, pl.program_id(1), dil)

    o, lse = pl.pallas_call(
        body, name=name, grid=(dil, l // ATT_SPAN), in_specs=ins, out_specs=[out, out],
        out_shape=[jax.ShapeDtypeStruct((l, dil * w), F32)] * 2, compiler_params=_params("parallel", "arbitrary"),
    )(pav, pav, pav, pav, pav, qn, kn)
    return o.reshape(t, w), lse.reshape(t, w)


def _att_bwd_call(name, group, dil, pa, qn, kn, do, dlse):
    t = pa.shape[0]
    l = t // dil
    w = ATT_GH * ATT_HEAD_DIM
    ins, out, gain = _att_specs(group, dil)
    pav = pa.reshape(l, dil * pa.shape[1])

    def body(q_ref, kp_ref, kc_ref, vp_ref, vc_ref, qn_ref, kn_ref, do_ref, dlse_ref,
             dq_ref, dkp_ref, dkc_ref, dvp_ref, dvc_ref, dqn_ref, dkn_ref):
        has_prev = pl.program_id(1)
        _, vjp = jax.vjp(lambda *a: _att_rows(*a, group, has_prev, dil), q_ref[...], kp_ref[...], kc_ref[...], vp_ref[...],
                         vc_ref[...], qn_ref[...], kn_ref[...])
        dq, dkp, dkc, dvp, dvc, dqn, dkn = vjp((do_ref[...], dlse_ref[...]))
        dq_ref[...], dkp_ref[...], dkc_ref[...], dvp_ref[...], dvc_ref[...] = dq, dkp, dkc, dvp, dvc

        @pl.when((pl.program_id(0) == 0) & (pl.program_id(1) == 0))
        def _():
            dqn_ref[...] = jnp.zeros_like(dqn_ref)
            dkn_ref[...] = jnp.zeros_like(dkn_ref)

        dqn_ref[...] += dqn
        dkn_ref[...] += dkn

    res = pl.pallas_call(
        body, name=name + "_bwd", grid=(dil, l // ATT_SPAN), in_specs=ins + [out, out],
        out_specs=[out] * 5 + [gain, gain],
        out_shape=[jax.ShapeDtypeStruct((l, dil * w), F32)] * 5 + [jax.ShapeDtypeStruct(qn.shape, F32)] * 2,
        compiler_params=_params("arbitrary", "arbitrary"),
    )(pav, pav, pav, pav, pav, qn, kn, do.reshape(l, dil * w), dlse.reshape(l, dil * w))
    dq, dkp, dkc, dvp, dvc, dqn, dkn = res
    back = lambda g: jnp.pad(g[ATT_SPAN:], ((0, ATT_SPAN), (0, 0)))
    return dq.reshape(t, w), (dkc + back(dkp)).reshape(t, w), (dvc + back(dvp)).reshape(t, w), dqn, dkn


def _att_mix(o1, o2, o3, l1, l2, l3):
    m = jnp.maximum(jnp.maximum(l1, l2), l3)
    e1, e2, e3 = jnp.exp(l1 - m), jnp.exp(l2 - m), jnp.exp(l3 - m)
    s = e1 + e2 + e3
    return (jnp.concatenate([o1 * (e1 / s), o2 * (e2 / s), o3 * (e3 / s)], axis=1),)


def att_branch(name, pa, qn, kn):
    e = ATT_HEAD_DIM
    gains = lambda p, g: p[ATT_GH * g:ATT_GH * (g + 1)].reshape(ATT_GH, 1, e)

    @jax.custom_vjp
    def groups(pa, qn, kn):
        res = [_att_fwd_call(f"{name}_att{g}", g, dil, pa, gains(qn, g), gains(kn, g)) for g, (_, dil) in enumerate(ATT_GROUPS)]
        return tuple(r[0] for r in res) + tuple(r[1] for r in res)

    def groups_fwd(pa, qn, kn):
        return groups(pa, qn, kn), (pa, qn, kn)

    def groups_bwd(res, cts):
        pa, qn, kn = res
        n = len(ATT_GROUPS)
        parts = [_att_bwd_call(f"{name}_att{g}", g, dil, pa, gains(qn, g), gains(kn, g), cts[g], cts[n + g])
                 for g, (_, dil) in enumerate(ATT_GROUPS)]
        d_pa = jnp.concatenate([p[i] for i in range(3) for p in parts], axis=1)
        return (d_pa, jnp.concatenate([p[3] for p in parts]).reshape(qn.shape), jnp.concatenate([p[4] for p in parts]).reshape(kn.shape))

    groups.defvjp(groups_fwd, groups_bwd)
    return rowwise(f"{name}_attmix", _att_mix, groups(pa, qn, kn))[0]


def dn_gates(name, ba, a_log, dt_bias):
    place = lambda p: jnp.pad(p.reshape(1, DN_HEADS), ((0, 0), (DN_HEADS, LANES - 2 * DN_HEADS)))

    def f(x, al, dt):
        lane = lax.broadcasted_iota(jnp.int32, x.shape, 1)
        return (jnp.where(lane < DN_HEADS, _sigmoid(x), -jnp.exp(al) * _softplus(x + dt)),)

    return rowwise(name, f, (ba,), (place(a_log), place(dt_bias)))[0]


def _dn_out(o, z, g):
    parts = []
    for h in range(DN_HEADS):
        sl = slice(h * DN_HEAD_DIM, (h + 1) * DN_HEAD_DIM)
        parts.append(_rms(o[:, sl], g[:, sl]) * _silu(z[:, sl]))
    return (jnp.concatenate(parts, axis=1),)


def _merge(ml, za, zb, zc):
    d = D_MODEL
    return (_sigmoid(ml[:, :d]) * za + _sigmoid(ml[:, d:2 * d]) * zb + _sigmoid(ml[:, 2 * d:]) * zc,)


def add_norm(name, x, pend, scale, gain):
    if pend is None:
        return x, rowwise(name, lambda a, g: (_rms(a, g),), (x,), (gain,))[0]

    def f(a, b, g):
        s = a + scale * b
        return s, _rms(s, g)

    return rowwise(name, f, (x, pend), (gain,))


W_IN_PIECES = (("rgx", 0, 1024), ("gate", 1024, 1024), ("att", 2048, 2304), ("dq", 4352, 1024), ("dk", 5376, 1024),
               ("dv", 6400, 1024), ("dz", 7424, 1024), ("ba", 8448, 16), ("mrg", 8464, 3072))
RG_PAR_BLOCKS = ("lane", "lane", "blk", "lane", "blk", "lane", "lane")


def mixer(name, u, w, p):
    mm = lambda nm, a, wt: mm_rows(nm, a[None], wt[None])
    pr = project_in(name + "_in", u, {k: w["in_" + k] for k, _, _ in W_IN_PIECES})
    ya = colwise(name + "_rg", _rg_block, (pr["rgx"], pr["gate"]),
                 (w["rg_conv_w"], p["rg_conv_b"], _block_diag(p["rg_w_r"]), p["rg_b_r"], _block_diag(p["rg_w_i"]),
                  p["rg_b_i"], p["rg_lambda"]), RG_PAR_BLOCKS, 1)[0]
    yb = att_branch(name, pr["att"], p["att_q_norm"], p["att_k_norm"])
    cw = w["dn_conv_w"]
    cq = colwise(name + "_dnq", _dn_conv_block("q"), (pr["dq"],), (cw[:, :1024],), ("lane",), 1)[0]
    ck = colwise(name + "_dnk", _dn_conv_block("k"), (pr["dk"],), (cw[:, 1024:2048],), ("lane",), 1)[0]
    cv = colwise(name + "_dnv", _dn_conv_block("v"), (pr["dv"],), (cw[:, 2048:],), ("lane",), 1)[0]
    gb = dn_gates(name + "_dngate", pr["ba"], p["dn_a_log"], p["dn_dt_bias"])
    o_dn = dn_core(cq, ck, cv, gb)
    yc = rowwise(name + "_dnout", _dn_out, (o_dn, pr["dz"]), (p["dn_out_norm"].reshape(1, D_MODEL),))[0]
    y = rowwise(name + "_merge", _merge, (pr["mrg"], mm(name + "_ba", ya, w["br_a"]), mm(name + "_bb", yb, w["br_b"]),
                                          mm(name + "_bc", yc, w["br_c"])))[0]
    return mm(name + "_out", y, w["w_out"])


def _loss_call(x, pend, target):
    t, d = x.shape
    tile = _row_tile(t)

    def body(x_ref, p_ref, t_ref, loss_ref, g_ref):
        err = x_ref[...] + 0.5 * p_ref[...] - t_ref[...]
        g_ref[...] = err * (1.0 / d)

        @pl.when(pl.program_id(0) == 0)
        def _():
            loss_ref[...] = jnp.zeros_like(loss_ref)

        loss_ref[...] += jnp.full(loss_ref.shape, 0.5 / d, F32) * jnp.sum(err * err)

    blk = pl.BlockSpec((tile, d), lambda i: (i, 0))
    loss, g = pl.pallas_call(
        body, name="loss", grid=(t // tile,), in_specs=[blk, blk, blk],
        out_specs=[pl.BlockSpec((8, LANES), lambda i: (0, 0)), blk],
        out_shape=[jax.ShapeDtypeStruct((8, LANES), F32), jax.ShapeDtypeStruct((t, d), F32)],
        compiler_params=_params("arbitrary"),
    )(x, pend, target)
    return loss[0, 0], g


@jax.custom_vjp
def loss_op(x, pend, target):
    return _loss_call(x, pend, target)[0]


def _loss_fwd(x, pend, target):
    loss, g = _loss_call(x, pend, target)
    return loss, g


def _loss_bwd(g, ct):
    return ct * g, (0.5 * ct) * g, None


loss_op.defvjp(_loss_fwd, _loss_bwd)


def first_ffn(wg, wu, wd, gain, x):
    x, h = add_norm("L0_n1", x, None, 0.0, gain)
    return x, ffn("L0_f1", h, wg, wu, wd)


def rest_of_step(g, conv, p, x, pend, target):
    scale = 0.5
    w = [split_layer({n: g[n, l] for n, _ in MATRICES if (n, l) in g}, {n: conv[n][l] for n, _ in CONVS}) for l in range(len(p))]
    for l in range(len(p)):
        n = f"L{l}"
        if l > 0:
            x, h = add_norm(n + "_n1", x, pend, scale, p[l]["ffn1_norm"])
            pend, scale = ffn(n + "_f1", h, w[l]["ffn1_w_gate"], w[l]["ffn1_w_up"], w[l]["ffn1_w_down"]), 0.5
        x, h = add_norm(n + "_nm", x, pend, scale, p[l]["mix_norm"])
        pend, scale = mixer(n + "_mx", h, w[l], p[l]), 1.0
        x, h = add_norm(n + "_n2", x, pend, scale, p[l]["ffn2_norm"])
        pend, scale = ffn(n + "_f2", h, w[l]["ffn2_w_gate"], w[l]["ffn2_w_up"], w[l]["ffn2_w_down"]), 0.5
    return loss_op(x, pend, target)


WEIGHT_NAMES = ("ffn1_norm", "ffn1_w_gate", "ffn1_w_up", "ffn1_w_down", "mix_norm", "w_in", "rg_conv_w", "rg_conv_b",
                "rg_w_r", "rg_b_r", "rg_w_i", "rg_b_i", "rg_lambda", "att_q_norm", "att_k_norm", "dn_conv_w", "dn_a_log",
                "dn_dt_bias", "dn_out_norm", "w_branch", "w_out", "ffn2_norm", "ffn2_w_gate", "ffn2_w_up", "ffn2_w_down")
MATRICES = (("ffn1_w_gate", 2), ("ffn1_w_up", 2), ("ffn1_w_down", 1), ("w_in", 2), ("w_branch", 1), ("w_out", 1),
            ("ffn2_w_gate", 2), ("ffn2_w_up", 2), ("ffn2_w_down", 1))
CONVS = (("rg_conv_w", 2), ("dn_conv_w", 2))
SHARD_AXIS = dict(MATRICES + CONVS)
SMALL_NAMES = tuple(n for n in WEIGHT_NAMES if n not in SHARD_AXIS)
ROW_PARAMS = ("ffn1_norm", "mix_norm", "rg_conv_b", "rg_b_r", "rg_b_i", "rg_lambda", "ffn2_norm")
FFN_MATS = ("ffn1_w_gate", "ffn1_w_up", "ffn1_w_down", "ffn2_w_gate", "ffn2_w_up", "ffn2_w_down")
TRANSPOSED_MATS = ("ffn1_w_gate", "ffn1_w_up", "ffn2_w_gate", "ffn2_w_up")
W_IN_SHARD = 2884
GATHER_ORDER = ((("ffn1_w_gate", 0), ("ffn1_w_up", 0), ("ffn1_w_down", 0)),
                (("w_in", 0), ("w_branch", 0), ("w_out", 0)),
                None)
GATHER_IDS = (1, 6, 7)
LATE_MATS = ("ffn2_w_gate", "ffn2_w_up", "ffn2_w_down", "w_out", "w_branch")
EXCHANGE_GROUPS = (lambda n, l: l == 1 and n in LATE_MATS,
                   lambda n, l: (l == 1) != (n in LATE_MATS),
                   lambda n, l: l == 0 and n == "w_in",
                   lambda n, l: l == 0 and n not in LATE_MATS and n != "w_in")


def _shard_minor(a, axis):
    a = jnp.moveaxis(a, 0, axis)
    return a.reshape(a.shape[:axis] + (N_CHIPS * a.shape[axis + 1],) + a.shape[axis + 2:])


def _w_in_piece(g, off, n):
    s = W_IN_SHARD
    parts = [g[j][:, max(off, j * s) - j * s:min(off + n, (j + 1) * s) - j * s]
             for j in range(N_CHIPS) if max(off, j * s) < min(off + n, (j + 1) * s)]
    return jnp.concatenate(parts, axis=1) if len(parts) > 1 else parts[0]


def _w_in_chip_grad(gl, j):
    s = W_IN_SHARD
    parts = [gl["in_" + k][:, max(off, j * s) - off:min(off + n, (j + 1) * s) - off]
             for k, off, n in W_IN_PIECES if max(off, j * s) < min(off + n, (j + 1) * s)]
    return jnp.concatenate(parts, axis=1)


def _layer_weights(g, conv):
    w = {n: g[n] for n in FFN_MATS if n in g}
    w["w_out"] = g["w_out"].reshape(D_MODEL, D_MODEL)
    for k, off, n in W_IN_PIECES:
        piece = _w_in_piece(g["w_in"], off, n)
        w["in_" + k] = jnp.pad(piece, ((0, 0), (0, LANES - n))) if n < LANES else piece
    wb = g["w_branch"].reshape(-1, D_MODEL)
    w["br_a"], w["br_b"], w["br_c"] = wb[:1024], wb[1024:1792], wb[1792:]
    return dict(w, **conv)


def _layer_weight_grads(gl):
    out = {n: gl[n] for n in FFN_MATS if n in gl}
    out["w_out"] = gl["w_out"].reshape(N_CHIPS, -1, D_MODEL)
    out["w_branch"] = jnp.concatenate([gl["br_a"], gl["br_b"], gl["br_c"]], axis=0).reshape(N_CHIPS, -1, D_MODEL)
    out["w_in"] = jnp.stack([_w_in_chip_grad(gl, j) for j in range(N_CHIPS)])
    return out, {n: gl[n] for n, _ in CONVS}


@jax.custom_vjp
def split_layer(g, conv):
    return _layer_weights(g, conv)


split_layer.defvjp(lambda g, conv: (_layer_weights(g, conv), None), lambda _, gw: _layer_weight_grads(gw))


def layer_small(small, l):
    p = {n: small[n][l] for n in SMALL_NAMES}
    for n in ROW_PARAMS:
        p[n] = small[n][l:l + 1]
    return p


def layer_small_grads(gp, small):
    return {n: jnp.stack([g[n] for g in gp]).reshape(small[n].shape) for n in SMALL_NAMES}


HBM_SPEC = pl.BlockSpec(memory_space=pl.ANY)


def _place():
    x, y, c = lax.axis_index("x"), lax.axis_index("y"), lax.axis_index("c")
    other_chips = [(1 - x, y), (x, 1 - y), (1 - x, 1 - y)]
    return x, y, c, 2 * x + y, (x, y, 1 - c), other_chips


def _half_rows(ref, lead, hc):
    hr = ref.shape[-2] // 2
    return ref.at[(*lead, pl.ds(pl.multiple_of(hc * hr, 16), hr), slice(None))]


def _chip_index():
    return (2 * lax.axis_index("x") + lax.axis_index("y")).astype(jnp.int32).reshape(1)


def cast_into_blocks(name, w):
    l, rows, cols = w.shape
    tr = rows // 2

    def body(me_ref, w_ref, *o_refs):
        for a, o_ref in enumerate(o_refs):
            o_ref[...] = w_ref[a:a + 1].astype(BF16)

    return pl.pallas_call(
        body, name=name, out_shape=[jax.ShapeDtypeStruct((N_CHIPS, rows, cols), BF16)] * l,
        grid_spec=pltpu.PrefetchScalarGridSpec(
            num_scalar_prefetch=1, grid=(rows // tr,),
            in_specs=[pl.BlockSpec((l, tr, cols), lambda i, me: (0, i, 0))],
            out_specs=[pl.BlockSpec((1, tr, cols), lambda i, me: (me[0], i, 0))] * l),
        compiler_params=_params("parallel"),
    )(_chip_index(), w)


def _gather_blocks(bufs_in, bufs_out, send_sems, recv_sems):
    n = len(bufs_in)
    x, y, c, me, sibling, chips = _place()

    def copy(s, src, dst, to):
        return pltpu.make_async_remote_copy(src_ref=src, dst_ref=dst, send_sem=send_sems.at[s], recv_sem=recv_sems.at[s],
                                            device_id=to, device_id_type=MESH)

    first, passed = [], []
    for j, (cx, cy) in enumerate(chips):
        for i in range(n):
            cp = copy(6 * i + j, _half_rows(bufs_in[i], (me,), c), _half_rows(bufs_out[i], (me,), c), (cx, cy, c))
            cp.start()
            first.append(cp)
    for j, (cx, cy) in enumerate(chips):
        k = 2 * cx + cy
        for i in range(n):
            copy(6 * i + j, _half_rows(bufs_in[i], (me,), c), _half_rows(bufs_out[i], (k,), c), (cx, cy, c)).wait_recv()
            cp = copy(6 * i + 3 + j, _half_rows(bufs_out[i], (k,), c), _half_rows(bufs_out[i], (k,), c), sibling)
            cp.start()
            passed.append(cp)
    for j, (cx, cy) in enumerate(chips):
        k = 2 * cx + cy
        for i in range(n):
            copy(6 * i + 3 + j, _half_rows(bufs_in[i], (me,), c), _half_rows(bufs_out[i], (k,), 1 - c), sibling).wait_recv()
    for cp in first + passed:
        cp.wait_send()


def _handshake(peers):
    barrier = pltpu.get_barrier_semaphore()
    for p in peers:
        pl.semaphore_signal(barrier, inc=1, device_id=p, device_id_type=MESH)
    pl.semaphore_wait(barrier, len(peers))


def allgather_blocks_sc(name, bufs, collective_id):
    n = len(bufs)
    refs = [jax.new_ref(b, memory_space=pltpu.MemorySpace.HBM) for b in bufs]

    @pl.kernel(mesh=plsc.ScalarSubcoreMesh(axis_name="sequencer", num_cores=1), name=name,
               scratch_types=(pltpu.SemaphoreType.DMA((6 * n,)), pltpu.SemaphoreType.DMA((6 * n,))),
               compiler_params=pltpu.CompilerParams(collective_id=collective_id))
    def launch(send_sems, recv_sems):
        x, y, c, me, sibling, chips = _place()
        _handshake([(cx, cy, c) for cx, cy in chips] + [sibling])
        _gather_blocks(refs, refs, send_sems, recv_sems)

    launch()
    return [jax.freeze(r) for r in refs]


PEER_FLIPS = tuple((fx, fy, fc) for fx in (0, 1) for fy in (0, 1) for fc in (0, 1))[1:]


def exchange_pieces_sc(name, gs, collective_id):
    n = len(gs)

    def body(*refs):
        ins, outs = refs[:n], refs[n:2 * n]
        send_sems, recv_sems = refs[2 * n:]
        x, y, c, me, sibling, chips = _place()
        my_dev = 4 * x + 2 * y + c
        flip = lambda v, f: 1 - v if f else v
        peers = [(flip(x, fx), flip(y, fy), flip(c, fc)) for fx, fy, fc in PEER_FLIPS]
        _handshake(peers)
        sends = []
        for r, (px, py, pc) in enumerate(peers):
            for i in range(n):
                cp = pltpu.make_async_remote_copy(
                    src_ref=_half_rows(ins[i], (2 * px + py,), pc), dst_ref=outs[i].at[my_dev], send_sem=send_sems.at[7 * i + r],
                    recv_sem=recv_sems.at[7 * i + r], device_id=(px, py, pc), device_id_type=MESH)
                cp.start()
                sends.append(cp)
        for r, (px, py, pc) in enumerate(peers):
            for i in range(n):
                pltpu.make_async_remote_copy(
                    src_ref=_half_rows(ins[i], (me,), c), dst_ref=outs[i].at[4 * px + 2 * py + pc], send_sem=send_sems.at[7 * i + r],
                    recv_sem=recv_sems.at[7 * i + r], device_id=(px, py, pc), device_id_type=MESH).wait_recv()
        for cp in sends:
            cp.wait_send()

    return pl.kernel(
        body, name=name, mesh=plsc.ScalarSubcoreMesh(axis_name="sequencer", num_cores=1),
        out_type=[jax.ShapeDtypeStruct((N_DEV, g.shape[1] // 2, g.shape[2]), g.dtype) for g in gs],
        scratch_types=[pltpu.SemaphoreType.DMA((7 * n,)), pltpu.SemaphoreType.DMA((7 * n,))],
        compiler_params=pltpu.CompilerParams(collective_id=collective_id),
    )(*gs)


def sibling_share_halves(name, fs):
    n = len(fs)
    every = (slice(None),)

    def body(*refs):
        ins, outs = refs[:n], refs[n:2 * n]
        send_sems, recv_sems = refs[2 * n:]
        x, y, c, me, sibling, chips = _place()
        sends = []
        for i in range(n):
            cp = pltpu.make_async_remote_copy(src_ref=_half_rows(ins[i], every, c), dst_ref=_half_rows(outs[i], every, c),
                                              send_sem=send_sems.at[i], recv_sem=recv_sems.at[i], device_id=sibling, device_id_type=MESH)
            cp.start()
            sends.append(cp)
        for i in range(n):
            pltpu.make_async_remote_copy(src_ref=_half_rows(ins[i], every, c), dst_ref=_half_rows(outs[i], every, 1 - c),
                                         send_sem=send_sems.at[i], recv_sem=recv_sems.at[i], device_id=sibling,
                                         device_id_type=MESH).wait_recv()
        for cp in sends:
            cp.wait_send()

    return pl.pallas_call(
        body, name=name, out_shape=[jax.ShapeDtypeStruct(f.shape, f.dtype) for f in fs],
        in_specs=[HBM_SPEC] * n, out_specs=[HBM_SPEC] * n, input_output_aliases={i: i for i in range(n)},
        scratch_shapes=[pltpu.SemaphoreType.DMA((n,)), pltpu.SemaphoreType.DMA((n,))],
    )(*fs)


def allgather_small_sc(name, v, collective_id):
    def body(v_ref, out_ref, send_sems, recv_sems, local_sem):
        x, y, c, me, sibling, chips = _place()
        my_dev = 4 * x + 2 * y + c
        flip = lambda a, f: 1 - a if f else a
        peers = [(flip(x, fx), flip(y, fy), flip(c, fc)) for fx, fy, fc in PEER_FLIPS]
        _handshake(peers)
        mine = pltpu.make_async_copy(v_ref, out_ref.at[my_dev], local_sem)
        mine.start()
        sends = []
        for r, peer in enumerate(peers):
            cp = pltpu.make_async_remote_copy(src_ref=v_ref, dst_ref=out_ref.at[my_dev], send_sem=send_sems.at[r],
                                              recv_sem=recv_sems.at[r], device_id=peer, device_id_type=MESH)
            cp.start()
            sends.append(cp)
        for r, (px, py, pc) in enumerate(peers):
            pltpu.make_async_remote_copy(src_ref=v_ref, dst_ref=out_ref.at[4 * px + 2 * py + pc], send_sem=send_sems.at[r],
                                         recv_sem=recv_sems.at[r], device_id=(px, py, pc), device_id_type=MESH).wait_recv()
        for cp in sends:
            cp.wait_send()
        mine.wait()

    return pl.kernel(
        body, name=name, mesh=plsc.ScalarSubcoreMesh(axis_name="sequencer", num_cores=1),
        out_type=jax.ShapeDtypeStruct((N_DEV,) + v.shape, v.dtype),
        scratch_types=[pltpu.SemaphoreType.DMA((7,)), pltpu.SemaphoreType.DMA((7,)), pltpu.SemaphoreType.DMA],
        compiler_params=pltpu.CompilerParams(collective_id=collective_id),
    )(v)


SUM_BLOCK_ELEMS = 512 * 1024


def sum_slabs(name, b):
    k, h, w = b.shape

    def body(b_ref, o_ref):
        acc = b_ref[0].astype(F32)
        for i in range(1, k):
            acc = acc + b_ref[i].astype(F32)
        o_ref[...] = acc

    return pl.pallas_call(
        body, name=name, out_shape=jax.ShapeDtypeStruct((h, w), F32),
        in_specs=[pl.BlockSpec(memory_space=pltpu.VMEM)], out_specs=pl.BlockSpec(memory_space=pltpu.VMEM),
        compiler_params=pltpu.CompilerParams(vmem_limit_bytes=VMEM_LIMIT),
    )(b)


def sum_pieces(name, pieces, gs):
    nl = len(pieces)
    k, h, w = pieces[0].shape
    tile = max(t for t in range(16, h + 1, 16) if h % t == 0 and (t * w <= SUM_BLOCK_ELEMS or t == 16))
    nt = h // tile
    x, y, c = lax.axis_index("x"), lax.axis_index("y"), lax.axis_index("c")
    place = [v.astype(jnp.int32).reshape(1) for v in (c, 2 * x + y, 4 * x + 2 * y + c)]

    assert nl == 2

    def tile_of(l, a, i):
        return i * a if l else i * (1 - a) + (nt - 1) * a

    def body(c_ref, me_ref, dev_ref, *refs):
        p_refs, g_refs, o_ref = refs[:nl], refs[nl:2 * nl], refs[2 * nl]
        my_dev = dev_ref[0]
        for l in range(nl):
            @pl.when(pl.program_id(0) == l)
            def _():
                o_ref[0] = jnp.zeros(o_ref.shape[1:], F32)
                for d in range(k):
                    @pl.when(my_dev == d)
                    def _():
                        o_ref[0] += g_refs[l][0].astype(F32)

                    @pl.when(my_dev != d)
                    def _():
                        o_ref[0] += p_refs[l][d].astype(F32)

    in_specs = [pl.BlockSpec((k, tile, w), functools.partial(lambda l, a, i, cc, me, dev: (0, tile_of(l, a, i), 0), l))
                for l in range(nl)]
    in_specs += [pl.BlockSpec((1, tile, w), functools.partial(lambda l, a, i, cc, me, dev: (me[0], cc[0] * nt + tile_of(l, a, i), 0), l))
                 for l in range(nl)]
    return pl.pallas_call(
        body, name=name, out_shape=jax.ShapeDtypeStruct((nl, 2 * h, w), F32),
        grid_spec=pltpu.PrefetchScalarGridSpec(
            num_scalar_prefetch=3, grid=(nl, nt), in_specs=in_specs,
            out_specs=pl.BlockSpec((1, tile, w), lambda a, i, cc, me, dev: (a, cc[0] * nt + i, 0))),
        compiler_params=_params("arbitrary", "arbitrary"),
    )(*place, *pieces, *gs)


def _adam_block(w, g, m, v):
    m = ADAM_B1 * m + (1.0 - ADAM_B1) * g
    v = ADAM_B2 * v + (1.0 - ADAM_B2) * (g * g)
    m_hat = m / (1.0 - ADAM_B1 ** ADAM_STEP)
    v_hat = v / (1.0 - ADAM_B2 ** ADAM_STEP)
    return -ADAM_LR * (m_hat / (jnp.sqrt(v_hat) + ADAM_EPS) + ADAM_WD * w), m, v


def adamw(name, w, g, m, v):
    shape = w.shape
    cols = shape[-1]
    rows = w.size // cols
    tile = 128 if rows % 128 == 0 else rows
    flat = [a.reshape(rows, cols) for a in (w, g, m, v)]

    def body(w_ref, g_ref, m_ref, v_ref, d_ref, nm_ref, nv_ref):
        d_ref[...], nm_ref[...], nv_ref[...] = _adam_block(w_ref[...], g_ref[...], m_ref[...], v_ref[...])

    blk = pl.BlockSpec((tile, cols), lambda i: (i, 0))
    res = pl.pallas_call(
        body, name=name, grid=(rows // tile,), in_specs=[blk] * 4, out_specs=[blk] * 3,
        out_shape=[jax.ShapeDtypeStruct((rows, cols), F32)] * 3, compiler_params=_params("parallel"),
    )(*flat)
    return tuple(r.reshape(shape) for r in res)


def _pack_small(values):
    flat = jnp.concatenate([v.reshape(-1) for v in values.values()])
    n = flat.shape[0]
    total = -(-n // (8 * LANES)) * (8 * LANES)
    return jnp.pad(flat, (0, total - n)).reshape(-1, LANES)


def _unpack_small(v, shapes):
    flat = v.reshape(-1)
    out, off = {}, 0
    for n, shape in shapes.items():
        sz = int(np.prod(shape))
        out[n] = flat[off:off + sz].reshape(shape)
        off += sz
    return out


def kernel(x, ffn1_norm, ffn1_w_gate, ffn1_w_up, ffn1_w_down, mix_norm, w_in, rg_conv_w, rg_conv_b, rg_w_r, rg_b_r, rg_w_i, rg_b_i, rg_lambda, att_q_norm, att_k_norm, dn_conv_w, dn_a_log, dn_dt_bias, dn_out_norm, w_branch, w_out, ffn2_norm, ffn2_w_gate, ffn2_w_up, ffn2_w_down, loss_target, m_ffn1_norm, m_ffn1_w_gate, m_ffn1_w_up, m_ffn1_w_down, m_mix_norm, m_w_in, m_rg_conv_w, m_rg_conv_b, m_rg_w_r, m_rg_b_r, m_rg_w_i, m_rg_b_i, m_rg_lambda, m_att_q_norm, m_att_k_norm, m_dn_conv_w, m_dn_a_log, m_dn_dt_bias, m_dn_out_norm, m_w_branch, m_w_out, m_ffn2_norm, m_ffn2_w_gate, m_ffn2_w_up, m_ffn2_w_down, v_ffn1_norm, v_ffn1_w_gate, v_ffn1_w_up, v_ffn1_w_down, v_mix_norm, v_w_in, v_rg_conv_w, v_rg_conv_b, v_rg_w_r, v_rg_b_r, v_rg_w_i, v_rg_b_i, v_rg_lambda, v_att_q_norm, v_att_k_norm, v_dn_conv_w, v_dn_a_log, v_dn_dt_bias, v_dn_out_norm, v_w_branch, v_w_out, v_ffn2_norm, v_ffn2_w_gate, v_ffn2_w_up, v_ffn2_w_down):
    given = dict(locals())
    for n in TRANSPOSED_MATS:
        for pre in ("", "m_", "v_"):
            given[pre + n] = jnp.swapaxes(given[pre + n], 1, 2)
    small = {n: given[n] for n in SMALL_NAMES}
    n_layers = ffn1_norm.shape[0]
    mat_names = [n for n, _ in MATRICES]
    conv_names = [n for n, _ in CONVS]

    blocks = {}
    for n in mat_names:
        for l, b in enumerate(cast_into_blocks("cast_" + n, given[n])):
            blocks[n, l] = b
    first, done = {}, []
    for i, wanted in enumerate(GATHER_ORDER[:-1]):
        bufs, _ = lax.optimization_barrier(([blocks[k] for k in wanted], done))
        done = allgather_blocks_sc(f"allgather_{i}", bufs, GATHER_IDS[i])
        first.update(zip(wanted, done))
    rest = {k: b for k, b in blocks.items() if k not in first}
    taps = jnp.concatenate([given[n].reshape(-1) for n in conv_names]).reshape(-1, LANES)
    taps = allgather_small_sc("allgather_taps", taps, 8).reshape(N_CHIPS, 2, -1)[:, 0]
    conv, off = {}, 0
    for n, ax in CONVS:
        sz = given[n].size
        conv[n] = _shard_minor(taps[:, off:off + sz].reshape((N_CHIPS,) + given[n].shape), ax)
        off += sz
    p = [layer_small(small, l) for l in range(n_layers)]

    ffn1_keys = GATHER_ORDER[0]
    (x1, pend), first_vjp = jax.vjp(first_ffn, *[first[k] for k in ffn1_keys], p[0]["ffn1_norm"], x[0])
    keys = list(rest)
    bufs, pend, second = lax.optimization_barrier(([rest[k] for k in keys], pend, [first[k] for k in GATHER_ORDER[1]]))
    gathered = dict(zip(keys, allgather_blocks_sc("allgather_2", bufs, GATHER_IDS[2])))
    gathered.update(zip(GATHER_ORDER[1], second))
    loss, (g_mats, g_conv, gp, gx1, gpend) = jax.value_and_grad(rest_of_step, argnums=(0, 1, 2, 3, 4))(
        gathered, conv, p, x1, pend, loss_target[0])
    *g_ffn1, gp[0]["ffn1_norm"], gx = first_vjp((gx1, gpend))
    g_mats.update(zip(ffn1_keys, g_ffn1))

    pieces = {}
    for i, group in enumerate(EXCHANGE_GROUPS):
        keys = [k for k in g_mats if group(*k)]
        pieces.update(zip(keys, exchange_pieces_sc(f"exchange_{i}", [g_mats[k] for k in keys], 2 + i)))
    halves = {n: sum_pieces("sum_" + n, [pieces[n, l] for l in range(n_layers)], [g_mats[n, l] for l in range(n_layers)])
              for n in mat_names}
    grads = {}
    for tag, names in (("late", [n for n in mat_names if n in LATE_MATS]), ("early", [n for n in mat_names if n not in LATE_MATS])):
        grads.update(zip(names, sibling_share_halves("share_" + tag, [halves[n] for n in names])))

    g_small = dict(layer_small_grads(gp, small), **g_conv, loss=loss.reshape(1))
    packed_small = _pack_small(g_small)
    slabs = allgather_small_sc("allgather_small", packed_small, 9)
    summed =_unpack_small(sum_slabs("sum_small", slabs), {n: g.shape for n, g in g_small.items()})
    chip = 2 * lax.axis_index("x") + lax.axis_index("y")
    for n in SMALL_NAMES:
        grads[n] = summed[n]
    for n, ax in CONVS:
        s = given[n].shape[ax]
        grads[n] = lax.dynamic_slice_in_dim(summed[n], chip * s, s, axis=ax)

    upd = {n: adamw("adamw_" + n, given[n], grads[n], given["m_" + n], given["v_" + n]) for n in WEIGHT_NAMES}
    out = lambda n, a: jnp.swapaxes(a, 1, 2) if n in TRANSPOSED_MATS else a
    return (summed["loss"][0], gx[None], *[out(n, grads[n]) for n in WEIGHT_NAMES], *[out(n, upd[n][0]) for n in WEIGHT_NAMES],
            *[out(n, upd[n][1]) for n in WEIGHT_NAMES], *[out(n, upd[n][2]) for n in WEIGHT_NAMES])
```

```python
import functools
import math

import jax
import jax.numpy as jnp
import numpy as np
from jax import lax
from jax.experimental import pallas as pl
from jax.experimental.pallas import tpu as pltpu
from jax.experimental.pallas import tpu_sc as plsc

F32 = jnp.float32
BF16 = jnp.bfloat16
MESH = pl.DeviceIdType.MESH

D_MODEL = 1024
FFN_DIM = 2816
RG_C = 8.0
ATT_GROUPS = ((128, 1), (512, 4), (2048, 16))
ATT_HEADS = 12
ATT_HEAD_DIM = 64
ATT_SPAN = 128
DN_HEADS = 8
DN_HEAD_DIM = 128
DN_CHUNK = 64
EPS = 1e-6
NEG_INF = -1e30
N_CHIPS = 4
N_DEV = 8

ADAM_LR, ADAM_B1, ADAM_B2, ADAM_EPS, ADAM_WD, ADAM_STEP = 0.001, 0.9, 0.999, 1e-08, 0.01, 10

LANES = 128
VMEM_LIMIT = 56 * 1024 * 1024


def _params(*sem):
    return pltpu.CompilerParams(dimension_semantics=sem or None, vmem_limit_bytes=VMEM_LIMIT)


def _sigmoid(x):
    return 1.0 / (1.0 + jnp.exp(-x))


def _silu(x):
    return x * _sigmoid(x)


def _softplus(x):
    return jnp.maximum(x, 0.0) + jnp.log(1.0 + jnp.exp(-jnp.abs(x)))


def _gelu(x):
    return 0.5 * x * (1.0 + jnp.tanh(math.sqrt(2.0 / math.pi) * (x + 0.044715 * (x * x * x))))


def _neg_expm1(x):
    series = -x * (1.0 + x * (0.5 + x * (1.0 / 6 + x * (1.0 / 24 + x * (1.0 / 120 + x * (1.0 / 720))))))
    return jnp.where(x > -0.25, series, 1.0 - jnp.exp(x))


def _rms(x, g):
    return x * lax.rsqrt(jnp.mean(x * x, axis=-1, keepdims=True) + EPS) * g


_MM_DIMS = {"nn": (((1,), (0,)), ((), ())), "nt": (((1,), (1,)), ((), ())), "tn": (((0,), (0,)), ((), ()))}


def _split(a):
    hi = a.astype(BF16)
    return hi, (a - hi.astype(F32)).astype(BF16)


def _mxu(a, b, form, passes):
    (ca, cb), _ = _MM_DIMS[form]
    if a.ndim == 3:
        dims = (((ca[0] + 1,), (cb[0] + 1,)), ((0,), (0,)))
    else:
        dims = _MM_DIMS[form]
    dg = lambda p, q: lax.dot_general(p, q, dims, preferred_element_type=F32)
    if passes == 1:
        return dg(a.astype(BF16), b.astype(BF16))
    (a_hi, a_lo), (b_hi, b_lo) = _split(a), _split(b)
    return dg(a_hi, b_hi) + (dg(a_hi, b_lo) + dg(a_lo, b_hi))


@functools.partial(jax.custom_vjp, nondiff_argnums=(2, 3))
def _mm(a, b, form, passes):
    return _mxu(a, b, form, passes)


def _mm_fwd(a, b, form, passes):
    return _mxu(a, b, form, passes), (a, b)


def _mm_bwd(form, passes, res, g):
    a, b = res
    if form == "nn":
        return _mm(g, b, "nt", passes), _mm(a, g, "tn", passes)
    if form == "nt":
        return _mm(g, b, "nn", passes), _mm(g, a, "tn", passes)
    return _mm(b, g, "nt", passes), _mm(a, g, "nn", passes)


_mm.defvjp(_mm_fwd, _mm_bwd)


def _dot(a, b):
    return _mm(a, b, "nn", 1)


def _dot_nt(a, b):
    return _mm(a, b, "nt", 1)


def _dot_tn(a, b):
    return _mm(a, b, "tn", 1)


def _dot3(a, b):
    return _mm(a, b, "nn", 3)


def _rows(shape):
    return lax.broadcasted_iota(jnp.int32, shape, len(shape) - 2)


def _roll_down(x, s, fill):
    return jnp.where(_rows(x.shape) >= s, pltpu.roll(x, s, x.ndim - 2), fill)


def _roll_up(x, s, fill):
    n = x.shape[-2]
    return jnp.where(_rows(x.shape) < n - s, pltpu.roll(x, n - s, x.ndim - 2), fill)


@functools.partial(jax.custom_vjp, nondiff_argnums=(1,))
def _shift(x, s):
    return _roll_down(x, s, 0.0)


def _shift_fwd(x, s):
    return _roll_down(x, s, 0.0), None


def _shift_bwd(s, _, g):
    return (_roll_up(g, s, 0.0),)


_shift.defvjp(_shift_fwd, _shift_bwd)


def _causal_conv(x, w):
    return w[0:1] * _shift(x, 3) + w[1:2] * _shift(x, 2) + w[2:3] * _shift(x, 1) + w[3:4] * x


@jax.custom_vjp
def _lin_scan(a, b):
    return _lin_scan_fwd(a, b)[0]


def _lin_scan_fwd(a, b):
    a0 = a
    s = 1
    while s < a.shape[0]:
        b = a * _roll_down(b, s, 0.0) + b
        a = a * _roll_down(a, s, 1.0)
        s *= 2
    return b, (a0, b)


def _lin_scan_bwd(res, g):
    a, h = res
    c = _roll_up(a, 1, 0.0)
    s = 1
    while s < a.shape[0]:
        g = c * _roll_up(g, s, 0.0) + g
        c = c * _roll_up(c, s, 1.0)
        s *= 2
    return g * _roll_down(h, 1, 0.0), g


_lin_scan.defvjp(_lin_scan_fwd, _lin_scan_bwd)


@jax.custom_vjp
def _cumsum_rows(x):
    s = 1
    while s < x.shape[-2]:
        x = x + _roll_down(x, s, 0.0)
        s *= 2
    return x


def _cumsum_rows_fwd(x):
    return _cumsum_rows(x), None


def _cumsum_rows_bwd(_, g):
    s = 1
    while s < g.shape[-2]:
        g = g + _roll_up(g, s, 0.0)
        s *= 2
    return (g,)


_cumsum_rows.defvjp(_cumsum_rows_fwd, _cumsum_rows_bwd)


ROW_BLOCK_BYTES = 14 * 1024 * 1024


def _row_tile(t, width=0):
    for tile in (512, 256):
        if t % tile == 0 and (tile == 256 or tile * width * 4 <= ROW_BLOCK_BYTES):
            return tile
    return t


def _rowwise_fwd_call(name, f, rows, pars, tile):
    t = rows[0].shape[0]
    outs = jax.eval_shape(f, *[jax.ShapeDtypeStruct((tile, r.shape[1]), F32) for r in rows],
                          *[jax.ShapeDtypeStruct(p.shape, F32) for p in pars])
    nr, npar = len(rows), len(pars)

    def body(*refs):
        ins = [r[...] for r in refs[:nr + npar]]
        res = f(*ins)
        for o_ref, o in zip(refs[nr + npar:], res):
            o_ref[...] = o.astype(o_ref.dtype)

    return pl.pallas_call(
        body, name=name, grid=(t // tile,),
        in_specs=[pl.BlockSpec((tile, r.shape[1]), lambda i: (i, 0)) for r in rows]
        + [pl.BlockSpec(p.shape, lambda i: (0, 0)) for p in pars],
        out_specs=[pl.BlockSpec((tile, o.shape[1]), lambda i: (i, 0)) for o in outs],
        out_shape=[jax.ShapeDtypeStruct((t, o.shape[1]), F32) for o in outs],
        compiler_params=_params("parallel"),
    )(*rows, *pars)


def _rowwise_bwd_call(name, f, rows, pars, cts, tile):
    t = rows[0].shape[0]
    nr, npar, nct = len(rows), len(pars), len(cts)

    def body(*refs):
        ins = [r[...] for r in refs[:nr + npar]]
        gs = tuple(r[...] for r in refs[nr + npar:nr + npar + nct])
        outs = refs[nr + npar + nct:]
        _, vjp = jax.vjp(f, *ins)
        d = vjp(gs)
        for o_ref, v in zip(outs[:nr], d[:nr]):
            o_ref[...] = v

        @pl.when(pl.program_id(0) == 0)
        def _():
            for o_ref in outs[nr:]:
                o_ref[...] = jnp.zeros_like(o_ref)

        for o_ref, v in zip(outs[nr:], d[nr:]):
            o_ref[...] += v

    res = pl.pallas_call(
        body, name=name, grid=(t // tile,),
        in_specs=[pl.BlockSpec((tile, r.shape[1]), lambda i: (i, 0)) for r in rows]
        + [pl.BlockSpec(p.shape, lambda i: (0, 0)) for p in pars]
        + [pl.BlockSpec((tile, c.shape[1]), lambda i: (i, 0)) for c in cts],
        out_specs=[pl.BlockSpec((tile, r.shape[1]), lambda i: (i, 0)) for r in rows]
        + [pl.BlockSpec(p.shape, lambda i: (0, 0)) for p in pars],
        out_shape=[jax.ShapeDtypeStruct(r.shape, F32) for r in rows]
        + [jax.ShapeDtypeStruct(p.shape, F32) for p in pars],
        compiler_params=_params("arbitrary"),
    )(*rows, *pars, *cts)
    return tuple(res[:nr]), tuple(res[nr:])


def rowwise(name, f, rows, pars=()):
    outs = jax.eval_shape(f, *[jax.ShapeDtypeStruct((8, r.shape[1]), F32) for r in rows],
                          *[jax.ShapeDtypeStruct(p.shape, F32) for p in pars])
    tile = _row_tile(rows[0].shape[0], 2 * sum(r.shape[1] for r in rows) + sum(o.shape[1] for o in outs))

    @jax.custom_vjp
    def op(rows, pars):
        return tuple(_rowwise_fwd_call(name, f, rows, pars, tile))

    def op_fwd(rows, pars):
        return op(rows, pars), (rows, pars)

    def op_bwd(res, cts):
        return _rowwise_bwd_call(name + "_bwd", f, res[0], res[1], tuple(cts), tile)

    op.defvjp(op_fwd, op_bwd)
    return op(tuple(rows), tuple(pars))


MM_TM = 512


def _tile_of(n, cap):
    best = None
    for c in range(LANES, min(n, cap) + 1, LANES):
        if n % c == 0:
            best = c
    return best or n


def _proj_dw(name, h, dys):
    m, k = h.shape
    n, tm = len(dys), 256
    steps = m // tm

    def body(h_ref, *refs):
        dy_refs, o_refs, accs = refs[:n], refs[n:2 * n], refs[2 * n:]
        ht = jnp.transpose(h_ref[...]).astype(BF16)
        first = pl.program_id(0) == 0
        for dy_ref, acc in zip(dy_refs, accs):
            for c0 in range(0, acc.shape[1], 1024):
                cols = slice(c0, min(c0 + 1024, acc.shape[1]))
                part = _dot(ht, dy_ref[:, cols])

                @pl.when(first)
                def _():
                    acc[:, cols] = part

                @pl.when(jnp.logical_not(first))
                def _():
                    acc[:, cols] += part

        @pl.when(pl.program_id(0) == steps - 1)
        def _():
            for o_ref, acc in zip(o_refs, accs):
                o_ref[...] = acc[...].astype(BF16)

    row = lambda width: pl.BlockSpec((tm, width), lambda i: (i, 0))
    return pl.pallas_call(
        body, name=name, grid=(steps,),
        in_specs=[row(k)] + [row(d.shape[1]) for d in dys],
        out_specs=[pl.BlockSpec((k, d.shape[1]), lambda i: (0, 0)) for d in dys],
        out_shape=[jax.ShapeDtypeStruct((k, d.shape[1]), BF16) for d in dys],
        scratch_shapes=[pltpu.VMEM((k, d.shape[1]), F32) for d in dys],
        compiler_params=_params("arbitrary"),
    )(h, *dys)


PROJ_GROUP_COLS = 4608


def _proj_dh(name, dys, ws, acc):
    m, k = dys[0].shape[0], ws[0].shape[0]
    n, tm = len(dys), 256

    def body(*refs):
        dy_refs, w_refs, rest = refs[:n], refs[n:2 * n], refs[2 * n:]
        total = _dot_nt(dy_refs[0][...], w_refs[0][...])
        for dy_ref, w_ref in zip(dy_refs[1:], w_refs[1:]):
            total = total + _dot_nt(dy_ref[...], w_ref[...])
        if acc is not None:
            total = total + rest[0][...]
        rest[-1][...] = total

    row = lambda width: pl.BlockSpec((tm, width), lambda i: (i, 0))
    return pl.pallas_call(
        body, name=name, grid=(m // tm,),
        in_specs=[row(d.shape[1]) for d in dys] + [pl.BlockSpec(w.shape, lambda i: (0, 0)) for w in ws] + ([row(k)] if acc is not None else []),
        out_specs=row(k), out_shape=jax.ShapeDtypeStruct((m, k), F32), compiler_params=_params("parallel"),
    )(*dys, *ws, *([acc] if acc is not None else []))


def _proj_fwd(name, h, ws):
    m, k = h.shape
    n, tm = len(ws), 256

    def body(h_ref, *refs):
        hv = h_ref[...].astype(BF16)
        for w_ref, o_ref in zip(refs[:n], refs[n:]):
            o_ref[...] = _dot(hv, w_ref[...])

    row = lambda width: pl.BlockSpec((tm, width), lambda i: (i, 0))
    return pl.pallas_call(
        body, name=name, grid=(m // tm,),
        in_specs=[row(k)] + [pl.BlockSpec(w.shape, lambda i: (0, 0)) for w in ws],
        out_specs=[row(w.shape[1]) for w in ws],
        out_shape=[jax.ShapeDtypeStruct((m, w.shape[1]), F32) for w in ws], compiler_params=_params("parallel"),
    )(h, *ws)


def project_in(name, h, ws):
    keys = list(ws)
    groups, cols = [[]], 0
    for p in keys:
        if groups[-1] and cols + ws[p].shape[1] > PROJ_GROUP_COLS:
            groups.append([])
            cols = 0
        groups[-1].append(p)
        cols += ws[p].shape[1]

    @jax.custom_vjp
    def op(h, ws):
        out = {}
        for i, group in enumerate(groups):
            out.update(zip(group, _proj_fwd(f"{name}_{i}", h, [ws[p] for p in group])))
        return out

    def op_fwd(h, ws):
        return op(h, ws), (h, ws)

    def op_bwd(res, dys):
        h, ws = res
        dh, dws = None, {}
        for i, group in enumerate(groups):
            dh = _proj_dh(f"{name}_dh{i}", [dys[p] for p in group], [ws[p] for p in group], dh)
            dws.update(zip(group, _proj_dw(f"{name}_dw{i}", h, [dys[p] for p in group])))
        return dh, dws

    op.defvjp(op_fwd, op_bwd)
    return op(h, ws)


def _ffn_up(name, h, wt):
    m, k = h.shape
    j, n, _ = wt.shape
    tm = MM_TM

    def body(h_ref, w_ref, o_ref):
        o_ref[0] = _dot_nt(h_ref[...], w_ref[0])

    return pl.pallas_call(
        body, name=name, grid=(m // tm, j),
        in_specs=[pl.BlockSpec((tm, k), lambda i, b: (i, 0)), pl.BlockSpec((1, n, k), lambda i, b: (b, 0, 0))],
        out_specs=pl.BlockSpec((1, tm, n), lambda i, b: (b, i, 0)),
        out_shape=jax.ShapeDtypeStruct((j, m, n), F32), compiler_params=_params("parallel", "parallel"),
    )(h, wt)


def _ffn_down(name, g, u, wd):
    j, m, n = g.shape
    d = wd.shape[2]
    tm = MM_TM

    def body(g_ref, u_ref, w_ref, o_ref):
        part = _dot(_silu(g_ref[0]) * u_ref[0], w_ref[0])

        @pl.when(pl.program_id(1) == 0)
        def _():
            o_ref[...] = part

        @pl.when(pl.program_id(1) > 0)
        def _():
            o_ref[...] += part

    act = pl.BlockSpec((1, tm, n), lambda i, b: (b, i, 0))
    return pl.pallas_call(
        body, name=name, grid=(m // tm, j),
        in_specs=[act, act, pl.BlockSpec((1, n, d), lambda i, b: (b, 0, 0))],
        out_specs=pl.BlockSpec((tm, d), lambda i, b: (i, 0)),
        out_shape=jax.ShapeDtypeStruct((m, d), F32), compiler_params=_params("parallel", "arbitrary"),
    )(g, u, wd)


def _ffn_down_bwd(name, dy, g, u, wd):
    j, m, n = g.shape
    d = wd.shape[2]
    tm = MM_TM

    def body(dy_ref, g_ref, u_ref, w_ref, dg_ref, du_ref):
        da = _dot_nt(dy_ref[...], w_ref[0])
        gv = g_ref[0]
        s = _sigmoid(gv)
        dg_ref[0] = da * u_ref[0] * (s * (1.0 + gv * (1.0 - s)))
        du_ref[0] = da * (gv * s)

    act = pl.BlockSpec((1, tm, n), lambda i, b: (b, i, 0))
    return pl.pallas_call(
        body, name=name, grid=(m // tm, j),
        in_specs=[pl.BlockSpec((tm, d), lambda i, b: (i, 0)), act, act, pl.BlockSpec((1, n, d), lambda i, b: (b, 0, 0))],
        out_specs=[act, act], out_shape=[jax.ShapeDtypeStruct((j, m, n), F32)] * 2,
        compiler_params=_params("parallel", "parallel"),
    )(dy, g, u, wd)


def _ffn_down_dw(name, g, u, dy):
    j, m, n = g.shape
    d = dy.shape[1]
    tn = _tile_of(d, 512)

    def body(g_ref, u_ref, dy_ref, o_ref):
        o_ref[0] = _dot_tn(_silu(g_ref[0]) * u_ref[0], dy_ref[...]).astype(BF16)

    act = pl.BlockSpec((1, m, n), lambda b, c: (b, 0, 0))
    return pl.pallas_call(
        body, name=name, grid=(j, d // tn),
        in_specs=[act, act, pl.BlockSpec((m, tn), lambda b, c: (0, c))],
        out_specs=pl.BlockSpec((1, n, tn), lambda b, c: (b, 0, c)),
        out_shape=jax.ShapeDtypeStruct((j, n, d), BF16), compiler_params=_params("parallel", "parallel"),
    )(g, u, dy)


def _ffn_up_dh(name, dg, du, wg, wu):
    j, m, n = dg.shape
    k = wg.shape[2]
    tm = MM_TM

    def body(dg_ref, du_ref, wg_ref, wu_ref, o_ref):
        part = _dot(dg_ref[0], wg_ref[0]) + _dot(du_ref[0], wu_ref[0])

        @pl.when(pl.program_id(1) == 0)
        def _():
            o_ref[...] = part

        @pl.when(pl.program_id(1) > 0)
        def _():
            o_ref[...] += part

    act = pl.BlockSpec((1, tm, n), lambda i, b: (b, i, 0))
    wsp = pl.BlockSpec((1, n, k), lambda i, b: (b, 0, 0))
    return pl.pallas_call(
        body, name=name, grid=(m // tm, j), in_specs=[act, act, wsp, wsp],
        out_specs=pl.BlockSpec((tm, k), lambda i, b: (i, 0)),
        out_shape=jax.ShapeDtypeStruct((m, k), F32), compiler_params=_params("parallel", "arbitrary"),
    )(dg, du, wg, wu)


def _ffn_up_dw(name, dy, h):
    j, m, n = dy.shape
    k = h.shape[1]
    tk = _tile_of(k, 512)

    def body(dy_ref, h_ref, o_ref):
        o_ref[0] = _dot_tn(dy_ref[0], h_ref[...]).astype(BF16)

    return pl.pallas_call(
        body, name=name, grid=(j, k // tk),
        in_specs=[pl.BlockSpec((1, m, n), lambda b, i: (b, 0, 0)), pl.BlockSpec((m, tk), lambda b, i: (0, i))],
        out_specs=pl.BlockSpec((1, n, tk), lambda b, i: (b, 0, i)),
        out_shape=jax.ShapeDtypeStruct((j, n, k), BF16), compiler_params=_params("parallel", "parallel"),
    )(dy, h)


def ffn(name, h, wg, wu, wd):
    @jax.custom_vjp
    def op(h, wg, wu, wd):
        return _ffn_down(name + "_d", _ffn_up(name + "_g", h, wg), _ffn_up(name + "_u", h, wu), wd)

    def op_fwd(h, wg, wu, wd):
        g, u = _ffn_up(name + "_g", h, wg), _ffn_up(name + "_u", h, wu)
        return _ffn_down(name + "_d", g, u, wd), (h, g, u, wg, wu, wd)

    def op_bwd(res, dy):
        h, g, u, wg, wu, wd = res
        dg, du = _ffn_down_bwd(name + "_d_bwd", dy, g, u, wd)
        return (_ffn_up_dh(name + "_dh", dg, du, wg, wu), _ffn_up_dw(name + "_g_dw", dg, h), _ffn_up_dw(name + "_u_dw", du, h),
                _ffn_down_dw(name + "_d_dw", g, u, dy))

    op.defvjp(op_fwd, op_bwd)
    return op(h, wg, wu, wd)


def _mmr_fwd(name, a, w):
    j, m, n = a.shape
    nn = w.shape[2]
    tm, tn = MM_TM, _tile_of(nn, 1024)

    def body(a_ref, w_ref, o_ref):
        part = _dot(a_ref[0], w_ref[0])

        @pl.when(pl.program_id(2) == 0)
        def _():
            o_ref[...] = part

        @pl.when(pl.program_id(2) > 0)
        def _():
            o_ref[...] += part

    return pl.pallas_call(
        body, name=name, grid=(m // tm, nn // tn, j),
        in_specs=[pl.BlockSpec((1, tm, n), lambda i, c, b: (b, i, 0)), pl.BlockSpec((1, n, tn), lambda i, c, b: (b, 0, c))],
        out_specs=pl.BlockSpec((tm, tn), lambda i, c, b: (i, c)),
        out_shape=jax.ShapeDtypeStruct((m, nn), F32),
        compiler_params=_params("parallel", "parallel", "arbitrary"),
    )(a, w)


def _mmr_da(name, dy, w):
    m, nn = dy.shape
    j, n, _ = w.shape
    tm = MM_TM

    def body(dy_ref, w_ref, o_ref):
        o_ref[0] = _dot_nt(dy_ref[...], w_ref[0])

    return pl.pallas_call(
        body, name=name, grid=(m // tm, j),
        in_specs=[pl.BlockSpec((tm, nn), lambda i, b: (i, 0)), pl.BlockSpec((1, n, nn), lambda i, b: (b, 0, 0))],
        out_specs=pl.BlockSpec((1, tm, n), lambda i, b: (b, i, 0)),
        out_shape=jax.ShapeDtypeStruct((j, m, n), F32),
        compiler_params=_params("parallel", "parallel"),
    )(dy, w)


def _mmr_dw(name, a, dy):
    j, m, n = a.shape
    nn = dy.shape[1]
    tn = _tile_of(nn, 512)

    def body(a_ref, dy_ref, o_ref):
        o_ref[0] = _dot_tn(a_ref[0], dy_ref[...]).astype(BF16)

    return pl.pallas_call(
        body, name=name, grid=(j, nn // tn),
        in_specs=[pl.BlockSpec((1, m, n), lambda b, c: (b, 0, 0)), pl.BlockSpec((m, tn), lambda b, c: (0, c))],
        out_specs=pl.BlockSpec((1, n, tn), lambda b, c: (b, 0, c)),
        out_shape=jax.ShapeDtypeStruct((j, n, nn), BF16),
        compiler_params=_params("parallel", "parallel"),
    )(a, dy)


def mm_rows(name, a, w):
    @jax.custom_vjp
    def op(a, w):
        return _mmr_fwd(name, a, w)

    def op_fwd(a, w):
        return op(a, w), (a, w)

    def op_bwd(res, dy):
        a, w = res
        return _mmr_da(name + "_da", dy, w), _mmr_dw(name + "_dw", a, dy)

    op.defvjp(op_fwd, op_bwd)
    return op(a, w)


def _colwise_specs(cols, pars, par_block):
    t = cols[0].shape[0]
    specs = [pl.BlockSpec((t, LANES), lambda j: (0, j)) for _ in cols]
    for p, blk in zip(pars, par_block):
        if blk == "lane":
            specs.append(pl.BlockSpec((p.shape[0], LANES), lambda j: (0, j)))
        else:
            specs.append(pl.BlockSpec((1,) + p.shape[1:], lambda j: (j, 0, 0)))
    return specs


def _colwise_fwd_call(name, f, cols, pars, par_block, n_out):
    t, c = cols[0].shape
    nc, npar = len(cols), len(pars)

    def body(*refs):
        ins = [r[...] for r in refs[:nc]] + [r[...] if b == "lane" else r[0] for r, b in zip(refs[nc:nc + npar], par_block)]
        res = f(*ins)
        for o_ref, o in zip(refs[nc + npar:], res):
            o_ref[...] = o

    return pl.pallas_call(
        body, name=name, grid=(c // LANES,),
        in_specs=_colwise_specs(cols, pars, par_block),
        out_specs=[pl.BlockSpec((t, LANES), lambda j: (0, j)) for _ in range(n_out)],
        out_shape=[jax.ShapeDtypeStruct((t, c), F32) for _ in range(n_out)],
        compiler_params=_params("parallel"),
    )(*cols, *pars)


def _colwise_bwd_call(name, f, cols, pars, par_block, cts):
    t, c = cols[0].shape
    nc, npar, nct = len(cols), len(pars), len(cts)

    def body(*refs):
        ins = [r[...] for r in refs[:nc]] + [r[...] if b == "lane" else r[0] for r, b in zip(refs[nc:nc + npar], par_block)]
        gs = tuple(r[...] for r in refs[nc + npar:nc + npar + nct])
        outs = refs[nc + npar + nct:]
        _, vjp = jax.vjp(f, *ins)
        d = vjp(gs)
        for o_ref, v in zip(outs[:nc], d[:nc]):
            o_ref[...] = v
        for o_ref, v, b in zip(outs[nc:], d[nc:], par_block):
            if b == "lane":
                o_ref[...] = v
            else:
                o_ref[0] = v

    res = pl.pallas_call(
        body, name=name, grid=(c // LANES,),
        in_specs=_colwise_specs(cols, pars, par_block) + [pl.BlockSpec((t, LANES), lambda j: (0, j)) for _ in cts],
        out_specs=_colwise_specs(cols, pars, par_block),
        out_shape=[jax.ShapeDtypeStruct(v.shape, F32) for v in (*cols, *pars)],
        compiler_params=_params("parallel"),
    )(*cols, *pars, *cts)
    return tuple(res[:nc]), tuple(res[nc:])


def colwise(name, f, cols, pars, par_block, n_out):
    @jax.custom_vjp
    def op(cols, pars):
        return tuple(_colwise_fwd_call(name, f, cols, pars, par_block, n_out))

    def op_fwd(cols, pars):
        return op(cols, pars), (cols, pars)

    def op_bwd(res, cts):
        return _colwise_bwd_call(name + "_bwd", f, res[0], res[1], par_block, tuple(cts))

    op.defvjp(op_fwd, op_bwd)
    return op(tuple(cols), tuple(pars))


def _rg_block(x, gate, cw, cb, wr, br, wi, bi, lam):
    xa = _causal_conv(x, cw) + cb
    r = _sigmoid(_dot(xa, wr) + br)
    i = _sigmoid(_dot(xa, wi) + bi)
    log_a = -RG_C * r * _softplus(-lam)
    a = jnp.exp(log_a)
    b = jnp.sqrt(_neg_expm1(2.0 * log_a)) * (i * xa)
    return (_lin_scan(a, b) * _gelu(gate),)


def _dn_conv_block(mode):
    def f(x, cw):
        c = _silu(_causal_conv(x, cw))
        if mode == "v":
            return (c,)
        c = c * lax.rsqrt(jnp.sum(c * c, axis=-1, keepdims=True) + EPS)
        return (c * (DN_HEAD_DIM ** -0.5),) if mode == "q" else (c,)
    return f


def _block_diag(w):
    w = w.reshape(8, 2, 64, 64)
    z = jnp.zeros((8, 64, 64), w.dtype)
    top = jnp.concatenate([w[:, 0], z], axis=2)
    bot = jnp.concatenate([z, w[:, 1]], axis=2)
    return jnp.concatenate([top, bot], axis=1)


DN_HP = 8


def _dn_block(S, qw, kw, vw, gb, h0, tinv=None):
    hp, hd = S.shape[0], DN_HEAD_DIM
    heads = lambda a: jnp.concatenate([a[None, :, j * hd:(j + 1) * hd] for j in range(hp)], axis=0)
    lane = lax.broadcasted_iota(jnp.int32, gb.shape, 1)
    col = lambda i: jnp.sum(jnp.where(lane == i, gb, 0.0), axis=1, keepdims=True)[None]
    beta = jnp.concatenate([col(h0 + j) for j in range(hp)], axis=0)
    g = jnp.concatenate([col(h0 + j + DN_HEADS) for j in range(hp)], axis=0)
    s_new, o, tinv = _dn_step(S, heads(qw), heads(kw), heads(vw), beta, g, tinv)
    return s_new, jnp.concatenate([o[j:j + 1].reshape(o.shape[1:]) for j in range(hp)], axis=1), tinv


@jax.custom_vjp
def _unit_lower_inverse(a):
    c = a.shape[-1]
    eye = (lax.broadcasted_iota(jnp.int32, (c, c), 0) == lax.broadcasted_iota(jnp.int32, (c, c), 1)).astype(F32)
    p = -a
    tinv = eye + p
    for _ in range(5):
        p = _dot3(p, p)
        tinv = tinv + _dot3(tinv, p)
    return tinv


def _unit_lower_inverse_fwd(a):
    t = _unit_lower_inverse(a)
    return t, t


def _unit_lower_inverse_bwd(t, g):
    return (-_mm(_mm(t, g, "tn", 3), t, "nt", 3),)


_unit_lower_inverse.defvjp(_unit_lower_inverse_fwd, _unit_lower_inverse_bwd)


@jax.custom_vjp
def _known_inverse(a, t):
    return t


_known_inverse.defvjp(lambda a, t: (t, t), lambda t, g: (_unit_lower_inverse_bwd(t, g)[0], jnp.zeros_like(t)))


def _dn_step(S, q, k, v, beta, g, tinv=None):
    c = DN_CHUNK
    ri = lax.broadcasted_iota(jnp.int32, (c, c), 0)
    ci = lax.broadcasted_iota(jnp.int32, (c, c), 1)
    incl, strict = ri >= ci, ri > ci
    gam = _cumsum_rows(g)
    gam_row = jnp.sum(jnp.where(ri <= ci, g, 0.0), axis=-2, keepdims=True)
    gam_last = jnp.sum(g, axis=-2, keepdims=True)
    decay = jnp.where(incl, jnp.exp(jnp.where(incl, gam - gam_row, 0.0)), 0.0)
    kb = k * beta
    vb = v * beta
    a = jnp.where(strict, _dot_nt(kb, k) * decay, 0.0)
    tinv = _unit_lower_inverse(a) if tinv is None else _known_inverse(a, tinv)
    e_gam = jnp.exp(gam)
    u0 = _dot3(tinv, vb)
    wk = _dot3(tinv, kb * e_gam)
    qk = jnp.where(incl, _dot_nt(q, k) * decay, 0.0)
    q_dec = q * e_gam
    k_dec = k * jnp.exp(gam_last - gam)
    u = u0 - _dot(wk, S)
    o = _dot(q_dec, S) + _dot(qk, u)
    s_new = S * jnp.exp(gam_last) + _dot_tn(k_dec, u)
    return s_new, o, tinv


def _dn_fwd_call(q, k, v, gb):
    t, w = q.shape
    n, hp, hd, c = t // DN_CHUNK, DN_HP, DN_HEAD_DIM, DN_CHUNK

    def body(q_ref, k_ref, v_ref, gb_ref, o_ref, s0_ref, ti_ref, s_scr):
        @pl.when(pl.program_id(1) == 0)
        def _():
            s_scr[...] = jnp.zeros_like(s_scr)

        s_old = s_scr[...]
        s0_ref[:, 0] = s_old
        s_new, o, tinv = _dn_block(s_old, q_ref[...], k_ref[...], v_ref[...], gb_ref[...], pl.program_id(0) * hp)
        o_ref[...] = o
        ti_ref[:, 0] = tinv
        s_scr[...] = s_new

    blk = pl.BlockSpec((c, hp * hd), lambda g, i: (i, g))
    return pl.pallas_call(
        body, name="dn_core", grid=(DN_HEADS // hp, n),
        in_specs=[blk, blk, blk, pl.BlockSpec((c, LANES), lambda g, i: (i, 0))],
        out_specs=[blk, pl.BlockSpec((hp, 1, hd, hd), lambda g, i: (g, i, 0, 0)), pl.BlockSpec((hp, 1, c, c), lambda g, i: (g, i, 0, 0))],
        out_shape=[jax.ShapeDtypeStruct((t, w), F32), jax.ShapeDtypeStruct((DN_HEADS, n, hd, hd), F32),
                   jax.ShapeDtypeStruct((DN_HEADS, n, c, c), F32)],
        scratch_shapes=[pltpu.VMEM((hp, hd, hd), F32)],
        compiler_params=_params("parallel", "arbitrary"),
    )(q, k, v, gb)


def _dn_bwd_call(q, k, v, gb, s0, ti, do):
    t, w = q.shape
    n, hp, hd, c = t // DN_CHUNK, DN_HP, DN_HEAD_DIM, DN_CHUNK
    ng = DN_HEADS // hp

    def body(q_ref, k_ref, v_ref, gb_ref, s0_ref, ti_ref, do_ref, dq_ref, dk_ref, dv_ref, dgb_ref, ds_scr):
        @pl.when(pl.program_id(1) == 0)
        def _():
            ds_scr[...] = jnp.zeros_like(ds_scr)

        h0, tinv = pl.program_id(0) * hp, ti_ref[:, 0]
        _, vjp = jax.vjp(lambda *a: _dn_block(*a, h0, tinv)[:2], s0_ref[:, 0], q_ref[...], k_ref[...], v_ref[...], gb_ref[...])
        ds, dq, dk, dv, dgb = vjp((ds_scr[...], do_ref[...]))
        ds_scr[...] = ds
        dq_ref[...], dk_ref[...], dv_ref[...] = dq, dk, dv
        dgb_ref[0] = dgb

    blk = pl.BlockSpec((c, hp * hd), lambda g, i: (n - 1 - i, g))
    res = pl.pallas_call(
        body, name="dn_core_bwd", grid=(ng, n),
        in_specs=[blk, blk, blk, pl.BlockSpec((c, LANES), lambda g, i: (n - 1 - i, 0)),
                  pl.BlockSpec((hp, 1, hd, hd), lambda g, i: (g, n - 1 - i, 0, 0)),
                  pl.BlockSpec((hp, 1, c, c), lambda g, i: (g, n - 1 - i, 0, 0)), blk],
        out_specs=[blk, blk, blk, pl.BlockSpec((1, c, LANES), lambda g, i: (g, n - 1 - i, 0))],
        out_shape=[jax.ShapeDtypeStruct((t, w), F32)] * 3 + [jax.ShapeDtypeStruct((ng, t, LANES), F32)],
        scratch_shapes=[pltpu.VMEM((hp, hd, hd), F32)],
        compiler_params=_params("parallel", "arbitrary"),
    )(q, k, v, gb, s0, ti, do)
    return res[0], res[1], res[2], jnp.sum(res[3], axis=0)


@jax.custom_vjp
def dn_core(q, k, v, gb):
    return _dn_fwd_call(q, k, v, gb)[0]


def _dn_core_fwd(q, k, v, gb):
    o, s0, ti = _dn_fwd_call(q, k, v, gb)
    return o, (q, k, v, gb, s0, ti)


def _dn_core_bwd(res, do):
    return _dn_bwd_call(*res, do)


dn_core.defvjp(_dn_core_fwd, _dn_core_bwd)


ATT_GH = 4


def _att_block(q, kp, kc, vp, vc, qn, kn, slope, has_prev, dil):
    s = ATT_SPAN
    qh = _rms(q, qn) * (ATT_HEAD_DIM ** -0.5)
    qi = lax.broadcasted_iota(jnp.int32, (s, s), 0)
    kj = lax.broadcasted_iota(jnp.int32, (s, s), 1)
    d_p = qi + s - kj
    d_c = qi - kj
    s_p = _dot_nt(qh, _rms(kp, kn)) - slope * (d_p * dil).astype(F32)
    s_c = _dot_nt(qh, _rms(kc, kn)) - slope * (d_c * dil).astype(F32)
    s_p = jnp.where((d_p <= s) & (has_prev > 0), s_p, NEG_INF)
    s_c = jnp.where(d_c >= 0, s_c, NEG_INF)
    m = lax.stop_gradient(jnp.maximum(jnp.max(s_p, axis=-1, keepdims=True), jnp.max(s_c, axis=-1, keepdims=True)))
    p_p = jnp.exp(s_p - m)
    p_c = jnp.exp(s_c - m)
    den = jnp.sum(p_p, axis=-1, keepdims=True) + jnp.sum(p_c, axis=-1, keepdims=True)
    o = _dot(p_p / den, vp) + _dot(p_c / den, vc)
    lse = m + jnp.log(den)
    return o, jnp.broadcast_to(lse, o.shape)


def _att_heads(a):
    e = ATT_HEAD_DIM
    return jnp.concatenate([a[None, :, h * e:(h + 1) * e] for h in range(ATT_GH)], axis=0)


def _att_lanes(a):
    return jnp.concatenate([a[h:h + 1].reshape(a.shape[1:]) for h in range(ATT_GH)], axis=1)


def _att_rows(q, kp, kc, vp, vc, qn, kn, group, has_prev, dil):
    head = lax.broadcasted_iota(jnp.int32, (ATT_GH, 1, 1), 0) + (ATT_GH * group + 1)
    slope = jnp.exp(head.astype(F32) * (-8.0 / ATT_HEADS * math.log(2.0)))
    o, lse = _att_block(_att_heads(q), _att_heads(kp), _att_heads(kc), _att_heads(vp), _att_heads(vc), qn, kn, slope, has_prev, dil)
    return _att_lanes(o), _att_lanes(lse)


def _att_specs(group, dil):
    blk = (ATT_SPAN, ATT_GH * ATT_HEAD_DIM)
    cur = lambda which: pl.BlockSpec(blk, lambda r, n: (n, r * 9 + 3 * which + group))
    prev = lambda which: pl.BlockSpec(blk, lambda r, n: (jnp.maximum(n - 1, 0), r * 9 + 3 * which + group))
    out = pl.BlockSpec(blk, lambda r, n: (n, r))
    gain = pl.BlockSpec((ATT_GH, 1, ATT_HEAD_DIM), lambda r, n: (0, 0, 0))
    return [cur(0), prev(1), cur(1), prev(2), cur(2), gain, gain], out, gain


def _att_fwd_call(name, group, dil, pa, qn, kn):
    t = pa.shape[0]
    l = t // dil
    w = ATT_GH * ATT_HEAD_DIM
    ins, out, _ = _att_specs(group, dil)
    pav = pa.reshape(l, dil * pa.shape[1])

    def body(q_ref, kp_ref, kc_ref, vp_ref, vc_ref, qn_ref, kn_ref, o_ref, lse_ref):
        o_ref[...], lse_ref[...] = _att_rows(q_ref[...], kp_ref[...], kc_ref[...], vp_ref[...], vc_ref[...], qn_ref[...],
                                             kn_ref[...], group, pl.program_id(1), dil)

    o, lse = pl.pallas_call(
        body, name=name, grid=(dil, l // ATT_SPAN), in_specs=ins, out_specs=[out, out],
        out_shape=[jax.ShapeDtypeStruct((l, dil * w), F32)] * 2, compiler_params=_params("parallel", "arbitrary"),
    )(pav, pav, pav, pav, pav, qn, kn)
    return o.reshape(t, w), lse.reshape(t, w)


def _att_bwd_call(name, group, dil, pa, qn, kn, do, dlse):
    t = pa.shape[0]
    l = t // dil
    w = ATT_GH * ATT_HEAD_DIM
    ins, out, gain = _att_specs(group, dil)
    pav = pa.reshape(l, dil * pa.shape[1])

    def body(q_ref, kp_ref, kc_ref, vp_ref, vc_ref, qn_ref, kn_ref, do_ref, dlse_ref,
             dq_ref, dkp_ref, dkc_ref, dvp_ref, dvc_ref, dqn_ref, dkn_ref):
        has_prev = pl.program_id(1)
        _, vjp = jax.vjp(lambda *a: _att_rows(*a, group, has_prev, dil), q_ref[...], kp_ref[...], kc_ref[...], vp_ref[...],
                         vc_ref[...], qn_ref[...], kn_ref[...])
        dq, dkp, dkc, dvp, dvc, dqn, dkn = vjp((do_ref[...], dlse_ref[...]))
        dq_ref[...], dkp_ref[...], dkc_ref[...], dvp_ref[...], dvc_ref[...] = dq, dkp, dkc, dvp, dvc

        @pl.when((pl.program_id(0) == 0) & (pl.program_id(1) == 0))
        def _():
            dqn_ref[...] = jnp.zeros_like(dqn_ref)
            dkn_ref[...] = jnp.zeros_like(dkn_ref)

        dqn_ref[...] += dqn
        dkn_ref[...] += dkn

    res = pl.pallas_call(
        body, name=name + "_bwd", grid=(dil, l // ATT_SPAN), in_specs=ins + [out, out],
        out_specs=[out] * 5 + [gain, gain],
        out_shape=[jax.ShapeDtypeStruct((l, dil * w), F32)] * 5 + [jax.ShapeDtypeStruct(qn.shape, F32)] * 2,
        compiler_params=_params("arbitrary", "arbitrary"),
    )(pav, pav, pav, pav, pav, qn, kn, do.reshape(l, dil * w), dlse.reshape(l, dil * w))
    dq, dkp, dkc, dvp, dvc, dqn, dkn = res
    back = lambda g: jnp.pad(g[ATT_SPAN:], ((0, ATT_SPAN), (0, 0)))
    return dq.reshape(t, w), (dkc + back(dkp)).reshape(t, w), (dvc + back(dvp)).reshape(t, w), dqn, dkn


def _att_mix(o1, o2, o3, l1, l2, l3):
    m = jnp.maximum(jnp.maximum(l1, l2), l3)
    e1, e2, e3 = jnp.exp(l1 - m), jnp.exp(l2 - m), jnp.exp(l3 - m)
    s = e1 + e2 + e3
    return (jnp.concatenate([o1 * (e1 / s), o2 * (e2 / s), o3 * (e3 / s)], axis=1),)


def att_branch(name, pa, qn, kn):
    e = ATT_HEAD_DIM
    gains = lambda p, g: p[ATT_GH * g:ATT_GH * (g + 1)].reshape(ATT_GH, 1, e)

    @jax.custom_vjp
    def groups(pa, qn, kn):
        res = [_att_fwd_call(f"{name}_att{g}", g, dil, pa, gains(qn, g), gains(kn, g)) for g, (_, dil) in enumerate(ATT_GROUPS)]
        return tuple(r[0] for r in res) + tuple(r[1] for r in res)

    def groups_fwd(pa, qn, kn):
        return groups(pa, qn, kn), (pa, qn, kn)

    def groups_bwd(res, cts):
        pa, qn, kn = res
        n = len(ATT_GROUPS)
        parts = [_att_bwd_call(f"{name}_att{g}", g, dil, pa, gains(qn, g), gains(kn, g), cts[g], cts[n + g])
                 for g, (_, dil) in enumerate(ATT_GROUPS)]
        d_pa = jnp.concatenate([p[i] for i in range(3) for p in parts], axis=1)
        return (d_pa, jnp.concatenate([p[3] for p in parts]).reshape(qn.shape), jnp.concatenate([p[4] for p in parts]).reshape(kn.shape))

    groups.defvjp(groups_fwd, groups_bwd)
    return rowwise(f"{name}_attmix", _att_mix, groups(pa, qn, kn))[0]


def dn_gates(name, ba, a_log, dt_bias):
    place = lambda p: jnp.pad(p.reshape(1, DN_HEADS), ((0, 0), (DN_HEADS, LANES - 2 * DN_HEADS)))

    def f(x, al, dt):
        lane = lax.broadcasted_iota(jnp.int32, x.shape, 1)
        return (jnp.where(lane < DN_HEADS, _sigmoid(x), -jnp.exp(al) * _softplus(x + dt)),)

    return rowwise(name, f, (ba,), (place(a_log), place(dt_bias)))[0]


def _dn_out(o, z, g):
    parts = []
    for h in range(DN_HEADS):
        sl = slice(h * DN_HEAD_DIM, (h + 1) * DN_HEAD_DIM)
        parts.append(_rms(o[:, sl], g[:, sl]) * _silu(z[:, sl]))
    return (jnp.concatenate(parts, axis=1),)


def _merge(ml, za, zb, zc):
    d = D_MODEL
    return (_sigmoid(ml[:, :d]) * za + _sigmoid(ml[:, d:2 * d]) * zb + _sigmoid(ml[:, 2 * d:]) * zc,)


def add_norm(name, x, pend, scale, gain):
    if pend is None:
        return x, rowwise(name, lambda a, g: (_rms(a, g),), (x,), (gain,))[0]

    def f(a, b, g):
        s = a + scale * b
        return s, _rms(s, g)

    return rowwise(name, f, (x, pend), (gain,))


W_IN_PIECES = (("rgx", 0, 1024), ("gate", 1024, 1024), ("att", 2048, 2304), ("dq", 4352, 1024), ("dk", 5376, 1024),
               ("dv", 6400, 1024), ("dz", 7424, 1024), ("ba", 8448, 16), ("mrg", 8464, 3072))
RG_PAR_BLOCKS = ("lane", "lane", "blk", "lane", "blk", "lane", "lane")


def mixer(name, u, w, p):
    mm = lambda nm, a, wt: mm_rows(nm, a[None], wt[None])
    pr = project_in(name + "_in", u, {k: w["in_" + k] for k, _, _ in W_IN_PIECES})
    ya = colwise(name + "_rg", _rg_block, (pr["rgx"], pr["gate"]),
                 (w["rg_conv_w"], p["rg_conv_b"], _block_diag(p["rg_w_r"]), p["rg_b_r"], _block_diag(p["rg_w_i"]),
                  p["rg_b_i"], p["rg_lambda"]), RG_PAR_BLOCKS, 1)[0]
    yb = att_branch(name, pr["att"], p["att_q_norm"], p["att_k_norm"])
    cw = w["dn_conv_w"]
    cq = colwise(name + "_dnq", _dn_conv_block("q"), (pr["dq"],), (cw[:, :1024],), ("lane",), 1)[0]
    ck = colwise(name + "_dnk", _dn_conv_block("k"), (pr["dk"],), (cw[:, 1024:2048],), ("lane",), 1)[0]
    cv = colwise(name + "_dnv", _dn_conv_block("v"), (pr["dv"],), (cw[:, 2048:],), ("lane",), 1)[0]
    gb = dn_gates(name + "_dngate", pr["ba"], p["dn_a_log"], p["dn_dt_bias"])
    o_dn = dn_core(cq, ck, cv, gb)
    yc = rowwise(name + "_dnout", _dn_out, (o_dn, pr["dz"]), (p["dn_out_norm"].reshape(1, D_MODEL),))[0]
    y = rowwise(name + "_merge", _merge, (pr["mrg"], mm(name + "_ba", ya, w["br_a"]), mm(name + "_bb", yb, w["br_b"]),
                                          mm(name + "_bc", yc, w["br_c"])))[0]
    return mm(name + "_out", y, w["w_out"])


def _loss_call(x, pend, target):
    t, d = x.shape
    tile = _row_tile(t)

    def body(x_ref, p_ref, t_ref, loss_ref, g_ref):
        err = x_ref[...] + 0.5 * p_ref[...] - t_ref[...]
        g_ref[...] = err * (1.0 / d)

        @pl.when(pl.program_id(0) == 0)
        def _():
            loss_ref[...] = jnp.zeros_like(loss_ref)

        loss_ref[...] += jnp.full(loss_ref.shape, 0.5 / d, F32) * jnp.sum(err * err)

    blk = pl.BlockSpec((tile, d), lambda i: (i, 0))
    loss, g = pl.pallas_call(
        body, name="loss", grid=(t // tile,), in_specs=[blk, blk, blk],
        out_specs=[pl.BlockSpec((8, LANES), lambda i: (0, 0)), blk],
        out_shape=[jax.ShapeDtypeStruct((8, LANES), F32), jax.ShapeDtypeStruct((t, d), F32)],
        compiler_params=_params("arbitrary"),
    )(x, pend, target)
    return loss[0, 0], g


@jax.custom_vjp
def loss_op(x, pend, target):
    return _loss_call(x, pend, target)[0]


def _loss_fwd(x, pend, target):
    loss, g = _loss_call(x, pend, target)
    return loss, g


def _loss_bwd(g, ct):
    return ct * g, (0.5 * ct) * g, None


loss_op.defvjp(_loss_fwd, _loss_bwd)


def first_ffn(wg, wu, wd, gain, x):
    x, h = add_norm("L0_n1", x, None, 0.0, gain)
    return x, ffn("L0_f1", h, wg, wu, wd)


def rest_of_step(g, conv, p, x, pend, target):
    scale = 0.5
    for l in range(len(p)):
        n = f"L{l}"
        gl = {m: g[m, l] for m, _ in MATRICES if (m, l) in g}
        if l > 0:
            (x, pend), gl = finish_gradients_first((x, pend), gl)
            w = split_layer(gl, {m: conv[m][l] for m, _ in CONVS})
            x, h = add_norm(n + "_n1", x, pend, scale, p[l]["ffn1_norm"])
            pend, scale = ffn(n + "_f1", h, w["ffn1_w_gate"], w["ffn1_w_up"], w["ffn1_w_down"]), 0.5
            x, h = add_norm(n + "_nm", x, pend, scale, p[l]["mix_norm"])
        else:
            x, h = add_norm(n + "_nm", x, pend, scale, p[l]["mix_norm"])
            h, gl = finish_gradients_first(h, gl)
            w = split_layer(gl, {m: conv[m][l] for m, _ in CONVS})
        pend, scale = mixer(n + "_mx", h, w, p[l]), 1.0
        x, h = add_norm(n + "_n2", x, pend, scale, p[l]["ffn2_norm"])
        pend, scale = ffn(n + "_f2", h, w["ffn2_w_gate"], w["ffn2_w_up"], w["ffn2_w_down"]), 0.5
    return loss_op(x, pend, target)


@jax.custom_vjp
def finish_gradients_first(acts, weights):
    return acts, weights


finish_gradients_first.defvjp(lambda acts, weights: ((acts, weights), None), lambda _, cts: lax.optimization_barrier(cts))


WEIGHT_NAMES = ("ffn1_norm", "ffn1_w_gate", "ffn1_w_up", "ffn1_w_down", "mix_norm", "w_in", "rg_conv_w", "rg_conv_b",
                "rg_w_r", "rg_b_r", "rg_w_i", "rg_b_i", "rg_lambda", "att_q_norm", "att_k_norm", "dn_conv_w", "dn_a_log",
                "dn_dt_bias", "dn_out_norm", "w_branch", "w_out", "ffn2_norm", "ffn2_w_gate", "ffn2_w_up", "ffn2_w_down")
MATRICES = (("ffn1_w_gate", 2), ("ffn1_w_up", 2), ("ffn1_w_down", 1), ("w_in", 2), ("w_branch", 1), ("w_out", 1),
            ("ffn2_w_gate", 2), ("ffn2_w_up", 2), ("ffn2_w_down", 1))
CONVS = (("rg_conv_w", 2), ("dn_conv_w", 2))
SHARD_AXIS = dict(MATRICES + CONVS)
SMALL_NAMES = tuple(n for n in WEIGHT_NAMES if n not in SHARD_AXIS)
ROW_PARAMS = ("ffn1_norm", "mix_norm", "rg_conv_b", "rg_b_r", "rg_b_i", "rg_lambda", "ffn2_norm")
FFN_MATS = ("ffn1_w_gate", "ffn1_w_up", "ffn1_w_down", "ffn2_w_gate", "ffn2_w_up", "ffn2_w_down")
TRANSPOSED_MATS = ("ffn1_w_gate", "ffn1_w_up", "ffn2_w_gate", "ffn2_w_up")
W_IN_SHARD = 2884
GATHER_ORDER = ((("ffn1_w_gate", 0), ("ffn1_w_up", 0), ("ffn1_w_down", 0)),
                (("w_in", 0), ("w_branch", 0), ("w_out", 0)),
                None)
GATHER_IDS = (1, 6, 7)
LATE_MATS = ("ffn2_w_gate", "ffn2_w_up", "ffn2_w_down", "w_out", "w_branch")
EXCHANGE_GROUPS = (lambda n, l: l == 1,
                   lambda n, l: l == 0 and (n in LATE_MATS or n == "w_in"),
                   lambda n, l: l == 0 and n not in LATE_MATS and n != "w_in")


def _shard_minor(a, axis):
    a = jnp.moveaxis(a, 0, axis)
    return a.reshape(a.shape[:axis] + (N_CHIPS * a.shape[axis + 1],) + a.shape[axis + 2:])


def _w_in_piece(g, off, n):
    s = W_IN_SHARD
    parts = [g[j][:, max(off, j * s) - j * s:min(off + n, (j + 1) * s) - j * s]
             for j in range(N_CHIPS) if max(off, j * s) < min(off + n, (j + 1) * s)]
    return jnp.concatenate(parts, axis=1) if len(parts) > 1 else parts[0]


def _w_in_chip_grad(gl, j):
    s = W_IN_SHARD
    parts = [gl["in_" + k][:, max(off, j * s) - off:min(off + n, (j + 1) * s) - off]
             for k, off, n in W_IN_PIECES if max(off, j * s) < min(off + n, (j + 1) * s)]
    return jnp.concatenate(parts, axis=1)


def _layer_weights(g, conv):
    w = {n: g[n] for n in FFN_MATS if n in g}
    w["w_out"] = g["w_out"].reshape(D_MODEL, D_MODEL)
    for k, off, n in W_IN_PIECES:
        piece = _w_in_piece(g["w_in"], off, n)
        w["in_" + k] = jnp.pad(piece, ((0, 0), (0, LANES - n))) if n < LANES else piece
    wb = g["w_branch"].reshape(-1, D_MODEL)
    w["br_a"], w["br_b"], w["br_c"] = wb[:1024], wb[1024:1792], wb[1792:]
    return dict(w, **conv)


def _layer_weight_grads(gl):
    out = {n: gl[n] for n in FFN_MATS if n in gl}
    out["w_out"] = gl["w_out"].reshape(N_CHIPS, -1, D_MODEL)
    out["w_branch"] = jnp.concatenate([gl["br_a"], gl["br_b"], gl["br_c"]], axis=0).reshape(N_CHIPS, -1, D_MODEL)
    out["w_in"] = jnp.stack([_w_in_chip_grad(gl, j) for j in range(N_CHIPS)])
    return out, {n: gl[n] for n, _ in CONVS}


@jax.custom_vjp
def split_layer(g, conv):
    return _layer_weights(g, conv)


split_layer.defvjp(lambda g, conv: (_layer_weights(g, conv), None), lambda _, gw: _layer_weight_grads(gw))


def layer_small(small, l):
    p = {n: small[n][l] for n in SMALL_NAMES}
    for n in ROW_PARAMS:
        p[n] = small[n][l:l + 1]
    return p


def layer_small_grads(gp, small):
    return {n: jnp.stack([g[n] for g in gp]).reshape(small[n].shape) for n in SMALL_NAMES}


HBM_SPEC = pl.BlockSpec(memory_space=pl.ANY)


def _place():
    x, y, c = lax.axis_index("x"), lax.axis_index("y"), lax.axis_index("c")
    other_chips = [(1 - x, y), (x, 1 - y), (1 - x, 1 - y)]
    return x, y, c, 2 * x + y, (x, y, 1 - c), other_chips


def _half_rows(ref, lead, hc):
    hr = ref.shape[-2] // 2
    return ref.at[(*lead, pl.ds(pl.multiple_of(hc * hr, 16), hr), slice(None))]


def _chip_index():
    return (2 * lax.axis_index("x") + lax.axis_index("y")).astype(jnp.int32).reshape(1)


def cast_into_blocks(name, w):
    l, rows, cols = w.shape
    tr = rows // 2

    def body(me_ref, w_ref, *o_refs):
        for a, o_ref in enumerate(o_refs):
            o_ref[...] = w_ref[a:a + 1].astype(BF16)

    return pl.pallas_call(
        body, name=name, out_shape=[jax.ShapeDtypeStruct((N_CHIPS, rows, cols), BF16)] * l,
        grid_spec=pltpu.PrefetchScalarGridSpec(
            num_scalar_prefetch=1, grid=(rows // tr,),
            in_specs=[pl.BlockSpec((l, tr, cols), lambda i, me: (0, i, 0))],
            out_specs=[pl.BlockSpec((1, tr, cols), lambda i, me: (me[0], i, 0))] * l),
        compiler_params=_params("parallel"),
    )(_chip_index(), w)


def _gather_blocks(bufs_in, bufs_out, send_sems, recv_sems):
    n = len(bufs_in)
    x, y, c, me, sibling, chips = _place()

    def copy(s, src, dst, to):
        return pltpu.make_async_remote_copy(src_ref=src, dst_ref=dst, send_sem=send_sems.at[s], recv_sem=recv_sems.at[s],
                                            device_id=to, device_id_type=MESH)

    first, passed = [], []
    for j, (cx, cy) in enumerate(chips):
        for i in range(n):
            cp = copy(6 * i + j, _half_rows(bufs_in[i], (me,), c), _half_rows(bufs_out[i], (me,), c), (cx, cy, c))
            cp.start()
            first.append(cp)
    for j, (cx, cy) in enumerate(chips):
        k = 2 * cx + cy
        for i in range(n):
            copy(6 * i + j, _half_rows(bufs_in[i], (me,), c), _half_rows(bufs_out[i], (k,), c), (cx, cy, c)).wait_recv()
            cp = copy(6 * i + 3 + j, _half_rows(bufs_out[i], (k,), c), _half_rows(bufs_out[i], (k,), c), sibling)
            cp.start()
            passed.append(cp)
    for j, (cx, cy) in enumerate(chips):
        k = 2 * cx + cy
        for i in range(n):
            copy(6 * i + 3 + j, _half_rows(bufs_in[i], (me,), c), _half_rows(bufs_out[i], (k,), 1 - c), sibling).wait_recv()
    for cp in first + passed:
        cp.wait_send()


def _handshake(peers):
    barrier = pltpu.get_barrier_semaphore()
    for p in peers:
        pl.semaphore_signal(barrier, inc=1, device_id=p, device_id_type=MESH)
    pl.semaphore_wait(barrier, len(peers))


def allgather_blocks_sc(name, bufs, collective_id):
    n = len(bufs)
    refs = [jax.new_ref(b, memory_space=pltpu.MemorySpace.HBM) for b in bufs]

    @pl.kernel(mesh=plsc.ScalarSubcoreMesh(axis_name="sequencer", num_cores=1), name=name,
               scratch_types=(pltpu.SemaphoreType.DMA((6 * n,)), pltpu.SemaphoreType.DMA((6 * n,))),
               compiler_params=pltpu.CompilerParams(collective_id=collective_id))
    def launch(send_sems, recv_sems):
        x, y, c, me, sibling, chips = _place()
        _handshake([(cx, cy, c) for cx, cy in chips] + [sibling])
        _gather_blocks(refs, refs, send_sems, recv_sems)

    launch()
    return [jax.freeze(r) for r in refs]


PEER_FLIPS = tuple((fx, fy, fc) for fx in (0, 1) for fy in (0, 1) for fc in (0, 1))[1:]


def exchange_pieces_sc(name, gs, collective_id):
    n = len(gs)

    def body(*refs):
        ins, outs = refs[:n], refs[n:2 * n]
        send_sems, recv_sems = refs[2 * n:]
        x, y, c, me, sibling, chips = _place()
        my_dev = 4 * x + 2 * y + c
        flip = lambda v, f: 1 - v if f else v
        peers = [(flip(x, fx), flip(y, fy), flip(c, fc)) for fx, fy, fc in PEER_FLIPS]
        _handshake(peers)
        sends = []
        for r, (px, py, pc) in enumerate(peers):
            for i in range(n):
                cp = pltpu.make_async_remote_copy(
                    src_ref=_half_rows(ins[i], (2 * px + py,), pc), dst_ref=outs[i].at[my_dev], send_sem=send_sems.at[7 * i + r],
                    recv_sem=recv_sems.at[7 * i + r], device_id=(px, py, pc), device_id_type=MESH)
                cp.start()
                sends.append(cp)
        for r, (px, py, pc) in enumerate(peers):
            for i in range(n):
                pltpu.make_async_remote_copy(
                    src_ref=_half_rows(ins[i], (me,), c), dst_ref=outs[i].at[4 * px + 2 * py + pc], send_sem=send_sems.at[7 * i + r],
                    recv_sem=recv_sems.at[7 * i + r], device_id=(px, py, pc), device_id_type=MESH).wait_recv()
        for cp in sends:
            cp.wait_send()

    return pl.kernel(
        body, name=name, mesh=plsc.ScalarSubcoreMesh(axis_name="sequencer", num_cores=1),
        out_type=[jax.ShapeDtypeStruct((N_DEV, g.shape[1] // 2, g.shape[2]), g.dtype) for g in gs],
        scratch_types=[pltpu.SemaphoreType.DMA((7 * n,)), pltpu.SemaphoreType.DMA((7 * n,))],
        compiler_params=pltpu.CompilerParams(collective_id=collective_id),
    )(*gs)


def sibling_share_halves(name, fs):
    n = len(fs)
    every = (slice(None),)

    def body(*refs):
        ins, outs = refs[:n], refs[n:2 * n]
        send_sems, recv_sems = refs[2 * n:]
        x, y, c, me, sibling, chips = _place()
        sends = []
        for i in range(n):
            cp = pltpu.make_async_remote_copy(src_ref=_half_rows(ins[i], every, c), dst_ref=_half_rows(outs[i], every, c),
                                              send_sem=send_sems.at[i], recv_sem=recv_sems.at[i], device_id=sibling, device_id_type=MESH)
            cp.start()
            sends.append(cp)
        for i in range(n):
            pltpu.make_async_remote_copy(src_ref=_half_rows(ins[i], every, c), dst_ref=_half_rows(outs[i], every, 1 - c),
                                         send_sem=send_sems.at[i], recv_sem=recv_sems.at[i], device_id=sibling,
                                         device_id_type=MESH).wait_recv()
        for cp in sends:
            cp.wait_send()

    return pl.pallas_call(
        body, name=name, out_shape=[jax.ShapeDtypeStruct(f.shape, f.dtype) for f in fs],
        in_specs=[HBM_SPEC] * n, out_specs=[HBM_SPEC] * n, input_output_aliases={i: i for i in range(n)},
        scratch_shapes=[pltpu.SemaphoreType.DMA((n,)), pltpu.SemaphoreType.DMA((n,))],
    )(*fs)


def allgather_small_sc(name, v, collective_id):
    def body(v_ref, out_ref, send_sems, recv_sems, local_sem):
        x, y, c, me, sibling, chips = _place()
        my_dev = 4 * x + 2 * y + c
        flip = lambda a, f: 1 - a if f else a
        peers = [(flip(x, fx), flip(y, fy), flip(c, fc)) for fx, fy, fc in PEER_FLIPS]
        _handshake(peers)
        mine = pltpu.make_async_copy(v_ref, out_ref.at[my_dev], local_sem)
        mine.start()
        sends = []
        for r, peer in enumerate(peers):
            cp = pltpu.make_async_remote_copy(src_ref=v_ref, dst_ref=out_ref.at[my_dev], send_sem=send_sems.at[r],
                                              recv_sem=recv_sems.at[r], device_id=peer, device_id_type=MESH)
            cp.start()
            sends.append(cp)
        for r, (px, py, pc) in enumerate(peers):
            pltpu.make_async_remote_copy(src_ref=v_ref, dst_ref=out_ref.at[4 * px + 2 * py + pc], send_sem=send_sems.at[r],
                                         recv_sem=recv_sems.at[r], device_id=(px, py, pc), device_id_type=MESH).wait_recv()
        for cp in sends:
            cp.wait_send()
        mine.wait()

    return pl.kernel(
        body, name=name, mesh=plsc.ScalarSubcoreMesh(axis_name="sequencer", num_cores=1),
        out_type=jax.ShapeDtypeStruct((N_DEV,) + v.shape, v.dtype),
        scratch_types=[pltpu.SemaphoreType.DMA((7,)), pltpu.SemaphoreType.DMA((7,)), pltpu.SemaphoreType.DMA],
        compiler_params=pltpu.CompilerParams(collective_id=collective_id),
    )(v)


SUM_BLOCK_ELEMS = 512 * 1024


def sum_slabs(name, b):
    k, h, w = b.shape

    def body(b_ref, o_ref):
        acc = b_ref[0].astype(F32)
        for i in range(1, k):
            acc = acc + b_ref[i].astype(F32)
        o_ref[...] = acc

    return pl.pallas_call(
        body, name=name, out_shape=jax.ShapeDtypeStruct((h, w), F32),
        in_specs=[pl.BlockSpec(memory_space=pltpu.VMEM)], out_specs=pl.BlockSpec(memory_space=pltpu.VMEM),
        compiler_params=pltpu.CompilerParams(vmem_limit_bytes=VMEM_LIMIT),
    )(b)


def sum_pieces(name, pieces, gs):
    nl = len(pieces)
    k, h, w = pieces[0].shape
    tile = max(t for t in range(16, h + 1, 16) if h % t == 0 and (t * w <= SUM_BLOCK_ELEMS or t == 16))
    nt = h // tile
    x, y, c = lax.axis_index("x"), lax.axis_index("y"), lax.axis_index("c")
    place = [v.astype(jnp.int32).reshape(1) for v in (c, 2 * x + y, 4 * x + 2 * y + c)]

    assert nl == 2

    def tile_of(l, a, i):
        return i * a if l else i * (1 - a) + (nt - 1) * a

    def body(c_ref, me_ref, dev_ref, *refs):
        p_refs, g_refs, o_ref = refs[:nl], refs[nl:2 * nl], refs[2 * nl]
        my_dev = dev_ref[0]
        for l in range(nl):
            @pl.when(pl.program_id(0) == l)
            def _():
                o_ref[0] = jnp.zeros(o_ref.shape[1:], F32)
                for d in range(k):
                    @pl.when(my_dev == d)
                    def _():
                        o_ref[0] += g_refs[l][0].astype(F32)

                    @pl.when(my_dev != d)
                    def _():
                        o_ref[0] += p_refs[l][d].astype(F32)

    in_specs = [pl.BlockSpec((k, tile, w), functools.partial(lambda l, a, i, cc, me, dev: (0, tile_of(l, a, i), 0), l))
                for l in range(nl)]
    in_specs += [pl.BlockSpec((1, tile, w), functools.partial(lambda l, a, i, cc, me, dev: (me[0], cc[0] * nt + tile_of(l, a, i), 0), l))
                 for l in range(nl)]
    return pl.pallas_call(
        body, name=name, out_shape=jax.ShapeDtypeStruct((nl, 2 * h, w), F32),
        grid_spec=pltpu.PrefetchScalarGridSpec(
            num_scalar_prefetch=3, grid=(nl, nt), in_specs=in_specs,
            out_specs=pl.BlockSpec((1, tile, w), lambda a, i, cc, me, dev: (a, cc[0] * nt + i, 0))),
        compiler_params=_params("arbitrary", "arbitrary"),
    )(*place, *pieces, *gs)


def _adam_block(w, g, m, v):
    m = ADAM_B1 * m + (1.0 - ADAM_B1) * g
    v = ADAM_B2 * v + (1.0 - ADAM_B2) * (g * g)
    m_hat = m / (1.0 - ADAM_B1 ** ADAM_STEP)
    v_hat = v / (1.0 - ADAM_B2 ** ADAM_STEP)
    return -ADAM_LR * (m_hat / (jnp.sqrt(v_hat) + ADAM_EPS) + ADAM_WD * w), m, v


def adamw(name, w, g, m, v):
    shape = w.shape
    cols = shape[-1]
    rows = w.size // cols
    tile = 128 if rows % 128 == 0 else rows
    flat = [a.reshape(rows, cols) for a in (w, g, m, v)]

    def body(w_ref, g_ref, m_ref, v_ref, d_ref, nm_ref, nv_ref):
        d_ref[...], nm_ref[...], nv_ref[...] = _adam_block(w_ref[...], g_ref[...], m_ref[...], v_ref[...])

    blk = pl.BlockSpec((tile, cols), lambda i: (i, 0))
    res = pl.pallas_call(
        body, name=name, grid=(rows // tile,), in_specs=[blk] * 4, out_specs=[blk] * 3,
        out_shape=[jax.ShapeDtypeStruct((rows, cols), F32)] * 3, compiler_params=_params("parallel"),
    )(*flat)
    return tuple(r.reshape(shape) for r in res)


def _pack_small(values):
    flat = jnp.concatenate([v.reshape(-1) for v in values.values()])
    n = flat.shape[0]
    total = -(-n // (8 * LANES)) * (8 * LANES)
    return jnp.pad(flat, (0, total - n)).reshape(-1, LANES)


def _unpack_small(v, shapes):
    flat = v.reshape(-1)
    out, off = {}, 0
    for n, shape in shapes.items():
        sz = int(np.prod(shape))
        out[n] = flat[off:off + sz].reshape(shape)
        off += sz
    return out


def kernel(x, ffn1_norm, ffn1_w_gate, ffn1_w_up, ffn1_w_down, mix_norm, w_in, rg_conv_w, rg_conv_b, rg_w_r, rg_b_r, rg_w_i, rg_b_i, rg_lambda, att_q_norm, att_k_norm, dn_conv_w, dn_a_log, dn_dt_bias, dn_out_norm, w_branch, w_out, ffn2_norm, ffn2_w_gate, ffn2_w_up, ffn2_w_down, loss_target, m_ffn1_norm, m_ffn1_w_gate, m_ffn1_w_up, m_ffn1_w_down, m_mix_norm, m_w_in, m_rg_conv_w, m_rg_conv_b, m_rg_w_r, m_rg_b_r, m_rg_w_i, m_rg_b_i, m_rg_lambda, m_att_q_norm, m_att_k_norm, m_dn_conv_w, m_dn_a_log, m_dn_dt_bias, m_dn_out_norm, m_w_branch, m_w_out, m_ffn2_norm, m_ffn2_w_gate, m_ffn2_w_up, m_ffn2_w_down, v_ffn1_norm, v_ffn1_w_gate, v_ffn1_w_up, v_ffn1_w_down, v_mix_norm, v_w_in, v_rg_conv_w, v_rg_conv_b, v_rg_w_r, v_rg_b_r, v_rg_w_i, v_rg_b_i, v_rg_lambda, v_att_q_norm, v_att_k_norm, v_dn_conv_w, v_dn_a_log, v_dn_dt_bias, v_dn_out_norm, v_w_branch, v_w_out, v_ffn2_norm, v_ffn2_w_gate, v_ffn2_w_up, v_ffn2_w_down):
    given = dict(locals())
    for n in TRANSPOSED_MATS:
        for pre in ("", "m_", "v_"):
            given[pre + n] = jnp.swapaxes(given[pre + n], 1, 2)
    small = {n: given[n] for n in SMALL_NAMES}
    n_layers = ffn1_norm.shape[0]
    mat_names = [n for n, _ in MATRICES]
    conv_names = [n for n, _ in CONVS]

    blocks = {}
    for n in mat_names:
        for l, b in enumerate(cast_into_blocks("cast_" + n, given[n])):
            blocks[n, l] = b
    first, done = {}, []
    for i, wanted in enumerate(GATHER_ORDER[:-1]):
        bufs, _ = lax.optimization_barrier(([blocks[k] for k in wanted], done))
        done = allgather_blocks_sc(f"allgather_{i}", bufs, GATHER_IDS[i])
        first.update(zip(wanted, done))
    rest = {k: b for k, b in blocks.items() if k not in first}
    taps = jnp.concatenate([given[n].reshape(-1) for n in conv_names]).reshape(-1, LANES)
    taps = allgather_small_sc("allgather_taps", taps, 8).reshape(N_CHIPS, 2, -1)[:, 0]
    conv, off = {}, 0
    for n, ax in CONVS:
        sz = given[n].size
        conv[n] = _shard_minor(taps[:, off:off + sz].reshape((N_CHIPS,) + given[n].shape), ax)
        off += sz
    p = [layer_small(small, l) for l in range(n_layers)]

    ffn1_keys = GATHER_ORDER[0]
    (x1, pend), first_vjp = jax.vjp(first_ffn, *[first[k] for k in ffn1_keys], p[0]["ffn1_norm"], x[0])
    keys = list(rest)
    bufs, pend, second = lax.optimization_barrier(([rest[k] for k in keys], pend, [first[k] for k in GATHER_ORDER[1]]))
    gathered = dict(zip(keys, allgather_blocks_sc("allgather_2", bufs, GATHER_IDS[2])))
    gathered.update(zip(GATHER_ORDER[1], second))
    loss, (g_mats, g_conv, gp, gx1, gpend) = jax.value_and_grad(rest_of_step, argnums=(0, 1, 2, 3, 4))(
        gathered, conv, p, x1, pend, loss_target[0])
    *g_ffn1, gp[0]["ffn1_norm"], gx = first_vjp((gx1, gpend))
    g_mats.update(zip(ffn1_keys, g_ffn1))

    pieces = {}
    for i, group in enumerate(EXCHANGE_GROUPS):
        keys = [k for k in g_mats if group(*k)]
        pieces.update(zip(keys, exchange_pieces_sc(f"exchange_{i}", [g_mats[k] for k in keys], 2 + i)))
    halves = {n: sum_pieces("sum_" + n, [pieces[n, l] for l in range(n_layers)], [g_mats[n, l] for l in range(n_layers)])
              for n in mat_names}
    grads = {}
    for tag, names in (("late", [n for n in mat_names if n in LATE_MATS]), ("early", [n for n in mat_names if n not in LATE_MATS])):
        grads.update(zip(names, sibling_share_halves("share_" + tag, [halves[n] for n in names])))

    g_small = dict(layer_small_grads(gp, small), **g_conv, loss=loss.reshape(1))
    packed_small = _pack_small(g_small)
    slabs = allgather_small_sc("allgather_small", packed_small, 9)
    summed =_unpack_small(sum_slabs("sum_small", slabs), {n: g.shape for n, g in g_small.items()})
    chip = 2 * lax.axis_index("x") + lax.axis_index("y")
    for n in SMALL_NAMES:
        grads[n] = summed[n]
    for n, ax in CONVS:
        s = given[n].shape[ax]
        grads[n] = lax.dynamic_slice_in_dim(summed[n], chip * s, s, axis=ax)

    upd = {n: adamw("adamw_" + n, given[n], grads[n], given["m_" + n], given["v_" + n]) for n in WEIGHT_NAMES}
    out = lambda n, a: jnp.swapaxes(a, 1, 2) if n in TRANSPOSED_MATS else a
    return (summed["loss"][0], gx[None], *[out(n, grads[n]) for n in WEIGHT_NAMES], *[out(n, upd[n][0]) for n in WEIGHT_NAMES],
            *[out(n, upd[n][1]) for n in WEIGHT_NAMES], *[out(n, upd[n][2]) for n in WEIGHT_NAMES])
```

```python
import functools
import math

import jax
import jax.numpy as jnp
import numpy as np
from jax import lax
from jax.experimental import pallas as pl
from jax.experimental.pallas import tpu as pltpu
from jax.experimental.pallas import tpu_sc as plsc

F32 = jnp.float32
BF16 = jnp.bfloat16
MESH = pl.DeviceIdType.MESH

D_MODEL = 1024
FFN_DIM = 2816
RG_C = 8.0
ATT_GROUPS = ((128, 1), (512, 4), (2048, 16))
ATT_HEADS = 12
ATT_HEAD_DIM = 64
ATT_SPAN = 128
DN_HEADS = 8
DN_HEAD_DIM = 128
DN_CHUNK = 64
EPS = 1e-6
NEG_INF = -1e30
N_CHIPS = 4
N_DEV = 8

ADAM_LR, ADAM_B1, ADAM_B2, ADAM_EPS, ADAM_WD, ADAM_STEP = 0.001, 0.9, 0.999, 1e-08, 0.01, 10

LANES = 128
VMEM_LIMIT = 56 * 1024 * 1024


def _params(*sem):
    return pltpu.CompilerParams(dimension_semantics=sem or None, vmem_limit_bytes=VMEM_LIMIT)


def _sigmoid(x):
    return 1.0 / (1.0 + jnp.exp(-x))


def _silu(x):
    return x * _sigmoid(x)


def _softplus(x):
    return jnp.maximum(x, 0.0) + jnp.log(1.0 + jnp.exp(-jnp.abs(x)))


def _gelu(x):
    return 0.5 * x * (1.0 + jnp.tanh(math.sqrt(2.0 / math.pi) * (x + 0.044715 * (x * x * x))))


def _neg_expm1(x):
    series = -x * (1.0 + x * (0.5 + x * (1.0 / 6 + x * (1.0 / 24 + x * (1.0 / 120 + x * (1.0 / 720))))))
    return jnp.where(x > -0.25, series, 1.0 - jnp.exp(x))


def _rms(x, g):
    return x * lax.rsqrt(jnp.mean(x * x, axis=-1, keepdims=True) + EPS) * g


_MM_DIMS = {"nn": (((1,), (0,)), ((), ())), "nt": (((1,), (1,)), ((), ())), "tn": (((0,), (0,)), ((), ()))}


def _split(a):
    hi = a.astype(BF16)
    return hi, (a - hi.astype(F32)).astype(BF16)


def _mxu(a, b, form, passes):
    (ca, cb), _ = _MM_DIMS[form]
    if a.ndim == 3:
        dims = (((ca[0] + 1,), (cb[0] + 1,)), ((0,), (0,)))
    else:
        dims = _MM_DIMS[form]
    dg = lambda p, q: lax.dot_general(p, q, dims, preferred_element_type=F32)
    if passes == 1:
        return dg(a.astype(BF16), b.astype(BF16))
    (a_hi, a_lo), (b_hi, b_lo) = _split(a), _split(b)
    return dg(a_hi, b_hi) + (dg(a_hi, b_lo) + dg(a_lo, b_hi))


@functools.partial(jax.custom_vjp, nondiff_argnums=(2, 3))
def _mm(a, b, form, passes):
    return _mxu(a, b, form, passes)


def _mm_fwd(a, b, form, passes):
    return _mxu(a, b, form, passes), (a, b)


def _mm_bwd(form, passes, res, g):
    a, b = res
    if form == "nn":
        return _mm(g, b, "nt", passes), _mm(a, g, "tn", passes)
    if form == "nt":
        return _mm(g, b, "nn", passes), _mm(g, a, "tn", passes)
    return _mm(b, g, "nt", passes), _mm(a, g, "nn", passes)


_mm.defvjp(_mm_fwd, _mm_bwd)


def _dot(a, b):
    return _mm(a, b, "nn", 1)


def _dot_nt(a, b):
    return _mm(a, b, "nt", 1)


def _dot_tn(a, b):
    return _mm(a, b, "tn", 1)


def _dot3(a, b):
    return _mm(a, b, "nn", 3)


def _rows(shape):
    return lax.broadcasted_iota(jnp.int32, shape, len(shape) - 2)


def _roll_down(x, s, fill):
    return jnp.where(_rows(x.shape) >= s, pltpu.roll(x, s, x.ndim - 2), fill)


def _roll_up(x, s, fill):
    n = x.shape[-2]
    return jnp.where(_rows(x.shape) < n - s, pltpu.roll(x, n - s, x.ndim - 2), fill)


@functools.partial(jax.custom_vjp, nondiff_argnums=(1,))
def _shift(x, s):
    return _roll_down(x, s, 0.0)


def _shift_fwd(x, s):
    return _roll_down(x, s, 0.0), None


def _shift_bwd(s, _, g):
    return (_roll_up(g, s, 0.0),)


_shift.defvjp(_shift_fwd, _shift_bwd)


def _causal_conv(x, w):
    return w[0:1] * _shift(x, 3) + w[1:2] * _shift(x, 2) + w[2:3] * _shift(x, 1) + w[3:4] * x


@jax.custom_vjp
def _lin_scan(a, b):
    return _lin_scan_fwd(a, b)[0]


def _lin_scan_fwd(a, b):
    a0 = a
    s = 1
    while s < a.shape[0]:
        b = a * _roll_down(b, s, 0.0) + b
        a = a * _roll_down(a, s, 1.0)
        s *= 2
    return b, (a0, b)


def _lin_scan_bwd(res, g):
    a, h = res
    c = _roll_up(a, 1, 0.0)
    s = 1
    while s < a.shape[0]:
        g = c * _roll_up(g, s, 0.0) + g
        c = c * _roll_up(c, s, 1.0)
        s *= 2
    return g * _roll_down(h, 1, 0.0), g


_lin_scan.defvjp(_lin_scan_fwd, _lin_scan_bwd)


@jax.custom_vjp
def _cumsum_rows(x):
    s = 1
    while s < x.shape[-2]:
        x = x + _roll_down(x, s, 0.0)
        s *= 2
    return x


def _cumsum_rows_fwd(x):
    return _cumsum_rows(x), None


def _cumsum_rows_bwd(_, g):
    s = 1
    while s < g.shape[-2]:
        g = g + _roll_up(g, s, 0.0)
        s *= 2
    return (g,)


_cumsum_rows.defvjp(_cumsum_rows_fwd, _cumsum_rows_bwd)


ROW_BLOCK_BYTES = 14 * 1024 * 1024


def _row_tile(t, width=0):
    for tile in (512, 256):
        if t % tile == 0 and (tile == 256 or tile * width * 4 <= ROW_BLOCK_BYTES):
            return tile
    return t


def _rowwise_fwd_call(name, f, rows, pars, tile):
    t = rows[0].shape[0]
    outs = jax.eval_shape(f, *[jax.ShapeDtypeStruct((tile, r.shape[1]), F32) for r in rows],
                          *[jax.ShapeDtypeStruct(p.shape, F32) for p in pars])
    nr, npar = len(rows), len(pars)

    def body(*refs):
        ins = [r[...] for r in refs[:nr + npar]]
        res = f(*ins)
        for o_ref, o in zip(refs[nr + npar:], res):
            o_ref[...] = o.astype(o_ref.dtype)

    return pl.pallas_call(
        body, name=name, grid=(t // tile,),
        in_specs=[pl.BlockSpec((tile, r.shape[1]), lambda i: (i, 0)) for r in rows]
        + [pl.BlockSpec(p.shape, lambda i: (0, 0)) for p in pars],
        out_specs=[pl.BlockSpec((tile, o.shape[1]), lambda i: (i, 0)) for o in outs],
        out_shape=[jax.ShapeDtypeStruct((t, o.shape[1]), F32) for o in outs],
        compiler_params=_params("parallel"),
    )(*rows, *pars)


def _rowwise_bwd_call(name, f, rows, pars, cts, tile):
    t = rows[0].shape[0]
    nr, npar, nct = len(rows), len(pars), len(cts)

    def body(*refs):
        ins = [r[...] for r in refs[:nr + npar]]
        gs = tuple(r[...] for r in refs[nr + npar:nr + npar + nct])
        outs = refs[nr + npar + nct:]
        _, vjp = jax.vjp(f, *ins)
        d = vjp(gs)
        for o_ref, v in zip(outs[:nr], d[:nr]):
            o_ref[...] = v

        @pl.when(pl.program_id(0) == 0)
        def _():
            for o_ref in outs[nr:]:
                o_ref[...] = jnp.zeros_like(o_ref)

        for o_ref, v in zip(outs[nr:], d[nr:]):
            o_ref[...] += v

    res = pl.pallas_call(
        body, name=name, grid=(t // tile,),
        in_specs=[pl.BlockSpec((tile, r.shape[1]), lambda i: (i, 0)) for r in rows]
        + [pl.BlockSpec(p.shape, lambda i: (0, 0)) for p in pars]
        + [pl.BlockSpec((tile, c.shape[1]), lambda i: (i, 0)) for c in cts],
        out_specs=[pl.BlockSpec((tile, r.shape[1]), lambda i: (i, 0)) for r in rows]
        + [pl.BlockSpec(p.shape, lambda i: (0, 0)) for p in pars],
        out_shape=[jax.ShapeDtypeStruct(r.shape, F32) for r in rows]
        + [jax.ShapeDtypeStruct(p.shape, F32) for p in pars],
        compiler_params=_params("arbitrary"),
    )(*rows, *pars, *cts)
    return tuple(res[:nr]), tuple(res[nr:])


def rowwise(name, f, rows, pars=()):
    outs = jax.eval_shape(f, *[jax.ShapeDtypeStruct((8, r.shape[1]), F32) for r in rows],
                          *[jax.ShapeDtypeStruct(p.shape, F32) for p in pars])
    tile = _row_tile(rows[0].shape[0], 2 * sum(r.shape[1] for r in rows) + sum(o.shape[1] for o in outs))

    @jax.custom_vjp
    def op(rows, pars):
        return tuple(_rowwise_fwd_call(name, f, rows, pars, tile))

    def op_fwd(rows, pars):
        return op(rows, pars), (rows, pars)

    def op_bwd(res, cts):
        return _rowwise_bwd_call(name + "_bwd", f, res[0], res[1], tuple(cts), tile)

    op.defvjp(op_fwd, op_bwd)
    return op(tuple(rows), tuple(pars))


MM_TM = 1024


def _tile_of(n, cap):
    best = None
    for c in range(LANES, min(n, cap) + 1, LANES):
        if n % c == 0:
            best = c
    return best or n


def _proj_dw(name, h, dys):
    m, k = h.shape
    n, tm = len(dys), 256
    steps = m // tm

    def body(h_ref, *refs):
        dy_refs, o_refs, accs = refs[:n], refs[n:2 * n], refs[2 * n:]
        ht = jnp.transpose(h_ref[...]).astype(BF16)
        first = pl.program_id(0) == 0
        for dy_ref, acc in zip(dy_refs, accs):
            for c0 in range(0, acc.shape[1], 1024):
                cols = slice(c0, min(c0 + 1024, acc.shape[1]))
                part = _dot(ht, dy_ref[:, cols])

                @pl.when(first)
                def _():
                    acc[:, cols] = part

                @pl.when(jnp.logical_not(first))
                def _():
                    acc[:, cols] += part

        @pl.when(pl.program_id(0) == steps - 1)
        def _():
            for o_ref, acc in zip(o_refs, accs):
                o_ref[...] = acc[...].astype(BF16)

    row = lambda width: pl.BlockSpec((tm, width), lambda i: (i, 0))
    return pl.pallas_call(
        body, name=name, grid=(steps,),
        in_specs=[row(k)] + [row(d.shape[1]) for d in dys],
        out_specs=[pl.BlockSpec((k, d.shape[1]), lambda i: (0, 0)) for d in dys],
        out_shape=[jax.ShapeDtypeStruct((k, d.shape[1]), BF16) for d in dys],
        scratch_shapes=[pltpu.VMEM((k, d.shape[1]), F32) for d in dys],
        compiler_params=_params("arbitrary"),
    )(h, *dys)


PROJ_GROUP_COLS = 4608


def _proj_dh(name, dys, ws, acc):
    m, k = dys[0].shape[0], ws[0].shape[0]
    n, tm = len(dys), 256

    def body(*refs):
        dy_refs, w_refs, rest = refs[:n], refs[n:2 * n], refs[2 * n:]
        total = _dot_nt(dy_refs[0][...], w_refs[0][...])
        for dy_ref, w_ref in zip(dy_refs[1:], w_refs[1:]):
            total = total + _dot_nt(dy_ref[...], w_ref[...])
        if acc is not None:
            total = total + rest[0][...]
        rest[-1][...] = total

    row = lambda width: pl.BlockSpec((tm, width), lambda i: (i, 0))
    return pl.pallas_call(
        body, name=name, grid=(m // tm,),
        in_specs=[row(d.shape[1]) for d in dys] + [pl.BlockSpec(w.shape, lambda i: (0, 0)) for w in ws] + ([row(k)] if acc is not None else []),
        out_specs=row(k), out_shape=jax.ShapeDtypeStruct((m, k), F32), compiler_params=_params("parallel"),
    )(*dys, *ws, *([acc] if acc is not None else []))


def _proj_fwd(name, h, ws):
    m, k = h.shape
    n, tm = len(ws), 256

    def body(h_ref, *refs):
        hv = h_ref[...].astype(BF16)
        for w_ref, o_ref in zip(refs[:n], refs[n:]):
            o_ref[...] = _dot(hv, w_ref[...])

    row = lambda width: pl.BlockSpec((tm, width), lambda i: (i, 0))
    return pl.pallas_call(
        body, name=name, grid=(m // tm,),
        in_specs=[row(k)] + [pl.BlockSpec(w.shape, lambda i: (0, 0)) for w in ws],
        out_specs=[row(w.shape[1]) for w in ws],
        out_shape=[jax.ShapeDtypeStruct((m, w.shape[1]), F32) for w in ws], compiler_params=_params("parallel"),
    )(h, *ws)


def project_in(name, h, ws):
    keys = list(ws)
    groups, cols = [[]], 0
    for p in keys:
        if groups[-1] and cols + ws[p].shape[1] > PROJ_GROUP_COLS:
            groups.append([])
            cols = 0
        groups[-1].append(p)
        cols += ws[p].shape[1]

    @jax.custom_vjp
    def op(h, ws):
        out = {}
        for i, group in enumerate(groups):
            out.update(zip(group, _proj_fwd(f"{name}_{i}", h, [ws[p] for p in group])))
        return out

    def op_fwd(h, ws):
        return op(h, ws), (h, ws)

    def op_bwd(res, dys):
        h, ws = res
        dh, dws = None, {}
        for i, group in enumerate(groups):
            dh = _proj_dh(f"{name}_dh{i}", [dys[p] for p in group], [ws[p] for p in group], dh)
            dws.update(zip(group, _proj_dw(f"{name}_dw{i}", h, [dys[p] for p in group])))
        return dh, dws

    op.defvjp(op_fwd, op_bwd)
    return op(h, ws)


def _ffn_up(name, h, wt):
    m, k = h.shape
    j, n, _ = wt.shape
    tm = MM_TM

    def body(h_ref, w_ref, o_ref):
        o_ref[0] = _dot_nt(h_ref[...], w_ref[0])

    return pl.pallas_call(
        body, name=name, grid=(m // tm, j),
        in_specs=[pl.BlockSpec((tm, k), lambda i, b: (i, 0)), pl.BlockSpec((1, n, k), lambda i, b: (b, 0, 0))],
        out_specs=pl.BlockSpec((1, tm, n), lambda i, b: (b, i, 0)),
        out_shape=jax.ShapeDtypeStruct((j, m, n), F32), compiler_params=_params("parallel", "parallel"),
    )(h, wt)


def _ffn_down(name, g, u, wd):
    j, m, n = g.shape
    d = wd.shape[2]
    tm = MM_TM

    def body(g_ref, u_ref, w_ref, o_ref):
        part = _dot(_silu(g_ref[0]) * u_ref[0], w_ref[0])

        @pl.when(pl.program_id(1) == 0)
        def _():
            o_ref[...] = part

        @pl.when(pl.program_id(1) > 0)
        def _():
            o_ref[...] += part

    act = pl.BlockSpec((1, tm, n), lambda i, b: (b, i, 0))
    return pl.pallas_call(
        body, name=name, grid=(m // tm, j),
        in_specs=[act, act, pl.BlockSpec((1, n, d), lambda i, b: (b, 0, 0))],
        out_specs=pl.BlockSpec((tm, d), lambda i, b: (i, 0)),
        out_shape=jax.ShapeDtypeStruct((m, d), F32), compiler_params=_params("parallel", "arbitrary"),
    )(g, u, wd)


def _ffn_down_bwd(name, dy, g, u, wd):
    j, m, n = g.shape
    d = wd.shape[2]
    tm = MM_TM

    def body(dy_ref, g_ref, u_ref, w_ref, dg_ref, du_ref):
        da = _dot_nt(dy_ref[...], w_ref[0])
        gv = g_ref[0]
        s = _sigmoid(gv)
        dg_ref[0] = da * u_ref[0] * (s * (1.0 + gv * (1.0 - s)))
        du_ref[0] = da * (gv * s)

    act = pl.BlockSpec((1, tm, n), lambda i, b: (b, i, 0))
    return pl.pallas_call(
        body, name=name, grid=(m // tm, j),
        in_specs=[pl.BlockSpec((tm, d), lambda i, b: (i, 0)), act, act, pl.BlockSpec((1, n, d), lambda i, b: (b, 0, 0))],
        out_specs=[act, act], out_shape=[jax.ShapeDtypeStruct((j, m, n), F32)] * 2,
        compiler_params=_params("parallel", "parallel"),
    )(dy, g, u, wd)


def _ffn_down_dw(name, g, u, dy):
    j, m, n = g.shape
    d = dy.shape[1]
    tn = _tile_of(d, 512)

    def body(g_ref, u_ref, dy_ref, o_ref):
        o_ref[0] = _dot_tn(_silu(g_ref[0]) * u_ref[0], dy_ref[...]).astype(BF16)

    act = pl.BlockSpec((1, m, n), lambda b, c: (b, 0, 0))
    return pl.pallas_call(
        body, name=name, grid=(j, d // tn),
        in_specs=[act, act, pl.BlockSpec((m, tn), lambda b, c: (0, c))],
        out_specs=pl.BlockSpec((1, n, tn), lambda b, c: (b, 0, c)),
        out_shape=jax.ShapeDtypeStruct((j, n, d), BF16), compiler_params=_params("parallel", "parallel"),
    )(g, u, dy)


def _ffn_up_dh(name, dg, du, wg, wu):
    j, m, n = dg.shape
    k = wg.shape[2]
    tm = MM_TM

    def body(dg_ref, du_ref, wg_ref, wu_ref, o_ref):
        part = _dot(dg_ref[0], wg_ref[0]) + _dot(du_ref[0], wu_ref[0])

        @pl.when(pl.program_id(1) == 0)
        def _():
            o_ref[...] = part

        @pl.when(pl.program_id(1) > 0)
        def _():
            o_ref[...] += part

    act = pl.BlockSpec((1, tm, n), lambda i, b: (b, i, 0))
    wsp = pl.BlockSpec((1, n, k), lambda i, b: (b, 0, 0))
    return pl.pallas_call(
        body, name=name, grid=(m // tm, j), in_specs=[act, act, wsp, wsp],
        out_specs=pl.BlockSpec((tm, k), lambda i, b: (i, 0)),
        out_shape=jax.ShapeDtypeStruct((m, k), F32), compiler_params=_params("parallel", "arbitrary"),
    )(dg, du, wg, wu)


def _ffn_up_dw(name, dy, h):
    j, m, n = dy.shape
    k = h.shape[1]
    tk = _tile_of(k, 512)

    def body(dy_ref, h_ref, o_ref):
        o_ref[0] = _dot_tn(dy_ref[0], h_ref[...]).astype(BF16)

    return pl.pallas_call(
        body, name=name, grid=(j, k // tk),
        in_specs=[pl.BlockSpec((1, m, n), lambda b, i: (b, 0, 0)), pl.BlockSpec((m, tk), lambda b, i: (0, i))],
        out_specs=pl.BlockSpec((1, n, tk), lambda b, i: (b, 0, i)),
        out_shape=jax.ShapeDtypeStruct((j, n, k), BF16), compiler_params=_params("parallel", "parallel"),
    )(dy, h)


def ffn(name, h, wg, wu, wd):
    @jax.custom_vjp
    def op(h, wg, wu, wd):
        return _ffn_down(name + "_d", _ffn_up(name + "_g", h, wg), _ffn_up(name + "_u", h, wu), wd)

    def op_fwd(h, wg, wu, wd):
        g, u = _ffn_up(name + "_g", h, wg), _ffn_up(name + "_u", h, wu)
        return _ffn_down(name + "_d", g, u, wd), (h, g, u, wg, wu, wd)

    def op_bwd(res, dy):
        h, g, u, wg, wu, wd = res
        dg, du = _ffn_down_bwd(name + "_d_bwd", dy, g, u, wd)
        return (_ffn_up_dh(name + "_dh", dg, du, wg, wu), _ffn_up_dw(name + "_g_dw", dg, h), _ffn_up_dw(name + "_u_dw", du, h),
                _ffn_down_dw(name + "_d_dw", g, u, dy))

    op.defvjp(op_fwd, op_bwd)
    return op(h, wg, wu, wd)


def _mmr_fwd(name, a, w):
    j, m, n = a.shape
    nn = w.shape[2]
    tm, tn = MM_TM, _tile_of(nn, 1024)

    def body(a_ref, w_ref, o_ref):
        part = _dot(a_ref[0], w_ref[0])

        @pl.when(pl.program_id(2) == 0)
        def _():
            o_ref[...] = part

        @pl.when(pl.program_id(2) > 0)
        def _():
            o_ref[...] += part

    return pl.pallas_call(
        body, name=name, grid=(m // tm, nn // tn, j),
        in_specs=[pl.BlockSpec((1, tm, n), lambda i, c, b: (b, i, 0)), pl.BlockSpec((1, n, tn), lambda i, c, b: (b, 0, c))],
        out_specs=pl.BlockSpec((tm, tn), lambda i, c, b: (i, c)),
        out_shape=jax.ShapeDtypeStruct((m, nn), F32),
        compiler_params=_params("parallel", "parallel", "arbitrary"),
    )(a, w)


def _mmr_da(name, dy, w):
    m, nn = dy.shape
    j, n, _ = w.shape
    tm = MM_TM

    def body(dy_ref, w_ref, o_ref):
        o_ref[0] = _dot_nt(dy_ref[...], w_ref[0])

    return pl.pallas_call(
        body, name=name, grid=(m // tm, j),
        in_specs=[pl.BlockSpec((tm, nn), lambda i, b: (i, 0)), pl.BlockSpec((1, n, nn), lambda i, b: (b, 0, 0))],
        out_specs=pl.BlockSpec((1, tm, n), lambda i, b: (b, i, 0)),
        out_shape=jax.ShapeDtypeStruct((j, m, n), F32),
        compiler_params=_params("parallel", "parallel"),
    )(dy, w)


def _mmr_dw(name, a, dy):
    j, m, n = a.shape
    nn = dy.shape[1]
    tn = _tile_of(nn, 512)

    def body(a_ref, dy_ref, o_ref):
        o_ref[0] = _dot_tn(a_ref[0], dy_ref[...]).astype(BF16)

    return pl.pallas_call(
        body, name=name, grid=(j, nn // tn),
        in_specs=[pl.BlockSpec((1, m, n), lambda b, c: (b, 0, 0)), pl.BlockSpec((m, tn), lambda b, c: (0, c))],
        out_specs=pl.BlockSpec((1, n, tn), lambda b, c: (b, 0, c)),
        out_shape=jax.ShapeDtypeStruct((j, n, nn), BF16),
        compiler_params=_params("parallel", "parallel"),
    )(a, dy)


def mm_rows(name, a, w):
    @jax.custom_vjp
    def op(a, w):
        return _mmr_fwd(name, a, w)

    def op_fwd(a, w):
        return op(a, w), (a, w)

    def op_bwd(res, dy):
        a, w = res
        return _mmr_da(name + "_da", dy, w), _mmr_dw(name + "_dw", a, dy)

    op.defvjp(op_fwd, op_bwd)
    return op(a, w)


def _colwise_specs(cols, pars, par_block):
    t = cols[0].shape[0]
    specs = [pl.BlockSpec((t, LANES), lambda j: (0, j)) for _ in cols]
    for p, blk in zip(pars, par_block):
        if blk == "lane":
            specs.append(pl.BlockSpec((p.shape[0], LANES), lambda j: (0, j)))
        else:
            specs.append(pl.BlockSpec((1,) + p.shape[1:], lambda j: (j, 0, 0)))
    return specs


def _colwise_fwd_call(name, f, cols, pars, par_block, n_out):
    t, c = cols[0].shape
    nc, npar = len(cols), len(pars)

    def body(*refs):
        ins = [r[...] for r in refs[:nc]] + [r[...] if b == "lane" else r[0] for r, b in zip(refs[nc:nc + npar], par_block)]
        res = f(*ins)
        for o_ref, o in zip(refs[nc + npar:], res):
            o_ref[...] = o

    return pl.pallas_call(
        body, name=name, grid=(c // LANES,),
        in_specs=_colwise_specs(cols, pars, par_block),
        out_specs=[pl.BlockSpec((t, LANES), lambda j: (0, j)) for _ in range(n_out)],
        out_shape=[jax.ShapeDtypeStruct((t, c), F32) for _ in range(n_out)],
        compiler_params=_params("parallel"),
    )(*cols, *pars)


def _colwise_bwd_call(name, f, cols, pars, par_block, cts):
    t, c = cols[0].shape
    nc, npar, nct = len(cols), len(pars), len(cts)

    def body(*refs):
        ins = [r[...] for r in refs[:nc]] + [r[...] if b == "lane" else r[0] for r, b in zip(refs[nc:nc + npar], par_block)]
        gs = tuple(r[...] for r in refs[nc + npar:nc + npar + nct])
        outs = refs[nc + npar + nct:]
        _, vjp = jax.vjp(f, *ins)
        d = vjp(gs)
        for o_ref, v in zip(outs[:nc], d[:nc]):
            o_ref[...] = v
        for o_ref, v, b in zip(outs[nc:], d[nc:], par_block):
            if b == "lane":
                o_ref[...] = v
            else:
                o_ref[0] = v

    res = pl.pallas_call(
        body, name=name, grid=(c // LANES,),
        in_specs=_colwise_specs(cols, pars, par_block) + [pl.BlockSpec((t, LANES), lambda j: (0, j)) for _ in cts],
        out_specs=_colwise_specs(cols, pars, par_block),
        out_shape=[jax.ShapeDtypeStruct(v.shape, F32) for v in (*cols, *pars)],
        compiler_params=_params("parallel"),
    )(*cols, *pars, *cts)
    return tuple(res[:nc]), tuple(res[nc:])


def colwise(name, f, cols, pars, par_block, n_out):
    @jax.custom_vjp
    def op(cols, pars):
        return tuple(_colwise_fwd_call(name, f, cols, pars, par_block, n_out))

    def op_fwd(cols, pars):
        return op(cols, pars), (cols, pars)

    def op_bwd(res, cts):
        return _colwise_bwd_call(name + "_bwd", f, res[0], res[1], par_block, tuple(cts))

    op.defvjp(op_fwd, op_bwd)
    return op(tuple(cols), tuple(pars))


def _rg_block(x, gate, cw, cb, wr, br, wi, bi, lam):
    xa = _causal_conv(x, cw) + cb
    r = _sigmoid(_dot(xa, wr) + br)
    i = _sigmoid(_dot(xa, wi) + bi)
    log_a = -RG_C * r * _softplus(-lam)
    a = jnp.exp(log_a)
    b = jnp.sqrt(_neg_expm1(2.0 * log_a)) * (i * xa)
    return (_lin_scan(a, b) * _gelu(gate),)


def _dn_conv_block(mode):
    def f(x, cw):
        c = _silu(_causal_conv(x, cw))
        if mode == "v":
            return (c,)
        c = c * lax.rsqrt(jnp.sum(c * c, axis=-1, keepdims=True) + EPS)
        return (c * (DN_HEAD_DIM ** -0.5),) if mode == "q" else (c,)
    return f


def _block_diag(w):
    w = w.reshape(8, 2, 64, 64)
    z = jnp.zeros((8, 64, 64), w.dtype)
    top = jnp.concatenate([w[:, 0], z], axis=2)
    bot = jnp.concatenate([z, w[:, 1]], axis=2)
    return jnp.concatenate([top, bot], axis=1)


DN_HP = 8


def _dn_block(S, qw, kw, vw, gb, h0, tinv=None):
    hp, hd = S.shape[0], DN_HEAD_DIM
    heads = lambda a: jnp.concatenate([a[None, :, j * hd:(j + 1) * hd] for j in range(hp)], axis=0)
    lane = lax.broadcasted_iota(jnp.int32, gb.shape, 1)
    col = lambda i: jnp.sum(jnp.where(lane == i, gb, 0.0), axis=1, keepdims=True)[None]
    beta = jnp.concatenate([col(h0 + j) for j in range(hp)], axis=0)
    g = jnp.concatenate([col(h0 + j + DN_HEADS) for j in range(hp)], axis=0)
    s_new, o, tinv = _dn_step(S, heads(qw), heads(kw), heads(vw), beta, g, tinv)
    return s_new, jnp.concatenate([o[j:j + 1].reshape(o.shape[1:]) for j in range(hp)], axis=1), tinv


@jax.custom_vjp
def _unit_lower_inverse(a):
    c = a.shape[-1]
    eye = (lax.broadcasted_iota(jnp.int32, (c, c), 0) == lax.broadcasted_iota(jnp.int32, (c, c), 1)).astype(F32)
    p = -a
    tinv = eye + p
    for _ in range(5):
        p = _dot3(p, p)
        tinv = tinv + _dot3(tinv, p)
    return tinv


def _unit_lower_inverse_fwd(a):
    t = _unit_lower_inverse(a)
    return t, t


def _unit_lower_inverse_bwd(t, g):
    return (-_mm(_mm(t, g, "tn", 3), t, "nt", 3),)


_unit_lower_inverse.defvjp(_unit_lower_inverse_fwd, _unit_lower_inverse_bwd)


@jax.custom_vjp
def _known_inverse(a, t):
    return t


_known_inverse.defvjp(lambda a, t: (t, t), lambda t, g: (_unit_lower_inverse_bwd(t, g)[0], jnp.zeros_like(t)))


def _dn_step(S, q, k, v, beta, g, tinv=None):
    c = DN_CHUNK
    ri = lax.broadcasted_iota(jnp.int32, (c, c), 0)
    ci = lax.broadcasted_iota(jnp.int32, (c, c), 1)
    incl, strict = ri >= ci, ri > ci
    gam = _cumsum_rows(g)
    gam_row = jnp.sum(jnp.where(ri <= ci, g, 0.0), axis=-2, keepdims=True)
    gam_last = jnp.sum(g, axis=-2, keepdims=True)
    decay = jnp.where(incl, jnp.exp(jnp.where(incl, gam - gam_row, 0.0)), 0.0)
    kb = k * beta
    vb = v * beta
    a = jnp.where(strict, _dot_nt(kb, k) * decay, 0.0)
    tinv = _unit_lower_inverse(a) if tinv is None else _known_inverse(a, tinv)
    e_gam = jnp.exp(gam)
    u0 = _dot3(tinv, vb)
    wk = _dot3(tinv, kb * e_gam)
    qk = jnp.where(incl, _dot_nt(q, k) * decay, 0.0)
    q_dec = q * e_gam
    k_dec = k * jnp.exp(gam_last - gam)
    u = u0 - _dot(wk, S)
    o = _dot(q_dec, S) + _dot(qk, u)
    s_new = S * jnp.exp(gam_last) + _dot_tn(k_dec, u)
    return s_new, o, tinv


def _dn_fwd_call(q, k, v, gb):
    t, w = q.shape
    n, hp, hd, c = t // DN_CHUNK, DN_HP, DN_HEAD_DIM, DN_CHUNK

    def body(q_ref, k_ref, v_ref, gb_ref, o_ref, s0_ref, ti_ref, s_scr):
        @pl.when(pl.program_id(1) == 0)
        def _():
            s_scr[...] = jnp.zeros_like(s_scr)

        s_old = s_scr[...]
        s0_ref[:, 0] = s_old
        s_new, o, tinv = _dn_block(s_old, q_ref[...], k_ref[...], v_ref[...], gb_ref[...], pl.program_id(0) * hp)
        o_ref[...] = o
        ti_ref[:, 0] = tinv
        s_scr[...] = s_new

    blk = pl.BlockSpec((c, hp * hd), lambda g, i: (i, g))
    return pl.pallas_call(
        body, name="dn_core", grid=(DN_HEADS // hp, n),
        in_specs=[blk, blk, blk, pl.BlockSpec((c, LANES), lambda g, i: (i, 0))],
        out_specs=[blk, pl.BlockSpec((hp, 1, hd, hd), lambda g, i: (g, i, 0, 0)), pl.BlockSpec((hp, 1, c, c), lambda g, i: (g, i, 0, 0))],
        out_shape=[jax.ShapeDtypeStruct((t, w), F32), jax.ShapeDtypeStruct((DN_HEADS, n, hd, hd), F32),
                   jax.ShapeDtypeStruct((DN_HEADS, n, c, c), F32)],
        scratch_shapes=[pltpu.VMEM((hp, hd, hd), F32)],
        compiler_params=_params("parallel", "arbitrary"),
    )(q, k, v, gb)


def _dn_bwd_call(q, k, v, gb, s0, ti, do):
    t, w = q.shape
    n, hp, hd, c = t // DN_CHUNK, DN_HP, DN_HEAD_DIM, DN_CHUNK
    ng = DN_HEADS // hp

    def body(q_ref, k_ref, v_ref, gb_ref, s0_ref, ti_ref, do_ref, dq_ref, dk_ref, dv_ref, dgb_ref, ds_scr):
        @pl.when(pl.program_id(1) == 0)
        def _():
            ds_scr[...] = jnp.zeros_like(ds_scr)

        h0, tinv = pl.program_id(0) * hp, ti_ref[:, 0]
        _, vjp = jax.vjp(lambda *a: _dn_block(*a, h0, tinv)[:2], s0_ref[:, 0], q_ref[...], k_ref[...], v_ref[...], gb_ref[...])
        ds, dq, dk, dv, dgb = vjp((ds_scr[...], do_ref[...]))
        ds_scr[...] = ds
        dq_ref[...], dk_ref[...], dv_ref[...] = dq, dk, dv
        dgb_ref[0] = dgb

    blk = pl.BlockSpec((c, hp * hd), lambda g, i: (n - 1 - i, g))
    res = pl.pallas_call(
        body, name="dn_core_bwd", grid=(ng, n),
        in_specs=[blk, blk, blk, pl.BlockSpec((c, LANES), lambda g, i: (n - 1 - i, 0)),
                  pl.BlockSpec((hp, 1, hd, hd), lambda g, i: (g, n - 1 - i, 0, 0)),
                  pl.BlockSpec((hp, 1, c, c), lambda g, i: (g, n - 1 - i, 0, 0)), blk],
        out_specs=[blk, blk, blk, pl.BlockSpec((1, c, LANES), lambda g, i: (g, n - 1 - i, 0))],
        out_shape=[jax.ShapeDtypeStruct((t, w), F32)] * 3 + [jax.ShapeDtypeStruct((ng, t, LANES), F32)],
        scratch_shapes=[pltpu.VMEM((hp, hd, hd), F32)],
        compiler_params=_params("parallel", "arbitrary"),
    )(q, k, v, gb, s0, ti, do)
    return res[0], res[1], res[2], jnp.sum(res[3], axis=0)


@jax.custom_vjp
def dn_core(q, k, v, gb):
    return _dn_fwd_call(q, k, v, gb)[0]


def _dn_core_fwd(q, k, v, gb):
    o, s0, ti = _dn_fwd_call(q, k, v, gb)
    return o, (q, k, v, gb, s0, ti)


def _dn_core_bwd(res, do):
    return _dn_bwd_call(*res, do)


dn_core.defvjp(_dn_core_fwd, _dn_core_bwd)


ATT_GH = 4


def _att_block(q, kp, kc, vp, vc, qn, kn, slope, has_prev, dil):
    s = ATT_SPAN
    qh = _rms(q, qn) * (ATT_HEAD_DIM ** -0.5)
    qi = lax.broadcasted_iota(jnp.int32, (s, s), 0)
    kj = lax.broadcasted_iota(jnp.int32, (s, s), 1)
    d_p = qi + s - kj
    d_c = qi - kj
    s_p = _dot_nt(qh, _rms(kp, kn)) - slope * (d_p * dil).astype(F32)
    s_c = _dot_nt(qh, _rms(kc, kn)) - slope * (d_c * dil).astype(F32)
    s_p = jnp.where((d_p <= s) & (has_prev > 0), s_p, NEG_INF)
    s_c = jnp.where(d_c >= 0, s_c, NEG_INF)
    m = lax.stop_gradient(jnp.maximum(jnp.max(s_p, axis=-1, keepdims=True), jnp.max(s_c, axis=-1, keepdims=True)))
    p_p = jnp.exp(s_p - m)
    p_c = jnp.exp(s_c - m)
    den = jnp.sum(p_p, axis=-1, keepdims=True) + jnp.sum(p_c, axis=-1, keepdims=True)
    o = _dot(p_p / den, vp) + _dot(p_c / den, vc)
    lse = m + jnp.log(den)
    return o, jnp.broadcast_to(lse, o.shape)


def _att_heads(a):
    e = ATT_HEAD_DIM
    return jnp.concatenate([a[None, :, h * e:(h + 1) * e] for h in range(ATT_GH)], axis=0)


def _att_lanes(a):
    return jnp.concatenate([a[h:h + 1].reshape(a.shape[1:]) for h in range(ATT_GH)], axis=1)


def _att_rows(q, kp, kc, vp, vc, qn, kn, group, has_prev, dil):
    head = lax.broadcasted_iota(jnp.int32, (ATT_GH, 1, 1), 0) + (ATT_GH * group + 1)
    slope = jnp.exp(head.astype(F32) * (-8.0 / ATT_HEADS * math.log(2.0)))
    o, lse = _att_block(_att_heads(q), _att_heads(kp), _att_heads(kc), _att_heads(vp), _att_heads(vc), qn, kn, slope, has_prev, dil)
    return _att_lanes(o), _att_lanes(lse)


def _att_specs(group, dil):
    blk = (ATT_SPAN, ATT_GH * ATT_HEAD_DIM)
    cur = lambda which: pl.BlockSpec(blk, lambda r, n: (n, r * 9 + 3 * which + group))
    prev = lambda which: pl.BlockSpec(blk, lambda r, n: (jnp.maximum(n - 1, 0), r * 9 + 3 * which + group))
    out = pl.BlockSpec(blk, lambda r, n: (n, r))
    gain = pl.BlockSpec((ATT_GH, 1, ATT_HEAD_DIM), lambda r, n: (0, 0, 0))
    return [cur(0), prev(1), cur(1), prev(2), cur(2), gain, gain], out, gain


def _att_fwd_call(name, group, dil, pa, qn, kn):
    t = pa.shape[0]
    l = t // dil
    w = ATT_GH * ATT_HEAD_DIM
    ins, out, _ = _att_specs(group, dil)
    pav = pa.reshape(l, dil * pa.shape[1])

    def body(q_ref, kp_ref, kc_ref, vp_ref, vc_ref, qn_ref, kn_ref, o_ref, lse_ref):
        o_ref[...], lse_ref[...] = _att_rows(q_ref[...], kp_ref[...], kc_ref[...], vp_ref[...], vc_ref[...], qn_ref[...],
                                             kn_ref[...], group, pl.program_id(1), dil)

    o, lse = pl.pallas_call(
        body, name=name, grid=(dil, l // ATT_SPAN), in_specs=ins, out_specs=[out, out],
        out_shape=[jax.ShapeDtypeStruct((l, dil * w), F32)] * 2, compiler_params=_params("parallel", "arbitrary"),
    )(pav, pav, pav, pav, pav, qn, kn)
    return o.reshape(t, w), lse.reshape(t, w)


def _att_bwd_call(name, group, dil, pa, qn, kn, do, dlse):
    t = pa.shape[0]
    l = t // dil
    w = ATT_GH * ATT_HEAD_DIM
    ins, out, gain = _att_specs(group, dil)
    pav = pa.reshape(l, dil * pa.shape[1])

    def body(q_ref, kp_ref, kc_ref, vp_ref, vc_ref, qn_ref, kn_ref, do_ref, dlse_ref,
             dq_ref, dkp_ref, dkc_ref, dvp_ref, dvc_ref, dqn_ref, dkn_ref):
        has_prev = pl.program_id(1)
        _, vjp = jax.vjp(lambda *a: _att_rows(*a, group, has_prev, dil), q_ref[...], kp_ref[...], kc_ref[...], vp_ref[...],
                         vc_ref[...], qn_ref[...], kn_ref[...])
        dq, dkp, dkc, dvp, dvc, dqn, dkn = vjp((do_ref[...], dlse_ref[...]))
        dq_ref[...], dkp_ref[...], dkc_ref[...], dvp_ref[...], dvc_ref[...] = dq, dkp, dkc, dvp, dvc

        @pl.when((pl.program_id(0) == 0) & (pl.program_id(1) == 0))
        def _():
            dqn_ref[...] = jnp.zeros_like(dqn_ref)
            dkn_ref[...] = jnp.zeros_like(dkn_ref)

        dqn_ref[...] += dqn
        dkn_ref[...] += dkn

    res = pl.pallas_call(
        body, name=name + "_bwd", grid=(dil, l // ATT_SPAN), in_specs=ins + [out, out],
        out_specs=[out] * 5 + [gain, gain],
        out_shape=[jax.ShapeDtypeStruct((l, dil * w), F32)] * 5 + [jax.ShapeDtypeStruct(qn.shape, F32)] * 2,
        compiler_params=_params("arbitrary", "arbitrary"),
    )(pav, pav, pav, pav, pav, qn, kn, do.reshape(l, dil * w), dlse.reshape(l, dil * w))
    dq, dkp, dkc, dvp, dvc, dqn, dkn = res
    back = lambda g: jnp.pad(g[ATT_SPAN:], ((0, ATT_SPAN), (0, 0)))
    return dq.reshape(t, w), (dkc + back(dkp)).reshape(t, w), (dvc + back(dvp)).reshape(t, w), dqn, dkn


def _att_mix(o1, o2, o3, l1, l2, l3):
    m = jnp.maximum(jnp.maximum(l1, l2), l3)
    e1, e2, e3 = jnp.exp(l1 - m), jnp.exp(l2 - m), jnp.exp(l3 - m)
    s = e1 + e2 + e3
    return (jnp.concatenate([o1 * (e1 / s), o2 * (e2 / s), o3 * (e3 / s)], axis=1),)


def att_branch(name, pa, qn, kn):
    e = ATT_HEAD_DIM
    gains = lambda p, g: p[ATT_GH * g:ATT_GH * (g + 1)].reshape(ATT_GH, 1, e)

    @jax.custom_vjp
    def groups(pa, qn, kn):
        res = [_att_fwd_call(f"{name}_att{g}", g, dil, pa, gains(qn, g), gains(kn, g)) for g, (_, dil) in enumerate(ATT_GROUPS)]
        return tuple(r[0] for r in res) + tuple(r[1] for r in res)

    def groups_fwd(pa, qn, kn):
        return groups(pa, qn, kn), (pa, qn, kn)

    def groups_bwd(res, cts):
        pa, qn, kn = res
        n = len(ATT_GROUPS)
        parts = [_att_bwd_call(f"{name}_att{g}", g, dil, pa, gains(qn, g), gains(kn, g), cts[g], cts[n + g])
                 for g, (_, dil) in enumerate(ATT_GROUPS)]
        d_pa = jnp.concatenate([p[i] for i in range(3) for p in parts], axis=1)
        return (d_pa, jnp.concatenate([p[3] for p in parts]).reshape(qn.shape), jnp.concatenate([p[4] for p in parts]).reshape(kn.shape))

    groups.defvjp(groups_fwd, groups_bwd)
    return rowwise(f"{name}_attmix", _att_mix, groups(pa, qn, kn))[0]


def dn_gates(name, ba, a_log, dt_bias):
    place = lambda p: jnp.pad(p.reshape(1, DN_HEADS), ((0, 0), (DN_HEADS, LANES - 2 * DN_HEADS)))

    def f(x, al, dt):
        lane = lax.broadcasted_iota(jnp.int32, x.shape, 1)
        return (jnp.where(lane < DN_HEADS, _sigmoid(x), -jnp.exp(al) * _softplus(x + dt)),)

    return rowwise(name, f, (ba,), (place(a_log), place(dt_bias)))[0]


def _dn_out(o, z, g):
    parts = []
    for h in range(DN_HEADS):
        sl = slice(h * DN_HEAD_DIM, (h + 1) * DN_HEAD_DIM)
        parts.append(_rms(o[:, sl], g[:, sl]) * _silu(z[:, sl]))
    return (jnp.concatenate(parts, axis=1),)


def _merge(ml, za, zb, zc):
    d = D_MODEL
    return (_sigmoid(ml[:, :d]) * za + _sigmoid(ml[:, d:2 * d]) * zb + _sigmoid(ml[:, 2 * d:]) * zc,)


def add_norm(name, x, pend, scale, gain):
    if pend is None:
        return x, rowwise(name, lambda a, g: (_rms(a, g),), (x,), (gain,))[0]

    def f(a, b, g):
        s = a + scale * b
        return s, _rms(s, g)

    return rowwise(name, f, (x, pend), (gain,))


W_IN_PIECES = (("rgx", 0, 1024), ("gate", 1024, 1024), ("att", 2048, 2304), ("dq", 4352, 1024), ("dk", 5376, 1024),
               ("dv", 6400, 1024), ("dz", 7424, 1024), ("ba", 8448, 16), ("mrg", 8464, 3072))
RG_PAR_BLOCKS = ("lane", "lane", "blk", "lane", "blk", "lane", "lane")


def mixer(name, u, w, p):
    mm = lambda nm, a, wt: mm_rows(nm, a[None], wt[None])
    pr = project_in(name + "_in", u, {k: w["in_" + k] for k, _, _ in W_IN_PIECES})
    ya = colwise(name + "_rg", _rg_block, (pr["rgx"], pr["gate"]),
                 (w["rg_conv_w"], p["rg_conv_b"], _block_diag(p["rg_w_r"]), p["rg_b_r"], _block_diag(p["rg_w_i"]),
                  p["rg_b_i"], p["rg_lambda"]), RG_PAR_BLOCKS, 1)[0]
    yb = att_branch(name, pr["att"], p["att_q_norm"], p["att_k_norm"])
    cw = w["dn_conv_w"]
    cq = colwise(name + "_dnq", _dn_conv_block("q"), (pr["dq"],), (cw[:, :1024],), ("lane",), 1)[0]
    ck = colwise(name + "_dnk", _dn_conv_block("k"), (pr["dk"],), (cw[:, 1024:2048],), ("lane",), 1)[0]
    cv = colwise(name + "_dnv", _dn_conv_block("v"), (pr["dv"],), (cw[:, 2048:],), ("lane",), 1)[0]
    gb = dn_gates(name + "_dngate", pr["ba"], p["dn_a_log"], p["dn_dt_bias"])
    o_dn = dn_core(cq, ck, cv, gb)
    yc = rowwise(name + "_dnout", _dn_out, (o_dn, pr["dz"]), (p["dn_out_norm"].reshape(1, D_MODEL),))[0]
    y = rowwise(name + "_merge", _merge, (pr["mrg"], mm(name + "_ba", ya, w["br_a"]), mm(name + "_bb", yb, w["br_b"]),
                                          mm(name + "_bc", yc, w["br_c"])))[0]
    return mm(name + "_out", y, w["w_out"])


def _loss_call(x, pend, target):
    t, d = x.shape
    tile = _row_tile(t)

    def body(x_ref, p_ref, t_ref, loss_ref, g_ref):
        err = x_ref[...] + 0.5 * p_ref[...] - t_ref[...]
        g_ref[...] = err * (1.0 / d)

        @pl.when(pl.program_id(0) == 0)
        def _():
            loss_ref[...] = jnp.zeros_like(loss_ref)

        loss_ref[...] += jnp.full(loss_ref.shape, 0.5 / d, F32) * jnp.sum(err * err)

    blk = pl.BlockSpec((tile, d), lambda i: (i, 0))
    loss, g = pl.pallas_call(
        body, name="loss", grid=(t // tile,), in_specs=[blk, blk, blk],
        out_specs=[pl.BlockSpec((8, LANES), lambda i: (0, 0)), blk],
        out_shape=[jax.ShapeDtypeStruct((8, LANES), F32), jax.ShapeDtypeStruct((t, d), F32)],
        compiler_params=_params("arbitrary"),
    )(x, pend, target)
    return loss[0, 0], g


@jax.custom_vjp
def loss_op(x, pend, target):
    return _loss_call(x, pend, target)[0]


def _loss_fwd(x, pend, target):
    loss, g = _loss_call(x, pend, target)
    return loss, g


def _loss_bwd(g, ct):
    return ct * g, (0.5 * ct) * g, None


loss_op.defvjp(_loss_fwd, _loss_bwd)


def first_ffn(wg, wu, wd, gain, x):
    x, h = add_norm("L0_n1", x, None, 0.0, gain)
    return x, ffn("L0_f1", h, wg, wu, wd)


def rest_of_step(g, conv, p, x, pend, target):
    scale = 0.5
    w = [split_layer({n: g[n, l] for n, _ in MATRICES if (n, l) in g}, {n: conv[n][l] for n, _ in CONVS}) for l in range(len(p))]
    for l in range(len(p)):
        n = f"L{l}"
        if l > 0:
            x, h = add_norm(n + "_n1", x, pend, scale, p[l]["ffn1_norm"])
            pend, scale = ffn(n + "_f1", h, w[l]["ffn1_w_gate"], w[l]["ffn1_w_up"], w[l]["ffn1_w_down"]), 0.5
        x, h = add_norm(n + "_nm", x, pend, scale, p[l]["mix_norm"])
        pend, scale = mixer(n + "_mx", h, w[l], p[l]), 1.0
        x, h = add_norm(n + "_n2", x, pend, scale, p[l]["ffn2_norm"])
        pend, scale = ffn(n + "_f2", h, w[l]["ffn2_w_gate"], w[l]["ffn2_w_up"], w[l]["ffn2_w_down"]), 0.5
    return loss_op(x, pend, target)


WEIGHT_NAMES = ("ffn1_norm", "ffn1_w_gate", "ffn1_w_up", "ffn1_w_down", "mix_norm", "w_in", "rg_conv_w", "rg_conv_b",
                "rg_w_r", "rg_b_r", "rg_w_i", "rg_b_i", "rg_lambda", "att_q_norm", "att_k_norm", "dn_conv_w", "dn_a_log",
                "dn_dt_bias", "dn_out_norm", "w_branch", "w_out", "ffn2_norm", "ffn2_w_gate", "ffn2_w_up", "ffn2_w_down")
MATRICES = (("ffn1_w_gate", 2), ("ffn1_w_up", 2), ("ffn1_w_down", 1), ("w_in", 2), ("w_branch", 1), ("w_out", 1),
            ("ffn2_w_gate", 2), ("ffn2_w_up", 2), ("ffn2_w_down", 1))
CONVS = (("rg_conv_w", 2), ("dn_conv_w", 2))
SHARD_AXIS = dict(MATRICES + CONVS)
SMALL_NAMES = tuple(n for n in WEIGHT_NAMES if n not in SHARD_AXIS)
ROW_PARAMS = ("ffn1_norm", "mix_norm", "rg_conv_b", "rg_b_r", "rg_b_i", "rg_lambda", "ffn2_norm")
FFN_MATS = ("ffn1_w_gate", "ffn1_w_up", "ffn1_w_down", "ffn2_w_gate", "ffn2_w_up", "ffn2_w_down")
TRANSPOSED_MATS = ("ffn1_w_gate", "ffn1_w_up", "ffn2_w_gate", "ffn2_w_up")
W_IN_SHARD = 2884
GATHER_ORDER = ((("ffn1_w_gate", 0), ("ffn1_w_up", 0), ("ffn1_w_down", 0)),
                (("w_in", 0), ("w_branch", 0), ("w_out", 0)),
                None)
GATHER_IDS = (1, 6, 7)
LATE_MATS = ("ffn2_w_gate", "ffn2_w_up", "ffn2_w_down", "w_out", "w_branch")
EXCHANGE_GROUPS = (lambda n, l: l == 1 and n in LATE_MATS,
                   lambda n, l: (l == 1) != (n in LATE_MATS),
                   lambda n, l: l == 0 and n == "w_in",
                   lambda n, l: l == 0 and n not in LATE_MATS and n != "w_in")


def _shard_minor(a, axis):
    a = jnp.moveaxis(a, 0, axis)
    return a.reshape(a.shape[:axis] + (N_CHIPS * a.shape[axis + 1],) + a.shape[axis + 2:])


def _w_in_piece(g, off, n):
    s = W_IN_SHARD
    parts = [g[j][:, max(off, j * s) - j * s:min(off + n, (j + 1) * s) - j * s]
             for j in range(N_CHIPS) if max(off, j * s) < min(off + n, (j + 1) * s)]
    return jnp.concatenate(parts, axis=1) if len(parts) > 1 else parts[0]


def _w_in_chip_grad(gl, j):
    s = W_IN_SHARD
    parts = [gl["in_" + k][:, max(off, j * s) - off:min(off + n, (j + 1) * s) - off]
             for k, off, n in W_IN_PIECES if max(off, j * s) < min(off + n, (j + 1) * s)]
    return jnp.concatenate(parts, axis=1)


def _layer_weights(g, conv):
    w = {n: g[n] for n in FFN_MATS if n in g}
    w["w_out"] = g["w_out"].reshape(D_MODEL, D_MODEL)
    for k, off, n in W_IN_PIECES:
        piece = _w_in_piece(g["w_in"], off, n)
        w["in_" + k] = jnp.pad(piece, ((0, 0), (0, LANES - n))) if n < LANES else piece
    wb = g["w_branch"].reshape(-1, D_MODEL)
    w["br_a"], w["br_b"], w["br_c"] = wb[:1024], wb[1024:1792], wb[1792:]
    return dict(w, **conv)


def _layer_weight_grads(gl):
    out = {n: gl[n] for n in FFN_MATS if n in gl}
    out["w_out"] = gl["w_out"].reshape(N_CHIPS, -1, D_MODEL)
    out["w_branch"] = jnp.concatenate([gl["br_a"], gl["br_b"], gl["br_c"]], axis=0).reshape(N_CHIPS, -1, D_MODEL)
    out["w_in"] = jnp.stack([_w_in_chip_grad(gl, j) for j in range(N_CHIPS)])
    return out, {n: gl[n] for n, _ in CONVS}


@jax.custom_vjp
def split_layer(g, conv):
    return _layer_weights(g, conv)


split_layer.defvjp(lambda g, conv: (_layer_weights(g, conv), None), lambda _, gw: _layer_weight_grads(gw))


def layer_small(small, l):
    p = {n: small[n][l] for n in SMALL_NAMES}
    for n in ROW_PARAMS:
        p[n] = small[n][l:l + 1]
    return p


def layer_small_grads(gp, small):
    return {n: jnp.stack([g[n] for g in gp]).reshape(small[n].shape) for n in SMALL_NAMES}


HBM_SPEC = pl.BlockSpec(memory_space=pl.ANY)


def _place():
    x, y, c = lax.axis_index("x"), lax.axis_index("y"), lax.axis_index("c")
    other_chips = [(1 - x, y), (x, 1 - y), (1 - x, 1 - y)]
    return x, y, c, 2 * x + y, (x, y, 1 - c), other_chips


def _half_rows(ref, lead, hc):
    hr = ref.shape[-2] // 2
    return ref.at[(*lead, pl.ds(pl.multiple_of(hc * hr, 16), hr), slice(None))]


def _chip_index():
    return (2 * lax.axis_index("x") + lax.axis_index("y")).astype(jnp.int32).reshape(1)


def cast_into_blocks(name, w):
    l, rows, cols = w.shape
    tr = rows // 2

    def body(me_ref, w_ref, *o_refs):
        for a, o_ref in enumerate(o_refs):
            o_ref[...] = w_ref[a:a + 1].astype(BF16)

    return pl.pallas_call(
        body, name=name, out_shape=[jax.ShapeDtypeStruct((N_CHIPS, rows, cols), BF16)] * l,
        grid_spec=pltpu.PrefetchScalarGridSpec(
            num_scalar_prefetch=1, grid=(rows // tr,),
            in_specs=[pl.BlockSpec((l, tr, cols), lambda i, me: (0, i, 0))],
            out_specs=[pl.BlockSpec((1, tr, cols), lambda i, me: (me[0], i, 0))] * l),
        compiler_params=_params("parallel"),
    )(_chip_index(), w)


def _gather_blocks(bufs_in, bufs_out, send_sems, recv_sems):
    n = len(bufs_in)
    x, y, c, me, sibling, chips = _place()

    def copy(s, src, dst, to):
        return pltpu.make_async_remote_copy(src_ref=src, dst_ref=dst, send_sem=send_sems.at[s], recv_sem=recv_sems.at[s],
                                            device_id=to, device_id_type=MESH)

    first, passed = [], []
    for j, (cx, cy) in enumerate(chips):
        for i in range(n):
            cp = copy(6 * i + j, _half_rows(bufs_in[i], (me,), c), _half_rows(bufs_out[i], (me,), c), (cx, cy, c))
            cp.start()
            first.append(cp)
    for j, (cx, cy) in enumerate(chips):
        k = 2 * cx + cy
        for i in range(n):
            copy(6 * i + j, _half_rows(bufs_in[i], (me,), c), _half_rows(bufs_out[i], (k,), c), (cx, cy, c)).wait_recv()
            cp = copy(6 * i + 3 + j, _half_rows(bufs_out[i], (k,), c), _half_rows(bufs_out[i], (k,), c), sibling)
            cp.start()
            passed.append(cp)
    for j, (cx, cy) in enumerate(chips):
        k = 2 * cx + cy
        for i in range(n):
            copy(6 * i + 3 + j, _half_rows(bufs_in[i], (me,), c), _half_rows(bufs_out[i], (k,), 1 - c), sibling).wait_recv()
    for cp in first + passed:
        cp.wait_send()


def _handshake(peers):
    barrier = pltpu.get_barrier_semaphore()
    for p in peers:
        pl.semaphore_signal(barrier, inc=1, device_id=p, device_id_type=MESH)
    pl.semaphore_wait(barrier, len(peers))


def allgather_blocks_sc(name, bufs, collective_id):
    n = len(bufs)
    refs = [jax.new_ref(b, memory_space=pltpu.MemorySpace.HBM) for b in bufs]

    @pl.kernel(mesh=plsc.ScalarSubcoreMesh(axis_name="sequencer", num_cores=1), name=name,
               scratch_types=(pltpu.SemaphoreType.DMA((6 * n,)), pltpu.SemaphoreType.DMA((6 * n,))),
               compiler_params=pltpu.CompilerParams(collective_id=collective_id))
    def launch(send_sems, recv_sems):
        x, y, c, me, sibling, chips = _place()
        _handshake([(cx, cy, c) for cx, cy in chips] + [sibling])
        _gather_blocks(refs, refs, send_sems, recv_sems)

    launch()
    return [jax.freeze(r) for r in refs]


PEER_FLIPS = tuple((fx, fy, fc) for fx in (0, 1) for fy in (0, 1) for fc in (0, 1))[1:]


def exchange_pieces_sc(name, gs, collective_id):
    n = len(gs)

    def body(*refs):
        ins, outs = refs[:n], refs[n:2 * n]
        send_sems, recv_sems = refs[2 * n:]
        x, y, c, me, sibling, chips = _place()
        my_dev = 4 * x + 2 * y + c
        flip = lambda v, f: 1 - v if f else v
        peers = [(flip(x, fx), flip(y, fy), flip(c, fc)) for fx, fy, fc in PEER_FLIPS]
        _handshake(peers)
        sends = []
        for r, (px, py, pc) in enumerate(peers):
            for i in range(n):
                cp = pltpu.make_async_remote_copy(
                    src_ref=_half_rows(ins[i], (2 * px + py,), pc), dst_ref=outs[i].at[my_dev], send_sem=send_sems.at[7 * i + r],
                    recv_sem=recv_sems.at[7 * i + r], device_id=(px, py, pc), device_id_type=MESH)
                cp.start()
                sends.append(cp)
        for r, (px, py, pc) in enumerate(peers):
            for i in range(n):
                pltpu.make_async_remote_copy(
                    src_ref=_half_rows(ins[i], (me,), c), dst_ref=outs[i].at[4 * px + 2 * py + pc], send_sem=send_sems.at[7 * i + r],
                    recv_sem=recv_sems.at[7 * i + r], device_id=(px, py, pc), device_id_type=MESH).wait_recv()
        for cp in sends:
            cp.wait_send()

    return pl.kernel(
        body, name=name, mesh=plsc.ScalarSubcoreMesh(axis_name="sequencer", num_cores=1),
        out_type=[jax.ShapeDtypeStruct((N_DEV, g.shape[1] // 2, g.shape[2]), g.dtype) for g in gs],
        scratch_types=[pltpu.SemaphoreType.DMA((7 * n,)), pltpu.SemaphoreType.DMA((7 * n,))],
        compiler_params=pltpu.CompilerParams(collective_id=collective_id),
    )(*gs)


def sibling_share_halves(name, fs):
    n = len(fs)
    every = (slice(None),)

    def body(*refs):
        ins, outs = refs[:n], refs[n:2 * n]
        send_sems, recv_sems = refs[2 * n:]
        x, y, c, me, sibling, chips = _place()
        sends = []
        for i in range(n):
            cp = pltpu.make_async_remote_copy(src_ref=_half_rows(ins[i], every, c), dst_ref=_half_rows(outs[i], every, c),
                                              send_sem=send_sems.at[i], recv_sem=recv_sems.at[i], device_id=sibling, device_id_type=MESH)
            cp.start()
            sends.append(cp)
        for i in range(n):
            pltpu.make_async_remote_copy(src_ref=_half_rows(ins[i], every, c), dst_ref=_half_rows(outs[i], every, 1 - c),
                                         send_sem=send_sems.at[i], recv_sem=recv_sems.at[i], device_id=sibling,
                                         device_id_type=MESH).wait_recv()
        for cp in sends:
            cp.wait_send()

    return pl.pallas_call(
        body, name=name, out_shape=[jax.ShapeDtypeStruct(f.shape, f.dtype) for f in fs],
        in_specs=[HBM_SPEC] * n, out_specs=[HBM_SPEC] * n, input_output_aliases={i: i for i in range(n)},
        scratch_shapes=[pltpu.SemaphoreType.DMA((n,)), pltpu.SemaphoreType.DMA((n,))],
    )(*fs)


def allgather_small_sc(name, v, collective_id):
    def body(v_ref, out_ref, send_sems, recv_sems, local_sem):
        x, y, c, me, sibling, chips = _place()
        my_dev = 4 * x + 2 * y + c
        flip = lambda a, f: 1 - a if f else a
        peers = [(flip(x, fx), flip(y, fy), flip(c, fc)) for fx, fy, fc in PEER_FLIPS]
        _handshake(peers)
        mine = pltpu.make_async_copy(v_ref, out_ref.at[my_dev], local_sem)
        mine.start()
        sends = []
        for r, peer in enumerate(peers):
            cp = pltpu.make_async_remote_copy(src_ref=v_ref, dst_ref=out_ref.at[my_dev], send_sem=send_sems.at[r],
                                              recv_sem=recv_sems.at[r], device_id=peer, device_id_type=MESH)
            cp.start()
            sends.append(cp)
        for r, (px, py, pc) in enumerate(peers):
            pltpu.make_async_remote_copy(src_ref=v_ref, dst_ref=out_ref.at[4 * px + 2 * py + pc], send_sem=send_sems.at[r],
                                         recv_sem=recv_sems.at[r], device_id=(px, py, pc), device_id_type=MESH).wait_recv()
        for cp in sends:
            cp.wait_send()
        mine.wait()

    return pl.kernel(
        body, name=name, mesh=plsc.ScalarSubcoreMesh(axis_name="sequencer", num_cores=1),
        out_type=jax.ShapeDtypeStruct((N_DEV,) + v.shape, v.dtype),
        scratch_types=[pltpu.SemaphoreType.DMA((7,)), pltpu.SemaphoreType.DMA((7,)), pltpu.SemaphoreType.DMA],
        compiler_params=pltpu.CompilerParams(collective_id=collective_id),
    )(v)


SUM_BLOCK_ELEMS = 512 * 1024


def sum_slabs(name, b):
    k, h, w = b.shape

    def body(b_ref, o_ref):
        acc = b_ref[0].astype(F32)
        for i in range(1, k):
            acc = acc + b_ref[i].astype(F32)
        o_ref[...] = acc

    return pl.pallas_call(
        body, name=name, out_shape=jax.ShapeDtypeStruct((h, w), F32),
        in_specs=[pl.BlockSpec(memory_space=pltpu.VMEM)], out_specs=pl.BlockSpec(memory_space=pltpu.VMEM),
        compiler_params=pltpu.CompilerParams(vmem_limit_bytes=VMEM_LIMIT),
    )(b)


def sum_pieces(name, pieces, gs):
    nl = len(pieces)
    k, h, w = pieces[0].shape
    tile = max(t for t in range(16, h + 1, 16) if h % t == 0 and (t * w <= SUM_BLOCK_ELEMS or t == 16))
    nt = h // tile
    x, y, c = lax.axis_index("x"), lax.axis_index("y"), lax.axis_index("c")
    place = [v.astype(jnp.int32).reshape(1) for v in (c, 2 * x + y, 4 * x + 2 * y + c)]

    assert nl == 2

    def tile_of(l, a, i):
        return i * a if l else i * (1 - a) + (nt - 1) * a

    def body(c_ref, me_ref, dev_ref, *refs):
        p_refs, g_refs, o_ref = refs[:nl], refs[nl:2 * nl], refs[2 * nl]
        my_dev = dev_ref[0]
        for l in range(nl):
            @pl.when(pl.program_id(0) == l)
            def _():
                o_ref[0] = jnp.zeros(o_ref.shape[1:], F32)
                for d in range(k):
                    @pl.when(my_dev == d)
                    def _():
                        o_ref[0] += g_refs[l][0].astype(F32)

                    @pl.when(my_dev != d)
                    def _():
                        o_ref[0] += p_refs[l][d].astype(F32)

    in_specs = [pl.BlockSpec((k, tile, w), functools.partial(lambda l, a, i, cc, me, dev: (0, tile_of(l, a, i), 0), l))
                for l in range(nl)]
    in_specs += [pl.BlockSpec((1, tile, w), functools.partial(lambda l, a, i, cc, me, dev: (me[0], cc[0] * nt + tile_of(l, a, i), 0), l))
                 for l in range(nl)]
    return pl.pallas_call(
        body, name=name, out_shape=jax.ShapeDtypeStruct((nl, 2 * h, w), F32),
        grid_spec=pltpu.PrefetchScalarGridSpec(
            num_scalar_prefetch=3, grid=(nl, nt), in_specs=in_specs,
            out_specs=pl.BlockSpec((1, tile, w), lambda a, i, cc, me, dev: (a, cc[0] * nt + i, 0))),
        compiler_params=_params("arbitrary", "arbitrary"),
    )(*place, *pieces, *gs)


def _adam_block(w, g, m, v):
    m = ADAM_B1 * m + (1.0 - ADAM_B1) * g
    v = ADAM_B2 * v + (1.0 - ADAM_B2) * (g * g)
    m_hat = m / (1.0 - ADAM_B1 ** ADAM_STEP)
    v_hat = v / (1.0 - ADAM_B2 ** ADAM_STEP)
    return -ADAM_LR * (m_hat / (jnp.sqrt(v_hat) + ADAM_EPS) + ADAM_WD * w), m, v


def adamw(name, w, g, m, v):
    shape = w.shape
    cols = shape[-1]
    rows = w.size // cols
    tile = 128 if rows % 128 == 0 else rows
    flat = [a.reshape(rows, cols) for a in (w, g, m, v)]

    def body(w_ref, g_ref, m_ref, v_ref, d_ref, nm_ref, nv_ref):
        d_ref[...], nm_ref[...], nv_ref[...] = _adam_block(w_ref[...], g_ref[...], m_ref[...], v_ref[...])

    blk = pl.BlockSpec((tile, cols), lambda i: (i, 0))
    res = pl.pallas_call(
        body, name=name, grid=(rows // tile,), in_specs=[blk] * 4, out_specs=[blk] * 3,
        out_shape=[jax.ShapeDtypeStruct((rows, cols), F32)] * 3, compiler_params=_params("parallel"),
    )(*flat)
    return tuple(r.reshape(shape) for r in res)


def _pack_small(values):
    flat = jnp.concatenate([v.reshape(-1) for v in values.values()])
    n = flat.shape[0]
    total = -(-n // (8 * LANES)) * (8 * LANES)
    return jnp.pad(flat, (0, total - n)).reshape(-1, LANES)


def _unpack_small(v, shapes):
    flat = v.reshape(-1)
    out, off = {}, 0
    for n, shape in shapes.items():
        sz = int(np.prod(shape))
        out[n] = flat[off:off + sz].reshape(shape)
        off += sz
    return out


def kernel(x, ffn1_norm, ffn1_w_gate, ffn1_w_up, ffn1_w_down, mix_norm, w_in, rg_conv_w, rg_conv_b, rg_w_r, rg_b_r, rg_w_i, rg_b_i, rg_lambda, att_q_norm, att_k_norm, dn_conv_w, dn_a_log, dn_dt_bias, dn_out_norm, w_branch, w_out, ffn2_norm, ffn2_w_gate, ffn2_w_up, ffn2_w_down, loss_target, m_ffn1_norm, m_ffn1_w_gate, m_ffn1_w_up, m_ffn1_w_down, m_mix_norm, m_w_in, m_rg_conv_w, m_rg_conv_b, m_rg_w_r, m_rg_b_r, m_rg_w_i, m_rg_b_i, m_rg_lambda, m_att_q_norm, m_att_k_norm, m_dn_conv_w, m_dn_a_log, m_dn_dt_bias, m_dn_out_norm, m_w_branch, m_w_out, m_ffn2_norm, m_ffn2_w_gate, m_ffn2_w_up, m_ffn2_w_down, v_ffn1_norm, v_ffn1_w_gate, v_ffn1_w_up, v_ffn1_w_down, v_mix_norm, v_w_in, v_rg_conv_w, v_rg_conv_b, v_rg_w_r, v_rg_b_r, v_rg_w_i, v_rg_b_i, v_rg_lambda, v_att_q_norm, v_att_k_norm, v_dn_conv_w, v_dn_a_log, v_dn_dt_bias, v_dn_out_norm, v_w_branch, v_w_out, v_ffn2_norm, v_ffn2_w_gate, v_ffn2_w_up, v_ffn2_w_down):
    given = dict(locals())
    for n in TRANSPOSED_MATS:
        for pre in ("", "m_", "v_"):
            given[pre + n] = jnp.swapaxes(given[pre + n], 1, 2)
    small = {n: given[n] for n in SMALL_NAMES}
    n_layers = ffn1_norm.shape[0]
    mat_names = [n for n, _ in MATRICES]
    conv_names = [n for n, _ in CONVS]

    blocks = {}
    for n in mat_names:
        for l, b in enumerate(cast_into_blocks("cast_" + n, given[n])):
            blocks[n, l] = b
    first, done = {}, []
    for i, wanted in enumerate(GATHER_ORDER[:-1]):
        bufs, _ = lax.optimization_barrier(([blocks[k] for k in wanted], done))
        done = allgather_blocks_sc(f"allgather_{i}", bufs, GATHER_IDS[i])
        first.update(zip(wanted, done))
    rest = {k: b for k, b in blocks.items() if k not in first}
    taps = jnp.concatenate([given[n].reshape(-1) for n in conv_names]).reshape(-1, LANES)
    taps = allgather_small_sc("allgather_taps", taps, 8).reshape(N_CHIPS, 2, -1)[:, 0]
    conv, off = {}, 0
    for n, ax in CONVS:
        sz = given[n].size
        conv[n] = _shard_minor(taps[:, off:off + sz].reshape((N_CHIPS,) + given[n].shape), ax)
        off += sz
    p = [layer_small(small, l) for l in range(n_layers)]

    ffn1_keys = GATHER_ORDER[0]
    (x1, pend), first_vjp = jax.vjp(first_ffn, *[first[k] for k in ffn1_keys], p[0]["ffn1_norm"], x[0])
    keys = list(rest)
    bufs, pend, second = lax.optimization_barrier(([rest[k] for k in keys], pend, [first[k] for k in GATHER_ORDER[1]]))
    gathered = dict(zip(keys, allgather_blocks_sc("allgather_2", bufs, GATHER_IDS[2])))
    gathered.update(zip(GATHER_ORDER[1], second))
    loss, (g_mats, g_conv, gp, gx1, gpend) = jax.value_and_grad(rest_of_step, argnums=(0, 1, 2, 3, 4))(
        gathered, conv, p, x1, pend, loss_target[0])
    *g_ffn1, gp[0]["ffn1_norm"], gx = first_vjp((gx1, gpend))
    g_mats.update(zip(ffn1_keys, g_ffn1))

    pieces = {}
    for i, group in enumerate(EXCHANGE_GROUPS):
        keys = [k for k in g_mats if group(*k)]
        pieces.update(zip(keys, exchange_pieces_sc(f"exchange_{i}", [g_mats[k] for k in keys], 2 + i)))
    halves = {n: sum_pieces("sum_" + n, [pieces[n, l] for l in range(n_layers)], [g_mats[n, l] for l in range(n_layers)])
              for n in mat_names}
    grads = {}
    for tag, names in (("late", [n for n in mat_names if n in LATE_MATS]), ("early", [n for n in mat_names if n not in LATE_MATS])):
        grads.update(zip(names, sibling_share_halves("share_" + tag, [halves[n] for n in names])))

    g_small = dict(layer_small_grads(gp, small), **g_conv, loss=loss.reshape(1))
    packed_small = _pack_small(g_small)
    slabs = allgather_small_sc("allgather_small", packed_small, 9)
    summed =_unpack_small(sum_slabs("sum_small", slabs), {n: g.shape for n, g in g_small.items()})
    chip = 2 * lax.axis_index("x") + lax.axis_index("y")
    for n in SMALL_NAMES:
        grads[n] = summed[n]
    for n, ax in CONVS:
        s = given[n].shape[ax]
        grads[n] = lax.dynamic_slice_in_dim(summed[n], chip * s, s, axis=ax)

    upd = {n: adamw("adamw_" + n, given[n], grads[n], given["m_" + n], given["v_" + n]) for n in WEIGHT_NAMES}
    out = lambda n, a: jnp.swapaxes(a, 1, 2) if n in TRANSPOSED_MATS else a
    return (summed["loss"][0], gx[None], *[out(n, grads[n]) for n in WEIGHT_NAMES], *[out(n, upd[n][0]) for n in WEIGHT_NAMES],
            *[out(n, upd[n][1]) for n in WEIGHT_NAMES], *[out(n, upd[n][2]) for n in WEIGHT_NAMES])
```

```python
import functools
import math

import jax
import jax.numpy as jnp
import numpy as np
from jax import lax
from jax.experimental import pallas as pl
from jax.experimental.pallas import tpu as pltpu
from jax.experimental.pallas import tpu_sc as plsc

F32 = jnp.float32
BF16 = jnp.bfloat16
MESH = pl.DeviceIdType.MESH

D_MODEL = 1024
FFN_DIM = 2816
RG_C = 8.0
ATT_GROUPS = ((128, 1), (512, 4), (2048, 16))
ATT_HEADS = 12
ATT_HEAD_DIM = 64
ATT_SPAN = 128
DN_HEADS = 8
DN_HEAD_DIM = 128
DN_CHUNK = 64
EPS = 1e-6
NEG_INF = -1e30
N_CHIPS = 4
N_DEV = 8

ADAM_LR, ADAM_B1, ADAM_B2, ADAM_EPS, ADAM_WD, ADAM_STEP = 0.001, 0.9, 0.999, 1e-08, 0.01, 10

LANES = 128
VMEM_LIMIT = 56 * 1024 * 1024


def _params(*sem):
    return pltpu.CompilerParams(dimension_semantics=sem or None, vmem_limit_bytes=VMEM_LIMIT)


def _sigmoid(x):
    return 1.0 / (1.0 + jnp.exp(-x))


def _silu(x):
    return x * _sigmoid(x)


def _softplus(x):
    return jnp.maximum(x, 0.0) + jnp.log(1.0 + jnp.exp(-jnp.abs(x)))


def _gelu(x):
    return 0.5 * x * (1.0 + jnp.tanh(math.sqrt(2.0 / math.pi) * (x + 0.044715 * (x * x * x))))


def _neg_expm1(x):
    series = -x * (1.0 + x * (0.5 + x * (1.0 / 6 + x * (1.0 / 24 + x * (1.0 / 120 + x * (1.0 / 720))))))
    return jnp.where(x > -0.25, series, 1.0 - jnp.exp(x))


def _rms(x, g):
    return x * lax.rsqrt(jnp.mean(x * x, axis=-1, keepdims=True) + EPS) * g


_MM_DIMS = {"nn": (((1,), (0,)), ((), ())), "nt": (((1,), (1,)), ((), ())), "tn": (((0,), (0,)), ((), ()))}


def _split(a):
    hi = a.astype(BF16)
    return hi, (a - hi.astype(F32)).astype(BF16)


def _mxu(a, b, form, passes):
    (ca, cb), _ = _MM_DIMS[form]
    if a.ndim == 3:
        dims = (((ca[0] + 1,), (cb[0] + 1,)), ((0,), (0,)))
    else:
        dims = _MM_DIMS[form]
    dg = lambda p, q: lax.dot_general(p, q, dims, preferred_element_type=F32)
    if passes == 1:
        return dg(a.astype(BF16), b.astype(BF16))
    (a_hi, a_lo), (b_hi, b_lo) = _split(a), _split(b)
    return dg(a_hi, b_hi) + (dg(a_hi, b_lo) + dg(a_lo, b_hi))


@functools.partial(jax.custom_vjp, nondiff_argnums=(2, 3))
def _mm(a, b, form, passes):
    return _mxu(a, b, form, passes)


def _mm_fwd(a, b, form, passes):
    return _mxu(a, b, form, passes), (a, b)


def _mm_bwd(form, passes, res, g):
    a, b = res
    if form == "nn":
        return _mm(g, b, "nt", passes), _mm(a, g, "tn", passes)
    if form == "nt":
        return _mm(g, b, "nn", passes), _mm(g, a, "tn", passes)
    return _mm(b, g, "nt", passes), _mm(a, g, "nn", passes)


_mm.defvjp(_mm_fwd, _mm_bwd)


def _dot(a, b):
    return _mm(a, b, "nn", 1)


def _dot_nt(a, b):
    return _mm(a, b, "nt", 1)


def _dot_tn(a, b):
    return _mm(a, b, "tn", 1)


def _dot3(a, b):
    return _mm(a, b, "nn", 3)


def _rows(shape):
    return lax.broadcasted_iota(jnp.int32, shape, len(shape) - 2)


def _roll_down(x, s, fill):
    return jnp.where(_rows(x.shape) >= s, pltpu.roll(x, s, x.ndim - 2), fill)


def _roll_up(x, s, fill):
    n = x.shape[-2]
    return jnp.where(_rows(x.shape) < n - s, pltpu.roll(x, n - s, x.ndim - 2), fill)


@functools.partial(jax.custom_vjp, nondiff_argnums=(1,))
def _shift(x, s):
    return _roll_down(x, s, 0.0)


def _shift_fwd(x, s):
    return _roll_down(x, s, 0.0), None


def _shift_bwd(s, _, g):
    return (_roll_up(g, s, 0.0),)


_shift.defvjp(_shift_fwd, _shift_bwd)


def _causal_conv(x, w):
    return w[0:1] * _shift(x, 3) + w[1:2] * _shift(x, 2) + w[2:3] * _shift(x, 1) + w[3:4] * x


@jax.custom_vjp
def _lin_scan(a, b):
    return _lin_scan_fwd(a, b)[0]


def _lin_scan_fwd(a, b):
    a0 = a
    s = 1
    while s < a.shape[0]:
        b = a * _roll_down(b, s, 0.0) + b
        a = a * _roll_down(a, s, 1.0)
        s *= 2
    return b, (a0, b)


def _lin_scan_bwd(res, g):
    a, h = res
    c = _roll_up(a, 1, 0.0)
    s = 1
    while s < a.shape[0]:
        g = c * _roll_up(g, s, 0.0) + g
        c = c * _roll_up(c, s, 1.0)
        s *= 2
    return g * _roll_down(h, 1, 0.0), g


_lin_scan.defvjp(_lin_scan_fwd, _lin_scan_bwd)


@jax.custom_vjp
def _cumsum_rows(x):
    s = 1
    while s < x.shape[-2]:
        x = x + _roll_down(x, s, 0.0)
        s *= 2
    return x


def _cumsum_rows_fwd(x):
    return _cumsum_rows(x), None


def _cumsum_rows_bwd(_, g):
    s = 1
    while s < g.shape[-2]:
        g = g + _roll_up(g, s, 0.0)
        s *= 2
    return (g,)


_cumsum_rows.defvjp(_cumsum_rows_fwd, _cumsum_rows_bwd)


ROW_BLOCK_BYTES = 14 * 1024 * 1024


def _row_tile(t, width=0):
    for tile in (512, 256):
        if t % tile == 0 and (tile == 256 or tile * width * 4 <= ROW_BLOCK_BYTES):
            return tile
    return t


def _rowwise_fwd_call(name, f, rows, pars, tile):
    t = rows[0].shape[0]
    outs = jax.eval_shape(f, *[jax.ShapeDtypeStruct((tile, r.shape[1]), F32) for r in rows],
                          *[jax.ShapeDtypeStruct(p.shape, F32) for p in pars])
    nr, npar = len(rows), len(pars)

    def body(*refs):
        ins = [r[...] for r in refs[:nr + npar]]
        res = f(*ins)
        for o_ref, o in zip(refs[nr + npar:], res):
            o_ref[...] = o.astype(o_ref.dtype)

    return pl.pallas_call(
        body, name=name, grid=(t // tile,),
        in_specs=[pl.BlockSpec((tile, r.shape[1]), lambda i: (i, 0)) for r in rows]
        + [pl.BlockSpec(p.shape, lambda i: (0, 0)) for p in pars],
        out_specs=[pl.BlockSpec((tile, o.shape[1]), lambda i: (i, 0)) for o in outs],
        out_shape=[jax.ShapeDtypeStruct((t, o.shape[1]), F32) for o in outs],
        compiler_params=_params("parallel"),
    )(*rows, *pars)


def _rowwise_bwd_call(name, f, rows, pars, cts, tile):
    t = rows[0].shape[0]
    nr, npar, nct = len(rows), len(pars), len(cts)

    def body(*refs):
        ins = [r[...] for r in refs[:nr + npar]]
        gs = tuple(r[...] for r in refs[nr + npar:nr + npar + nct])
        outs = refs[nr + npar + nct:]
        _, vjp = jax.vjp(f, *ins)
        d = vjp(gs)
        for o_ref, v in zip(outs[:nr], d[:nr]):
            o_ref[...] = v

        @pl.when(pl.program_id(0) == 0)
        def _():
            for o_ref in outs[nr:]:
                o_ref[...] = jnp.zeros_like(o_ref)

        for o_ref, v in zip(outs[nr:], d[nr:]):
            o_ref[...] += v

    res = pl.pallas_call(
        body, name=name, grid=(t // tile,),
        in_specs=[pl.BlockSpec((tile, r.shape[1]), lambda i: (i, 0)) for r in rows]
        + [pl.BlockSpec(p.shape, lambda i: (0, 0)) for p in pars]
        + [pl.BlockSpec((tile, c.shape[1]), lambda i: (i, 0)) for c in cts],
        out_specs=[pl.BlockSpec((tile, r.shape[1]), lambda i: (i, 0)) for r in rows]
        + [pl.BlockSpec(p.shape, lambda i: (0, 0)) for p in pars],
        out_shape=[jax.ShapeDtypeStruct(r.shape, F32) for r in rows]
        + [jax.ShapeDtypeStruct(p.shape, F32) for p in pars],
        compiler_params=_params("arbitrary"),
    )(*rows, *pars, *cts)
    return tuple(res[:nr]), tuple(res[nr:])


def rowwise(name, f, rows, pars=()):
    outs = jax.eval_shape(f, *[jax.ShapeDtypeStruct((8, r.shape[1]), F32) for r in rows],
                          *[jax.ShapeDtypeStruct(p.shape, F32) for p in pars])
    tile = _row_tile(rows[0].shape[0], 2 * sum(r.shape[1] for r in rows) + sum(o.shape[1] for o in outs))

    @jax.custom_vjp
    def op(rows, pars):
        return tuple(_rowwise_fwd_call(name, f, rows, pars, tile))

    def op_fwd(rows, pars):
        return op(rows, pars), (rows, pars)

    def op_bwd(res, cts):
        return _rowwise_bwd_call(name + "_bwd", f, res[0], res[1], tuple(cts), tile)

    op.defvjp(op_fwd, op_bwd)
    return op(tuple(rows), tuple(pars))


MM_TM = 1024


def _tile_of(n, cap):
    best = None
    for c in range(LANES, min(n, cap) + 1, LANES):
        if n % c == 0:
            best = c
    return best or n


def _proj_dw(name, h, dys):
    m, k = h.shape
    n, tm = len(dys), 256
    steps = m // tm

    def body(h_ref, *refs):
        dy_refs, o_refs, accs = refs[:n], refs[n:2 * n], refs[2 * n:]
        ht = jnp.transpose(h_ref[...]).astype(BF16)
        first = pl.program_id(0) == 0
        for dy_ref, acc in zip(dy_refs, accs):
            for c0 in range(0, acc.shape[1], 1024):
                cols = slice(c0, min(c0 + 1024, acc.shape[1]))
                part = _dot(ht, dy_ref[:, cols])

                @pl.when(first)
                def _():
                    acc[:, cols] = part

                @pl.when(jnp.logical_not(first))
                def _():
                    acc[:, cols] += part

        @pl.when(pl.program_id(0) == steps - 1)
        def _():
            for o_ref, acc in zip(o_refs, accs):
                o_ref[...] = acc[...].astype(BF16)

    row = lambda width: pl.BlockSpec((tm, width), lambda i: (i, 0))
    return pl.pallas_call(
        body, name=name, grid=(steps,),
        in_specs=[row(k)] + [row(d.shape[1]) for d in dys],
        out_specs=[pl.BlockSpec((k, d.shape[1]), lambda i: (0, 0)) for d in dys],
        out_shape=[jax.ShapeDtypeStruct((k, d.shape[1]), BF16) for d in dys],
        scratch_shapes=[pltpu.VMEM((k, d.shape[1]), F32) for d in dys],
        compiler_params=_params("arbitrary"),
    )(h, *dys)


PROJ_GROUP_COLS = 4608


def _proj_dh(name, dys, ws, acc):
    m, k = dys[0].shape[0], ws[0].shape[0]
    n, tm = len(dys), 512

    def body(*refs):
        dy_refs, w_refs, rest = refs[:n], refs[n:2 * n], refs[2 * n:]
        total = _dot_nt(dy_refs[0][...], w_refs[0][...])
        for dy_ref, w_ref in zip(dy_refs[1:], w_refs[1:]):
            total = total + _dot_nt(dy_ref[...], w_ref[...])
        if acc is not None:
            total = total + rest[0][...]
        rest[-1][...] = total

    row = lambda width: pl.BlockSpec((tm, width), lambda i: (i, 0))
    return pl.pallas_call(
        body, name=name, grid=(m // tm,),
        in_specs=[row(d.shape[1]) for d in dys] + [pl.BlockSpec(w.shape, lambda i: (0, 0)) for w in ws] + ([row(k)] if acc is not None else []),
        out_specs=row(k), out_shape=jax.ShapeDtypeStruct((m, k), F32), compiler_params=_params("parallel"),
    )(*dys, *ws, *([acc] if acc is not None else []))


def _proj_fwd(name, h, ws):
    m, k = h.shape
    n, tm = len(ws), 512

    def body(h_ref, *refs):
        hv = h_ref[...].astype(BF16)
        for w_ref, o_ref in zip(refs[:n], refs[n:]):
            o_ref[...] = _dot(hv, w_ref[...])

    row = lambda width: pl.BlockSpec((tm, width), lambda i: (i, 0))
    return pl.pallas_call(
        body, name=name, grid=(m // tm,),
        in_specs=[row(k)] + [pl.BlockSpec(w.shape, lambda i: (0, 0)) for w in ws],
        out_specs=[row(w.shape[1]) for w in ws],
        out_shape=[jax.ShapeDtypeStruct((m, w.shape[1]), F32) for w in ws], compiler_params=_params("parallel"),
    )(h, *ws)


def project_in(name, h, ws):
    keys = list(ws)
    groups, cols = [[]], 0
    for p in keys:
        if groups[-1] and cols + ws[p].shape[1] > PROJ_GROUP_COLS:
            groups.append([])
            cols = 0
        groups[-1].append(p)
        cols += ws[p].shape[1]

    @jax.custom_vjp
    def op(h, ws):
        out = {}
        for i, group in enumerate(groups):
            out.update(zip(group, _proj_fwd(f"{name}_{i}", h, [ws[p] for p in group])))
        return out

    def op_fwd(h, ws):
        return op(h, ws), (h, ws)

    def op_bwd(res, dys):
        h, ws = res
        dh, dws = None, {}
        for i, group in enumerate(groups):
            dh = _proj_dh(f"{name}_dh{i}", [dys[p] for p in group], [ws[p] for p in group], dh)
            dws.update(zip(group, _proj_dw(f"{name}_dw{i}", h, [dys[p] for p in group])))
        return dh, dws

    op.defvjp(op_fwd, op_bwd)
    return op(h, ws)


def _ffn_up(name, h, wt):
    m, k = h.shape
    j, n, _ = wt.shape
    tm = MM_TM

    def body(h_ref, w_ref, o_ref):
        o_ref[0] = _dot_nt(h_ref[...], w_ref[0])

    return pl.pallas_call(
        body, name=name, grid=(m // tm, j),
        in_specs=[pl.BlockSpec((tm, k), lambda i, b: (i, 0)), pl.BlockSpec((1, n, k), lambda i, b: (b, 0, 0))],
        out_specs=pl.BlockSpec((1, tm, n), lambda i, b: (b, i, 0)),
        out_shape=jax.ShapeDtypeStruct((j, m, n), F32), compiler_params=_params("parallel", "parallel"),
    )(h, wt)


def _ffn_down(name, g, u, wd):
    j, m, n = g.shape
    d = wd.shape[2]
    tm = MM_TM

    def body(g_ref, u_ref, w_ref, o_ref):
        part = _dot(_silu(g_ref[0]) * u_ref[0], w_ref[0])

        @pl.when(pl.program_id(1) == 0)
        def _():
            o_ref[...] = part

        @pl.when(pl.program_id(1) > 0)
        def _():
            o_ref[...] += part

    act = pl.BlockSpec((1, tm, n), lambda i, b: (b, i, 0))
    return pl.pallas_call(
        body, name=name, grid=(m // tm, j),
        in_specs=[act, act, pl.BlockSpec((1, n, d), lambda i, b: (b, 0, 0))],
        out_specs=pl.BlockSpec((tm, d), lambda i, b: (i, 0)),
        out_shape=jax.ShapeDtypeStruct((m, d), F32), compiler_params=_params("parallel", "arbitrary"),
    )(g, u, wd)


def _ffn_down_bwd(name, dy, g, u, wd):
    j, m, n = g.shape
    d = wd.shape[2]
    tm = MM_TM

    def body(dy_ref, g_ref, u_ref, w_ref, dg_ref, du_ref):
        da = _dot_nt(dy_ref[...], w_ref[0])
        gv = g_ref[0]
        s = _sigmoid(gv)
        dg_ref[0] = da * u_ref[0] * (s * (1.0 + gv * (1.0 - s)))
        du_ref[0] = da * (gv * s)

    act = pl.BlockSpec((1, tm, n), lambda i, b: (b, i, 0))
    return pl.pallas_call(
        body, name=name, grid=(m // tm, j),
        in_specs=[pl.BlockSpec((tm, d), lambda i, b: (i, 0)), act, act, pl.BlockSpec((1, n, d), lambda i, b: (b, 0, 0))],
        out_specs=[act, act], out_shape=[jax.ShapeDtypeStruct((j, m, n), F32)] * 2,
        compiler_params=_params("parallel", "parallel"),
    )(dy, g, u, wd)


def _ffn_down_dw(name, g, u, dy):
    j, m, n = g.shape
    d = dy.shape[1]
    tn = _tile_of(d, 512)

    def body(g_ref, u_ref, dy_ref, o_ref):
        o_ref[0] = _dot_tn(_silu(g_ref[0]) * u_ref[0], dy_ref[...]).astype(BF16)

    act = pl.BlockSpec((1, m, n), lambda b, c: (b, 0, 0))
    return pl.pallas_call(
        body, name=name, grid=(j, d // tn),
        in_specs=[act, act, pl.BlockSpec((m, tn), lambda b, c: (0, c))],
        out_specs=pl.BlockSpec((1, n, tn), lambda b, c: (b, 0, c)),
        out_shape=jax.ShapeDtypeStruct((j, n, d), BF16), compiler_params=_params("parallel", "parallel"),
    )(g, u, dy)


def _ffn_up_dh(name, dg, du, wg, wu):
    j, m, n = dg.shape
    k = wg.shape[2]
    tm = MM_TM

    def body(dg_ref, du_ref, wg_ref, wu_ref, o_ref):
        part = _dot(dg_ref[0], wg_ref[0]) + _dot(du_ref[0], wu_ref[0])

        @pl.when(pl.program_id(1) == 0)
        def _():
            o_ref[...] = part

        @pl.when(pl.program_id(1) > 0)
        def _():
            o_ref[...] += part

    act = pl.BlockSpec((1, tm, n), lambda i, b: (b, i, 0))
    wsp = pl.BlockSpec((1, n, k), lambda i, b: (b, 0, 0))
    return pl.pallas_call(
        body, name=name, grid=(m // tm, j), in_specs=[act, act, wsp, wsp],
        out_specs=pl.BlockSpec((tm, k), lambda i, b: (i, 0)),
        out_shape=jax.ShapeDtypeStruct((m, k), F32), compiler_params=_params("parallel", "arbitrary"),
    )(dg, du, wg, wu)


def _ffn_up_dw(name, dy, h):
    j, m, n = dy.shape
    k = h.shape[1]
    tk = _tile_of(k, 512)

    def body(dy_ref, h_ref, o_ref):
        o_ref[0] = _dot_tn(dy_ref[0], h_ref[...]).astype(BF16)

    return pl.pallas_call(
        body, name=name, grid=(j, k // tk),
        in_specs=[pl.BlockSpec((1, m, n), lambda b, i: (b, 0, 0)), pl.BlockSpec((m, tk), lambda b, i: (0, i))],
        out_specs=pl.BlockSpec((1, n, tk), lambda b, i: (b, 0, i)),
        out_shape=jax.ShapeDtypeStruct((j, n, k), BF16), compiler_params=_params("parallel", "parallel"),
    )(dy, h)


def ffn(name, h, wg, wu, wd):
    @jax.custom_vjp
    def op(h, wg, wu, wd):
        return _ffn_down(name + "_d", _ffn_up(name + "_g", h, wg), _ffn_up(name + "_u", h, wu), wd)

    def op_fwd(h, wg, wu, wd):
        g, u = _ffn_up(name + "_g", h, wg), _ffn_up(name + "_u", h, wu)
        return _ffn_down(name + "_d", g, u, wd), (h, g, u, wg, wu, wd)

    def op_bwd(res, dy):
        h, g, u, wg, wu, wd = res
        dg, du = _ffn_down_bwd(name + "_d_bwd", dy, g, u, wd)
        return (_ffn_up_dh(name + "_dh", dg, du, wg, wu), _ffn_up_dw(name + "_g_dw", dg, h), _ffn_up_dw(name + "_u_dw", du, h),
                _ffn_down_dw(name + "_d_dw", g, u, dy))

    op.defvjp(op_fwd, op_bwd)
    return op(h, wg, wu, wd)


def _mmr_fwd(name, a, w):
    j, m, n = a.shape
    nn = w.shape[2]
    tm, tn = MM_TM, _tile_of(nn, 1024)

    def body(a_ref, w_ref, o_ref):
        part = _dot(a_ref[0], w_ref[0])

        @pl.when(pl.program_id(2) == 0)
        def _():
            o_ref[...] = part

        @pl.when(pl.program_id(2) > 0)
        def _():
            o_ref[...] += part

    return pl.pallas_call(
        body, name=name, grid=(m // tm, nn // tn, j),
        in_specs=[pl.BlockSpec((1, tm, n), lambda i, c, b: (b, i, 0)), pl.BlockSpec((1, n, tn), lambda i, c, b: (b, 0, c))],
        out_specs=pl.BlockSpec((tm, tn), lambda i, c, b: (i, c)),
        out_shape=jax.ShapeDtypeStruct((m, nn), F32),
        compiler_params=_params("parallel", "parallel", "arbitrary"),
    )(a, w)


def _mmr_da(name, dy, w):
    m, nn = dy.shape
    j, n, _ = w.shape
    tm = MM_TM

    def body(dy_ref, w_ref, o_ref):
        o_ref[0] = _dot_nt(dy_ref[...], w_ref[0])

    return pl.pallas_call(
        body, name=name, grid=(m // tm, j),
        in_specs=[pl.BlockSpec((tm, nn), lambda i, b: (i, 0)), pl.BlockSpec((1, n, nn), lambda i, b: (b, 0, 0))],
        out_specs=pl.BlockSpec((1, tm, n), lambda i, b: (b, i, 0)),
        out_shape=jax.ShapeDtypeStruct((j, m, n), F32),
        compiler_params=_params("parallel", "parallel"),
    )(dy, w)


def _mmr_dw(name, a, dy):
    j, m, n = a.shape
    nn = dy.shape[1]
    tn = _tile_of(nn, 512)

    def body(a_ref, dy_ref, o_ref):
        o_ref[0] = _dot_tn(a_ref[0], dy_ref[...]).astype(BF16)

    return pl.pallas_call(
        body, name=name, grid=(j, nn // tn),
        in_specs=[pl.BlockSpec((1, m, n), lambda b, c: (b, 0, 0)), pl.BlockSpec((m, tn), lambda b, c: (0, c))],
        out_specs=pl.BlockSpec((1, n, tn), lambda b, c: (b, 0, c)),
        out_shape=jax.ShapeDtypeStruct((j, n, nn), BF16),
        compiler_params=_params("parallel", "parallel"),
    )(a, dy)


def mm_rows(name, a, w):
    @jax.custom_vjp
    def op(a, w):
        return _mmr_fwd(name, a, w)

    def op_fwd(a, w):
        return op(a, w), (a, w)

    def op_bwd(res, dy):
        a, w = res
        return _mmr_da(name + "_da", dy, w), _mmr_dw(name + "_dw", a, dy)

    op.defvjp(op_fwd, op_bwd)
    return op(a, w)


def _colwise_specs(cols, pars, par_block):
    t = cols[0].shape[0]
    specs = [pl.BlockSpec((t, LANES), lambda j: (0, j)) for _ in cols]
    for p, blk in zip(pars, par_block):
        if blk == "lane":
            specs.append(pl.BlockSpec((p.shape[0], LANES), lambda j: (0, j)))
        else:
            specs.append(pl.BlockSpec((1,) + p.shape[1:], lambda j: (j, 0, 0)))
    return specs


def _colwise_fwd_call(name, f, cols, pars, par_block, n_out):
    t, c = cols[0].shape
    nc, npar = len(cols), len(pars)

    def body(*refs):
        ins = [r[...] for r in refs[:nc]] + [r[...] if b == "lane" else r[0] for r, b in zip(refs[nc:nc + npar], par_block)]
        res = f(*ins)
        for o_ref, o in zip(refs[nc + npar:], res):
            o_ref[...] = o

    return pl.pallas_call(
        body, name=name, grid=(c // LANES,),
        in_specs=_colwise_specs(cols, pars, par_block),
        out_specs=[pl.BlockSpec((t, LANES), lambda j: (0, j)) for _ in range(n_out)],
        out_shape=[jax.ShapeDtypeStruct((t, c), F32) for _ in range(n_out)],
        compiler_params=_params("parallel"),
    )(*cols, *pars)


def _colwise_bwd_call(name, f, cols, pars, par_block, cts):
    t, c = cols[0].shape
    nc, npar, nct = len(cols), len(pars), len(cts)

    def body(*refs):
        ins = [r[...] for r in refs[:nc]] + [r[...] if b == "lane" else r[0] for r, b in zip(refs[nc:nc + npar], par_block)]
        gs = tuple(r[...] for r in refs[nc + npar:nc + npar + nct])
        outs = refs[nc + npar + nct:]
        _, vjp = jax.vjp(f, *ins)
        d = vjp(gs)
        for o_ref, v in zip(outs[:nc], d[:nc]):
            o_ref[...] = v
        for o_ref, v, b in zip(outs[nc:], d[nc:], par_block):
            if b == "lane":
                o_ref[...] = v
            else:
                o_ref[0] = v

    res = pl.pallas_call(
        body, name=name, grid=(c // LANES,),
        in_specs=_colwise_specs(cols, pars, par_block) + [pl.BlockSpec((t, LANES), lambda j: (0, j)) for _ in cts],
        out_specs=_colwise_specs(cols, pars, par_block),
        out_shape=[jax.ShapeDtypeStruct(v.shape, F32) for v in (*cols, *pars)],
        compiler_params=_params("parallel"),
    )(*cols, *pars, *cts)
    return tuple(res[:nc]), tuple(res[nc:])


def colwise(name, f, cols, pars, par_block, n_out):
    @jax.custom_vjp
    def op(cols, pars):
        return tuple(_colwise_fwd_call(name, f, cols, pars, par_block, n_out))

    def op_fwd(cols, pars):
        return op(cols, pars), (cols, pars)

    def op_bwd(res, cts):
        return _colwise_bwd_call(name + "_bwd", f, res[0], res[1], par_block, tuple(cts))

    op.defvjp(op_fwd, op_bwd)
    return op(tuple(cols), tuple(pars))


def _rg_block(x, gate, cw, cb, wr, br, wi, bi, lam):
    xa = _causal_conv(x, cw) + cb
    r = _sigmoid(_dot(xa, wr) + br)
    i = _sigmoid(_dot(xa, wi) + bi)
    log_a = -RG_C * r * _softplus(-lam)
    a = jnp.exp(log_a)
    b = jnp.sqrt(_neg_expm1(2.0 * log_a)) * (i * xa)
    return (_lin_scan(a, b) * _gelu(gate),)


def _dn_conv_block(mode):
    def f(x, cw):
        c = _silu(_causal_conv(x, cw))
        if mode == "v":
            return (c,)
        c = c * lax.rsqrt(jnp.sum(c * c, axis=-1, keepdims=True) + EPS)
        return (c * (DN_HEAD_DIM ** -0.5),) if mode == "q" else (c,)
    return f


def _block_diag(w):
    w = w.reshape(8, 2, 64, 64)
    z = jnp.zeros((8, 64, 64), w.dtype)
    top = jnp.concatenate([w[:, 0], z], axis=2)
    bot = jnp.concatenate([z, w[:, 1]], axis=2)
    return jnp.concatenate([top, bot], axis=1)


DN_HP = 8


def _dn_block(S, qw, kw, vw, gb, h0, tinv=None):
    hp, hd = S.shape[0], DN_HEAD_DIM
    heads = lambda a: jnp.concatenate([a[None, :, j * hd:(j + 1) * hd] for j in range(hp)], axis=0)
    lane = lax.broadcasted_iota(jnp.int32, gb.shape, 1)
    col = lambda i: jnp.sum(jnp.where(lane == i, gb, 0.0), axis=1, keepdims=True)[None]
    beta = jnp.concatenate([col(h0 + j) for j in range(hp)], axis=0)
    g = jnp.concatenate([col(h0 + j + DN_HEADS) for j in range(hp)], axis=0)
    s_new, o, tinv = _dn_step(S, heads(qw), heads(kw), heads(vw), beta, g, tinv)
    return s_new, jnp.concatenate([o[j:j + 1].reshape(o.shape[1:]) for j in range(hp)], axis=1), tinv


@jax.custom_vjp
def _unit_lower_inverse(a):
    c = a.shape[-1]
    eye = (lax.broadcasted_iota(jnp.int32, (c, c), 0) == lax.broadcasted_iota(jnp.int32, (c, c), 1)).astype(F32)
    p = -a
    tinv = eye + p
    for _ in range(5):
        p = _dot3(p, p)
        tinv = tinv + _dot3(tinv, p)
    return tinv


def _unit_lower_inverse_fwd(a):
    t = _unit_lower_inverse(a)
    return t, t


def _unit_lower_inverse_bwd(t, g):
    return (-_mm(_mm(t, g, "tn", 3), t, "nt", 3),)


_unit_lower_inverse.defvjp(_unit_lower_inverse_fwd, _unit_lower_inverse_bwd)


@jax.custom_vjp
def _known_inverse(a, t):
    return t


_known_inverse.defvjp(lambda a, t: (t, t), lambda t, g: (_unit_lower_inverse_bwd(t, g)[0], jnp.zeros_like(t)))


def _dn_step(S, q, k, v, beta, g, tinv=None):
    c = DN_CHUNK
    ri = lax.broadcasted_iota(jnp.int32, (c, c), 0)
    ci = lax.broadcasted_iota(jnp.int32, (c, c), 1)
    incl, strict = ri >= ci, ri > ci
    gam = _cumsum_rows(g)
    gam_row = jnp.sum(jnp.where(ri <= ci, g, 0.0), axis=-2, keepdims=True)
    gam_last = jnp.sum(g, axis=-2, keepdims=True)
    decay = jnp.where(incl, jnp.exp(jnp.where(incl, gam - gam_row, 0.0)), 0.0)
    kb = k * beta
    vb = v * beta
    a = jnp.where(strict, _dot_nt(kb, k) * decay, 0.0)
    tinv = _unit_lower_inverse(a) if tinv is None else _known_inverse(a, tinv)
    e_gam = jnp.exp(gam)
    u0 = _dot3(tinv, vb)
    wk = _dot3(tinv, kb * e_gam)
    qk = jnp.where(incl, _dot_nt(q, k) * decay, 0.0)
    q_dec = q * e_gam
    k_dec = k * jnp.exp(gam_last - gam)
    u = u0 - _dot(wk, S)
    o = _dot(q_dec, S) + _dot(qk, u)
    s_new = S * jnp.exp(gam_last) + _dot_tn(k_dec, u)
    return s_new, o, tinv


def _dn_fwd_call(q, k, v, gb):
    t, w = q.shape
    n, hp, hd, c = t // DN_CHUNK, DN_HP, DN_HEAD_DIM, DN_CHUNK

    def body(q_ref, k_ref, v_ref, gb_ref, o_ref, s0_ref, ti_ref, s_scr):
        @pl.when(pl.program_id(1) == 0)
        def _():
            s_scr[...] = jnp.zeros_like(s_scr)

        s_old = s_scr[...]
        s0_ref[:, 0] = s_old
        s_new, o, tinv = _dn_block(s_old, q_ref[...], k_ref[...], v_ref[...], gb_ref[...], pl.program_id(0) * hp)
        o_ref[...] = o
        ti_ref[:, 0] = tinv
        s_scr[...] = s_new

    blk = pl.BlockSpec((c, hp * hd), lambda g, i: (i, g))
    return pl.pallas_call(
        body, name="dn_core", grid=(DN_HEADS // hp, n),
        in_specs=[blk, blk, blk, pl.BlockSpec((c, LANES), lambda g, i: (i, 0))],
        out_specs=[blk, pl.BlockSpec((hp, 1, hd, hd), lambda g, i: (g, i, 0, 0)), pl.BlockSpec((hp, 1, c, c), lambda g, i: (g, i, 0, 0))],
        out_shape=[jax.ShapeDtypeStruct((t, w), F32), jax.ShapeDtypeStruct((DN_HEADS, n, hd, hd), F32),
                   jax.ShapeDtypeStruct((DN_HEADS, n, c, c), F32)],
        scratch_shapes=[pltpu.VMEM((hp, hd, hd), F32)],
        compiler_params=_params("parallel", "arbitrary"),
    )(q, k, v, gb)


def _dn_bwd_call(q, k, v, gb, s0, ti, do):
    t, w = q.shape
    n, hp, hd, c = t // DN_CHUNK, DN_HP, DN_HEAD_DIM, DN_CHUNK
    ng = DN_HEADS // hp

    def body(q_ref, k_ref, v_ref, gb_ref, s0_ref, ti_ref, do_ref, dq_ref, dk_ref, dv_ref, dgb_ref, ds_scr):
        @pl.when(pl.program_id(1) == 0)
        def _():
            ds_scr[...] = jnp.zeros_like(ds_scr)

        h0, tinv = pl.program_id(0) * hp, ti_ref[:, 0]
        _, vjp = jax.vjp(lambda *a: _dn_block(*a, h0, tinv)[:2], s0_ref[:, 0], q_ref[...], k_ref[...], v_ref[...], gb_ref[...])
        ds, dq, dk, dv, dgb = vjp((ds_scr[...], do_ref[...]))
        ds_scr[...] = ds
        dq_ref[...], dk_ref[...], dv_ref[...] = dq, dk, dv
        dgb_ref[0] = dgb

    blk = pl.BlockSpec((c, hp * hd), lambda g, i: (n - 1 - i, g))
    res = pl.pallas_call(
        body, name="dn_core_bwd", grid=(ng, n),
        in_specs=[blk, blk, blk, pl.BlockSpec((c, LANES), lambda g, i: (n - 1 - i, 0)),
                  pl.BlockSpec((hp, 1, hd, hd), lambda g, i: (g, n - 1 - i, 0, 0)),
                  pl.BlockSpec((hp, 1, c, c), lambda g, i: (g, n - 1 - i, 0, 0)), blk],
        out_specs=[blk, blk, blk, pl.BlockSpec((1, c, LANES), lambda g, i: (g, n - 1 - i, 0))],
        out_shape=[jax.ShapeDtypeStruct((t, w), F32)] * 3 + [jax.ShapeDtypeStruct((ng, t, LANES), F32)],
        scratch_shapes=[pltpu.VMEM((hp, hd, hd), F32)],
        compiler_params=_params("parallel", "arbitrary"),
    )(q, k, v, gb, s0, ti, do)
    return res[0], res[1], res[2], jnp.sum(res[3], axis=0)


@jax.custom_vjp
def dn_core(q, k, v, gb):
    return _dn_fwd_call(q, k, v, gb)[0]


def _dn_core_fwd(q, k, v, gb):
    o, s0, ti = _dn_fwd_call(q, k, v, gb)
    return o, (q, k, v, gb, s0, ti)


def _dn_core_bwd(res, do):
    return _dn_bwd_call(*res, do)


dn_core.defvjp(_dn_core_fwd, _dn_core_bwd)


ATT_GH = 4


def _att_block(q, kp, kc, vp, vc, qn, kn, slope, has_prev, dil):
    s = ATT_SPAN
    qh = _rms(q, qn) * (ATT_HEAD_DIM ** -0.5)
    qi = lax.broadcasted_iota(jnp.int32, (s, s), 0)
    kj = lax.broadcasted_iota(jnp.int32, (s, s), 1)
    d_p = qi + s - kj
    d_c = qi - kj
    s_p = _dot_nt(qh, _rms(kp, kn)) - slope * (d_p * dil).astype(F32)
    s_c = _dot_nt(qh, _rms(kc, kn)) - slope * (d_c * dil).astype(F32)
    s_p = jnp.where((d_p <= s) & (has_prev > 0), s_p, NEG_INF)
    s_c = jnp.where(d_c >= 0, s_c, NEG_INF)
    m = lax.stop_gradient(jnp.maximum(jnp.max(s_p, axis=-1, keepdims=True), jnp.max(s_c, axis=-1, keepdims=True)))
    p_p = jnp.exp(s_p - m)
    p_c = jnp.exp(s_c - m)
    den = jnp.sum(p_p, axis=-1, keepdims=True) + jnp.sum(p_c, axis=-1, keepdims=True)
    o = _dot(p_p / den, vp) + _dot(p_c / den, vc)
    lse = m + jnp.log(den)
    return o, jnp.broadcast_to(lse, o.shape)


def _att_heads(a):
    e = ATT_HEAD_DIM
    return jnp.concatenate([a[None, :, h * e:(h + 1) * e] for h in range(ATT_GH)], axis=0)


def _att_lanes(a):
    return jnp.concatenate([a[h:h + 1].reshape(a.shape[1:]) for h in range(ATT_GH)], axis=1)


def _att_rows(q, kp, kc, vp, vc, qn, kn, group, has_prev, dil):
    head = lax.broadcasted_iota(jnp.int32, (ATT_GH, 1, 1), 0) + (ATT_GH * group + 1)
    slope = jnp.exp(head.astype(F32) * (-8.0 / ATT_HEADS * math.log(2.0)))
    o, lse = _att_block(_att_heads(q), _att_heads(kp), _att_heads(kc), _att_heads(vp), _att_heads(vc), qn, kn, slope, has_prev, dil)
    return _att_lanes(o), _att_lanes(lse)


def _att_specs(group, dil):
    blk = (ATT_SPAN, ATT_GH * ATT_HEAD_DIM)
    cur = lambda which: pl.BlockSpec(blk, lambda r, n: (n, r * 9 + 3 * which + group))
    prev = lambda which: pl.BlockSpec(blk, lambda r, n: (jnp.maximum(n - 1, 0), r * 9 + 3 * which + group))
    out = pl.BlockSpec(blk, lambda r, n: (n, r))
    gain = pl.BlockSpec((ATT_GH, 1, ATT_HEAD_DIM), lambda r, n: (0, 0, 0))
    return [cur(0), prev(1), cur(1), prev(2), cur(2), gain, gain], out, gain


def _att_fwd_call(name, group, dil, pa, qn, kn):
    t = pa.shape[0]
    l = t // dil
    w = ATT_GH * ATT_HEAD_DIM
    ins, out, _ = _att_specs(group, dil)
    pav = pa.reshape(l, dil * pa.shape[1])

    def body(q_ref, kp_ref, kc_ref, vp_ref, vc_ref, qn_ref, kn_ref, o_ref, lse_ref):
        o_ref[...], lse_ref[...] = _att_rows(q_ref[...], kp_ref[...], kc_ref[...], vp_ref[...], vc_ref[...], qn_ref[...],
                                             kn_ref[...], group, pl.program_id(1), dil)

    o, lse = pl.pallas_call(
        body, name=name, grid=(dil, l // ATT_SPAN), in_specs=ins, out_specs=[out, out],
        out_shape=[jax.ShapeDtypeStruct((l, dil * w), F32)] * 2, compiler_params=_params("parallel", "arbitrary"),
    )(pav, pav, pav, pav, pav, qn, kn)
    return o.reshape(t, w), lse.reshape(t, w)


def _att_bwd_call(name, group, dil, pa, qn, kn, do, dlse):
    t = pa.shape[0]
    l = t // dil
    w = ATT_GH * ATT_HEAD_DIM
    ins, out, gain = _att_specs(group, dil)
    pav = pa.reshape(l, dil * pa.shape[1])

    def body(q_ref, kp_ref, kc_ref, vp_ref, vc_ref, qn_ref, kn_ref, do_ref, dlse_ref,
             dq_ref, dkp_ref, dkc_ref, dvp_ref, dvc_ref, dqn_ref, dkn_ref):
        has_prev = pl.program_id(1)
        _, vjp = jax.vjp(lambda *a: _att_rows(*a, group, has_prev, dil), q_ref[...], kp_ref[...], kc_ref[...], vp_ref[...],
                         vc_ref[...], qn_ref[...], kn_ref[...])
        dq, dkp, dkc, dvp, dvc, dqn, dkn = vjp((do_ref[...], dlse_ref[...]))
        dq_ref[...], dkp_ref[...], dkc_ref[...], dvp_ref[...], dvc_ref[...] = dq, dkp, dkc, dvp, dvc

        @pl.when((pl.program_id(0) == 0) & (pl.program_id(1) == 0))
        def _():
            dqn_ref[...] = jnp.zeros_like(dqn_ref)
            dkn_ref[...] = jnp.zeros_like(dkn_ref)

        dqn_ref[...] += dqn
        dkn_ref[...] += dkn

    res = pl.pallas_call(
        body, name=name + "_bwd", grid=(dil, l // ATT_SPAN), in_specs=ins + [out, out],
        out_specs=[out] * 5 + [gain, gain],
        out_shape=[jax.ShapeDtypeStruct((l, dil * w), F32)] * 5 + [jax.ShapeDtypeStruct(qn.shape, F32)] * 2,
        compiler_params=_params("arbitrary", "arbitrary"),
    )(pav, pav, pav, pav, pav, qn, kn, do.reshape(l, dil * w), dlse.reshape(l, dil * w))
    dq, dkp, dkc, dvp, dvc, dqn, dkn = res
    back = lambda g: jnp.pad(g[ATT_SPAN:], ((0, ATT_SPAN), (0, 0)))
    return dq.reshape(t, w), (dkc + back(dkp)).reshape(t, w), (dvc + back(dvp)).reshape(t, w), dqn, dkn


def _att_mix(o1, o2, o3, l1, l2, l3):
    m = jnp.maximum(jnp.maximum(l1, l2), l3)
    e1, e2, e3 = jnp.exp(l1 - m), jnp.exp(l2 - m), jnp.exp(l3 - m)
    s = e1 + e2 + e3
    return (jnp.concatenate([o1 * (e1 / s), o2 * (e2 / s), o3 * (e3 / s)], axis=1),)


def att_branch(name, pa, qn, kn):
    e = ATT_HEAD_DIM
    gains = lambda p, g: p[ATT_GH * g:ATT_GH * (g + 1)].reshape(ATT_GH, 1, e)

    @jax.custom_vjp
    def groups(pa, qn, kn):
        res = [_att_fwd_call(f"{name}_att{g}", g, dil, pa, gains(qn, g), gains(kn, g)) for g, (_, dil) in enumerate(ATT_GROUPS)]
        return tuple(r[0] for r in res) + tuple(r[1] for r in res)

    def groups_fwd(pa, qn, kn):
        return groups(pa, qn, kn), (pa, qn, kn)

    def groups_bwd(res, cts):
        pa, qn, kn = res
        n = len(ATT_GROUPS)
        parts = [_att_bwd_call(f"{name}_att{g}", g, dil, pa, gains(qn, g), gains(kn, g), cts[g], cts[n + g])
                 for g, (_, dil) in enumerate(ATT_GROUPS)]
        d_pa = jnp.concatenate([p[i] for i in range(3) for p in parts], axis=1)
        return (d_pa, jnp.concatenate([p[3] for p in parts]).reshape(qn.shape), jnp.concatenate([p[4] for p in parts]).reshape(kn.shape))

    groups.defvjp(groups_fwd, groups_bwd)
    return rowwise(f"{name}_attmix", _att_mix, groups(pa, qn, kn))[0]


def dn_gates(name, ba, a_log, dt_bias):
    place = lambda p: jnp.pad(p.reshape(1, DN_HEADS), ((0, 0), (DN_HEADS, LANES - 2 * DN_HEADS)))

    def f(x, al, dt):
        lane = lax.broadcasted_iota(jnp.int32, x.shape, 1)
        return (jnp.where(lane < DN_HEADS, _sigmoid(x), -jnp.exp(al) * _softplus(x + dt)),)

    return rowwise(name, f, (ba,), (place(a_log), place(dt_bias)))[0]


def _dn_out(o, z, g):
    parts = []
    for h in range(DN_HEADS):
        sl = slice(h * DN_HEAD_DIM, (h + 1) * DN_HEAD_DIM)
        parts.append(_rms(o[:, sl], g[:, sl]) * _silu(z[:, sl]))
    return (jnp.concatenate(parts, axis=1),)


def _merge(ml, za, zb, zc):
    d = D_MODEL
    return (_sigmoid(ml[:, :d]) * za + _sigmoid(ml[:, d:2 * d]) * zb + _sigmoid(ml[:, 2 * d:]) * zc,)


def add_norm(name, x, pend, scale, gain):
    if pend is None:
        return x, rowwise(name, lambda a, g: (_rms(a, g),), (x,), (gain,))[0]

    def f(a, b, g):
        s = a + scale * b
        return s, _rms(s, g)

    return rowwise(name, f, (x, pend), (gain,))


W_IN_PIECES = (("rgx", 0, 1024), ("gate", 1024, 1024), ("att", 2048, 2304), ("dq", 4352, 1024), ("dk", 5376, 1024),
               ("dv", 6400, 1024), ("dz", 7424, 1024), ("ba", 8448, 16), ("mrg", 8464, 3072))
RG_PAR_BLOCKS = ("lane", "lane", "blk", "lane", "blk", "lane", "lane")


def mixer(name, u, w, p):
    mm = lambda nm, a, wt: mm_rows(nm, a[None], wt[None])
    pr = project_in(name + "_in", u, {k: w["in_" + k] for k, _, _ in W_IN_PIECES})
    ya = colwise(name + "_rg", _rg_block, (pr["rgx"], pr["gate"]),
                 (w["rg_conv_w"], p["rg_conv_b"], _block_diag(p["rg_w_r"]), p["rg_b_r"], _block_diag(p["rg_w_i"]),
                  p["rg_b_i"], p["rg_lambda"]), RG_PAR_BLOCKS, 1)[0]
    yb = att_branch(name, pr["att"], p["att_q_norm"], p["att_k_norm"])
    cw = w["dn_conv_w"]
    cq = colwise(name + "_dnq", _dn_conv_block("q"), (pr["dq"],), (cw[:, :1024],), ("lane",), 1)[0]
    ck = colwise(name + "_dnk", _dn_conv_block("k"), (pr["dk"],), (cw[:, 1024:2048],), ("lane",), 1)[0]
    cv = colwise(name + "_dnv", _dn_conv_block("v"), (pr["dv"],), (cw[:, 2048:],), ("lane",), 1)[0]
    gb = dn_gates(name + "_dngate", pr["ba"], p["dn_a_log"], p["dn_dt_bias"])
    o_dn = dn_core(cq, ck, cv, gb)
    yc = rowwise(name + "_dnout", _dn_out, (o_dn, pr["dz"]), (p["dn_out_norm"].reshape(1, D_MODEL),))[0]
    y = rowwise(name + "_merge", _merge, (pr["mrg"], mm(name + "_ba", ya, w["br_a"]), mm(name + "_bb", yb, w["br_b"]),
                                          mm(name + "_bc", yc, w["br_c"])))[0]
    return mm(name + "_out", y, w["w_out"])


def _loss_call(x, pend, target):
    t, d = x.shape
    tile = _row_tile(t)

    def body(x_ref, p_ref, t_ref, loss_ref, g_ref):
        err = x_ref[...] + 0.5 * p_ref[...] - t_ref[...]
        g_ref[...] = err * (1.0 / d)

        @pl.when(pl.program_id(0) == 0)
        def _():
            loss_ref[...] = jnp.zeros_like(loss_ref)

        loss_ref[...] += jnp.full(loss_ref.shape, 0.5 / d, F32) * jnp.sum(err * err)

    blk = pl.BlockSpec((tile, d), lambda i: (i, 0))
    loss, g = pl.pallas_call(
        body, name="loss", grid=(t // tile,), in_specs=[blk, blk, blk],
        out_specs=[pl.BlockSpec((8, LANES), lambda i: (0, 0)), blk],
        out_shape=[jax.ShapeDtypeStruct((8, LANES), F32), jax.ShapeDtypeStruct((t, d), F32)],
        compiler_params=_params("arbitrary"),
    )(x, pend, target)
    return loss[0, 0], g


@jax.custom_vjp
def loss_op(x, pend, target):
    return _loss_call(x, pend, target)[0]


def _loss_fwd(x, pend, target):
    loss, g = _loss_call(x, pend, target)
    return loss, g


def _loss_bwd(g, ct):
    return ct * g, (0.5 * ct) * g, None


loss_op.defvjp(_loss_fwd, _loss_bwd)


def first_ffn(wg, wu, wd, gain, x):
    x, h = add_norm("L0_n1", x, None, 0.0, gain)
    return x, ffn("L0_f1", h, wg, wu, wd)


def rest_of_step(g, conv, p, x, pend, target):
    scale = 0.5
    w = [split_layer({n: g[n, l] for n, _ in MATRICES if (n, l) in g}, {n: conv[n][l] for n, _ in CONVS}) for l in range(len(p))]
    for l in range(len(p)):
        n = f"L{l}"
        if l > 0:
            x, h = add_norm(n + "_n1", x, pend, scale, p[l]["ffn1_norm"])
            pend, scale = ffn(n + "_f1", h, w[l]["ffn1_w_gate"], w[l]["ffn1_w_up"], w[l]["ffn1_w_down"]), 0.5
        x, h = add_norm(n + "_nm", x, pend, scale, p[l]["mix_norm"])
        pend, scale = mixer(n + "_mx", h, w[l], p[l]), 1.0
        x, h = add_norm(n + "_n2", x, pend, scale, p[l]["ffn2_norm"])
        pend, scale = ffn(n + "_f2", h, w[l]["ffn2_w_gate"], w[l]["ffn2_w_up"], w[l]["ffn2_w_down"]), 0.5
    return loss_op(x, pend, target)


WEIGHT_NAMES = ("ffn1_norm", "ffn1_w_gate", "ffn1_w_up", "ffn1_w_down", "mix_norm", "w_in", "rg_conv_w", "rg_conv_b",
                "rg_w_r", "rg_b_r", "rg_w_i", "rg_b_i", "rg_lambda", "att_q_norm", "att_k_norm", "dn_conv_w", "dn_a_log",
                "dn_dt_bias", "dn_out_norm", "w_branch", "w_out", "ffn2_norm", "ffn2_w_gate", "ffn2_w_up", "ffn2_w_down")
MATRICES = (("ffn1_w_gate", 2), ("ffn1_w_up", 2), ("ffn1_w_down", 1), ("w_in", 2), ("w_branch", 1), ("w_out", 1),
            ("ffn2_w_gate", 2), ("ffn2_w_up", 2), ("ffn2_w_down", 1))
CONVS = (("rg_conv_w", 2), ("dn_conv_w", 2))
SHARD_AXIS = dict(MATRICES + CONVS)
SMALL_NAMES = tuple(n for n in WEIGHT_NAMES if n not in SHARD_AXIS)
ROW_PARAMS = ("ffn1_norm", "mix_norm", "rg_conv_b", "rg_b_r", "rg_b_i", "rg_lambda", "ffn2_norm")
FFN_MATS = ("ffn1_w_gate", "ffn1_w_up", "ffn1_w_down", "ffn2_w_gate", "ffn2_w_up", "ffn2_w_down")
TRANSPOSED_MATS = ("ffn1_w_gate", "ffn1_w_up", "ffn2_w_gate", "ffn2_w_up")
W_IN_SHARD = 2884
GATHER_ORDER = ((("ffn1_w_gate", 0), ("ffn1_w_up", 0), ("ffn1_w_down", 0)),
                (("w_in", 0), ("w_branch", 0), ("w_out", 0)),
                None)
GATHER_IDS = (1, 6, 7)
LATE_MATS = ("ffn2_w_gate", "ffn2_w_up", "ffn2_w_down", "w_out", "w_branch")
EXCHANGE_GROUPS = (lambda n, l: l == 1 and n in LATE_MATS,
                   lambda n, l: (l == 1) != (n in LATE_MATS),
                   lambda n, l: l == 0 and n == "w_in",
                   lambda n, l: l == 0 and n not in LATE_MATS and n != "w_in")


def _shard_minor(a, axis):
    a = jnp.moveaxis(a, 0, axis)
    return a.reshape(a.shape[:axis] + (N_CHIPS * a.shape[axis + 1],) + a.shape[axis + 2:])


def _w_in_piece(g, off, n):
    s = W_IN_SHARD
    parts = [g[j][:, max(off, j * s) - j * s:min(off + n, (j + 1) * s) - j * s]
             for j in range(N_CHIPS) if max(off, j * s) < min(off + n, (j + 1) * s)]
    return jnp.concatenate(parts, axis=1) if len(parts) > 1 else parts[0]


def _w_in_chip_grad(gl, j):
    s = W_IN_SHARD
    parts = [gl["in_" + k][:, max(off, j * s) - off:min(off + n, (j + 1) * s) - off]
             for k, off, n in W_IN_PIECES if max(off, j * s) < min(off + n, (j + 1) * s)]
    return jnp.concatenate(parts, axis=1)


def _layer_weights(g, conv):
    w = {n: g[n] for n in FFN_MATS if n in g}
    w["w_out"] = g["w_out"].reshape(D_MODEL, D_MODEL)
    for k, off, n in W_IN_PIECES:
        piece = _w_in_piece(g["w_in"], off, n)
        w["in_" + k] = jnp.pad(piece, ((0, 0), (0, LANES - n))) if n < LANES else piece
    wb = g["w_branch"].reshape(-1, D_MODEL)
    w["br_a"], w["br_b"], w["br_c"] = wb[:1024], wb[1024:1792], wb[1792:]
    return dict(w, **conv)


def _layer_weight_grads(gl):
    out = {n: gl[n] for n in FFN_MATS if n in gl}
    out["w_out"] = gl["w_out"].reshape(N_CHIPS, -1, D_MODEL)
    out["w_branch"] = jnp.concatenate([gl["br_a"], gl["br_b"], gl["br_c"]], axis=0).reshape(N_CHIPS, -1, D_MODEL)
    out["w_in"] = jnp.stack([_w_in_chip_grad(gl, j) for j in range(N_CHIPS)])
    return out, {n: gl[n] for n, _ in CONVS}


@jax.custom_vjp
def split_layer(g, conv):
    return _layer_weights(g, conv)


split_layer.defvjp(lambda g, conv: (_layer_weights(g, conv), None), lambda _, gw: _layer_weight_grads(gw))


def layer_small(small, l):
    p = {n: small[n][l] for n in SMALL_NAMES}
    for n in ROW_PARAMS:
        p[n] = small[n][l:l + 1]
    return p


def layer_small_grads(gp, small):
    return {n: jnp.stack([g[n] for g in gp]).reshape(small[n].shape) for n in SMALL_NAMES}


HBM_SPEC = pl.BlockSpec(memory_space=pl.ANY)


def _place():
    x, y, c = lax.axis_index("x"), lax.axis_index("y"), lax.axis_index("c")
    other_chips = [(1 - x, y), (x, 1 - y), (1 - x, 1 - y)]
    return x, y, c, 2 * x + y, (x, y, 1 - c), other_chips


def _half_rows(ref, lead, hc):
    hr = ref.shape[-2] // 2
    return ref.at[(*lead, pl.ds(pl.multiple_of(hc * hr, 16), hr), slice(None))]


def _chip_index():
    return (2 * lax.axis_index("x") + lax.axis_index("y")).astype(jnp.int32).reshape(1)


def cast_into_blocks(name, w):
    l, rows, cols = w.shape
    tr = rows // 2

    def body(me_ref, w_ref, *o_refs):
        for a, o_ref in enumerate(o_refs):
            o_ref[...] = w_ref[a:a + 1].astype(BF16)

    return pl.pallas_call(
        body, name=name, out_shape=[jax.ShapeDtypeStruct((N_CHIPS, rows, cols), BF16)] * l,
        grid_spec=pltpu.PrefetchScalarGridSpec(
            num_scalar_prefetch=1, grid=(rows // tr,),
            in_specs=[pl.BlockSpec((l, tr, cols), lambda i, me: (0, i, 0))],
            out_specs=[pl.BlockSpec((1, tr, cols), lambda i, me: (me[0], i, 0))] * l),
        compiler_params=_params("parallel"),
    )(_chip_index(), w)


def _gather_blocks(bufs_in, bufs_out, send_sems, recv_sems):
    n = len(bufs_in)
    x, y, c, me, sibling, chips = _place()

    def copy(s, src, dst, to):
        return pltpu.make_async_remote_copy(src_ref=src, dst_ref=dst, send_sem=send_sems.at[s], recv_sem=recv_sems.at[s],
                                            device_id=to, device_id_type=MESH)

    first, passed = [], []
    for j, (cx, cy) in enumerate(chips):
        for i in range(n):
            cp = copy(6 * i + j, _half_rows(bufs_in[i], (me,), c), _half_rows(bufs_out[i], (me,), c), (cx, cy, c))
            cp.start()
            first.append(cp)
    for j, (cx, cy) in enumerate(chips):
        k = 2 * cx + cy
        for i in range(n):
            copy(6 * i + j, _half_rows(bufs_in[i], (me,), c), _half_rows(bufs_out[i], (k,), c), (cx, cy, c)).wait_recv()
            cp = copy(6 * i + 3 + j, _half_rows(bufs_out[i], (k,), c), _half_rows(bufs_out[i], (k,), c), sibling)
            cp.start()
            passed.append(cp)
    for j, (cx, cy) in enumerate(chips):
        k = 2 * cx + cy
        for i in range(n):
            copy(6 * i + 3 + j, _half_rows(bufs_in[i], (me,), c), _half_rows(bufs_out[i], (k,), 1 - c), sibling).wait_recv()
    for cp in first + passed:
        cp.wait_send()


def _handshake(peers):
    barrier = pltpu.get_barrier_semaphore()
    for p in peers:
        pl.semaphore_signal(barrier, inc=1, device_id=p, device_id_type=MESH)
    pl.semaphore_wait(barrier, len(peers))


def allgather_blocks_sc(name, bufs, collective_id):
    n = len(bufs)
    refs = [jax.new_ref(b, memory_space=pltpu.MemorySpace.HBM) for b in bufs]

    @pl.kernel(mesh=plsc.ScalarSubcoreMesh(axis_name="sequencer", num_cores=1), name=name,
               scratch_types=(pltpu.SemaphoreType.DMA((6 * n,)), pltpu.SemaphoreType.DMA((6 * n,))),
               compiler_params=pltpu.CompilerParams(collective_id=collective_id))
    def launch(send_sems, recv_sems):
        x, y, c, me, sibling, chips = _place()
        _handshake([(cx, cy, c) for cx, cy in chips] + [sibling])
        _gather_blocks(refs, refs, send_sems, recv_sems)

    launch()
    return [jax.freeze(r) for r in refs]


PEER_FLIPS = tuple((fx, fy, fc) for fx in (0, 1) for fy in (0, 1) for fc in (0, 1))[1:]


def exchange_pieces_sc(name, gs, collective_id):
    n = len(gs)

    def body(*refs):
        ins, outs = refs[:n], refs[n:2 * n]
        send_sems, recv_sems = refs[2 * n:]
        x, y, c, me, sibling, chips = _place()
        my_dev = 4 * x + 2 * y + c
        flip = lambda v, f: 1 - v if f else v
        peers = [(flip(x, fx), flip(y, fy), flip(c, fc)) for fx, fy, fc in PEER_FLIPS]
        _handshake(peers)
        sends = []
        for r, (px, py, pc) in enumerate(peers):
            for i in range(n):
                cp = pltpu.make_async_remote_copy(
                    src_ref=_half_rows(ins[i], (2 * px + py,), pc), dst_ref=outs[i].at[my_dev], send_sem=send_sems.at[7 * i + r],
                    recv_sem=recv_sems.at[7 * i + r], device_id=(px, py, pc), device_id_type=MESH)
                cp.start()
                sends.append(cp)
        for r, (px, py, pc) in enumerate(peers):
            for i in range(n):
                pltpu.make_async_remote_copy(
                    src_ref=_half_rows(ins[i], (me,), c), dst_ref=outs[i].at[4 * px + 2 * py + pc], send_sem=send_sems.at[7 * i + r],
                    recv_sem=recv_sems.at[7 * i + r], device_id=(px, py, pc), device_id_type=MESH).wait_recv()
        for cp in sends:
            cp.wait_send()

    return pl.kernel(
        body, name=name, mesh=plsc.ScalarSubcoreMesh(axis_name="sequencer", num_cores=1),
        out_type=[jax.ShapeDtypeStruct((N_DEV, g.shape[1] // 2, g.shape[2]), g.dtype) for g in gs],
        scratch_types=[pltpu.SemaphoreType.DMA((7 * n,)), pltpu.SemaphoreType.DMA((7 * n,))],
        compiler_params=pltpu.CompilerParams(collective_id=collective_id),
    )(*gs)


def sibling_share_halves(name, fs):
    n = len(fs)
    every = (slice(None),)

    def body(*refs):
        ins, outs = refs[:n], refs[n:2 * n]
        send_sems, recv_sems = refs[2 * n:]
        x, y, c, me, sibling, chips = _place()
        sends = []
        for i in range(n):
            cp = pltpu.make_async_remote_copy(src_ref=_half_rows(ins[i], every, c), dst_ref=_half_rows(outs[i], every, c),
                                              send_sem=send_sems.at[i], recv_sem=recv_sems.at[i], device_id=sibling, device_id_type=MESH)
            cp.start()
            sends.append(cp)
        for i in range(n):
            pltpu.make_async_remote_copy(src_ref=_half_rows(ins[i], every, c), dst_ref=_half_rows(outs[i], every, 1 - c),
                                         send_sem=send_sems.at[i], recv_sem=recv_sems.at[i], device_id=sibling,
                                         device_id_type=MESH).wait_recv()
        for cp in sends:
            cp.wait_send()

    return pl.pallas_call(
        body, name=name, out_shape=[jax.ShapeDtypeStruct(f.shape, f.dtype) for f in fs],
        in_specs=[HBM_SPEC] * n, out_specs=[HBM_SPEC] * n, input_output_aliases={i: i for i in range(n)},
        scratch_shapes=[pltpu.SemaphoreType.DMA((n,)), pltpu.SemaphoreType.DMA((n,))],
    )(*fs)


def allgather_small_sc(name, v, collective_id):
    def body(v_ref, out_ref, send_sems, recv_sems, local_sem):
        x, y, c, me, sibling, chips = _place()
        my_dev = 4 * x + 2 * y + c
        flip = lambda a, f: 1 - a if f else a
        peers = [(flip(x, fx), flip(y, fy), flip(c, fc)) for fx, fy, fc in PEER_FLIPS]
        _handshake(peers)
        mine = pltpu.make_async_copy(v_ref, out_ref.at[my_dev], local_sem)
        mine.start()
        sends = []
        for r, peer in enumerate(peers):
            cp = pltpu.make_async_remote_copy(src_ref=v_ref, dst_ref=out_ref.at[my_dev], send_sem=send_sems.at[r],
                                              recv_sem=recv_sems.at[r], device_id=peer, device_id_type=MESH)
            cp.start()
            sends.append(cp)
        for r, (px, py, pc) in enumerate(peers):
            pltpu.make_async_remote_copy(src_ref=v_ref, dst_ref=out_ref.at[4 * px + 2 * py + pc], send_sem=send_sems.at[r],
                                         recv_sem=recv_sems.at[r], device_id=(px, py, pc), device_id_type=MESH).wait_recv()
        for cp in sends:
            cp.wait_send()
        mine.wait()

    return pl.kernel(
        body, name=name, mesh=plsc.ScalarSubcoreMesh(axis_name="sequencer", num_cores=1),
        out_type=jax.ShapeDtypeStruct((N_DEV,) + v.shape, v.dtype),
        scratch_types=[pltpu.SemaphoreType.DMA((7,)), pltpu.SemaphoreType.DMA((7,)), pltpu.SemaphoreType.DMA],
        compiler_params=pltpu.CompilerParams(collective_id=collective_id),
    )(v)


SUM_BLOCK_ELEMS = 512 * 1024


def sum_slabs(name, b):
    k, h, w = b.shape

    def body(b_ref, o_ref):
        acc = b_ref[0].astype(F32)
        for i in range(1, k):
            acc = acc + b_ref[i].astype(F32)
        o_ref[...] = acc

    return pl.pallas_call(
        body, name=name, out_shape=jax.ShapeDtypeStruct((h, w), F32),
        in_specs=[pl.BlockSpec(memory_space=pltpu.VMEM)], out_specs=pl.BlockSpec(memory_space=pltpu.VMEM),
        compiler_params=pltpu.CompilerParams(vmem_limit_bytes=VMEM_LIMIT),
    )(b)


def sum_pieces(name, pieces, gs):
    nl = len(pieces)
    k, h, w = pieces[0].shape
    tile = max(t for t in range(16, h + 1, 16) if h % t == 0 and (t * w <= SUM_BLOCK_ELEMS or t == 16))
    nt = h // tile
    x, y, c = lax.axis_index("x"), lax.axis_index("y"), lax.axis_index("c")
    place = [v.astype(jnp.int32).reshape(1) for v in (c, 2 * x + y, 4 * x + 2 * y + c)]

    assert nl == 2

    def tile_of(l, a, i):
        return i * a if l else i * (1 - a) + (nt - 1) * a

    def body(c_ref, me_ref, dev_ref, *refs):
        p_refs, g_refs, o_ref = refs[:nl], refs[nl:2 * nl], refs[2 * nl]
        my_dev = dev_ref[0]
        for l in range(nl):
            @pl.when(pl.program_id(0) == l)
            def _():
                o_ref[0] = jnp.zeros(o_ref.shape[1:], F32)
                for d in range(k):
                    @pl.when(my_dev == d)
                    def _():
                        o_ref[0] += g_refs[l][0].astype(F32)

                    @pl.when(my_dev != d)
                    def _():
                        o_ref[0] += p_refs[l][d].astype(F32)

    in_specs = [pl.BlockSpec((k, tile, w), functools.partial(lambda l, a, i, cc, me, dev: (0, tile_of(l, a, i), 0), l))
                for l in range(nl)]
    in_specs += [pl.BlockSpec((1, tile, w), functools.partial(lambda l, a, i, cc, me, dev: (me[0], cc[0] * nt + tile_of(l, a, i), 0), l))
                 for l in range(nl)]
    return pl.pallas_call(
        body, name=name, out_shape=jax.ShapeDtypeStruct((nl, 2 * h, w), F32),
        grid_spec=pltpu.PrefetchScalarGridSpec(
            num_scalar_prefetch=3, grid=(nl, nt), in_specs=in_specs,
            out_specs=pl.BlockSpec((1, tile, w), lambda a, i, cc, me, dev: (a, cc[0] * nt + i, 0))),
        compiler_params=_params("arbitrary", "arbitrary"),
    )(*place, *pieces, *gs)


def _adam_block(w, g, m, v):
    m = ADAM_B1 * m + (1.0 - ADAM_B1) * g
    v = ADAM_B2 * v + (1.0 - ADAM_B2) * (g * g)
    m_hat = m / (1.0 - ADAM_B1 ** ADAM_STEP)
    v_hat = v / (1.0 - ADAM_B2 ** ADAM_STEP)
    return -ADAM_LR * (m_hat / (jnp.sqrt(v_hat) + ADAM_EPS) + ADAM_WD * w), m, v


def adamw(name, w, g, m, v):
    shape = w.shape
    cols = shape[-1]
    rows = w.size // cols
    tile = 128 if rows % 128 == 0 else rows
    flat = [a.reshape(rows, cols) for a in (w, g, m, v)]

    def body(w_ref, g_ref, m_ref, v_ref, d_ref, nm_ref, nv_ref):
        d_ref[...], nm_ref[...], nv_ref[...] = _adam_block(w_ref[...], g_ref[...], m_ref[...], v_ref[...])

    blk = pl.BlockSpec((tile, cols), lambda i: (i, 0))
    res = pl.pallas_call(
        body, name=name, grid=(rows // tile,), in_specs=[blk] * 4, out_specs=[blk] * 3,
        out_shape=[jax.ShapeDtypeStruct((rows, cols), F32)] * 3, compiler_params=_params("parallel"),
    )(*flat)
    return tuple(r.reshape(shape) for r in res)


def _pack_small(values):
    flat = jnp.concatenate([v.reshape(-1) for v in values.values()])
    n = flat.shape[0]
    total = -(-n // (8 * LANES)) * (8 * LANES)
    return jnp.pad(flat, (0, total - n)).reshape(-1, LANES)


def _unpack_small(v, shapes):
    flat = v.reshape(-1)
    out, off = {}, 0
    for n, shape in shapes.items():
        sz = int(np.prod(shape))
        out[n] = flat[off:off + sz].reshape(shape)
        off += sz
    return out


def kernel(x, ffn1_norm, ffn1_w_gate, ffn1_w_up, ffn1_w_down, mix_norm, w_in, rg_conv_w, rg_conv_b, rg_w_r, rg_b_r, rg_w_i, rg_b_i, rg_lambda, att_q_norm, att_k_norm, dn_conv_w, dn_a_log, dn_dt_bias, dn_out_norm, w_branch, w_out, ffn2_norm, ffn2_w_gate, ffn2_w_up, ffn2_w_down, loss_target, m_ffn1_norm, m_ffn1_w_gate, m_ffn1_w_up, m_ffn1_w_down, m_mix_norm, m_w_in, m_rg_conv_w, m_rg_conv_b, m_rg_w_r, m_rg_b_r, m_rg_w_i, m_rg_b_i, m_rg_lambda, m_att_q_norm, m_att_k_norm, m_dn_conv_w, m_dn_a_log, m_dn_dt_bias, m_dn_out_norm, m_w_branch, m_w_out, m_ffn2_norm, m_ffn2_w_gate, m_ffn2_w_up, m_ffn2_w_down, v_ffn1_norm, v_ffn1_w_gate, v_ffn1_w_up, v_ffn1_w_down, v_mix_norm, v_w_in, v_rg_conv_w, v_rg_conv_b, v_rg_w_r, v_rg_b_r, v_rg_w_i, v_rg_b_i, v_rg_lambda, v_att_q_norm, v_att_k_norm, v_dn_conv_w, v_dn_a_log, v_dn_dt_bias, v_dn_out_norm, v_w_branch, v_w_out, v_ffn2_norm, v_ffn2_w_gate, v_ffn2_w_up, v_ffn2_w_down):
    given = dict(locals())
    for n in TRANSPOSED_MATS:
        for pre in ("", "m_", "v_"):
            given[pre + n] = jnp.swapaxes(given[pre + n], 1, 2)
    small = {n: given[n] for n in SMALL_NAMES}
    n_layers = ffn1_norm.shape[0]
    mat_names = [n for n, _ in MATRICES]
    conv_names = [n for n, _ in CONVS]

    blocks = {}
    for n in mat_names:
        for l, b in enumerate(cast_into_blocks("cast_" + n, given[n])):
            blocks[n, l] = b
    first, done = {}, []
    for i, wanted in enumerate(GATHER_ORDER[:-1]):
        bufs, _ = lax.optimization_barrier(([blocks[k] for k in wanted], done))
        done = allgather_blocks_sc(f"allgather_{i}", bufs, GATHER_IDS[i])
        first.update(zip(wanted, done))
    rest = {k: b for k, b in blocks.items() if k not in first}
    taps = jnp.concatenate([given[n].reshape(-1) for n in conv_names]).reshape(-1, LANES)
    taps = allgather_small_sc("allgather_taps", taps, 8).reshape(N_CHIPS, 2, -1)[:, 0]
    conv, off = {}, 0
    for n, ax in CONVS:
        sz = given[n].size
        conv[n] = _shard_minor(taps[:, off:off + sz].reshape((N_CHIPS,) + given[n].shape), ax)
        off += sz
    p = [layer_small(small, l) for l in range(n_layers)]

    ffn1_keys = GATHER_ORDER[0]
    (x1, pend), first_vjp = jax.vjp(first_ffn, *[first[k] for k in ffn1_keys], p[0]["ffn1_norm"], x[0])
    keys = list(rest)
    bufs, pend, second = lax.optimization_barrier(([rest[k] for k in keys], pend, [first[k] for k in GATHER_ORDER[1]]))
    gathered = dict(zip(keys, allgather_blocks_sc("allgather_2", bufs, GATHER_IDS[2])))
    gathered.update(zip(GATHER_ORDER[1], second))
    loss, (g_mats, g_conv, gp, gx1, gpend) = jax.value_and_grad(rest_of_step, argnums=(0, 1, 2, 3, 4))(
        gathered, conv, p, x1, pend, loss_target[0])
    *g_ffn1, gp[0]["ffn1_norm"], gx = first_vjp((gx1, gpend))
    g_mats.update(zip(ffn1_keys, g_ffn1))

    pieces = {}
    for i, group in enumerate(EXCHANGE_GROUPS):
        keys = [k for k in g_mats if group(*k)]
        pieces.update(zip(keys, exchange_pieces_sc(f"exchange_{i}", [g_mats[k] for k in keys], 2 + i)))
    halves = {n: sum_pieces("sum_" + n, [pieces[n, l] for l in range(n_layers)], [g_mats[n, l] for l in range(n_layers)])
              for n in mat_names}
    grads = {}
    for tag, names in (("late", [n for n in mat_names if n in LATE_MATS]), ("early", [n for n in mat_names if n not in LATE_MATS])):
        grads.update(zip(names, sibling_share_halves("share_" + tag, [halves[n] for n in names])))

    g_small = dict(layer_small_grads(gp, small), **g_conv, loss=loss.reshape(1))
    packed_small = _pack_small(g_small)
    slabs = allgather_small_sc("allgather_small", packed_small, 9)
    summed =_unpack_small(sum_slabs("sum_small", slabs), {n: g.shape for n, g in g_small.items()})
    chip = 2 * lax.axis_index("x") + lax.axis_index("y")
    for n in SMALL_NAMES:
        grads[n] = summed[n]
    for n, ax in CONVS:
        s = given[n].shape[ax]
        grads[n] = lax.dynamic_slice_in_dim(summed[n], chip * s, s, axis=ax)

    upd = {n: adamw("adamw_" + n, given[n], grads[n], given["m_" + n], given["v_" + n]) for n in WEIGHT_NAMES}
    out = lambda n, a: jnp.swapaxes(a, 1, 2) if n in TRANSPOSED_MATS else a
    return (summed["loss"][0], gx[None], *[out(n, grads[n]) for n in WEIGHT_NAMES], *[out(n, upd[n][0]) for n in WEIGHT_NAMES],
            *[out(n, upd[n][1]) for n in WEIGHT_NAMES], *[out(n, upd[n][2]) for n in WEIGHT_NAMES])
```

```python
import functools
import math

import jax
import jax.numpy as jnp
import numpy as np
from jax import lax
from jax.experimental import pallas as pl
from jax.experimental.pallas import tpu as pltpu
from jax.experimental.pallas import tpu_sc as plsc

F32 = jnp.float32
BF16 = jnp.bfloat16
MESH = pl.DeviceIdType.MESH

D_MODEL = 1024
FFN_DIM = 2816
RG_C = 8.0
ATT_GROUPS = ((128, 1), (512, 4), (2048, 16))
ATT_HEADS = 12
ATT_HEAD_DIM = 64
ATT_SPAN = 128
DN_HEADS = 8
DN_HEAD_DIM = 128
DN_CHUNK = 64
EPS = 1e-6
NEG_INF = -1e30
N_CHIPS = 4
N_DEV = 8

ADAM_LR, ADAM_B1, ADAM_B2, ADAM_EPS, ADAM_WD, ADAM_STEP = 0.001, 0.9, 0.999, 1e-08, 0.01, 10

LANES = 128
VMEM_LIMIT = 56 * 1024 * 1024


def _params(*sem):
    return pltpu.CompilerParams(dimension_semantics=sem or None, vmem_limit_bytes=VMEM_LIMIT)


def _sigmoid(x):
    return 1.0 / (1.0 + jnp.exp(-x))


def _silu(x):
    return x * _sigmoid(x)


def _softplus(x):
    return jnp.maximum(x, 0.0) + jnp.log(1.0 + jnp.exp(-jnp.abs(x)))


def _gelu(x):
    return 0.5 * x * (1.0 + jnp.tanh(math.sqrt(2.0 / math.pi) * (x + 0.044715 * (x * x * x))))


def _neg_expm1(x):
    series = -x * (1.0 + x * (0.5 + x * (1.0 / 6 + x * (1.0 / 24 + x * (1.0 / 120 + x * (1.0 / 720))))))
    return jnp.where(x > -0.25, series, 1.0 - jnp.exp(x))


def _rms(x, g):
    return x * lax.rsqrt(jnp.mean(x * x, axis=-1, keepdims=True) + EPS) * g


_MM_DIMS = {"nn": (((1,), (0,)), ((), ())), "nt": (((1,), (1,)), ((), ())), "tn": (((0,), (0,)), ((), ()))}


def _split(a):
    hi = a.astype(BF16)
    return hi, (a - hi.astype(F32)).astype(BF16)


def _mxu(a, b, form, passes):
    (ca, cb), _ = _MM_DIMS[form]
    if a.ndim == 3:
        dims = (((ca[0] + 1,), (cb[0] + 1,)), ((0,), (0,)))
    else:
        dims = _MM_DIMS[form]
    dg = lambda p, q: lax.dot_general(p, q, dims, preferred_element_type=F32)
    if passes == 1:
        return dg(a.astype(BF16), b.astype(BF16))
    (a_hi, a_lo), (b_hi, b_lo) = _split(a), _split(b)
    return dg(a_hi, b_hi) + (dg(a_hi, b_lo) + dg(a_lo, b_hi))


@functools.partial(jax.custom_vjp, nondiff_argnums=(2, 3))
def _mm(a, b, form, passes):
    return _mxu(a, b, form, passes)


def _mm_fwd(a, b, form, passes):
    return _mxu(a, b, form, passes), (a, b)


def _mm_bwd(form, passes, res, g):
    a, b = res
    if form == "nn":
        return _mm(g, b, "nt", passes), _mm(a, g, "tn", passes)
    if form == "nt":
        return _mm(g, b, "nn", passes), _mm(g, a, "tn", passes)
    return _mm(b, g, "nt", passes), _mm(a, g, "nn", passes)


_mm.defvjp(_mm_fwd, _mm_bwd)


def _dot(a, b):
    return _mm(a, b, "nn", 1)


def _dot_nt(a, b):
    return _mm(a, b, "nt", 1)


def _dot_tn(a, b):
    return _mm(a, b, "tn", 1)


def _dot3(a, b):
    return _mm(a, b, "nn", 3)


def _rows(shape):
    return lax.broadcasted_iota(jnp.int32, shape, len(shape) - 2)


def _roll_down(x, s, fill):
    return jnp.where(_rows(x.shape) >= s, pltpu.roll(x, s, x.ndim - 2), fill)


def _roll_up(x, s, fill):
    n = x.shape[-2]
    return jnp.where(_rows(x.shape) < n - s, pltpu.roll(x, n - s, x.ndim - 2), fill)


@functools.partial(jax.custom_vjp, nondiff_argnums=(1,))
def _shift(x, s):
    return _roll_down(x, s, 0.0)


def _shift_fwd(x, s):
    return _roll_down(x, s, 0.0), None


def _shift_bwd(s, _, g):
    return (_roll_up(g, s, 0.0),)


_shift.defvjp(_shift_fwd, _shift_bwd)


def _causal_conv(x, w):
    return w[0:1] * _shift(x, 3) + w[1:2] * _shift(x, 2) + w[2:3] * _shift(x, 1) + w[3:4] * x


@jax.custom_vjp
def _lin_scan(a, b):
    return _lin_scan_fwd(a, b)[0]


def _lin_scan_fwd(a, b):
    a0 = a
    s = 1
    while s < a.shape[0]:
        b = a * _roll_down(b, s, 0.0) + b
        a = a * _roll_down(a, s, 1.0)
        s *= 2
    return b, (a0, b)


def _lin_scan_bwd(res, g):
    a, h = res
    c = _roll_up(a, 1, 0.0)
    s = 1
    while s < a.shape[0]:
        g = c * _roll_up(g, s, 0.0) + g
        c = c * _roll_up(c, s, 1.0)
        s *= 2
    return g * _roll_down(h, 1, 0.0), g


_lin_scan.defvjp(_lin_scan_fwd, _lin_scan_bwd)


@jax.custom_vjp
def _cumsum_rows(x):
    s = 1
    while s < x.shape[-2]:
        x = x + _roll_down(x, s, 0.0)
        s *= 2
    return x


def _cumsum_rows_fwd(x):
    return _cumsum_rows(x), None


def _cumsum_rows_bwd(_, g):
    s = 1
    while s < g.shape[-2]:
        g = g + _roll_up(g, s, 0.0)
        s *= 2
    return (g,)


_cumsum_rows.defvjp(_cumsum_rows_fwd, _cumsum_rows_bwd)


ROW_BLOCK_BYTES = 14 * 1024 * 1024


def _row_tile(t, width=0):
    for tile in (512, 256):
        if t % tile == 0 and (tile == 256 or tile * width * 4 <= ROW_BLOCK_BYTES):
            return tile
    return t


def _rowwise_fwd_call(name, f, rows, pars, tile):
    t = rows[0].shape[0]
    outs = jax.eval_shape(f, *[jax.ShapeDtypeStruct((tile, r.shape[1]), F32) for r in rows],
                          *[jax.ShapeDtypeStruct(p.shape, F32) for p in pars])
    nr, npar = len(rows), len(pars)

    def body(*refs):
        ins = [r[...] for r in refs[:nr + npar]]
        res = f(*ins)
        for o_ref, o in zip(refs[nr + npar:], res):
            o_ref[...] = o.astype(o_ref.dtype)

    return pl.pallas_call(
        body, name=name, grid=(t // tile,),
        in_specs=[pl.BlockSpec((tile, r.shape[1]), lambda i: (i, 0)) for r in rows]
        + [pl.BlockSpec(p.shape, lambda i: (0, 0)) for p in pars],
        out_specs=[pl.BlockSpec((tile, o.shape[1]), lambda i: (i, 0)) for o in outs],
        out_shape=[jax.ShapeDtypeStruct((t, o.shape[1]), F32) for o in outs],
        compiler_params=_params("parallel"),
    )(*rows, *pars)


def _rowwise_bwd_call(name, f, rows, pars, cts, tile):
    t = rows[0].shape[0]
    nr, npar, nct = len(rows), len(pars), len(cts)

    def body(*refs):
        ins = [r[...] for r in refs[:nr + npar]]
        gs = tuple(r[...] for r in refs[nr + npar:nr + npar + nct])
        outs = refs[nr + npar + nct:]
        _, vjp = jax.vjp(f, *ins)
        d = vjp(gs)
        for o_ref, v in zip(outs[:nr], d[:nr]):
            o_ref[...] = v

        @pl.when(pl.program_id(0) == 0)
        def _():
            for o_ref in outs[nr:]:
                o_ref[...] = jnp.zeros_like(o_ref)

        for o_ref, v in zip(outs[nr:], d[nr:]):
            o_ref[...] += v

    res = pl.pallas_call(
        body, name=name, grid=(t // tile,),
        in_specs=[pl.BlockSpec((tile, r.shape[1]), lambda i: (i, 0)) for r in rows]
        + [pl.BlockSpec(p.shape, lambda i: (0, 0)) for p in pars]
        + [pl.BlockSpec((tile, c.shape[1]), lambda i: (i, 0)) for c in cts],
        out_specs=[pl.BlockSpec((tile, r.shape[1]), lambda i: (i, 0)) for r in rows]
        + [pl.BlockSpec(p.shape, lambda i: (0, 0)) for p in pars],
        out_shape=[jax.ShapeDtypeStruct(r.shape, F32) for r in rows]
        + [jax.ShapeDtypeStruct(p.shape, F32) for p in pars],
        compiler_params=_params("arbitrary"),
    )(*rows, *pars, *cts)
    return tuple(res[:nr]), tuple(res[nr:])


def rowwise(name, f, rows, pars=()):
    outs = jax.eval_shape(f, *[jax.ShapeDtypeStruct((8, r.shape[1]), F32) for r in rows],
                          *[jax.ShapeDtypeStruct(p.shape, F32) for p in pars])
    tile = _row_tile(rows[0].shape[0], 2 * sum(r.shape[1] for r in rows) + sum(o.shape[1] for o in outs))

    @jax.custom_vjp
    def op(rows, pars):
        return tuple(_rowwise_fwd_call(name, f, rows, pars, tile))

    def op_fwd(rows, pars):
        return op(rows, pars), (rows, pars)

    def op_bwd(res, cts):
        return _rowwise_bwd_call(name + "_bwd", f, res[0], res[1], tuple(cts), tile)

    op.defvjp(op_fwd, op_bwd)
    return op(tuple(rows), tuple(pars))


MM_TM = 1024


def _tile_of(n, cap):
    best = None
    for c in range(LANES, min(n, cap) + 1, LANES):
        if n % c == 0:
            best = c
    return best or n


def _proj_dw(name, h, dys):
    m, k = h.shape
    n, tm = len(dys), 256
    steps = m // tm

    def body(h_ref, *refs):
        dy_refs, o_refs, accs = refs[:n], refs[n:2 * n], refs[2 * n:]
        ht = jnp.transpose(h_ref[...]).astype(BF16)
        first = pl.program_id(0) == 0
        for dy_ref, acc in zip(dy_refs, accs):
            for c0 in range(0, acc.shape[1], 1024):
                cols = slice(c0, min(c0 + 1024, acc.shape[1]))
                part = _dot(ht, dy_ref[:, cols])

                @pl.when(first)
                def _():
                    acc[:, cols] = part

                @pl.when(jnp.logical_not(first))
                def _():
                    acc[:, cols] += part

        @pl.when(pl.program_id(0) == steps - 1)
        def _():
            for o_ref, acc in zip(o_refs, accs):
                o_ref[...] = acc[...].astype(BF16)

    row = lambda width: pl.BlockSpec((tm, width), lambda i: (i, 0))
    return pl.pallas_call(
        body, name=name, grid=(steps,),
        in_specs=[row(k)] + [row(d.shape[1]) for d in dys],
        out_specs=[pl.BlockSpec((k, d.shape[1]), lambda i: (0, 0)) for d in dys],
        out_shape=[jax.ShapeDtypeStruct((k, d.shape[1]), BF16) for d in dys],
        scratch_shapes=[pltpu.VMEM((k, d.shape[1]), F32) for d in dys],
        compiler_params=_params("arbitrary"),
    )(h, *dys)


PROJ_GROUP_COLS = 4608


def _proj_dh(name, dys, ws, acc):
    m, k = dys[0].shape[0], ws[0].shape[0]
    n, tm = len(dys), 256

    def body(*refs):
        dy_refs, w_refs, rest = refs[:n], refs[n:2 * n], refs[2 * n:]
        total = _dot_nt(dy_refs[0][...], w_refs[0][...])
        for dy_ref, w_ref in zip(dy_refs[1:], w_refs[1:]):
            total = total + _dot_nt(dy_ref[...], w_ref[...])
        if acc is not None:
            total = total + rest[0][...]
        rest[-1][...] = total

    row = lambda width: pl.BlockSpec((tm, width), lambda i: (i, 0))
    return pl.pallas_call(
        body, name=name, grid=(m // tm,),
        in_specs=[row(d.shape[1]) for d in dys] + [pl.BlockSpec(w.shape, lambda i: (0, 0)) for w in ws] + ([row(k)] if acc is not None else []),
        out_specs=row(k), out_shape=jax.ShapeDtypeStruct((m, k), F32), compiler_params=_params("parallel"),
    )(*dys, *ws, *([acc] if acc is not None else []))


def _proj_fwd(name, h, ws):
    m, k = h.shape
    n, tm = len(ws), 256

    def body(h_ref, *refs):
        hv = h_ref[...].astype(BF16)
        for w_ref, o_ref in zip(refs[:n], refs[n:]):
            o_ref[...] = _dot(hv, w_ref[...])

    row = lambda width: pl.BlockSpec((tm, width), lambda i: (i, 0))
    return pl.pallas_call(
        body, name=name, grid=(m // tm,),
        in_specs=[row(k)] + [pl.BlockSpec(w.shape, lambda i: (0, 0)) for w in ws],
        out_specs=[row(w.shape[1]) for w in ws],
        out_shape=[jax.ShapeDtypeStruct((m, w.shape[1]), F32) for w in ws], compiler_params=_params("parallel"),
    )(h, *ws)


def project_in(name, h, ws):
    keys = list(ws)
    groups, cols = [[]], 0
    for p in keys:
        if groups[-1] and cols + ws[p].shape[1] > PROJ_GROUP_COLS:
            groups.append([])
            cols = 0
        groups[-1].append(p)
        cols += ws[p].shape[1]

    @jax.custom_vjp
    def op(h, ws):
        out = {}
        for i, group in enumerate(groups):
            out.update(zip(group, _proj_fwd(f"{name}_{i}", h, [ws[p] for p in group])))
        return out

    def op_fwd(h, ws):
        return op(h, ws), (h, ws)

    def op_bwd(res, dys):
        h, ws = res
        dh, dws = None, {}
        for i, group in enumerate(groups):
            dh = _proj_dh(f"{name}_dh{i}", [dys[p] for p in group], [ws[p] for p in group], dh)
            dws.update(zip(group, _proj_dw(f"{name}_dw{i}", h, [dys[p] for p in group])))
        return dh, dws

    op.defvjp(op_fwd, op_bwd)
    return op(h, ws)


def _ffn_up(name, h, wt):
    m, k = h.shape
    j, n, _ = wt.shape
    tm = MM_TM

    def body(h_ref, w_ref, o_ref):
        o_ref[0] = _dot_nt(h_ref[...], w_ref[0])

    return pl.pallas_call(
        body, name=name, grid=(m // tm, j),
        in_specs=[pl.BlockSpec((tm, k), lambda i, b: (i, 0)), pl.BlockSpec((1, n, k), lambda i, b: (b, 0, 0))],
        out_specs=pl.BlockSpec((1, tm, n), lambda i, b: (b, i, 0)),
        out_shape=jax.ShapeDtypeStruct((j, m, n), F32), compiler_params=_params("parallel", "parallel"),
    )(h, wt)


def _ffn_down(name, g, u, wd):
    j, m, n = g.shape
    d = wd.shape[2]
    tm = MM_TM

    def body(g_ref, u_ref, w_ref, o_ref):
        part = _dot(_silu(g_ref[0]) * u_ref[0], w_ref[0])

        @pl.when(pl.program_id(1) == 0)
        def _():
            o_ref[...] = part

        @pl.when(pl.program_id(1) > 0)
        def _():
            o_ref[...] += part

    act = pl.BlockSpec((1, tm, n), lambda i, b: (b, i, 0))
    return pl.pallas_call(
        body, name=name, grid=(m // tm, j),
        in_specs=[act, act, pl.BlockSpec((1, n, d), lambda i, b: (b, 0, 0))],
        out_specs=pl.BlockSpec((tm, d), lambda i, b: (i, 0)),
        out_shape=jax.ShapeDtypeStruct((m, d), F32), compiler_params=_params("parallel", "arbitrary"),
    )(g, u, wd)


def _ffn_down_bwd(name, dy, g, u, wd):
    j, m, n = g.shape
    d = wd.shape[2]
    tm = MM_TM

    def body(dy_ref, g_ref, u_ref, w_ref, dg_ref, du_ref):
        da = _dot_nt(dy_ref[...], w_ref[0])
        gv = g_ref[0]
        s = _sigmoid(gv)
        dg_ref[0] = da * u_ref[0] * (s * (1.0 + gv * (1.0 - s)))
        du_ref[0] = da * (gv * s)

    act = pl.BlockSpec((1, tm, n), lambda i, b: (b, i, 0))
    return pl.pallas_call(
        body, name=name, grid=(m // tm, j),
        in_specs=[pl.BlockSpec((tm, d), lambda i, b: (i, 0)), act, act, pl.BlockSpec((1, n, d), lambda i, b: (b, 0, 0))],
        out_specs=[act, act], out_shape=[jax.ShapeDtypeStruct((j, m, n), F32)] * 2,
        compiler_params=_params("parallel", "parallel"),
    )(dy, g, u, wd)


def _ffn_down_dw(name, g, u, dy):
    j, m, n = g.shape
    d = dy.shape[1]
    tn = _tile_of(d, 1024)

    def body(g_ref, u_ref, dy_ref, o_ref):
        o_ref[0] = _dot_tn(_silu(g_ref[0]) * u_ref[0], dy_ref[...]).astype(BF16)

    act = pl.BlockSpec((1, m, n), lambda b, c: (b, 0, 0))
    return pl.pallas_call(
        body, name=name, grid=(j, d // tn),
        in_specs=[act, act, pl.BlockSpec((m, tn), lambda b, c: (0, c))],
        out_specs=pl.BlockSpec((1, n, tn), lambda b, c: (b, 0, c)),
        out_shape=jax.ShapeDtypeStruct((j, n, d), BF16), compiler_params=_params("parallel", "parallel"),
    )(g, u, dy)


def _ffn_up_dh(name, dg, du, wg, wu):
    j, m, n = dg.shape
    k = wg.shape[2]
    tm = MM_TM

    def body(dg_ref, du_ref, wg_ref, wu_ref, o_ref):
        part = _dot(dg_ref[0], wg_ref[0]) + _dot(du_ref[0], wu_ref[0])

        @pl.when(pl.program_id(1) == 0)
        def _():
            o_ref[...] = part

        @pl.when(pl.program_id(1) > 0)
        def _():
            o_ref[...] += part

    act = pl.BlockSpec((1, tm, n), lambda i, b: (b, i, 0))
    wsp = pl.BlockSpec((1, n, k), lambda i, b: (b, 0, 0))
    return pl.pallas_call(
        body, name=name, grid=(m // tm, j), in_specs=[act, act, wsp, wsp],
        out_specs=pl.BlockSpec((tm, k), lambda i, b: (i, 0)),
        out_shape=jax.ShapeDtypeStruct((m, k), F32), compiler_params=_params("parallel", "arbitrary"),
    )(dg, du, wg, wu)


def _ffn_up_dw(name, dy, h):
    j, m, n = dy.shape
    k = h.shape[1]
    tk = _tile_of(k, 1024)

    def body(dy_ref, h_ref, o_ref):
        o_ref[0] = _dot_tn(dy_ref[0], h_ref[...]).astype(BF16)

    return pl.pallas_call(
        body, name=name, grid=(j, k // tk),
        in_specs=[pl.BlockSpec((1, m, n), lambda b, i: (b, 0, 0)), pl.BlockSpec((m, tk), lambda b, i: (0, i))],
        out_specs=pl.BlockSpec((1, n, tk), lambda b, i: (b, 0, i)),
        out_shape=jax.ShapeDtypeStruct((j, n, k), BF16), compiler_params=_params("parallel", "parallel"),
    )(dy, h)


def ffn(name, h, wg, wu, wd):
    @jax.custom_vjp
    def op(h, wg, wu, wd):
        return _ffn_down(name + "_d", _ffn_up(name + "_g", h, wg), _ffn_up(name + "_u", h, wu), wd)

    def op_fwd(h, wg, wu, wd):
        g, u = _ffn_up(name + "_g", h, wg), _ffn_up(name + "_u", h, wu)
        return _ffn_down(name + "_d", g, u, wd), (h, g, u, wg, wu, wd)

    def op_bwd(res, dy):
        h, g, u, wg, wu, wd = res
        dg, du = _ffn_down_bwd(name + "_d_bwd", dy, g, u, wd)
        return (_ffn_up_dh(name + "_dh", dg, du, wg, wu), _ffn_up_dw(name + "_g_dw", dg, h), _ffn_up_dw(name + "_u_dw", du, h),
                _ffn_down_dw(name + "_d_dw", g, u, dy))

    op.defvjp(op_fwd, op_bwd)
    return op(h, wg, wu, wd)


def _mmr_fwd(name, a, w):
    j, m, n = a.shape
    nn = w.shape[2]
    tm, tn = MM_TM, _tile_of(nn, 1024)

    def body(a_ref, w_ref, o_ref):
        part = _dot(a_ref[0], w_ref[0])

        @pl.when(pl.program_id(2) == 0)
        def _():
            o_ref[...] = part

        @pl.when(pl.program_id(2) > 0)
        def _():
            o_ref[...] += part

    return pl.pallas_call(
        body, name=name, grid=(m // tm, nn // tn, j),
        in_specs=[pl.BlockSpec((1, tm, n), lambda i, c, b: (b, i, 0)), pl.BlockSpec((1, n, tn), lambda i, c, b: (b, 0, c))],
        out_specs=pl.BlockSpec((tm, tn), lambda i, c, b: (i, c)),
        out_shape=jax.ShapeDtypeStruct((m, nn), F32),
        compiler_params=_params("parallel", "parallel", "arbitrary"),
    )(a, w)


def _mmr_da(name, dy, w):
    m, nn = dy.shape
    j, n, _ = w.shape
    tm = MM_TM

    def body(dy_ref, w_ref, o_ref):
        o_ref[0] = _dot_nt(dy_ref[...], w_ref[0])

    return pl.pallas_call(
        body, name=name, grid=(m // tm, j),
        in_specs=[pl.BlockSpec((tm, nn), lambda i, b: (i, 0)), pl.BlockSpec((1, n, nn), lambda i, b: (b, 0, 0))],
        out_specs=pl.BlockSpec((1, tm, n), lambda i, b: (b, i, 0)),
        out_shape=jax.ShapeDtypeStruct((j, m, n), F32),
        compiler_params=_params("parallel", "parallel"),
    )(dy, w)


def _mmr_dw(name, a, dy):
    j, m, n = a.shape
    nn = dy.shape[1]
    tn = _tile_of(nn, 512)

    def body(a_ref, dy_ref, o_ref):
        o_ref[0] = _dot_tn(a_ref[0], dy_ref[...]).astype(BF16)

    return pl.pallas_call(
        body, name=name, grid=(j, nn // tn),
        in_specs=[pl.BlockSpec((1, m, n), lambda b, c: (b, 0, 0)), pl.BlockSpec((m, tn), lambda b, c: (0, c))],
        out_specs=pl.BlockSpec((1, n, tn), lambda b, c: (b, 0, c)),
        out_shape=jax.ShapeDtypeStruct((j, n, nn), BF16),
        compiler_params=_params("parallel", "parallel"),
    )(a, dy)


def mm_rows(name, a, w):
    @jax.custom_vjp
    def op(a, w):
        return _mmr_fwd(name, a, w)

    def op_fwd(a, w):
        return op(a, w), (a, w)

    def op_bwd(res, dy):
        a, w = res
        return _mmr_da(name + "_da", dy, w), _mmr_dw(name + "_dw", a, dy)

    op.defvjp(op_fwd, op_bwd)
    return op(a, w)


def _colwise_specs(cols, pars, par_block):
    t = cols[0].shape[0]
    specs = [pl.BlockSpec((t, LANES), lambda j: (0, j)) for _ in cols]
    for p, blk in zip(pars, par_block):
        if blk == "lane":
            specs.append(pl.BlockSpec((p.shape[0], LANES), lambda j: (0, j)))
        else:
            specs.append(pl.BlockSpec((1,) + p.shape[1:], lambda j: (j, 0, 0)))
    return specs


def _colwise_fwd_call(name, f, cols, pars, par_block, n_out):
    t, c = cols[0].shape
    nc, npar = len(cols), len(pars)

    def body(*refs):
        ins = [r[...] for r in refs[:nc]] + [r[...] if b == "lane" else r[0] for r, b in zip(refs[nc:nc + npar], par_block)]
        res = f(*ins)
        for o_ref, o in zip(refs[nc + npar:], res):
            o_ref[...] = o

    return pl.pallas_call(
        body, name=name, grid=(c // LANES,),
        in_specs=_colwise_specs(cols, pars, par_block),
        out_specs=[pl.BlockSpec((t, LANES), lambda j: (0, j)) for _ in range(n_out)],
        out_shape=[jax.ShapeDtypeStruct((t, c), F32) for _ in range(n_out)],
        compiler_params=_params("parallel"),
    )(*cols, *pars)


def _colwise_bwd_call(name, f, cols, pars, par_block, cts):
    t, c = cols[0].shape
    nc, npar, nct = len(cols), len(pars), len(cts)

    def body(*refs):
        ins = [r[...] for r in refs[:nc]] + [r[...] if b == "lane" else r[0] for r, b in zip(refs[nc:nc + npar], par_block)]
        gs = tuple(r[...] for r in refs[nc + npar:nc + npar + nct])
        outs = refs[nc + npar + nct:]
        _, vjp = jax.vjp(f, *ins)
        d = vjp(gs)
        for o_ref, v in zip(outs[:nc], d[:nc]):
            o_ref[...] = v
        for o_ref, v, b in zip(outs[nc:], d[nc:], par_block):
            if b == "lane":
                o_ref[...] = v
            else:
                o_ref[0] = v

    res = pl.pallas_call(
        body, name=name, grid=(c // LANES,),
        in_specs=_colwise_specs(cols, pars, par_block) + [pl.BlockSpec((t, LANES), lambda j: (0, j)) for _ in cts],
        out_specs=_colwise_specs(cols, pars, par_block),
        out_shape=[jax.ShapeDtypeStruct(v.shape, F32) for v in (*cols, *pars)],
        compiler_params=_params("parallel"),
    )(*cols, *pars, *cts)
    return tuple(res[:nc]), tuple(res[nc:])


def colwise(name, f, cols, pars, par_block, n_out):
    @jax.custom_vjp
    def op(cols, pars):
        return tuple(_colwise_fwd_call(name, f, cols, pars, par_block, n_out))

    def op_fwd(cols, pars):
        return op(cols, pars), (cols, pars)

    def op_bwd(res, cts):
        return _colwise_bwd_call(name + "_bwd", f, res[0], res[1], par_block, tuple(cts))

    op.defvjp(op_fwd, op_bwd)
    return op(tuple(cols), tuple(pars))


def _rg_block(x, gate, cw, cb, wr, br, wi, bi, lam):
    xa = _causal_conv(x, cw) + cb
    r = _sigmoid(_dot(xa, wr) + br)
    i = _sigmoid(_dot(xa, wi) + bi)
    log_a = -RG_C * r * _softplus(-lam)
    a = jnp.exp(log_a)
    b = jnp.sqrt(_neg_expm1(2.0 * log_a)) * (i * xa)
    return (_lin_scan(a, b) * _gelu(gate),)


def _dn_conv_block(mode):
    def f(x, cw):
        c = _silu(_causal_conv(x, cw))
        if mode == "v":
            return (c,)
        c = c * lax.rsqrt(jnp.sum(c * c, axis=-1, keepdims=True) + EPS)
        return (c * (DN_HEAD_DIM ** -0.5),) if mode == "q" else (c,)
    return f


def _block_diag(w):
    w = w.reshape(8, 2, 64, 64)
    z = jnp.zeros((8, 64, 64), w.dtype)
    top = jnp.concatenate([w[:, 0], z], axis=2)
    bot = jnp.concatenate([z, w[:, 1]], axis=2)
    return jnp.concatenate([top, bot], axis=1)


DN_HP = 8


def _dn_block(S, qw, kw, vw, gb, h0, tinv=None):
    hp, hd = S.shape[0], DN_HEAD_DIM
    heads = lambda a: jnp.concatenate([a[None, :, j * hd:(j + 1) * hd] for j in range(hp)], axis=0)
    lane = lax.broadcasted_iota(jnp.int32, gb.shape, 1)
    col = lambda i: jnp.sum(jnp.where(lane == i, gb, 0.0), axis=1, keepdims=True)[None]
    beta = jnp.concatenate([col(h0 + j) for j in range(hp)], axis=0)
    g = jnp.concatenate([col(h0 + j + DN_HEADS) for j in range(hp)], axis=0)
    s_new, o, tinv = _dn_step(S, heads(qw), heads(kw), heads(vw), beta, g, tinv)
    return s_new, jnp.concatenate([o[j:j + 1].reshape(o.shape[1:]) for j in range(hp)], axis=1), tinv


@jax.custom_vjp
def _unit_lower_inverse(a):
    c = a.shape[-1]
    eye = (lax.broadcasted_iota(jnp.int32, (c, c), 0) == lax.broadcasted_iota(jnp.int32, (c, c), 1)).astype(F32)
    p = -a
    tinv = eye + p
    for _ in range(5):
        p = _dot3(p, p)
        tinv = tinv + _dot3(tinv, p)
    return tinv


def _unit_lower_inverse_fwd(a):
    t = _unit_lower_inverse(a)
    return t, t


def _unit_lower_inverse_bwd(t, g):
    return (-_mm(_mm(t, g, "tn", 3), t, "nt", 3),)


_unit_lower_inverse.defvjp(_unit_lower_inverse_fwd, _unit_lower_inverse_bwd)


@jax.custom_vjp
def _known_inverse(a, t):
    return t


_known_inverse.defvjp(lambda a, t: (t, t), lambda t, g: (_unit_lower_inverse_bwd(t, g)[0], jnp.zeros_like(t)))


def _dn_step(S, q, k, v, beta, g, tinv=None):
    c = DN_CHUNK
    ri = lax.broadcasted_iota(jnp.int32, (c, c), 0)
    ci = lax.broadcasted_iota(jnp.int32, (c, c), 1)
    incl, strict = ri >= ci, ri > ci
    gam = _cumsum_rows(g)
    gam_row = jnp.sum(jnp.where(ri <= ci, g, 0.0), axis=-2, keepdims=True)
    gam_last = jnp.sum(g, axis=-2, keepdims=True)
    decay = jnp.where(incl, jnp.exp(jnp.where(incl, gam - gam_row, 0.0)), 0.0)
    kb = k * beta
    vb = v * beta
    a = jnp.where(strict, _dot_nt(kb, k) * decay, 0.0)
    tinv = _unit_lower_inverse(a) if tinv is None else _known_inverse(a, tinv)
    e_gam = jnp.exp(gam)
    u0 = _dot3(tinv, vb)
    wk = _dot3(tinv, kb * e_gam)
    qk = jnp.where(incl, _dot_nt(q, k) * decay, 0.0)
    q_dec = q * e_gam
    k_dec = k * jnp.exp(gam_last - gam)
    u = u0 - _dot(wk, S)
    o = _dot(q_dec, S) + _dot(qk, u)
    s_new = S * jnp.exp(gam_last) + _dot_tn(k_dec, u)
    return s_new, o, tinv


def _dn_fwd_call(q, k, v, gb):
    t, w = q.shape
    n, hp, hd, c = t // DN_CHUNK, DN_HP, DN_HEAD_DIM, DN_CHUNK

    def body(q_ref, k_ref, v_ref, gb_ref, o_ref, s0_ref, ti_ref, s_scr):
        @pl.when(pl.program_id(1) == 0)
        def _():
            s_scr[...] = jnp.zeros_like(s_scr)

        s_old = s_scr[...]
        s0_ref[:, 0] = s_old
        s_new, o, tinv = _dn_block(s_old, q_ref[...], k_ref[...], v_ref[...], gb_ref[...], pl.program_id(0) * hp)
        o_ref[...] = o
        ti_ref[:, 0] = tinv
        s_scr[...] = s_new

    blk = pl.BlockSpec((c, hp * hd), lambda g, i: (i, g))
    return pl.pallas_call(
        body, name="dn_core", grid=(DN_HEADS // hp, n),
        in_specs=[blk, blk, blk, pl.BlockSpec((c, LANES), lambda g, i: (i, 0))],
        out_specs=[blk, pl.BlockSpec((hp, 1, hd, hd), lambda g, i: (g, i, 0, 0)), pl.BlockSpec((hp, 1, c, c), lambda g, i: (g, i, 0, 0))],
        out_shape=[jax.ShapeDtypeStruct((t, w), F32), jax.ShapeDtypeStruct((DN_HEADS, n, hd, hd), F32),
                   jax.ShapeDtypeStruct((DN_HEADS, n, c, c), F32)],
        scratch_shapes=[pltpu.VMEM((hp, hd, hd), F32)],
        compiler_params=_params("parallel", "arbitrary"),
    )(q, k, v, gb)


def _dn_bwd_call(q, k, v, gb, s0, ti, do):
    t, w = q.shape
    n, hp, hd, c = t // DN_CHUNK, DN_HP, DN_HEAD_DIM, DN_CHUNK
    ng = DN_HEADS // hp

    def body(q_ref, k_ref, v_ref, gb_ref, s0_ref, ti_ref, do_ref, dq_ref, dk_ref, dv_ref, dgb_ref, ds_scr):
        @pl.when(pl.program_id(1) == 0)
        def _():
            ds_scr[...] = jnp.zeros_like(ds_scr)

        h0, tinv = pl.program_id(0) * hp, ti_ref[:, 0]
        _, vjp = jax.vjp(lambda *a: _dn_block(*a, h0, tinv)[:2], s0_ref[:, 0], q_ref[...], k_ref[...], v_ref[...], gb_ref[...])
        ds, dq, dk, dv, dgb = vjp((ds_scr[...], do_ref[...]))
        ds_scr[...] = ds
        dq_ref[...], dk_ref[...], dv_ref[...] = dq, dk, dv
        dgb_ref[0] = dgb

    blk = pl.BlockSpec((c, hp * hd), lambda g, i: (n - 1 - i, g))
    res = pl.pallas_call(
        body, name="dn_core_bwd", grid=(ng, n),
        in_specs=[blk, blk, blk, pl.BlockSpec((c, LANES), lambda g, i: (n - 1 - i, 0)),
                  pl.BlockSpec((hp, 1, hd, hd), lambda g, i: (g, n - 1 - i, 0, 0)),
                  pl.BlockSpec((hp, 1, c, c), lambda g, i: (g, n - 1 - i, 0, 0)), blk],
        out_specs=[blk, blk, blk, pl.BlockSpec((1, c, LANES), lambda g, i: (g, n - 1 - i, 0))],
        out_shape=[jax.ShapeDtypeStruct((t, w), F32)] * 3 + [jax.ShapeDtypeStruct((ng, t, LANES), F32)],
        scratch_shapes=[pltpu.VMEM((hp, hd, hd), F32)],
        compiler_params=_params("parallel", "arbitrary"),
    )(q, k, v, gb, s0, ti, do)
    return res[0], res[1], res[2], jnp.sum(res[3], axis=0)


@jax.custom_vjp
def dn_core(q, k, v, gb):
    return _dn_fwd_call(q, k, v, gb)[0]


def _dn_core_fwd(q, k, v, gb):
    o, s0, ti = _dn_fwd_call(q, k, v, gb)
    return o, (q, k, v, gb, s0, ti)


def _dn_core_bwd(res, do):
    return _dn_bwd_call(*res, do)


dn_core.defvjp(_dn_core_fwd, _dn_core_bwd)


ATT_GH = 4


def _att_block(q, kp, kc, vp, vc, qn, kn, slope, has_prev, dil):
    s = ATT_SPAN
    qh = _rms(q, qn) * (ATT_HEAD_DIM ** -0.5)
    qi = lax.broadcasted_iota(jnp.int32, (s, s), 0)
    kj = lax.broadcasted_iota(jnp.int32, (s, s), 1)
    d_p = qi + s - kj
    d_c = qi - kj
    s_p = _dot_nt(qh, _rms(kp, kn)) - slope * (d_p * dil).astype(F32)
    s_c = _dot_nt(qh, _rms(kc, kn)) - slope * (d_c * dil).astype(F32)
    s_p = jnp.where((d_p <= s) & (has_prev > 0), s_p, NEG_INF)
    s_c = jnp.where(d_c >= 0, s_c, NEG_INF)
    m = lax.stop_gradient(jnp.maximum(jnp.max(s_p, axis=-1, keepdims=True), jnp.max(s_c, axis=-1, keepdims=True)))
    p_p = jnp.exp(s_p - m)
    p_c = jnp.exp(s_c - m)
    den = jnp.sum(p_p, axis=-1, keepdims=True) + jnp.sum(p_c, axis=-1, keepdims=True)
    o = _dot(p_p / den, vp) + _dot(p_c / den, vc)
    lse = m + jnp.log(den)
    return o, jnp.broadcast_to(lse, o.shape)


def _att_heads(a):
    e = ATT_HEAD_DIM
    return jnp.concatenate([a[None, :, h * e:(h + 1) * e] for h in range(ATT_GH)], axis=0)


def _att_lanes(a):
    return jnp.concatenate([a[h:h + 1].reshape(a.shape[1:]) for h in range(ATT_GH)], axis=1)


def _att_rows(q, kp, kc, vp, vc, qn, kn, group, has_prev, dil):
    head = lax.broadcasted_iota(jnp.int32, (ATT_GH, 1, 1), 0) + (ATT_GH * group + 1)
    slope = jnp.exp(head.astype(F32) * (-8.0 / ATT_HEADS * math.log(2.0)))
    o, lse = _att_block(_att_heads(q), _att_heads(kp), _att_heads(kc), _att_heads(vp), _att_heads(vc), qn, kn, slope, has_prev, dil)
    return _att_lanes(o), _att_lanes(lse)


def _att_specs(group, dil):
    blk = (ATT_SPAN, ATT_GH * ATT_HEAD_DIM)
    cur = lambda which: pl.BlockSpec(blk, lambda r, n: (n, r * 9 + 3 * which + group))
    prev = lambda which: pl.BlockSpec(blk, lambda r, n: (jnp.maximum(n - 1, 0), r * 9 + 3 * which + group))
    out = pl.BlockSpec(blk, lambda r, n: (n, r))
    gain = pl.BlockSpec((ATT_GH, 1, ATT_HEAD_DIM), lambda r, n: (0, 0, 0))
    return [cur(0), prev(1), cur(1), prev(2), cur(2), gain, gain], out, gain


def _att_fwd_call(name, group, dil, pa, qn, kn):
    t = pa.shape[0]
    l = t // dil
    w = ATT_GH * ATT_HEAD_DIM
    ins, out, _ = _att_specs(group, dil)
    pav = pa.reshape(l, dil * pa.shape[1])

    def body(q_ref, kp_ref, kc_ref, vp_ref, vc_ref, qn_ref, kn_ref, o_ref, lse_ref):
        o_ref[...], lse_ref[...] = _att_rows(q_ref[...], kp_ref[...], kc_ref[...], vp_ref[...], vc_ref[...], qn_ref[...],
                                             kn_ref[...], group, pl.program_id(1), dil)

    o, lse = pl.pallas_call(
        body, name=name, grid=(dil, l // ATT_SPAN), in_specs=ins, out_specs=[out, out],
        out_shape=[jax.ShapeDtypeStruct((l, dil * w), F32)] * 2, compiler_params=_params("parallel", "arbitrary"),
    )(pav, pav, pav, pav, pav, qn, kn)
    return o.reshape(t, w), lse.reshape(t, w)


def _att_bwd_call(name, group, dil, pa, qn, kn, do, dlse):
    t = pa.shape[0]
    l = t // dil
    w = ATT_GH * ATT_HEAD_DIM
    ins, out, gain = _att_specs(group, dil)
    pav = pa.reshape(l, dil * pa.shape[1])

    def body(q_ref, kp_ref, kc_ref, vp_ref, vc_ref, qn_ref, kn_ref, do_ref, dlse_ref,
             dq_ref, dkp_ref, dkc_ref, dvp_ref, dvc_ref, dqn_ref, dkn_ref):
        has_prev = pl.program_id(1)
        _, vjp = jax.vjp(lambda *a: _att_rows(*a, group, has_prev, dil), q_ref[...], kp_ref[...], kc_ref[...], vp_ref[...],
                         vc_ref[...], qn_ref[...], kn_ref[...])
        dq, dkp, dkc, dvp, dvc, dqn, dkn = vjp((do_ref[...], dlse_ref[...]))
        dq_ref[...], dkp_ref[...], dkc_ref[...], dvp_ref[...], dvc_ref[...] = dq, dkp, dkc, dvp, dvc

        @pl.when((pl.program_id(0) == 0) & (pl.program_id(1) == 0))
        def _():
            dqn_ref[...] = jnp.zeros_like(dqn_ref)
            dkn_ref[...] = jnp.zeros_like(dkn_ref)

        dqn_ref[...] += dqn
        dkn_ref[...] += dkn

    res = pl.pallas_call(
        body, name=name + "_bwd", grid=(dil, l // ATT_SPAN), in_specs=ins + [out, out],
        out_specs=[out] * 5 + [gain, gain],
        out_shape=[jax.ShapeDtypeStruct((l, dil * w), F32)] * 5 + [jax.ShapeDtypeStruct(qn.shape, F32)] * 2,
        compiler_params=_params("arbitrary", "arbitrary"),
    )(pav, pav, pav, pav, pav, qn, kn, do.reshape(l, dil * w), dlse.reshape(l, dil * w))
    dq, dkp, dkc, dvp, dvc, dqn, dkn = res
    back = lambda g: jnp.pad(g[ATT_SPAN:], ((0, ATT_SPAN), (0, 0)))
    return dq.reshape(t, w), (dkc + back(dkp)).reshape(t, w), (dvc + back(dvp)).reshape(t, w), dqn, dkn


def _att_mix(o1, o2, o3, l1, l2, l3):
    m = jnp.maximum(jnp.maximum(l1, l2), l3)
    e1, e2, e3 = jnp.exp(l1 - m), jnp.exp(l2 - m), jnp.exp(l3 - m)
    s = e1 + e2 + e3
    return (jnp.concatenate([o1 * (e1 / s), o2 * (e2 / s), o3 * (e3 / s)], axis=1),)


def att_branch(name, pa, qn, kn):
    e = ATT_HEAD_DIM
    gains = lambda p, g: p[ATT_GH * g:ATT_GH * (g + 1)].reshape(ATT_GH, 1, e)

    @jax.custom_vjp
    def groups(pa, qn, kn):
        res = [_att_fwd_call(f"{name}_att{g}", g, dil, pa, gains(qn, g), gains(kn, g)) for g, (_, dil) in enumerate(ATT_GROUPS)]
        return tuple(r[0] for r in res) + tuple(r[1] for r in res)

    def groups_fwd(pa, qn, kn):
        return groups(pa, qn, kn), (pa, qn, kn)

    def groups_bwd(res, cts):
        pa, qn, kn = res
        n = len(ATT_GROUPS)
        parts = [_att_bwd_call(f"{name}_att{g}", g, dil, pa, gains(qn, g), gains(kn, g), cts[g], cts[n + g])
                 for g, (_, dil) in enumerate(ATT_GROUPS)]
        d_pa = jnp.concatenate([p[i] for i in range(3) for p in parts], axis=1)
        return (d_pa, jnp.concatenate([p[3] for p in parts]).reshape(qn.shape), jnp.concatenate([p[4] for p in parts]).reshape(kn.shape))

    groups.defvjp(groups_fwd, groups_bwd)
    return rowwise(f"{name}_attmix", _att_mix, groups(pa, qn, kn))[0]


def dn_gates(name, ba, a_log, dt_bias):
    place = lambda p: jnp.pad(p.reshape(1, DN_HEADS), ((0, 0), (DN_HEADS, LANES - 2 * DN_HEADS)))

    def f(x, al, dt):
        lane = lax.broadcasted_iota(jnp.int32, x.shape, 1)
        return (jnp.where(lane < DN_HEADS, _sigmoid(x), -jnp.exp(al) * _softplus(x + dt)),)

    return rowwise(name, f, (ba,), (place(a_log), place(dt_bias)))[0]


def _dn_out(o, z, g):
    parts = []
    for h in range(DN_HEADS):
        sl = slice(h * DN_HEAD_DIM, (h + 1) * DN_HEAD_DIM)
        parts.append(_rms(o[:, sl], g[:, sl]) * _silu(z[:, sl]))
    return (jnp.concatenate(parts, axis=1),)


def _merge(ml, za, zb, zc):
    d = D_MODEL
    return (_sigmoid(ml[:, :d]) * za + _sigmoid(ml[:, d:2 * d]) * zb + _sigmoid(ml[:, 2 * d:]) * zc,)


def add_norm(name, x, pend, scale, gain):
    if pend is None:
        return x, rowwise(name, lambda a, g: (_rms(a, g),), (x,), (gain,))[0]

    def f(a, b, g):
        s = a + scale * b
        return s, _rms(s, g)

    return rowwise(name, f, (x, pend), (gain,))


W_IN_PIECES = (("rgx", 0, 1024), ("gate", 1024, 1024), ("att", 2048, 2304), ("dq", 4352, 1024), ("dk", 5376, 1024),
               ("dv", 6400, 1024), ("dz", 7424, 1024), ("ba", 8448, 16), ("mrg", 8464, 3072))
RG_PAR_BLOCKS = ("lane", "lane", "blk", "lane", "blk", "lane", "lane")


def mixer(name, u, w, p):
    mm = lambda nm, a, wt: mm_rows(nm, a[None], wt[None])
    pr = project_in(name + "_in", u, {k: w["in_" + k] for k, _, _ in W_IN_PIECES})
    ya = colwise(name + "_rg", _rg_block, (pr["rgx"], pr["gate"]),
                 (w["rg_conv_w"], p["rg_conv_b"], _block_diag(p["rg_w_r"]), p["rg_b_r"], _block_diag(p["rg_w_i"]),
                  p["rg_b_i"], p["rg_lambda"]), RG_PAR_BLOCKS, 1)[0]
    yb = att_branch(name, pr["att"], p["att_q_norm"], p["att_k_norm"])
    cw = w["dn_conv_w"]
    cq = colwise(name + "_dnq", _dn_conv_block("q"), (pr["dq"],), (cw[:, :1024],), ("lane",), 1)[0]
    ck = colwise(name + "_dnk", _dn_conv_block("k"), (pr["dk"],), (cw[:, 1024:2048],), ("lane",), 1)[0]
    cv = colwise(name + "_dnv", _dn_conv_block("v"), (pr["dv"],), (cw[:, 2048:],), ("lane",), 1)[0]
    gb = dn_gates(name + "_dngate", pr["ba"], p["dn_a_log"], p["dn_dt_bias"])
    o_dn = dn_core(cq, ck, cv, gb)
    yc = rowwise(name + "_dnout", _dn_out, (o_dn, pr["dz"]), (p["dn_out_norm"].reshape(1, D_MODEL),))[0]
    y = rowwise(name + "_merge", _merge, (pr["mrg"], mm(name + "_ba", ya, w["br_a"]), mm(name + "_bb", yb, w["br_b"]),
                                          mm(name + "_bc", yc, w["br_c"])))[0]
    return mm(name + "_out", y, w["w_out"])


def _loss_call(x, pend, target):
    t, d = x.shape
    tile = _row_tile(t)

    def body(x_ref, p_ref, t_ref, loss_ref, g_ref):
        err = x_ref[...] + 0.5 * p_ref[...] - t_ref[...]
        g_ref[...] = err * (1.0 / d)

        @pl.when(pl.program_id(0) == 0)
        def _():
            loss_ref[...] = jnp.zeros_like(loss_ref)

        loss_ref[...] += jnp.full(loss_ref.shape, 0.5 / d, F32) * jnp.sum(err * err)

    blk = pl.BlockSpec((tile, d), lambda i: (i, 0))
    loss, g = pl.pallas_call(
        body, name="loss", grid=(t // tile,), in_specs=[blk, blk, blk],
        out_specs=[pl.BlockSpec((8, LANES), lambda i: (0, 0)), blk],
        out_shape=[jax.ShapeDtypeStruct((8, LANES), F32), jax.ShapeDtypeStruct((t, d), F32)],
        compiler_params=_params("arbitrary"),
    )(x, pend, target)
    return loss[0, 0], g


@jax.custom_vjp
def loss_op(x, pend, target):
    return _loss_call(x, pend, target)[0]


def _loss_fwd(x, pend, target):
    loss, g = _loss_call(x, pend, target)
    return loss, g


def _loss_bwd(g, ct):
    return ct * g, (0.5 * ct) * g, None


loss_op.defvjp(_loss_fwd, _loss_bwd)


def first_ffn(wg, wu, wd, gain, x):
    x, h = add_norm("L0_n1", x, None, 0.0, gain)
    return x, ffn("L0_f1", h, wg, wu, wd)


def rest_of_step(g, conv, p, x, pend, target):
    scale = 0.5
    w = [split_layer({n: g[n, l] for n, _ in MATRICES if (n, l) in g}, {n: conv[n][l] for n, _ in CONVS}) for l in range(len(p))]
    for l in range(len(p)):
        n = f"L{l}"
        if l > 0:
            x, h = add_norm(n + "_n1", x, pend, scale, p[l]["ffn1_norm"])
            pend, scale = ffn(n + "_f1", h, w[l]["ffn1_w_gate"], w[l]["ffn1_w_up"], w[l]["ffn1_w_down"]), 0.5
        x, h = add_norm(n + "_nm", x, pend, scale, p[l]["mix_norm"])
        pend, scale = mixer(n + "_mx", h, w[l], p[l]), 1.0
        x, h = add_norm(n + "_n2", x, pend, scale, p[l]["ffn2_norm"])
        pend, scale = ffn(n + "_f2", h, w[l]["ffn2_w_gate"], w[l]["ffn2_w_up"], w[l]["ffn2_w_down"]), 0.5
    return loss_op(x, pend, target)


WEIGHT_NAMES = ("ffn1_norm", "ffn1_w_gate", "ffn1_w_up", "ffn1_w_down", "mix_norm", "w_in", "rg_conv_w", "rg_conv_b",
                "rg_w_r", "rg_b_r", "rg_w_i", "rg_b_i", "rg_lambda", "att_q_norm", "att_k_norm", "dn_conv_w", "dn_a_log",
                "dn_dt_bias", "dn_out_norm", "w_branch", "w_out", "ffn2_norm", "ffn2_w_gate", "ffn2_w_up", "ffn2_w_down")
MATRICES = (("ffn1_w_gate", 2), ("ffn1_w_up", 2), ("ffn1_w_down", 1), ("w_in", 2), ("w_branch", 1), ("w_out", 1),
            ("ffn2_w_gate", 2), ("ffn2_w_up", 2), ("ffn2_w_down", 1))
CONVS = (("rg_conv_w", 2), ("dn_conv_w", 2))
SHARD_AXIS = dict(MATRICES + CONVS)
SMALL_NAMES = tuple(n for n in WEIGHT_NAMES if n not in SHARD_AXIS)
ROW_PARAMS = ("ffn1_norm", "mix_norm", "rg_conv_b", "rg_b_r", "rg_b_i", "rg_lambda", "ffn2_norm")
FFN_MATS = ("ffn1_w_gate", "ffn1_w_up", "ffn1_w_down", "ffn2_w_gate", "ffn2_w_up", "ffn2_w_down")
TRANSPOSED_MATS = ("ffn1_w_gate", "ffn1_w_up", "ffn2_w_gate", "ffn2_w_up")
W_IN_SHARD = 2884
GATHER_ORDER = ((("ffn1_w_gate", 0), ("ffn1_w_up", 0), ("ffn1_w_down", 0)),
                (("w_in", 0), ("w_branch", 0), ("w_out", 0)),
                None)
GATHER_IDS = (1, 6, 7)
LATE_MATS = ("ffn2_w_gate", "ffn2_w_up", "ffn2_w_down", "w_out", "w_branch")
EXCHANGE_GROUPS = (lambda n, l: l == 1 and n in LATE_MATS,
                   lambda n, l: (l == 1) != (n in LATE_MATS),
                   lambda n, l: l == 0 and n == "w_in",
                   lambda n, l: l == 0 and n not in LATE_MATS and n != "w_in")


def _shard_minor(a, axis):
    a = jnp.moveaxis(a, 0, axis)
    return a.reshape(a.shape[:axis] + (N_CHIPS * a.shape[axis + 1],) + a.shape[axis + 2:])


def _w_in_piece(g, off, n):
    s = W_IN_SHARD
    parts = [g[j][:, max(off, j * s) - j * s:min(off + n, (j + 1) * s) - j * s]
             for j in range(N_CHIPS) if max(off, j * s) < min(off + n, (j + 1) * s)]
    return jnp.concatenate(parts, axis=1) if len(parts) > 1 else parts[0]


def _w_in_chip_grad(gl, j):
    s = W_IN_SHARD
    parts = [gl["in_" + k][:, max(off, j * s) - off:min(off + n, (j + 1) * s) - off]
             for k, off, n in W_IN_PIECES if max(off, j * s) < min(off + n, (j + 1) * s)]
    return jnp.concatenate(parts, axis=1)


def _layer_weights(g, conv):
    w = {n: g[n] for n in FFN_MATS if n in g}
    w["w_out"] = g["w_out"].reshape(D_MODEL, D_MODEL)
    for k, off, n in W_IN_PIECES:
        piece = _w_in_piece(g["w_in"], off, n)
        w["in_" + k] = jnp.pad(piece, ((0, 0), (0, LANES - n))) if n < LANES else piece
    wb = g["w_branch"].reshape(-1, D_MODEL)
    w["br_a"], w["br_b"], w["br_c"] = wb[:1024], wb[1024:1792], wb[1792:]
    return dict(w, **conv)


def _layer_weight_grads(gl):
    out = {n: gl[n] for n in FFN_MATS if n in gl}
    out["w_out"] = gl["w_out"].reshape(N_CHIPS, -1, D_MODEL)
    out["w_branch"] = jnp.concatenate([gl["br_a"], gl["br_b"], gl["br_c"]], axis=0).reshape(N_CHIPS, -1, D_MODEL)
    out["w_in"] = jnp.stack([_w_in_chip_grad(gl, j) for j in range(N_CHIPS)])
    return out, {n: gl[n] for n, _ in CONVS}


@jax.custom_vjp
def split_layer(g, conv):
    return _layer_weights(g, conv)


split_layer.defvjp(lambda g, conv: (_layer_weights(g, conv), None), lambda _, gw: _layer_weight_grads(gw))


def layer_small(small, l):
    p = {n: small[n][l] for n in SMALL_NAMES}
    for n in ROW_PARAMS:
        p[n] = small[n][l:l + 1]
    return p


def layer_small_grads(gp, small):
    return {n: jnp.stack([g[n] for g in gp]).reshape(small[n].shape) for n in SMALL_NAMES}


HBM_SPEC = pl.BlockSpec(memory_space=pl.ANY)


def _place():
    x, y, c = lax.axis_index("x"), lax.axis_index("y"), lax.axis_index("c")
    other_chips = [(1 - x, y), (x, 1 - y), (1 - x, 1 - y)]
    return x, y, c, 2 * x + y, (x, y, 1 - c), other_chips


def _half_rows(ref, lead, hc):
    hr = ref.shape[-2] // 2
    return ref.at[(*lead, pl.ds(pl.multiple_of(hc * hr, 16), hr), slice(None))]


def _chip_index():
    return (2 * lax.axis_index("x") + lax.axis_index("y")).astype(jnp.int32).reshape(1)


def cast_into_blocks(name, w):
    l, rows, cols = w.shape
    tr = rows // 2

    def body(me_ref, w_ref, *o_refs):
        for a, o_ref in enumerate(o_refs):
            o_ref[...] = w_ref[a:a + 1].astype(BF16)

    return pl.pallas_call(
        body, name=name, out_shape=[jax.ShapeDtypeStruct((N_CHIPS, rows, cols), BF16)] * l,
        grid_spec=pltpu.PrefetchScalarGridSpec(
            num_scalar_prefetch=1, grid=(rows // tr,),
            in_specs=[pl.BlockSpec((l, tr, cols), lambda i, me: (0, i, 0))],
            out_specs=[pl.BlockSpec((1, tr, cols), lambda i, me: (me[0], i, 0))] * l),
        compiler_params=_params("parallel"),
    )(_chip_index(), w)


def _gather_blocks(bufs_in, bufs_out, send_sems, recv_sems):
    n = len(bufs_in)
    x, y, c, me, sibling, chips = _place()

    def copy(s, src, dst, to):
        return pltpu.make_async_remote_copy(src_ref=src, dst_ref=dst, send_sem=send_sems.at[s], recv_sem=recv_sems.at[s],
                                            device_id=to, device_id_type=MESH)

    first, passed = [], []
    for j, (cx, cy) in enumerate(chips):
        for i in range(n):
            cp = copy(6 * i + j, _half_rows(bufs_in[i], (me,), c), _half_rows(bufs_out[i], (me,), c), (cx, cy, c))
            cp.start()
            first.append(cp)
    for j, (cx, cy) in enumerate(chips):
        k = 2 * cx + cy
        for i in range(n):
            copy(6 * i + j, _half_rows(bufs_in[i], (me,), c), _half_rows(bufs_out[i], (k,), c), (cx, cy, c)).wait_recv()
            cp = copy(6 * i + 3 + j, _half_rows(bufs_out[i], (k,), c), _half_rows(bufs_out[i], (k,), c), sibling)
            cp.start()
            passed.append(cp)
    for j, (cx, cy) in enumerate(chips):
        k = 2 * cx + cy
        for i in range(n):
            copy(6 * i + 3 + j, _half_rows(bufs_in[i], (me,), c), _half_rows(bufs_out[i], (k,), 1 - c), sibling).wait_recv()
    for cp in first + passed:
        cp.wait_send()


def _handshake(peers):
    barrier = pltpu.get_barrier_semaphore()
    for p in peers:
        pl.semaphore_signal(barrier, inc=1, device_id=p, device_id_type=MESH)
    pl.semaphore_wait(barrier, len(peers))


def allgather_blocks_sc(name, bufs, collective_id):
    n = len(bufs)
    refs = [jax.new_ref(b, memory_space=pltpu.MemorySpace.HBM) for b in bufs]

    @pl.kernel(mesh=plsc.ScalarSubcoreMesh(axis_name="sequencer", num_cores=1), name=name,
               scratch_types=(pltpu.SemaphoreType.DMA((6 * n,)), pltpu.SemaphoreType.DMA((6 * n,))),
               compiler_params=pltpu.CompilerParams(collective_id=collective_id))
    def launch(send_sems, recv_sems):
        x, y, c, me, sibling, chips = _place()
        _handshake([(cx, cy, c) for cx, cy in chips] + [sibling])
        _gather_blocks(refs, refs, send_sems, recv_sems)

    launch()
    return [jax.freeze(r) for r in refs]


PEER_FLIPS = tuple((fx, fy, fc) for fx in (0, 1) for fy in (0, 1) for fc in (0, 1))[1:]


def exchange_pieces_sc(name, gs, collective_id):
    n = len(gs)

    def body(*refs):
        ins, outs = refs[:n], refs[n:2 * n]
        send_sems, recv_sems = refs[2 * n:]
        x, y, c, me, sibling, chips = _place()
        my_dev = 4 * x + 2 * y + c
        flip = lambda v, f: 1 - v if f else v
        peers = [(flip(x, fx), flip(y, fy), flip(c, fc)) for fx, fy, fc in PEER_FLIPS]
        _handshake(peers)
        sends = []
        for r, (px, py, pc) in enumerate(peers):
            for i in range(n):
                cp = pltpu.make_async_remote_copy(
                    src_ref=_half_rows(ins[i], (2 * px + py,), pc), dst_ref=outs[i].at[my_dev], send_sem=send_sems.at[7 * i + r],
                    recv_sem=recv_sems.at[7 * i + r], device_id=(px, py, pc), device_id_type=MESH)
                cp.start()
                sends.append(cp)
        for r, (px, py, pc) in enumerate(peers):
            for i in range(n):
                pltpu.make_async_remote_copy(
                    src_ref=_half_rows(ins[i], (me,), c), dst_ref=outs[i].at[4 * px + 2 * py + pc], send_sem=send_sems.at[7 * i + r],
                    recv_sem=recv_sems.at[7 * i + r], device_id=(px, py, pc), device_id_type=MESH).wait_recv()
        for cp in sends:
            cp.wait_send()

    return pl.kernel(
        body, name=name, mesh=plsc.ScalarSubcoreMesh(axis_name="sequencer", num_cores=1),
        out_type=[jax.ShapeDtypeStruct((N_DEV, g.shape[1] // 2, g.shape[2]), g.dtype) for g in gs],
        scratch_types=[pltpu.SemaphoreType.DMA((7 * n,)), pltpu.SemaphoreType.DMA((7 * n,))],
        compiler_params=pltpu.CompilerParams(collective_id=collective_id),
    )(*gs)


def sibling_share_halves(name, fs):
    n = len(fs)
    every = (slice(None),)

    def body(*refs):
        ins, outs = refs[:n], refs[n:2 * n]
        send_sems, recv_sems = refs[2 * n:]
        x, y, c, me, sibling, chips = _place()
        sends = []
        for i in range(n):
            cp = pltpu.make_async_remote_copy(src_ref=_half_rows(ins[i], every, c), dst_ref=_half_rows(outs[i], every, c),
                                              send_sem=send_sems.at[i], recv_sem=recv_sems.at[i], device_id=sibling, device_id_type=MESH)
            cp.start()
            sends.append(cp)
        for i in range(n):
            pltpu.make_async_remote_copy(src_ref=_half_rows(ins[i], every, c), dst_ref=_half_rows(outs[i], every, 1 - c),
                                         send_sem=send_sems.at[i], recv_sem=recv_sems.at[i], device_id=sibling,
                                         device_id_type=MESH).wait_recv()
        for cp in sends:
            cp.wait_send()

    return pl.pallas_call(
        body, name=name, out_shape=[jax.ShapeDtypeStruct(f.shape, f.dtype) for f in fs],
        in_specs=[HBM_SPEC] * n, out_specs=[HBM_SPEC] * n, input_output_aliases={i: i for i in range(n)},
        scratch_shapes=[pltpu.SemaphoreType.DMA((n,)), pltpu.SemaphoreType.DMA((n,))],
    )(*fs)


def allgather_small_sc(name, v, collective_id):
    def body(v_ref, out_ref, send_sems, recv_sems, local_sem):
        x, y, c, me, sibling, chips = _place()
        my_dev = 4 * x + 2 * y + c
        flip = lambda a, f: 1 - a if f else a
        peers = [(flip(x, fx), flip(y, fy), flip(c, fc)) for fx, fy, fc in PEER_FLIPS]
        _handshake(peers)
        mine = pltpu.make_async_copy(v_ref, out_ref.at[my_dev], local_sem)
        mine.start()
        sends = []
        for r, peer in enumerate(peers):
            cp = pltpu.make_async_remote_copy(src_ref=v_ref, dst_ref=out_ref.at[my_dev], send_sem=send_sems.at[r],
                                              recv_sem=recv_sems.at[r], device_id=peer, device_id_type=MESH)
            cp.start()
            sends.append(cp)
        for r, (px, py, pc) in enumerate(peers):
            pltpu.make_async_remote_copy(src_ref=v_ref, dst_ref=out_ref.at[4 * px + 2 * py + pc], send_sem=send_sems.at[r],
                                         recv_sem=recv_sems.at[r], device_id=(px, py, pc), device_id_type=MESH).wait_recv()
        for cp in sends:
            cp.wait_send()
        mine.wait()

    return pl.kernel(
        body, name=name, mesh=plsc.ScalarSubcoreMesh(axis_name="sequencer", num_cores=1),
        out_type=jax.ShapeDtypeStruct((N_DEV,) + v.shape, v.dtype),
        scratch_types=[pltpu.SemaphoreType.DMA((7,)), pltpu.SemaphoreType.DMA((7,)), pltpu.SemaphoreType.DMA],
        compiler_params=pltpu.CompilerParams(collective_id=collective_id),
    )(v)


SUM_BLOCK_ELEMS = 512 * 1024


def sum_slabs(name, b):
    k, h, w = b.shape

    def body(b_ref, o_ref):
        acc = b_ref[0].astype(F32)
        for i in range(1, k):
            acc = acc + b_ref[i].astype(F32)
        o_ref[...] = acc

    return pl.pallas_call(
        body, name=name, out_shape=jax.ShapeDtypeStruct((h, w), F32),
        in_specs=[pl.BlockSpec(memory_space=pltpu.VMEM)], out_specs=pl.BlockSpec(memory_space=pltpu.VMEM),
        compiler_params=pltpu.CompilerParams(vmem_limit_bytes=VMEM_LIMIT),
    )(b)


def sum_pieces(name, pieces, gs):
    nl = len(pieces)
    k, h, w = pieces[0].shape
    tile = max(t for t in range(16, h + 1, 16) if h % t == 0 and (t * w <= SUM_BLOCK_ELEMS or t == 16))
    nt = h // tile
    x, y, c = lax.axis_index("x"), lax.axis_index("y"), lax.axis_index("c")
    place = [v.astype(jnp.int32).reshape(1) for v in (c, 2 * x + y, 4 * x + 2 * y + c)]

    assert nl == 2

    def tile_of(l, a, i):
        return i * a if l else i * (1 - a) + (nt - 1) * a

    def body(c_ref, me_ref, dev_ref, *refs):
        p_refs, g_refs, o_ref = refs[:nl], refs[nl:2 * nl], refs[2 * nl]
        my_dev = dev_ref[0]
        for l in range(nl):
            @pl.when(pl.program_id(0) == l)
            def _():
                o_ref[0] = jnp.zeros(o_ref.shape[1:], F32)
                for d in range(k):
                    @pl.when(my_dev == d)
                    def _():
                        o_ref[0] += g_refs[l][0].astype(F32)

                    @pl.when(my_dev != d)
                    def _():
                        o_ref[0] += p_refs[l][d].astype(F32)

    in_specs = [pl.BlockSpec((k, tile, w), functools.partial(lambda l, a, i, cc, me, dev: (0, tile_of(l, a, i), 0), l))
                for l in range(nl)]
    in_specs += [pl.BlockSpec((1, tile, w), functools.partial(lambda l, a, i, cc, me, dev: (me[0], cc[0] * nt + tile_of(l, a, i), 0), l))
                 for l in range(nl)]
    return pl.pallas_call(
        body, name=name, out_shape=jax.ShapeDtypeStruct((nl, 2 * h, w), F32),
        grid_spec=pltpu.PrefetchScalarGridSpec(
            num_scalar_prefetch=3, grid=(nl, nt), in_specs=in_specs,
            out_specs=pl.BlockSpec((1, tile, w), lambda a, i, cc, me, dev: (a, cc[0] * nt + i, 0))),
        compiler_params=_params("arbitrary", "arbitrary"),
    )(*place, *pieces, *gs)


def _adam_block(w, g, m, v):
    m = ADAM_B1 * m + (1.0 - ADAM_B1) * g
    v = ADAM_B2 * v + (1.0 - ADAM_B2) * (g * g)
    m_hat = m / (1.0 - ADAM_B1 ** ADAM_STEP)
    v_hat = v / (1.0 - ADAM_B2 ** ADAM_STEP)
    return -ADAM_LR * (m_hat / (jnp.sqrt(v_hat) + ADAM_EPS) + ADAM_WD * w), m, v


def adamw(name, w, g, m, v):
    shape = w.shape
    cols = shape[-1]
    rows = w.size // cols
    tile = 128 if rows % 128 == 0 else rows
    flat = [a.reshape(rows, cols) for a in (w, g, m, v)]

    def body(w_ref, g_ref, m_ref, v_ref, d_ref, nm_ref, nv_ref):
        d_ref[...], nm_ref[...], nv_ref[...] = _adam_block(w_ref[...], g_ref[...], m_ref[...], v_ref[...])

    blk = pl.BlockSpec((tile, cols), lambda i: (i, 0))
    res = pl.pallas_call(
        body, name=name, grid=(rows // tile,), in_specs=[blk] * 4, out_specs=[blk] * 3,
        out_shape=[jax.ShapeDtypeStruct((rows, cols), F32)] * 3, compiler_params=_params("parallel"),
    )(*flat)
    return tuple(r.reshape(shape) for r in res)


def _pack_small(values):
    flat = jnp.concatenate([v.reshape(-1) for v in values.values()])
    n = flat.shape[0]
    total = -(-n // (8 * LANES)) * (8 * LANES)
    return jnp.pad(flat, (0, total - n)).reshape(-1, LANES)


def _unpack_small(v, shapes):
    flat = v.reshape(-1)
    out, off = {}, 0
    for n, shape in shapes.items():
        sz = int(np.prod(shape))
        out[n] = flat[off:off + sz].reshape(shape)
        off += sz
    return out


def kernel(x, ffn1_norm, ffn1_w_gate, ffn1_w_up, ffn1_w_down, mix_norm, w_in, rg_conv_w, rg_conv_b, rg_w_r, rg_b_r, rg_w_i, rg_b_i, rg_lambda, att_q_norm, att_k_norm, dn_conv_w, dn_a_log, dn_dt_bias, dn_out_norm, w_branch, w_out, ffn2_norm, ffn2_w_gate, ffn2_w_up, ffn2_w_down, loss_target, m_ffn1_norm, m_ffn1_w_gate, m_ffn1_w_up, m_ffn1_w_down, m_mix_norm, m_w_in, m_rg_conv_w, m_rg_conv_b, m_rg_w_r, m_rg_b_r, m_rg_w_i, m_rg_b_i, m_rg_lambda, m_att_q_norm, m_att_k_norm, m_dn_conv_w, m_dn_a_log, m_dn_dt_bias, m_dn_out_norm, m_w_branch, m_w_out, m_ffn2_norm, m_ffn2_w_gate, m_ffn2_w_up, m_ffn2_w_down, v_ffn1_norm, v_ffn1_w_gate, v_ffn1_w_up, v_ffn1_w_down, v_mix_norm, v_w_in, v_rg_conv_w, v_rg_conv_b, v_rg_w_r, v_rg_b_r, v_rg_w_i, v_rg_b_i, v_rg_lambda, v_att_q_norm, v_att_k_norm, v_dn_conv_w, v_dn_a_log, v_dn_dt_bias, v_dn_out_norm, v_w_branch, v_w_out, v_ffn2_norm, v_ffn2_w_gate, v_ffn2_w_up, v_ffn2_w_down):
    given = dict(locals())
    for n in TRANSPOSED_MATS:
        for pre in ("", "m_", "v_"):
            given[pre + n] = jnp.swapaxes(given[pre + n], 1, 2)
    small = {n: given[n] for n in SMALL_NAMES}
    n_layers = ffn1_norm.shape[0]
    mat_names = [n for n, _ in MATRICES]
    conv_names = [n for n, _ in CONVS]

    blocks = {}
    for n in mat_names:
        for l, b in enumerate(cast_into_blocks("cast_" + n, given[n])):
            blocks[n, l] = b
    first, done = {}, []
    for i, wanted in enumerate(GATHER_ORDER[:-1]):
        bufs, _ = lax.optimization_barrier(([blocks[k] for k in wanted], done))
        done = allgather_blocks_sc(f"allgather_{i}", bufs, GATHER_IDS[i])
        first.update(zip(wanted, done))
    rest = {k: b for k, b in blocks.items() if k not in first}
    taps = jnp.concatenate([given[n].reshape(-1) for n in conv_names]).reshape(-1, LANES)
    taps = allgather_small_sc("allgather_taps", taps, 8).reshape(N_CHIPS, 2, -1)[:, 0]
    conv, off = {}, 0
    for n, ax in CONVS:
        sz = given[n].size
        conv[n] = _shard_minor(taps[:, off:off + sz].reshape((N_CHIPS,) + given[n].shape), ax)
        off += sz
    p = [layer_small(small, l) for l in range(n_layers)]

    ffn1_keys = GATHER_ORDER[0]
    (x1, pend), first_vjp = jax.vjp(first_ffn, *[first[k] for k in ffn1_keys], p[0]["ffn1_norm"], x[0])
    keys = list(rest)
    bufs, pend, second = lax.optimization_barrier(([rest[k] for k in keys], pend, [first[k] for k in GATHER_ORDER[1]]))
    gathered = dict(zip(keys, allgather_blocks_sc("allgather_2", bufs, GATHER_IDS[2])))
    gathered.update(zip(GATHER_ORDER[1], second))
    loss, (g_mats, g_conv, gp, gx1, gpend) = jax.value_and_grad(rest_of_step, argnums=(0, 1, 2, 3, 4))(
        gathered, conv, p, x1, pend, loss_target[0])
    *g_ffn1, gp[0]["ffn1_norm"], gx = first_vjp((gx1, gpend))
    g_mats.update(zip(ffn1_keys, g_ffn1))

    pieces = {}
    for i, group in enumerate(EXCHANGE_GROUPS):
        keys = [k for k in g_mats if group(*k)]
        pieces.update(zip(keys, exchange_pieces_sc(f"exchange_{i}", [g_mats[k] for k in keys], 2 + i)))
    halves = {n: sum_pieces("sum_" + n, [pieces[n, l] for l in range(n_layers)], [g_mats[n, l] for l in range(n_layers)])
              for n in mat_names}
    grads = {}
    for tag, names in (("late", [n for n in mat_names if n in LATE_MATS]), ("early", [n for n in mat_names if n not in LATE_MATS])):
        grads.update(zip(names, sibling_share_halves("share_" + tag, [halves[n] for n in names])))

    g_small = dict(layer_small_grads(gp, small), **g_conv, loss=loss.reshape(1))
    packed_small = _pack_small(g_small)
    slabs = allgather_small_sc("allgather_small", packed_small, 9)
    summed =_unpack_small(sum_slabs("sum_small", slabs), {n: g.shape for n, g in g_small.items()})
    chip = 2 * lax.axis_index("x") + lax.axis_index("y")
    for n in SMALL_NAMES:
        grads[n] = summed[n]
    for n, ax in CONVS:
        s = given[n].shape[ax]
        grads[n] = lax.dynamic_slice_in_dim(summed[n], chip * s, s, axis=ax)

    upd = {n: adamw("adamw_" + n, given[n], grads[n], given["m_" + n], given["v_" + n]) for n in WEIGHT_NAMES}
    out = lambda n, a: jnp.swapaxes(a, 1, 2) if n in TRANSPOSED_MATS else a
    return (summed["loss"][0], gx[None], *[out(n, grads[n]) for n in WEIGHT_NAMES], *[out(n, upd[n][0]) for n in WEIGHT_NAMES],
            *[out(n, upd[n][1]) for n in WEIGHT_NAMES], *[out(n, upd[n][2]) for n in WEIGHT_NAMES])
```
